```python
import math
import jax
import jax.numpy as jnp
from jax import lax
import numpy as np

D_MODEL = 1024
BATCH = 8
SEQ = 2048
DEPTH = 4

N_MIXERS = 4
NORM_EPS = 1e-6
MLA_HEADS = 16
MLA_Q_RANK = 384
MLA_KV_RANK = 256
MLA_NOPE = 64
MLA_ROPE = 32
MLA_V = 64
MLA_QK = MLA_NOPE + MLA_ROPE
MLA_WIDTH = MLA_HEADS * MLA_V
MLA_IN = MLA_Q_RANK + MLA_KV_RANK + MLA_ROPE + MLA_WIDTH
ROPE_THETA = 10000.0
Q_BLOCK = 128
GLA_HEADS = 4
GLA_DK = D_MODEL // (2 * GLA_HEADS)
GLA_DV = D_MODEL // GLA_HEADS
GLA_KEY = GLA_HEADS * GLA_DK
GLA_VAL = GLA_HEADS * GLA_DV
GLA_GATE_RANK = 16
GLA_TAU = 16.0
GLA_CHUNK = 64
GLA_IN = 2 * GLA_KEY + 2 * GLA_VAL + GLA_GATE_RANK
LRU_WIDTH = 5 * D_MODEL // 4
LRU_BLOCKS = 10
LRU_BLOCK = LRU_WIDTH // LRU_BLOCKS
LRU_C = 8.0
CONV_W = 4
SSD_INNER = 2 * D_MODEL
SSD_HEADDIM = 64
SSD_HEADS = SSD_INNER // SSD_HEADDIM
SSD_GROUPS = 8
SSD_HPG = SSD_HEADS // SSD_GROUPS
SSD_STATE = 128
SSD_CHUNK = 64
SSD_CONV_DIM = SSD_INNER + 2 * SSD_GROUPS * SSD_STATE
SSD_IN = SSD_INNER + SSD_CONV_DIM + SSD_HEADS

kernel_name = 'hybrid_mla_gla_rglru_ssd_trunk'


def _layers_of(m):
    return len(range(m, DEPTH, N_MIXERS))


def rmsnorm(x, g):
    xf = x.astype(jnp.float32)
    y = xf * lax.rsqrt(jnp.mean(xf * xf, axis=-1, keepdims=True) + NORM_EPS)
    return (y * g.astype(jnp.float32)).astype(x.dtype)


def rope_tables(positions):
    inv_freq = ROPE_THETA ** (-jnp.arange(0, MLA_ROPE, 2, dtype=jnp.float32) / MLA_ROPE)
    ang = positions.astype(jnp.float32)[..., None] * inv_freq
    return jnp.cos(ang), jnp.sin(ang)


def apply_rope(t, cos, sin):
    t1, t2 = jnp.split(t.astype(jnp.float32), 2, axis=-1)
    return jnp.concatenate([t1 * cos - t2 * sin, t2 * cos + t1 * sin], axis=-1).astype(t.dtype)


def causal_depthwise_conv(u, w, b):
    y = lax.conv_general_dilated(u, w[:, None, :].astype(u.dtype), window_strides=(1,),
                                 padding=[(CONV_W - 1, 0)], dimension_numbers=('NWC', 'WIO', 'NWC'),
                                 feature_group_count=u.shape[-1])
    return y + b.astype(u.dtype)


def mla_mixer(h, cos, sin, w_in, g_q, w_uq, g_kv, w_ukv, w_out):
    B, S, _ = h.shape
    c_q, c_kv, k_r, gate = jnp.split(h @ w_in, [MLA_Q_RANK, MLA_Q_RANK + MLA_KV_RANK,
                                                MLA_Q_RANK + MLA_KV_RANK + MLA_ROPE], axis=-1)
    q = (rmsnorm(c_q, g_q) @ w_uq).reshape(B, S, MLA_HEADS, MLA_QK)
    q_nope = q[..., :MLA_NOPE]
    q_rope = apply_rope(q[..., MLA_NOPE:], cos[:, :, None], sin[:, :, None])
    kv = (rmsnorm(c_kv, g_kv) @ w_ukv).reshape(B, S, MLA_HEADS, MLA_NOPE + MLA_V)
    k_nope, v = kv[..., :MLA_NOPE], kv[..., MLA_NOPE:]
    k_rope = apply_rope(k_r, cos, sin)
    scale = MLA_QK ** -0.5
    outs = []
    for start in range(0, S, Q_BLOCK):
        end = start + Q_BLOCK
        s = (jnp.einsum('bqhd,bkhd->bhqk', q_nope[:, start:end], k_nope[:, :end])
             + jnp.einsum('bqhr,bkr->bhqk', q_rope[:, start:end], k_rope[:, :end])).astype(jnp.float32) * scale
        mask = jnp.arange(start, end)[:, None] >= jnp.arange(end)[None, :]
        p = jax.nn.softmax(jnp.where(mask, s, -jnp.inf), axis=-1).astype(v.dtype)
        outs.append(jnp.einsum('bhqk,bkhd->bqhd', p, v[:, :end]))
    o = jnp.concatenate(outs, axis=1).reshape(B, S, MLA_WIDTH)
    return (o * jax.nn.silu(gate)) @ w_out


def gla_mixer(h, w_in, w_gk2, b_gk, g_o, w_out):
    B, S, _ = h.shape
    N, C = S // GLA_CHUNK, GLA_CHUNK
    f32 = jnp.float32
    q, k, v, gate, gk = jnp.split(h @ w_in, [GLA_KEY, 2 * GLA_KEY, 2 * GLA_KEY + GLA_VAL,
                                             2 * GLA_KEY + 2 * GLA_VAL], axis=-1)
    log_a = jax.nn.log_sigmoid((gk @ w_gk2 + b_gk).astype(f32)) / GLA_TAU

    def chunks(t, d):
        return t.reshape(B, N, C, GLA_HEADS, d).transpose(0, 3, 1, 2, 4).astype(f32)

    q = chunks(q, GLA_DK) * GLA_DK ** -0.5
    k = chunks(k, GLA_DK)
    v = chunks(v, GLA_DV)
    b = jnp.cumsum(chunks(log_a, GLA_DK), axis=3)
    q_t = q * jnp.exp(b)
    k_t = k * jnp.exp(-b)
    causal = jnp.tril(jnp.ones((C, C), dtype=bool))
    att = jnp.where(causal, jnp.einsum('bhnik,bhnjk->bhnij', q_t, k_t), 0.0)
    o_intra = jnp.einsum('bhnij,bhnjv->bhniv', att, v)
    b_last = b[:, :, :, -1]
    d_state = jnp.einsum('bhnck,bhncv->bhnkv', k * jnp.exp(b_last[:, :, :, None] - b), v)

    def step(s_prev, inp):
        decay, ds = inp
        return decay[..., None] * s_prev + ds, s_prev

    s0 = jnp.zeros((B, GLA_HEADS, GLA_DK, GLA_DV), f32)
    _, s_prev = lax.scan(step, s0, (jnp.moveaxis(jnp.exp(b_last), 2, 0), jnp.moveaxis(d_state, 2, 0)))
    s_prev = jnp.moveaxis(s_prev, 0, 2)
    o = o_intra + jnp.einsum('bhnck,bhnkv->bhncv', q_t, s_prev)
    o = o.transpose(0, 2, 3, 1, 4).reshape(B, S, GLA_HEADS, GLA_DV)
    o = rmsnorm(o, g_o).reshape(B, S, GLA_VAL).astype(h.dtype)
    return (o * jax.nn.silu(gate)) @ w_out


def rglru_mixer(h, w_in, conv_w, conv_b, w_a, b_a, w_x, b_x, lam, w_out):
    B, S, _ = h.shape
    f32 = jnp.float32
    gate, u = jnp.split(h @ w_in, 2, axis=-1)
    u = causal_depthwise_conv(u, conv_w, conv_b)
    ub = u.reshape(B, S, LRU_BLOCKS, LRU_BLOCK)
    r = jax.nn.sigmoid(jnp.einsum('bsni,nij->bsnj', ub, w_a).reshape(B, S, LRU_WIDTH) + b_a).astype(f32)
    i = jax.nn.sigmoid(jnp.einsum('bsni,nij->bsnj', ub, w_x).reshape(B, S, LRU_WIDTH) + b_x).astype(f32)
    log_a = -LRU_C * r * jax.nn.softplus(-lam.astype(f32))
    a = jnp.exp(log_a)
    b = jnp.sqrt(-jnp.expm1(2.0 * log_a)) * (i * u.astype(f32))

    def combine(left, right):
        a1, b1 = left
        a2, b2 = right
        return a1 * a2, a2 * b1 + b2

    _, hs = lax.associative_scan(combine, (a, b), axis=1)
    return (hs.astype(h.dtype) * jax.nn.silu(gate)) @ w_out


def ssd_mixer(h, w_in, conv_w, conv_b, dt_bias, a_log, d_skip, g_norm, w_out):
    B, S, _ = h.shape
    N, L, G, HG, P, NS = S // SSD_CHUNK, SSD_CHUNK, SSD_GROUPS, SSD_HPG, SSD_HEADDIM, SSD_STATE
    f32 = jnp.float32
    z, xbc, dt = jnp.split(h @ w_in, [SSD_INNER, SSD_INNER + SSD_CONV_DIM], axis=-1)
    xbc = jax.nn.silu(causal_depthwise_conv(xbc, conv_w, conv_b))
    x, bm, cm = jnp.split(xbc, [SSD_INNER, SSD_INNER + G * NS], axis=-1)
    dt = jax.nn.softplus(dt.astype(f32) + dt_bias.astype(f32))
    A = -jnp.exp(a_log.astype(f32))
    x = x.astype(f32).reshape(B, N, L, G, HG, P)
    bm = bm.astype(f32).reshape(B, N, L, G, NS)
    cm = cm.astype(f32).reshape(B, N, L, G, NS)
    dt_c = dt.reshape(B, N, L, G, HG)
    cs = jnp.cumsum(jnp.moveaxis(dt_c * A.reshape(G, HG), 2, -1), axis=-1)
    xdt = x * dt_c[..., None]
    causal = jnp.tril(jnp.ones((L, L), dtype=bool))
    seg = cs[..., :, None] - cs[..., None, :]
    lmat = jnp.exp(jnp.where(causal, seg, -jnp.inf))
    cb = jnp.einsum('bnigs,bnjgs->bngij', cm, bm)
    y_diag = jnp.einsum('bngij,bnghij,bnjghp->bnighp', cb, lmat, xdt)
    decay = jnp.exp(cs[..., -1:] - cs)
    states = jnp.einsum('bnjgs,bnghj,bnjghp->bnghps', bm, decay, xdt)

    def step(s_prev, inp):
        dec, st = inp
        return dec[..., None, None] * s_prev + st, s_prev

    s0 = jnp.zeros((B, G, HG, P, NS), f32)
    _, s_prev = lax.scan(step, s0, (jnp.moveaxis(jnp.exp(cs[..., -1]), 1, 0), jnp.moveaxis(states, 1, 0)))
    s_prev = jnp.moveaxis(s_prev, 0, 1)
    y_off = jnp.einsum('bnigs,bnghps,bnghi->bnighp', cm, s_prev, jnp.exp(cs))
    y = (y_diag + y_off).reshape(B, S, SSD_HEADS, P) + d_skip.astype(f32)[:, None] * x.reshape(B, S, SSD_HEADS, P)
    y = y.reshape(B, S, SSD_INNER) * jax.nn.silu(z.astype(f32))
    y = rmsnorm(y.reshape(B, S, G, SSD_INNER // G), g_norm.reshape(G, SSD_INNER // G))
    return y.reshape(B, S, SSD_INNER).astype(h.dtype) @ w_out


def _fwd_setup_inputs(seed: int = 0) -> dict:
    key = jax.random.key(seed)
    ks = iter(jax.random.split(key, 48))
    f32 = jnp.float32

    def dense(shape, fan_in):
        return jax.random.normal(next(ks), shape, f32) * fan_in ** -0.5

    def gain(shape):
        return 1.0 + 0.02 * jax.random.normal(next(ks), shape, f32)

    def small(shape, scale=0.02):
        return scale * jax.random.normal(next(ks), shape, f32)

    nA, nB, nC, nD = (_layers_of(m) for m in range(N_MIXERS))
    x = jax.random.normal(next(ks), (BATCH, SEQ, D_MODEL), f32)
    positions = (jnp.arange(SEQ, dtype=jnp.int32)[None, :]
                 + jax.random.randint(next(ks), (BATCH, 1), 0, 4096, dtype=jnp.int32))
    norm_g = gain((DEPTH, D_MODEL))
    final_g = gain((D_MODEL,))
    mla_w_in = dense((nA, D_MODEL, MLA_IN), D_MODEL)
    mla_g_q = gain((nA, MLA_Q_RANK))
    mla_w_uq = dense((nA, MLA_Q_RANK, MLA_HEADS * MLA_QK), MLA_Q_RANK)
    mla_g_kv = gain((nA, MLA_KV_RANK))
    mla_w_ukv = dense((nA, MLA_KV_RANK, MLA_HEADS * (MLA_NOPE + MLA_V)), MLA_KV_RANK)
    mla_w_out = dense((nA, MLA_WIDTH, D_MODEL), MLA_WIDTH)
    gla_w_in = dense((nB, D_MODEL, GLA_IN), D_MODEL)
    gla_w_gk2 = dense((nB, GLA_GATE_RANK, GLA_KEY), GLA_GATE_RANK)
    gla_b_gk = small((nB, GLA_KEY), 0.1)
    gla_g_o = gain((nB, GLA_DV))
    gla_w_out = dense((nB, GLA_VAL, D_MODEL), GLA_VAL)
    lru_w_in = dense((nC, D_MODEL, 2 * LRU_WIDTH), D_MODEL)
    lru_conv_w = dense((nC, CONV_W, LRU_WIDTH), CONV_W)
    lru_conv_b = small((nC, LRU_WIDTH))
    lru_w_a = dense((nC, LRU_BLOCKS, LRU_BLOCK, LRU_BLOCK), LRU_BLOCK)
    lru_b_a = small((nC, LRU_WIDTH))
    lru_w_x = dense((nC, LRU_BLOCKS, LRU_BLOCK, LRU_BLOCK), LRU_BLOCK)
    lru_b_x = small((nC, LRU_WIDTH))
    a0 = jax.random.uniform(next(ks), (nC, LRU_WIDTH), f32, 0.9, 0.999) ** (1.0 / LRU_C)
    lru_lam = jnp.log(a0) - jnp.log1p(-a0)
    lru_w_out = dense((nC, LRU_WIDTH, D_MODEL), LRU_WIDTH)
    ssd_w_in = dense((nD, D_MODEL, SSD_IN), D_MODEL)
    ssd_conv_w = dense((nD, CONV_W, SSD_CONV_DIM), CONV_W)
    ssd_conv_b = small((nD, SSD_CONV_DIM))
    dt0 = jnp.exp(jax.random.uniform(next(ks), (nD, SSD_HEADS), f32, math.log(1e-3), math.log(1e-1)))
    ssd_dt_bias = dt0 + jnp.log(-jnp.expm1(-dt0))
    ssd_a_log = jnp.log(jax.random.uniform(next(ks), (nD, SSD_HEADS), f32, 1.0, 16.0))
    ssd_d = gain((nD, SSD_HEADS))
    ssd_g_norm = gain((nD, SSD_INNER))
    ssd_w_out = dense((nD, SSD_INNER, D_MODEL), SSD_INNER)
    return {'x': x, 'positions': positions, 'norm_g': norm_g, 'final_g': final_g,
            'mla_w_in': mla_w_in, 'mla_g_q': mla_g_q, 'mla_w_uq': mla_w_uq, 'mla_g_kv': mla_g_kv,
            'mla_w_ukv': mla_w_ukv, 'mla_w_out': mla_w_out,
            'gla_w_in': gla_w_in, 'gla_w_gk2': gla_w_gk2, 'gla_b_gk': gla_b_gk, 'gla_g_o': gla_g_o,
            'gla_w_out': gla_w_out,
            'lru_w_in': lru_w_in, 'lru_conv_w': lru_conv_w, 'lru_conv_b': lru_conv_b, 'lru_w_a': lru_w_a,
            'lru_b_a': lru_b_a, 'lru_w_x': lru_w_x, 'lru_b_x': lru_b_x, 'lru_lam': lru_lam,
            'lru_w_out': lru_w_out,
            'ssd_w_in': ssd_w_in, 'ssd_conv_w': ssd_conv_w, 'ssd_conv_b': ssd_conv_b,
            'ssd_dt_bias': ssd_dt_bias, 'ssd_a_log': ssd_a_log, 'ssd_d': ssd_d, 'ssd_g_norm': ssd_g_norm,
            'ssd_w_out': ssd_w_out}


def _fwd_reference(x, positions, norm_g, final_g,
              mla_w_in, mla_g_q, mla_w_uq, mla_g_kv, mla_w_ukv, mla_w_out,
              gla_w_in, gla_w_gk2, gla_b_gk, gla_g_o, gla_w_out,
              lru_w_in, lru_conv_w, lru_conv_b, lru_w_a, lru_b_a, lru_w_x, lru_b_x, lru_lam, lru_w_out,
              ssd_w_in, ssd_conv_w, ssd_conv_b, ssd_dt_bias, ssd_a_log, ssd_d, ssd_g_norm, ssd_w_out):
    cos, sin = rope_tables(positions)
    h = x
    for i in range(DEPTH):
        m, j = i % N_MIXERS, i // N_MIXERS
        u = rmsnorm(h, norm_g[i])
        if m == 0:
            y = mla_mixer(u, cos, sin, mla_w_in[j], mla_g_q[j], mla_w_uq[j], mla_g_kv[j], mla_w_ukv[j], mla_w_out[j])
        elif m == 1:
            y = gla_mixer(u, gla_w_in[j], gla_w_gk2[j], gla_b_gk[j], gla_g_o[j], gla_w_out[j])
        elif m == 2:
            y = rglru_mixer(u, lru_w_in[j], lru_conv_w[j], lru_conv_b[j], lru_w_a[j], lru_b_a[j],
                            lru_w_x[j], lru_b_x[j], lru_lam[j], lru_w_out[j])
        else:
            y = ssd_mixer(u, ssd_w_in[j], ssd_conv_w[j], ssd_conv_b[j], ssd_dt_bias[j], ssd_a_log[j],
                          ssd_d[j], ssd_g_norm[j], ssd_w_out[j])
        h = h + y
    return rmsnorm(h, final_g)


import jax as _jax
import jax.numpy as _jnp

TWIN_FORMAT = 'train_step'
FWD_PARAMS = ['x', 'positions', 'norm_g', 'final_g', 'mla_w_in', 'mla_g_q', 'mla_w_uq', 'mla_g_kv', 'mla_w_ukv', 'mla_w_out', 'gla_w_in', 'gla_w_gk2', 'gla_b_gk', 'gla_g_o', 'gla_w_out', 'lru_w_in', 'lru_conv_w', 'lru_conv_b', 'lru_w_a', 'lru_b_a', 'lru_w_x', 'lru_b_x', 'lru_lam', 'lru_w_out', 'ssd_w_in', 'ssd_conv_w', 'ssd_conv_b', 'ssd_dt_bias', 'ssd_a_log', 'ssd_d', 'ssd_g_norm', 'ssd_w_out']
TWIN_WEIGHTS = ['norm_g', 'final_g', 'mla_w_in', 'mla_g_q', 'mla_w_uq', 'mla_g_kv', 'mla_w_ukv', 'mla_w_out', 'gla_w_in', 'gla_w_gk2', 'gla_b_gk', 'gla_g_o', 'gla_w_out', 'lru_w_in', 'lru_conv_w', 'lru_conv_b', 'lru_w_a', 'lru_b_a', 'lru_w_x', 'lru_b_x', 'lru_lam', 'lru_w_out', 'ssd_w_in', 'ssd_conv_w', 'ssd_conv_b', 'ssd_dt_bias', 'ssd_a_log', 'ssd_d', 'ssd_g_norm', 'ssd_w_out']
TWIN_DIFF_INPUT = 'x'
TWIN_INPUTS = ['x', 'positions', 'norm_g', 'final_g', 'mla_w_in', 'mla_g_q', 'mla_w_uq', 'mla_g_kv', 'mla_w_ukv', 'mla_w_out', 'gla_w_in', 'gla_w_gk2', 'gla_b_gk', 'gla_g_o', 'gla_w_out', 'lru_w_in', 'lru_conv_w', 'lru_conv_b', 'lru_w_a', 'lru_b_a', 'lru_w_x', 'lru_b_x', 'lru_lam', 'lru_w_out', 'ssd_w_in', 'ssd_conv_w', 'ssd_conv_b', 'ssd_dt_bias', 'ssd_a_log', 'ssd_d', 'ssd_g_norm', 'ssd_w_out', 'loss_target', 'm_norm_g', 'm_final_g', 'm_mla_w_in', 'm_mla_g_q', 'm_mla_w_uq', 'm_mla_g_kv', 'm_mla_w_ukv', 'm_mla_w_out', 'm_gla_w_in', 'm_gla_w_gk2', 'm_gla_b_gk', 'm_gla_g_o', 'm_gla_w_out', 'm_lru_w_in', 'm_lru_conv_w', 'm_lru_conv_b', 'm_lru_w_a', 'm_lru_b_a', 'm_lru_w_x', 'm_lru_b_x', 'm_lru_lam', 'm_lru_w_out', 'm_ssd_w_in', 'm_ssd_conv_w', 'm_ssd_conv_b', 'm_ssd_dt_bias', 'm_ssd_a_log', 'm_ssd_d', 'm_ssd_g_norm', 'm_ssd_w_out', 'v_norm_g', 'v_final_g', 'v_mla_w_in', 'v_mla_g_q', 'v_mla_w_uq', 'v_mla_g_kv', 'v_mla_w_ukv', 'v_mla_w_out', 'v_gla_w_in', 'v_gla_w_gk2', 'v_gla_b_gk', 'v_gla_g_o', 'v_gla_w_out', 'v_lru_w_in', 'v_lru_conv_w', 'v_lru_conv_b', 'v_lru_w_a', 'v_lru_b_a', 'v_lru_w_x', 'v_lru_b_x', 'v_lru_lam', 'v_lru_w_out', 'v_ssd_w_in', 'v_ssd_conv_w', 'v_ssd_conv_b', 'v_ssd_dt_bias', 'v_ssd_a_log', 'v_ssd_d', 'v_ssd_g_norm', 'v_ssd_w_out']
TWIN_OUTPUTS = ['loss', 'grad_x', 'grad_norm_g', 'grad_final_g', 'grad_mla_w_in', 'grad_mla_g_q', 'grad_mla_w_uq', 'grad_mla_g_kv', 'grad_mla_w_ukv', 'grad_mla_w_out', 'grad_gla_w_in', 'grad_gla_w_gk2', 'grad_gla_b_gk', 'grad_gla_g_o', 'grad_gla_w_out', 'grad_lru_w_in', 'grad_lru_conv_w', 'grad_lru_conv_b', 'grad_lru_w_a', 'grad_lru_b_a', 'grad_lru_w_x', 'grad_lru_b_x', 'grad_lru_lam', 'grad_lru_w_out', 'grad_ssd_w_in', 'grad_ssd_conv_w', 'grad_ssd_conv_b', 'grad_ssd_dt_bias', 'grad_ssd_a_log', 'grad_ssd_d', 'grad_ssd_g_norm', 'grad_ssd_w_out', 'delta_norm_g', 'delta_final_g', 'delta_mla_w_in', 'delta_mla_g_q', 'delta_mla_w_uq', 'delta_mla_g_kv', 'delta_mla_w_ukv', 'delta_mla_w_out', 'delta_gla_w_in', 'delta_gla_w_gk2', 'delta_gla_b_gk', 'delta_gla_g_o', 'delta_gla_w_out', 'delta_lru_w_in', 'delta_lru_conv_w', 'delta_lru_conv_b', 'delta_lru_w_a', 'delta_lru_b_a', 'delta_lru_w_x', 'delta_lru_b_x', 'delta_lru_lam', 'delta_lru_w_out', 'delta_ssd_w_in', 'delta_ssd_conv_w', 'delta_ssd_conv_b', 'delta_ssd_dt_bias', 'delta_ssd_a_log', 'delta_ssd_d', 'delta_ssd_g_norm', 'delta_ssd_w_out', 'new_m_norm_g', 'new_m_final_g', 'new_m_mla_w_in', 'new_m_mla_g_q', 'new_m_mla_w_uq', 'new_m_mla_g_kv', 'new_m_mla_w_ukv', 'new_m_mla_w_out', 'new_m_gla_w_in', 'new_m_gla_w_gk2', 'new_m_gla_b_gk', 'new_m_gla_g_o', 'new_m_gla_w_out', 'new_m_lru_w_in', 'new_m_lru_conv_w', 'new_m_lru_conv_b', 'new_m_lru_w_a', 'new_m_lru_b_a', 'new_m_lru_w_x', 'new_m_lru_b_x', 'new_m_lru_lam', 'new_m_lru_w_out', 'new_m_ssd_w_in', 'new_m_ssd_conv_w', 'new_m_ssd_conv_b', 'new_m_ssd_dt_bias', 'new_m_ssd_a_log', 'new_m_ssd_d', 'new_m_ssd_g_norm', 'new_m_ssd_w_out', 'new_v_norm_g', 'new_v_final_g', 'new_v_mla_w_in', 'new_v_mla_g_q', 'new_v_mla_w_uq', 'new_v_mla_g_kv', 'new_v_mla_w_ukv', 'new_v_mla_w_out', 'new_v_gla_w_in', 'new_v_gla_w_gk2', 'new_v_gla_b_gk', 'new_v_gla_g_o', 'new_v_gla_w_out', 'new_v_lru_w_in', 'new_v_lru_conv_w', 'new_v_lru_conv_b', 'new_v_lru_w_a', 'new_v_lru_b_a', 'new_v_lru_w_x', 'new_v_lru_b_x', 'new_v_lru_lam', 'new_v_lru_w_out', 'new_v_ssd_w_in', 'new_v_ssd_conv_w', 'new_v_ssd_conv_b', 'new_v_ssd_dt_bias', 'new_v_ssd_a_log', 'new_v_ssd_d', 'new_v_ssd_g_norm', 'new_v_ssd_w_out']
TWIN_LEAF_KINDS = {'loss': 'loss', 'grad_x': 'grad_x', 'grad_norm_g': 'grad_w', 'grad_final_g': 'grad_w', 'grad_mla_w_in': 'grad_w', 'grad_mla_g_q': 'grad_w', 'grad_mla_w_uq': 'grad_w', 'grad_mla_g_kv': 'grad_w', 'grad_mla_w_ukv': 'grad_w', 'grad_mla_w_out': 'grad_w', 'grad_gla_w_in': 'grad_w', 'grad_gla_w_gk2': 'grad_w', 'grad_gla_b_gk': 'grad_w', 'grad_gla_g_o': 'grad_w', 'grad_gla_w_out': 'grad_w', 'grad_lru_w_in': 'grad_w', 'grad_lru_conv_w': 'grad_w', 'grad_lru_conv_b': 'grad_w', 'grad_lru_w_a': 'grad_w', 'grad_lru_b_a': 'grad_w', 'grad_lru_w_x': 'grad_w', 'grad_lru_b_x': 'grad_w', 'grad_lru_lam': 'grad_w', 'grad_lru_w_out': 'grad_w', 'grad_ssd_w_in': 'grad_w', 'grad_ssd_conv_w': 'grad_w', 'grad_ssd_conv_b': 'grad_w', 'grad_ssd_dt_bias': 'grad_w', 'grad_ssd_a_log': 'grad_w', 'grad_ssd_d': 'grad_w', 'grad_ssd_g_norm': 'grad_w', 'grad_ssd_w_out': 'grad_w', 'delta_norm_g': 'delta_w', 'delta_final_g': 'delta_w', 'delta_mla_w_in': 'delta_w', 'delta_mla_g_q': 'delta_w', 'delta_mla_w_uq': 'delta_w', 'delta_mla_g_kv': 'delta_w', 'delta_mla_w_ukv': 'delta_w', 'delta_mla_w_out': 'delta_w', 'delta_gla_w_in': 'delta_w', 'delta_gla_w_gk2': 'delta_w', 'delta_gla_b_gk': 'delta_w', 'delta_gla_g_o': 'delta_w', 'delta_gla_w_out': 'delta_w', 'delta_lru_w_in': 'delta_w', 'delta_lru_conv_w': 'delta_w', 'delta_lru_conv_b': 'delta_w', 'delta_lru_w_a': 'delta_w', 'delta_lru_b_a': 'delta_w', 'delta_lru_w_x': 'delta_w', 'delta_lru_b_x': 'delta_w', 'delta_lru_lam': 'delta_w', 'delta_lru_w_out': 'delta_w', 'delta_ssd_w_in': 'delta_w', 'delta_ssd_conv_w': 'delta_w', 'delta_ssd_conv_b': 'delta_w', 'delta_ssd_dt_bias': 'delta_w', 'delta_ssd_a_log': 'delta_w', 'delta_ssd_d': 'delta_w', 'delta_ssd_g_norm': 'delta_w', 'delta_ssd_w_out': 'delta_w', 'new_m_norm_g': 'new_m', 'new_m_final_g': 'new_m', 'new_m_mla_w_in': 'new_m', 'new_m_mla_g_q': 'new_m', 'new_m_mla_w_uq': 'new_m', 'new_m_mla_g_kv': 'new_m', 'new_m_mla_w_ukv': 'new_m', 'new_m_mla_w_out': 'new_m', 'new_m_gla_w_in': 'new_m', 'new_m_gla_w_gk2': 'new_m', 'new_m_gla_b_gk': 'new_m', 'new_m_gla_g_o': 'new_m', 'new_m_gla_w_out': 'new_m', 'new_m_lru_w_in': 'new_m', 'new_m_lru_conv_w': 'new_m', 'new_m_lru_conv_b': 'new_m', 'new_m_lru_w_a': 'new_m', 'new_m_lru_b_a': 'new_m', 'new_m_lru_w_x': 'new_m', 'new_m_lru_b_x': 'new_m', 'new_m_lru_lam': 'new_m', 'new_m_lru_w_out': 'new_m', 'new_m_ssd_w_in': 'new_m', 'new_m_ssd_conv_w': 'new_m', 'new_m_ssd_conv_b': 'new_m', 'new_m_ssd_dt_bias': 'new_m', 'new_m_ssd_a_log': 'new_m', 'new_m_ssd_d': 'new_m', 'new_m_ssd_g_norm': 'new_m', 'new_m_ssd_w_out': 'new_m', 'new_v_norm_g': 'new_v', 'new_v_final_g': 'new_v', 'new_v_mla_w_in': 'new_v', 'new_v_mla_g_q': 'new_v', 'new_v_mla_w_uq': 'new_v', 'new_v_mla_g_kv': 'new_v', 'new_v_mla_w_ukv': 'new_v', 'new_v_mla_w_out': 'new_v', 'new_v_gla_w_in': 'new_v', 'new_v_gla_w_gk2': 'new_v', 'new_v_gla_b_gk': 'new_v', 'new_v_gla_g_o': 'new_v', 'new_v_gla_w_out': 'new_v', 'new_v_lru_w_in': 'new_v', 'new_v_lru_conv_w': 'new_v', 'new_v_lru_conv_b': 'new_v', 'new_v_lru_w_a': 'new_v', 'new_v_lru_b_a': 'new_v', 'new_v_lru_w_x': 'new_v', 'new_v_lru_b_x': 'new_v', 'new_v_lru_lam': 'new_v', 'new_v_lru_w_out': 'new_v', 'new_v_ssd_w_in': 'new_v', 'new_v_ssd_conv_w': 'new_v', 'new_v_ssd_conv_b': 'new_v', 'new_v_ssd_dt_bias': 'new_v', 'new_v_ssd_a_log': 'new_v', 'new_v_ssd_d': 'new_v', 'new_v_ssd_g_norm': 'new_v', 'new_v_ssd_w_out': 'new_v'}


def _forward(args):
    return _fwd_reference(*[args[k] for k in FWD_PARAMS])


def _output_shape():
    out = _jax.eval_shape(lambda: _forward(_fwd_setup_inputs(0)))
    return out.shape, out.dtype

N_MICROBATCH = 1
ADAM_LR = 0.001
ADAM_B1 = 0.9
ADAM_B2 = 0.999
ADAM_EPS = 1e-08
ADAM_WD = 0.01
ADAM_STEP = 10
PER_EXAMPLE_BATCH_AXIS = {'x': 0, 'positions': 0, 'loss_target': 0}
SHARED_INPUTS = []
_WEIGHT_DTYPES = {'norm_g': _jnp.float32, 'final_g': _jnp.float32, 'mla_w_in': _jnp.float32, 'mla_g_q': _jnp.float32, 'mla_w_uq': _jnp.float32, 'mla_g_kv': _jnp.float32, 'mla_w_ukv': _jnp.float32, 'mla_w_out': _jnp.float32, 'gla_w_in': _jnp.float32, 'gla_w_gk2': _jnp.float32, 'gla_b_gk': _jnp.float32, 'gla_g_o': _jnp.float32, 'gla_w_out': _jnp.float32, 'lru_w_in': _jnp.float32, 'lru_conv_w': _jnp.float32, 'lru_conv_b': _jnp.float32, 'lru_w_a': _jnp.float32, 'lru_b_a': _jnp.float32, 'lru_w_x': _jnp.float32, 'lru_b_x': _jnp.float32, 'lru_lam': _jnp.float32, 'lru_w_out': _jnp.float32, 'ssd_w_in': _jnp.float32, 'ssd_conv_w': _jnp.float32, 'ssd_conv_b': _jnp.float32, 'ssd_dt_bias': _jnp.float32, 'ssd_a_log': _jnp.float32, 'ssd_d': _jnp.float32, 'ssd_g_norm': _jnp.float32, 'ssd_w_out': _jnp.float32}
MOMENT_SCALE = {'norm_g': 1.071601e-01, 'final_g': 1.601800e+01, 'mla_w_in': 4.101779e-02, 'mla_g_q': 4.054958e-02, 'mla_w_uq': 1.922984e-02, 'mla_g_kv': 7.330497e-02, 'mla_w_ukv': 2.485784e-02, 'mla_w_out': 2.947436e-02, 'gla_w_in': 9.638474e-02, 'gla_w_gk2': 1.335668e-02, 'gla_b_gk': 5.379683e-02, 'gla_g_o': 1.698357e-01, 'gla_w_out': 8.149373e-02, 'lru_w_in': 4.356192e-02, 'lru_conv_w': 4.465890e-02, 'lru_conv_b': 4.796625e-01, 'lru_w_a': 1.419499e-02, 'lru_b_a': 1.102903e-02, 'lru_w_x': 2.505740e-02, 'lru_b_x': 1.453974e-02, 'lru_lam': 2.081599e-02, 'lru_w_out': 5.161197e-02, 'ssd_w_in': 4.857100e-02, 'ssd_conv_w': 4.139656e-02, 'ssd_conv_b': 6.079686e-02, 'ssd_dt_bias': 9.977405e-02, 'ssd_a_log': 1.188704e-01, 'ssd_d': 3.074524e-01, 'ssd_g_norm': 5.463340e-02, 'ssd_w_out': 7.950501e-02}


def _to_microbatches(a, axis):
    t = _jnp.moveaxis(a, axis, 0)
    t = t.reshape((N_MICROBATCH, t.shape[0] // N_MICROBATCH) + t.shape[1:])
    return _jnp.moveaxis(t, 1, axis + 1)


def setup_inputs(seed: int = 0) -> dict:
    inp = _fwd_setup_inputs(seed)
    key = _jax.random.fold_in(_jax.random.key(seed), 7919)
    shape, _ = _output_shape()
    out = dict(inp)
    out["loss_target"] = _jax.random.normal(_jax.random.fold_in(key, 0), shape, _jnp.float32)
    for i, name in enumerate(TWIN_WEIGHTS):
        w = inp[name].astype(_jnp.float32)
        if MOMENT_SCALE is None:
            s = _jnp.sqrt(_jnp.mean(_jnp.square(w)) + 1e-30)
        else:
            s = MOMENT_SCALE[name]
        km, kv = _jax.random.split(_jax.random.fold_in(key, i + 1))
        out[name] = w
        out["m_" + name] = s * _jax.random.normal(km, w.shape, _jnp.float32)
        out["v_" + name] = (s * s) * _jax.random.uniform(kv, w.shape, _jnp.float32, 0.5, 1.5)
    if N_MICROBATCH > 1:
        for name, axis in PER_EXAMPLE_BATCH_AXIS.items():
            out[name] = _to_microbatches(out[name], axis)
    return {'x': out['x'], 'positions': out['positions'], 'norm_g': out['norm_g'], 'final_g': out['final_g'], 'mla_w_in': out['mla_w_in'], 'mla_g_q': out['mla_g_q'], 'mla_w_uq': out['mla_w_uq'], 'mla_g_kv': out['mla_g_kv'], 'mla_w_ukv': out['mla_w_ukv'], 'mla_w_out': out['mla_w_out'], 'gla_w_in': out['gla_w_in'], 'gla_w_gk2': out['gla_w_gk2'], 'gla_b_gk': out['gla_b_gk'], 'gla_g_o': out['gla_g_o'], 'gla_w_out': out['gla_w_out'], 'lru_w_in': out['lru_w_in'], 'lru_conv_w': out['lru_conv_w'], 'lru_conv_b': out['lru_conv_b'], 'lru_w_a': out['lru_w_a'], 'lru_b_a': out['lru_b_a'], 'lru_w_x': out['lru_w_x'], 'lru_b_x': out['lru_b_x'], 'lru_lam': out['lru_lam'], 'lru_w_out': out['lru_w_out'], 'ssd_w_in': out['ssd_w_in'], 'ssd_conv_w': out['ssd_conv_w'], 'ssd_conv_b': out['ssd_conv_b'], 'ssd_dt_bias': out['ssd_dt_bias'], 'ssd_a_log': out['ssd_a_log'], 'ssd_d': out['ssd_d'], 'ssd_g_norm': out['ssd_g_norm'], 'ssd_w_out': out['ssd_w_out'], 'loss_target': out['loss_target'], 'm_norm_g': out['m_norm_g'], 'm_final_g': out['m_final_g'], 'm_mla_w_in': out['m_mla_w_in'], 'm_mla_g_q': out['m_mla_g_q'], 'm_mla_w_uq': out['m_mla_w_uq'], 'm_mla_g_kv': out['m_mla_g_kv'], 'm_mla_w_ukv': out['m_mla_w_ukv'], 'm_mla_w_out': out['m_mla_w_out'], 'm_gla_w_in': out['m_gla_w_in'], 'm_gla_w_gk2': out['m_gla_w_gk2'], 'm_gla_b_gk': out['m_gla_b_gk'], 'm_gla_g_o': out['m_gla_g_o'], 'm_gla_w_out': out['m_gla_w_out'], 'm_lru_w_in': out['m_lru_w_in'], 'm_lru_conv_w': out['m_lru_conv_w'], 'm_lru_conv_b': out['m_lru_conv_b'], 'm_lru_w_a': out['m_lru_w_a'], 'm_lru_b_a': out['m_lru_b_a'], 'm_lru_w_x': out['m_lru_w_x'], 'm_lru_b_x': out['m_lru_b_x'], 'm_lru_lam': out['m_lru_lam'], 'm_lru_w_out': out['m_lru_w_out'], 'm_ssd_w_in': out['m_ssd_w_in'], 'm_ssd_conv_w': out['m_ssd_conv_w'], 'm_ssd_conv_b': out['m_ssd_conv_b'], 'm_ssd_dt_bias': out['m_ssd_dt_bias'], 'm_ssd_a_log': out['m_ssd_a_log'], 'm_ssd_d': out['m_ssd_d'], 'm_ssd_g_norm': out['m_ssd_g_norm'], 'm_ssd_w_out': out['m_ssd_w_out'], 'v_norm_g': out['v_norm_g'], 'v_final_g': out['v_final_g'], 'v_mla_w_in': out['v_mla_w_in'], 'v_mla_g_q': out['v_mla_g_q'], 'v_mla_w_uq': out['v_mla_w_uq'], 'v_mla_g_kv': out['v_mla_g_kv'], 'v_mla_w_ukv': out['v_mla_w_ukv'], 'v_mla_w_out': out['v_mla_w_out'], 'v_gla_w_in': out['v_gla_w_in'], 'v_gla_w_gk2': out['v_gla_w_gk2'], 'v_gla_b_gk': out['v_gla_b_gk'], 'v_gla_g_o': out['v_gla_g_o'], 'v_gla_w_out': out['v_gla_w_out'], 'v_lru_w_in': out['v_lru_w_in'], 'v_lru_conv_w': out['v_lru_conv_w'], 'v_lru_conv_b': out['v_lru_conv_b'], 'v_lru_w_a': out['v_lru_w_a'], 'v_lru_b_a': out['v_lru_b_a'], 'v_lru_w_x': out['v_lru_w_x'], 'v_lru_b_x': out['v_lru_b_x'], 'v_lru_lam': out['v_lru_lam'], 'v_lru_w_out': out['v_lru_w_out'], 'v_ssd_w_in': out['v_ssd_w_in'], 'v_ssd_conv_w': out['v_ssd_conv_w'], 'v_ssd_conv_b': out['v_ssd_conv_b'], 'v_ssd_dt_bias': out['v_ssd_dt_bias'], 'v_ssd_a_log': out['v_ssd_a_log'], 'v_ssd_d': out['v_ssd_d'], 'v_ssd_g_norm': out['v_ssd_g_norm'], 'v_ssd_w_out': out['v_ssd_w_out']}


def _loss(weights, diff, rest, loss_target):
    with _jax.named_scope("forward"):
        args = {**rest, TWIN_DIFF_INPUT: diff, **{k: w.astype(_WEIGHT_DTYPES[k]) for k, w in weights.items()}}
        y = _forward(args)
    with _jax.named_scope("loss_head"):
        err = _jnp.square(y.astype(_jnp.float32) - loss_target)
        return 0.5 * _jnp.sum(_jnp.mean(err, axis=-1)) if err.ndim else 0.5 * err


def _adamw(w, g, m, v):
    m = ADAM_B1 * m + (1.0 - ADAM_B1) * g
    v = ADAM_B2 * v + (1.0 - ADAM_B2) * _jnp.square(g)
    m_hat = m / (1.0 - ADAM_B1 ** ADAM_STEP)
    v_hat = v / (1.0 - ADAM_B2 ** ADAM_STEP)
    delta = -ADAM_LR * (m_hat / (_jnp.sqrt(v_hat) + ADAM_EPS) + ADAM_WD * w)
    return delta, m, v


def reference(x, positions, norm_g, final_g, mla_w_in, mla_g_q, mla_w_uq, mla_g_kv, mla_w_ukv, mla_w_out, gla_w_in, gla_w_gk2, gla_b_gk, gla_g_o, gla_w_out, lru_w_in, lru_conv_w, lru_conv_b, lru_w_a, lru_b_a, lru_w_x, lru_b_x, lru_lam, lru_w_out, ssd_w_in, ssd_conv_w, ssd_conv_b, ssd_dt_bias, ssd_a_log, ssd_d, ssd_g_norm, ssd_w_out, loss_target, m_norm_g, m_final_g, m_mla_w_in, m_mla_g_q, m_mla_w_uq, m_mla_g_kv, m_mla_w_ukv, m_mla_w_out, m_gla_w_in, m_gla_w_gk2, m_gla_b_gk, m_gla_g_o, m_gla_w_out, m_lru_w_in, m_lru_conv_w, m_lru_conv_b, m_lru_w_a, m_lru_b_a, m_lru_w_x, m_lru_b_x, m_lru_lam, m_lru_w_out, m_ssd_w_in, m_ssd_conv_w, m_ssd_conv_b, m_ssd_dt_bias, m_ssd_a_log, m_ssd_d, m_ssd_g_norm, m_ssd_w_out, v_norm_g, v_final_g, v_mla_w_in, v_mla_g_q, v_mla_w_uq, v_mla_g_kv, v_mla_w_ukv, v_mla_w_out, v_gla_w_in, v_gla_w_gk2, v_gla_b_gk, v_gla_g_o, v_gla_w_out, v_lru_w_in, v_lru_conv_w, v_lru_conv_b, v_lru_w_a, v_lru_b_a, v_lru_w_x, v_lru_b_x, v_lru_lam, v_lru_w_out, v_ssd_w_in, v_ssd_conv_w, v_ssd_conv_b, v_ssd_dt_bias, v_ssd_a_log, v_ssd_d, v_ssd_g_norm, v_ssd_w_out):
    given = dict(x=x, positions=positions, norm_g=norm_g, final_g=final_g, mla_w_in=mla_w_in, mla_g_q=mla_g_q, mla_w_uq=mla_w_uq, mla_g_kv=mla_g_kv, mla_w_ukv=mla_w_ukv, mla_w_out=mla_w_out, gla_w_in=gla_w_in, gla_w_gk2=gla_w_gk2, gla_b_gk=gla_b_gk, gla_g_o=gla_g_o, gla_w_out=gla_w_out, lru_w_in=lru_w_in, lru_conv_w=lru_conv_w, lru_conv_b=lru_conv_b, lru_w_a=lru_w_a, lru_b_a=lru_b_a, lru_w_x=lru_w_x, lru_b_x=lru_b_x, lru_lam=lru_lam, lru_w_out=lru_w_out, ssd_w_in=ssd_w_in, ssd_conv_w=ssd_conv_w, ssd_conv_b=ssd_conv_b, ssd_dt_bias=ssd_dt_bias, ssd_a_log=ssd_a_log, ssd_d=ssd_d, ssd_g_norm=ssd_g_norm, ssd_w_out=ssd_w_out, loss_target=loss_target, m_norm_g=m_norm_g, m_final_g=m_final_g, m_mla_w_in=m_mla_w_in, m_mla_g_q=m_mla_g_q, m_mla_w_uq=m_mla_w_uq, m_mla_g_kv=m_mla_g_kv, m_mla_w_ukv=m_mla_w_ukv, m_mla_w_out=m_mla_w_out, m_gla_w_in=m_gla_w_in, m_gla_w_gk2=m_gla_w_gk2, m_gla_b_gk=m_gla_b_gk, m_gla_g_o=m_gla_g_o, m_gla_w_out=m_gla_w_out, m_lru_w_in=m_lru_w_in, m_lru_conv_w=m_lru_conv_w, m_lru_conv_b=m_lru_conv_b, m_lru_w_a=m_lru_w_a, m_lru_b_a=m_lru_b_a, m_lru_w_x=m_lru_w_x, m_lru_b_x=m_lru_b_x, m_lru_lam=m_lru_lam, m_lru_w_out=m_lru_w_out, m_ssd_w_in=m_ssd_w_in, m_ssd_conv_w=m_ssd_conv_w, m_ssd_conv_b=m_ssd_conv_b, m_ssd_dt_bias=m_ssd_dt_bias, m_ssd_a_log=m_ssd_a_log, m_ssd_d=m_ssd_d, m_ssd_g_norm=m_ssd_g_norm, m_ssd_w_out=m_ssd_w_out, v_norm_g=v_norm_g, v_final_g=v_final_g, v_mla_w_in=v_mla_w_in, v_mla_g_q=v_mla_g_q, v_mla_w_uq=v_mla_w_uq, v_mla_g_kv=v_mla_g_kv, v_mla_w_ukv=v_mla_w_ukv, v_mla_w_out=v_mla_w_out, v_gla_w_in=v_gla_w_in, v_gla_w_gk2=v_gla_w_gk2, v_gla_b_gk=v_gla_b_gk, v_gla_g_o=v_gla_g_o, v_gla_w_out=v_gla_w_out, v_lru_w_in=v_lru_w_in, v_lru_conv_w=v_lru_conv_w, v_lru_conv_b=v_lru_conv_b, v_lru_w_a=v_lru_w_a, v_lru_b_a=v_lru_b_a, v_lru_w_x=v_lru_w_x, v_lru_b_x=v_lru_b_x, v_lru_lam=v_lru_lam, v_lru_w_out=v_lru_w_out, v_ssd_w_in=v_ssd_w_in, v_ssd_conv_w=v_ssd_conv_w, v_ssd_conv_b=v_ssd_conv_b, v_ssd_dt_bias=v_ssd_dt_bias, v_ssd_a_log=v_ssd_a_log, v_ssd_d=v_ssd_d, v_ssd_g_norm=v_ssd_g_norm, v_ssd_w_out=v_ssd_w_out)
    weights = {n: given[n] for n in TWIN_WEIGHTS}
    shared = {n: given[n] for n in SHARED_INPUTS}
    per_example = {n: given[n] for n in ['x', 'positions']}
    grad_fn = _jax.value_and_grad(_loss, argnums=(0, 1))

    def one_microbatch(ex, loss_target):
        ex = dict(ex)
        diff = ex.pop(TWIN_DIFF_INPUT)
        return grad_fn(weights, diff, {**shared, **ex}, loss_target)

    if N_MICROBATCH == 1:
        loss, (grad_w, grad_x) = one_microbatch(per_example, given["loss_target"])
    else:
        def body(carry, xs):
            loss_sum, grad_sum = carry
            l_k, (gw_k, gx_k) = one_microbatch(xs[0], xs[1])
            with _jax.named_scope("update"):
                return (loss_sum + l_k, _jax.tree.map(_jnp.add, grad_sum, gw_k)), gx_k

        init = (_jnp.zeros((), _jnp.float32), _jax.tree.map(_jnp.zeros_like, weights))
        (loss, grad_w), grad_x = _jax.lax.scan(body, init, (per_example, given["loss_target"]))
    with _jax.named_scope("update"):
        delta_w, new_m, new_v = {}, {}, {}
        for n in TWIN_WEIGHTS:
            delta_w[n], new_m[n], new_v[n] = _adamw(weights[n], grad_w[n], given["m_" + n], given["v_" + n])
    return (loss, grad_x, *[grad_w[n] for n in TWIN_WEIGHTS], *[delta_w[n] for n in TWIN_WEIGHTS],
            *[new_m[n] for n in TWIN_WEIGHTS], *[new_v[n] for n in TWIN_WEIGHTS])
```

```python
import functools
import math

import jax
import jax.numpy as jnp
from jax import lax
from jax.experimental import pallas as pl
from jax.experimental.pallas import tpu as pltpu

F32 = jnp.float32
BF16 = jnp.bfloat16

V7X_VMEM_BYTES = 64 * 1024 * 1024
VMEM_LIMIT = V7X_VMEM_BYTES - 8 * 1024 * 1024
LANE = 128

D_MODEL = 1024
NORM_EPS = 1e-6
MLA_HEADS, MLA_Q_RANK, MLA_KV_RANK, MLA_NOPE, MLA_ROPE, MLA_V = 16, 384, 256, 64, 32, 64
MLA_QK = MLA_NOPE + MLA_ROPE
ROPE_THETA = 10000.0
GLA_HEADS, GLA_DK, GLA_DV, GLA_RANK, GLA_TAU, GLA_CHUNK = 4, 128, 256, 16, 16.0, 64
LRU_WIDTH, LRU_BLOCKS, LRU_BLOCK, LRU_C, CONV_W = 1280, 10, 128, 8.0, 4
SSD_INNER, SSD_P, SSD_HEADS, SSD_GROUPS, SSD_HPG, SSD_STATE, SSD_CHUNK = 2048, 64, 32, 8, 4, 128, 64
ADAM_LR, ADAM_B1, ADAM_B2, ADAM_EPS, ADAM_WD, ADAM_STEP = 0.001, 0.9, 0.999, 1e-08, 0.01, 10

_CP = functools.partial(pltpu.CompilerParams, vmem_limit_bytes=VMEM_LIMIT)


def _bdot(a, b):
    return jnp.dot(a.astype(BF16), b.astype(BF16), preferred_element_type=F32)


def _bdot_nt(a, b):
    return lax.dot_general(a.astype(BF16), b.astype(BF16), (((1,), (1,)), ((), ())), preferred_element_type=F32)


def _bdot_tn(a, b):
    return lax.dot_general(a.astype(BF16), b.astype(BF16), (((0,), (0,)), ((), ())), preferred_element_type=F32)


def _hdot(a, b):
    return jnp.dot(a, b, preferred_element_type=F32, precision=lax.Precision.HIGHEST)


def _hdot_nt(a, b):
    return lax.dot_general(a, b, (((1,), (1,)), ((), ())), preferred_element_type=F32, precision=lax.Precision.HIGHEST)


def _hdot_tn(a, b):
    return lax.dot_general(a, b, (((0,), (0,)), ((), ())), preferred_element_type=F32, precision=lax.Precision.HIGHEST)


def _tri(n):
    r = lax.broadcasted_iota(jnp.int32, (n, n), 0)
    c = lax.broadcasted_iota(jnp.int32, (n, n), 1)
    return r >= c


def _rms(x, g):
    return x * lax.rsqrt(jnp.mean(x * x, axis=-1, keepdims=True) + NORM_EPS) * g


def _silu(x):
    return x * jax.nn.sigmoid(x)


def _shift_rows(x, prev, j):
    if j == 0:
        return x
    t = x.shape[0]

    def fwd_impl(x, prev):
        row = lax.broadcasted_iota(jnp.int32, x.shape, 0)
        return jnp.where(row >= j, pltpu.roll(x, j, 0), pltpu.roll(prev, j, 0))

    @jax.custom_vjp
    def sh(x, prev):
        return fwd_impl(x, prev)

    def sh_fwd(x, prev):
        return fwd_impl(x, prev), None

    def sh_bwd(_, gy):
        row = lax.broadcasted_iota(jnp.int32, gy.shape, 0)
        back = pltpu.roll(gy, t - j, 0)
        return jnp.where(row < t - j, back, 0.0), jnp.where(row >= t - j, back, 0.0)

    sh.defvjp(sh_fwd, sh_bwd)
    return sh(x, prev)


def _one_minus_exp(x):
    series = -x * (1.0 + x * (0.5 + x * (1.0 / 6.0 + x * (1.0 / 24.0 + x * (1.0 / 120.0)))))
    return jnp.where(x > -0.05, series, 1.0 - jnp.exp(x))


def _tile(n, cap):
    if n <= cap:
        return n
    best = None
    for t in range(LANE, cap + 1, LANE):
        if n % t == 0:
            best = t
    assert best is not None, (n, cap)
    return best


def _mm(name, a, b, *, ta=False, tb=False, add=None, out_dtype=F32):
    m, k = (a.shape[1], a.shape[0]) if ta else a.shape
    n, kb = (b.shape[0], b.shape[1]) if tb else (b.shape[1], b.shape[0])
    assert k == kb, (name, a.shape, b.shape, ta, tb)
    tm, tn, tk = _tile(m, 512), _tile(n, 512), _tile(k, 2048)
    nk = k // tk
    dn = (((0 if ta else 1,), (1 if tb else 0,)), ((), ()))
    has_add = add is not None

    def body(*refs):
        a_ref, b_ref = refs[0], refs[1]
        o_ref, acc = refs[-2], refs[-1]
        kk = pl.program_id(2)

        @pl.when(kk == 0)
        def _():
            acc[...] = jnp.zeros(acc.shape, F32)

        acc[...] += lax.dot_general(a_ref[...].astype(BF16), b_ref[...].astype(BF16), dn, preferred_element_type=F32)

        @pl.when(kk == nk - 1)
        def _():
            r = acc[...]
            if has_add:
                r = r + refs[2][...].astype(F32)
            o_ref[...] = r.astype(out_dtype)

    a_spec = pl.BlockSpec((tk, tm), lambda i, j, q: (q, i)) if ta else pl.BlockSpec((tm, tk), lambda i, j, q: (i, q))
    b_spec = pl.BlockSpec((tn, tk), lambda i, j, q: (j, q)) if tb else pl.BlockSpec((tk, tn), lambda i, j, q: (q, j))
    o_spec = pl.BlockSpec((tm, tn), lambda i, j, q: (i, j))
    in_specs, args = [a_spec, b_spec], [a, b]
    if has_add:
        in_specs.append(o_spec)
        args.append(add)
    return pl.pallas_call(
        body, name=name, grid=(m // tm, n // tn, nk), in_specs=in_specs, out_specs=o_spec,
        out_shape=jax.ShapeDtypeStruct((m, n), out_dtype), scratch_shapes=[pltpu.VMEM((tm, tn), F32)],
        compiler_params=_CP(dimension_semantics=("parallel", "parallel", "arbitrary")),
    )(*args)


class In:
    def __init__(self, arr, block, imap, kind="x", per_h=False, gdtype=F32):
        self.arr, self.block, self.imap, self.kind, self.per_h, self.gdtype = arr, tuple(block), imap, kind, per_h, gdtype

    def spec(self, rev_g=None):
        imap = self.imap
        if rev_g is None:
            return pl.BlockSpec(self.block, lambda h, g: imap(h, g))
        return pl.BlockSpec(self.block, lambda h, g: imap(h, rev_g - 1 - g))


class Out:
    def __init__(self, shape, dtype, block, imap):
        self.shape, self.dtype, self.block, self.imap = tuple(shape), dtype, tuple(block), imap

    def spec(self, rev_g=None):
        imap = self.imap
        if rev_g is None:
            return pl.BlockSpec(self.block, lambda h, g: imap(h, g))
        return pl.BlockSpec(self.block, lambda h, g: imap(h, rev_g - 1 - g))


def _load_f32(ref):
    v = ref[...]
    return v.astype(F32) if jnp.issubdtype(v.dtype, jnp.floating) else v


def _state_out(grid, shape):
    nd = len(shape)
    return Out(tuple(grid) + tuple(shape), F32, (None, None) + tuple(shape), lambda h, g: (h, g) + (0,) * nd)


def _op_fwd(name, f, grid, ins, outs, state_shapes=()):
    n_in, n_out, n_st = len(ins), len(outs), len(state_shapes)
    st_outs = [_state_out(grid, s) for s in state_shapes]

    def body(*refs):
        in_refs = refs[:n_in]
        out_refs = refs[n_in:n_in + n_out]
        sv_refs = refs[n_in + n_out:n_in + n_out + n_st]
        st_scr = refs[n_in + n_out + n_st:]
        g = pl.program_id(1)
        if n_st:
            @pl.when(g == 0)
            def _():
                for s in st_scr:
                    s[...] = jnp.zeros(s.shape, F32)
        vals = [_load_f32(r) for r in in_refs]
        sts = [s[...] for s in st_scr]
        o, ns = f(g, vals, sts)
        for r, v in zip(out_refs, o):
            r[...] = v.astype(r.dtype)
        for r, s in zip(sv_refs, sts):
            r[...] = s
        for s, v in zip(st_scr, ns):
            s[...] = v

    all_outs = list(outs) + st_outs
    res = pl.pallas_call(
        body, name=name, grid=tuple(grid), in_specs=[i.spec() for i in ins], out_specs=[o.spec() for o in all_outs],
        out_shape=[jax.ShapeDtypeStruct(o.shape, o.dtype) for o in all_outs],
        scratch_shapes=[pltpu.VMEM(tuple(s), F32) for s in state_shapes],
        compiler_params=_CP(dimension_semantics=("arbitrary", "arbitrary")),
    )(*[i.arr for i in ins])
    return list(res[:n_out]), list(res[n_out:])


def _op_bwd(name, f, grid, ins, outs, state_shapes, saved, douts, addto=None):
    n_in, n_out, n_st = len(ins), len(outs), len(state_shapes)
    n_g = grid[1]
    addto = addto or {}
    diff = [k for k, i in enumerate(ins) if i.kind in ("x", "p")]
    add_idx = sorted(addto)
    st_ins = [In(s, o.block, o.imap, "c") for s, o in zip(saved, [_state_out(grid, s) for s in state_shapes])]
    dout_ins = [In(d, o.block, o.imap, "c") for d, o in zip(douts, outs)]
    add_ins = [In(addto[k], ins[k].block, ins[k].imap, "c") for k in add_idx]
    g_outs = []
    for k in diff:
        i = ins[k]
        if i.kind == "x":
            g_outs.append(Out(i.arr.shape, i.gdtype, i.block, i.imap))
        else:
            g_outs.append(Out(i.arr.shape, F32, i.block, i.imap))

    def body(*refs):
        p = 0
        in_refs = refs[p:p + n_in]; p += n_in
        sv_refs = refs[p:p + n_st]; p += n_st
        do_refs = refs[p:p + n_out]; p += n_out
        ad_refs = refs[p:p + len(add_idx)]; p += len(add_idx)
        go_refs = refs[p:p + len(diff)]; p += len(diff)
        ds_scr = refs[p:]
        hh = pl.program_id(0)
        step = pl.program_id(1)
        g = n_g - 1 - step
        if n_st:
            @pl.when(step == 0)
            def _():
                for s in ds_scr:
                    s[...] = jnp.zeros(s.shape, F32)
        vals = [_load_f32(r) for r in in_refs]
        sts = [r[...] for r in sv_refs]

        def fw(dvals, states):
            full = list(vals)
            for k, v in zip(diff, dvals):
                full[k] = v
            o, ns = f(g, full, states)
            return list(o), list(ns)

        _, vjp = jax.vjp(fw, [vals[k] for k in diff], sts)
        cts = [r[...].astype(F32) for r in do_refs]
        dns = [s[...] for s in ds_scr]
        dvals, dsts = vjp((cts, dns))
        adds = dict(zip(add_idx, ad_refs))
        for k, r, dv in zip(diff, go_refs, dvals):
            i = ins[k]
            if i.kind == "x":
                if k in adds:
                    dv = dv + adds[k][...].astype(F32)
                r[...] = dv.astype(r.dtype)
            else:
                first = (step == 0) if i.per_h else jnp.logical_and(step == 0, hh == 0)

                @pl.when(first)
                def _(r=r, dv=dv):
                    r[...] = dv

                @pl.when(jnp.logical_not(first))
                def _(r=r, dv=dv):
                    r[...] += dv
        for s, v in zip(ds_scr, dsts):
            s[...] = v

    all_ins = list(ins) + st_ins + dout_ins + add_ins
    res = pl.pallas_call(
        body, name=name, grid=tuple(grid), in_specs=[i.spec(n_g) for i in all_ins], out_specs=[o.spec(n_g) for o in g_outs],
        out_shape=[jax.ShapeDtypeStruct(o.shape, o.dtype) for o in g_outs],
        scratch_shapes=[pltpu.VMEM(tuple(s), F32) for s in state_shapes],
        compiler_params=_CP(dimension_semantics=("arbitrary", "arbitrary")),
    )(*[i.arr for i in all_ins])
    return list(res)


class Op:
    def __init__(self, name, f, grid, ins, outs, state_shapes=()):
        self.name, self.f, self.grid, self.ins, self.outs, self.state_shapes = name, f, grid, ins, outs, state_shapes
        self.saved = None

    def fwd(self):
        res, self.saved = _op_fwd(self.name + "_fwd", self.f, self.grid, self.ins, self.outs, self.state_shapes)
        return res

    def bwd(self, douts, addto=None):
        return _op_bwd(self.name + "_bwd", self.f, self.grid, self.ins, self.outs, self.state_shapes, self.saved, douts, addto)


def _rows(arr, t, kind="x", gdtype=F32):
    return In(arr, (t, arr.shape[1]), lambda h, g: (g, 0), kind, gdtype=gdtype)


def _whole(arr, kind="p"):
    nd = arr.ndim
    return In(arr, arr.shape, lambda h, g: (0,) * nd, kind)


def _rows_out(s, n, t, dtype):
    return Out((s, n), dtype, (t, n), lambda h, g: (g, 0))


ROW_T = 256


def _rms_op(name, x, gain, out_dtype=BF16, gdtype=F32):
    s, n = x.shape

    def f(g, vals, sts):
        return [_rms(vals[0], vals[1])], []

    return Op(name, f, (1, s // ROW_T), [_rows(x, ROW_T, gdtype=gdtype), _whole(gain.reshape(1, n))], [_rows_out(s, n, ROW_T, out_dtype)])


def _mla_prep_op(qn, q1, q2, kn, kr, v, cos, sin):
    s = qn.shape[0]
    hd, half = MLA_HEADS, MLA_ROPE // 2

    def f(g, vals, sts):
        qn, q1, q2, kn, kr, v, cos, sin = vals
        cos_h, sin_h = jnp.tile(cos, (1, hd)), jnp.tile(sin, (1, hd))
        r1 = q1 * cos_h - q2 * sin_h
        r2 = q2 * cos_h + q1 * sin_h
        k1, k2 = kr[:, 0:half], kr[:, half:2 * half]
        kr1 = k1 * cos - k2 * sin
        kr2 = k2 * cos + k1 * sin
        zpad = jnp.zeros((qn.shape[0], LANE - MLA_QK), F32)
        qs, ks, vs = [], [], []
        for h in range(hd):
            a, b = h * MLA_NOPE, (h + 1) * MLA_NOPE
            c, d = h * half, (h + 1) * half
            qs.append(jnp.concatenate([qn[:, a:b], r1[:, c:d], r2[:, c:d], zpad], axis=1))
            ks.append(jnp.concatenate([kn[:, a:b], kr1, kr2, zpad], axis=1))
            vs.append(v[:, a:b])
        return [jnp.stack(qs, 0), jnp.stack(ks, 0), jnp.stack(vs, 0)], []

    ins = [_rows(qn, ROW_T, gdtype=BF16), _rows(q1, ROW_T, gdtype=BF16), _rows(q2, ROW_T, gdtype=BF16), _rows(kn, ROW_T, gdtype=BF16),
           _rows(kr, ROW_T, gdtype=BF16), _rows(v, ROW_T, gdtype=BF16), _rows(cos, ROW_T, "c"), _rows(sin, ROW_T, "c")]
    outs = [Out((hd, s, LANE), BF16, (hd, ROW_T, LANE), lambda h, g: (0, g, 0)),
            Out((hd, s, LANE), BF16, (hd, ROW_T, LANE), lambda h, g: (0, g, 0)),
            Out((hd, s, MLA_V), BF16, (hd, ROW_T, MLA_V), lambda h, g: (0, g, 0))]
    return Op("mla_prep", f, (1, s // ROW_T), ins, outs)


ATT_TQ = 256


def _mla_attn_op(q, k, v):
    hd, s, _ = q.shape
    scale = MLA_QK ** -0.5

    def f(g, vals, sts):
        q, k, v = vals
        sc = _bdot_nt(q, k) * scale
        r = lax.broadcasted_iota(jnp.int32, sc.shape, 0) + g * ATT_TQ
        c = lax.broadcasted_iota(jnp.int32, sc.shape, 1)
        sc = jnp.where(r >= c, sc, -1e30)
        m = lax.stop_gradient(jnp.max(sc, axis=-1, keepdims=True))
        p = jnp.exp(sc - m)
        p = p / jnp.sum(p, axis=-1, keepdims=True)
        return [_bdot(p, v)], []

    ins = [In(q, (None, ATT_TQ, LANE), lambda h, g: (h, g, 0), "x"),
           In(k, (None, s, LANE), lambda h, g: (h, 0, 0), "p", per_h=True),
           In(v, (None, s, MLA_V), lambda h, g: (h, 0, 0), "p", per_h=True)]
    outs = [Out((hd, s, MLA_V), F32, (None, ATT_TQ, MLA_V), lambda h, g: (h, g, 0))]
    return Op("mla_attn", f, (hd, s // ATT_TQ), ins, outs)


def _mla_post_op(o, gate):
    hd, s, _ = o.shape

    def f(g, vals, sts):
        o, gate = vals
        cat = jnp.concatenate([o[h] for h in range(hd)], axis=1)
        return [cat * _silu(gate)], []

    ins = [In(o, (hd, ROW_T, MLA_V), lambda h, g: (0, g, 0), "x"), _rows(gate, ROW_T, gdtype=BF16)]
    return Op("mla_post", f, (1, s // ROW_T), ins, [_rows_out(s, hd * MLA_V, ROW_T, BF16)])


def _gla_gate_op(gk, w2, b):
    s = gk.shape[0]

    def f(g, vals, sts):
        gk, w2, b = vals
        return [jax.nn.log_sigmoid(_bdot(gk, w2) + b) / GLA_TAU], []

    ins = [_rows(gk, ROW_T, gdtype=BF16), _whole(w2), _whole(b)]
    return Op("gla_gate", f, (1, s // ROW_T), ins, [_rows_out(s, GLA_HEADS * GLA_DK, ROW_T, F32)])


def _gla_core_op(q, k, v, gate, la, g_o):
    s = q.shape[0]
    c = GLA_CHUNK

    def f(g, vals, sts):
        q, k, v, gate, la, g_o = vals
        st = sts[0]
        tri = _tri(c)
        b = _hdot(tri.astype(F32), la)
        b_last = jnp.sum(la, axis=0, keepdims=True)
        qt = q * (GLA_DK ** -0.5) * jnp.exp(b)
        kt = k * jnp.exp(-b)
        att = jnp.where(tri, _bdot_nt(qt, kt), 0.0)
        o = _bdot(att, v) + _bdot(qt, st)
        d_state = _bdot_tn(k * jnp.exp(b_last - b), v)
        b_last_col = _hdot_tn(la, jnp.ones((c, 1), F32))
        new_st = jnp.exp(b_last_col) * st + d_state
        y = _rms(o, g_o) * _silu(gate)
        return [y], [new_st]

    hk = lambda h, g: (g, h)
    ins = [In(q, (c, GLA_DK), hk, "x", gdtype=BF16), In(k, (c, GLA_DK), hk, "x", gdtype=BF16), In(v, (c, GLA_DV), hk, "x", gdtype=BF16),
           In(gate, (c, GLA_DV), hk, "x", gdtype=BF16), In(la, (c, GLA_DK), hk, "x"), _whole(g_o)]
    outs = [Out((s, GLA_HEADS * GLA_DV), BF16, (c, GLA_DV), hk)]
    return Op("gla_core", f, (GLA_HEADS, s // c), ins, outs, [(GLA_DK, GLA_DV)])


LRU_T = 256


def _lru_op(gate, u, conv_w, conv_b, w_a, b_a, w_x, b_x, lam):
    s, w = u.shape
    t = LRU_T

    def f(g, vals, sts):
        gate, u, cw, cb, w_a, b_a, w_x, b_x, lam = vals
        u_prev, h_prev = sts
        uc = cb
        for kk in range(CONV_W):
            uc = uc + cw[kk] * _shift_rows(u, u_prev, CONV_W - 1 - kk)
        ra, ri = [], []
        for n in range(LRU_BLOCKS):
            blk = uc[:, n * LRU_BLOCK:(n + 1) * LRU_BLOCK]
            ra.append(_bdot(blk, w_a[n]))
            ri.append(_bdot(blk, w_x[n]))
        r = jax.nn.sigmoid(jnp.concatenate(ra, axis=1) + b_a)
        i = jax.nn.sigmoid(jnp.concatenate(ri, axis=1) + b_x)
        log_a = -LRU_C * r * jax.nn.softplus(-lam)
        a = jnp.exp(log_a)
        bb = jnp.sqrt(_one_minus_exp(2.0 * log_a)) * (i * uc)
        zero = jnp.zeros_like(a)
        sh = 1
        while sh < t:
            a_s = _shift_rows(a - 1.0, zero, sh) + 1.0
            b_s = _shift_rows(bb, zero, sh)
            bb = a * b_s + bb
            a = a * a_s
            sh *= 2
        hs = bb + a * h_prev
        last = (lax.broadcasted_iota(jnp.int32, hs.shape, 0) == t - 1).astype(F32)
        h_last = jnp.sum(hs * last, axis=0, keepdims=True)
        return [hs * _silu(gate)], [u, h_last]

    ins = [_rows(gate, t, gdtype=BF16), _rows(u, t, gdtype=BF16), _whole(conv_w), _whole(conv_b), _whole(w_a), _whole(b_a), _whole(w_x),
           _whole(b_x), _whole(lam)]
    return Op("lru_core", f, (1, s // t), ins, [_rows_out(s, w, t, BF16)], [(t, w), (1, w)])


def _ssd_conv_op(xbc, conv_w, conv_b):
    s, w = xbc.shape
    t = ROW_T
    n_x, n_b = SSD_INNER, SSD_GROUPS * SSD_STATE

    def f(g, vals, sts):
        xbc, cw, cb = vals
        acc = cb
        for kk in range(CONV_W):
            acc = acc + cw[kk] * _shift_rows(xbc, sts[0], CONV_W - 1 - kk)
        y = _silu(acc)
        return [y[:, :n_x], y[:, n_x:n_x + n_b], y[:, n_x + n_b:]], [xbc]

    ins = [_rows(xbc, t, gdtype=BF16), _whole(conv_w), _whole(conv_b)]
    outs = [_rows_out(s, n_x, t, F32), _rows_out(s, n_b, t, F32), _rows_out(s, n_b, t, F32)]
    return Op("ssd_conv", f, (1, s // t), ins, outs, [(t, w)])


def _ssd_core_op(x, bm, cm, z, dt, dt_bias, a_log, d_skip, g_norm):
    s = x.shape[0]
    c, hg, p = SSD_CHUNK, SSD_HPG, SSD_P
    gw = hg * p

    def f(g, vals, sts):
        x, bm, cm, z, dtr, dt_bias, a_log, d_skip, g_norm = vals
        st = sts[0]
        tri = _tri(c)
        dt = jax.nn.softplus(dtr + dt_bias)
        da = dt * (-jnp.exp(a_log))
        cs = _hdot(tri.astype(F32), da)
        cs_last = jnp.sum(da, axis=0, keepdims=True)
        cb = _bdot_nt(cm, bm)
        lane = lax.broadcasted_iota(jnp.int32, (c, hg), 1)
        ys, new_st = [], []
        for h in range(hg):
            cs_h = cs[:, h:h + 1]
            cs_row = _hdot_nt((lane == h).astype(F32), cs)
            seg = jnp.where(tri, cs_h - cs_row, 0.0)
            lmat = jnp.where(tri, jnp.exp(seg), 0.0)
            x_h = x[:, h * p:(h + 1) * p]
            xdt = x_h * dt[:, h:h + 1]
            y_diag = _bdot(cb * lmat, xdt)
            decay = jnp.exp(cs_last[:, h:h + 1] - cs_h)
            states = _bdot_tn(xdt * decay, bm)
            y_off = _bdot_nt(cm, st[h]) * jnp.exp(cs_h)
            new_st.append(jnp.exp(cs_last[:, h:h + 1]) * st[h] + states)
            ys.append(y_diag + y_off + d_skip[:, h:h + 1] * x_h)
        y = jnp.concatenate(ys, axis=1) * _silu(z)
        return [_rms(y, g_norm)], [jnp.stack(new_st, 0)]

    ins = [In(x, (c, gw), lambda h, g: (g, h), "x"), In(bm, (c, SSD_STATE), lambda h, g: (g, h), "x"),
           In(cm, (c, SSD_STATE), lambda h, g: (g, h), "x"), In(z, (c, gw), lambda h, g: (g, h), "x", gdtype=BF16),
           In(dt, (None, c, hg), lambda h, g: (h, g, 0), "x"),
           In(dt_bias, (None, 1, hg), lambda h, g: (h, 0, 0), "p", per_h=True),
           In(a_log, (None, 1, hg), lambda h, g: (h, 0, 0), "p", per_h=True),
           In(d_skip, (None, 1, hg), lambda h, g: (h, 0, 0), "p", per_h=True),
           In(g_norm, (1, gw), lambda h, g: (0, h), "p", per_h=True)]
    outs = [Out((s, SSD_INNER), BF16, (c, gw), lambda h, g: (g, h))]
    return Op("ssd_core", f, (SSD_GROUPS, s // c), ins, outs, [(hg, p, SSD_STATE)])


def _loss_op(h, target, final_g):
    s, n = h.shape
    t = ROW_T
    n_g = s // t

    def body(h_ref, t_ref, g_ref, loss_ref, dh_ref, dg_ref):
        step = pl.program_id(0)

        def lossf(hv, gv):
            err = _rms(hv, gv) - t_ref[...]
            return 0.5 * jnp.sum(jnp.mean(err * err, axis=-1))

        l, (dh, dg) = jax.value_and_grad(lossf, argnums=(0, 1))(h_ref[...], g_ref[...])
        dh_ref[...] = dh

        @pl.when(step == 0)
        def _():
            loss_ref[...] = jnp.zeros(loss_ref.shape, F32)
            dg_ref[...] = jnp.zeros(dg_ref.shape, F32)

        loss_ref[...] += jnp.full(loss_ref.shape, l, F32)
        dg_ref[...] += dg

    row = pl.BlockSpec((t, n), lambda g: (g, 0))
    one = pl.BlockSpec((1, n), lambda g: (0, 0))
    return pl.pallas_call(
        body, name="loss_head", grid=(n_g,), in_specs=[row, row, one],
        out_specs=[pl.BlockSpec((1, LANE), lambda g: (0, 0)), row, one],
        out_shape=[jax.ShapeDtypeStruct((1, LANE), F32), jax.ShapeDtypeStruct((s, n), F32), jax.ShapeDtypeStruct((1, n), F32)],
        compiler_params=_CP(dimension_semantics=("arbitrary",)),
    )(h, target, final_g.reshape(1, n))


def _pad_cols(w, n):
    return jnp.pad(w, ((0, 0), (0, n - w.shape[1])))


def _pad_rows(w, n):
    return jnp.pad(w, ((0, n - w.shape[0]), (0, 0)))


def _proj_bwd(tag, u, dps, ws):
    du = None
    for i, (dp, w) in enumerate(zip(dps, ws)):
        du = _mm(f"{tag}_du{i}", dp, w, tb=True, add=du)
    dws = [_mm(f"{tag}_dw{i}", u, dp, ta=True) for i, dp in enumerate(dps)]
    return du, dws


def _mla_layer(h, norm_g, w, cos, sin):
    bf = lambda a: a.astype(BF16)
    w_in, w_uq, w_ukv = w["mla_w_in"], w["mla_w_uq"], w["mla_w_ukv"]
    a0, a1, a2 = MLA_Q_RANK, MLA_Q_RANK + MLA_KV_RANK, MLA_Q_RANK + MLA_KV_RANK + MLA_ROPE
    w_cq, w_ckv, w_kr, w_g = bf(w_in[:, :a0]), bf(w_in[:, a0:a1]), bf(_pad_cols(w_in[:, a1:a2], LANE)), bf(w_in[:, a2:])
    uq = w_uq.reshape(MLA_Q_RANK, MLA_HEADS, MLA_QK)
    half = MLA_ROPE // 2
    w_qn = bf(uq[:, :, :MLA_NOPE].reshape(MLA_Q_RANK, -1))
    w_q1 = bf(uq[:, :, MLA_NOPE:MLA_NOPE + half].reshape(MLA_Q_RANK, -1))
    w_q2 = bf(uq[:, :, MLA_NOPE + half:].reshape(MLA_Q_RANK, -1))
    ukv = w_ukv.reshape(MLA_KV_RANK, MLA_HEADS, MLA_NOPE + MLA_V)
    w_kn = bf(ukv[:, :, :MLA_NOPE].reshape(MLA_KV_RANK, -1))
    w_v = bf(ukv[:, :, MLA_NOPE:].reshape(MLA_KV_RANK, -1))
    w_out = bf(w["mla_w_out"])

    n0 = _rms_op("mla_norm", h, norm_g)
    u, = n0.fwd()
    cq, ckv, kr, gate = (_mm(f"mla_in{i}", u, wi) for i, wi in enumerate((w_cq, w_ckv, w_kr, w_g)))
    nq = _rms_op("mla_qnorm", cq, w["mla_g_q"], gdtype=BF16)
    nkv = _rms_op("mla_kvnorm", ckv, w["mla_g_kv"], gdtype=BF16)
    qn_, = nq.fwd()
    kvn_, = nkv.fwd()
    qn, q1, q2 = (_mm(f"mla_uq{i}", qn_, wi) for i, wi in enumerate((w_qn, w_q1, w_q2)))
    kn, v = (_mm(f"mla_ukv{i}", kvn_, wi) for i, wi in enumerate((w_kn, w_v)))
    prep = _mla_prep_op(qn, q1, q2, kn, kr, v, cos, sin)
    qh, kh, vh = prep.fwd()
    attn = _mla_attn_op(qh, kh, vh)
    o, = attn.fwd()
    post = _mla_post_op(o, gate)
    y, = post.fwd()
    h_out = _mm("mla_out", y, w_out, add=h)

    def bwd(dh):
        dy = _mm("mla_out_dy", dh, w_out, tb=True, out_dtype=BF16)
        d_w_out = _mm("mla_out_dw", y, dh, ta=True)
        do, dgate = post.bwd([dy])
        dqh, dkh, dvh = attn.bwd([do])
        dqn, dq1, dq2, dkn, dkr, dv = prep.bwd([dqh, dkh, dvh])
        dqn_, d_uq = _proj_bwd("mla_uq", qn_, (dqn, dq1, dq2), (w_qn, w_q1, w_q2))
        dkvn_, d_ukv = _proj_bwd("mla_ukv", kvn_, (dkn, dv), (w_kn, w_v))
        dcq, d_g_q = nq.bwd([dqn_])
        dckv, d_g_kv = nkv.bwd([dkvn_])
        du, d_in = _proj_bwd("mla_in", u, (dcq, dckv, dkr, dgate), (w_cq, w_ckv, w_kr, w_g))
        dh_in, d_norm = n0.bwd([du], addto={0: dh})
        shp = (MLA_Q_RANK, MLA_HEADS, -1)
        g_uq = jnp.concatenate([d_uq[0].reshape(shp), d_uq[1].reshape(shp), d_uq[2].reshape(shp)], axis=2).reshape(MLA_Q_RANK, -1)
        shp = (MLA_KV_RANK, MLA_HEADS, -1)
        g_ukv = jnp.concatenate([d_ukv[0].reshape(shp), d_ukv[1].reshape(shp)], axis=2).reshape(MLA_KV_RANK, -1)
        g_in = jnp.concatenate([d_in[0], d_in[1], d_in[2][:, :MLA_ROPE], d_in[3]], axis=1)
        return dh_in, d_norm, {"mla_w_in": g_in, "mla_g_q": d_g_q.reshape(-1), "mla_w_uq": g_uq, "mla_g_kv": d_g_kv.reshape(-1),
                               "mla_w_ukv": g_ukv, "mla_w_out": d_w_out}

    return h_out, bwd


def _gla_layer(h, norm_g, w):
    bf = lambda a: a.astype(BF16)
    w_in = w["gla_w_in"]
    nk, nv = GLA_HEADS * GLA_DK, GLA_HEADS * GLA_DV
    cuts = (0, nk, 2 * nk, 2 * nk + nv, 2 * nk + 2 * nv)
    w_q, w_k, w_v, w_g = (bf(w_in[:, cuts[i]:cuts[i + 1]]) for i in range(4))
    w_gk = bf(_pad_cols(w_in[:, cuts[4]:], LANE))
    w2 = _pad_rows(w["gla_w_gk2"], LANE)
    b_gk = w["gla_b_gk"].reshape(1, -1)
    g_o = w["gla_g_o"].reshape(1, -1)
    w_out = bf(w["gla_w_out"])

    n0 = _rms_op("gla_norm", h, norm_g)
    u, = n0.fwd()
    q, k, v, gate, gk = (_mm(f"gla_in{i}", u, wi) for i, wi in enumerate((w_q, w_k, w_v, w_g, w_gk)))
    gop = _gla_gate_op(gk, w2, b_gk)
    la, = gop.fwd()
    core = _gla_core_op(q, k, v, gate, la, g_o)
    y, = core.fwd()
    h_out = _mm("gla_out", y, w_out, add=h)

    def bwd(dh):
        dy = _mm("gla_out_dy", dh, w_out, tb=True, out_dtype=BF16)
        d_w_out = _mm("gla_out_dw", y, dh, ta=True)
        dq, dk, dv, dgate, dla, d_g_o = core.bwd([dy])
        dgk, d_w2, d_b = gop.bwd([dla])
        du, d_in = _proj_bwd("gla_in", u, (dq, dk, dv, dgate, dgk), (w_q, w_k, w_v, w_g, w_gk))
        dh_in, d_norm = n0.bwd([du], addto={0: dh})
        g_in = jnp.concatenate([d_in[0], d_in[1], d_in[2], d_in[3], d_in[4][:, :GLA_RANK]], axis=1)
        return dh_in, d_norm, {"gla_w_in": g_in, "gla_w_gk2": d_w2[:GLA_RANK], "gla_b_gk": d_b.reshape(-1), "gla_g_o": d_g_o.reshape(-1),
                               "gla_w_out": d_w_out}

    return h_out, bwd


def _lru_layer(h, norm_g, w):
    bf = lambda a: a.astype(BF16)
    w_in = w["lru_w_in"]
    w_g, w_u = bf(w_in[:, :LRU_WIDTH]), bf(w_in[:, LRU_WIDTH:])
    row = lambda a: a.reshape(1, -1)
    w_out = bf(w["lru_w_out"])

    n0 = _rms_op("lru_norm", h, norm_g)
    u_, = n0.fwd()
    gate, u = (_mm(f"lru_in{i}", u_, wi) for i, wi in enumerate((w_g, w_u)))
    core = _lru_op(gate, u, w["lru_conv_w"].reshape(CONV_W, 1, -1), row(w["lru_conv_b"]), w["lru_w_a"], row(w["lru_b_a"]), w["lru_w_x"],
                   row(w["lru_b_x"]), row(w["lru_lam"]))
    y, = core.fwd()
    h_out = _mm("lru_out", y, w_out, add=h)

    def bwd(dh):
        dy = _mm("lru_out_dy", dh, w_out, tb=True, out_dtype=BF16)
        d_w_out = _mm("lru_out_dw", y, dh, ta=True)
        dgate, du, d_cw, d_cb, d_wa, d_ba, d_wx, d_bx, d_lam = core.bwd([dy])
        du_, d_in = _proj_bwd("lru_in", u_, (dgate, du), (w_g, w_u))
        dh_in, d_norm = n0.bwd([du_], addto={0: dh})
        return dh_in, d_norm, {"lru_w_in": jnp.concatenate(d_in, axis=1), "lru_conv_w": d_cw.reshape(CONV_W, -1), "lru_conv_b": d_cb.reshape(-1),
                               "lru_w_a": d_wa, "lru_b_a": d_ba.reshape(-1), "lru_w_x": d_wx, "lru_b_x": d_bx.reshape(-1),
                               "lru_lam": d_lam.reshape(-1), "lru_w_out": d_w_out}

    return h_out, bwd


def _ssd_layer(h, norm_g, w):
    bf = lambda a: a.astype(BF16)
    s = h.shape[0]
    w_in = w["ssd_w_in"]
    conv_dim = SSD_INNER + 2 * SSD_GROUPS * SSD_STATE
    w_z, w_xbc = bf(w_in[:, :SSD_INNER]), bf(w_in[:, SSD_INNER:SSD_INNER + conv_dim])
    w_dt = bf(_pad_cols(w_in[:, SSD_INNER + conv_dim:], LANE))
    grp = lambda a: a.reshape(SSD_GROUPS, 1, SSD_HPG)
    w_out = bf(w["ssd_w_out"])

    n0 = _rms_op("ssd_norm", h, norm_g)
    u, = n0.fwd()
    z, xbc, dtp = (_mm(f"ssd_in{i}", u, wi) for i, wi in enumerate((w_z, w_xbc, w_dt)))
    conv = _ssd_conv_op(xbc, w["ssd_conv_w"].reshape(CONV_W, 1, -1), w["ssd_conv_b"].reshape(1, -1))
    x, bm, cm = conv.fwd()
    dt = dtp[:, :SSD_HEADS].reshape(s, SSD_GROUPS, SSD_HPG).transpose(1, 0, 2)
    core = _ssd_core_op(x, bm, cm, z, dt, grp(w["ssd_dt_bias"]), grp(w["ssd_a_log"]), grp(w["ssd_d"]), w["ssd_g_norm"].reshape(1, -1))
    y, = core.fwd()
    h_out = _mm("ssd_out", y, w_out, add=h)

    def bwd(dh):
        dy = _mm("ssd_out_dy", dh, w_out, tb=True, out_dtype=BF16)
        d_w_out = _mm("ssd_out_dw", y, dh, ta=True)
        dx, dbm, dcm, dz, ddt, d_dtb, d_alog, d_d, d_gn = core.bwd([dy])
        dxbc, d_cw, d_cb = conv.bwd([dx, dbm, dcm])
        ddtp = _pad_cols(ddt.transpose(1, 0, 2).reshape(s, SSD_HEADS), LANE).astype(BF16)
        du, d_in = _proj_bwd("ssd_in", u, (dz, dxbc, ddtp), (w_z, w_xbc, w_dt))
        dh_in, d_norm = n0.bwd([du], addto={0: dh})
        g_in = jnp.concatenate([d_in[0], d_in[1], d_in[2][:, :SSD_HEADS]], axis=1)
        return dh_in, d_norm, {"ssd_w_in": g_in, "ssd_conv_w": d_cw.reshape(CONV_W, -1), "ssd_conv_b": d_cb.reshape(-1),
                               "ssd_dt_bias": d_dtb.reshape(-1), "ssd_a_log": d_alog.reshape(-1), "ssd_d": d_d.reshape(-1),
                               "ssd_g_norm": d_gn.reshape(-1), "ssd_w_out": d_w_out}

    return h_out, bwd


def _rope_tables(positions):
    inv_freq = ROPE_THETA ** (-jnp.arange(0, MLA_ROPE, 2, dtype=F32) / MLA_ROPE)
    ang = positions.astype(F32)[:, None] * inv_freq
    return jnp.cos(ang), jnp.sin(ang)


def _local_step(x, positions, target, w):
    cos, sin = _rope_tables(positions)
    ng = w["norm_g"]
    h1, b0 = _mla_layer(x, ng[0], w, cos, sin)
    h2, b1 = _gla_layer(h1, ng[1], w)
    h3, b2 = _lru_layer(h2, ng[2], w)
    h4, b3 = _ssd_layer(h3, ng[3], w)
    loss, dh, d_final = _loss_op(h4, target, w["final_g"])
    grads = {"final_g": d_final.reshape(-1)}
    d_norms = [None] * 4
    for i, b in ((3, b3), (2, b2), (1, b1), (0, b0)):
        dh, dn, gw = b(dh)
        d_norms[i] = dn
        grads.update(gw)
    grads["norm_g"] = jnp.concatenate(d_norms, axis=0)
    return loss[0, 0], dh, grads


WEIGHTS = ["norm_g", "final_g", "mla_w_in", "mla_g_q", "mla_w_uq", "mla_g_kv", "mla_w_ukv", "mla_w_out", "gla_w_in", "gla_w_gk2", "gla_b_gk",
           "gla_g_o", "gla_w_out", "lru_w_in", "lru_conv_w", "lru_conv_b", "lru_w_a", "lru_b_a", "lru_w_x", "lru_b_x", "lru_lam", "lru_w_out",
           "ssd_w_in", "ssd_conv_w", "ssd_conv_b", "ssd_dt_bias", "ssd_a_log", "ssd_d", "ssd_g_norm", "ssd_w_out"]
BIG = ["mla_w_in", "mla_w_uq", "mla_w_ukv", "mla_w_out", "gla_w_in", "gla_w_out", "lru_w_in", "lru_w_out", "ssd_w_in", "ssd_w_out"]
SMALL = ["gla_w_gk2", "gla_b_gk", "gla_g_o", "lru_conv_w", "lru_conv_b", "lru_b_a", "lru_b_x", "lru_lam", "ssd_conv_w", "ssd_conv_b", "ssd_g_norm"]
REPL = ["norm_g", "final_g", "mla_g_q", "mla_g_kv", "lru_w_a", "lru_w_x", "ssd_dt_bias", "ssd_a_log", "ssd_d"]
N_CHIPS, N_DEV = 4, 8
PACK_W = 1024
ADAM_ROWS = 256
SMALL_ROWS = 64


def _shard_axis(name):
    return 0 if name.endswith("_w_out") else -1


def _pack(arrs, dtype, row_mult):
    flat = jnp.concatenate([a.reshape(-1).astype(dtype) for a in arrs])
    per = PACK_W * row_mult
    total = -(-flat.shape[0] // per) * per
    return jnp.pad(flat, (0, total - flat.shape[0])).reshape(-1, PACK_W)


def _unpack(buf, shapes):
    flat = buf.reshape(-1)
    out, off = [], 0
    for s in shapes:
        n = math.prod(s)
        out.append(flat[off:off + n].reshape(s))
        off += n
    return out


def _mesh_pos():
    return lax.axis_index("x"), lax.axis_index("y"), lax.axis_index("c")


def _gather_weights(wbig, wsmall):
    ops = (wbig, wsmall)
    n_op = len(ops)

    def body(*refs):
        srcs, dsts = refs[:n_op], refs[n_op:2 * n_op]
        send_sems, recv_sems, local_sems = refs[2 * n_op:]
        x, y, c = _mesh_pos()
        chips = [(1 - x, y), (x, 1 - y), (1 - x, 1 - y)]
        mine = 2 * x + y
        local = [pltpu.make_async_copy(srcs[i], dsts[i].at[mine], local_sems.at[i]) for i in range(n_op)]
        for cp in local:
            cp.start()

        def copy(i, k, slot, to):
            return pltpu.make_async_remote_copy(src_ref=srcs[i], dst_ref=dsts[i].at[slot], send_sem=send_sems.at[i * 3 + k],
                                                recv_sem=recv_sems.at[i * 3 + k], device_id=to, device_id_type=pl.DeviceIdType.MESH)

        sends = [copy(i, k, mine, (*chip, c)) for i in range(n_op) for k, chip in enumerate(chips)]
        for cp in sends:
            cp.start()
        for i in range(n_op):
            for k, (px, py) in enumerate(chips):
                copy(i, k, 2 * px + py, (x, y, c)).wait_recv()
        for cp in sends:
            cp.wait_send()
        for cp in local:
            cp.wait()

    any_spec = pl.BlockSpec(memory_space=pl.ANY)
    return pl.pallas_call(
        body, name="gather_weights", in_specs=[any_spec] * n_op, out_specs=[any_spec] * n_op,
        out_shape=[jax.ShapeDtypeStruct((N_CHIPS,) + o.shape, o.dtype) for o in ops],
        scratch_shapes=[pltpu.SemaphoreType.DMA((3 * n_op,)), pltpu.SemaphoreType.DMA((3 * n_op,)), pltpu.SemaphoreType.DMA((n_op,))],
    )(*ops)


def _exchange_grads(gbig, gsmall, grepl):
    ops = (gbig, gsmall, grepl)
    per_chip = (True, True, False)
    n_op = len(ops)

    def body(*refs):
        srcs, dsts = refs[:n_op], refs[n_op:2 * n_op]
        send_sems, recv_sems, local_sems = refs[2 * n_op:]
        x, y, c = _mesh_pos()
        sibling = (x, y, 1 - c)
        chips = [(1 - x, y), (x, 1 - y), (1 - x, 1 - y)]

        def dev(px, py, pc):
            return 4 * px + 2 * py + pc

        def part(i, px, py):
            return srcs[i].at[2 * px + py] if per_chip[i] else srcs[i]

        def copy(i, k, src, slot, to):
            return pltpu.make_async_remote_copy(src_ref=src, dst_ref=dsts[i].at[slot], send_sem=send_sems.at[i * 7 + k],
                                                recv_sem=recv_sems.at[i * 7 + k], device_id=to, device_id_type=pl.DeviceIdType.MESH)

        me = dev(x, y, c)
        local = [pltpu.make_async_copy(part(i, x, y), dsts[i].at[me], local_sems.at[i]) for i in range(n_op)]
        for cp in local:
            cp.start()
        first = []
        for i in range(n_op):
            first.append(copy(i, 0, part(i, x, y), me, sibling))
            first += [copy(i, 1 + k, part(i, px, py), me, (px, py, c)) for k, (px, py) in enumerate(chips)]
        for cp in first:
            cp.start()
        passed = []
        for i in range(n_op):
            for k, (px, py) in enumerate(chips):
                slot = dev(px, py, c)
                copy(i, 1 + k, srcs[i].at[0] if per_chip[i] else srcs[i], slot, (x, y, c)).wait_recv()
                fwd = copy(i, 4 + k, dsts[i].at[slot], slot, sibling)
                fwd.start()
                passed.append(fwd)
        for i in range(n_op):
            dummy = srcs[i].at[0] if per_chip[i] else srcs[i]
            copy(i, 0, dummy, dev(x, y, 1 - c), (x, y, c)).wait_recv()
            for k, (px, py) in enumerate(chips):
                copy(i, 4 + k, dummy, dev(px, py, 1 - c), (x, y, c)).wait_recv()
        for cp in first + passed:
            cp.wait_send()
        for cp in local:
            cp.wait()

    any_spec = pl.BlockSpec(memory_space=pl.ANY)
    return pl.pallas_call(
        body, name="exchange_grads", in_specs=[any_spec] * n_op, out_specs=[any_spec] * n_op,
        out_shape=[jax.ShapeDtypeStruct((N_DEV,) + o.shape[-2:], o.dtype) for o in ops],
        scratch_shapes=[pltpu.SemaphoreType.DMA((7 * n_op,)), pltpu.SemaphoreType.DMA((7 * n_op,)), pltpu.SemaphoreType.DMA((n_op,))],
    )(*ops)


def _adamw(name, parts, w, m, v):
    rows = w.shape[0]
    t = ADAM_ROWS if rows % ADAM_ROWS == 0 else SMALL_ROWS
    assert rows % t == 0, (name, rows)
    c1 = 1.0 - ADAM_B1 ** ADAM_STEP
    c2 = 1.0 - ADAM_B2 ** ADAM_STEP

    def body(p_ref, w_ref, m_ref, v_ref, g_ref, d_ref, nm_ref, nv_ref):
        g = p_ref[0].astype(F32)
        for d in range(1, N_DEV):
            g = g + p_ref[d].astype(F32)
        nm = ADAM_B1 * m_ref[...] + (1.0 - ADAM_B1) * g
        nv = ADAM_B2 * v_ref[...] + (1.0 - ADAM_B2) * (g * g)
        g_ref[...] = g
        nm_ref[...] = nm
        nv_ref[...] = nv
        d_ref[...] = -ADAM_LR * ((nm / c1) / (jnp.sqrt(nv / c2) + ADAM_EPS) + ADAM_WD * w_ref[...])

    row = pl.BlockSpec((t, PACK_W), lambda i: (i, 0))
    return pl.pallas_call(
        body, name=name, grid=(rows // t,), in_specs=[pl.BlockSpec((N_DEV, t, PACK_W), lambda i: (0, i, 0)), row, row, row],
        out_specs=[row] * 4, out_shape=[jax.ShapeDtypeStruct((rows, PACK_W), F32)] * 4,
        compiler_params=_CP(dimension_semantics=("parallel",)),
    )(parts, w, m, v)


def _train_step(x, positions, target, wts, ms, vs):
    shard_shapes = {n: wts[n].shape for n in WEIGHTS}
    big_shapes = [shard_shapes[n] for n in BIG]
    small_shapes = [shard_shapes[n] for n in SMALL]
    repl_shapes = [shard_shapes[n] for n in REPL]

    gbig, gsmall = _gather_weights(_pack([wts[n] for n in BIG], BF16, ADAM_ROWS), _pack([wts[n] for n in SMALL], F32, SMALL_ROWS))
    full = {n: wts[n] for n in REPL}
    per_chip_big = [_unpack(gbig[j], big_shapes) for j in range(N_CHIPS)]
    per_chip_small = [_unpack(gsmall[j], small_shapes) for j in range(N_CHIPS)]
    for k, n in enumerate(BIG):
        full[n] = jnp.concatenate([per_chip_big[j][k] for j in range(N_CHIPS)], axis=_shard_axis(n))
    for k, n in enumerate(SMALL):
        full[n] = jnp.concatenate([per_chip_small[j][k] for j in range(N_CHIPS)], axis=_shard_axis(n))

    loss, dx, grads = _local_step(x, positions, target, full)

    def shard(n, j):
        return jnp.split(grads[n], N_CHIPS, axis=_shard_axis(n))[j]

    pbig = jnp.stack([_pack([shard(n, j) for n in BIG], BF16, ADAM_ROWS) for j in range(N_CHIPS)])
    psmall = jnp.stack([_pack([shard(n, j) for n in SMALL], F32, SMALL_ROWS) for j in range(N_CHIPS)])
    prepl = _pack([grads[n] for n in REPL], F32, SMALL_ROWS)
    abig, asmall, arepl = _exchange_grads(pbig, psmall, prepl)

    out = {}
    for tag, names, parts, shapes, dtype_rows in (("adam_big", BIG, abig, big_shapes, ADAM_ROWS), ("adam_small", SMALL, asmall, small_shapes, SMALL_ROWS),
                                                   ("adam_repl", REPL, arepl, repl_shapes, SMALL_ROWS)):
        packed = [_pack([d[n] for n in names], F32, dtype_rows) for d in (wts, ms, vs)]
        res = _adamw(tag, parts, *packed)
        for kind, buf in zip(("grad", "delta", "new_m", "new_v"), res):
            for n, a in zip(names, _unpack(buf, shapes)):
                out[kind, n] = a
    loss = lax.psum(loss, ("x", "y", "c"))
    return loss, dx, out


def kernel(x, positions, norm_g, final_g, mla_w_in, mla_g_q, mla_w_uq, mla_g_kv, mla_w_ukv, mla_w_out, gla_w_in, gla_w_gk2, gla_b_gk, gla_g_o, gla_w_out, lru_w_in, lru_conv_w, lru_conv_b, lru_w_a, lru_b_a, lru_w_x, lru_b_x, lru_lam, lru_w_out, ssd_w_in, ssd_conv_w, ssd_conv_b, ssd_dt_bias, ssd_a_log, ssd_d, ssd_g_norm, ssd_w_out, loss_target, m_norm_g, m_final_g, m_mla_w_in, m_mla_g_q, m_mla_w_uq, m_mla_g_kv, m_mla_w_ukv, m_mla_w_out, m_gla_w_in, m_gla_w_gk2, m_gla_b_gk, m_gla_g_o, m_gla_w_out, m_lru_w_in, m_lru_conv_w, m_lru_conv_b, m_lru_w_a, m_lru_b_a, m_lru_w_x, m_lru_b_x, m_lru_lam, m_lru_w_out, m_ssd_w_in, m_ssd_conv_w, m_ssd_conv_b, m_ssd_dt_bias, m_ssd_a_log, m_ssd_d, m_ssd_g_norm, m_ssd_w_out, v_norm_g, v_final_g, v_mla_w_in, v_mla_g_q, v_mla_w_uq, v_mla_g_kv, v_mla_w_ukv, v_mla_w_out, v_gla_w_in, v_gla_w_gk2, v_gla_b_gk, v_gla_g_o, v_gla_w_out, v_lru_w_in, v_lru_conv_w, v_lru_conv_b, v_lru_w_a, v_lru_b_a, v_lru_w_x, v_lru_b_x, v_lru_lam, v_lru_w_out, v_ssd_w_in, v_ssd_conv_w, v_ssd_conv_b, v_ssd_dt_bias, v_ssd_a_log, v_ssd_d, v_ssd_g_norm, v_ssd_w_out):
    given = dict(locals())
    stacked = [n for n in WEIGHTS if n not in ("norm_g", "final_g")]

    def blocks(prefix):
        return {n: (given[prefix + n][0] if n in stacked else given[prefix + n]) for n in WEIGHTS}

    loss, dx, out = _train_step(x[0], positions[0], loss_target[0], blocks(""), blocks("m_"), blocks("v_"))
    res = [loss, dx[None]]
    for kind in ("grad", "delta", "new_m", "new_v"):
        res += [(out[kind, n][None] if n in stacked else out[kind, n]) for n in WEIGHTS]
    return tuple(res)
```

```python
import functools
import math

import jax
import jax.numpy as jnp
from jax import lax
from jax.experimental import pallas as pl
from jax.experimental.pallas import tpu as pltpu

F32 = jnp.float32
BF16 = jnp.bfloat16

V7X_VMEM_BYTES = 64 * 1024 * 1024
VMEM_LIMIT = V7X_VMEM_BYTES - 8 * 1024 * 1024
LANE = 128

D_MODEL = 1024
NORM_EPS = 1e-6
MLA_HEADS, MLA_Q_RANK, MLA_KV_RANK, MLA_NOPE, MLA_ROPE, MLA_V = 16, 384, 256, 64, 32, 64
MLA_QK = MLA_NOPE + MLA_ROPE
ROPE_THETA = 10000.0
GLA_HEADS, GLA_DK, GLA_DV, GLA_RANK, GLA_TAU, GLA_CHUNK = 4, 128, 256, 16, 16.0, 64
LRU_WIDTH, LRU_BLOCKS, LRU_BLOCK, LRU_C, CONV_W = 1280, 10, 128, 8.0, 4
SSD_INNER, SSD_P, SSD_HEADS, SSD_GROUPS, SSD_HPG, SSD_STATE, SSD_CHUNK = 2048, 64, 32, 8, 4, 128, 64
ADAM_LR, ADAM_B1, ADAM_B2, ADAM_EPS, ADAM_WD, ADAM_STEP = 0.001, 0.9, 0.999, 1e-08, 0.01, 10

_CP = functools.partial(pltpu.CompilerParams, vmem_limit_bytes=VMEM_LIMIT)


def _bdot(a, b):
    return jnp.dot(a.astype(BF16), b.astype(BF16), preferred_element_type=F32)


def _bdot_nt(a, b):
    return lax.dot_general(a.astype(BF16), b.astype(BF16), (((1,), (1,)), ((), ())), preferred_element_type=F32)


def _bdot_tn(a, b):
    return lax.dot_general(a.astype(BF16), b.astype(BF16), (((0,), (0,)), ((), ())), preferred_element_type=F32)


def _hdot(a, b):
    return jnp.dot(a, b, preferred_element_type=F32, precision=lax.Precision.HIGHEST)


def _hdot_nt(a, b):
    return lax.dot_general(a, b, (((1,), (1,)), ((), ())), preferred_element_type=F32, precision=lax.Precision.HIGHEST)


def _hdot_tn(a, b):
    return lax.dot_general(a, b, (((0,), (0,)), ((), ())), preferred_element_type=F32, precision=lax.Precision.HIGHEST)


def _tri(n):
    r = lax.broadcasted_iota(jnp.int32, (n, n), 0)
    c = lax.broadcasted_iota(jnp.int32, (n, n), 1)
    return r >= c


def _rms(x, g):
    return x * lax.rsqrt(jnp.mean(x * x, axis=-1, keepdims=True) + NORM_EPS) * g


def _silu(x):
    return x * jax.nn.sigmoid(x)


def _shift_rows(x, prev, j):
    if j == 0:
        return x
    t = x.shape[0]

    def fwd_impl(x, prev):
        row = lax.broadcasted_iota(jnp.int32, x.shape, 0)
        return jnp.where(row >= j, pltpu.roll(x, j, 0), pltpu.roll(prev, j, 0))

    @jax.custom_vjp
    def sh(x, prev):
        return fwd_impl(x, prev)

    def sh_fwd(x, prev):
        return fwd_impl(x, prev), None

    def sh_bwd(_, gy):
        row = lax.broadcasted_iota(jnp.int32, gy.shape, 0)
        back = pltpu.roll(gy, t - j, 0)
        return jnp.where(row < t - j, back, 0.0), jnp.where(row >= t - j, back, 0.0)

    sh.defvjp(sh_fwd, sh_bwd)
    return sh(x, prev)


def _one_minus_exp(x):
    series = -x * (1.0 + x * (0.5 + x * (1.0 / 6.0 + x * (1.0 / 24.0 + x * (1.0 / 120.0)))))
    return jnp.where(x > -0.05, series, 1.0 - jnp.exp(x))


def _tile(n, cap):
    if n <= cap:
        return n
    best = None
    for t in range(LANE, cap + 1, LANE):
        if n % t == 0:
            best = t
    assert best is not None, (n, cap)
    return best


def _mm(name, a, b, *, ta=False, tb=False, add=None, out_dtype=F32):
    m, k = (a.shape[1], a.shape[0]) if ta else a.shape
    n, kb = (b.shape[0], b.shape[1]) if tb else (b.shape[1], b.shape[0])
    assert k == kb, (name, a.shape, b.shape, ta, tb)
    tm, tn, tk = _tile(m, 512), _tile(n, 512), _tile(k, 2048)
    nk = k // tk
    dn = (((0 if ta else 1,), (1 if tb else 0,)), ((), ()))
    has_add = add is not None

    def body(*refs):
        a_ref, b_ref = refs[0], refs[1]
        o_ref, acc = refs[-2], refs[-1]
        kk = pl.program_id(2)

        @pl.when(kk == 0)
        def _():
            acc[...] = jnp.zeros(acc.shape, F32)

        acc[...] += lax.dot_general(a_ref[...].astype(BF16), b_ref[...].astype(BF16), dn, preferred_element_type=F32)

        @pl.when(kk == nk - 1)
        def _():
            r = acc[...]
            if has_add:
                r = r + refs[2][...].astype(F32)
            o_ref[...] = r.astype(out_dtype)

    a_spec = pl.BlockSpec((tk, tm), lambda i, j, q: (q, i)) if ta else pl.BlockSpec((tm, tk), lambda i, j, q: (i, q))
    b_spec = pl.BlockSpec((tn, tk), lambda i, j, q: (j, q)) if tb else pl.BlockSpec((tk, tn), lambda i, j, q: (q, j))
    o_spec = pl.BlockSpec((tm, tn), lambda i, j, q: (i, j))
    in_specs, args = [a_spec, b_spec], [a, b]
    if has_add:
        in_specs.append(o_spec)
        args.append(add)
    return pl.pallas_call(
        body, name=name, grid=(m // tm, n // tn, nk), in_specs=in_specs, out_specs=o_spec,
        out_shape=jax.ShapeDtypeStruct((m, n), out_dtype), scratch_shapes=[pltpu.VMEM((tm, tn), F32)],
        compiler_params=_CP(dimension_semantics=("parallel", "parallel", "arbitrary")),
    )(*args)


class In:
    def __init__(self, arr, block, imap, kind="x", per_h=False, gdtype=F32):
        self.arr, self.block, self.imap, self.kind, self.per_h, self.gdtype = arr, tuple(block), imap, kind, per_h, gdtype

    def spec(self, rev_g=None):
        imap = self.imap
        if rev_g is None:
            return pl.BlockSpec(self.block, lambda h, g: imap(h, g))
        return pl.BlockSpec(self.block, lambda h, g: imap(h, rev_g - 1 - g))


class Out:
    def __init__(self, shape, dtype, block, imap):
        self.shape, self.dtype, self.block, self.imap = tuple(shape), dtype, tuple(block), imap

    def spec(self, rev_g=None):
        imap = self.imap
        if rev_g is None:
            return pl.BlockSpec(self.block, lambda h, g: imap(h, g))
        return pl.BlockSpec(self.block, lambda h, g: imap(h, rev_g - 1 - g))


def _load_f32(ref):
    v = ref[...]
    return v.astype(F32) if jnp.issubdtype(v.dtype, jnp.floating) else v


def _state_out(grid, shape):
    nd = len(shape)
    return Out(tuple(grid) + tuple(shape), F32, (None, None) + tuple(shape), lambda h, g: (h, g) + (0,) * nd)


def _op_fwd(name, f, grid, ins, outs, state_shapes=()):
    n_in, n_out, n_st = len(ins), len(outs), len(state_shapes)
    st_outs = [_state_out(grid, s) for s in state_shapes]

    def body(*refs):
        in_refs = refs[:n_in]
        out_refs = refs[n_in:n_in + n_out]
        sv_refs = refs[n_in + n_out:n_in + n_out + n_st]
        st_scr = refs[n_in + n_out + n_st:]
        g = pl.program_id(1)
        if n_st:
            @pl.when(g == 0)
            def _():
                for s in st_scr:
                    s[...] = jnp.zeros(s.shape, F32)
        vals = [_load_f32(r) for r in in_refs]
        sts = [s[...] for s in st_scr]
        o, ns = f(g, vals, sts)
        for r, v in zip(out_refs, o):
            r[...] = v.astype(r.dtype)
        for r, s in zip(sv_refs, sts):
            r[...] = s
        for s, v in zip(st_scr, ns):
            s[...] = v

    all_outs = list(outs) + st_outs
    res = pl.pallas_call(
        body, name=name, grid=tuple(grid), in_specs=[i.spec() for i in ins], out_specs=[o.spec() for o in all_outs],
        out_shape=[jax.ShapeDtypeStruct(o.shape, o.dtype) for o in all_outs],
        scratch_shapes=[pltpu.VMEM(tuple(s), F32) for s in state_shapes],
        compiler_params=_CP(dimension_semantics=("arbitrary", "arbitrary")),
    )(*[i.arr for i in ins])
    return list(res[:n_out]), list(res[n_out:])


def _op_bwd(name, f, grid, ins, outs, state_shapes, saved, douts, addto=None):
    n_in, n_out, n_st = len(ins), len(outs), len(state_shapes)
    n_g = grid[1]
    addto = addto or {}
    diff = [k for k, i in enumerate(ins) if i.kind in ("x", "p")]
    add_idx = sorted(addto)
    st_ins = [In(s, o.block, o.imap, "c") for s, o in zip(saved, [_state_out(grid, s) for s in state_shapes])]
    dout_ins = [In(d, o.block, o.imap, "c") for d, o in zip(douts, outs)]
    add_ins = [In(addto[k], ins[k].block, ins[k].imap, "c") for k in add_idx]
    g_outs = []
    for k in diff:
        i = ins[k]
        if i.kind == "x":
            g_outs.append(Out(i.arr.shape, i.gdtype, i.block, i.imap))
        else:
            g_outs.append(Out(i.arr.shape, F32, i.block, i.imap))

    def body(*refs):
        p = 0
        in_refs = refs[p:p + n_in]; p += n_in
        sv_refs = refs[p:p + n_st]; p += n_st
        do_refs = refs[p:p + n_out]; p += n_out
        ad_refs = refs[p:p + len(add_idx)]; p += len(add_idx)
        go_refs = refs[p:p + len(diff)]; p += len(diff)
        ds_scr = refs[p:]
        hh = pl.program_id(0)
        step = pl.program_id(1)
        g = n_g - 1 - step
        if n_st:
            @pl.when(step == 0)
            def _():
                for s in ds_scr:
                    s[...] = jnp.zeros(s.shape, F32)
        vals = [_load_f32(r) for r in in_refs]
        sts = [r[...] for r in sv_refs]

        def fw(dvals, states):
            full = list(vals)
            for k, v in zip(diff, dvals):
                full[k] = v
            o, ns = f(g, full, states)
            return list(o), list(ns)

        _, vjp = jax.vjp(fw, [vals[k] for k in diff], sts)
        cts = [r[...].astype(F32) for r in do_refs]
        dns = [s[...] for s in ds_scr]
        dvals, dsts = vjp((cts, dns))
        adds = dict(zip(add_idx, ad_refs))
        for k, r, dv in zip(diff, go_refs, dvals):
            i = ins[k]
            if i.kind == "x":
                if k in adds:
                    dv = dv + adds[k][...].astype(F32)
                r[...] = dv.astype(r.dtype)
            else:
                first = (step == 0) if i.per_h else jnp.logical_and(step == 0, hh == 0)

                @pl.when(first)
                def _(r=r, dv=dv):
                    r[...] = dv

                @pl.when(jnp.logical_not(first))
                def _(r=r, dv=dv):
                    r[...] += dv
        for s, v in zip(ds_scr, dsts):
            s[...] = v

    all_ins = list(ins) + st_ins + dout_ins + add_ins
    res = pl.pallas_call(
        body, name=name, grid=tuple(grid), in_specs=[i.spec(n_g) for i in all_ins], out_specs=[o.spec(n_g) for o in g_outs],
        out_shape=[jax.ShapeDtypeStruct(o.shape, o.dtype) for o in g_outs],
        scratch_shapes=[pltpu.VMEM(tuple(s), F32) for s in state_shapes],
        compiler_params=_CP(dimension_semantics=("arbitrary", "arbitrary")),
    )(*[i.arr for i in all_ins])
    return list(res)


class Op:
    def __init__(self, name, f, grid, ins, outs, state_shapes=()):
        self.name, self.f, self.grid, self.ins, self.outs, self.state_shapes = name, f, grid, ins, outs, state_shapes
        self.saved = None

    def fwd(self):
        res, self.saved = _op_fwd(self.name + "_fwd", self.f, self.grid, self.ins, self.outs, self.state_shapes)
        return res

    def bwd(self, douts, addto=None):
        return _op_bwd(self.name + "_bwd", self.f, self.grid, self.ins, self.outs, self.state_shapes, self.saved, douts, addto)


def _rows(arr, t, kind="x", gdtype=F32):
    return In(arr, (t, arr.shape[1]), lambda h, g: (g, 0), kind, gdtype=gdtype)


def _whole(arr, kind="p"):
    nd = arr.ndim
    return In(arr, arr.shape, lambda h, g: (0,) * nd, kind)


def _rows_out(s, n, t, dtype):
    return Out((s, n), dtype, (t, n), lambda h, g: (g, 0))


ROW_T = 256


def _rms_op(name, x, gain, out_dtype=BF16, gdtype=F32):
    s, n = x.shape

    def f(g, vals, sts):
        return [_rms(vals[0], vals[1])], []

    return Op(name, f, (1, s // ROW_T), [_rows(x, ROW_T, gdtype=gdtype), _whole(gain.reshape(1, n))], [_rows_out(s, n, ROW_T, out_dtype)])


def _mla_prep_op(qn, q1, q2, kn, kr, v, cos, sin):
    s = qn.shape[0]
    hd, half = MLA_HEADS, MLA_ROPE // 2

    def f(g, vals, sts):
        qn, q1, q2, kn, kr, v, cos, sin = vals
        cos_h, sin_h = jnp.tile(cos, (1, hd)), jnp.tile(sin, (1, hd))
        r1 = q1 * cos_h - q2 * sin_h
        r2 = q2 * cos_h + q1 * sin_h
        k1, k2 = kr[:, 0:half], kr[:, half:2 * half]
        kr1 = k1 * cos - k2 * sin
        kr2 = k2 * cos + k1 * sin
        zpad = jnp.zeros((qn.shape[0], LANE - MLA_QK), F32)
        qs, ks, vs = [], [], []
        for h in range(hd):
            a, b = h * MLA_NOPE, (h + 1) * MLA_NOPE
            c, d = h * half, (h + 1) * half
            qs.append(jnp.concatenate([qn[:, a:b], r1[:, c:d], r2[:, c:d], zpad], axis=1))
            ks.append(jnp.concatenate([kn[:, a:b], kr1, kr2, zpad], axis=1))
            vs.append(v[:, a:b])
        return [jnp.stack(qs, 0), jnp.stack(ks, 0), jnp.stack(vs, 0)], []

    ins = [_rows(qn, ROW_T, gdtype=BF16), _rows(q1, ROW_T, gdtype=BF16), _rows(q2, ROW_T, gdtype=BF16), _rows(kn, ROW_T, gdtype=BF16),
           _rows(kr, ROW_T, gdtype=BF16), _rows(v, ROW_T, gdtype=BF16), _rows(cos, ROW_T, "c"), _rows(sin, ROW_T, "c")]
    outs = [Out((hd, s, LANE), BF16, (hd, ROW_T, LANE), lambda h, g: (0, g, 0)),
            Out((hd, s, LANE), BF16, (hd, ROW_T, LANE), lambda h, g: (0, g, 0)),
            Out((hd, s, MLA_V), BF16, (hd, ROW_T, MLA_V), lambda h, g: (0, g, 0))]
    return Op("mla_prep", f, (1, s // ROW_T), ins, outs)


ATT_TQ = 256


def _mla_attn_op(q, k, v):
    hd, s, _ = q.shape
    scale = MLA_QK ** -0.5

    def f(g, vals, sts):
        q, k, v = vals
        sc = _bdot_nt(q, k) * scale
        r = lax.broadcasted_iota(jnp.int32, sc.shape, 0) + g * ATT_TQ
        c = lax.broadcasted_iota(jnp.int32, sc.shape, 1)
        sc = jnp.where(r >= c, sc, -1e30)
        m = lax.stop_gradient(jnp.max(sc, axis=-1, keepdims=True))
        p = jnp.exp(sc - m)
        p = p / jnp.sum(p, axis=-1, keepdims=True)
        return [_bdot(p, v)], []

    ins = [In(q, (None, ATT_TQ, LANE), lambda h, g: (h, g, 0), "x"),
           In(k, (None, s, LANE), lambda h, g: (h, 0, 0), "p", per_h=True),
           In(v, (None, s, MLA_V), lambda h, g: (h, 0, 0), "p", per_h=True)]
    outs = [Out((hd, s, MLA_V), F32, (None, ATT_TQ, MLA_V), lambda h, g: (h, g, 0))]
    return Op("mla_attn", f, (hd, s // ATT_TQ), ins, outs)


def _mla_post_op(o, gate):
    hd, s, _ = o.shape

    def f(g, vals, sts):
        o, gate = vals
        cat = jnp.concatenate([o[h] for h in range(hd)], axis=1)
        return [cat * _silu(gate)], []

    ins = [In(o, (hd, ROW_T, MLA_V), lambda h, g: (0, g, 0), "x"), _rows(gate, ROW_T, gdtype=BF16)]
    return Op("mla_post", f, (1, s // ROW_T), ins, [_rows_out(s, hd * MLA_V, ROW_T, BF16)])


def _gla_gate_op(gk, w2, b):
    s = gk.shape[0]

    def f(g, vals, sts):
        gk, w2, b = vals
        return [jax.nn.log_sigmoid(_bdot(gk, w2) + b) / GLA_TAU], []

    ins = [_rows(gk, ROW_T, gdtype=BF16), _whole(w2), _whole(b)]
    return Op("gla_gate", f, (1, s // ROW_T), ins, [_rows_out(s, GLA_HEADS * GLA_DK, ROW_T, F32)])


def _gla_core_op(q, k, v, gate, la, g_o):
    s = q.shape[0]
    c = GLA_CHUNK

    def f(g, vals, sts):
        q, k, v, gate, la, g_o = vals
        st = sts[0]
        tri = _tri(c)
        b = _hdot(tri.astype(F32), la)
        b_last = jnp.sum(la, axis=0, keepdims=True)
        qt = q * (GLA_DK ** -0.5) * jnp.exp(b)
        kt = k * jnp.exp(-b)
        att = jnp.where(tri, _bdot_nt(qt, kt), 0.0)
        o = _bdot(att, v) + _bdot(qt, st)
        d_state = _bdot_tn(k * jnp.exp(b_last - b), v)
        b_last_col = _hdot_tn(la, jnp.ones((c, 1), F32))
        new_st = jnp.exp(b_last_col) * st + d_state
        y = _rms(o, g_o) * _silu(gate)
        return [y], [new_st]

    hk = lambda h, g: (g, h)
    ins = [In(q, (c, GLA_DK), hk, "x", gdtype=BF16), In(k, (c, GLA_DK), hk, "x", gdtype=BF16), In(v, (c, GLA_DV), hk, "x", gdtype=BF16),
           In(gate, (c, GLA_DV), hk, "x", gdtype=BF16), In(la, (c, GLA_DK), hk, "x"), _whole(g_o)]
    outs = [Out((s, GLA_HEADS * GLA_DV), BF16, (c, GLA_DV), hk)]
    return Op("gla_core", f, (GLA_HEADS, s // c), ins, outs, [(GLA_DK, GLA_DV)])


LRU_T = 256


def _lru_op(gate, u, conv_w, conv_b, w_a, b_a, w_x, b_x, lam):
    s, w = u.shape
    t = LRU_T

    def f(g, vals, sts):
        gate, u, cw, cb, w_a, b_a, w_x, b_x, lam = vals
        u_prev, h_prev = sts
        uc = cb
        for kk in range(CONV_W):
            uc = uc + cw[kk] * _shift_rows(u, u_prev, CONV_W - 1 - kk)
        ra, ri = [], []
        for n in range(LRU_BLOCKS):
            blk = uc[:, n * LRU_BLOCK:(n + 1) * LRU_BLOCK]
            ra.append(_bdot(blk, w_a[n]))
            ri.append(_bdot(blk, w_x[n]))
        r = jax.nn.sigmoid(jnp.concatenate(ra, axis=1) + b_a)
        i = jax.nn.sigmoid(jnp.concatenate(ri, axis=1) + b_x)
        log_a = -LRU_C * r * jax.nn.softplus(-lam)
        a = jnp.exp(log_a)
        bb = jnp.sqrt(_one_minus_exp(2.0 * log_a)) * (i * uc)
        zero = jnp.zeros_like(a)
        sh = 1
        while sh < t:
            a_s = _shift_rows(a - 1.0, zero, sh) + 1.0
            b_s = _shift_rows(bb, zero, sh)
            bb = a * b_s + bb
            a = a * a_s
            sh *= 2
        hs = bb + a * h_prev
        last = (lax.broadcasted_iota(jnp.int32, hs.shape, 0) == t - 1).astype(F32)
        h_last = jnp.sum(hs * last, axis=0, keepdims=True)
        return [hs * _silu(gate)], [u, h_last]

    ins = [_rows(gate, t, gdtype=BF16), _rows(u, t, gdtype=BF16), _whole(conv_w), _whole(conv_b), _whole(w_a), _whole(b_a), _whole(w_x),
           _whole(b_x), _whole(lam)]
    return Op("lru_core", f, (1, s // t), ins, [_rows_out(s, w, t, BF16)], [(t, w), (1, w)])


def _ssd_conv_op(xbc, conv_w, conv_b):
    s, w = xbc.shape
    t = ROW_T
    n_x, n_b = SSD_INNER, SSD_GROUPS * SSD_STATE

    def f(g, vals, sts):
        xbc, cw, cb = vals
        acc = cb
        for kk in range(CONV_W):
            acc = acc + cw[kk] * _shift_rows(xbc, sts[0], CONV_W - 1 - kk)
        y = _silu(acc)
        return [y[:, :n_x], y[:, n_x:n_x + n_b], y[:, n_x + n_b:]], [xbc]

    ins = [_rows(xbc, t, gdtype=BF16), _whole(conv_w), _whole(conv_b)]
    outs = [_rows_out(s, n_x, t, F32), _rows_out(s, n_b, t, F32), _rows_out(s, n_b, t, F32)]
    return Op("ssd_conv", f, (1, s // t), ins, outs, [(t, w)])


SSD_L = 256


def _ssd_core_op(x, bm, cm, z, dt, dt_bias, a_log, d_skip, g_norm):
    s = x.shape[0]
    c, hg, p = SSD_L, SSD_HPG, SSD_P
    gw = hg * p

    def f(g, vals, sts):
        x, bm, cm, z, dtr, dt_bias, a_log, d_skip, g_norm = vals
        st = sts[0]
        tri = _tri(c)
        dt = jax.nn.softplus(dtr + dt_bias)
        da = dt * (-jnp.exp(a_log))
        cs = _hdot(tri.astype(F32), da)
        cs_last = jnp.sum(da, axis=0, keepdims=True)
        cb = _bdot_nt(cm, bm)
        lane = lax.broadcasted_iota(jnp.int32, (c, hg), 1)
        ys, new_st = [], []
        for h in range(hg):
            cs_h = cs[:, h:h + 1]
            cs_row = _hdot_nt((lane == h).astype(F32), cs)
            seg = jnp.where(tri, cs_h - cs_row, 0.0)
            lmat = jnp.where(tri, jnp.exp(seg), 0.0)
            x_h = x[:, h * p:(h + 1) * p]
            xdt = x_h * dt[:, h:h + 1]
            y_diag = _bdot(cb * lmat, xdt)
            decay = jnp.exp(cs_last[:, h:h + 1] - cs_h)
            states = _bdot_tn(xdt * decay, bm)
            y_off = _bdot_nt(cm, st[h]) * jnp.exp(cs_h)
            new_st.append(jnp.exp(cs_last[:, h:h + 1]) * st[h] + states)
            ys.append(y_diag + y_off + d_skip[:, h:h + 1] * x_h)
        y = jnp.concatenate(ys, axis=1) * _silu(z)
        return [_rms(y, g_norm)], [jnp.stack(new_st, 0)]

    ins = [In(x, (c, gw), lambda h, g: (g, h), "x"), In(bm, (c, SSD_STATE), lambda h, g: (g, h), "x"),
           In(cm, (c, SSD_STATE), lambda h, g: (g, h), "x"), In(z, (c, gw), lambda h, g: (g, h), "x", gdtype=BF16),
           In(dt, (None, c, hg), lambda h, g: (h, g, 0), "x"),
           In(dt_bias, (None, 1, hg), lambda h, g: (h, 0, 0), "p", per_h=True),
           In(a_log, (None, 1, hg), lambda h, g: (h, 0, 0), "p", per_h=True),
           In(d_skip, (None, 1, hg), lambda h, g: (h, 0, 0), "p", per_h=True),
           In(g_norm, (1, gw), lambda h, g: (0, h), "p", per_h=True)]
    outs = [Out((s, SSD_INNER), BF16, (c, gw), lambda h, g: (g, h))]
    return Op("ssd_core", f, (SSD_GROUPS, s // c), ins, outs, [(hg, p, SSD_STATE)])


def _loss_op(h, target, final_g):
    s, n = h.shape
    t = ROW_T
    n_g = s // t

    def body(h_ref, t_ref, g_ref, loss_ref, dh_ref, dg_ref):
        step = pl.program_id(0)

        def lossf(hv, gv):
            err = _rms(hv, gv) - t_ref[...]
            return 0.5 * jnp.sum(jnp.mean(err * err, axis=-1))

        l, (dh, dg) = jax.value_and_grad(lossf, argnums=(0, 1))(h_ref[...], g_ref[...])
        dh_ref[...] = dh

        @pl.when(step == 0)
        def _():
            loss_ref[...] = jnp.zeros(loss_ref.shape, F32)
            dg_ref[...] = jnp.zeros(dg_ref.shape, F32)

        loss_ref[...] += jnp.full(loss_ref.shape, l, F32)
        dg_ref[...] += dg

    row = pl.BlockSpec((t, n), lambda g: (g, 0))
    one = pl.BlockSpec((1, n), lambda g: (0, 0))
    return pl.pallas_call(
        body, name="loss_head", grid=(n_g,), in_specs=[row, row, one],
        out_specs=[pl.BlockSpec((1, LANE), lambda g: (0, 0)), row, one],
        out_shape=[jax.ShapeDtypeStruct((1, LANE), F32), jax.ShapeDtypeStruct((s, n), F32), jax.ShapeDtypeStruct((1, n), F32)],
        compiler_params=_CP(dimension_semantics=("arbitrary",)),
    )(h, target, final_g.reshape(1, n))


def _pad_cols(w, n):
    return jnp.pad(w, ((0, 0), (0, n - w.shape[1])))


def _pad_rows(w, n):
    return jnp.pad(w, ((0, n - w.shape[0]), (0, 0)))


def _proj_bwd(tag, u, dps, ws):
    du = None
    for i, (dp, w) in enumerate(zip(dps, ws)):
        du = _mm(f"{tag}_du{i}", dp, w, tb=True, add=du)
    dws = [_mm(f"{tag}_dw{i}", u, dp, ta=True) for i, dp in enumerate(dps)]
    return du, dws


def _mla_layer(h, norm_g, w, cos, sin):
    bf = lambda a: a.astype(BF16)
    w_in, w_uq, w_ukv = w["mla_w_in"], w["mla_w_uq"], w["mla_w_ukv"]
    a0, a1, a2 = MLA_Q_RANK, MLA_Q_RANK + MLA_KV_RANK, MLA_Q_RANK + MLA_KV_RANK + MLA_ROPE
    w_cq, w_ckv, w_kr, w_g = bf(w_in[:, :a0]), bf(w_in[:, a0:a1]), bf(_pad_cols(w_in[:, a1:a2], LANE)), bf(w_in[:, a2:])
    uq = w_uq.reshape(MLA_Q_RANK, MLA_HEADS, MLA_QK)
    half = MLA_ROPE // 2
    w_qn = bf(uq[:, :, :MLA_NOPE].reshape(MLA_Q_RANK, -1))
    w_q1 = bf(uq[:, :, MLA_NOPE:MLA_NOPE + half].reshape(MLA_Q_RANK, -1))
    w_q2 = bf(uq[:, :, MLA_NOPE + half:].reshape(MLA_Q_RANK, -1))
    ukv = w_ukv.reshape(MLA_KV_RANK, MLA_HEADS, MLA_NOPE + MLA_V)
    w_kn = bf(ukv[:, :, :MLA_NOPE].reshape(MLA_KV_RANK, -1))
    w_v = bf(ukv[:, :, MLA_NOPE:].reshape(MLA_KV_RANK, -1))
    w_out = bf(w["mla_w_out"])

    n0 = _rms_op("mla_norm", h, norm_g)
    u, = n0.fwd()
    cq, ckv, kr, gate = (_mm(f"mla_in{i}", u, wi) for i, wi in enumerate((w_cq, w_ckv, w_kr, w_g)))
    nq = _rms_op("mla_qnorm", cq, w["mla_g_q"], gdtype=BF16)
    nkv = _rms_op("mla_kvnorm", ckv, w["mla_g_kv"], gdtype=BF16)
    qn_, = nq.fwd()
    kvn_, = nkv.fwd()
    qn, q1, q2 = (_mm(f"mla_uq{i}", qn_, wi) for i, wi in enumerate((w_qn, w_q1, w_q2)))
    kn, v = (_mm(f"mla_ukv{i}", kvn_, wi) for i, wi in enumerate((w_kn, w_v)))
    prep = _mla_prep_op(qn, q1, q2, kn, kr, v, cos, sin)
    qh, kh, vh = prep.fwd()
    attn = _mla_attn_op(qh, kh, vh)
    o, = attn.fwd()
    post = _mla_post_op(o, gate)
    y, = post.fwd()
    h_out = _mm("mla_out", y, w_out, add=h)

    def bwd(dh):
        dy = _mm("mla_out_dy", dh, w_out, tb=True, out_dtype=BF16)
        d_w_out = _mm("mla_out_dw", y, dh, ta=True)
        do, dgate = post.bwd([dy])
        dqh, dkh, dvh = attn.bwd([do])
        dqn, dq1, dq2, dkn, dkr, dv = prep.bwd([dqh, dkh, dvh])
        dqn_, d_uq = _proj_bwd("mla_uq", qn_, (dqn, dq1, dq2), (w_qn, w_q1, w_q2))
        dkvn_, d_ukv = _proj_bwd("mla_ukv", kvn_, (dkn, dv), (w_kn, w_v))
        dcq, d_g_q = nq.bwd([dqn_])
        dckv, d_g_kv = nkv.bwd([dkvn_])
        du, d_in = _proj_bwd("mla_in", u, (dcq, dckv, dkr, dgate), (w_cq, w_ckv, w_kr, w_g))
        dh_in, d_norm = n0.bwd([du], addto={0: dh})
        shp = (MLA_Q_RANK, MLA_HEADS, -1)
        g_uq = jnp.concatenate([d_uq[0].reshape(shp), d_uq[1].reshape(shp), d_uq[2].reshape(shp)], axis=2).reshape(MLA_Q_RANK, -1)
        shp = (MLA_KV_RANK, MLA_HEADS, -1)
        g_ukv = jnp.concatenate([d_ukv[0].reshape(shp), d_ukv[1].reshape(shp)], axis=2).reshape(MLA_KV_RANK, -1)
        g_in = jnp.concatenate([d_in[0], d_in[1], d_in[2][:, :MLA_ROPE], d_in[3]], axis=1)
        return dh_in, d_norm, {"mla_w_in": g_in, "mla_g_q": d_g_q.reshape(-1), "mla_w_uq": g_uq, "mla_g_kv": d_g_kv.reshape(-1),
                               "mla_w_ukv": g_ukv, "mla_w_out": d_w_out}

    return h_out, bwd


def _gla_layer(h, norm_g, w):
    bf = lambda a: a.astype(BF16)
    w_in = w["gla_w_in"]
    nk, nv = GLA_HEADS * GLA_DK, GLA_HEADS * GLA_DV
    cuts = (0, nk, 2 * nk, 2 * nk + nv, 2 * nk + 2 * nv)
    w_q, w_k, w_v, w_g = (bf(w_in[:, cuts[i]:cuts[i + 1]]) for i in range(4))
    w_gk = bf(_pad_cols(w_in[:, cuts[4]:], LANE))
    w2 = _pad_rows(w["gla_w_gk2"], LANE)
    b_gk = w["gla_b_gk"].reshape(1, -1)
    g_o = w["gla_g_o"].reshape(1, -1)
    w_out = bf(w["gla_w_out"])

    n0 = _rms_op("gla_norm", h, norm_g)
    u, = n0.fwd()
    q, k, v, gate, gk = (_mm(f"gla_in{i}", u, wi) for i, wi in enumerate((w_q, w_k, w_v, w_g, w_gk)))
    gop = _gla_gate_op(gk, w2, b_gk)
    la, = gop.fwd()
    core = _gla_core_op(q, k, v, gate, la, g_o)
    y, = core.fwd()
    h_out = _mm("gla_out", y, w_out, add=h)

    def bwd(dh):
        dy = _mm("gla_out_dy", dh, w_out, tb=True, out_dtype=BF16)
        d_w_out = _mm("gla_out_dw", y, dh, ta=True)
        dq, dk, dv, dgate, dla, d_g_o = core.bwd([dy])
        dgk, d_w2, d_b = gop.bwd([dla])
        du, d_in = _proj_bwd("gla_in", u, (dq, dk, dv, dgate, dgk), (w_q, w_k, w_v, w_g, w_gk))
        dh_in, d_norm = n0.bwd([du], addto={0: dh})
        g_in = jnp.concatenate([d_in[0], d_in[1], d_in[2], d_in[3], d_in[4][:, :GLA_RANK]], axis=1)
        return dh_in, d_norm, {"gla_w_in": g_in, "gla_w_gk2": d_w2[:GLA_RANK], "gla_b_gk": d_b.reshape(-1), "gla_g_o": d_g_o.reshape(-1),
                               "gla_w_out": d_w_out}

    return h_out, bwd


def _lru_layer(h, norm_g, w):
    bf = lambda a: a.astype(BF16)
    w_in = w["lru_w_in"]
    w_g, w_u = bf(w_in[:, :LRU_WIDTH]), bf(w_in[:, LRU_WIDTH:])
    row = lambda a: a.reshape(1, -1)
    w_out = bf(w["lru_w_out"])

    n0 = _rms_op("lru_norm", h, norm_g)
    u_, = n0.fwd()
    gate, u = (_mm(f"lru_in{i}", u_, wi) for i, wi in enumerate((w_g, w_u)))
    core = _lru_op(gate, u, w["lru_conv_w"].reshape(CONV_W, 1, -1), row(w["lru_conv_b"]), w["lru_w_a"], row(w["lru_b_a"]), w["lru_w_x"],
                   row(w["lru_b_x"]), row(w["lru_lam"]))
    y, = core.fwd()
    h_out = _mm("lru_out", y, w_out, add=h)

    def bwd(dh):
        dy = _mm("lru_out_dy", dh, w_out, tb=True, out_dtype=BF16)
        d_w_out = _mm("lru_out_dw", y, dh, ta=True)
        dgate, du, d_cw, d_cb, d_wa, d_ba, d_wx, d_bx, d_lam = core.bwd([dy])
        du_, d_in = _proj_bwd("lru_in", u_, (dgate, du), (w_g, w_u))
        dh_in, d_norm = n0.bwd([du_], addto={0: dh})
        return dh_in, d_norm, {"lru_w_in": jnp.concatenate(d_in, axis=1), "lru_conv_w": d_cw.reshape(CONV_W, -1), "lru_conv_b": d_cb.reshape(-1),
                               "lru_w_a": d_wa, "lru_b_a": d_ba.reshape(-1), "lru_w_x": d_wx, "lru_b_x": d_bx.reshape(-1),
                               "lru_lam": d_lam.reshape(-1), "lru_w_out": d_w_out}

    return h_out, bwd


def _ssd_layer(h, norm_g, w):
    bf = lambda a: a.astype(BF16)
    s = h.shape[0]
    w_in = w["ssd_w_in"]
    conv_dim = SSD_INNER + 2 * SSD_GROUPS * SSD_STATE
    w_z, w_xbc = bf(w_in[:, :SSD_INNER]), bf(w_in[:, SSD_INNER:SSD_INNER + conv_dim])
    w_dt = bf(_pad_cols(w_in[:, SSD_INNER + conv_dim:], LANE))
    grp = lambda a: a.reshape(SSD_GROUPS, 1, SSD_HPG)
    w_out = bf(w["ssd_w_out"])

    n0 = _rms_op("ssd_norm", h, norm_g)
    u, = n0.fwd()
    z, xbc, dtp = (_mm(f"ssd_in{i}", u, wi) for i, wi in enumerate((w_z, w_xbc, w_dt)))
    conv = _ssd_conv_op(xbc, w["ssd_conv_w"].reshape(CONV_W, 1, -1), w["ssd_conv_b"].reshape(1, -1))
    x, bm, cm = conv.fwd()
    dt = dtp[:, :SSD_HEADS].reshape(s, SSD_GROUPS, SSD_HPG).transpose(1, 0, 2)
    core = _ssd_core_op(x, bm, cm, z, dt, grp(w["ssd_dt_bias"]), grp(w["ssd_a_log"]), grp(w["ssd_d"]), w["ssd_g_norm"].reshape(1, -1))
    y, = core.fwd()
    h_out = _mm("ssd_out", y, w_out, add=h)

    def bwd(dh):
        dy = _mm("ssd_out_dy", dh, w_out, tb=True, out_dtype=BF16)
        d_w_out = _mm("ssd_out_dw", y, dh, ta=True)
        dx, dbm, dcm, dz, ddt, d_dtb, d_alog, d_d, d_gn = core.bwd([dy])
        dxbc, d_cw, d_cb = conv.bwd([dx, dbm, dcm])
        ddtp = _pad_cols(ddt.transpose(1, 0, 2).reshape(s, SSD_HEADS), LANE).astype(BF16)
        du, d_in = _proj_bwd("ssd_in", u, (dz, dxbc, ddtp), (w_z, w_xbc, w_dt))
        dh_in, d_norm = n0.bwd([du], addto={0: dh})
        g_in = jnp.concatenate([d_in[0], d_in[1], d_in[2][:, :SSD_HEADS]], axis=1)
        return dh_in, d_norm, {"ssd_w_in": g_in, "ssd_conv_w": d_cw.reshape(CONV_W, -1), "ssd_conv_b": d_cb.reshape(-1),
                               "ssd_dt_bias": d_dtb.reshape(-1), "ssd_a_log": d_alog.reshape(-1), "ssd_d": d_d.reshape(-1),
                               "ssd_g_norm": d_gn.reshape(-1), "ssd_w_out": d_w_out}

    return h_out, bwd


def _rope_tables(positions):
    inv_freq = ROPE_THETA ** (-jnp.arange(0, MLA_ROPE, 2, dtype=F32) / MLA_ROPE)
    ang = positions.astype(F32)[:, None] * inv_freq
    return jnp.cos(ang), jnp.sin(ang)


def _local_step(x, positions, target, w):
    cos, sin = _rope_tables(positions)
    ng = w["norm_g"]
    h1, b0 = _mla_layer(x, ng[0], w, cos, sin)
    h2, b1 = _gla_layer(h1, ng[1], w)
    h3, b2 = _lru_layer(h2, ng[2], w)
    h4, b3 = _ssd_layer(h3, ng[3], w)
    loss, dh, d_final = _loss_op(h4, target, w["final_g"])
    grads = {"final_g": d_final.reshape(-1)}
    d_norms = [None] * 4
    for i, b in ((3, b3), (2, b2), (1, b1), (0, b0)):
        dh, dn, gw = b(dh)
        d_norms[i] = dn
        grads.update(gw)
    grads["norm_g"] = jnp.concatenate(d_norms, axis=0)
    return loss[0, 0], dh, grads


WEIGHTS = ["norm_g", "final_g", "mla_w_in", "mla_g_q", "mla_w_uq", "mla_g_kv", "mla_w_ukv", "mla_w_out", "gla_w_in", "gla_w_gk2", "gla_b_gk",
           "gla_g_o", "gla_w_out", "lru_w_in", "lru_conv_w", "lru_conv_b", "lru_w_a", "lru_b_a", "lru_w_x", "lru_b_x", "lru_lam", "lru_w_out",
           "ssd_w_in", "ssd_conv_w", "ssd_conv_b", "ssd_dt_bias", "ssd_a_log", "ssd_d", "ssd_g_norm", "ssd_w_out"]
BIG = ["mla_w_in", "mla_w_uq", "mla_w_ukv", "mla_w_out", "gla_w_in", "gla_w_out", "lru_w_in", "lru_w_out", "ssd_w_in", "ssd_w_out"]
SMALL = ["gla_w_gk2", "gla_b_gk", "gla_g_o", "lru_conv_w", "lru_conv_b", "lru_b_a", "lru_b_x", "lru_lam", "ssd_conv_w", "ssd_conv_b", "ssd_g_norm"]
REPL = ["norm_g", "final_g", "mla_g_q", "mla_g_kv", "lru_w_a", "lru_w_x", "ssd_dt_bias", "ssd_a_log", "ssd_d"]
N_CHIPS, N_DEV = 4, 8
PACK_W = 1024
ADAM_ROWS = 256
SMALL_ROWS = 64


def _shard_axis(name):
    return 0 if name.endswith("_w_out") else -1


def _pack(arrs, dtype, row_mult):
    flat = jnp.concatenate([a.reshape(-1).astype(dtype) for a in arrs])
    per = PACK_W * row_mult
    total = -(-flat.shape[0] // per) * per
    return jnp.pad(flat, (0, total - flat.shape[0])).reshape(-1, PACK_W)


def _unpack(buf, shapes):
    flat = buf.reshape(-1)
    out, off = [], 0
    for s in shapes:
        n = math.prod(s)
        out.append(flat[off:off + n].reshape(s))
        off += n
    return out


def _mesh_pos():
    return lax.axis_index("x"), lax.axis_index("y"), lax.axis_index("c")


def _gather_weights(ops):
    n_op = len(ops)

    def body(*refs):
        srcs, dsts = refs[:n_op], refs[n_op:2 * n_op]
        send_sems, recv_sems, local_sems = refs[2 * n_op:]
        x, y, c = _mesh_pos()
        chips = [(1 - x, y), (x, 1 - y), (1 - x, 1 - y)]
        mine = 2 * x + y
        local = [pltpu.make_async_copy(srcs[i], dsts[i].at[mine], local_sems.at[i]) for i in range(n_op)]
        for cp in local:
            cp.start()

        def copy(i, k, slot, to):
            return pltpu.make_async_remote_copy(src_ref=srcs[i], dst_ref=dsts[i].at[slot], send_sem=send_sems.at[i * 3 + k],
                                                recv_sem=recv_sems.at[i * 3 + k], device_id=to, device_id_type=pl.DeviceIdType.MESH)

        sends = [copy(i, k, mine, (*chip, c)) for i in range(n_op) for k, chip in enumerate(chips)]
        for cp in sends:
            cp.start()
        for i in range(n_op):
            for k, (px, py) in enumerate(chips):
                copy(i, k, 2 * px + py, (x, y, c)).wait_recv()
        for cp in sends:
            cp.wait_send()
        for cp in local:
            cp.wait()

    any_spec = pl.BlockSpec(memory_space=pl.ANY)
    return pl.pallas_call(
        body, name="gather_weights", in_specs=[any_spec] * n_op, out_specs=[any_spec] * n_op,
        out_shape=[jax.ShapeDtypeStruct((N_CHIPS,) + o.shape, o.dtype) for o in ops],
        scratch_shapes=[pltpu.SemaphoreType.DMA((3 * n_op,)), pltpu.SemaphoreType.DMA((3 * n_op,)), pltpu.SemaphoreType.DMA((n_op,))],
    )(*ops)


def _exchange_grads(chip_ops, all_ops):
    ops = tuple(chip_ops) + tuple(all_ops)
    per_chip = (True,) * len(chip_ops) + (False,) * len(all_ops)
    n_op = len(ops)

    def body(*refs):
        srcs, dsts = refs[:n_op], refs[n_op:2 * n_op]
        send_sems, recv_sems, local_sems = refs[2 * n_op:]
        x, y, c = _mesh_pos()
        sibling = (x, y, 1 - c)
        chips = [(1 - x, y), (x, 1 - y), (1 - x, 1 - y)]

        def dev(px, py, pc):
            return 4 * px + 2 * py + pc

        def part(i, px, py):
            return srcs[i].at[2 * px + py] if per_chip[i] else srcs[i]

        def copy(i, k, src, slot, to):
            return pltpu.make_async_remote_copy(src_ref=src, dst_ref=dsts[i].at[slot], send_sem=send_sems.at[i * 7 + k],
                                                recv_sem=recv_sems.at[i * 7 + k], device_id=to, device_id_type=pl.DeviceIdType.MESH)

        me = dev(x, y, c)
        local = [pltpu.make_async_copy(part(i, x, y), dsts[i].at[me], local_sems.at[i]) for i in range(n_op)]
        for cp in local:
            cp.start()
        first = []
        for i in range(n_op):
            first.append(copy(i, 0, part(i, x, y), me, sibling))
            first += [copy(i, 1 + k, part(i, px, py), me, (px, py, c)) for k, (px, py) in enumerate(chips)]
        for cp in first:
            cp.start()
        passed = []
        for i in range(n_op):
            for k, (px, py) in enumerate(chips):
                slot = dev(px, py, c)
                copy(i, 1 + k, srcs[i].at[0] if per_chip[i] else srcs[i], slot, (x, y, c)).wait_recv()
                fwd = copy(i, 4 + k, dsts[i].at[slot], slot, sibling)
                fwd.start()
                passed.append(fwd)
        for i in range(n_op):
            dummy = srcs[i].at[0] if per_chip[i] else srcs[i]
            copy(i, 0, dummy, dev(x, y, 1 - c), (x, y, c)).wait_recv()
            for k, (px, py) in enumerate(chips):
                copy(i, 4 + k, dummy, dev(px, py, 1 - c), (x, y, c)).wait_recv()
        for cp in first + passed:
            cp.wait_send()
        for cp in local:
            cp.wait()

    any_spec = pl.BlockSpec(memory_space=pl.ANY)
    return pl.pallas_call(
        body, name="exchange_grads", in_specs=[any_spec] * n_op, out_specs=[any_spec] * n_op,
        out_shape=[jax.ShapeDtypeStruct((N_DEV,) + o.shape[-2:], o.dtype) for o in ops],
        scratch_shapes=[pltpu.SemaphoreType.DMA((7 * n_op,)), pltpu.SemaphoreType.DMA((7 * n_op,)), pltpu.SemaphoreType.DMA((n_op,))],
    )(*ops)


def _adamw(name, parts, w, m, v):
    rows, cols = w.shape
    t = next(c for c in (ADAM_ROWS, ADAM_ROWS // 2, SMALL_ROWS) if rows % c == 0)
    c1 = 1.0 - ADAM_B1 ** ADAM_STEP
    c2 = 1.0 - ADAM_B2 ** ADAM_STEP

    def body(p_ref, w_ref, m_ref, v_ref, g_ref, d_ref, nm_ref, nv_ref):
        g = p_ref[0].astype(F32)
        for d in range(1, N_DEV):
            g = g + p_ref[d].astype(F32)
        nm = ADAM_B1 * m_ref[...] + (1.0 - ADAM_B1) * g
        nv = ADAM_B2 * v_ref[...] + (1.0 - ADAM_B2) * (g * g)
        g_ref[...] = g
        nm_ref[...] = nm
        nv_ref[...] = nv
        d_ref[...] = -ADAM_LR * ((nm / c1) / (jnp.sqrt(nv / c2) + ADAM_EPS) + ADAM_WD * w_ref[...])

    row = pl.BlockSpec((t, cols), lambda i: (i, 0))
    return pl.pallas_call(
        body, name=name, grid=(rows // t,), in_specs=[pl.BlockSpec((N_DEV, t, cols), lambda i: (0, i, 0)), row, row, row],
        out_specs=[row] * 4, out_shape=[jax.ShapeDtypeStruct((rows, cols), F32)] * 4,
        compiler_params=_CP(dimension_semantics=("parallel",)),
    )(parts, w, m, v)


def _train_step(x, positions, target, wts, ms, vs):
    small_shapes = [wts[n].shape for n in SMALL]
    repl_shapes = [wts[n].shape for n in REPL]

    gathered = _gather_weights([wts[n].astype(BF16) for n in BIG] + [_pack([wts[n] for n in SMALL], F32, SMALL_ROWS)])
    full = {n: wts[n] for n in REPL}
    for k, n in enumerate(BIG):
        full[n] = jnp.concatenate([gathered[k][j] for j in range(N_CHIPS)], axis=_shard_axis(n))
    per_chip_small = [_unpack(gathered[-1][j], small_shapes) for j in range(N_CHIPS)]
    for k, n in enumerate(SMALL):
        full[n] = jnp.concatenate([per_chip_small[j][k] for j in range(N_CHIPS)], axis=_shard_axis(n))

    loss, dx, grads = _local_step(x, positions, target, full)

    def shards(n, dtype):
        return jnp.stack([p.astype(dtype) for p in jnp.split(grads[n], N_CHIPS, axis=_shard_axis(n))])

    psmall = jnp.stack([_pack([jnp.split(grads[n], N_CHIPS, axis=_shard_axis(n))[j] for n in SMALL], F32, SMALL_ROWS) for j in range(N_CHIPS)])
    prepl = _pack([grads[n] for n in REPL], F32, SMALL_ROWS)
    parts = _exchange_grads([shards(n, BF16) for n in BIG] + [psmall], [prepl])

    out = {}
    kinds = ("grad", "delta", "new_m", "new_v")
    for k, n in enumerate(BIG):
        for kind, a in zip(kinds, _adamw("adam_" + n, parts[k], wts[n], ms[n], vs[n])):
            out[kind, n] = a
    for tag, names, p, shapes in (("adam_small", SMALL, parts[-2], small_shapes), ("adam_repl", REPL, parts[-1], repl_shapes)):
        packed = [_pack([d[n] for n in names], F32, SMALL_ROWS) for d in (wts, ms, vs)]
        for kind, buf in zip(kinds, _adamw(tag, p, *packed)):
            for n, a in zip(names, _unpack(buf, shapes)):
                out[kind, n] = a
    loss = lax.psum(loss, ("x", "y", "c"))
    return loss, dx, out


def kernel(x, positions, norm_g, final_g, mla_w_in, mla_g_q, mla_w_uq, mla_g_kv, mla_w_ukv, mla_w_out, gla_w_in, gla_w_gk2, gla_b_gk, gla_g_o, gla_w_out, lru_w_in, lru_conv_w, lru_conv_b, lru_w_a, lru_b_a, lru_w_x, lru_b_x, lru_lam, lru_w_out, ssd_w_in, ssd_conv_w, ssd_conv_b, ssd_dt_bias, ssd_a_log, ssd_d, ssd_g_norm, ssd_w_out, loss_target, m_norm_g, m_final_g, m_mla_w_in, m_mla_g_q, m_mla_w_uq, m_mla_g_kv, m_mla_w_ukv, m_mla_w_out, m_gla_w_in, m_gla_w_gk2, m_gla_b_gk, m_gla_g_o, m_gla_w_out, m_lru_w_in, m_lru_conv_w, m_lru_conv_b, m_lru_w_a, m_lru_b_a, m_lru_w_x, m_lru_b_x, m_lru_lam, m_lru_w_out, m_ssd_w_in, m_ssd_conv_w, m_ssd_conv_b, m_ssd_dt_bias, m_ssd_a_log, m_ssd_d, m_ssd_g_norm, m_ssd_w_out, v_norm_g, v_final_g, v_mla_w_in, v_mla_g_q, v_mla_w_uq, v_mla_g_kv, v_mla_w_ukv, v_mla_w_out, v_gla_w_in, v_gla_w_gk2, v_gla_b_gk, v_gla_g_o, v_gla_w_out, v_lru_w_in, v_lru_conv_w, v_lru_conv_b, v_lru_w_a, v_lru_b_a, v_lru_w_x, v_lru_b_x, v_lru_lam, v_lru_w_out, v_ssd_w_in, v_ssd_conv_w, v_ssd_conv_b, v_ssd_dt_bias, v_ssd_a_log, v_ssd_d, v_ssd_g_norm, v_ssd_w_out):
    given = dict(locals())
    stacked = [n for n in WEIGHTS if n not in ("norm_g", "final_g")]

    def blocks(prefix):
        return {n: (given[prefix + n][0] if n in stacked else given[prefix + n]) for n in WEIGHTS}

    loss, dx, out = _train_step(x[0], positions[0], loss_target[0], blocks(""), blocks("m_"), blocks("v_"))
    res = [loss, dx[None]]
    for kind in ("grad", "delta", "new_m", "new_v"):
        res += [(out[kind, n][None] if n in stacked else out[kind, n]) for n in WEIGHTS]
    return tuple(res)
```

```python
import functools
import math

import jax
import jax.numpy as jnp
from jax import lax
from jax.experimental import pallas as pl
from jax.experimental.pallas import tpu as pltpu

F32 = jnp.float32
BF16 = jnp.bfloat16

V7X_VMEM_BYTES = 64 * 1024 * 1024
VMEM_LIMIT = V7X_VMEM_BYTES - 8 * 1024 * 1024
LANE = 128

D_MODEL = 1024
NORM_EPS = 1e-6
MLA_HEADS, MLA_Q_RANK, MLA_KV_RANK, MLA_NOPE, MLA_ROPE, MLA_V = 16, 384, 256, 64, 32, 64
MLA_QK = MLA_NOPE + MLA_ROPE
ROPE_THETA = 10000.0
GLA_HEADS, GLA_DK, GLA_DV, GLA_RANK, GLA_TAU, GLA_CHUNK = 4, 128, 256, 16, 16.0, 64
LRU_WIDTH, LRU_BLOCKS, LRU_BLOCK, LRU_C, CONV_W = 1280, 10, 128, 8.0, 4
SSD_INNER, SSD_P, SSD_HEADS, SSD_GROUPS, SSD_HPG, SSD_STATE, SSD_CHUNK = 2048, 64, 32, 8, 4, 128, 64
ADAM_LR, ADAM_B1, ADAM_B2, ADAM_EPS, ADAM_WD, ADAM_STEP = 0.001, 0.9, 0.999, 1e-08, 0.01, 10

_CP = functools.partial(pltpu.CompilerParams, vmem_limit_bytes=VMEM_LIMIT)


def _bdot(a, b):
    return jnp.dot(a.astype(BF16), b.astype(BF16), preferred_element_type=F32)


def _bdot_nt(a, b):
    return lax.dot_general(a.astype(BF16), b.astype(BF16), (((1,), (1,)), ((), ())), preferred_element_type=F32)


def _bdot_tn(a, b):
    return lax.dot_general(a.astype(BF16), b.astype(BF16), (((0,), (0,)), ((), ())), preferred_element_type=F32)


def _hdot(a, b):
    return jnp.dot(a, b, preferred_element_type=F32, precision=lax.Precision.HIGHEST)


def _hdot_nt(a, b):
    return lax.dot_general(a, b, (((1,), (1,)), ((), ())), preferred_element_type=F32, precision=lax.Precision.HIGHEST)


def _hdot_tn(a, b):
    return lax.dot_general(a, b, (((0,), (0,)), ((), ())), preferred_element_type=F32, precision=lax.Precision.HIGHEST)


def _tri(n):
    r = lax.broadcasted_iota(jnp.int32, (n, n), 0)
    c = lax.broadcasted_iota(jnp.int32, (n, n), 1)
    return r >= c


def _rms(x, g):
    return x * lax.rsqrt(jnp.mean(x * x, axis=-1, keepdims=True) + NORM_EPS) * g


def _silu(x):
    return x * jax.nn.sigmoid(x)


def _shift_rows(x, prev, j):
    if j == 0:
        return x
    t = x.shape[0]

    def fwd_impl(x, prev):
        row = lax.broadcasted_iota(jnp.int32, x.shape, 0)
        return jnp.where(row >= j, pltpu.roll(x, j, 0), pltpu.roll(prev, j, 0))

    @jax.custom_vjp
    def sh(x, prev):
        return fwd_impl(x, prev)

    def sh_fwd(x, prev):
        return fwd_impl(x, prev), None

    def sh_bwd(_, gy):
        row = lax.broadcasted_iota(jnp.int32, gy.shape, 0)
        back = pltpu.roll(gy, t - j, 0)
        return jnp.where(row < t - j, back, 0.0), jnp.where(row >= t - j, back, 0.0)

    sh.defvjp(sh_fwd, sh_bwd)
    return sh(x, prev)


def _one_minus_exp(x):
    series = -x * (1.0 + x * (0.5 + x * (1.0 / 6.0 + x * (1.0 / 24.0 + x * (1.0 / 120.0)))))
    return jnp.where(x > -0.05, series, 1.0 - jnp.exp(x))


def _tile(n, cap):
    if n <= cap:
        return n
    best = None
    for t in range(LANE, cap + 1, LANE):
        if n % t == 0:
            best = t
    assert best is not None, (n, cap)
    return best


def _mm(name, a, b, *, ta=False, tb=False, add=None, out_dtype=F32):
    m, k = (a.shape[1], a.shape[0]) if ta else a.shape
    n, kb = (b.shape[0], b.shape[1]) if tb else (b.shape[1], b.shape[0])
    assert k == kb, (name, a.shape, b.shape, ta, tb)
    tm, tn, tk = _tile(m, 512), _tile(n, 512), _tile(k, 2048)
    nk = k // tk
    dn = (((0 if ta else 1,), (1 if tb else 0,)), ((), ()))
    has_add = add is not None

    def body(*refs):
        a_ref, b_ref = refs[0], refs[1]
        o_ref, acc = refs[-2], refs[-1]
        kk = pl.program_id(2)

        @pl.when(kk == 0)
        def _():
            acc[...] = jnp.zeros(acc.shape, F32)

        acc[...] += lax.dot_general(a_ref[...].astype(BF16), b_ref[...].astype(BF16), dn, preferred_element_type=F32)

        @pl.when(kk == nk - 1)
        def _():
            r = acc[...]
            if has_add:
                r = r + refs[2][...].astype(F32)
            o_ref[...] = r.astype(out_dtype)

    a_spec = pl.BlockSpec((tk, tm), lambda i, j, q: (q, i)) if ta else pl.BlockSpec((tm, tk), lambda i, j, q: (i, q))
    b_spec = pl.BlockSpec((tn, tk), lambda i, j, q: (j, q)) if tb else pl.BlockSpec((tk, tn), lambda i, j, q: (q, j))
    o_spec = pl.BlockSpec((tm, tn), lambda i, j, q: (i, j))
    in_specs, args = [a_spec, b_spec], [a, b]
    if has_add:
        in_specs.append(o_spec)
        args.append(add)
    return pl.pallas_call(
        body, name=name, grid=(m // tm, n // tn, nk), in_specs=in_specs, out_specs=o_spec,
        out_shape=jax.ShapeDtypeStruct((m, n), out_dtype), scratch_shapes=[pltpu.VMEM((tm, tn), F32)],
        compiler_params=_CP(dimension_semantics=("parallel", "parallel", "arbitrary")),
    )(*args)


class In:
    def __init__(self, arr, block, imap, kind="x", per_h=False, gdtype=F32):
        self.arr, self.block, self.imap, self.kind, self.per_h, self.gdtype = arr, tuple(block), imap, kind, per_h, gdtype

    def spec(self, rev_g=None):
        imap = self.imap
        if rev_g is None:
            return pl.BlockSpec(self.block, lambda h, g: imap(h, g))
        return pl.BlockSpec(self.block, lambda h, g: imap(h, rev_g - 1 - g))


class Out:
    def __init__(self, shape, dtype, block, imap):
        self.shape, self.dtype, self.block, self.imap = tuple(shape), dtype, tuple(block), imap

    def spec(self, rev_g=None):
        imap = self.imap
        if rev_g is None:
            return pl.BlockSpec(self.block, lambda h, g: imap(h, g))
        return pl.BlockSpec(self.block, lambda h, g: imap(h, rev_g - 1 - g))


def _load_f32(ref):
    v = ref[...]
    return v.astype(F32) if jnp.issubdtype(v.dtype, jnp.floating) else v


def _state_out(grid, shape):
    nd = len(shape)
    return Out(tuple(grid) + tuple(shape), F32, (None, None) + tuple(shape), lambda h, g: (h, g) + (0,) * nd)


def _carry(comm, grid, refs, n_in, n_out, n_scr):
    n_c = len(comm.ops) if comm is not None else 0
    n_s = len(comm.sem_shapes) if comm is not None else 0
    p = 0
    in_refs = refs[p:p + n_in]; p += n_in
    c_src = refs[p:p + n_c]; p += n_c
    out_refs = refs[p:p + n_out]; p += n_out
    c_dst = refs[p:p + n_c]; p += n_c
    scr = refs[p:p + n_scr]; p += n_scr
    c_sem = refs[p:p + n_s]
    first = jnp.logical_and(pl.program_id(0) == 0, pl.program_id(1) == 0)
    last = jnp.logical_and(pl.program_id(0) == grid[0] - 1, pl.program_id(1) == grid[1] - 1)
    return in_refs, out_refs, scr, (c_src, c_dst, c_sem), first, last


def _carry_specs(comm):
    if comm is None:
        return [], [], [], [], []
    any_spec = pl.BlockSpec(memory_space=pl.ANY)
    n = len(comm.ops)
    return [any_spec] * n, list(comm.ops), [any_spec] * n, list(comm.out_shapes), list(comm.sem_shapes)


def _op_fwd(name, f, grid, ins, outs, state_shapes=(), comm=None):
    n_in, n_out, n_st = len(ins), len(outs), len(state_shapes)
    st_outs = [_state_out(grid, s) for s in state_shapes]

    def body(*refs):
        in_refs, o_refs, st_scr, cargs, first, last = _carry(comm, grid, refs, n_in, n_out + n_st, n_st)
        out_refs, sv_refs = o_refs[:n_out], o_refs[n_out:]
        if comm is not None:
            @pl.when(first)
            def _():
                comm.start(*cargs)
        g = pl.program_id(1)
        if n_st:
            @pl.when(g == 0)
            def _():
                for s in st_scr:
                    s[...] = jnp.zeros(s.shape, F32)
        vals = [_load_f32(r) for r in in_refs]
        sts = [s[...] for s in st_scr]
        o, ns = f(g, vals, sts)
        for r, v in zip(out_refs, o):
            r[...] = v.astype(r.dtype)
        for r, s in zip(sv_refs, sts):
            r[...] = s
        for s, v in zip(st_scr, ns):
            s[...] = v
        if comm is not None:
            @pl.when(last)
            def _():
                comm.finish(*cargs)

    all_outs = list(outs) + st_outs
    c_in_specs, c_args, c_out_specs, c_out_shapes, c_sems = _carry_specs(comm)
    res = pl.pallas_call(
        body, name=name, grid=tuple(grid), in_specs=[i.spec() for i in ins] + c_in_specs,
        out_specs=[o.spec() for o in all_outs] + c_out_specs,
        out_shape=[jax.ShapeDtypeStruct(o.shape, o.dtype) for o in all_outs] + c_out_shapes,
        scratch_shapes=[pltpu.VMEM(tuple(s), F32) for s in state_shapes] + c_sems,
        compiler_params=_CP(dimension_semantics=("arbitrary", "arbitrary")),
    )(*[i.arr for i in ins], *c_args)
    n_all = n_out + n_st
    return list(res[:n_out]), list(res[n_out:n_all]), list(res[n_all:])


def _op_bwd(name, f, grid, ins, outs, state_shapes, saved, douts, addto=None, comm=None):
    n_in, n_out, n_st = len(ins), len(outs), len(state_shapes)
    n_g = grid[1]
    addto = addto or {}
    diff = [k for k, i in enumerate(ins) if i.kind in ("x", "p")]
    add_idx = sorted(addto)
    st_ins = [In(s, o.block, o.imap, "c") for s, o in zip(saved, [_state_out(grid, s) for s in state_shapes])]
    dout_ins = [In(d, o.block, o.imap, "c") for d, o in zip(douts, outs)]
    add_ins = [In(addto[k], ins[k].block, ins[k].imap, "c") for k in add_idx]
    g_outs = []
    for k in diff:
        i = ins[k]
        if i.kind == "x":
            g_outs.append(Out(i.arr.shape, i.gdtype, i.block, i.imap))
        else:
            g_outs.append(Out(i.arr.shape, F32, i.block, i.imap))

    def body(*refs):
        all_in, go_refs, ds_scr, cargs, first_step, last_step = _carry(comm, grid, refs, n_in + n_st + n_out + len(add_idx), len(diff), n_st)
        if comm is not None:
            @pl.when(first_step)
            def _():
                comm.start(*cargs)
        p = 0
        in_refs = all_in[p:p + n_in]; p += n_in
        sv_refs = all_in[p:p + n_st]; p += n_st
        do_refs = all_in[p:p + n_out]; p += n_out
        ad_refs = all_in[p:p + len(add_idx)]
        hh = pl.program_id(0)
        step = pl.program_id(1)
        g = n_g - 1 - step
        if n_st:
            @pl.when(step == 0)
            def _():
                for s in ds_scr:
                    s[...] = jnp.zeros(s.shape, F32)
        vals = [_load_f32(r) for r in in_refs]
        sts = [r[...] for r in sv_refs]

        def fw(dvals, states):
            full = list(vals)
            for k, v in zip(diff, dvals):
                full[k] = v
            o, ns = f(g, full, states)
            return list(o), list(ns)

        _, vjp = jax.vjp(fw, [vals[k] for k in diff], sts)
        cts = [r[...].astype(F32) for r in do_refs]
        dns = [s[...] for s in ds_scr]
        dvals, dsts = vjp((cts, dns))
        adds = dict(zip(add_idx, ad_refs))
        for k, r, dv in zip(diff, go_refs, dvals):
            i = ins[k]
            if i.kind == "x":
                if k in adds:
                    dv = dv + adds[k][...].astype(F32)
                r[...] = dv.astype(r.dtype)
            else:
                first = (step == 0) if i.per_h else jnp.logical_and(step == 0, hh == 0)

                @pl.when(first)
                def _(r=r, dv=dv):
                    r[...] = dv

                @pl.when(jnp.logical_not(first))
                def _(r=r, dv=dv):
                    r[...] += dv
        for s, v in zip(ds_scr, dsts):
            s[...] = v
        if comm is not None:
            @pl.when(last_step)
            def _():
                comm.finish(*cargs)

    all_ins = list(ins) + st_ins + dout_ins + add_ins
    c_in_specs, c_args, c_out_specs, c_out_shapes, c_sems = _carry_specs(comm)
    res = pl.pallas_call(
        body, name=name, grid=tuple(grid), in_specs=[i.spec(n_g) for i in all_ins] + c_in_specs,
        out_specs=[o.spec(n_g) for o in g_outs] + c_out_specs,
        out_shape=[jax.ShapeDtypeStruct(o.shape, o.dtype) for o in g_outs] + c_out_shapes,
        scratch_shapes=[pltpu.VMEM(tuple(s), F32) for s in state_shapes] + c_sems,
        compiler_params=_CP(dimension_semantics=("arbitrary", "arbitrary")),
    )(*[i.arr for i in all_ins], *c_args)
    return list(res[:len(g_outs)]), list(res[len(g_outs):])


class Op:
    def __init__(self, name, f, grid, ins, outs, state_shapes=()):
        self.name, self.f, self.grid, self.ins, self.outs, self.state_shapes = name, f, grid, ins, outs, state_shapes
        self.saved = None

    def fwd(self, comm=None):
        res, self.saved, self.fwd_comm_out = _op_fwd(self.name + "_fwd", self.f, self.grid, self.ins, self.outs, self.state_shapes, comm)
        return res

    def bwd(self, douts, addto=None, comm=None):
        res, self.bwd_comm_out = _op_bwd(self.name + "_bwd", self.f, self.grid, self.ins, self.outs, self.state_shapes, self.saved, douts,
                                         addto, comm)
        return res


def _rows(arr, t, kind="x", gdtype=F32):
    return In(arr, (t, arr.shape[1]), lambda h, g: (g, 0), kind, gdtype=gdtype)


def _whole(arr, kind="p"):
    nd = arr.ndim
    return In(arr, arr.shape, lambda h, g: (0,) * nd, kind)


def _rows_out(s, n, t, dtype):
    return Out((s, n), dtype, (t, n), lambda h, g: (g, 0))


ROW_T = 256


def _rms_op(name, x, gain, out_dtype=BF16, gdtype=F32):
    s, n = x.shape

    def f(g, vals, sts):
        return [_rms(vals[0], vals[1])], []

    return Op(name, f, (1, s // ROW_T), [_rows(x, ROW_T, gdtype=gdtype), _whole(gain.reshape(1, n))], [_rows_out(s, n, ROW_T, out_dtype)])


def _mla_prep_op(qn, q1, q2, kn, kr, v, cos, sin):
    s = qn.shape[0]
    hd, half = MLA_HEADS, MLA_ROPE // 2

    def f(g, vals, sts):
        qn, q1, q2, kn, kr, v, cos, sin = vals
        cos_h, sin_h = jnp.tile(cos, (1, hd)), jnp.tile(sin, (1, hd))
        r1 = q1 * cos_h - q2 * sin_h
        r2 = q2 * cos_h + q1 * sin_h
        k1, k2 = kr[:, 0:half], kr[:, half:2 * half]
        kr1 = k1 * cos - k2 * sin
        kr2 = k2 * cos + k1 * sin
        zpad = jnp.zeros((qn.shape[0], LANE - MLA_QK), F32)
        qs, ks, vs = [], [], []
        for h in range(hd):
            a, b = h * MLA_NOPE, (h + 1) * MLA_NOPE
            c, d = h * half, (h + 1) * half
            qs.append(jnp.concatenate([qn[:, a:b], r1[:, c:d], r2[:, c:d], zpad], axis=1))
            ks.append(jnp.concatenate([kn[:, a:b], kr1, kr2, zpad], axis=1))
            vs.append(v[:, a:b])
        return [jnp.stack(qs, 0), jnp.stack(ks, 0), jnp.stack(vs, 0)], []

    ins = [_rows(qn, ROW_T, gdtype=BF16), _rows(q1, ROW_T, gdtype=BF16), _rows(q2, ROW_T, gdtype=BF16), _rows(kn, ROW_T, gdtype=BF16),
           _rows(kr, ROW_T, gdtype=BF16), _rows(v, ROW_T, gdtype=BF16), _rows(cos, ROW_T, "c"), _rows(sin, ROW_T, "c")]
    outs = [Out((hd, s, LANE), BF16, (hd, ROW_T, LANE), lambda h, g: (0, g, 0)),
            Out((hd, s, LANE), BF16, (hd, ROW_T, LANE), lambda h, g: (0, g, 0)),
            Out((hd, s, MLA_V), BF16, (hd, ROW_T, MLA_V), lambda h, g: (0, g, 0))]
    return Op("mla_prep", f, (1, s // ROW_T), ins, outs)


ATT_TQ = 256


def _mla_attn_op(q, k, v):
    hd, s, _ = q.shape
    scale = MLA_QK ** -0.5

    def f(g, vals, sts):
        q, k, v = vals
        sc = _bdot_nt(q, k) * scale
        r = lax.broadcasted_iota(jnp.int32, sc.shape, 0) + g * ATT_TQ
        c = lax.broadcasted_iota(jnp.int32, sc.shape, 1)
        sc = jnp.where(r >= c, sc, -1e30)
        m = lax.stop_gradient(jnp.max(sc, axis=-1, keepdims=True))
        p = jnp.exp(sc - m)
        p = p / jnp.sum(p, axis=-1, keepdims=True)
        return [_bdot(p, v)], []

    ins = [In(q, (None, ATT_TQ, LANE), lambda h, g: (h, g, 0), "x"),
           In(k, (None, s, LANE), lambda h, g: (h, 0, 0), "p", per_h=True),
           In(v, (None, s, MLA_V), lambda h, g: (h, 0, 0), "p", per_h=True)]
    outs = [Out((hd, s, MLA_V), F32, (None, ATT_TQ, MLA_V), lambda h, g: (h, g, 0))]
    return Op("mla_attn", f, (hd, s // ATT_TQ), ins, outs)


def _mla_post_op(o, gate):
    hd, s, _ = o.shape

    def f(g, vals, sts):
        o, gate = vals
        cat = jnp.concatenate([o[h] for h in range(hd)], axis=1)
        return [cat * _silu(gate)], []

    ins = [In(o, (hd, ROW_T, MLA_V), lambda h, g: (0, g, 0), "x"), _rows(gate, ROW_T, gdtype=BF16)]
    return Op("mla_post", f, (1, s // ROW_T), ins, [_rows_out(s, hd * MLA_V, ROW_T, BF16)])


def _gla_gate_op(gk, w2, b):
    s = gk.shape[0]

    def f(g, vals, sts):
        gk, w2, b = vals
        return [jax.nn.log_sigmoid(_bdot(gk, w2) + b) / GLA_TAU], []

    ins = [_rows(gk, ROW_T, gdtype=BF16), _whole(w2), _whole(b)]
    return Op("gla_gate", f, (1, s // ROW_T), ins, [_rows_out(s, GLA_HEADS * GLA_DK, ROW_T, F32)])


def _gla_core_op(q, k, v, gate, la, g_o):
    s = q.shape[0]
    c = GLA_CHUNK

    def f(g, vals, sts):
        q, k, v, gate, la, g_o = vals
        st = sts[0]
        tri = _tri(c)
        b = _hdot(tri.astype(F32), la)
        b_last = jnp.sum(la, axis=0, keepdims=True)
        qt = q * (GLA_DK ** -0.5) * jnp.exp(b)
        kt = k * jnp.exp(-b)
        att = jnp.where(tri, _bdot_nt(qt, kt), 0.0)
        o = _bdot(att, v) + _bdot(qt, st)
        d_state = _bdot_tn(k * jnp.exp(b_last - b), v)
        b_last_col = _hdot_tn(la, jnp.ones((c, 1), F32))
        new_st = jnp.exp(b_last_col) * st + d_state
        y = _rms(o, g_o) * _silu(gate)
        return [y], [new_st]

    hk = lambda h, g: (g, h)
    ins = [In(q, (c, GLA_DK), hk, "x", gdtype=BF16), In(k, (c, GLA_DK), hk, "x", gdtype=BF16), In(v, (c, GLA_DV), hk, "x", gdtype=BF16),
           In(gate, (c, GLA_DV), hk, "x", gdtype=BF16), In(la, (c, GLA_DK), hk, "x"), _whole(g_o)]
    outs = [Out((s, GLA_HEADS * GLA_DV), BF16, (c, GLA_DV), hk)]
    return Op("gla_core", f, (GLA_HEADS, s // c), ins, outs, [(GLA_DK, GLA_DV)])


LRU_T = 256


def _lru_op(gate, u, conv_w, conv_b, w_a, b_a, w_x, b_x, lam):
    s, w = u.shape
    t = LRU_T

    def f(g, vals, sts):
        gate, u, cw, cb, w_a, b_a, w_x, b_x, lam = vals
        u_prev, h_prev = sts
        uc = cb
        for kk in range(CONV_W):
            uc = uc + cw[kk] * _shift_rows(u, u_prev, CONV_W - 1 - kk)
        ra, ri = [], []
        for n in range(LRU_BLOCKS):
            blk = uc[:, n * LRU_BLOCK:(n + 1) * LRU_BLOCK]
            ra.append(_bdot(blk, w_a[n]))
            ri.append(_bdot(blk, w_x[n]))
        r = jax.nn.sigmoid(jnp.concatenate(ra, axis=1) + b_a)
        i = jax.nn.sigmoid(jnp.concatenate(ri, axis=1) + b_x)
        log_a = -LRU_C * r * jax.nn.softplus(-lam)
        a = jnp.exp(log_a)
        bb = jnp.sqrt(_one_minus_exp(2.0 * log_a)) * (i * uc)
        zero = jnp.zeros_like(a)
        sh = 1
        while sh < t:
            a_s = _shift_rows(a - 1.0, zero, sh) + 1.0
            b_s = _shift_rows(bb, zero, sh)
            bb = a * b_s + bb
            a = a * a_s
            sh *= 2
        hs = bb + a * h_prev
        last = (lax.broadcasted_iota(jnp.int32, hs.shape, 0) == t - 1).astype(F32)
        h_last = jnp.sum(hs * last, axis=0, keepdims=True)
        return [hs * _silu(gate)], [u, h_last]

    ins = [_rows(gate, t, gdtype=BF16), _rows(u, t, gdtype=BF16), _whole(conv_w), _whole(conv_b), _whole(w_a), _whole(b_a), _whole(w_x),
           _whole(b_x), _whole(lam)]
    return Op("lru_core", f, (1, s // t), ins, [_rows_out(s, w, t, BF16)], [(t, w), (1, w)])


def _ssd_conv_op(xbc, conv_w, conv_b):
    s, w = xbc.shape
    t = ROW_T
    n_x, n_b = SSD_INNER, SSD_GROUPS * SSD_STATE

    def f(g, vals, sts):
        xbc, cw, cb = vals
        acc = cb
        for kk in range(CONV_W):
            acc = acc + cw[kk] * _shift_rows(xbc, sts[0], CONV_W - 1 - kk)
        y = _silu(acc)
        return [y[:, :n_x], y[:, n_x:n_x + n_b], y[:, n_x + n_b:]], [xbc]

    ins = [_rows(xbc, t, gdtype=BF16), _whole(conv_w), _whole(conv_b)]
    outs = [_rows_out(s, n_x, t, F32), _rows_out(s, n_b, t, F32), _rows_out(s, n_b, t, F32)]
    return Op("ssd_conv", f, (1, s // t), ins, outs, [(t, w)])


SSD_L = 256


def _ssd_core_op(x, bm, cm, z, dt, dt_bias, a_log, d_skip, g_norm):
    s = x.shape[0]
    c, hg, p = SSD_L, SSD_HPG, SSD_P
    gw = hg * p

    def f(g, vals, sts):
        x, bm, cm, z, dtr, dt_bias, a_log, d_skip, g_norm = vals
        st = sts[0]
        tri = _tri(c)
        dt = jax.nn.softplus(dtr + dt_bias)
        da = dt * (-jnp.exp(a_log))
        cs = _hdot(tri.astype(F32), da)
        cs_last = jnp.sum(da, axis=0, keepdims=True)
        cb = _bdot_nt(cm, bm)
        lane = lax.broadcasted_iota(jnp.int32, (c, hg), 1)
        ys, new_st = [], []
        for h in range(hg):
            cs_h = cs[:, h:h + 1]
            cs_row = _hdot_nt((lane == h).astype(F32), cs)
            seg = jnp.where(tri, cs_h - cs_row, 0.0)
            lmat = jnp.where(tri, jnp.exp(seg), 0.0)
            x_h = x[:, h * p:(h + 1) * p]
            xdt = x_h * dt[:, h:h + 1]
            y_diag = _bdot(cb * lmat, xdt)
            decay = jnp.exp(cs_last[:, h:h + 1] - cs_h)
            states = _bdot_tn(xdt * decay, bm)
            y_off = _bdot_nt(cm, st[h]) * jnp.exp(cs_h)
            new_st.append(jnp.exp(cs_last[:, h:h + 1]) * st[h] + states)
            ys.append(y_diag + y_off + d_skip[:, h:h + 1] * x_h)
        y = jnp.concatenate(ys, axis=1) * _silu(z)
        return [_rms(y, g_norm)], [jnp.stack(new_st, 0)]

    ins = [In(x, (c, gw), lambda h, g: (g, h), "x"), In(bm, (c, SSD_STATE), lambda h, g: (g, h), "x"),
           In(cm, (c, SSD_STATE), lambda h, g: (g, h), "x"), In(z, (c, gw), lambda h, g: (g, h), "x", gdtype=BF16),
           In(dt, (None, c, hg), lambda h, g: (h, g, 0), "x"),
           In(dt_bias, (None, 1, hg), lambda h, g: (h, 0, 0), "p", per_h=True),
           In(a_log, (None, 1, hg), lambda h, g: (h, 0, 0), "p", per_h=True),
           In(d_skip, (None, 1, hg), lambda h, g: (h, 0, 0), "p", per_h=True),
           In(g_norm, (1, gw), lambda h, g: (0, h), "p", per_h=True)]
    outs = [Out((s, SSD_INNER), BF16, (c, gw), lambda h, g: (g, h))]
    return Op("ssd_core", f, (SSD_GROUPS, s // c), ins, outs, [(hg, p, SSD_STATE)])


def _loss_op(h, target, final_g):
    s, n = h.shape
    t = ROW_T
    n_g = s // t

    def body(h_ref, t_ref, g_ref, loss_ref, dh_ref, dg_ref):
        step = pl.program_id(0)

        def lossf(hv, gv):
            err = _rms(hv, gv) - t_ref[...]
            return 0.5 * jnp.sum(jnp.mean(err * err, axis=-1))

        l, (dh, dg) = jax.value_and_grad(lossf, argnums=(0, 1))(h_ref[...], g_ref[...])
        dh_ref[...] = dh

        @pl.when(step == 0)
        def _():
            loss_ref[...] = jnp.zeros(loss_ref.shape, F32)
            dg_ref[...] = jnp.zeros(dg_ref.shape, F32)

        loss_ref[...] += jnp.full(loss_ref.shape, l, F32)
        dg_ref[...] += dg

    row = pl.BlockSpec((t, n), lambda g: (g, 0))
    one = pl.BlockSpec((1, n), lambda g: (0, 0))
    return pl.pallas_call(
        body, name="loss_head", grid=(n_g,), in_specs=[row, row, one],
        out_specs=[pl.BlockSpec((1, LANE), lambda g: (0, 0)), row, one],
        out_shape=[jax.ShapeDtypeStruct((1, LANE), F32), jax.ShapeDtypeStruct((s, n), F32), jax.ShapeDtypeStruct((1, n), F32)],
        compiler_params=_CP(dimension_semantics=("arbitrary",)),
    )(h, target, final_g.reshape(1, n))


def _pad_cols(w, n):
    return jnp.pad(w, ((0, 0), (0, n - w.shape[1])))


def _pad_rows(w, n):
    return jnp.pad(w, ((0, n - w.shape[0]), (0, 0)))


def _proj_bwd(tag, u, dps, ws):
    du = None
    for i, (dp, w) in enumerate(zip(dps, ws)):
        du = _mm(f"{tag}_du{i}", dp, w, tb=True, add=du)
    dws = [_mm(f"{tag}_dw{i}", u, dp, ta=True) for i, dp in enumerate(dps)]
    return du, dws


def _mla_layer(h, norm_g, w, cos, sin, fwd_comm=None):
    bf = lambda a: a.astype(BF16)
    w_in, w_uq, w_ukv = w["mla_w_in"], w["mla_w_uq"], w["mla_w_ukv"]
    a0, a1, a2 = MLA_Q_RANK, MLA_Q_RANK + MLA_KV_RANK, MLA_Q_RANK + MLA_KV_RANK + MLA_ROPE
    w_cq, w_ckv, w_kr, w_g = bf(w_in[:, :a0]), bf(w_in[:, a0:a1]), bf(_pad_cols(w_in[:, a1:a2], LANE)), bf(w_in[:, a2:])
    uq = w_uq.reshape(MLA_Q_RANK, MLA_HEADS, MLA_QK)
    half = MLA_ROPE // 2
    w_qn = bf(uq[:, :, :MLA_NOPE].reshape(MLA_Q_RANK, -1))
    w_q1 = bf(uq[:, :, MLA_NOPE:MLA_NOPE + half].reshape(MLA_Q_RANK, -1))
    w_q2 = bf(uq[:, :, MLA_NOPE + half:].reshape(MLA_Q_RANK, -1))
    ukv = w_ukv.reshape(MLA_KV_RANK, MLA_HEADS, MLA_NOPE + MLA_V)
    w_kn = bf(ukv[:, :, :MLA_NOPE].reshape(MLA_KV_RANK, -1))
    w_v = bf(ukv[:, :, MLA_NOPE:].reshape(MLA_KV_RANK, -1))
    w_out = bf(w["mla_w_out"])

    n0 = _rms_op("mla_norm", h, norm_g)
    u, = n0.fwd()
    cq, ckv, kr, gate = (_mm(f"mla_in{i}", u, wi) for i, wi in enumerate((w_cq, w_ckv, w_kr, w_g)))
    nq = _rms_op("mla_qnorm", cq, w["mla_g_q"], gdtype=BF16)
    nkv = _rms_op("mla_kvnorm", ckv, w["mla_g_kv"], gdtype=BF16)
    qn_, = nq.fwd()
    kvn_, = nkv.fwd()
    qn, q1, q2 = (_mm(f"mla_uq{i}", qn_, wi) for i, wi in enumerate((w_qn, w_q1, w_q2)))
    kn, v = (_mm(f"mla_ukv{i}", kvn_, wi) for i, wi in enumerate((w_kn, w_v)))
    prep = _mla_prep_op(qn, q1, q2, kn, kr, v, cos, sin)
    qh, kh, vh = prep.fwd()
    attn = _mla_attn_op(qh, kh, vh)
    o, = attn.fwd(fwd_comm)
    post = _mla_post_op(o, gate)
    y, = post.fwd()
    h_out = _mm("mla_out", y, w_out, add=h)

    def bwd(dh, comm=None):
        dy = _mm("mla_out_dy", dh, w_out, tb=True, out_dtype=BF16)
        d_w_out = _mm("mla_out_dw", y, dh, ta=True)
        do, dgate = post.bwd([dy])
        dqh, dkh, dvh = attn.bwd([do], comm=comm)
        dqn, dq1, dq2, dkn, dkr, dv = prep.bwd([dqh, dkh, dvh])
        dqn_, d_uq = _proj_bwd("mla_uq", qn_, (dqn, dq1, dq2), (w_qn, w_q1, w_q2))
        dkvn_, d_ukv = _proj_bwd("mla_ukv", kvn_, (dkn, dv), (w_kn, w_v))
        dcq, d_g_q = nq.bwd([dqn_])
        dckv, d_g_kv = nkv.bwd([dkvn_])
        du, d_in = _proj_bwd("mla_in", u, (dcq, dckv, dkr, dgate), (w_cq, w_ckv, w_kr, w_g))
        dh_in, d_norm = n0.bwd([du], addto={0: dh})
        shp = (MLA_Q_RANK, MLA_HEADS, -1)
        g_uq = jnp.concatenate([d_uq[0].reshape(shp), d_uq[1].reshape(shp), d_uq[2].reshape(shp)], axis=2).reshape(MLA_Q_RANK, -1)
        shp = (MLA_KV_RANK, MLA_HEADS, -1)
        g_ukv = jnp.concatenate([d_ukv[0].reshape(shp), d_ukv[1].reshape(shp)], axis=2).reshape(MLA_KV_RANK, -1)
        g_in = jnp.concatenate([d_in[0], d_in[1], d_in[2][:, :MLA_ROPE], d_in[3]], axis=1)
        return dh_in, d_norm, {"mla_w_in": g_in, "mla_g_q": d_g_q.reshape(-1), "mla_w_uq": g_uq, "mla_g_kv": d_g_kv.reshape(-1),
                               "mla_w_ukv": g_ukv, "mla_w_out": d_w_out}, attn.bwd_comm_out

    return h_out, bwd, attn.fwd_comm_out


def _gla_layer(h, norm_g, w, fwd_comm=None):
    bf = lambda a: a.astype(BF16)
    w_in = w["gla_w_in"]
    nk, nv = GLA_HEADS * GLA_DK, GLA_HEADS * GLA_DV
    cuts = (0, nk, 2 * nk, 2 * nk + nv, 2 * nk + 2 * nv)
    w_q, w_k, w_v, w_g = (bf(w_in[:, cuts[i]:cuts[i + 1]]) for i in range(4))
    w_gk = bf(_pad_cols(w_in[:, cuts[4]:], LANE))
    w2 = _pad_rows(w["gla_w_gk2"], LANE)
    b_gk = w["gla_b_gk"].reshape(1, -1)
    g_o = w["gla_g_o"].reshape(1, -1)
    w_out = bf(w["gla_w_out"])

    n0 = _rms_op("gla_norm", h, norm_g)
    u, = n0.fwd()
    q, k, v, gate, gk = (_mm(f"gla_in{i}", u, wi) for i, wi in enumerate((w_q, w_k, w_v, w_g, w_gk)))
    gop = _gla_gate_op(gk, w2, b_gk)
    la, = gop.fwd()
    core = _gla_core_op(q, k, v, gate, la, g_o)
    y, = core.fwd(fwd_comm)
    h_out = _mm("gla_out", y, w_out, add=h)

    def bwd(dh):
        dy = _mm("gla_out_dy", dh, w_out, tb=True, out_dtype=BF16)
        d_w_out = _mm("gla_out_dw", y, dh, ta=True)
        dq, dk, dv, dgate, dla, d_g_o = core.bwd([dy])
        dgk, d_w2, d_b = gop.bwd([dla])
        du, d_in = _proj_bwd("gla_in", u, (dq, dk, dv, dgate, dgk), (w_q, w_k, w_v, w_g, w_gk))
        dh_in, d_norm = n0.bwd([du], addto={0: dh})
        g_in = jnp.concatenate([d_in[0], d_in[1], d_in[2], d_in[3], d_in[4][:, :GLA_RANK]], axis=1)
        return dh_in, d_norm, {"gla_w_in": g_in, "gla_w_gk2": d_w2[:GLA_RANK], "gla_b_gk": d_b.reshape(-1), "gla_g_o": d_g_o.reshape(-1),
                               "gla_w_out": d_w_out}

    return h_out, bwd, core.fwd_comm_out


def _lru_layer(h, norm_g, w):
    bf = lambda a: a.astype(BF16)
    w_in = w["lru_w_in"]
    w_g, w_u = bf(w_in[:, :LRU_WIDTH]), bf(w_in[:, LRU_WIDTH:])
    row = lambda a: a.reshape(1, -1)
    w_out = bf(w["lru_w_out"])

    n0 = _rms_op("lru_norm", h, norm_g)
    u_, = n0.fwd()
    gate, u = (_mm(f"lru_in{i}", u_, wi) for i, wi in enumerate((w_g, w_u)))
    core = _lru_op(gate, u, w["lru_conv_w"].reshape(CONV_W, 1, -1), row(w["lru_conv_b"]), w["lru_w_a"], row(w["lru_b_a"]), w["lru_w_x"],
                   row(w["lru_b_x"]), row(w["lru_lam"]))
    y, = core.fwd()
    h_out = _mm("lru_out", y, w_out, add=h)

    def bwd(dh):
        dy = _mm("lru_out_dy", dh, w_out, tb=True, out_dtype=BF16)
        d_w_out = _mm("lru_out_dw", y, dh, ta=True)
        dgate, du, d_cw, d_cb, d_wa, d_ba, d_wx, d_bx, d_lam = core.bwd([dy])
        du_, d_in = _proj_bwd("lru_in", u_, (dgate, du), (w_g, w_u))
        dh_in, d_norm = n0.bwd([du_], addto={0: dh})
        return dh_in, d_norm, {"lru_w_in": jnp.concatenate(d_in, axis=1), "lru_conv_w": d_cw.reshape(CONV_W, -1), "lru_conv_b": d_cb.reshape(-1),
                               "lru_w_a": d_wa, "lru_b_a": d_ba.reshape(-1), "lru_w_x": d_wx, "lru_b_x": d_bx.reshape(-1),
                               "lru_lam": d_lam.reshape(-1), "lru_w_out": d_w_out}

    return h_out, bwd


def _ssd_layer(h, norm_g, w):
    bf = lambda a: a.astype(BF16)
    s = h.shape[0]
    w_in = w["ssd_w_in"]
    conv_dim = SSD_INNER + 2 * SSD_GROUPS * SSD_STATE
    w_z, w_xbc = bf(w_in[:, :SSD_INNER]), bf(w_in[:, SSD_INNER:SSD_INNER + conv_dim])
    w_dt = bf(_pad_cols(w_in[:, SSD_INNER + conv_dim:], LANE))
    grp = lambda a: a.reshape(SSD_GROUPS, 1, SSD_HPG)
    w_out = bf(w["ssd_w_out"])

    n0 = _rms_op("ssd_norm", h, norm_g)
    u, = n0.fwd()
    z, xbc, dtp = (_mm(f"ssd_in{i}", u, wi) for i, wi in enumerate((w_z, w_xbc, w_dt)))
    conv = _ssd_conv_op(xbc, w["ssd_conv_w"].reshape(CONV_W, 1, -1), w["ssd_conv_b"].reshape(1, -1))
    x, bm, cm = conv.fwd()
    dt = dtp[:, :SSD_HEADS].reshape(s, SSD_GROUPS, SSD_HPG).transpose(1, 0, 2)
    core = _ssd_core_op(x, bm, cm, z, dt, grp(w["ssd_dt_bias"]), grp(w["ssd_a_log"]), grp(w["ssd_d"]), w["ssd_g_norm"].reshape(1, -1))
    y, = core.fwd()
    h_out = _mm("ssd_out", y, w_out, add=h)

    def bwd(dh):
        dy = _mm("ssd_out_dy", dh, w_out, tb=True, out_dtype=BF16)
        d_w_out = _mm("ssd_out_dw", y, dh, ta=True)
        dx, dbm, dcm, dz, ddt, d_dtb, d_alog, d_d, d_gn = core.bwd([dy])
        dxbc, d_cw, d_cb = conv.bwd([dx, dbm, dcm])
        ddtp = _pad_cols(ddt.transpose(1, 0, 2).reshape(s, SSD_HEADS), LANE).astype(BF16)
        du, d_in = _proj_bwd("ssd_in", u, (dz, dxbc, ddtp), (w_z, w_xbc, w_dt))
        dh_in, d_norm = n0.bwd([du], addto={0: dh})
        g_in = jnp.concatenate([d_in[0], d_in[1], d_in[2][:, :SSD_HEADS]], axis=1)
        return dh_in, d_norm, {"ssd_w_in": g_in, "ssd_conv_w": d_cw.reshape(CONV_W, -1), "ssd_conv_b": d_cb.reshape(-1),
                               "ssd_dt_bias": d_dtb.reshape(-1), "ssd_a_log": d_alog.reshape(-1), "ssd_d": d_d.reshape(-1),
                               "ssd_g_norm": d_gn.reshape(-1), "ssd_w_out": d_w_out}

    return h_out, bwd


def _rope_tables(positions):
    inv_freq = ROPE_THETA ** (-jnp.arange(0, MLA_ROPE, 2, dtype=F32) / MLA_ROPE)
    ang = positions.astype(F32)[:, None] * inv_freq
    return jnp.cos(ang), jnp.sin(ang)


WEIGHTS = ["norm_g", "final_g", "mla_w_in", "mla_g_q", "mla_w_uq", "mla_g_kv", "mla_w_ukv", "mla_w_out", "gla_w_in", "gla_w_gk2", "gla_b_gk",
           "gla_g_o", "gla_w_out", "lru_w_in", "lru_conv_w", "lru_conv_b", "lru_w_a", "lru_b_a", "lru_w_x", "lru_b_x", "lru_lam", "lru_w_out",
           "ssd_w_in", "ssd_conv_w", "ssd_conv_b", "ssd_dt_bias", "ssd_a_log", "ssd_d", "ssd_g_norm", "ssd_w_out"]
BIG = ["mla_w_in", "mla_w_uq", "mla_w_ukv", "mla_w_out", "gla_w_in", "gla_w_out", "lru_w_in", "lru_w_out", "ssd_w_in", "ssd_w_out"]
SMALL = ["gla_w_gk2", "gla_b_gk", "gla_g_o", "lru_conv_w", "lru_conv_b", "lru_b_a", "lru_b_x", "lru_lam", "ssd_conv_w", "ssd_conv_b", "ssd_g_norm"]
REPL = ["norm_g", "final_g", "mla_g_q", "mla_g_kv", "lru_w_a", "lru_w_x", "ssd_dt_bias", "ssd_a_log", "ssd_d"]
N_CHIPS, N_DEV = 4, 8
PACK_W = 1024
ADAM_ROWS = 256
SMALL_ROWS = 64


def _shard_axis(name):
    return 0 if name.endswith("_w_out") else -1


def _pack(arrs, dtype, row_mult):
    flat = jnp.concatenate([a.reshape(-1).astype(dtype) for a in arrs])
    per = PACK_W * row_mult
    total = -(-flat.shape[0] // per) * per
    return jnp.pad(flat, (0, total - flat.shape[0])).reshape(-1, PACK_W)


def _unpack(buf, shapes):
    flat = buf.reshape(-1)
    out, off = [], 0
    for s in shapes:
        n = math.prod(s)
        out.append(flat[off:off + n].reshape(s))
        off += n
    return out


def _mesh_pos():
    return lax.axis_index("x"), lax.axis_index("y"), lax.axis_index("c")


class GatherComm:
    def __init__(self, ops):
        self.ops = list(ops)
        n = len(self.ops)
        self.out_shapes = [jax.ShapeDtypeStruct((N_CHIPS,) + o.shape, o.dtype) for o in self.ops]
        self.sem_shapes = [pltpu.SemaphoreType.DMA((3 * n,)), pltpu.SemaphoreType.DMA((3 * n,)), pltpu.SemaphoreType.DMA((n,))]

    def _copies(self, srcs, dsts, sems):
        send_sems, recv_sems, local_sems = sems
        n = len(self.ops)
        x, y, c = _mesh_pos()
        chips = [(1 - x, y), (x, 1 - y), (1 - x, 1 - y)]
        mine = 2 * x + y

        def copy(i, k, slot, to):
            return pltpu.make_async_remote_copy(src_ref=srcs[i], dst_ref=dsts[i].at[slot], send_sem=send_sems.at[i * 3 + k],
                                                recv_sem=recv_sems.at[i * 3 + k], device_id=to, device_id_type=pl.DeviceIdType.MESH)

        local = [pltpu.make_async_copy(srcs[i], dsts[i].at[mine], local_sems.at[i]) for i in range(n)]
        sends = [copy(i, k, mine, (*chip, c)) for i in range(n) for k, chip in enumerate(chips)]
        recvs = [copy(i, k, 2 * px + py, (x, y, c)) for i in range(n) for k, (px, py) in enumerate(chips)]
        return local, sends, recvs

    def start(self, srcs, dsts, sems):
        local, sends, _ = self._copies(srcs, dsts, sems)
        for cp in local + sends:
            cp.start()

    def finish(self, srcs, dsts, sems):
        local, sends, recvs = self._copies(srcs, dsts, sems)
        for cp in recvs:
            cp.wait_recv()
        for cp in sends:
            cp.wait_send()
        for cp in local:
            cp.wait()


class ExchangeComm:
    def __init__(self, chip_ops, all_ops=()):
        self.ops = list(chip_ops) + list(all_ops)
        self.per_chip = (True,) * len(chip_ops) + (False,) * len(all_ops)
        n = len(self.ops)
        self.out_shapes = [jax.ShapeDtypeStruct((N_DEV,) + o.shape[-2:], o.dtype) for o in self.ops]
        self.sem_shapes = [pltpu.SemaphoreType.DMA((7 * n,)), pltpu.SemaphoreType.DMA((7 * n,)), pltpu.SemaphoreType.DMA((n,))]

    def _copies(self, srcs, dsts, sems):
        send_sems, recv_sems, local_sems = sems
        n, per_chip = len(self.ops), self.per_chip
        x, y, c = _mesh_pos()
        me_id, sibling = (x, y, c), (x, y, 1 - c)
        chips = [(1 - x, y), (x, 1 - y), (1 - x, 1 - y)]

        def dev(px, py, pc):
            return 4 * px + 2 * py + pc

        def part(i, px, py):
            return srcs[i].at[2 * px + py] if per_chip[i] else srcs[i]

        def copy(i, k, src, slot, to):
            return pltpu.make_async_remote_copy(src_ref=src, dst_ref=dsts[i].at[slot], send_sem=send_sems.at[i * 7 + k],
                                                recv_sem=recv_sems.at[i * 7 + k], device_id=to, device_id_type=pl.DeviceIdType.MESH)

        me = dev(x, y, c)
        local = [pltpu.make_async_copy(part(i, x, y), dsts[i].at[me], local_sems.at[i]) for i in range(n)]
        first, ici_recvs, passed, sib_recvs = [], [], [], []
        for i in range(n):
            first.append(copy(i, 0, part(i, x, y), me, sibling))
            first += [copy(i, 1 + k, part(i, px, py), me, (px, py, c)) for k, (px, py) in enumerate(chips)]
            sib_recvs.append(copy(i, 0, part(i, x, y), dev(x, y, 1 - c), me_id))
            for k, (px, py) in enumerate(chips):
                slot = dev(px, py, c)
                ici_recvs.append(copy(i, 1 + k, part(i, x, y), slot, me_id))
                passed.append(copy(i, 4 + k, dsts[i].at[slot], slot, sibling))
                sib_recvs.append(copy(i, 4 + k, part(i, x, y), dev(px, py, 1 - c), me_id))
        return local, first, ici_recvs, passed, sib_recvs

    def start(self, srcs, dsts, sems):
        local, first, _, _, _ = self._copies(srcs, dsts, sems)
        for cp in local + first:
            cp.start()

    def finish(self, srcs, dsts, sems):
        local, first, ici_recvs, passed, sib_recvs = self._copies(srcs, dsts, sems)
        for rc, fw in zip(ici_recvs, passed):
            rc.wait_recv()
            fw.start()
        for cp in sib_recvs:
            cp.wait_recv()
        for cp in first + passed:
            cp.wait_send()
        for cp in local:
            cp.wait()


def _run_comm(name, comm):
    n = len(comm.ops)

    def body(*refs):
        srcs, dsts, sems = refs[:n], refs[n:2 * n], refs[2 * n:]
        comm.start(srcs, dsts, sems)
        comm.finish(srcs, dsts, sems)

    any_spec = pl.BlockSpec(memory_space=pl.ANY)
    return pl.pallas_call(body, name=name, in_specs=[any_spec] * n, out_specs=[any_spec] * n, out_shape=comm.out_shapes,
                          scratch_shapes=comm.sem_shapes)(*comm.ops)


def _adamw(name, parts, w, m, v):
    rows, cols = w.shape
    t = next(c for c in (ADAM_ROWS, ADAM_ROWS // 2, SMALL_ROWS) if rows % c == 0)
    c1 = 1.0 - ADAM_B1 ** ADAM_STEP
    c2 = 1.0 - ADAM_B2 ** ADAM_STEP

    def body(p_ref, w_ref, m_ref, v_ref, g_ref, d_ref, nm_ref, nv_ref):
        g = p_ref[0].astype(F32)
        for d in range(1, N_DEV):
            g = g + p_ref[d].astype(F32)
        nm = ADAM_B1 * m_ref[...] + (1.0 - ADAM_B1) * g
        nv = ADAM_B2 * v_ref[...] + (1.0 - ADAM_B2) * (g * g)
        g_ref[...] = g
        nm_ref[...] = nm
        nv_ref[...] = nv
        d_ref[...] = -ADAM_LR * ((nm / c1) / (jnp.sqrt(nv / c2) + ADAM_EPS) + ADAM_WD * w_ref[...])

    row = pl.BlockSpec((t, cols), lambda i: (i, 0))
    return pl.pallas_call(
        body, name=name, grid=(rows // t,), in_specs=[pl.BlockSpec((N_DEV, t, cols), lambda i: (0, i, 0)), row, row, row],
        out_specs=[row] * 4, out_shape=[jax.ShapeDtypeStruct((rows, cols), F32)] * 4,
        compiler_params=_CP(dimension_semantics=("parallel",)),
    )(parts, w, m, v)


def _train_step(x, positions, target, wts, ms, vs):
    small_shapes = [wts[n].shape for n in SMALL]
    repl_shapes = [wts[n].shape for n in REPL]

    big_of = {tag: [n for n in BIG if n.startswith(tag)] for tag in ("mla", "gla", "lru", "ssd")}
    full = {n: wts[n] for n in REPL}

    def gather_comm(names, extra=()):
        return GatherComm([wts[n].astype(BF16) for n in names] + list(extra))

    def assemble(names, got):
        for k, n in enumerate(names):
            full[n] = jnp.concatenate([got[k][j] for j in range(N_CHIPS)], axis=_shard_axis(n))

    got = _run_comm("gather_first", gather_comm(big_of["mla"], [_pack([wts[n] for n in SMALL], F32, SMALL_ROWS)]))
    assemble(big_of["mla"], got)
    per_chip_small = [_unpack(got[-1][j], small_shapes) for j in range(N_CHIPS)]
    for k, n in enumerate(SMALL):
        full[n] = jnp.concatenate([per_chip_small[j][k] for j in range(N_CHIPS)], axis=_shard_axis(n))

    cos, sin = _rope_tables(positions)
    ng = full["norm_g"]
    h1, b0, got = _mla_layer(x, ng[0], full, cos, sin, fwd_comm=gather_comm(big_of["gla"] + big_of["lru"]))
    assemble(big_of["gla"] + big_of["lru"], got)
    h2, b1, got = _gla_layer(h1, ng[1], full, fwd_comm=gather_comm(big_of["ssd"]))
    assemble(big_of["ssd"], got)
    h3, b2 = _lru_layer(h2, ng[2], full)
    h4, b3 = _ssd_layer(h3, ng[3], full)
    loss, dh, d_final = _loss_op(h4, target, full["final_g"])
    loss = loss[0, 0]
    grads = {"final_g": d_final.reshape(-1)}
    d_norms = [None] * 4
    for i, b in ((3, b3), (2, b2), (1, b1)):
        dh, d_norms[i], gw = b(dh)
        grads.update(gw)

    def shards(n, dtype):
        return jnp.stack([p.astype(dtype) for p in jnp.split(grads[n], N_CHIPS, axis=_shard_axis(n))])

    early = big_of["ssd"] + big_of["lru"] + big_of["gla"]
    dx, d_norms[0], gw, early_parts = b0(dh, comm=ExchangeComm([shards(n, BF16) for n in early]))
    grads.update(gw)
    grads["norm_g"] = jnp.concatenate(d_norms, axis=0)
    psmall = jnp.stack([_pack([jnp.split(grads[n], N_CHIPS, axis=_shard_axis(n))[j] for n in SMALL], F32, SMALL_ROWS) for j in range(N_CHIPS)])
    prepl = _pack([grads[n] for n in REPL], F32, SMALL_ROWS)
    late_parts = _run_comm("exchange_last", ExchangeComm([shards(n, BF16) for n in big_of["mla"]] + [psmall], [prepl]))
    parts = dict(zip(early, early_parts))
    parts.update(zip(big_of["mla"], late_parts))

    out = {}
    kinds = ("grad", "delta", "new_m", "new_v")
    for n in BIG:
        for kind, a in zip(kinds, _adamw("adam_" + n, parts[n], wts[n], ms[n], vs[n])):
            out[kind, n] = a
    for tag, names, p, shapes in (("adam_small", SMALL, late_parts[-2], small_shapes), ("adam_repl", REPL, late_parts[-1], repl_shapes)):
        packed = [_pack([d[n] for n in names], F32, SMALL_ROWS) for d in (wts, ms, vs)]
        for kind, buf in zip(kinds, _adamw(tag, p, *packed)):
            for n, a in zip(names, _unpack(buf, shapes)):
                out[kind, n] = a
    loss = lax.psum(loss, ("x", "y", "c"))
    return loss, dx, out


def kernel(x, positions, norm_g, final_g, mla_w_in, mla_g_q, mla_w_uq, mla_g_kv, mla_w_ukv, mla_w_out, gla_w_in, gla_w_gk2, gla_b_gk, gla_g_o, gla_w_out, lru_w_in, lru_conv_w, lru_conv_b, lru_w_a, lru_b_a, lru_w_x, lru_b_x, lru_lam, lru_w_out, ssd_w_in, ssd_conv_w, ssd_conv_b, ssd_dt_bias, ssd_a_log, ssd_d, ssd_g_norm, ssd_w_out, loss_target, m_norm_g, m_final_g, m_mla_w_in, m_mla_g_q, m_mla_w_uq, m_mla_g_kv, m_mla_w_ukv, m_mla_w_out, m_gla_w_in, m_gla_w_gk2, m_gla_b_gk, m_gla_g_o, m_gla_w_out, m_lru_w_in, m_lru_conv_w, m_lru_conv_b, m_lru_w_a, m_lru_b_a, m_lru_w_x, m_lru_b_x, m_lru_lam, m_lru_w_out, m_ssd_w_in, m_ssd_conv_w, m_ssd_conv_b, m_ssd_dt_bias, m_ssd_a_log, m_ssd_d, m_ssd_g_norm, m_ssd_w_out, v_norm_g, v_final_g, v_mla_w_in, v_mla_g_q, v_mla_w_uq, v_mla_g_kv, v_mla_w_ukv, v_mla_w_out, v_gla_w_in, v_gla_w_gk2, v_gla_b_gk, v_gla_g_o, v_gla_w_out, v_lru_w_in, v_lru_conv_w, v_lru_conv_b, v_lru_w_a, v_lru_b_a, v_lru_w_x, v_lru_b_x, v_lru_lam, v_lru_w_out, v_ssd_w_in, v_ssd_conv_w, v_ssd_conv_b, v_ssd_dt_bias, v_ssd_a_log, v_ssd_d, v_ssd_g_norm, v_ssd_w_out):
    given = dict(locals())
    stacked = [n for n in WEIGHTS if n not in ("norm_g", "final_g")]

    def blocks(prefix):
        return {n: (given[prefix + n][0] if n in stacked else given[prefix + n]) for n in WEIGHTS}

    loss, dx, out = _train_step(x[0], positions[0], loss_target[0], blocks(""), blocks("m_"), blocks("v_"))
    res = [loss, dx[None]]
    for kind in ("grad", "delta", "new_m", "new_v"):
        res += [(out[kind, n][None] if n in stacked else out[kind, n]) for n in WEIGHTS]
    return tuple(res)
```

```python
import functools
import math

import jax
import jax.numpy as jnp
from jax import lax
from jax.experimental import pallas as pl
from jax.experimental.pallas import tpu as pltpu

F32 = jnp.float32
BF16 = jnp.bfloat16

V7X_VMEM_BYTES = 64 * 1024 * 1024
VMEM_LIMIT = V7X_VMEM_BYTES - 8 * 1024 * 1024
LANE = 128

D_MODEL = 1024
NORM_EPS = 1e-6
MLA_HEADS, MLA_Q_RANK, MLA_KV_RANK, MLA_NOPE, MLA_ROPE, MLA_V = 16, 384, 256, 64, 32, 64
MLA_QK = MLA_NOPE + MLA_ROPE
ROPE_THETA = 10000.0
GLA_HEADS, GLA_DK, GLA_DV, GLA_RANK, GLA_TAU, GLA_CHUNK = 4, 128, 256, 16, 16.0, 64
LRU_WIDTH, LRU_BLOCKS, LRU_BLOCK, LRU_C, CONV_W = 1280, 10, 128, 8.0, 4
SSD_INNER, SSD_P, SSD_HEADS, SSD_GROUPS, SSD_HPG, SSD_STATE, SSD_CHUNK = 2048, 64, 32, 8, 4, 128, 64
ADAM_LR, ADAM_B1, ADAM_B2, ADAM_EPS, ADAM_WD, ADAM_STEP = 0.001, 0.9, 0.999, 1e-08, 0.01, 10

_CP = functools.partial(pltpu.CompilerParams, vmem_limit_bytes=VMEM_LIMIT)


def _bdot(a, b):
    return jnp.dot(a.astype(BF16), b.astype(BF16), preferred_element_type=F32)


def _bdot_nt(a, b):
    return lax.dot_general(a.astype(BF16), b.astype(BF16), (((1,), (1,)), ((), ())), preferred_element_type=F32)


def _bdot_tn(a, b):
    return lax.dot_general(a.astype(BF16), b.astype(BF16), (((0,), (0,)), ((), ())), preferred_element_type=F32)


def _hdot(a, b):
    return jnp.dot(a, b, preferred_element_type=F32, precision=lax.Precision.HIGHEST)


def _hdot_nt(a, b):
    return lax.dot_general(a, b, (((1,), (1,)), ((), ())), preferred_element_type=F32, precision=lax.Precision.HIGHEST)


def _hdot_tn(a, b):
    return lax.dot_general(a, b, (((0,), (0,)), ((), ())), preferred_element_type=F32, precision=lax.Precision.HIGHEST)


def _tri(n):
    r = lax.broadcasted_iota(jnp.int32, (n, n), 0)
    c = lax.broadcasted_iota(jnp.int32, (n, n), 1)
    return r >= c


def _rms(x, g):
    return x * lax.rsqrt(jnp.mean(x * x, axis=-1, keepdims=True) + NORM_EPS) * g


def _silu(x):
    return x * jax.nn.sigmoid(x)


def _shift_rows(x, prev, j):
    if j == 0:
        return x
    t = x.shape[0]

    def fwd_impl(x, prev):
        row = lax.broadcasted_iota(jnp.int32, x.shape, 0)
        return jnp.where(row >= j, pltpu.roll(x, j, 0), pltpu.roll(prev, j, 0))

    @jax.custom_vjp
    def sh(x, prev):
        return fwd_impl(x, prev)

    def sh_fwd(x, prev):
        return fwd_impl(x, prev), None

    def sh_bwd(_, gy):
        row = lax.broadcasted_iota(jnp.int32, gy.shape, 0)
        back = pltpu.roll(gy, t - j, 0)
        return jnp.where(row < t - j, back, 0.0), jnp.where(row >= t - j, back, 0.0)

    sh.defvjp(sh_fwd, sh_bwd)
    return sh(x, prev)


def _one_minus_exp(x):
    series = -x * (1.0 + x * (0.5 + x * (1.0 / 6.0 + x * (1.0 / 24.0 + x * (1.0 / 120.0)))))
    return jnp.where(x > -0.05, series, 1.0 - jnp.exp(x))


def _tile(n, cap):
    if n <= cap:
        return n
    best = None
    for t in range(LANE, cap + 1, LANE):
        if n % t == 0:
            best = t
    assert best is not None, (n, cap)
    return best


MM_TM, MM_TN, MM_TK = 1024, 512, 2048


def _mm(name, a, b, *, ta=False, tb=False, add=None, out_dtype=F32):
    m, k = (a.shape[1], a.shape[0]) if ta else a.shape
    n, kb = (b.shape[0], b.shape[1]) if tb else (b.shape[1], b.shape[0])
    assert k == kb, (name, a.shape, b.shape, ta, tb)
    tm, tn, tk = _tile(m, MM_TM), _tile(n, MM_TN), _tile(k, MM_TK)
    nk = k // tk
    dn = (((0 if ta else 1,), (1 if tb else 0,)), ((), ()))
    has_add = add is not None

    def finish(refs, r):
        if has_add:
            r = r + refs[2][...].astype(F32)
        return r.astype(out_dtype)

    def body_one(*refs):
        a_ref, b_ref, o_ref = refs[0], refs[1], refs[-1]
        o_ref[...] = finish(refs, lax.dot_general(a_ref[...].astype(BF16), b_ref[...].astype(BF16), dn, preferred_element_type=F32))

    def body_acc(*refs):
        a_ref, b_ref = refs[0], refs[1]
        o_ref, acc = refs[-2], refs[-1]
        kk = pl.program_id(2)

        @pl.when(kk == 0)
        def _():
            acc[...] = jnp.zeros(acc.shape, F32)

        acc[...] += lax.dot_general(a_ref[...].astype(BF16), b_ref[...].astype(BF16), dn, preferred_element_type=F32)

        @pl.when(kk == nk - 1)
        def _():
            o_ref[...] = finish(refs, acc[...])

    a_spec = pl.BlockSpec((tk, tm), lambda i, j, q: (q, i)) if ta else pl.BlockSpec((tm, tk), lambda i, j, q: (i, q))
    b_spec = pl.BlockSpec((tn, tk), lambda i, j, q: (j, q)) if tb else pl.BlockSpec((tk, tn), lambda i, j, q: (q, j))
    o_spec = pl.BlockSpec((tm, tn), lambda i, j, q: (i, j))
    in_specs, args = [a_spec, b_spec], [a, b]
    if has_add:
        in_specs.append(o_spec)
        args.append(add)
    return pl.pallas_call(
        body_one if nk == 1 else body_acc, name=name, grid=(m // tm, n // tn, nk), in_specs=in_specs, out_specs=o_spec,
        out_shape=jax.ShapeDtypeStruct((m, n), out_dtype), scratch_shapes=[] if nk == 1 else [pltpu.VMEM((tm, tn), F32)],
        compiler_params=_CP(dimension_semantics=("parallel", "parallel", "arbitrary")),
    )(*args)


class In:
    def __init__(self, arr, block, imap, kind="x", per_h=False, gdtype=F32):
        self.arr, self.block, self.imap, self.kind, self.per_h, self.gdtype = arr, tuple(block), imap, kind, per_h, gdtype

    def spec(self, rev_g=None):
        imap = self.imap
        if rev_g is None:
            return pl.BlockSpec(self.block, lambda h, g: imap(h, g))
        return pl.BlockSpec(self.block, lambda h, g: imap(h, rev_g - 1 - g))


class Out:
    def __init__(self, shape, dtype, block, imap):
        self.shape, self.dtype, self.block, self.imap = tuple(shape), dtype, tuple(block), imap

    def spec(self, rev_g=None):
        imap = self.imap
        if rev_g is None:
            return pl.BlockSpec(self.block, lambda h, g: imap(h, g))
        return pl.BlockSpec(self.block, lambda h, g: imap(h, rev_g - 1 - g))


def _load_f32(ref):
    v = ref[...]
    return v.astype(F32) if jnp.issubdtype(v.dtype, jnp.floating) else v


def _state_out(grid, shape):
    nd = len(shape)
    return Out(tuple(grid) + tuple(shape), F32, (None, None) + tuple(shape), lambda h, g: (h, g) + (0,) * nd)


def _carry(comm, grid, refs, n_in, n_out, n_scr):
    n_c = len(comm.ops) if comm is not None else 0
    n_s = len(comm.sem_shapes) if comm is not None else 0
    p = 0
    in_refs = refs[p:p + n_in]; p += n_in
    c_src = refs[p:p + n_c]; p += n_c
    out_refs = refs[p:p + n_out]; p += n_out
    c_dst = refs[p:p + n_c]; p += n_c
    scr = refs[p:p + n_scr]; p += n_scr
    c_sem = refs[p:p + n_s]
    first = jnp.logical_and(pl.program_id(0) == 0, pl.program_id(1) == 0)
    last = jnp.logical_and(pl.program_id(0) == grid[0] - 1, pl.program_id(1) == grid[1] - 1)
    return in_refs, out_refs, scr, (c_src, c_dst, c_sem), first, last


def _carry_specs(comm):
    if comm is None:
        return [], [], [], [], []
    any_spec = pl.BlockSpec(memory_space=pl.ANY)
    n = len(comm.ops)
    return [any_spec] * n, list(comm.ops), [any_spec] * n, list(comm.out_shapes), list(comm.sem_shapes)


def _op_fwd(name, f, grid, ins, outs, state_shapes=(), comm=None):
    n_in, n_out, n_st = len(ins), len(outs), len(state_shapes)
    st_outs = [_state_out(grid, s) for s in state_shapes]

    def body(*refs):
        in_refs, o_refs, st_scr, cargs, first, last = _carry(comm, grid, refs, n_in, n_out + n_st, n_st)
        out_refs, sv_refs = o_refs[:n_out], o_refs[n_out:]
        if comm is not None:
            @pl.when(first)
            def _():
                comm.start(*cargs)
        g = pl.program_id(1)
        if n_st:
            @pl.when(g == 0)
            def _():
                for s in st_scr:
                    s[...] = jnp.zeros(s.shape, F32)
        vals = [_load_f32(r) for r in in_refs]
        sts = [s[...] for s in st_scr]
        o, ns = f(g, vals, sts)
        for r, v in zip(out_refs, o):
            r[...] = v.astype(r.dtype)
        for r, s in zip(sv_refs, sts):
            r[...] = s
        for s, v in zip(st_scr, ns):
            s[...] = v
        if comm is not None:
            @pl.when(last)
            def _():
                comm.finish(*cargs)

    all_outs = list(outs) + st_outs
    c_in_specs, c_args, c_out_specs, c_out_shapes, c_sems = _carry_specs(comm)
    res = pl.pallas_call(
        body, name=name, grid=tuple(grid), in_specs=[i.spec() for i in ins] + c_in_specs,
        out_specs=[o.spec() for o in all_outs] + c_out_specs,
        out_shape=[jax.ShapeDtypeStruct(o.shape, o.dtype) for o in all_outs] + c_out_shapes,
        scratch_shapes=[pltpu.VMEM(tuple(s), F32) for s in state_shapes] + c_sems,
        compiler_params=_CP(dimension_semantics=("arbitrary", "arbitrary")),
    )(*[i.arr for i in ins], *c_args)
    n_all = n_out + n_st
    return list(res[:n_out]), list(res[n_out:n_all]), list(res[n_all:])


def _op_bwd(name, f, grid, ins, outs, state_shapes, saved, douts, addto=None, comm=None):
    n_in, n_out, n_st = len(ins), len(outs), len(state_shapes)
    n_g = grid[1]
    addto = addto or {}
    diff = [k for k, i in enumerate(ins) if i.kind in ("x", "p")]
    add_idx = sorted(addto)
    st_ins = [In(s, o.block, o.imap, "c") for s, o in zip(saved, [_state_out(grid, s) for s in state_shapes])]
    dout_ins = [In(d, o.block, o.imap, "c") for d, o in zip(douts, outs)]
    add_ins = [In(addto[k], ins[k].block, ins[k].imap, "c") for k in add_idx]
    g_outs = []
    for k in diff:
        i = ins[k]
        if i.kind == "x":
            g_outs.append(Out(i.arr.shape, i.gdtype, i.block, i.imap))
        else:
            g_outs.append(Out(i.arr.shape, F32, i.block, i.imap))

    def body(*refs):
        all_in, go_refs, ds_scr, cargs, first_step, last_step = _carry(comm, grid, refs, n_in + n_st + n_out + len(add_idx), len(diff), n_st)
        if comm is not None:
            @pl.when(first_step)
            def _():
                comm.start(*cargs)
        p = 0
        in_refs = all_in[p:p + n_in]; p += n_in
        sv_refs = all_in[p:p + n_st]; p += n_st
        do_refs = all_in[p:p + n_out]; p += n_out
        ad_refs = all_in[p:p + len(add_idx)]
        hh = pl.program_id(0)
        step = pl.program_id(1)
        g = n_g - 1 - step
        if n_st:
            @pl.when(step == 0)
            def _():
                for s in ds_scr:
                    s[...] = jnp.zeros(s.shape, F32)
        vals = [_load_f32(r) for r in in_refs]
        sts = [r[...] for r in sv_refs]

        def fw(dvals, states):
            full = list(vals)
            for k, v in zip(diff, dvals):
                full[k] = v
            o, ns = f(g, full, states)
            return list(o), list(ns)

        _, vjp = jax.vjp(fw, [vals[k] for k in diff], sts)
        cts = [r[...].astype(F32) for r in do_refs]
        dns = [s[...] for s in ds_scr]
        dvals, dsts = vjp((cts, dns))
        adds = dict(zip(add_idx, ad_refs))
        for k, r, dv in zip(diff, go_refs, dvals):
            i = ins[k]
            if i.kind == "x":
                if k in adds:
                    dv = dv + adds[k][...].astype(F32)
                r[...] = dv.astype(r.dtype)
            else:
                first = (step == 0) if i.per_h else jnp.logical_and(step == 0, hh == 0)

                @pl.when(first)
                def _(r=r, dv=dv):
                    r[...] = dv

                @pl.when(jnp.logical_not(first))
                def _(r=r, dv=dv):
                    r[...] += dv
        for s, v in zip(ds_scr, dsts):
            s[...] = v
        if comm is not None:
            @pl.when(last_step)
            def _():
                comm.finish(*cargs)

    all_ins = list(ins) + st_ins + dout_ins + add_ins
    c_in_specs, c_args, c_out_specs, c_out_shapes, c_sems = _carry_specs(comm)
    res = pl.pallas_call(
        body, name=name, grid=tuple(grid), in_specs=[i.spec(n_g) for i in all_ins] + c_in_specs,
        out_specs=[o.spec(n_g) for o in g_outs] + c_out_specs,
        out_shape=[jax.ShapeDtypeStruct(o.shape, o.dtype) for o in g_outs] + c_out_shapes,
        scratch_shapes=[pltpu.VMEM(tuple(s), F32) for s in state_shapes] + c_sems,
        compiler_params=_CP(dimension_semantics=("arbitrary", "arbitrary")),
    )(*[i.arr for i in all_ins], *c_args)
    return list(res[:len(g_outs)]), list(res[len(g_outs):])


class Op:
    def __init__(self, name, f, grid, ins, outs, state_shapes=()):
        self.name, self.f, self.grid, self.ins, self.outs, self.state_shapes = name, f, grid, ins, outs, state_shapes
        self.saved = None

    def fwd(self, comm=None):
        res, self.saved, self.fwd_comm_out = _op_fwd(self.name + "_fwd", self.f, self.grid, self.ins, self.outs, self.state_shapes, comm)
        return res

    def bwd(self, douts, addto=None, comm=None):
        res, self.bwd_comm_out = _op_bwd(self.name + "_bwd", self.f, self.grid, self.ins, self.outs, self.state_shapes, self.saved, douts,
                                         addto, comm)
        return res


def _rows(arr, t, kind="x", gdtype=F32):
    return In(arr, (t, arr.shape[1]), lambda h, g: (g, 0), kind, gdtype=gdtype)


def _whole(arr, kind="p"):
    nd = arr.ndim
    return In(arr, arr.shape, lambda h, g: (0,) * nd, kind)


def _rows_out(s, n, t, dtype):
    return Out((s, n), dtype, (t, n), lambda h, g: (g, 0))


ROW_T = 256


def _rms_op(name, x, gain, out_dtype=BF16, gdtype=F32):
    s, n = x.shape

    def f(g, vals, sts):
        return [_rms(vals[0], vals[1])], []

    return Op(name, f, (1, s // ROW_T), [_rows(x, ROW_T, gdtype=gdtype), _whole(gain.reshape(1, n))], [_rows_out(s, n, ROW_T, out_dtype)])


def _mla_prep_op(qn, q1, q2, kn, kr, v, cos, sin):
    s = qn.shape[0]
    hd, half = MLA_HEADS, MLA_ROPE // 2

    def f(g, vals, sts):
        qn, q1, q2, kn, kr, v, cos, sin = vals
        cos_h, sin_h = jnp.tile(cos, (1, hd)), jnp.tile(sin, (1, hd))
        r1 = q1 * cos_h - q2 * sin_h
        r2 = q2 * cos_h + q1 * sin_h
        k1, k2 = kr[:, 0:half], kr[:, half:2 * half]
        kr1 = k1 * cos - k2 * sin
        kr2 = k2 * cos + k1 * sin
        zpad = jnp.zeros((qn.shape[0], LANE - MLA_QK), F32)
        qs, ks, vs = [], [], []
        for h in range(hd):
            a, b = h * MLA_NOPE, (h + 1) * MLA_NOPE
            c, d = h * half, (h + 1) * half
            qs.append(jnp.concatenate([qn[:, a:b], r1[:, c:d], r2[:, c:d], zpad], axis=1))
            ks.append(jnp.concatenate([kn[:, a:b], kr1, kr2, zpad], axis=1))
            vs.append(v[:, a:b])
        return [jnp.stack(qs, 0), jnp.stack(ks, 0), jnp.stack(vs, 0)], []

    ins = [_rows(qn, ROW_T, gdtype=BF16), _rows(q1, ROW_T, gdtype=BF16), _rows(q2, ROW_T, gdtype=BF16), _rows(kn, ROW_T, gdtype=BF16),
           _rows(kr, ROW_T, gdtype=BF16), _rows(v, ROW_T, gdtype=BF16), _rows(cos, ROW_T, "c"), _rows(sin, ROW_T, "c")]
    outs = [Out((hd, s, LANE), BF16, (hd, ROW_T, LANE), lambda h, g: (0, g, 0)),
            Out((hd, s, LANE), BF16, (hd, ROW_T, LANE), lambda h, g: (0, g, 0)),
            Out((hd, s, MLA_V), BF16, (hd, ROW_T, MLA_V), lambda h, g: (0, g, 0))]
    return Op("mla_prep", f, (1, s // ROW_T), ins, outs)


ATT_TQ = 256


def _mla_attn_op(q, k, v):
    hd, s, _ = q.shape
    scale = MLA_QK ** -0.5

    def f(g, vals, sts):
        q, k, v = vals
        sc = _bdot_nt(q, k) * scale
        r = lax.broadcasted_iota(jnp.int32, sc.shape, 0) + g * ATT_TQ
        c = lax.broadcasted_iota(jnp.int32, sc.shape, 1)
        sc = jnp.where(r >= c, sc, -1e30)
        m = lax.stop_gradient(jnp.max(sc, axis=-1, keepdims=True))
        p = jnp.exp(sc - m)
        p = p / jnp.sum(p, axis=-1, keepdims=True)
        return [_bdot(p, v)], []

    ins = [In(q, (None, ATT_TQ, LANE), lambda h, g: (h, g, 0), "x"),
           In(k, (None, s, LANE), lambda h, g: (h, 0, 0), "p", per_h=True),
           In(v, (None, s, MLA_V), lambda h, g: (h, 0, 0), "p", per_h=True)]
    outs = [Out((hd, s, MLA_V), F32, (None, ATT_TQ, MLA_V), lambda h, g: (h, g, 0))]
    return Op("mla_attn", f, (hd, s // ATT_TQ), ins, outs)


def _mla_post_op(o, gate):
    hd, s, _ = o.shape

    def f(g, vals, sts):
        o, gate = vals
        cat = jnp.concatenate([o[h] for h in range(hd)], axis=1)
        return [cat * _silu(gate)], []

    ins = [In(o, (hd, ROW_T, MLA_V), lambda h, g: (0, g, 0), "x"), _rows(gate, ROW_T, gdtype=BF16)]
    return Op("mla_post", f, (1, s // ROW_T), ins, [_rows_out(s, hd * MLA_V, ROW_T, BF16)])


def _gla_gate_op(gk, w2, b):
    s = gk.shape[0]

    def f(g, vals, sts):
        gk, w2, b = vals
        return [jax.nn.log_sigmoid(_bdot(gk, w2) + b) / GLA_TAU], []

    ins = [_rows(gk, ROW_T, gdtype=BF16), _whole(w2), _whole(b)]
    return Op("gla_gate", f, (1, s // ROW_T), ins, [_rows_out(s, GLA_HEADS * GLA_DK, ROW_T, F32)])


def _gla_core_op(q, k, v, gate, la, g_o):
    s = q.shape[0]
    c = GLA_CHUNK

    def f(g, vals, sts):
        q, k, v, gate, la, g_o = vals
        st = sts[0]
        tri = _tri(c)
        b = _hdot(tri.astype(F32), la)
        b_last = jnp.sum(la, axis=0, keepdims=True)
        qt = q * (GLA_DK ** -0.5) * jnp.exp(b)
        kt = k * jnp.exp(-b)
        att = jnp.where(tri, _bdot_nt(qt, kt), 0.0)
        o = _bdot(att, v) + _bdot(qt, st)
        d_state = _bdot_tn(k * jnp.exp(b_last - b), v)
        b_last_col = _hdot_tn(la, jnp.ones((c, 1), F32))
        new_st = jnp.exp(b_last_col) * st + d_state
        y = _rms(o, g_o) * _silu(gate)
        return [y], [new_st]

    hk = lambda h, g: (g, h)
    ins = [In(q, (c, GLA_DK), hk, "x", gdtype=BF16), In(k, (c, GLA_DK), hk, "x", gdtype=BF16), In(v, (c, GLA_DV), hk, "x", gdtype=BF16),
           In(gate, (c, GLA_DV), hk, "x", gdtype=BF16), In(la, (c, GLA_DK), hk, "x"), _whole(g_o)]
    outs = [Out((s, GLA_HEADS * GLA_DV), BF16, (c, GLA_DV), hk)]
    return Op("gla_core", f, (GLA_HEADS, s // c), ins, outs, [(GLA_DK, GLA_DV)])


LRU_T = 256


def _lru_op(gate, u, conv_w, conv_b, w_a, b_a, w_x, b_x, lam):
    s, w = u.shape
    t = LRU_T

    def f(g, vals, sts):
        gate, u, cw, cb, w_a, b_a, w_x, b_x, lam = vals
        u_prev, h_prev = sts
        uc = cb
        for kk in range(CONV_W):
            uc = uc + cw[kk] * _shift_rows(u, u_prev, CONV_W - 1 - kk)
        ra, ri = [], []
        for n in range(LRU_BLOCKS):
            blk = uc[:, n * LRU_BLOCK:(n + 1) * LRU_BLOCK]
            ra.append(_bdot(blk, w_a[n]))
            ri.append(_bdot(blk, w_x[n]))
        r = jax.nn.sigmoid(jnp.concatenate(ra, axis=1) + b_a)
        i = jax.nn.sigmoid(jnp.concatenate(ri, axis=1) + b_x)
        log_a = -LRU_C * r * jax.nn.softplus(-lam)
        a = jnp.exp(log_a)
        bb = jnp.sqrt(_one_minus_exp(2.0 * log_a)) * (i * uc)
        zero = jnp.zeros_like(a)
        sh = 1
        while sh < t:
            a_s = _shift_rows(a - 1.0, zero, sh) + 1.0
            b_s = _shift_rows(bb, zero, sh)
            bb = a * b_s + bb
            a = a * a_s
            sh *= 2
        hs = bb + a * h_prev
        last = (lax.broadcasted_iota(jnp.int32, hs.shape, 0) == t - 1).astype(F32)
        h_last = jnp.sum(hs * last, axis=0, keepdims=True)
        return [hs * _silu(gate)], [u, h_last]

    ins = [_rows(gate, t, gdtype=BF16), _rows(u, t, gdtype=BF16), _whole(conv_w), _whole(conv_b), _whole(w_a), _whole(b_a), _whole(w_x),
           _whole(b_x), _whole(lam)]
    return Op("lru_core", f, (1, s // t), ins, [_rows_out(s, w, t, BF16)], [(t, w), (1, w)])


def _ssd_conv_op(xbc, conv_w, conv_b):
    s, w = xbc.shape
    t = ROW_T
    n_x, n_b = SSD_INNER, SSD_GROUPS * SSD_STATE

    def f(g, vals, sts):
        xbc, cw, cb = vals
        acc = cb
        for kk in range(CONV_W):
            acc = acc + cw[kk] * _shift_rows(xbc, sts[0], CONV_W - 1 - kk)
        y = _silu(acc)
        return [y[:, :n_x], y[:, n_x:n_x + n_b], y[:, n_x + n_b:]], [xbc]

    ins = [_rows(xbc, t, gdtype=BF16), _whole(conv_w), _whole(conv_b)]
    outs = [_rows_out(s, n_x, t, F32), _rows_out(s, n_b, t, F32), _rows_out(s, n_b, t, F32)]
    return Op("ssd_conv", f, (1, s // t), ins, outs, [(t, w)])


SSD_L = 256


def _ssd_core_op(x, bm, cm, z, dt, dt_bias, a_log, d_skip, g_norm):
    s = x.shape[0]
    c, hg, p = SSD_L, SSD_HPG, SSD_P
    gw = hg * p

    def f(g, vals, sts):
        x, bm, cm, z, dtr, dt_bias, a_log, d_skip, g_norm = vals
        st = sts[0]
        tri = _tri(c)
        dt = jax.nn.softplus(dtr + dt_bias)
        da = dt * (-jnp.exp(a_log))
        cs = _hdot(tri.astype(F32), da)
        cs_last = jnp.sum(da, axis=0, keepdims=True)
        cb = _bdot_nt(cm, bm)
        lane = lax.broadcasted_iota(jnp.int32, (c, hg), 1)
        ys, new_st = [], []
        for h in range(hg):
            cs_h = cs[:, h:h + 1]
            cs_row = _hdot_nt((lane == h).astype(F32), cs)
            seg = jnp.where(tri, cs_h - cs_row, 0.0)
            lmat = jnp.where(tri, jnp.exp(seg), 0.0)
            x_h = x[:, h * p:(h + 1) * p]
            xdt = x_h * dt[:, h:h + 1]
            y_diag = _bdot(cb * lmat, xdt)
            decay = jnp.exp(cs_last[:, h:h + 1] - cs_h)
            states = _bdot_tn(xdt * decay, bm)
            y_off = _bdot_nt(cm, st[h]) * jnp.exp(cs_h)
            new_st.append(jnp.exp(cs_last[:, h:h + 1]) * st[h] + states)
            ys.append(y_diag + y_off + d_skip[:, h:h + 1] * x_h)
        y = jnp.concatenate(ys, axis=1) * _silu(z)
        return [_rms(y, g_norm)], [jnp.stack(new_st, 0)]

    ins = [In(x, (c, gw), lambda h, g: (g, h), "x"), In(bm, (c, SSD_STATE), lambda h, g: (g, h), "x"),
           In(cm, (c, SSD_STATE), lambda h, g: (g, h), "x"), In(z, (c, gw), lambda h, g: (g, h), "x", gdtype=BF16),
           In(dt, (None, c, hg), lambda h, g: (h, g, 0), "x"),
           In(dt_bias, (None, 1, hg), lambda h, g: (h, 0, 0), "p", per_h=True),
           In(a_log, (None, 1, hg), lambda h, g: (h, 0, 0), "p", per_h=True),
           In(d_skip, (None, 1, hg), lambda h, g: (h, 0, 0), "p", per_h=True),
           In(g_norm, (1, gw), lambda h, g: (0, h), "p", per_h=True)]
    outs = [Out((s, SSD_INNER), BF16, (c, gw), lambda h, g: (g, h))]
    return Op("ssd_core", f, (SSD_GROUPS, s // c), ins, outs, [(hg, p, SSD_STATE)])


def _loss_op(h, target, final_g):
    s, n = h.shape
    t = ROW_T
    n_g = s // t

    def body(h_ref, t_ref, g_ref, loss_ref, dh_ref, dg_ref):
        step = pl.program_id(0)

        def lossf(hv, gv):
            err = _rms(hv, gv) - t_ref[...]
            return 0.5 * jnp.sum(jnp.mean(err * err, axis=-1))

        l, (dh, dg) = jax.value_and_grad(lossf, argnums=(0, 1))(h_ref[...], g_ref[...])
        dh_ref[...] = dh

        @pl.when(step == 0)
        def _():
            loss_ref[...] = jnp.zeros(loss_ref.shape, F32)
            dg_ref[...] = jnp.zeros(dg_ref.shape, F32)

        loss_ref[...] += jnp.full(loss_ref.shape, l, F32)
        dg_ref[...] += dg

    row = pl.BlockSpec((t, n), lambda g: (g, 0))
    one = pl.BlockSpec((1, n), lambda g: (0, 0))
    return pl.pallas_call(
        body, name="loss_head", grid=(n_g,), in_specs=[row, row, one],
        out_specs=[pl.BlockSpec((1, LANE), lambda g: (0, 0)), row, one],
        out_shape=[jax.ShapeDtypeStruct((1, LANE), F32), jax.ShapeDtypeStruct((s, n), F32), jax.ShapeDtypeStruct((1, n), F32)],
        compiler_params=_CP(dimension_semantics=("arbitrary",)),
    )(h, target, final_g.reshape(1, n))


def _pad_cols(w, n):
    return jnp.pad(w, ((0, 0), (0, n - w.shape[1])))


def _pad_rows(w, n):
    return jnp.pad(w, ((0, n - w.shape[0]), (0, 0)))


def _proj_bwd(tag, u, dps, ws):
    du = None
    for i, (dp, w) in enumerate(zip(dps, ws)):
        du = _mm(f"{tag}_du{i}", dp, w, tb=True, add=du)
    dws = [_mm(f"{tag}_dw{i}", u, dp, ta=True) for i, dp in enumerate(dps)]
    return du, dws


def _mla_layer(h, norm_g, w, cos, sin, fwd_comm=None):
    bf = lambda a: a.astype(BF16)
    w_in, w_uq, w_ukv = w["mla_w_in"], w["mla_w_uq"], w["mla_w_ukv"]
    a0, a1, a2 = MLA_Q_RANK, MLA_Q_RANK + MLA_KV_RANK, MLA_Q_RANK + MLA_KV_RANK + MLA_ROPE
    w_cq, w_ckv, w_kr, w_g = bf(w_in[:, :a0]), bf(w_in[:, a0:a1]), bf(_pad_cols(w_in[:, a1:a2], LANE)), bf(w_in[:, a2:])
    uq = w_uq.reshape(MLA_Q_RANK, MLA_HEADS, MLA_QK)
    half = MLA_ROPE // 2
    w_qn = bf(uq[:, :, :MLA_NOPE].reshape(MLA_Q_RANK, -1))
    w_q1 = bf(uq[:, :, MLA_NOPE:MLA_NOPE + half].reshape(MLA_Q_RANK, -1))
    w_q2 = bf(uq[:, :, MLA_NOPE + half:].reshape(MLA_Q_RANK, -1))
    ukv = w_ukv.reshape(MLA_KV_RANK, MLA_HEADS, MLA_NOPE + MLA_V)
    w_kn = bf(ukv[:, :, :MLA_NOPE].reshape(MLA_KV_RANK, -1))
    w_v = bf(ukv[:, :, MLA_NOPE:].reshape(MLA_KV_RANK, -1))
    w_out = bf(w["mla_w_out"])

    n0 = _rms_op("mla_norm", h, norm_g)
    u, = n0.fwd()
    cq, ckv, kr, gate = (_mm(f"mla_in{i}", u, wi) for i, wi in enumerate((w_cq, w_ckv, w_kr, w_g)))
    nq = _rms_op("mla_qnorm", cq, w["mla_g_q"], gdtype=BF16)
    nkv = _rms_op("mla_kvnorm", ckv, w["mla_g_kv"], gdtype=BF16)
    qn_, = nq.fwd()
    kvn_, = nkv.fwd()
    qn, q1, q2 = (_mm(f"mla_uq{i}", qn_, wi) for i, wi in enumerate((w_qn, w_q1, w_q2)))
    kn, v = (_mm(f"mla_ukv{i}", kvn_, wi) for i, wi in enumerate((w_kn, w_v)))
    prep = _mla_prep_op(qn, q1, q2, kn, kr, v, cos, sin)
    qh, kh, vh = prep.fwd()
    attn = _mla_attn_op(qh, kh, vh)
    o, = attn.fwd(fwd_comm)
    post = _mla_post_op(o, gate)
    y, = post.fwd()
    h_out = _mm("mla_out", y, w_out, add=h)

    def bwd(dh, comm=None):
        dy = _mm("mla_out_dy", dh, w_out, tb=True, out_dtype=BF16)
        d_w_out = _mm("mla_out_dw", y, dh, ta=True)
        do, dgate = post.bwd([dy])
        dqh, dkh, dvh = attn.bwd([do], comm=comm)
        dqn, dq1, dq2, dkn, dkr, dv = prep.bwd([dqh, dkh, dvh])
        dqn_, d_uq = _proj_bwd("mla_uq", qn_, (dqn, dq1, dq2), (w_qn, w_q1, w_q2))
        dkvn_, d_ukv = _proj_bwd("mla_ukv", kvn_, (dkn, dv), (w_kn, w_v))
        dcq, d_g_q = nq.bwd([dqn_])
        dckv, d_g_kv = nkv.bwd([dkvn_])
        du, d_in = _proj_bwd("mla_in", u, (dcq, dckv, dkr, dgate), (w_cq, w_ckv, w_kr, w_g))
        dh_in, d_norm = n0.bwd([du], addto={0: dh})
        shp = (MLA_Q_RANK, MLA_HEADS, -1)
        g_uq = jnp.concatenate([d_uq[0].reshape(shp), d_uq[1].reshape(shp), d_uq[2].reshape(shp)], axis=2).reshape(MLA_Q_RANK, -1)
        shp = (MLA_KV_RANK, MLA_HEADS, -1)
        g_ukv = jnp.concatenate([d_ukv[0].reshape(shp), d_ukv[1].reshape(shp)], axis=2).reshape(MLA_KV_RANK, -1)
        g_in = jnp.concatenate([d_in[0], d_in[1], d_in[2][:, :MLA_ROPE], d_in[3]], axis=1)
        return dh_in, d_norm, {"mla_w_in": g_in, "mla_g_q": d_g_q.reshape(-1), "mla_w_uq": g_uq, "mla_g_kv": d_g_kv.reshape(-1),
                               "mla_w_ukv": g_ukv, "mla_w_out": d_w_out}, attn.bwd_comm_out

    return h_out, bwd, attn.fwd_comm_out


def _gla_layer(h, norm_g, w, fwd_comm=None):
    bf = lambda a: a.astype(BF16)
    w_in = w["gla_w_in"]
    nk, nv = GLA_HEADS * GLA_DK, GLA_HEADS * GLA_DV
    cuts = (0, nk, 2 * nk, 2 * nk + nv, 2 * nk + 2 * nv)
    w_q, w_k, w_v, w_g = (bf(w_in[:, cuts[i]:cuts[i + 1]]) for i in range(4))
    w_gk = bf(_pad_cols(w_in[:, cuts[4]:], LANE))
    w2 = _pad_rows(w["gla_w_gk2"], LANE)
    b_gk = w["gla_b_gk"].reshape(1, -1)
    g_o = w["gla_g_o"].reshape(1, -1)
    w_out = bf(w["gla_w_out"])

    n0 = _rms_op("gla_norm", h, norm_g)
    u, = n0.fwd()
    q, k, v, gate, gk = (_mm(f"gla_in{i}", u, wi) for i, wi in enumerate((w_q, w_k, w_v, w_g, w_gk)))
    gop = _gla_gate_op(gk, w2, b_gk)
    la, = gop.fwd()
    core = _gla_core_op(q, k, v, gate, la, g_o)
    y, = core.fwd(fwd_comm)
    h_out = _mm("gla_out", y, w_out, add=h)

    def bwd(dh, comm=None):
        dy = _mm("gla_out_dy", dh, w_out, tb=True, out_dtype=BF16)
        d_w_out = _mm("gla_out_dw", y, dh, ta=True)
        dq, dk, dv, dgate, dla, d_g_o = core.bwd([dy], comm=comm)
        dgk, d_w2, d_b = gop.bwd([dla])
        du, d_in = _proj_bwd("gla_in", u, (dq, dk, dv, dgate, dgk), (w_q, w_k, w_v, w_g, w_gk))
        dh_in, d_norm = n0.bwd([du], addto={0: dh})
        g_in = jnp.concatenate([d_in[0], d_in[1], d_in[2], d_in[3], d_in[4][:, :GLA_RANK]], axis=1)
        return dh_in, d_norm, {"gla_w_in": g_in, "gla_w_gk2": d_w2[:GLA_RANK], "gla_b_gk": d_b.reshape(-1), "gla_g_o": d_g_o.reshape(-1),
                               "gla_w_out": d_w_out}, core.bwd_comm_out

    return h_out, bwd, core.fwd_comm_out


def _lru_layer(h, norm_g, w):
    bf = lambda a: a.astype(BF16)
    w_in = w["lru_w_in"]
    w_g, w_u = bf(w_in[:, :LRU_WIDTH]), bf(w_in[:, LRU_WIDTH:])
    row = lambda a: a.reshape(1, -1)
    w_out = bf(w["lru_w_out"])

    n0 = _rms_op("lru_norm", h, norm_g)
    u_, = n0.fwd()
    gate, u = (_mm(f"lru_in{i}", u_, wi) for i, wi in enumerate((w_g, w_u)))
    core = _lru_op(gate, u, w["lru_conv_w"].reshape(CONV_W, 1, -1), row(w["lru_conv_b"]), w["lru_w_a"], row(w["lru_b_a"]), w["lru_w_x"],
                   row(w["lru_b_x"]), row(w["lru_lam"]))
    y, = core.fwd()
    h_out = _mm("lru_out", y, w_out, add=h)

    def bwd(dh, comm=None):
        dy = _mm("lru_out_dy", dh, w_out, tb=True, out_dtype=BF16)
        d_w_out = _mm("lru_out_dw", y, dh, ta=True)
        dgate, du, d_cw, d_cb, d_wa, d_ba, d_wx, d_bx, d_lam = core.bwd([dy], comm=comm)
        du_, d_in = _proj_bwd("lru_in", u_, (dgate, du), (w_g, w_u))
        dh_in, d_norm = n0.bwd([du_], addto={0: dh})
        return dh_in, d_norm, {"lru_w_in": jnp.concatenate(d_in, axis=1), "lru_conv_w": d_cw.reshape(CONV_W, -1), "lru_conv_b": d_cb.reshape(-1),
                               "lru_w_a": d_wa, "lru_b_a": d_ba.reshape(-1), "lru_w_x": d_wx, "lru_b_x": d_bx.reshape(-1),
                               "lru_lam": d_lam.reshape(-1), "lru_w_out": d_w_out}, core.bwd_comm_out

    return h_out, bwd


def _ssd_layer(h, norm_g, w):
    bf = lambda a: a.astype(BF16)
    s = h.shape[0]
    w_in = w["ssd_w_in"]
    conv_dim = SSD_INNER + 2 * SSD_GROUPS * SSD_STATE
    w_z, w_xbc = bf(w_in[:, :SSD_INNER]), bf(w_in[:, SSD_INNER:SSD_INNER + conv_dim])
    w_dt = bf(_pad_cols(w_in[:, SSD_INNER + conv_dim:], LANE))
    grp = lambda a: a.reshape(SSD_GROUPS, 1, SSD_HPG)
    w_out = bf(w["ssd_w_out"])

    n0 = _rms_op("ssd_norm", h, norm_g)
    u, = n0.fwd()
    z, xbc, dtp = (_mm(f"ssd_in{i}", u, wi) for i, wi in enumerate((w_z, w_xbc, w_dt)))
    conv = _ssd_conv_op(xbc, w["ssd_conv_w"].reshape(CONV_W, 1, -1), w["ssd_conv_b"].reshape(1, -1))
    x, bm, cm = conv.fwd()
    dt = dtp[:, :SSD_HEADS].reshape(s, SSD_GROUPS, SSD_HPG).transpose(1, 0, 2)
    core = _ssd_core_op(x, bm, cm, z, dt, grp(w["ssd_dt_bias"]), grp(w["ssd_a_log"]), grp(w["ssd_d"]), w["ssd_g_norm"].reshape(1, -1))
    y, = core.fwd()
    h_out = _mm("ssd_out", y, w_out, add=h)

    def bwd(dh):
        dy = _mm("ssd_out_dy", dh, w_out, tb=True, out_dtype=BF16)
        d_w_out = _mm("ssd_out_dw", y, dh, ta=True)
        dx, dbm, dcm, dz, ddt, d_dtb, d_alog, d_d, d_gn = core.bwd([dy])
        dxbc, d_cw, d_cb = conv.bwd([dx, dbm, dcm])
        ddtp = _pad_cols(ddt.transpose(1, 0, 2).reshape(s, SSD_HEADS), LANE).astype(BF16)
        du, d_in = _proj_bwd("ssd_in", u, (dz, dxbc, ddtp), (w_z, w_xbc, w_dt))
        dh_in, d_norm = n0.bwd([du], addto={0: dh})
        g_in = jnp.concatenate([d_in[0], d_in[1], d_in[2][:, :SSD_HEADS]], axis=1)
        return dh_in, d_norm, {"ssd_w_in": g_in, "ssd_conv_w": d_cw.reshape(CONV_W, -1), "ssd_conv_b": d_cb.reshape(-1),
                               "ssd_dt_bias": d_dtb.reshape(-1), "ssd_a_log": d_alog.reshape(-1), "ssd_d": d_d.reshape(-1),
                               "ssd_g_norm": d_gn.reshape(-1), "ssd_w_out": d_w_out}

    return h_out, bwd


def _rope_tables(positions):
    inv_freq = ROPE_THETA ** (-jnp.arange(0, MLA_ROPE, 2, dtype=F32) / MLA_ROPE)
    ang = positions.astype(F32)[:, None] * inv_freq
    return jnp.cos(ang), jnp.sin(ang)


WEIGHTS = ["norm_g", "final_g", "mla_w_in", "mla_g_q", "mla_w_uq", "mla_g_kv", "mla_w_ukv", "mla_w_out", "gla_w_in", "gla_w_gk2", "gla_b_gk",
           "gla_g_o", "gla_w_out", "lru_w_in", "lru_conv_w", "lru_conv_b", "lru_w_a", "lru_b_a", "lru_w_x", "lru_b_x", "lru_lam", "lru_w_out",
           "ssd_w_in", "ssd_conv_w", "ssd_conv_b", "ssd_dt_bias", "ssd_a_log", "ssd_d", "ssd_g_norm", "ssd_w_out"]
BIG = ["mla_w_in", "mla_w_uq", "mla_w_ukv", "mla_w_out", "gla_w_in", "gla_w_out", "lru_w_in", "lru_w_out", "ssd_w_in", "ssd_w_out"]
SMALL = ["gla_w_gk2", "gla_b_gk", "gla_g_o", "lru_conv_w", "lru_conv_b", "lru_b_a", "lru_b_x", "lru_lam", "ssd_conv_w", "ssd_conv_b", "ssd_g_norm"]
REPL = ["norm_g", "final_g", "mla_g_q", "mla_g_kv", "lru_w_a", "lru_w_x", "ssd_dt_bias", "ssd_a_log", "ssd_d"]
REPL_EARLY = ["lru_w_a", "lru_w_x"]
REPL_LATE = [n for n in REPL if n not in REPL_EARLY]
N_CHIPS, N_DEV = 4, 8
PACK_W = 1024
ADAM_ROWS = 256
SMALL_ROWS = 64


def _shard_axis(name):
    return 0 if name.endswith("_w_out") else -1


def _pack(arrs, dtype, row_mult):
    flat = jnp.concatenate([a.reshape(-1).astype(dtype) for a in arrs])
    per = PACK_W * row_mult
    total = -(-flat.shape[0] // per) * per
    return jnp.pad(flat, (0, total - flat.shape[0])).reshape(-1, PACK_W)


def _unpack(buf, shapes):
    flat = buf.reshape(-1)
    out, off = [], 0
    for s in shapes:
        n = math.prod(s)
        out.append(flat[off:off + n].reshape(s))
        off += n
    return out


def _mesh_pos():
    return lax.axis_index("x"), lax.axis_index("y"), lax.axis_index("c")


class GatherComm:
    def __init__(self, ops):
        self.ops = list(ops)
        n = len(self.ops)
        assert all(o.ndim == 2 and o.shape[0] % 32 == 0 for o in self.ops), [o.shape for o in self.ops]
        self.out_shapes = [jax.ShapeDtypeStruct((N_CHIPS,) + o.shape, o.dtype) for o in self.ops]
        self.sem_shapes = [pltpu.SemaphoreType.DMA((6 * n,)), pltpu.SemaphoreType.DMA((6 * n,)), pltpu.SemaphoreType.DMA((n,))]

    def _copies(self, srcs, dsts, sems):
        send_sems, recv_sems, local_sems = sems
        n = len(self.ops)
        x, y, c = _mesh_pos()
        me_id, sibling = (x, y, c), (x, y, 1 - c)
        chips = [(1 - x, y), (x, 1 - y), (1 - x, 1 - y)]
        mine = 2 * x + y

        def half(i, cc):
            h = self.ops[i].shape[0] // 2
            return pl.ds(cc * h, h)

        def copy(i, k, src, slot, cc, to):
            return pltpu.make_async_remote_copy(src_ref=src, dst_ref=dsts[i].at[slot, half(i, cc)], send_sem=send_sems.at[i * 6 + k],
                                                recv_sem=recv_sems.at[i * 6 + k], device_id=to, device_id_type=pl.DeviceIdType.MESH)

        local = [pltpu.make_async_copy(srcs[i], dsts[i].at[mine], local_sems.at[i]) for i in range(n)]
        first, ici_recvs, passed, sib_recvs = [], [], [], []
        for i in range(n):
            my_half = srcs[i].at[half(i, c)]
            for k, (px, py) in enumerate(chips):
                slot = 2 * px + py
                first.append(copy(i, k, my_half, mine, c, (px, py, c)))
                ici_recvs.append(copy(i, k, my_half, slot, c, me_id))
                passed.append(copy(i, 3 + k, dsts[i].at[slot, half(i, c)], slot, c, sibling))
                sib_recvs.append(copy(i, 3 + k, my_half, slot, 1 - c, me_id))
        return local, first, ici_recvs, passed, sib_recvs

    def start(self, srcs, dsts, sems):
        local, first, _, _, _ = self._copies(srcs, dsts, sems)
        for cp in local + first:
            cp.start()

    def finish(self, srcs, dsts, sems):
        local, first, ici_recvs, passed, sib_recvs = self._copies(srcs, dsts, sems)
        for rc, fw in zip(ici_recvs, passed):
            rc.wait_recv()
            fw.start()
        for cp in sib_recvs:
            cp.wait_recv()
        for cp in first + passed:
            cp.wait_send()
        for cp in local:
            cp.wait()


class ExchangeComm:
    def __init__(self, chip_ops, all_ops=()):
        self.ops = list(chip_ops) + list(all_ops)
        self.per_chip = (True,) * len(chip_ops) + (False,) * len(all_ops)
        n = len(self.ops)
        self.out_shapes = [jax.ShapeDtypeStruct((N_DEV,) + o.shape[-2:], o.dtype) for o in self.ops]
        self.sem_shapes = [pltpu.SemaphoreType.DMA((7 * n,)), pltpu.SemaphoreType.DMA((7 * n,)), pltpu.SemaphoreType.DMA((n,))]

    def _copies(self, srcs, dsts, sems):
        send_sems, recv_sems, local_sems = sems
        n, per_chip = len(self.ops), self.per_chip
        x, y, c = _mesh_pos()
        me_id, sibling = (x, y, c), (x, y, 1 - c)
        chips = [(1 - x, y), (x, 1 - y), (1 - x, 1 - y)]

        def dev(px, py, pc):
            return 4 * px + 2 * py + pc

        def part(i, px, py):
            return srcs[i].at[2 * px + py] if per_chip[i] else srcs[i]

        def copy(i, k, src, slot, to):
            return pltpu.make_async_remote_copy(src_ref=src, dst_ref=dsts[i].at[slot], send_sem=send_sems.at[i * 7 + k],
                                                recv_sem=recv_sems.at[i * 7 + k], device_id=to, device_id_type=pl.DeviceIdType.MESH)

        me = dev(x, y, c)
        local = [pltpu.make_async_copy(part(i, x, y), dsts[i].at[me], local_sems.at[i]) for i in range(n)]
        first, ici_recvs, passed, sib_recvs = [], [], [], []
        for i in range(n):
            first.append(copy(i, 0, part(i, x, y), me, sibling))
            first += [copy(i, 1 + k, part(i, px, py), me, (px, py, c)) for k, (px, py) in enumerate(chips)]
            sib_recvs.append(copy(i, 0, part(i, x, y), dev(x, y, 1 - c), me_id))
            for k, (px, py) in enumerate(chips):
                slot = dev(px, py, c)
                ici_recvs.append(copy(i, 1 + k, part(i, x, y), slot, me_id))
                passed.append(copy(i, 4 + k, dsts[i].at[slot], slot, sibling))
                sib_recvs.append(copy(i, 4 + k, part(i, x, y), dev(px, py, 1 - c), me_id))
        return local, first, ici_recvs, passed, sib_recvs

    def start(self, srcs, dsts, sems):
        local, first, _, _, _ = self._copies(srcs, dsts, sems)
        for cp in local + first:
            cp.start()

    def finish(self, srcs, dsts, sems):
        local, first, ici_recvs, passed, sib_recvs = self._copies(srcs, dsts, sems)
        for rc, fw in zip(ici_recvs, passed):
            rc.wait_recv()
            fw.start()
        for cp in sib_recvs:
            cp.wait_recv()
        for cp in first + passed:
            cp.wait_send()
        for cp in local:
            cp.wait()


def _run_comm(name, comm):
    n = len(comm.ops)

    def body(*refs):
        srcs, dsts, sems = refs[:n], refs[n:2 * n], refs[2 * n:]
        comm.start(srcs, dsts, sems)
        comm.finish(srcs, dsts, sems)

    any_spec = pl.BlockSpec(memory_space=pl.ANY)
    return pl.pallas_call(body, name=name, in_specs=[any_spec] * n, out_specs=[any_spec] * n, out_shape=comm.out_shapes,
                          scratch_shapes=comm.sem_shapes)(*comm.ops)


def _adamw(name, parts, w, m, v):
    rows, cols = w.shape
    t = next(c for c in (ADAM_ROWS, ADAM_ROWS // 2, SMALL_ROWS) if rows % c == 0)
    c1 = 1.0 - ADAM_B1 ** ADAM_STEP
    c2 = 1.0 - ADAM_B2 ** ADAM_STEP

    def body(p_ref, w_ref, m_ref, v_ref, g_ref, d_ref, nm_ref, nv_ref):
        g = p_ref[0].astype(F32)
        for d in range(1, N_DEV):
            g = g + p_ref[d].astype(F32)
        nm = ADAM_B1 * m_ref[...] + (1.0 - ADAM_B1) * g
        nv = ADAM_B2 * v_ref[...] + (1.0 - ADAM_B2) * (g * g)
        g_ref[...] = g
        nm_ref[...] = nm
        nv_ref[...] = nv
        d_ref[...] = -ADAM_LR * ((nm / c1) / (jnp.sqrt(nv / c2) + ADAM_EPS) + ADAM_WD * w_ref[...])

    row = pl.BlockSpec((t, cols), lambda i: (i, 0))
    return pl.pallas_call(
        body, name=name, grid=(rows // t,), in_specs=[pl.BlockSpec((N_DEV, t, cols), lambda i: (0, i, 0)), row, row, row],
        out_specs=[row] * 4, out_shape=[jax.ShapeDtypeStruct((rows, cols), F32)] * 4,
        compiler_params=_CP(dimension_semantics=("parallel",)),
    )(parts, w, m, v)


def _train_step(x, positions, target, wts, ms, vs):
    small_shapes = [wts[n].shape for n in SMALL]

    big_of = {tag: [n for n in BIG if n.startswith(tag)] for tag in ("mla", "gla", "lru", "ssd")}
    full = {n: wts[n] for n in REPL}

    def gather_comm(names, extra=()):
        return GatherComm([wts[n].astype(BF16) for n in names] + list(extra))

    def assemble(names, got):
        for k, n in enumerate(names):
            full[n] = jnp.concatenate([got[k][j] for j in range(N_CHIPS)], axis=_shard_axis(n))

    got = _run_comm("gather_first", gather_comm(big_of["mla"], [_pack([wts[n] for n in SMALL], F32, SMALL_ROWS)]))
    assemble(big_of["mla"], got)
    per_chip_small = [_unpack(got[-1][j], small_shapes) for j in range(N_CHIPS)]
    for k, n in enumerate(SMALL):
        full[n] = jnp.concatenate([per_chip_small[j][k] for j in range(N_CHIPS)], axis=_shard_axis(n))

    cos, sin = _rope_tables(positions)
    ng = full["norm_g"]
    h1, b0, got = _mla_layer(x, ng[0], full, cos, sin, fwd_comm=gather_comm(big_of["gla"] + big_of["lru"]))
    assemble(big_of["gla"] + big_of["lru"], got)
    h2, b1, got = _gla_layer(h1, ng[1], full, fwd_comm=gather_comm(big_of["ssd"]))
    assemble(big_of["ssd"], got)
    h3, b2 = _lru_layer(h2, ng[2], full)
    h4, b3 = _ssd_layer(h3, ng[3], full)
    loss, dh, d_final = _loss_op(h4, target, full["final_g"])
    loss = loss[0, 0]
    grads = {"final_g": d_final.reshape(-1)}
    d_norms = [None] * 4
    def shards(n, dtype=BF16):
        return jnp.stack([p.astype(dtype) for p in jnp.split(grads[n], N_CHIPS, axis=_shard_axis(n))])

    parts = {}
    dh, d_norms[3], gw = b3(dh)
    grads.update(gw)
    names = ["ssd_w_in"]
    dh, d_norms[2], gw, got = b2(dh, comm=ExchangeComm([shards(n) for n in names]))
    parts.update(zip(names, got))
    grads.update(gw)
    names = ["ssd_w_out", "lru_w_in", "lru_w_out"]
    dh, d_norms[1], gw, got = b1(dh, comm=ExchangeComm([shards(n) for n in names], [_pack([grads[n] for n in REPL_EARLY], F32, SMALL_ROWS)]))
    parts.update(zip(names, got))
    repl_early_parts = got[-1]
    grads.update(gw)
    names = ["gla_w_in", "gla_w_out"]
    dx, d_norms[0], gw, got = b0(dh, comm=ExchangeComm([shards(n) for n in names]))
    parts.update(zip(names, got))
    grads.update(gw)
    grads["norm_g"] = jnp.concatenate(d_norms, axis=0)
    psmall = jnp.stack([_pack([jnp.split(grads[n], N_CHIPS, axis=_shard_axis(n))[j] for n in SMALL], F32, SMALL_ROWS) for j in range(N_CHIPS)])
    prepl = _pack([grads[n] for n in REPL_LATE], F32, SMALL_ROWS)
    late_parts = _run_comm("exchange_last", ExchangeComm([shards(n) for n in big_of["mla"]] + [psmall], [prepl]))
    parts.update(zip(big_of["mla"], late_parts))

    out = {}
    kinds = ("grad", "delta", "new_m", "new_v")
    for n in BIG:
        for kind, a in zip(kinds, _adamw("adam_" + n, parts[n], wts[n], ms[n], vs[n])):
            out[kind, n] = a
    for tag, names, p in (("adam_small", SMALL, late_parts[-2]), ("adam_repl_early", REPL_EARLY, repl_early_parts),
                          ("adam_repl_late", REPL_LATE, late_parts[-1])):
        shapes = [wts[n].shape for n in names]
        packed = [_pack([d[n] for n in names], F32, SMALL_ROWS) for d in (wts, ms, vs)]
        for kind, buf in zip(kinds, _adamw(tag, p, *packed)):
            for n, a in zip(names, _unpack(buf, shapes)):
                out[kind, n] = a
    loss = lax.psum(loss, ("x", "y", "c"))
    return loss, dx, out


def kernel(x, positions, norm_g, final_g, mla_w_in, mla_g_q, mla_w_uq, mla_g_kv, mla_w_ukv, mla_w_out, gla_w_in, gla_w_gk2, gla_b_gk, gla_g_o, gla_w_out, lru_w_in, lru_conv_w, lru_conv_b, lru_w_a, lru_b_a, lru_w_x, lru_b_x, lru_lam, lru_w_out, ssd_w_in, ssd_conv_w, ssd_conv_b, ssd_dt_bias, ssd_a_log, ssd_d, ssd_g_norm, ssd_w_out, loss_target, m_norm_g, m_final_g, m_mla_w_in, m_mla_g_q, m_mla_w_uq, m_mla_g_kv, m_mla_w_ukv, m_mla_w_out, m_gla_w_in, m_gla_w_gk2, m_gla_b_gk, m_gla_g_o, m_gla_w_out, m_lru_w_in, m_lru_conv_w, m_lru_conv_b, m_lru_w_a, m_lru_b_a, m_lru_w_x, m_lru_b_x, m_lru_lam, m_lru_w_out, m_ssd_w_in, m_ssd_conv_w, m_ssd_conv_b, m_ssd_dt_bias, m_ssd_a_log, m_ssd_d, m_ssd_g_norm, m_ssd_w_out, v_norm_g, v_final_g, v_mla_w_in, v_mla_g_q, v_mla_w_uq, v_mla_g_kv, v_mla_w_ukv, v_mla_w_out, v_gla_w_in, v_gla_w_gk2, v_gla_b_gk, v_gla_g_o, v_gla_w_out, v_lru_w_in, v_lru_conv_w, v_lru_conv_b, v_lru_w_a, v_lru_b_a, v_lru_w_x, v_lru_b_x, v_lru_lam, v_lru_w_out, v_ssd_w_in, v_ssd_conv_w, v_ssd_conv_b, v_ssd_dt_bias, v_ssd_a_log, v_ssd_d, v_ssd_g_norm, v_ssd_w_out):
    given = dict(locals())
    stacked = [n for n in WEIGHTS if n not in ("norm_g", "final_g")]

    def blocks(prefix):
        return {n: (given[prefix + n][0] if n in stacked else given[prefix + n]) for n in WEIGHTS}

    loss, dx, out = _train_step(x[0], positions[0], loss_target[0], blocks(""), blocks("m_"), blocks("v_"))
    res = [loss, dx[None]]
    for kind in ("grad", "delta", "new_m", "new_v"):
        res += [(out[kind, n][None] if n in stacked else out[kind, n]) for n in WEIGHTS]
    return tuple(res)
```

```python
import functools
import math

import jax
import jax.numpy as jnp
from jax import lax
from jax.experimental import pallas as pl
from jax.experimental.pallas import tpu as pltpu

F32 = jnp.float32
BF16 = jnp.bfloat16

V7X_VMEM_BYTES = 64 * 1024 * 1024
VMEM_LIMIT = V7X_VMEM_BYTES - 8 * 1024 * 1024
LANE = 128

D_MODEL = 1024
NORM_EPS = 1e-6
MLA_HEADS, MLA_Q_RANK, MLA_KV_RANK, MLA_NOPE, MLA_ROPE, MLA_V = 16, 384, 256, 64, 32, 64
MLA_QK = MLA_NOPE + MLA_ROPE
ROPE_THETA = 10000.0
GLA_HEADS, GLA_DK, GLA_DV, GLA_RANK, GLA_TAU, GLA_CHUNK = 4, 128, 256, 16, 16.0, 64
LRU_WIDTH, LRU_BLOCKS, LRU_BLOCK, LRU_C, CONV_W = 1280, 10, 128, 8.0, 4
SSD_INNER, SSD_P, SSD_HEADS, SSD_GROUPS, SSD_HPG, SSD_STATE, SSD_CHUNK = 2048, 64, 32, 8, 4, 128, 64
ADAM_LR, ADAM_B1, ADAM_B2, ADAM_EPS, ADAM_WD, ADAM_STEP = 0.001, 0.9, 0.999, 1e-08, 0.01, 10

_CP = functools.partial(pltpu.CompilerParams, vmem_limit_bytes=VMEM_LIMIT)


def _bdot(a, b):
    return jnp.dot(a.astype(BF16), b.astype(BF16), preferred_element_type=F32)


def _bdot_nt(a, b):
    return lax.dot_general(a.astype(BF16), b.astype(BF16), (((1,), (1,)), ((), ())), preferred_element_type=F32)


def _bdot_tn(a, b):
    return lax.dot_general(a.astype(BF16), b.astype(BF16), (((0,), (0,)), ((), ())), preferred_element_type=F32)


def _hdot(a, b):
    return jnp.dot(a, b, preferred_element_type=F32, precision=lax.Precision.HIGHEST)


def _hdot_nt(a, b):
    return lax.dot_general(a, b, (((1,), (1,)), ((), ())), preferred_element_type=F32, precision=lax.Precision.HIGHEST)


def _hdot_tn(a, b):
    return lax.dot_general(a, b, (((0,), (0,)), ((), ())), preferred_element_type=F32, precision=lax.Precision.HIGHEST)


def _tri(n):
    r = lax.broadcasted_iota(jnp.int32, (n, n), 0)
    c = lax.broadcasted_iota(jnp.int32, (n, n), 1)
    return r >= c


def _rms(x, g):
    return x * lax.rsqrt(jnp.mean(x * x, axis=-1, keepdims=True) + NORM_EPS) * g


def _silu(x):
    return x * jax.nn.sigmoid(x)


def _shift_rows(x, prev, j):
    if j == 0:
        return x
    t = x.shape[0]

    def fwd_impl(x, prev):
        row = lax.broadcasted_iota(jnp.int32, x.shape, 0)
        return jnp.where(row >= j, pltpu.roll(x, j, 0), pltpu.roll(prev, j, 0))

    @jax.custom_vjp
    def sh(x, prev):
        return fwd_impl(x, prev)

    def sh_fwd(x, prev):
        return fwd_impl(x, prev), None

    def sh_bwd(_, gy):
        row = lax.broadcasted_iota(jnp.int32, gy.shape, 0)
        back = pltpu.roll(gy, t - j, 0)
        return jnp.where(row < t - j, back, 0.0), jnp.where(row >= t - j, back, 0.0)

    sh.defvjp(sh_fwd, sh_bwd)
    return sh(x, prev)


def _one_minus_exp(x):
    series = -x * (1.0 + x * (0.5 + x * (1.0 / 6.0 + x * (1.0 / 24.0 + x * (1.0 / 120.0)))))
    return jnp.where(x > -0.05, series, 1.0 - jnp.exp(x))


def _tile(n, cap):
    if n <= cap:
        return n
    best = None
    for t in range(LANE, cap + 1, LANE):
        if n % t == 0:
            best = t
    assert best is not None, (n, cap)
    return best


MM_TM, MM_TN, MM_TK = 1024, 512, 2048


def _mm(name, a, b, *, ta=False, tb=False, add=None, out_dtype=F32):
    m, k = (a.shape[1], a.shape[0]) if ta else a.shape
    n, kb = (b.shape[0], b.shape[1]) if tb else (b.shape[1], b.shape[0])
    assert k == kb, (name, a.shape, b.shape, ta, tb)
    tm, tn, tk = _tile(m, MM_TM), _tile(n, MM_TN), _tile(k, MM_TK)
    nk = k // tk
    dn = (((0 if ta else 1,), (1 if tb else 0,)), ((), ()))
    has_add = add is not None

    def finish(refs, r):
        if has_add:
            r = r + refs[2][...].astype(F32)
        return r.astype(out_dtype)

    def body_one(*refs):
        a_ref, b_ref, o_ref = refs[0], refs[1], refs[-1]
        o_ref[...] = finish(refs, lax.dot_general(a_ref[...].astype(BF16), b_ref[...].astype(BF16), dn, preferred_element_type=F32))

    def body_acc(*refs):
        a_ref, b_ref = refs[0], refs[1]
        o_ref, acc = refs[-2], refs[-1]
        kk = pl.program_id(2)

        @pl.when(kk == 0)
        def _():
            acc[...] = jnp.zeros(acc.shape, F32)

        acc[...] += lax.dot_general(a_ref[...].astype(BF16), b_ref[...].astype(BF16), dn, preferred_element_type=F32)

        @pl.when(kk == nk - 1)
        def _():
            o_ref[...] = finish(refs, acc[...])

    a_spec = pl.BlockSpec((tk, tm), lambda i, j, q: (q, i)) if ta else pl.BlockSpec((tm, tk), lambda i, j, q: (i, q))
    b_spec = pl.BlockSpec((tn, tk), lambda i, j, q: (j, q)) if tb else pl.BlockSpec((tk, tn), lambda i, j, q: (q, j))
    o_spec = pl.BlockSpec((tm, tn), lambda i, j, q: (i, j))
    in_specs, args = [a_spec, b_spec], [a, b]
    if has_add:
        in_specs.append(o_spec)
        args.append(add)
    return pl.pallas_call(
        body_one if nk == 1 else body_acc, name=name, grid=(m // tm, n // tn, nk), in_specs=in_specs, out_specs=o_spec,
        out_shape=jax.ShapeDtypeStruct((m, n), out_dtype), scratch_shapes=[] if nk == 1 else [pltpu.VMEM((tm, tn), F32)],
        compiler_params=_CP(dimension_semantics=("parallel", "parallel", "arbitrary")),
    )(*args)


class In:
    def __init__(self, arr, block, imap, kind="x", per_h=False, gdtype=F32, gshape=None, gimap=None):
        self.arr, self.block, self.imap, self.kind, self.per_h, self.gdtype = arr, tuple(block), imap, kind, per_h, gdtype
        self.gshape = tuple(gshape) if gshape is not None else tuple(arr.shape)
        self.gimap = gimap if gimap is not None else imap

    def spec(self, rev_g=None):
        imap = self.imap
        if rev_g is None:
            return pl.BlockSpec(self.block, lambda h, g: imap(h, g))
        return pl.BlockSpec(self.block, lambda h, g: imap(h, rev_g - 1 - g))


class Out:
    def __init__(self, shape, dtype, block, imap):
        self.shape, self.dtype, self.block, self.imap = tuple(shape), dtype, tuple(block), imap

    def spec(self, rev_g=None):
        imap = self.imap
        if rev_g is None:
            return pl.BlockSpec(self.block, lambda h, g: imap(h, g))
        return pl.BlockSpec(self.block, lambda h, g: imap(h, rev_g - 1 - g))


def _load_f32(ref):
    v = ref[...]
    return v.astype(F32) if jnp.issubdtype(v.dtype, jnp.floating) else v


def _state_out(grid, shape):
    nd = len(shape)
    return Out(tuple(grid) + tuple(shape), F32, (None, None) + tuple(shape), lambda h, g: (h, g) + (0,) * nd)


def _carry(comm, grid, refs, n_in, n_out, n_scr):
    n_c = len(comm.ops) if comm is not None else 0
    n_s = len(comm.sem_shapes) if comm is not None else 0
    p = 0
    in_refs = refs[p:p + n_in]; p += n_in
    c_src = refs[p:p + n_c]; p += n_c
    out_refs = refs[p:p + n_out]; p += n_out
    c_dst = refs[p:p + n_c]; p += n_c
    scr = refs[p:p + n_scr]; p += n_scr
    c_sem = refs[p:p + n_s]
    first = jnp.logical_and(pl.program_id(0) == 0, pl.program_id(1) == 0)
    last = jnp.logical_and(pl.program_id(0) == grid[0] - 1, pl.program_id(1) == grid[1] - 1)
    return in_refs, out_refs, scr, (c_src, c_dst, c_sem), first, last


def _carry_specs(comm):
    if comm is None:
        return [], [], [], [], []
    any_spec = pl.BlockSpec(memory_space=pl.ANY)
    n = len(comm.ops)
    return [any_spec] * n, list(comm.ops), [any_spec] * n, list(comm.out_shapes), list(comm.sem_shapes)


def _op_fwd(name, f, grid, ins, outs, state_shapes=(), comm=None):
    n_in, n_out, n_st = len(ins), len(outs), len(state_shapes)
    st_outs = [_state_out(grid, s) for s in state_shapes]

    def body(*refs):
        in_refs, o_refs, st_scr, cargs, first, last = _carry(comm, grid, refs, n_in, n_out + n_st, n_st)
        out_refs, sv_refs = o_refs[:n_out], o_refs[n_out:]
        if comm is not None:
            @pl.when(first)
            def _():
                comm.start(*cargs)
        g = pl.program_id(1)
        if n_st:
            @pl.when(g == 0)
            def _():
                for s in st_scr:
                    s[...] = jnp.zeros(s.shape, F32)
        vals = [_load_f32(r) for r in in_refs]
        sts = [s[...] for s in st_scr]
        o, ns = f(g, vals, sts)
        for r, v in zip(out_refs, o):
            r[...] = v.astype(r.dtype)
        for r, s in zip(sv_refs, sts):
            r[...] = s
        for s, v in zip(st_scr, ns):
            s[...] = v
        if comm is not None:
            @pl.when(last)
            def _():
                comm.finish(*cargs)

    all_outs = list(outs) + st_outs
    c_in_specs, c_args, c_out_specs, c_out_shapes, c_sems = _carry_specs(comm)
    res = pl.pallas_call(
        body, name=name, grid=tuple(grid), in_specs=[i.spec() for i in ins] + c_in_specs,
        out_specs=[o.spec() for o in all_outs] + c_out_specs,
        out_shape=[jax.ShapeDtypeStruct(o.shape, o.dtype) for o in all_outs] + c_out_shapes,
        scratch_shapes=[pltpu.VMEM(tuple(s), F32) for s in state_shapes] + c_sems,
        compiler_params=_CP(dimension_semantics=("arbitrary", "arbitrary")),
    )(*[i.arr for i in ins], *c_args)
    n_all = n_out + n_st
    return list(res[:n_out]), list(res[n_out:n_all]), list(res[n_all:])


def _op_bwd(name, f, grid, ins, outs, state_shapes, saved, douts, addto=None, comm=None):
    n_in, n_out, n_st = len(ins), len(outs), len(state_shapes)
    n_g = grid[1]
    addto = addto or {}
    diff = [k for k, i in enumerate(ins) if i.kind in ("x", "p")]
    add_idx = sorted(addto)
    st_ins = [In(s, o.block, o.imap, "c") for s, o in zip(saved, [_state_out(grid, s) for s in state_shapes])]
    dout_ins = [In(d, o.block, o.imap, "c") for d, o in zip(douts, outs)]
    add_ins = []
    for k in add_idx:
        i, a = ins[k], addto[k]
        blk = i.block if i.kind == "x" else i.block[:-2] + a.shape[-2:]
        add_ins.append(In(a, blk, i.gimap if i.kind == "x" else i.imap, "c"))
    g_outs = []
    for k in diff:
        i = ins[k]
        g_outs.append(Out(i.gshape, i.gdtype if i.kind == "x" else F32, i.block, i.gimap))

    def body(*refs):
        all_in, go_refs, ds_scr, cargs, first_step, last_step = _carry(comm, grid, refs, n_in + n_st + n_out + len(add_idx), len(diff), n_st)
        if comm is not None:
            @pl.when(first_step)
            def _():
                comm.start(*cargs)
        p = 0
        in_refs = all_in[p:p + n_in]; p += n_in
        sv_refs = all_in[p:p + n_st]; p += n_st
        do_refs = all_in[p:p + n_out]; p += n_out
        ad_refs = all_in[p:p + len(add_idx)]
        hh = pl.program_id(0)
        step = pl.program_id(1)
        g = n_g - 1 - step
        if n_st:
            @pl.when(step == 0)
            def _():
                for s in ds_scr:
                    s[...] = jnp.zeros(s.shape, F32)
        vals = [_load_f32(r) for r in in_refs]
        sts = [r[...] for r in sv_refs]

        def fw(dvals, states):
            full = list(vals)
            for k, v in zip(diff, dvals):
                full[k] = v
            o, ns = f(g, full, states)
            return list(o), list(ns)

        _, vjp = jax.vjp(fw, [vals[k] for k in diff], sts)
        cts = [r[...].astype(F32) for r in do_refs]
        dns = [s[...] for s in ds_scr]
        dvals, dsts = vjp((cts, dns))
        adds = dict(zip(add_idx, ad_refs))
        for k, r, dv in zip(diff, go_refs, dvals):
            i = ins[k]
            if i.kind == "x":
                if k in adds:
                    dv = dv + adds[k][...].astype(F32)
                r[...] = dv.astype(r.dtype)
            else:
                first = (step == 0) if i.per_h else jnp.logical_and(step == 0, hh == 0)

                @pl.when(first)
                def _(r=r, dv=dv, k=k):
                    r[...] = dv
                    if k in adds:
                        lead = adds[k].shape[0]
                        r[0:lead] += adds[k][...]

                @pl.when(jnp.logical_not(first))
                def _(r=r, dv=dv):
                    r[...] += dv
        for s, v in zip(ds_scr, dsts):
            s[...] = v
        if comm is not None:
            @pl.when(last_step)
            def _():
                comm.finish(*cargs)

    all_ins = list(ins) + st_ins + dout_ins + add_ins
    c_in_specs, c_args, c_out_specs, c_out_shapes, c_sems = _carry_specs(comm)
    res = pl.pallas_call(
        body, name=name, grid=tuple(grid), in_specs=[i.spec(n_g) for i in all_ins] + c_in_specs,
        out_specs=[o.spec(n_g) for o in g_outs] + c_out_specs,
        out_shape=[jax.ShapeDtypeStruct(o.shape, o.dtype) for o in g_outs] + c_out_shapes,
        scratch_shapes=[pltpu.VMEM(tuple(s), F32) for s in state_shapes] + c_sems,
        compiler_params=_CP(dimension_semantics=("arbitrary", "arbitrary")),
    )(*[i.arr for i in all_ins], *c_args)
    return list(res[:len(g_outs)]), list(res[len(g_outs):])


class Op:
    def __init__(self, name, f, grid, ins, outs, state_shapes=()):
        self.name, self.f, self.grid, self.ins, self.outs, self.state_shapes = name, f, grid, ins, outs, state_shapes
        self.saved = None

    def fwd(self, comm=None):
        res, self.saved, self.fwd_comm_out = _op_fwd(self.name + "_fwd", self.f, self.grid, self.ins, self.outs, self.state_shapes, comm)
        return res

    def bwd(self, douts, addto=None, comm=None):
        res, self.bwd_comm_out = _op_bwd(self.name + "_bwd", self.f, self.grid, self.ins, self.outs, self.state_shapes, self.saved, douts,
                                         addto, comm)
        return res


def _rows(arr, t, kind="x", gdtype=F32):
    return In(arr, (t, arr.shape[1]), lambda h, g: (g, 0), kind, gdtype=gdtype)


def _whole(arr, kind="p"):
    nd = arr.ndim
    return In(arr, arr.shape, lambda h, g: (0,) * nd, kind)


def _rows_out(s, n, t, dtype):
    return Out((s, n), dtype, (t, n), lambda h, g: (g, 0))


ROW_T = 256


def _rms_op(name, x, gain, out_dtype=BF16, gdtype=F32):
    s, n = x.shape

    def f(g, vals, sts):
        return [_rms(vals[0], vals[1])], []

    return Op(name, f, (1, s // ROW_T), [_rows(x, ROW_T, gdtype=gdtype), _whole(gain.reshape(1, n))], [_rows_out(s, n, ROW_T, out_dtype)])


def _mla_prep_op(qn, q1, q2, kn, kr, v, cos, sin):
    s = qn.shape[0]
    hd, half = MLA_HEADS, MLA_ROPE // 2

    def f(g, vals, sts):
        qn, q1, q2, kn, kr, v, cos, sin = vals
        cos_h, sin_h = jnp.tile(cos, (1, hd)), jnp.tile(sin, (1, hd))
        r1 = q1 * cos_h - q2 * sin_h
        r2 = q2 * cos_h + q1 * sin_h
        k1, k2 = kr[:, 0:half], kr[:, half:2 * half]
        kr1 = k1 * cos - k2 * sin
        kr2 = k2 * cos + k1 * sin
        zpad = jnp.zeros((qn.shape[0], LANE - MLA_QK), F32)
        qs, ks, vs = [], [], []
        for h in range(hd):
            a, b = h * MLA_NOPE, (h + 1) * MLA_NOPE
            c, d = h * half, (h + 1) * half
            qs.append(jnp.concatenate([qn[:, a:b], r1[:, c:d], r2[:, c:d], zpad], axis=1))
            ks.append(jnp.concatenate([kn[:, a:b], kr1, kr2, zpad], axis=1))
            vs.append(v[:, a:b])
        return [jnp.stack(qs, 0), jnp.stack(ks, 0), jnp.stack(vs, 0)], []

    ins = [_rows(qn, ROW_T, gdtype=BF16), _rows(q1, ROW_T, gdtype=BF16), _rows(q2, ROW_T, gdtype=BF16), _rows(kn, ROW_T, gdtype=BF16),
           _rows(kr, ROW_T, gdtype=BF16), _rows(v, ROW_T, gdtype=BF16), _rows(cos, ROW_T, "c"), _rows(sin, ROW_T, "c")]
    outs = [Out((hd, s, LANE), BF16, (hd, ROW_T, LANE), lambda h, g: (0, g, 0)),
            Out((hd, s, LANE), BF16, (hd, ROW_T, LANE), lambda h, g: (0, g, 0)),
            Out((hd, s, MLA_V), BF16, (hd, ROW_T, MLA_V), lambda h, g: (0, g, 0))]
    return Op("mla_prep", f, (1, s // ROW_T), ins, outs)


ATT_TQ = 256
ATT_LEVELS = 4


def _mla_attn_op(q, k, v, level):
    hd, s, _ = q.shape
    span = s // ATT_LEVELS
    klen = (level + 1) * span
    g_off = level * (span // ATT_TQ)
    scale = MLA_QK ** -0.5

    def f(g, vals, sts):
        q, k, v = vals
        sc = _bdot_nt(q, k) * scale
        r = lax.broadcasted_iota(jnp.int32, sc.shape, 0) + (g + g_off) * ATT_TQ
        c = lax.broadcasted_iota(jnp.int32, sc.shape, 1)
        sc = jnp.where(r >= c, sc, -1e30)
        m = lax.stop_gradient(jnp.max(sc, axis=-1, keepdims=True))
        p = jnp.exp(sc - m)
        p = p / jnp.sum(p, axis=-1, keepdims=True)
        return [_bdot(p, v)], []

    ins = [In(q, (None, ATT_TQ, LANE), lambda h, g: (h, g + g_off, 0), "x", gdtype=BF16, gshape=(hd, span, LANE), gimap=lambda h, g: (h, g, 0)),
           In(k, (None, klen, LANE), lambda h, g: (h, 0, 0), "p", per_h=True, gshape=(hd, klen, LANE)),
           In(v, (None, klen, MLA_V), lambda h, g: (h, 0, 0), "p", per_h=True, gshape=(hd, klen, MLA_V))]
    outs = [Out((hd, span, MLA_V), F32, (None, ATT_TQ, MLA_V), lambda h, g: (h, g, 0))]
    return Op(f"mla_attn{level}", f, (hd, span // ATT_TQ), ins, outs)


def _mla_post_op(o, gate):
    hd, s, _ = o.shape

    def f(g, vals, sts):
        o, gate = vals
        cat = jnp.concatenate([o[h] for h in range(hd)], axis=1)
        return [cat * _silu(gate)], []

    ins = [In(o, (hd, ROW_T, MLA_V), lambda h, g: (0, g, 0), "x"), _rows(gate, ROW_T, gdtype=BF16)]
    return Op("mla_post", f, (1, s // ROW_T), ins, [_rows_out(s, hd * MLA_V, ROW_T, BF16)])


def _gla_gate_op(gk, w2, b):
    s = gk.shape[0]

    def f(g, vals, sts):
        gk, w2, b = vals
        return [jax.nn.log_sigmoid(_bdot(gk, w2) + b) / GLA_TAU], []

    ins = [_rows(gk, ROW_T, gdtype=BF16), _whole(w2), _whole(b)]
    return Op("gla_gate", f, (1, s // ROW_T), ins, [_rows_out(s, GLA_HEADS * GLA_DK, ROW_T, F32)])


def _gla_core_op(q, k, v, gate, la, g_o):
    s = q.shape[0]
    c, nh = GLA_CHUNK, GLA_HEADS

    def f(g, vals, sts):
        q, k, v, gate, la, g_o = vals
        tri = _tri(c)
        b = _hdot(tri.astype(F32), la)
        b_last = jnp.sum(la, axis=0, keepdims=True)
        b_last_col = _hdot_tn(la, jnp.ones((c, 1), F32))
        qt = q * (GLA_DK ** -0.5) * jnp.exp(b)
        kt = k * jnp.exp(-b)
        kd = k * jnp.exp(b_last - b)
        ys, new_sts = [], []
        for h in range(nh):
            ks, vs = slice(h * GLA_DK, (h + 1) * GLA_DK), slice(h * GLA_DV, (h + 1) * GLA_DV)
            att = jnp.where(tri, _bdot_nt(qt[:, ks], kt[:, ks]), 0.0)
            o = _bdot(att, v[:, vs]) + _bdot(qt[:, ks], sts[h])
            new_sts.append(jnp.exp(b_last_col[ks]) * sts[h] + _bdot_tn(kd[:, ks], v[:, vs]))
            ys.append(_rms(o, g_o) * _silu(gate[:, vs]))
        return [jnp.concatenate(ys, axis=1)], new_sts

    ins = [_rows(q, c, gdtype=BF16), _rows(k, c, gdtype=BF16), _rows(v, c, gdtype=BF16), _rows(gate, c, gdtype=BF16), _rows(la, c), _whole(g_o)]
    outs = [_rows_out(s, nh * GLA_DV, c, BF16)]
    return Op("gla_core", f, (1, s // c), ins, outs, [(GLA_DK, GLA_DV)] * nh)


LRU_T = 256


def _lru_op(gate, u, conv_w, conv_b, w_a, b_a, w_x, b_x, lam):
    s, w = u.shape
    t = LRU_T

    def f(g, vals, sts):
        gate, u, cw, cb, w_a, b_a, w_x, b_x, lam = vals
        u_prev, h_prev = sts
        uc = cb
        for kk in range(CONV_W):
            uc = uc + cw[kk] * _shift_rows(u, u_prev, CONV_W - 1 - kk)
        ra, ri = [], []
        for n in range(LRU_BLOCKS):
            blk = uc[:, n * LRU_BLOCK:(n + 1) * LRU_BLOCK]
            ra.append(_bdot(blk, w_a[n]))
            ri.append(_bdot(blk, w_x[n]))
        r = jax.nn.sigmoid(jnp.concatenate(ra, axis=1) + b_a)
        i = jax.nn.sigmoid(jnp.concatenate(ri, axis=1) + b_x)
        log_a = -LRU_C * r * jax.nn.softplus(-lam)
        a = jnp.exp(log_a)
        bb = jnp.sqrt(_one_minus_exp(2.0 * log_a)) * (i * uc)
        zero = jnp.zeros_like(a)
        sh = 1
        while sh < t:
            a_s = _shift_rows(a - 1.0, zero, sh) + 1.0
            b_s = _shift_rows(bb, zero, sh)
            bb = a * b_s + bb
            a = a * a_s
            sh *= 2
        hs = bb + a * h_prev
        last = (lax.broadcasted_iota(jnp.int32, hs.shape, 0) == t - 1).astype(F32)
        h_last = jnp.sum(hs * last, axis=0, keepdims=True)
        return [hs * _silu(gate)], [u, h_last]

    ins = [_rows(gate, t, gdtype=BF16), _rows(u, t, gdtype=BF16), _whole(conv_w), _whole(conv_b), _whole(w_a), _whole(b_a), _whole(w_x),
           _whole(b_x), _whole(lam)]
    return Op("lru_core", f, (1, s // t), ins, [_rows_out(s, w, t, BF16)], [(t, w), (1, w)])


def _ssd_conv_op(xbc, conv_w, conv_b):
    s, w = xbc.shape
    t = ROW_T
    n_x, n_b = SSD_INNER, SSD_GROUPS * SSD_STATE

    def f(g, vals, sts):
        xbc, cw, cb = vals
        acc = cb
        for kk in range(CONV_W):
            acc = acc + cw[kk] * _shift_rows(xbc, sts[0], CONV_W - 1 - kk)
        y = _silu(acc)
        return [y[:, :n_x], y[:, n_x:n_x + n_b], y[:, n_x + n_b:]], [xbc]

    ins = [_rows(xbc, t, gdtype=BF16), _whole(conv_w), _whole(conv_b)]
    outs = [_rows_out(s, n_x, t, F32), _rows_out(s, n_b, t, F32), _rows_out(s, n_b, t, F32)]
    return Op("ssd_conv", f, (1, s // t), ins, outs, [(t, w)])


SSD_L = 256


def _ssd_core_op(x, bm, cm, z, dt, dt_bias, a_log, d_skip, g_norm):
    s = x.shape[0]
    c, hg, p = SSD_L, SSD_HPG, SSD_P
    gw = hg * p

    def f(g, vals, sts):
        x, bm, cm, z, dtr, dt_bias, a_log, d_skip, g_norm = vals
        st = sts[0]
        tri = _tri(c)
        dt = jax.nn.softplus(dtr + dt_bias)
        da = dt * (-jnp.exp(a_log))
        cs = _hdot(tri.astype(F32), da)
        cs_last = jnp.sum(da, axis=0, keepdims=True)
        cb = _bdot_nt(cm, bm)
        lane = lax.broadcasted_iota(jnp.int32, (c, hg), 1)
        ys, new_st = [], []
        for h in range(hg):
            cs_h = cs[:, h:h + 1]
            cs_row = _hdot_nt((lane == h).astype(F32), cs)
            seg = jnp.where(tri, cs_h - cs_row, 0.0)
            lmat = jnp.where(tri, jnp.exp(seg), 0.0)
            x_h = x[:, h * p:(h + 1) * p]
            xdt = x_h * dt[:, h:h + 1]
            y_diag = _bdot(cb * lmat, xdt)
            decay = jnp.exp(cs_last[:, h:h + 1] - cs_h)
            states = _bdot_tn(xdt * decay, bm)
            y_off = _bdot_nt(cm, st[h]) * jnp.exp(cs_h)
            new_st.append(jnp.exp(cs_last[:, h:h + 1]) * st[h] + states)
            ys.append(y_diag + y_off + d_skip[:, h:h + 1] * x_h)
        y = jnp.concatenate(ys, axis=1) * _silu(z)
        return [_rms(y, g_norm)], [jnp.stack(new_st, 0)]

    ins = [In(x, (c, gw), lambda h, g: (g, h), "x"), In(bm, (c, SSD_STATE), lambda h, g: (g, h), "x"),
           In(cm, (c, SSD_STATE), lambda h, g: (g, h), "x"), In(z, (c, gw), lambda h, g: (g, h), "x", gdtype=BF16),
           In(dt, (None, c, hg), lambda h, g: (h, g, 0), "x"),
           In(dt_bias, (None, 1, hg), lambda h, g: (h, 0, 0), "p", per_h=True),
           In(a_log, (None, 1, hg), lambda h, g: (h, 0, 0), "p", per_h=True),
           In(d_skip, (None, 1, hg), lambda h, g: (h, 0, 0), "p", per_h=True),
           In(g_norm, (1, gw), lambda h, g: (0, h), "p", per_h=True)]
    outs = [Out((s, SSD_INNER), BF16, (c, gw), lambda h, g: (g, h))]
    return Op("ssd_core", f, (SSD_GROUPS, s // c), ins, outs, [(hg, p, SSD_STATE)])


def _loss_op(h, target, final_g):
    s, n = h.shape
    t = ROW_T
    n_g = s // t

    def body(h_ref, t_ref, g_ref, loss_ref, dh_ref, dg_ref):
        step = pl.program_id(0)

        def lossf(hv, gv):
            err = _rms(hv, gv) - t_ref[...]
            return 0.5 * jnp.sum(jnp.mean(err * err, axis=-1))

        l, (dh, dg) = jax.value_and_grad(lossf, argnums=(0, 1))(h_ref[...], g_ref[...])
        dh_ref[...] = dh

        @pl.when(step == 0)
        def _():
            loss_ref[...] = jnp.zeros(loss_ref.shape, F32)
            dg_ref[...] = jnp.zeros(dg_ref.shape, F32)

        loss_ref[...] += jnp.full(loss_ref.shape, l, F32)
        dg_ref[...] += dg

    row = pl.BlockSpec((t, n), lambda g: (g, 0))
    one = pl.BlockSpec((1, n), lambda g: (0, 0))
    return pl.pallas_call(
        body, name="loss_head", grid=(n_g,), in_specs=[row, row, one],
        out_specs=[pl.BlockSpec((1, LANE), lambda g: (0, 0)), row, one],
        out_shape=[jax.ShapeDtypeStruct((1, LANE), F32), jax.ShapeDtypeStruct((s, n), F32), jax.ShapeDtypeStruct((1, n), F32)],
        compiler_params=_CP(dimension_semantics=("arbitrary",)),
    )(h, target, final_g.reshape(1, n))


def _pad_cols(w, n):
    return jnp.pad(w, ((0, 0), (0, n - w.shape[1])))


def _pad_rows(w, n):
    return jnp.pad(w, ((0, n - w.shape[0]), (0, 0)))


def _proj_bwd(tag, u, dps, ws):
    du = None
    for i, (dp, w) in enumerate(zip(dps, ws)):
        du = _mm(f"{tag}_du{i}", dp, w, tb=True, add=du)
    dws = [_mm(f"{tag}_dw{i}", u, dp, ta=True) for i, dp in enumerate(dps)]
    return du, dws


def _mla_layer(h, norm_g, w, cos, sin, fwd_comms=(None,) * ATT_LEVELS):
    bf = lambda a: a.astype(BF16)
    w_in, w_uq, w_ukv = w["mla_w_in"], w["mla_w_uq"], w["mla_w_ukv"]
    a0, a1, a2 = MLA_Q_RANK, MLA_Q_RANK + MLA_KV_RANK, MLA_Q_RANK + MLA_KV_RANK + MLA_ROPE
    w_cq, w_ckv, w_kr, w_g = bf(w_in[:, :a0]), bf(w_in[:, a0:a1]), bf(_pad_cols(w_in[:, a1:a2], LANE)), bf(w_in[:, a2:])
    uq = w_uq.reshape(MLA_Q_RANK, MLA_HEADS, MLA_QK)
    half = MLA_ROPE // 2
    w_qn = bf(uq[:, :, :MLA_NOPE].reshape(MLA_Q_RANK, -1))
    w_q1 = bf(uq[:, :, MLA_NOPE:MLA_NOPE + half].reshape(MLA_Q_RANK, -1))
    w_q2 = bf(uq[:, :, MLA_NOPE + half:].reshape(MLA_Q_RANK, -1))
    ukv = w_ukv.reshape(MLA_KV_RANK, MLA_HEADS, MLA_NOPE + MLA_V)
    w_kn = bf(ukv[:, :, :MLA_NOPE].reshape(MLA_KV_RANK, -1))
    w_v = bf(ukv[:, :, MLA_NOPE:].reshape(MLA_KV_RANK, -1))
    w_out = bf(w["mla_w_out"])

    n0 = _rms_op("mla_norm", h, norm_g)
    u, = n0.fwd()
    cq, ckv, kr, gate = (_mm(f"mla_in{i}", u, wi) for i, wi in enumerate((w_cq, w_ckv, w_kr, w_g)))
    nq = _rms_op("mla_qnorm", cq, w["mla_g_q"], gdtype=BF16)
    nkv = _rms_op("mla_kvnorm", ckv, w["mla_g_kv"], gdtype=BF16)
    qn_, = nq.fwd()
    kvn_, = nkv.fwd()
    qn, q1, q2 = (_mm(f"mla_uq{i}", qn_, wi) for i, wi in enumerate((w_qn, w_q1, w_q2)))
    kn, v = (_mm(f"mla_ukv{i}", kvn_, wi) for i, wi in enumerate((w_kn, w_v)))
    prep = _mla_prep_op(qn, q1, q2, kn, kr, v, cos, sin)
    qh, kh, vh = prep.fwd()
    attns = [_mla_attn_op(qh, kh, vh, lv) for lv in range(ATT_LEVELS)]
    o = jnp.concatenate([a.fwd(cm)[0] for a, cm in zip(attns, fwd_comms)], axis=1)
    post = _mla_post_op(o, gate)
    y, = post.fwd()
    h_out = _mm("mla_out", y, w_out, add=h)

    def bwd(dh, comms=(None,) * ATT_LEVELS):
        dy = _mm("mla_out_dy", dh, w_out, tb=True, out_dtype=BF16)
        d_w_out = _mm("mla_out_dw", y, dh, ta=True)
        do, dgate = post.bwd([dy])
        span = do.shape[1] // ATT_LEVELS
        dq_parts, dkh, dvh = [], None, None
        for lv, (a, cm) in enumerate(zip(attns, comms)):
            dq, dkh, dvh = a.bwd([do[:, lv * span:(lv + 1) * span]], addto=None if lv == 0 else {1: dkh, 2: dvh}, comm=cm)
            dq_parts.append(dq)
        dqh = jnp.concatenate(dq_parts, axis=1)
        dqn, dq1, dq2, dkn, dkr, dv = prep.bwd([dqh, dkh, dvh])
        dqn_, d_uq = _proj_bwd("mla_uq", qn_, (dqn, dq1, dq2), (w_qn, w_q1, w_q2))
        dkvn_, d_ukv = _proj_bwd("mla_ukv", kvn_, (dkn, dv), (w_kn, w_v))
        dcq, d_g_q = nq.bwd([dqn_])
        dckv, d_g_kv = nkv.bwd([dkvn_])
        du, d_in = _proj_bwd("mla_in", u, (dcq, dckv, dkr, dgate), (w_cq, w_ckv, w_kr, w_g))
        dh_in, d_norm = n0.bwd([du], addto={0: dh})
        shp = (MLA_Q_RANK, MLA_HEADS, -1)
        g_uq = jnp.concatenate([d_uq[0].reshape(shp), d_uq[1].reshape(shp), d_uq[2].reshape(shp)], axis=2).reshape(MLA_Q_RANK, -1)
        shp = (MLA_KV_RANK, MLA_HEADS, -1)
        g_ukv = jnp.concatenate([d_ukv[0].reshape(shp), d_ukv[1].reshape(shp)], axis=2).reshape(MLA_KV_RANK, -1)
        g_in = jnp.concatenate([d_in[0], d_in[1], d_in[2][:, :MLA_ROPE], d_in[3]], axis=1)
        return dh_in, d_norm, {"mla_w_in": g_in, "mla_g_q": d_g_q.reshape(-1), "mla_w_uq": g_uq, "mla_g_kv": d_g_kv.reshape(-1),
                               "mla_w_ukv": g_ukv, "mla_w_out": d_w_out}, [a.bwd_comm_out for a in attns]

    return h_out, bwd, [a.fwd_comm_out for a in attns]


def _gla_layer(h, norm_g, w, fwd_comm=None):
    bf = lambda a: a.astype(BF16)
    w_in = w["gla_w_in"]
    nk, nv = GLA_HEADS * GLA_DK, GLA_HEADS * GLA_DV
    cuts = (0, nk, 2 * nk, 2 * nk + nv, 2 * nk + 2 * nv)
    w_q, w_k, w_v, w_g = (bf(w_in[:, cuts[i]:cuts[i + 1]]) for i in range(4))
    w_gk = bf(_pad_cols(w_in[:, cuts[4]:], LANE))
    w2 = _pad_rows(w["gla_w_gk2"], LANE)
    b_gk = w["gla_b_gk"].reshape(1, -1)
    g_o = w["gla_g_o"].reshape(1, -1)
    w_out = bf(w["gla_w_out"])

    n0 = _rms_op("gla_norm", h, norm_g)
    u, = n0.fwd()
    q, k, v, gate, gk = (_mm(f"gla_in{i}", u, wi) for i, wi in enumerate((w_q, w_k, w_v, w_g, w_gk)))
    gop = _gla_gate_op(gk, w2, b_gk)
    la, = gop.fwd()
    core = _gla_core_op(q, k, v, gate, la, g_o)
    y, = core.fwd(fwd_comm)
    h_out = _mm("gla_out", y, w_out, add=h)

    def bwd(dh, comm=None):
        dy = _mm("gla_out_dy", dh, w_out, tb=True, out_dtype=BF16)
        d_w_out = _mm("gla_out_dw", y, dh, ta=True)
        dq, dk, dv, dgate, dla, d_g_o = core.bwd([dy], comm=comm)
        dgk, d_w2, d_b = gop.bwd([dla])
        du, d_in = _proj_bwd("gla_in", u, (dq, dk, dv, dgate, dgk), (w_q, w_k, w_v, w_g, w_gk))
        dh_in, d_norm = n0.bwd([du], addto={0: dh})
        g_in = jnp.concatenate([d_in[0], d_in[1], d_in[2], d_in[3], d_in[4][:, :GLA_RANK]], axis=1)
        return dh_in, d_norm, {"gla_w_in": g_in, "gla_w_gk2": d_w2[:GLA_RANK], "gla_b_gk": d_b.reshape(-1), "gla_g_o": d_g_o.reshape(-1),
                               "gla_w_out": d_w_out}, core.bwd_comm_out

    return h_out, bwd, core.fwd_comm_out


def _lru_layer(h, norm_g, w):
    bf = lambda a: a.astype(BF16)
    w_in = w["lru_w_in"]
    w_g, w_u = bf(w_in[:, :LRU_WIDTH]), bf(w_in[:, LRU_WIDTH:])
    row = lambda a: a.reshape(1, -1)
    w_out = bf(w["lru_w_out"])

    n0 = _rms_op("lru_norm", h, norm_g)
    u_, = n0.fwd()
    gate, u = (_mm(f"lru_in{i}", u_, wi) for i, wi in enumerate((w_g, w_u)))
    core = _lru_op(gate, u, w["lru_conv_w"].reshape(CONV_W, 1, -1), row(w["lru_conv_b"]), w["lru_w_a"], row(w["lru_b_a"]), w["lru_w_x"],
                   row(w["lru_b_x"]), row(w["lru_lam"]))
    y, = core.fwd()
    h_out = _mm("lru_out", y, w_out, add=h)

    def bwd(dh, comm=None):
        dy = _mm("lru_out_dy", dh, w_out, tb=True, out_dtype=BF16)
        d_w_out = _mm("lru_out_dw", y, dh, ta=True)
        dgate, du, d_cw, d_cb, d_wa, d_ba, d_wx, d_bx, d_lam = core.bwd([dy], comm=comm)
        du_, d_in = _proj_bwd("lru_in", u_, (dgate, du), (w_g, w_u))
        dh_in, d_norm = n0.bwd([du_], addto={0: dh})
        return dh_in, d_norm, {"lru_w_in": jnp.concatenate(d_in, axis=1), "lru_conv_w": d_cw.reshape(CONV_W, -1), "lru_conv_b": d_cb.reshape(-1),
                               "lru_w_a": d_wa, "lru_b_a": d_ba.reshape(-1), "lru_w_x": d_wx, "lru_b_x": d_bx.reshape(-1),
                               "lru_lam": d_lam.reshape(-1), "lru_w_out": d_w_out}, core.bwd_comm_out

    return h_out, bwd


def _ssd_layer(h, norm_g, w):
    bf = lambda a: a.astype(BF16)
    s = h.shape[0]
    w_in = w["ssd_w_in"]
    conv_dim = SSD_INNER + 2 * SSD_GROUPS * SSD_STATE
    w_z, w_xbc = bf(w_in[:, :SSD_INNER]), bf(w_in[:, SSD_INNER:SSD_INNER + conv_dim])
    w_dt = bf(_pad_cols(w_in[:, SSD_INNER + conv_dim:], LANE))
    grp = lambda a: a.reshape(SSD_GROUPS, 1, SSD_HPG)
    w_out = bf(w["ssd_w_out"])

    n0 = _rms_op("ssd_norm", h, norm_g)
    u, = n0.fwd()
    z, xbc, dtp = (_mm(f"ssd_in{i}", u, wi) for i, wi in enumerate((w_z, w_xbc, w_dt)))
    conv = _ssd_conv_op(xbc, w["ssd_conv_w"].reshape(CONV_W, 1, -1), w["ssd_conv_b"].reshape(1, -1))
    x, bm, cm = conv.fwd()
    dt = dtp[:, :SSD_HEADS].reshape(s, SSD_GROUPS, SSD_HPG).transpose(1, 0, 2)
    core = _ssd_core_op(x, bm, cm, z, dt, grp(w["ssd_dt_bias"]), grp(w["ssd_a_log"]), grp(w["ssd_d"]), w["ssd_g_norm"].reshape(1, -1))
    y, = core.fwd()
    h_out = _mm("ssd_out", y, w_out, add=h)

    def bwd(dh):
        dy = _mm("ssd_out_dy", dh, w_out, tb=True, out_dtype=BF16)
        d_w_out = _mm("ssd_out_dw", y, dh, ta=True)
        dx, dbm, dcm, dz, ddt, d_dtb, d_alog, d_d, d_gn = core.bwd([dy])
        dxbc, d_cw, d_cb = conv.bwd([dx, dbm, dcm])
        ddtp = _pad_cols(ddt.transpose(1, 0, 2).reshape(s, SSD_HEADS), LANE).astype(BF16)
        du, d_in = _proj_bwd("ssd_in", u, (dz, dxbc, ddtp), (w_z, w_xbc, w_dt))
        dh_in, d_norm = n0.bwd([du], addto={0: dh})
        g_in = jnp.concatenate([d_in[0], d_in[1], d_in[2][:, :SSD_HEADS]], axis=1)
        return dh_in, d_norm, {"ssd_w_in": g_in, "ssd_conv_w": d_cw.reshape(CONV_W, -1), "ssd_conv_b": d_cb.reshape(-1),
                               "ssd_dt_bias": d_dtb.reshape(-1), "ssd_a_log": d_alog.reshape(-1), "ssd_d": d_d.reshape(-1),
                               "ssd_g_norm": d_gn.reshape(-1), "ssd_w_out": d_w_out}

    return h_out, bwd


def _rope_tables(positions):
    inv_freq = ROPE_THETA ** (-jnp.arange(0, MLA_ROPE, 2, dtype=F32) / MLA_ROPE)
    ang = positions.astype(F32)[:, None] * inv_freq
    return jnp.cos(ang), jnp.sin(ang)


WEIGHTS = ["norm_g", "final_g", "mla_w_in", "mla_g_q", "mla_w_uq", "mla_g_kv", "mla_w_ukv", "mla_w_out", "gla_w_in", "gla_w_gk2", "gla_b_gk",
           "gla_g_o", "gla_w_out", "lru_w_in", "lru_conv_w", "lru_conv_b", "lru_w_a", "lru_b_a", "lru_w_x", "lru_b_x", "lru_lam", "lru_w_out",
           "ssd_w_in", "ssd_conv_w", "ssd_conv_b", "ssd_dt_bias", "ssd_a_log", "ssd_d", "ssd_g_norm", "ssd_w_out"]
BIG = ["mla_w_in", "mla_w_uq", "mla_w_ukv", "mla_w_out", "gla_w_in", "gla_w_out", "lru_w_in", "lru_w_out", "ssd_w_in", "ssd_w_out"]
SMALL = ["gla_w_gk2", "gla_b_gk", "gla_g_o", "lru_conv_w", "lru_conv_b", "lru_b_a", "lru_b_x", "lru_lam", "ssd_conv_w", "ssd_conv_b", "ssd_g_norm"]
REPL = ["norm_g", "final_g", "mla_g_q", "mla_g_kv", "lru_w_a", "lru_w_x", "ssd_dt_bias", "ssd_a_log", "ssd_d"]
REPL_EARLY = ["lru_w_a", "lru_w_x"]
REPL_LATE = [n for n in REPL if n not in REPL_EARLY]
N_CHIPS, N_DEV = 4, 8
PACK_W = 1024
ADAM_ROWS = 256
SMALL_ROWS = 64


def _shard_axis(name):
    return 0 if name.endswith("_w_out") else -1


def _pack(arrs, dtype, row_mult):
    flat = jnp.concatenate([a.reshape(-1).astype(dtype) for a in arrs])
    per = PACK_W * row_mult
    total = -(-flat.shape[0] // per) * per
    return jnp.pad(flat, (0, total - flat.shape[0])).reshape(-1, PACK_W)


def _unpack(buf, shapes):
    flat = buf.reshape(-1)
    out, off = [], 0
    for s in shapes:
        n = math.prod(s)
        out.append(flat[off:off + n].reshape(s))
        off += n
    return out


def _mesh_pos():
    return lax.axis_index("x"), lax.axis_index("y"), lax.axis_index("c")


class GatherComm:
    def __init__(self, ops):
        self.ops = list(ops)
        n = len(self.ops)
        assert all(o.ndim == 2 and o.shape[0] % 32 == 0 for o in self.ops), [o.shape for o in self.ops]
        self.out_shapes = [jax.ShapeDtypeStruct((N_CHIPS,) + o.shape, o.dtype) for o in self.ops]
        self.sem_shapes = [pltpu.SemaphoreType.DMA((6 * n,)), pltpu.SemaphoreType.DMA((6 * n,)), pltpu.SemaphoreType.DMA((n,))]

    def _copies(self, srcs, dsts, sems):
        send_sems, recv_sems, local_sems = sems
        n = len(self.ops)
        x, y, c = _mesh_pos()
        me_id, sibling = (x, y, c), (x, y, 1 - c)
        chips = [(1 - x, y), (x, 1 - y), (1 - x, 1 - y)]
        mine = 2 * x + y

        def half(i, cc):
            h = self.ops[i].shape[0] // 2
            return pl.ds(cc * h, h)

        def copy(i, k, src, slot, cc, to):
            return pltpu.make_async_remote_copy(src_ref=src, dst_ref=dsts[i].at[slot, half(i, cc)], send_sem=send_sems.at[i * 6 + k],
                                                recv_sem=recv_sems.at[i * 6 + k], device_id=to, device_id_type=pl.DeviceIdType.MESH)

        local = [pltpu.make_async_copy(srcs[i], dsts[i].at[mine], local_sems.at[i]) for i in range(n)]
        first, ici_recvs, passed, sib_recvs = [], [], [], []
        for i in range(n):
            my_half = srcs[i].at[half(i, c)]
            for k, (px, py) in enumerate(chips):
                slot = 2 * px + py
                first.append(copy(i, k, my_half, mine, c, (px, py, c)))
                ici_recvs.append(copy(i, k, my_half, slot, c, me_id))
                passed.append(copy(i, 3 + k, dsts[i].at[slot, half(i, c)], slot, c, sibling))
                sib_recvs.append(copy(i, 3 + k, my_half, slot, 1 - c, me_id))
        return local, first, ici_recvs, passed, sib_recvs

    def start(self, srcs, dsts, sems):
        local, first, _, _, _ = self._copies(srcs, dsts, sems)
        for cp in local + first:
            cp.start()

    def finish(self, srcs, dsts, sems):
        local, first, ici_recvs, passed, sib_recvs = self._copies(srcs, dsts, sems)
        for rc, fw in zip(ici_recvs, passed):
            rc.wait_recv()
            fw.start()
        for cp in sib_recvs:
            cp.wait_recv()
        for cp in first + passed:
            cp.wait_send()
        for cp in local:
            cp.wait()


class ExchangeComm:
    def __init__(self, chip_ops, all_ops=()):
        self.ops = list(chip_ops) + list(all_ops)
        self.per_chip = (True,) * len(chip_ops) + (False,) * len(all_ops)
        n = len(self.ops)
        self.out_shapes = [jax.ShapeDtypeStruct((N_DEV,) + o.shape[-2:], o.dtype) for o in self.ops]
        self.sem_shapes = [pltpu.SemaphoreType.DMA((7 * n,)), pltpu.SemaphoreType.DMA((7 * n,)), pltpu.SemaphoreType.DMA((n,))]

    def _copies(self, srcs, dsts, sems):
        send_sems, recv_sems, local_sems = sems
        n, per_chip = len(self.ops), self.per_chip
        x, y, c = _mesh_pos()
        me_id, sibling = (x, y, c), (x, y, 1 - c)
        chips = [(1 - x, y), (x, 1 - y), (1 - x, 1 - y)]

        def dev(px, py, pc):
            return 4 * px + 2 * py + pc

        def part(i, px, py):
            return srcs[i].at[2 * px + py] if per_chip[i] else srcs[i]

        def copy(i, k, src, slot, to):
            return pltpu.make_async_remote_copy(src_ref=src, dst_ref=dsts[i].at[slot], send_sem=send_sems.at[i * 7 + k],
                                                recv_sem=recv_sems.at[i * 7 + k], device_id=to, device_id_type=pl.DeviceIdType.MESH)

        me = dev(x, y, c)
        local = [pltpu.make_async_copy(part(i, x, y), dsts[i].at[me], local_sems.at[i]) for i in range(n)]
        first, ici_recvs, passed, sib_recvs = [], [], [], []
        for i in range(n):
            first.append(copy(i, 0, part(i, x, y), me, sibling))
            first += [copy(i, 1 + k, part(i, px, py), me, (px, py, c)) for k, (px, py) in enumerate(chips)]
            sib_recvs.append(copy(i, 0, part(i, x, y), dev(x, y, 1 - c), me_id))
            for k, (px, py) in enumerate(chips):
                slot = dev(px, py, c)
                ici_recvs.append(copy(i, 1 + k, part(i, x, y), slot, me_id))
                passed.append(copy(i, 4 + k, dsts[i].at[slot], slot, sibling))
                sib_recvs.append(copy(i, 4 + k, part(i, x, y), dev(px, py, 1 - c), me_id))
        return local, first, ici_recvs, passed, sib_recvs

    def start(self, srcs, dsts, sems):
        local, first, _, _, _ = self._copies(srcs, dsts, sems)
        for cp in local + first:
            cp.start()

    def finish(self, srcs, dsts, sems):
        local, first, ici_recvs, passed, sib_recvs = self._copies(srcs, dsts, sems)
        for rc, fw in zip(ici_recvs, passed):
            rc.wait_recv()
            fw.start()
        for cp in sib_recvs:
            cp.wait_recv()
        for cp in first + passed:
            cp.wait_send()
        for cp in local:
            cp.wait()


def _run_comm(name, comm):
    n = len(comm.ops)

    def body(*refs):
        srcs, dsts, sems = refs[:n], refs[n:2 * n], refs[2 * n:]
        comm.start(srcs, dsts, sems)
        comm.finish(srcs, dsts, sems)

    any_spec = pl.BlockSpec(memory_space=pl.ANY)
    return pl.pallas_call(body, name=name, in_specs=[any_spec] * n, out_specs=[any_spec] * n, out_shape=comm.out_shapes,
                          scratch_shapes=comm.sem_shapes)(*comm.ops)


def _adamw(name, parts, w, m, v):
    rows, cols = w.shape
    t = next(c for c in (ADAM_ROWS, ADAM_ROWS // 2, SMALL_ROWS) if rows % c == 0)
    c1 = 1.0 - ADAM_B1 ** ADAM_STEP
    c2 = 1.0 - ADAM_B2 ** ADAM_STEP

    def body(p_ref, w_ref, m_ref, v_ref, g_ref, d_ref, nm_ref, nv_ref):
        g = p_ref[0].astype(F32)
        for d in range(1, N_DEV):
            g = g + p_ref[d].astype(F32)
        nm = ADAM_B1 * m_ref[...] + (1.0 - ADAM_B1) * g
        nv = ADAM_B2 * v_ref[...] + (1.0 - ADAM_B2) * (g * g)
        g_ref[...] = g
        nm_ref[...] = nm
        nv_ref[...] = nv
        d_ref[...] = -ADAM_LR * ((nm / c1) / (jnp.sqrt(nv / c2) + ADAM_EPS) + ADAM_WD * w_ref[...])

    row = pl.BlockSpec((t, cols), lambda i: (i, 0))
    return pl.pallas_call(
        body, name=name, grid=(rows // t,), in_specs=[pl.BlockSpec((N_DEV, t, cols), lambda i: (0, i, 0)), row, row, row],
        out_specs=[row] * 4, out_shape=[jax.ShapeDtypeStruct((rows, cols), F32)] * 4,
        compiler_params=_CP(dimension_semantics=("parallel",)),
    )(parts, w, m, v)


def _train_step(x, positions, target, wts, ms, vs):
    small_shapes = [wts[n].shape for n in SMALL]

    big_of = {tag: [n for n in BIG if n.startswith(tag)] for tag in ("mla", "gla", "lru", "ssd")}
    full = {n: wts[n] for n in REPL}

    def gather_comm(names, extra=()):
        return GatherComm([wts[n].astype(BF16) for n in names] + list(extra))

    def assemble(names, got):
        for k, n in enumerate(names):
            full[n] = jnp.concatenate([got[k][j] for j in range(N_CHIPS)], axis=_shard_axis(n))

    got = _run_comm("gather_first", gather_comm(big_of["mla"], [_pack([wts[n] for n in SMALL], F32, SMALL_ROWS)]))
    assemble(big_of["mla"], got)
    per_chip_small = [_unpack(got[-1][j], small_shapes) for j in range(N_CHIPS)]
    for k, n in enumerate(SMALL):
        full[n] = jnp.concatenate([per_chip_small[j][k] for j in range(N_CHIPS)], axis=_shard_axis(n))

    cos, sin = _rope_tables(positions)
    ng = full["norm_g"]
    by_level = ["gla_w_out", "lru_w_out", "lru_w_in", "gla_w_in"]
    h1, b0, got = _mla_layer(x, ng[0], full, cos, sin, fwd_comms=[gather_comm([n]) for n in by_level])
    for n, g1 in zip(by_level, got):
        assemble([n], g1)
    h2, b1, got = _gla_layer(h1, ng[1], full, fwd_comm=gather_comm(big_of["ssd"]))
    assemble(big_of["ssd"], got)
    h3, b2 = _lru_layer(h2, ng[2], full)
    h4, b3 = _ssd_layer(h3, ng[3], full)
    loss, dh, d_final = _loss_op(h4, target, full["final_g"])
    loss = loss[0, 0]
    grads = {"final_g": d_final.reshape(-1)}
    d_norms = [None] * 4
    def shards(n, dtype=BF16):
        return jnp.stack([p.astype(dtype) for p in jnp.split(grads[n], N_CHIPS, axis=_shard_axis(n))])

    parts = {}
    dh, d_norms[3], gw = b3(dh)
    grads.update(gw)
    names = ["ssd_w_in"]
    dh, d_norms[2], gw, got = b2(dh, comm=ExchangeComm([shards(n) for n in names]))
    parts.update(zip(names, got))
    grads.update(gw)
    names = ["ssd_w_out", "lru_w_in", "lru_w_out"]
    dh, d_norms[1], gw, got = b1(dh, comm=ExchangeComm([shards(n) for n in names], [_pack([grads[n] for n in REPL_EARLY], F32, SMALL_ROWS)]))
    parts.update(zip(names, got))
    repl_early_parts = got[-1]
    grads.update(gw)
    by_level = [None, None, "gla_w_out", "gla_w_in"]
    dx, d_norms[0], gw, got = b0(dh, comms=[None if n is None else ExchangeComm([shards(n)]) for n in by_level])
    parts.update({n: g1[0] for n, g1 in zip(by_level, got) if n is not None})
    grads.update(gw)
    grads["norm_g"] = jnp.concatenate(d_norms, axis=0)
    psmall = jnp.stack([_pack([jnp.split(grads[n], N_CHIPS, axis=_shard_axis(n))[j] for n in SMALL], F32, SMALL_ROWS) for j in range(N_CHIPS)])
    prepl = _pack([grads[n] for n in REPL_LATE], F32, SMALL_ROWS)
    late_parts = _run_comm("exchange_last", ExchangeComm([shards(n) for n in big_of["mla"]] + [psmall], [prepl]))
    parts.update(zip(big_of["mla"], late_parts))

    out = {}
    kinds = ("grad", "delta", "new_m", "new_v")
    for n in BIG:
        for kind, a in zip(kinds, _adamw("adam_" + n, parts[n], wts[n], ms[n], vs[n])):
            out[kind, n] = a
    for tag, names, p in (("adam_small", SMALL, late_parts[-2]), ("adam_repl_early", REPL_EARLY, repl_early_parts),
                          ("adam_repl_late", REPL_LATE, late_parts[-1])):
        shapes = [wts[n].shape for n in names]
        packed = [_pack([d[n] for n in names], F32, SMALL_ROWS) for d in (wts, ms, vs)]
        for kind, buf in zip(kinds, _adamw(tag, p, *packed)):
            for n, a in zip(names, _unpack(buf, shapes)):
                out[kind, n] = a
    loss = lax.psum(loss, ("x", "y", "c"))
    return loss, dx, out


def kernel(x, positions, norm_g, final_g, mla_w_in, mla_g_q, mla_w_uq, mla_g_kv, mla_w_ukv, mla_w_out, gla_w_in, gla_w_gk2, gla_b_gk, gla_g_o, gla_w_out, lru_w_in, lru_conv_w, lru_conv_b, lru_w_a, lru_b_a, lru_w_x, lru_b_x, lru_lam, lru_w_out, ssd_w_in, ssd_conv_w, ssd_conv_b, ssd_dt_bias, ssd_a_log, ssd_d, ssd_g_norm, ssd_w_out, loss_target, m_norm_g, m_final_g, m_mla_w_in, m_mla_g_q, m_mla_w_uq, m_mla_g_kv, m_mla_w_ukv, m_mla_w_out, m_gla_w_in, m_gla_w_gk2, m_gla_b_gk, m_gla_g_o, m_gla_w_out, m_lru_w_in, m_lru_conv_w, m_lru_conv_b, m_lru_w_a, m_lru_b_a, m_lru_w_x, m_lru_b_x, m_lru_lam, m_lru_w_out, m_ssd_w_in, m_ssd_conv_w, m_ssd_conv_b, m_ssd_dt_bias, m_ssd_a_log, m_ssd_d, m_ssd_g_norm, m_ssd_w_out, v_norm_g, v_final_g, v_mla_w_in, v_mla_g_q, v_mla_w_uq, v_mla_g_kv, v_mla_w_ukv, v_mla_w_out, v_gla_w_in, v_gla_w_gk2, v_gla_b_gk, v_gla_g_o, v_gla_w_out, v_lru_w_in, v_lru_conv_w, v_lru_conv_b, v_lru_w_a, v_lru_b_a, v_lru_w_x, v_lru_b_x, v_lru_lam, v_lru_w_out, v_ssd_w_in, v_ssd_conv_w, v_ssd_conv_b, v_ssd_dt_bias, v_ssd_a_log, v_ssd_d, v_ssd_g_norm, v_ssd_w_out):
    given = dict(locals())
    stacked = [n for n in WEIGHTS if n not in ("norm_g", "final_g")]

    def blocks(prefix):
        return {n: (given[prefix + n][0] if n in stacked else given[prefix + n]) for n in WEIGHTS}

    loss, dx, out = _train_step(x[0], positions[0], loss_target[0], blocks(""), blocks("m_"), blocks("v_"))
    res = [loss, dx[None]]
    for kind in ("grad", "delta", "new_m", "new_v"):
        res += [(out[kind, n][None] if n in stacked else out[kind, n]) for n in WEIGHTS]
    return tuple(res)
```

```python
import functools
import math

import jax
import jax.numpy as jnp
from jax import lax
from jax.experimental import pallas as pl
from jax.experimental.pallas import tpu as pltpu

F32 = jnp.float32
BF16 = jnp.bfloat16

V7X_VMEM_BYTES = 64 * 1024 * 1024
VMEM_LIMIT = V7X_VMEM_BYTES - 8 * 1024 * 1024
LANE = 128

D_MODEL = 1024
NORM_EPS = 1e-6
MLA_HEADS, MLA_Q_RANK, MLA_KV_RANK, MLA_NOPE, MLA_ROPE, MLA_V = 16, 384, 256, 64, 32, 64
MLA_QK = MLA_NOPE + MLA_ROPE
ROPE_THETA = 10000.0
GLA_HEADS, GLA_DK, GLA_DV, GLA_RANK, GLA_TAU, GLA_CHUNK = 4, 128, 256, 16, 16.0, 64
LRU_WIDTH, LRU_BLOCKS, LRU_BLOCK, LRU_C, CONV_W = 1280, 10, 128, 8.0, 4
SSD_INNER, SSD_P, SSD_HEADS, SSD_GROUPS, SSD_HPG, SSD_STATE, SSD_CHUNK = 2048, 64, 32, 8, 4, 128, 64
ADAM_LR, ADAM_B1, ADAM_B2, ADAM_EPS, ADAM_WD, ADAM_STEP = 0.001, 0.9, 0.999, 1e-08, 0.01, 10

_CP = functools.partial(pltpu.CompilerParams, vmem_limit_bytes=VMEM_LIMIT)


def _bdot(a, b):
    return jnp.dot(a.astype(BF16), b.astype(BF16), preferred_element_type=F32)


def _bdot_nt(a, b):
    return lax.dot_general(a.astype(BF16), b.astype(BF16), (((1,), (1,)), ((), ())), preferred_element_type=F32)


def _bdot_tn(a, b):
    return lax.dot_general(a.astype(BF16), b.astype(BF16), (((0,), (0,)), ((), ())), preferred_element_type=F32)


def _tri(n):
    r = lax.broadcasted_iota(jnp.int32, (n, n), 0)
    c = lax.broadcasted_iota(jnp.int32, (n, n), 1)
    return r >= c


def _rms(x, g):
    return x * lax.rsqrt(jnp.mean(x * x, axis=-1, keepdims=True) + NORM_EPS) * g


def _silu(x):
    return x * jax.nn.sigmoid(x)


def _shift_rows(x, prev, j):
    if j == 0:
        return x
    t = x.shape[0]

    def fwd_impl(x, prev):
        row = lax.broadcasted_iota(jnp.int32, x.shape, 0)
        return jnp.where(row >= j, pltpu.roll(x, j, 0), pltpu.roll(prev, j, 0))

    @jax.custom_vjp
    def sh(x, prev):
        return fwd_impl(x, prev)

    def sh_fwd(x, prev):
        return fwd_impl(x, prev), None

    def sh_bwd(_, gy):
        row = lax.broadcasted_iota(jnp.int32, gy.shape, 0)
        back = pltpu.roll(gy, t - j, 0)
        return jnp.where(row < t - j, back, 0.0), jnp.where(row >= t - j, back, 0.0)

    sh.defvjp(sh_fwd, sh_bwd)
    return sh(x, prev)


def _cumsum_rows(x):
    zero = jnp.zeros_like(x)
    sh = 1
    while sh < x.shape[0]:
        x = x + _shift_rows(x, zero, sh)
        sh *= 2
    return x


def _one_minus_exp(x):
    series = -x * (1.0 + x * (0.5 + x * (1.0 / 6.0 + x * (1.0 / 24.0 + x * (1.0 / 120.0)))))
    return jnp.where(x > -0.05, series, 1.0 - jnp.exp(x))


def _tile(n, cap):
    if n <= cap:
        return n
    best = None
    for t in range(LANE, cap + 1, LANE):
        if n % t == 0:
            best = t
    assert best is not None, (n, cap)
    return best


MM_TM, MM_TN, MM_TK = 1024, 512, 2048


def _mm(name, a, b, *, ta=False, tb=False, add=None, out_dtype=F32):
    m, k = (a.shape[1], a.shape[0]) if ta else a.shape
    n, kb = (b.shape[0], b.shape[1]) if tb else (b.shape[1], b.shape[0])
    assert k == kb, (name, a.shape, b.shape, ta, tb)
    tm, tn, tk = _tile(m, MM_TM), _tile(n, MM_TN), _tile(k, MM_TK)
    nk = k // tk
    dn = (((0 if ta else 1,), (1 if tb else 0,)), ((), ()))
    has_add = add is not None

    def finish(refs, r):
        if has_add:
            r = r + refs[2][...].astype(F32)
        return r.astype(out_dtype)

    def body_one(*refs):
        a_ref, b_ref, o_ref = refs[0], refs[1], refs[-1]
        o_ref[...] = finish(refs, lax.dot_general(a_ref[...].astype(BF16), b_ref[...].astype(BF16), dn, preferred_element_type=F32))

    def body_acc(*refs):
        a_ref, b_ref = refs[0], refs[1]
        o_ref, acc = refs[-2], refs[-1]
        kk = pl.program_id(2)

        @pl.when(kk == 0)
        def _():
            acc[...] = jnp.zeros(acc.shape, F32)

        acc[...] += lax.dot_general(a_ref[...].astype(BF16), b_ref[...].astype(BF16), dn, preferred_element_type=F32)

        @pl.when(kk == nk - 1)
        def _():
            o_ref[...] = finish(refs, acc[...])

    a_spec = pl.BlockSpec((tk, tm), lambda i, j, q: (q, i)) if ta else pl.BlockSpec((tm, tk), lambda i, j, q: (i, q))
    b_spec = pl.BlockSpec((tn, tk), lambda i, j, q: (j, q)) if tb else pl.BlockSpec((tk, tn), lambda i, j, q: (q, j))
    o_spec = pl.BlockSpec((tm, tn), lambda i, j, q: (i, j))
    in_specs, args = [a_spec, b_spec], [a, b]
    if has_add:
        in_specs.append(o_spec)
        args.append(add)
    return pl.pallas_call(
        body_one if nk == 1 else body_acc, name=name, grid=(m // tm, n // tn, nk), in_specs=in_specs, out_specs=o_spec,
        out_shape=jax.ShapeDtypeStruct((m, n), out_dtype), scratch_shapes=[] if nk == 1 else [pltpu.VMEM((tm, tn), F32)],
        compiler_params=_CP(dimension_semantics=("parallel", "parallel", "arbitrary")),
    )(*args)


class In:
    def __init__(self, arr, block, imap, kind="x", per_h=False, gdtype=F32, gshape=None, gimap=None):
        self.arr, self.block, self.imap, self.kind, self.per_h, self.gdtype = arr, tuple(block), imap, kind, per_h, gdtype
        self.gshape = tuple(gshape) if gshape is not None else tuple(arr.shape)
        self.gimap = gimap if gimap is not None else imap

    def spec(self, rev_g=None):
        imap = self.imap
        if rev_g is None:
            return pl.BlockSpec(self.block, lambda h, g: imap(h, g))
        return pl.BlockSpec(self.block, lambda h, g: imap(h, rev_g - 1 - g))


class Out:
    def __init__(self, shape, dtype, block, imap):
        self.shape, self.dtype, self.block, self.imap = tuple(shape), dtype, tuple(block), imap

    def spec(self, rev_g=None):
        imap = self.imap
        if rev_g is None:
            return pl.BlockSpec(self.block, lambda h, g: imap(h, g))
        return pl.BlockSpec(self.block, lambda h, g: imap(h, rev_g - 1 - g))


def _load_f32(ref):
    v = ref[...]
    return v.astype(F32) if jnp.issubdtype(v.dtype, jnp.floating) else v


def _state_out(grid, shape):
    nd = len(shape)
    return Out(tuple(grid) + tuple(shape), F32, (None, None) + tuple(shape), lambda h, g: (h, g) + (0,) * nd)


def _carry(comm, grid, refs, n_in, n_out, n_scr):
    n_c = len(comm.ops) if comm is not None else 0
    n_s = len(comm.sem_shapes) if comm is not None else 0
    p = 0
    in_refs = refs[p:p + n_in]; p += n_in
    c_src = refs[p:p + n_c]; p += n_c
    out_refs = refs[p:p + n_out]; p += n_out
    c_dst = refs[p:p + n_c]; p += n_c
    scr = refs[p:p + n_scr]; p += n_scr
    c_sem = refs[p:p + n_s]
    first = jnp.logical_and(pl.program_id(0) == 0, pl.program_id(1) == 0)
    last = jnp.logical_and(pl.program_id(0) == grid[0] - 1, pl.program_id(1) == grid[1] - 1)
    return in_refs, out_refs, scr, (c_src, c_dst, c_sem), first, last


def _carry_specs(comm):
    if comm is None:
        return [], [], [], [], []
    any_spec = pl.BlockSpec(memory_space=pl.ANY)
    n = len(comm.ops)
    return [any_spec] * n, list(comm.ops), [any_spec] * n, list(comm.out_shapes), list(comm.sem_shapes)


def _op_fwd(name, f, grid, ins, outs, state_shapes=(), comm=None):
    n_in, n_out, n_st = len(ins), len(outs), len(state_shapes)
    st_outs = [_state_out(grid, s) for s in state_shapes]

    def body(*refs):
        in_refs, o_refs, st_scr, cargs, first, last = _carry(comm, grid, refs, n_in, n_out + n_st, n_st)
        out_refs, sv_refs = o_refs[:n_out], o_refs[n_out:]
        if comm is not None:
            @pl.when(first)
            def _():
                comm.start(*cargs)
        g = pl.program_id(1)
        if n_st:
            @pl.when(g == 0)
            def _():
                for s in st_scr:
                    s[...] = jnp.zeros(s.shape, F32)
        vals = [_load_f32(r) for r in in_refs]
        sts = [s[...] for s in st_scr]
        o, ns = f(g, vals, sts)
        for r, v in zip(out_refs, o):
            r[...] = v.astype(r.dtype)
        for r, s in zip(sv_refs, sts):
            r[...] = s
        for s, v in zip(st_scr, ns):
            s[...] = v
        if comm is not None:
            @pl.when(last)
            def _():
                comm.finish(*cargs)

    all_outs = list(outs) + st_outs
    c_in_specs, c_args, c_out_specs, c_out_shapes, c_sems = _carry_specs(comm)
    res = pl.pallas_call(
        body, name=name, grid=tuple(grid), in_specs=[i.spec() for i in ins] + c_in_specs,
        out_specs=[o.spec() for o in all_outs] + c_out_specs,
        out_shape=[jax.ShapeDtypeStruct(o.shape, o.dtype) for o in all_outs] + c_out_shapes,
        scratch_shapes=[pltpu.VMEM(tuple(s), F32) for s in state_shapes] + c_sems,
        compiler_params=_CP(dimension_semantics=("arbitrary", "arbitrary")),
    )(*[i.arr for i in ins], *c_args)
    n_all = n_out + n_st
    return list(res[:n_out]), list(res[n_out:n_all]), list(res[n_all:])


def _op_bwd(name, f, grid, ins, outs, state_shapes, saved, douts, addto=None, comm=None):
    n_in, n_out, n_st = len(ins), len(outs), len(state_shapes)
    n_g = grid[1]
    addto = addto or {}
    diff = [k for k, i in enumerate(ins) if i.kind in ("x", "p")]
    add_idx = sorted(addto)
    st_ins = [In(s, o.block, o.imap, "c") for s, o in zip(saved, [_state_out(grid, s) for s in state_shapes])]
    dout_ins = [In(d, o.block, o.imap, "c") for d, o in zip(douts, outs)]
    add_ins = []
    for k in add_idx:
        i, a = ins[k], addto[k]
        blk = i.block if i.kind == "x" else i.block[:-2] + a.shape[-2:]
        add_ins.append(In(a, blk, i.gimap if i.kind == "x" else i.imap, "c"))
    g_outs = []
    for k in diff:
        i = ins[k]
        g_outs.append(Out(i.gshape, i.gdtype if i.kind == "x" else F32, i.block, i.gimap))

    def body(*refs):
        all_in, go_refs, ds_scr, cargs, first_step, last_step = _carry(comm, grid, refs, n_in + n_st + n_out + len(add_idx), len(diff), n_st)
        if comm is not None:
            @pl.when(first_step)
            def _():
                comm.start(*cargs)
        p = 0
        in_refs = all_in[p:p + n_in]; p += n_in
        sv_refs = all_in[p:p + n_st]; p += n_st
        do_refs = all_in[p:p + n_out]; p += n_out
        ad_refs = all_in[p:p + len(add_idx)]
        hh = pl.program_id(0)
        step = pl.program_id(1)
        g = n_g - 1 - step
        if n_st:
            @pl.when(step == 0)
            def _():
                for s in ds_scr:
                    s[...] = jnp.zeros(s.shape, F32)
        vals = [_load_f32(r) for r in in_refs]
        sts = [r[...] for r in sv_refs]

        def fw(dvals, states):
            full = list(vals)
            for k, v in zip(diff, dvals):
                full[k] = v
            o, ns = f(g, full, states)
            return list(o), list(ns)

        _, vjp = jax.vjp(fw, [vals[k] for k in diff], sts)
        cts = [r[...].astype(F32) for r in do_refs]
        dns = [s[...] for s in ds_scr]
        dvals, dsts = vjp((cts, dns))
        adds = dict(zip(add_idx, ad_refs))
        for k, r, dv in zip(diff, go_refs, dvals):
            i = ins[k]
            if i.kind == "x":
                if k in adds:
                    dv = dv + adds[k][...].astype(F32)
                r[...] = dv.astype(r.dtype)
            else:
                first = (step == 0) if i.per_h else jnp.logical_and(step == 0, hh == 0)

                @pl.when(first)
                def _(r=r, dv=dv, k=k):
                    r[...] = dv
                    if k in adds:
                        lead = adds[k].shape[0]
                        r[0:lead] += adds[k][...]

                @pl.when(jnp.logical_not(first))
                def _(r=r, dv=dv):
                    r[...] += dv
        for s, v in zip(ds_scr, dsts):
            s[...] = v
        if comm is not None:
            @pl.when(last_step)
            def _():
                comm.finish(*cargs)

    all_ins = list(ins) + st_ins + dout_ins + add_ins
    c_in_specs, c_args, c_out_specs, c_out_shapes, c_sems = _carry_specs(comm)
    res = pl.pallas_call(
        body, name=name, grid=tuple(grid), in_specs=[i.spec(n_g) for i in all_ins] + c_in_specs,
        out_specs=[o.spec(n_g) for o in g_outs] + c_out_specs,
        out_shape=[jax.ShapeDtypeStruct(o.shape, o.dtype) for o in g_outs] + c_out_shapes,
        scratch_shapes=[pltpu.VMEM(tuple(s), F32) for s in state_shapes] + c_sems,
        compiler_params=_CP(dimension_semantics=("arbitrary", "arbitrary")),
    )(*[i.arr for i in all_ins], *c_args)
    return list(res[:len(g_outs)]), list(res[len(g_outs):])


class Op:
    def __init__(self, name, f, grid, ins, outs, state_shapes=()):
        self.name, self.f, self.grid, self.ins, self.outs, self.state_shapes = name, f, grid, ins, outs, state_shapes
        self.saved = None

    def fwd(self, comm=None):
        res, self.saved, self.fwd_comm_out = _op_fwd(self.name + "_fwd", self.f, self.grid, self.ins, self.outs, self.state_shapes, comm)
        return res

    def bwd(self, douts, addto=None, comm=None):
        res, self.bwd_comm_out = _op_bwd(self.name + "_bwd", self.f, self.grid, self.ins, self.outs, self.state_shapes, self.saved, douts,
                                         addto, comm)
        return res


def _rows(arr, t, kind="x", gdtype=F32):
    return In(arr, (t, arr.shape[1]), lambda h, g: (g, 0), kind, gdtype=gdtype)


def _whole(arr, kind="p"):
    nd = arr.ndim
    return In(arr, arr.shape, lambda h, g: (0,) * nd, kind)


def _rows_out(s, n, t, dtype):
    return Out((s, n), dtype, (t, n), lambda h, g: (g, 0))


ROW_T = 256


def _rms_op(name, x, gain, out_dtype=BF16, gdtype=F32):
    s, n = x.shape

    def f(g, vals, sts):
        return [_rms(vals[0], vals[1])], []

    return Op(name, f, (1, s // ROW_T), [_rows(x, ROW_T, gdtype=gdtype), _whole(gain.reshape(1, n))], [_rows_out(s, n, ROW_T, out_dtype)])


def _mla_prep_op(qn, q1, q2, kn, kr, v, cos, sin):
    s = qn.shape[0]
    hd, half = MLA_HEADS, MLA_ROPE // 2

    def f(g, vals, sts):
        qn, q1, q2, kn, kr, v, cos, sin = vals
        cos_h, sin_h = jnp.tile(cos, (1, hd)), jnp.tile(sin, (1, hd))
        r1 = q1 * cos_h - q2 * sin_h
        r2 = q2 * cos_h + q1 * sin_h
        k1, k2 = kr[:, 0:half], kr[:, half:2 * half]
        kr1 = k1 * cos - k2 * sin
        kr2 = k2 * cos + k1 * sin
        zpad = jnp.zeros((qn.shape[0], LANE - MLA_QK), F32)
        qs, ks, vs = [], [], []
        for h in range(hd):
            a, b = h * MLA_NOPE, (h + 1) * MLA_NOPE
            c, d = h * half, (h + 1) * half
            qs.append(jnp.concatenate([qn[:, a:b], r1[:, c:d], r2[:, c:d], zpad], axis=1))
            ks.append(jnp.concatenate([kn[:, a:b], kr1, kr2, zpad], axis=1))
            vs.append(v[:, a:b])
        return [jnp.stack(qs, 0), jnp.stack(ks, 0), jnp.stack(vs, 0)], []

    ins = [_rows(qn, ROW_T, gdtype=BF16), _rows(q1, ROW_T, gdtype=BF16), _rows(q2, ROW_T, gdtype=BF16), _rows(kn, ROW_T, gdtype=BF16),
           _rows(kr, ROW_T, gdtype=BF16), _rows(v, ROW_T, gdtype=BF16), _rows(cos, ROW_T, "c"), _rows(sin, ROW_T, "c")]
    outs = [Out((hd, s, LANE), BF16, (hd, ROW_T, LANE), lambda h, g: (0, g, 0)),
            Out((hd, s, LANE), BF16, (hd, ROW_T, LANE), lambda h, g: (0, g, 0)),
            Out((hd, s, MLA_V), BF16, (hd, ROW_T, MLA_V), lambda h, g: (0, g, 0))]
    return Op("mla_prep", f, (1, s // ROW_T), ins, outs)


ATT_TQ = 256
ATT_LEVELS = 4


def _mla_attn_op(q, k, v, level):
    hd, s, _ = q.shape
    span = s // ATT_LEVELS
    klen = (level + 1) * span
    g_off = level * (span // ATT_TQ)
    scale = MLA_QK ** -0.5

    def f(g, vals, sts):
        q, k, v = vals
        sc = _bdot_nt(q, k) * scale
        r = lax.broadcasted_iota(jnp.int32, sc.shape, 0) + (g + g_off) * ATT_TQ
        c = lax.broadcasted_iota(jnp.int32, sc.shape, 1)
        sc = jnp.where(r >= c, sc, -1e30)
        m = lax.stop_gradient(jnp.max(sc, axis=-1, keepdims=True))
        p = jnp.exp(sc - m)
        p = p / jnp.sum(p, axis=-1, keepdims=True)
        return [_bdot(p, v)], []

    ins = [In(q, (None, ATT_TQ, LANE), lambda h, g: (h, g + g_off, 0), "x", gdtype=BF16, gshape=(hd, span, LANE), gimap=lambda h, g: (h, g, 0)),
           In(k, (None, klen, LANE), lambda h, g: (h, 0, 0), "p", per_h=True, gshape=(hd, klen, LANE)),
           In(v, (None, klen, MLA_V), lambda h, g: (h, 0, 0), "p", per_h=True, gshape=(hd, klen, MLA_V))]
    outs = [Out((hd, span, MLA_V), F32, (None, ATT_TQ, MLA_V), lambda h, g: (h, g, 0))]
    return Op(f"mla_attn{level}", f, (hd, span // ATT_TQ), ins, outs)


def _mla_post_op(o, gate):
    hd, s, _ = o.shape

    def f(g, vals, sts):
        o, gate = vals
        cat = jnp.concatenate([o[h] for h in range(hd)], axis=1)
        return [cat * _silu(gate)], []

    ins = [In(o, (hd, ROW_T, MLA_V), lambda h, g: (0, g, 0), "x"), _rows(gate, ROW_T, gdtype=BF16)]
    return Op("mla_post", f, (1, s // ROW_T), ins, [_rows_out(s, hd * MLA_V, ROW_T, BF16)])


def _gla_gate_op(gk, w2, b):
    s = gk.shape[0]

    def f(g, vals, sts):
        gk, w2, b = vals
        return [jax.nn.log_sigmoid(_bdot(gk, w2) + b) / GLA_TAU], []

    ins = [_rows(gk, ROW_T, gdtype=BF16), _whole(w2), _whole(b)]
    return Op("gla_gate", f, (1, s // ROW_T), ins, [_rows_out(s, GLA_HEADS * GLA_DK, ROW_T, F32)])


def _gla_core_op(q, k, v, gate, la, g_o):
    s = q.shape[0]
    c, nh = GLA_CHUNK, GLA_HEADS

    def f(g, vals, sts):
        q, k, v, gate, la, g_o = vals
        tri = _tri(c)
        b = _cumsum_rows(la)
        b_last = jnp.sum(la, axis=0, keepdims=True)
        qt = q * (GLA_DK ** -0.5) * jnp.exp(b)
        kt = k * jnp.exp(-b)
        kd = k * jnp.exp(b_last - b)
        ys, new_sts = [], []
        for h in range(nh):
            ks, vs = slice(h * GLA_DK, (h + 1) * GLA_DK), slice(h * GLA_DV, (h + 1) * GLA_DV)
            att = jnp.where(tri, _bdot_nt(qt[:, ks], kt[:, ks]), 0.0)
            o = _bdot(att, v[:, vs]) + _bdot_nt(qt[:, ks], sts[h])
            new_sts.append(jnp.exp(b_last[:, ks]) * sts[h] + _bdot_tn(v[:, vs], kd[:, ks]))
            ys.append(_rms(o, g_o) * _silu(gate[:, vs]))
        return [jnp.concatenate(ys, axis=1)], new_sts

    ins = [_rows(q, c, gdtype=BF16), _rows(k, c, gdtype=BF16), _rows(v, c, gdtype=BF16), _rows(gate, c, gdtype=BF16), _rows(la, c), _whole(g_o)]
    outs = [_rows_out(s, nh * GLA_DV, c, BF16)]
    return Op("gla_core", f, (1, s // c), ins, outs, [(GLA_DV, GLA_DK)] * nh)


LRU_T = 256


def _lru_op(gate, u, conv_w, conv_b, w_a, b_a, w_x, b_x, lam):
    s, w = u.shape
    t = LRU_T

    def f(g, vals, sts):
        gate, u, cw, cb, w_a, b_a, w_x, b_x, lam = vals
        u_prev, h_prev = sts
        uc = cb
        for kk in range(CONV_W):
            uc = uc + cw[kk] * _shift_rows(u, u_prev, CONV_W - 1 - kk)
        ra, ri = [], []
        for n in range(LRU_BLOCKS):
            blk = uc[:, n * LRU_BLOCK:(n + 1) * LRU_BLOCK]
            ra.append(_bdot(blk, w_a[n]))
            ri.append(_bdot(blk, w_x[n]))
        r = jax.nn.sigmoid(jnp.concatenate(ra, axis=1) + b_a)
        i = jax.nn.sigmoid(jnp.concatenate(ri, axis=1) + b_x)
        log_a = -LRU_C * r * jax.nn.softplus(-lam)
        a = jnp.exp(log_a)
        bb = jnp.sqrt(_one_minus_exp(2.0 * log_a)) * (i * uc)
        zero = jnp.zeros_like(a)
        sh = 1
        while sh < t:
            a_s = _shift_rows(a - 1.0, zero, sh) + 1.0
            b_s = _shift_rows(bb, zero, sh)
            bb = a * b_s + bb
            a = a * a_s
            sh *= 2
        hs = bb + a * h_prev
        last = (lax.broadcasted_iota(jnp.int32, hs.shape, 0) == t - 1).astype(F32)
        h_last = jnp.sum(hs * last, axis=0, keepdims=True)
        return [hs * _silu(gate)], [u, h_last]

    ins = [_rows(gate, t, gdtype=BF16), _rows(u, t, gdtype=BF16), _whole(conv_w), _whole(conv_b), _whole(w_a), _whole(b_a), _whole(w_x),
           _whole(b_x), _whole(lam)]
    return Op("lru_core", f, (1, s // t), ins, [_rows_out(s, w, t, BF16)], [(t, w), (1, w)])


def _ssd_conv_op(xbc, conv_w, conv_b):
    s, w = xbc.shape
    t = ROW_T
    n_x, n_b = SSD_INNER, SSD_GROUPS * SSD_STATE

    def f(g, vals, sts):
        xbc, cw, cb = vals
        acc = cb
        for kk in range(CONV_W):
            acc = acc + cw[kk] * _shift_rows(xbc, sts[0], CONV_W - 1 - kk)
        y = _silu(acc)
        return [y[:, :n_x], y[:, n_x:n_x + n_b], y[:, n_x + n_b:]], [xbc]

    ins = [_rows(xbc, t, gdtype=BF16), _whole(conv_w), _whole(conv_b)]
    outs = [_rows_out(s, n_x, t, F32), _rows_out(s, n_b, t, F32), _rows_out(s, n_b, t, F32)]
    return Op("ssd_conv", f, (1, s // t), ins, outs, [(t, w)])


SSD_L = 256


def _ssd_core_op(x, bm, cm, z, dt, dt_bias, a_log, d_skip, g_norm):
    s = x.shape[0]
    c, hg, p = SSD_L, SSD_HPG, SSD_P
    gw = hg * p

    def f(g, vals, sts):
        x, bm, cm, z, dtr, dt_bias, a_log, d_skip, g_norm = vals
        tri = _tri(c)
        dt = jax.nn.softplus(dtr + dt_bias)
        da = dt * (-jnp.exp(a_log))
        cs = _cumsum_rows(da)
        cs_last = jnp.sum(da, axis=0, keepdims=True)
        cs_t = jnp.transpose(jnp.concatenate([cs, jnp.zeros((c, LANE - hg), F32)], axis=1))
        cb = _bdot_nt(cm, bm)
        ys, new_st = [], []
        for h in range(hg):
            cs_h = cs[:, h:h + 1]
            cs_row = cs_t[h:h + 1, :]
            seg = jnp.where(tri, cs_h - cs_row, 0.0)
            lmat = jnp.where(tri, jnp.exp(seg), 0.0)
            x_h = x[:, h * p:(h + 1) * p]
            xdt = x_h * dt[:, h:h + 1]
            y_diag = _bdot(cb * lmat, xdt)
            decay = jnp.exp(cs_last[:, h:h + 1] - cs_h)
            states = _bdot_tn(xdt * decay, bm)
            y_off = _bdot_nt(cm, sts[h]) * jnp.exp(cs_h)
            new_st.append(jnp.exp(cs_last[:, h:h + 1]) * sts[h] + states)
            ys.append(y_diag + y_off + d_skip[:, h:h + 1] * x_h)
        y = jnp.concatenate(ys, axis=1) * _silu(z)
        return [_rms(y, g_norm)], new_st

    ins = [In(x, (c, gw), lambda h, g: (g, h), "x"), In(bm, (c, SSD_STATE), lambda h, g: (g, h), "x"),
           In(cm, (c, SSD_STATE), lambda h, g: (g, h), "x"), In(z, (c, gw), lambda h, g: (g, h), "x", gdtype=BF16),
           In(dt, (None, c, hg), lambda h, g: (h, g, 0), "x"),
           In(dt_bias, (None, 1, hg), lambda h, g: (h, 0, 0), "p", per_h=True),
           In(a_log, (None, 1, hg), lambda h, g: (h, 0, 0), "p", per_h=True),
           In(d_skip, (None, 1, hg), lambda h, g: (h, 0, 0), "p", per_h=True),
           In(g_norm, (1, gw), lambda h, g: (0, h), "p", per_h=True)]
    outs = [Out((s, SSD_INNER), BF16, (c, gw), lambda h, g: (g, h))]
    return Op("ssd_core", f, (SSD_GROUPS, s // c), ins, outs, [(p, SSD_STATE)] * hg)


def _loss_op(h, target, final_g):
    s, n = h.shape
    t = ROW_T
    n_g = s // t

    def body(h_ref, t_ref, g_ref, loss_ref, dh_ref, dg_ref):
        step = pl.program_id(0)

        def lossf(hv, gv):
            err = _rms(hv, gv) - t_ref[...]
            return 0.5 * jnp.sum(jnp.mean(err * err, axis=-1))

        l, (dh, dg) = jax.value_and_grad(lossf, argnums=(0, 1))(h_ref[...], g_ref[...])
        dh_ref[...] = dh

        @pl.when(step == 0)
        def _():
            loss_ref[...] = jnp.zeros(loss_ref.shape, F32)
            dg_ref[...] = jnp.zeros(dg_ref.shape, F32)

        loss_ref[...] += jnp.full(loss_ref.shape, l, F32)
        dg_ref[...] += dg

    row = pl.BlockSpec((t, n), lambda g: (g, 0))
    one = pl.BlockSpec((1, n), lambda g: (0, 0))
    return pl.pallas_call(
        body, name="loss_head", grid=(n_g,), in_specs=[row, row, one],
        out_specs=[pl.BlockSpec((1, LANE), lambda g: (0, 0)), row, one],
        out_shape=[jax.ShapeDtypeStruct((1, LANE), F32), jax.ShapeDtypeStruct((s, n), F32), jax.ShapeDtypeStruct((1, n), F32)],
        compiler_params=_CP(dimension_semantics=("arbitrary",)),
    )(h, target, final_g.reshape(1, n))


def _pad_cols(w, n):
    return jnp.pad(w, ((0, 0), (0, n - w.shape[1])))


def _pad_rows(w, n):
    return jnp.pad(w, ((0, n - w.shape[0]), (0, 0)))


def _proj_bwd(tag, u, dps, ws):
    du = None
    for i, (dp, w) in enumerate(zip(dps, ws)):
        du = _mm(f"{tag}_du{i}", dp, w, tb=True, add=du)
    dws = [_mm(f"{tag}_dw{i}", u, dp, ta=True, out_dtype=BF16) for i, dp in enumerate(dps)]
    return du, dws


def _mla_layer(h, norm_g, w, cos, sin, fwd_comms=(None,) * ATT_LEVELS):
    bf = lambda a: a.astype(BF16)
    w_in, w_uq, w_ukv = w["mla_w_in"], w["mla_w_uq"], w["mla_w_ukv"]
    a0, a1, a2 = MLA_Q_RANK, MLA_Q_RANK + MLA_KV_RANK, MLA_Q_RANK + MLA_KV_RANK + MLA_ROPE
    w_cq, w_ckv, w_kr, w_g = bf(w_in[:, :a0]), bf(w_in[:, a0:a1]), bf(_pad_cols(w_in[:, a1:a2], LANE)), bf(w_in[:, a2:])
    uq = w_uq.reshape(MLA_Q_RANK, MLA_HEADS, MLA_QK)
    half = MLA_ROPE // 2
    w_qn = bf(uq[:, :, :MLA_NOPE].reshape(MLA_Q_RANK, -1))
    w_q1 = bf(uq[:, :, MLA_NOPE:MLA_NOPE + half].reshape(MLA_Q_RANK, -1))
    w_q2 = bf(uq[:, :, MLA_NOPE + half:].reshape(MLA_Q_RANK, -1))
    ukv = w_ukv.reshape(MLA_KV_RANK, MLA_HEADS, MLA_NOPE + MLA_V)
    w_kn = bf(ukv[:, :, :MLA_NOPE].reshape(MLA_KV_RANK, -1))
    w_v = bf(ukv[:, :, MLA_NOPE:].reshape(MLA_KV_RANK, -1))
    w_out = bf(w["mla_w_out"])

    n0 = _rms_op("mla_norm", h, norm_g)
    u, = n0.fwd()
    cq, ckv, kr, gate = (_mm(f"mla_in{i}", u, wi) for i, wi in enumerate((w_cq, w_ckv, w_kr, w_g)))
    nq = _rms_op("mla_qnorm", cq, w["mla_g_q"], gdtype=BF16)
    nkv = _rms_op("mla_kvnorm", ckv, w["mla_g_kv"], gdtype=BF16)
    qn_, = nq.fwd()
    kvn_, = nkv.fwd()
    qn, q1, q2 = (_mm(f"mla_uq{i}", qn_, wi) for i, wi in enumerate((w_qn, w_q1, w_q2)))
    kn, v = (_mm(f"mla_ukv{i}", kvn_, wi) for i, wi in enumerate((w_kn, w_v)))
    prep = _mla_prep_op(qn, q1, q2, kn, kr, v, cos, sin)
    qh, kh, vh = prep.fwd()
    attns = [_mla_attn_op(qh, kh, vh, lv) for lv in range(ATT_LEVELS)]
    o = jnp.concatenate([a.fwd(cm)[0] for a, cm in zip(attns, fwd_comms)], axis=1)
    post = _mla_post_op(o, gate)
    y, = post.fwd()
    h_out = _mm("mla_out", y, w_out, add=h)

    def bwd(dh, make_comms=None):
        dy = _mm("mla_out_dy", dh, w_out, tb=True, out_dtype=BF16)
        d_w_out = _mm("mla_out_dw", y, dh, ta=True, out_dtype=BF16)
        comms = make_comms(d_w_out) if make_comms is not None else (None,) * ATT_LEVELS
        do, dgate = post.bwd([dy])
        span = do.shape[1] // ATT_LEVELS
        dq_parts, dkh, dvh = [], None, None
        for lv, (a, cm) in enumerate(zip(attns, comms)):
            dq, dkh, dvh = a.bwd([do[:, lv * span:(lv + 1) * span]], addto=None if lv == 0 else {1: dkh, 2: dvh}, comm=cm)
            dq_parts.append(dq)
        dqh = jnp.concatenate(dq_parts, axis=1)
        dqn, dq1, dq2, dkn, dkr, dv = prep.bwd([dqh, dkh, dvh])
        dqn_, d_uq = _proj_bwd("mla_uq", qn_, (dqn, dq1, dq2), (w_qn, w_q1, w_q2))
        dkvn_, d_ukv = _proj_bwd("mla_ukv", kvn_, (dkn, dv), (w_kn, w_v))
        dcq, d_g_q = nq.bwd([dqn_])
        dckv, d_g_kv = nkv.bwd([dkvn_])
        du, d_in = _proj_bwd("mla_in", u, (dcq, dckv, dkr, dgate), (w_cq, w_ckv, w_kr, w_g))
        dh_in, d_norm = n0.bwd([du], addto={0: dh})
        shp = (MLA_Q_RANK, MLA_HEADS, -1)
        g_uq = jnp.concatenate([d_uq[0].reshape(shp), d_uq[1].reshape(shp), d_uq[2].reshape(shp)], axis=2).reshape(MLA_Q_RANK, -1)
        shp = (MLA_KV_RANK, MLA_HEADS, -1)
        g_ukv = jnp.concatenate([d_ukv[0].reshape(shp), d_ukv[1].reshape(shp)], axis=2).reshape(MLA_KV_RANK, -1)
        g_in = jnp.concatenate([d_in[0], d_in[1], d_in[2][:, :MLA_ROPE], d_in[3]], axis=1)
        return dh_in, d_norm, {"mla_w_in": g_in, "mla_g_q": d_g_q.reshape(-1), "mla_w_uq": g_uq, "mla_g_kv": d_g_kv.reshape(-1),
                               "mla_w_ukv": g_ukv, "mla_w_out": d_w_out}, [a.bwd_comm_out for a in attns]

    return h_out, bwd, [a.fwd_comm_out for a in attns]


def _gla_layer(h, norm_g, w, fwd_comm=None):
    bf = lambda a: a.astype(BF16)
    w_in = w["gla_w_in"]
    nk, nv = GLA_HEADS * GLA_DK, GLA_HEADS * GLA_DV
    cuts = (0, nk, 2 * nk, 2 * nk + nv, 2 * nk + 2 * nv)
    w_q, w_k, w_v, w_g = (bf(w_in[:, cuts[i]:cuts[i + 1]]) for i in range(4))
    w_gk = bf(_pad_cols(w_in[:, cuts[4]:], LANE))
    w2 = _pad_rows(w["gla_w_gk2"], LANE)
    b_gk = w["gla_b_gk"].reshape(1, -1)
    g_o = w["gla_g_o"].reshape(1, -1)
    w_out = bf(w["gla_w_out"])

    n0 = _rms_op("gla_norm", h, norm_g)
    u, = n0.fwd()
    q, k, v, gate, gk = (_mm(f"gla_in{i}", u, wi) for i, wi in enumerate((w_q, w_k, w_v, w_g, w_gk)))
    gop = _gla_gate_op(gk, w2, b_gk)
    la, = gop.fwd()
    core = _gla_core_op(q, k, v, gate, la, g_o)
    y, = core.fwd(fwd_comm)
    h_out = _mm("gla_out", y, w_out, add=h)

    def bwd(dh, make_comm=None):
        dy = _mm("gla_out_dy", dh, w_out, tb=True, out_dtype=BF16)
        d_w_out = _mm("gla_out_dw", y, dh, ta=True, out_dtype=BF16)
        dq, dk, dv, dgate, dla, d_g_o = core.bwd([dy], comm=None if make_comm is None else make_comm(d_w_out))
        dgk, d_w2, d_b = gop.bwd([dla])
        du, d_in = _proj_bwd("gla_in", u, (dq, dk, dv, dgate, dgk), (w_q, w_k, w_v, w_g, w_gk))
        dh_in, d_norm = n0.bwd([du], addto={0: dh})
        g_in = jnp.concatenate([d_in[0], d_in[1], d_in[2], d_in[3], d_in[4][:, :GLA_RANK]], axis=1)
        return dh_in, d_norm, {"gla_w_in": g_in, "gla_w_gk2": d_w2[:GLA_RANK], "gla_b_gk": d_b.reshape(-1), "gla_g_o": d_g_o.reshape(-1),
                               "gla_w_out": d_w_out}, core.bwd_comm_out

    return h_out, bwd, core.fwd_comm_out


def _lru_layer(h, norm_g, w, fwd_comm=None):
    bf = lambda a: a.astype(BF16)
    w_in = w["lru_w_in"]
    w_g, w_u = bf(w_in[:, :LRU_WIDTH]), bf(w_in[:, LRU_WIDTH:])
    row = lambda a: a.reshape(1, -1)
    w_out = bf(w["lru_w_out"])

    n0 = _rms_op("lru_norm", h, norm_g)
    u_, = n0.fwd()
    gate, u = (_mm(f"lru_in{i}", u_, wi) for i, wi in enumerate((w_g, w_u)))
    core = _lru_op(gate, u, w["lru_conv_w"].reshape(CONV_W, 1, -1), row(w["lru_conv_b"]), w["lru_w_a"], row(w["lru_b_a"]), w["lru_w_x"],
                   row(w["lru_b_x"]), row(w["lru_lam"]))
    y, = core.fwd(fwd_comm)
    h_out = _mm("lru_out", y, w_out, add=h)

    def bwd(dh, make_comm=None):
        dy = _mm("lru_out_dy", dh, w_out, tb=True, out_dtype=BF16)
        d_w_out = _mm("lru_out_dw", y, dh, ta=True, out_dtype=BF16)
        dgate, du, d_cw, d_cb, d_wa, d_ba, d_wx, d_bx, d_lam = core.bwd([dy], comm=None if make_comm is None else make_comm(d_w_out))
        du_, d_in = _proj_bwd("lru_in", u_, (dgate, du), (w_g, w_u))
        dh_in, d_norm = n0.bwd([du_], addto={0: dh})
        return dh_in, d_norm, {"lru_w_in": jnp.concatenate(d_in, axis=1), "lru_conv_w": d_cw.reshape(CONV_W, -1), "lru_conv_b": d_cb.reshape(-1),
                               "lru_w_a": d_wa, "lru_b_a": d_ba.reshape(-1), "lru_w_x": d_wx, "lru_b_x": d_bx.reshape(-1),
                               "lru_lam": d_lam.reshape(-1), "lru_w_out": d_w_out}, core.bwd_comm_out

    return h_out, bwd, core.fwd_comm_out


def _ssd_layer(h, norm_g, w, fwd_comm=None, late_w_out=None):
    bf = lambda a: a.astype(BF16)
    s = h.shape[0]
    w_in = w["ssd_w_in"]
    conv_dim = SSD_INNER + 2 * SSD_GROUPS * SSD_STATE
    w_z, w_xbc = bf(w_in[:, :SSD_INNER]), bf(w_in[:, SSD_INNER:SSD_INNER + conv_dim])
    w_dt = bf(_pad_cols(w_in[:, SSD_INNER + conv_dim:], LANE))
    grp = lambda a: a.reshape(SSD_GROUPS, 1, SSD_HPG)

    n0 = _rms_op("ssd_norm", h, norm_g)
    u, = n0.fwd()
    z, xbc, dtp = (_mm(f"ssd_in{i}", u, wi) for i, wi in enumerate((w_z, w_xbc, w_dt)))
    conv = _ssd_conv_op(xbc, w["ssd_conv_w"].reshape(CONV_W, 1, -1), w["ssd_conv_b"].reshape(1, -1))
    x, bm, cm = conv.fwd()
    dt = dtp[:, :SSD_HEADS].reshape(s, SSD_GROUPS, SSD_HPG).transpose(1, 0, 2)
    core = _ssd_core_op(x, bm, cm, z, dt, grp(w["ssd_dt_bias"]), grp(w["ssd_a_log"]), grp(w["ssd_d"]), w["ssd_g_norm"].reshape(1, -1))
    y, = core.fwd(fwd_comm)
    w_out = bf(w["ssd_w_out"]) if late_w_out is None else late_w_out(core.fwd_comm_out)
    h_out = _mm("ssd_out", y, w_out, add=h)

    def bwd(dh, make_comm=None):
        dy = _mm("ssd_out_dy", dh, w_out, tb=True, out_dtype=BF16)
        d_w_out = _mm("ssd_out_dw", y, dh, ta=True, out_dtype=BF16)
        dx, dbm, dcm, dz, ddt, d_dtb, d_alog, d_d, d_gn = core.bwd([dy], comm=None if make_comm is None else make_comm(d_w_out))
        dxbc, d_cw, d_cb = conv.bwd([dx, dbm, dcm])
        ddtp = _pad_cols(ddt.transpose(1, 0, 2).reshape(s, SSD_HEADS), LANE).astype(BF16)
        du, d_in = _proj_bwd("ssd_in", u, (dz, dxbc, ddtp), (w_z, w_xbc, w_dt))
        dh_in, d_norm = n0.bwd([du], addto={0: dh})
        g_in = jnp.concatenate([d_in[0], d_in[1], d_in[2][:, :SSD_HEADS]], axis=1)
        return dh_in, d_norm, {"ssd_w_in": g_in, "ssd_conv_w": d_cw.reshape(CONV_W, -1), "ssd_conv_b": d_cb.reshape(-1),
                               "ssd_dt_bias": d_dtb.reshape(-1), "ssd_a_log": d_alog.reshape(-1), "ssd_d": d_d.reshape(-1),
                               "ssd_g_norm": d_gn.reshape(-1), "ssd_w_out": d_w_out}, core.bwd_comm_out

    return h_out, bwd


def _rope_tables(positions):
    inv_freq = ROPE_THETA ** (-jnp.arange(0, MLA_ROPE, 2, dtype=F32) / MLA_ROPE)
    ang = positions.astype(F32)[:, None] * inv_freq
    return jnp.cos(ang), jnp.sin(ang)


WEIGHTS = ["norm_g", "final_g", "mla_w_in", "mla_g_q", "mla_w_uq", "mla_g_kv", "mla_w_ukv", "mla_w_out", "gla_w_in", "gla_w_gk2", "gla_b_gk",
           "gla_g_o", "gla_w_out", "lru_w_in", "lru_conv_w", "lru_conv_b", "lru_w_a", "lru_b_a", "lru_w_x", "lru_b_x", "lru_lam", "lru_w_out",
           "ssd_w_in", "ssd_conv_w", "ssd_conv_b", "ssd_dt_bias", "ssd_a_log", "ssd_d", "ssd_g_norm", "ssd_w_out"]
BIG = ["mla_w_in", "mla_w_uq", "mla_w_ukv", "mla_w_out", "gla_w_in", "gla_w_out", "lru_w_in", "lru_w_out", "ssd_w_in", "ssd_w_out"]
SMALL = ["gla_w_gk2", "gla_b_gk", "gla_g_o", "lru_conv_w", "lru_conv_b", "lru_b_a", "lru_b_x", "lru_lam", "ssd_conv_w", "ssd_conv_b", "ssd_g_norm"]
REPL = ["norm_g", "final_g", "mla_g_q", "mla_g_kv", "lru_w_a", "lru_w_x", "ssd_dt_bias", "ssd_a_log", "ssd_d"]
REPL_EARLY = ["lru_w_a", "lru_w_x"]
REPL_LATE = [n for n in REPL if n not in REPL_EARLY]
N_CHIPS, N_DEV = 4, 8
PACK_W = 1024
ADAM_ROWS = 256
SMALL_ROWS = 64


def _shard_axis(name):
    return 0 if name.endswith("_w_out") else -1


def _pack(arrs, dtype, row_mult):
    flat = jnp.concatenate([a.reshape(-1).astype(dtype) for a in arrs])
    per = PACK_W * row_mult
    total = -(-flat.shape[0] // per) * per
    return jnp.pad(flat, (0, total - flat.shape[0])).reshape(-1, PACK_W)


def _unpack(buf, shapes):
    flat = buf.reshape(-1)
    out, off = [], 0
    for s in shapes:
        n = math.prod(s)
        out.append(flat[off:off + n].reshape(s))
        off += n
    return out


def _mesh_pos():
    return lax.axis_index("x"), lax.axis_index("y"), lax.axis_index("c")


class GatherComm:
    def __init__(self, ops):
        self.ops = list(ops)
        n = len(self.ops)
        assert all(o.ndim == 2 and o.shape[0] % 32 == 0 for o in self.ops), [o.shape for o in self.ops]
        self.out_shapes = [jax.ShapeDtypeStruct((N_CHIPS,) + o.shape, o.dtype) for o in self.ops]
        self.sem_shapes = [pltpu.SemaphoreType.DMA((6 * n,)), pltpu.SemaphoreType.DMA((6 * n,)), pltpu.SemaphoreType.DMA((n,))]

    def _copies(self, srcs, dsts, sems):
        send_sems, recv_sems, local_sems = sems
        n = len(self.ops)
        x, y, c = _mesh_pos()
        me_id, sibling = (x, y, c), (x, y, 1 - c)
        chips = [(1 - x, y), (x, 1 - y), (1 - x, 1 - y)]
        mine = 2 * x + y

        def half(i, cc):
            h = self.ops[i].shape[0] // 2
            return pl.ds(cc * h, h)

        def copy(i, k, src, slot, cc, to):
            return pltpu.make_async_remote_copy(src_ref=src, dst_ref=dsts[i].at[slot, half(i, cc)], send_sem=send_sems.at[i * 6 + k],
                                                recv_sem=recv_sems.at[i * 6 + k], device_id=to, device_id_type=pl.DeviceIdType.MESH)

        local = [pltpu.make_async_copy(srcs[i], dsts[i].at[mine], local_sems.at[i]) for i in range(n)]
        first, ici_recvs, passed, sib_recvs = [], [], [], []
        for i in range(n):
            my_half = srcs[i].at[half(i, c)]
            for k, (px, py) in enumerate(chips):
                slot = 2 * px + py
                first.append(copy(i, k, my_half, mine, c, (px, py, c)))
                ici_recvs.append(copy(i, k, my_half, slot, c, me_id))
                passed.append(copy(i, 3 + k, dsts[i].at[slot, half(i, c)], slot, c, sibling))
                sib_recvs.append(copy(i, 3 + k, my_half, slot, 1 - c, me_id))
        return local, first, ici_recvs, passed, sib_recvs

    def start(self, srcs, dsts, sems):
        local, first, _, _, _ = self._copies(srcs, dsts, sems)
        for cp in local + first:
            cp.start()

    def finish(self, srcs, dsts, sems):
        local, first, ici_recvs, passed, sib_recvs = self._copies(srcs, dsts, sems)
        for rc, fw in zip(ici_recvs, passed):
            rc.wait_recv()
            fw.start()
        for cp in sib_recvs:
            cp.wait_recv()
        for cp in first + passed:
            cp.wait_send()
        for cp in local:
            cp.wait()


class ExchangeComm:
    def __init__(self, chip_ops, all_ops=()):
        self.ops = list(chip_ops) + list(all_ops)
        self.per_chip = (True,) * len(chip_ops) + (False,) * len(all_ops)
        n = len(self.ops)
        self.out_shapes = [jax.ShapeDtypeStruct((N_DEV,) + o.shape[-2:], o.dtype) for o in self.ops]
        self.sem_shapes = [pltpu.SemaphoreType.DMA((7 * n,)), pltpu.SemaphoreType.DMA((7 * n,)), pltpu.SemaphoreType.DMA((n,))]

    def _copies(self, srcs, dsts, sems):
        send_sems, recv_sems, local_sems = sems
        n, per_chip = len(self.ops), self.per_chip
        x, y, c = _mesh_pos()
        me_id, sibling = (x, y, c), (x, y, 1 - c)
        chips = [(1 - x, y), (x, 1 - y), (1 - x, 1 - y)]

        def dev(px, py, pc):
            return 4 * px + 2 * py + pc

        def part(i, px, py):
            return srcs[i].at[2 * px + py] if per_chip[i] else srcs[i]

        def copy(i, k, src, slot, to):
            return pltpu.make_async_remote_copy(src_ref=src, dst_ref=dsts[i].at[slot], send_sem=send_sems.at[i * 7 + k],
                                                recv_sem=recv_sems.at[i * 7 + k], device_id=to, device_id_type=pl.DeviceIdType.MESH)

        me = dev(x, y, c)
        local = [pltpu.make_async_copy(part(i, x, y), dsts[i].at[me], local_sems.at[i]) for i in range(n)]
        first, ici_recvs, passed, sib_recvs = [], [], [], []
        for i in range(n):
            first.append(copy(i, 0, part(i, x, y), me, sibling))
            first += [copy(i, 1 + k, part(i, px, py), me, (px, py, c)) for k, (px, py) in enumerate(chips)]
            sib_recvs.append(copy(i, 0, part(i, x, y), dev(x, y, 1 - c), me_id))
            for k, (px, py) in enumerate(chips):
                slot = dev(px, py, c)
                ici_recvs.append(copy(i, 1 + k, part(i, x, y), slot, me_id))
                passed.append(copy(i, 4 + k, dsts[i].at[slot], slot, sibling))
                sib_recvs.append(copy(i, 4 + k, part(i, x, y), dev(px, py, 1 - c), me_id))
        return local, first, ici_recvs, passed, sib_recvs

    def start(self, srcs, dsts, sems):
        local, first, _, _, _ = self._copies(srcs, dsts, sems)
        for cp in local + first:
            cp.start()

    def finish(self, srcs, dsts, sems):
        local, first, ici_recvs, passed, sib_recvs = self._copies(srcs, dsts, sems)
        for rc, fw in zip(ici_recvs, passed):
            rc.wait_recv()
            fw.start()
        for cp in sib_recvs:
            cp.wait_recv()
        for cp in first + passed:
            cp.wait_send()
        for cp in local:
            cp.wait()


def _run_comm(name, comm):
    n = len(comm.ops)

    def body(*refs):
        srcs, dsts, sems = refs[:n], refs[n:2 * n], refs[2 * n:]
        comm.start(srcs, dsts, sems)
        comm.finish(srcs, dsts, sems)

    any_spec = pl.BlockSpec(memory_space=pl.ANY)
    return pl.pallas_call(body, name=name, in_specs=[any_spec] * n, out_specs=[any_spec] * n, out_shape=comm.out_shapes,
                          scratch_shapes=comm.sem_shapes)(*comm.ops)


def _adamw(name, parts, w, m, v, lead=False):
    plist = list(parts) if isinstance(parts, (list, tuple)) else [parts]
    n_p = len(plist)
    rows, cols = w.shape[-2:]
    t = next(c for c in (ADAM_ROWS, ADAM_ROWS // 2, SMALL_ROWS) if all(p.shape[1] % c == 0 for p in plist))
    starts = [sum(p.shape[1] for p in plist[:k]) // t for k in range(n_p)]
    counts = [p.shape[1] // t for p in plist]
    assert sum(p.shape[1] for p in plist) == rows, (name, rows)
    c1 = 1.0 - ADAM_B1 ** ADAM_STEP
    c2 = 1.0 - ADAM_B2 ** ADAM_STEP

    def body(*refs):
        p_refs = refs[:n_p]
        w_ref, m_ref, v_ref, g_ref, d_ref, nm_ref, nv_ref = refs[n_p:]
        g = None
        for k, p_ref in enumerate(p_refs):
            gk = p_ref[0].astype(F32)
            for d in range(1, N_DEV):
                gk = gk + p_ref[d].astype(F32)
            g = gk if g is None else jnp.where(pl.program_id(0) >= starts[k], gk, g)
        nm = ADAM_B1 * m_ref[...] + (1.0 - ADAM_B1) * g
        nv = ADAM_B2 * v_ref[...] + (1.0 - ADAM_B2) * (g * g)
        g_ref[...] = g
        nm_ref[...] = nm
        nv_ref[...] = nv
        d_ref[...] = -ADAM_LR * ((nm / c1) / (jnp.sqrt(nv / c2) + ADAM_EPS) + ADAM_WD * w_ref[...])

    row = pl.BlockSpec((None, t, cols), lambda i: (0, i, 0)) if lead else pl.BlockSpec((t, cols), lambda i: (i, 0))
    return pl.pallas_call(
        body, name=name, grid=(rows // t,),
        in_specs=[pl.BlockSpec((N_DEV, t, cols), lambda i, lo=lo, n=n: (0, jnp.clip(i - lo, 0, n - 1), 0)) for lo, n in zip(starts, counts)]
        + [row, row, row],
        out_specs=[row] * 4, out_shape=[jax.ShapeDtypeStruct(w.shape, F32)] * 4,
        compiler_params=_CP(dimension_semantics=("parallel",)),
    )(*plist, w, m, v)


def _train_step(x, positions, target, wts, ms, vs, raw):
    small_shapes = [wts[n].shape for n in SMALL]

    big_of = {tag: [n for n in BIG if n.startswith(tag)] for tag in ("mla", "gla", "lru", "ssd")}
    full = {n: wts[n] for n in REPL}

    def gather_comm(names, extra=()):
        return GatherComm([wts[n].astype(BF16) for n in names] + list(extra))

    def assemble(names, got):
        for k, n in enumerate(names):
            full[n] = jnp.concatenate([got[k][j] for j in range(N_CHIPS)], axis=_shard_axis(n))

    got = _run_comm("gather_first", gather_comm(big_of["mla"], [_pack([wts[n] for n in SMALL], F32, SMALL_ROWS)]))
    assemble(big_of["mla"], got)
    per_chip_small = [_unpack(got[-1][j], small_shapes) for j in range(N_CHIPS)]
    for k, n in enumerate(SMALL):
        full[n] = jnp.concatenate([per_chip_small[j][k] for j in range(N_CHIPS)], axis=_shard_axis(n))

    cos, sin = _rope_tables(positions)
    ng = full["norm_g"]
    by_level = ["gla_w_out", "lru_w_out", "lru_w_in", "gla_w_in"]
    h1, b0, got = _mla_layer(x, ng[0], full, cos, sin, fwd_comms=[gather_comm([n]) for n in by_level])
    for n, g1 in zip(by_level, got):
        assemble([n], g1)

    def joined(got_k, axis):
        return jnp.concatenate([got_k[j] for j in range(N_CHIPS)], axis=axis)

    ssd_in = wts["ssd_w_in"].astype(BF16)
    half = ssd_in.shape[0] // 2
    h2, b1, got = _gla_layer(h1, ng[1], full, fwd_comm=GatherComm([ssd_in[:half]]))
    top = joined(got[0], -1)
    h3, b2, got = _lru_layer(h2, ng[2], full, fwd_comm=GatherComm([ssd_in[half:]]))
    full["ssd_w_in"] = jnp.concatenate([top, joined(got[0], -1)], axis=0)
    h4, b3 = _ssd_layer(h3, ng[3], full, fwd_comm=gather_comm(["ssd_w_out"]), late_w_out=lambda got: joined(got[0], 0))
    loss, dh, d_final = _loss_op(h4, target, full["final_g"])
    loss = loss[0, 0]
    grads = {"final_g": d_final.reshape(-1)}
    d_norms = [None] * 4

    def shards_of(n, g):
        return jnp.stack(jnp.split(g.astype(BF16), N_CHIPS, axis=_shard_axis(n)))

    def shards(n):
        return shards_of(n, grads[n])

    parts = {}
    dh, d_norms[3], gw, got = b3(dh, make_comm=lambda dw: ExchangeComm([shards_of("ssd_w_out", dw)]))
    parts["ssd_w_out"] = got[0]
    grads.update(gw)
    ssd_in_g = shards("ssd_w_in")
    half = ssd_in_g.shape[1] // 2
    dh, d_norms[2], gw, got = b2(dh, make_comm=lambda dw: ExchangeComm([ssd_in_g[:, :half], shards_of("lru_w_out", dw)]))
    parts["ssd_w_in"], parts["lru_w_out"] = [got[0]], got[1]
    grads.update(gw)
    dh, d_norms[1], gw, got = b1(dh, make_comm=lambda dw: ExchangeComm([ssd_in_g[:, half:]]))
    parts["ssd_w_in"].append(got[0])
    grads.update(gw)
    repl_early = _pack([grads[n] for n in REPL_EARLY], BF16, SMALL_ROWS)

    def mla_comms(dw):
        return [ExchangeComm([shards_of("mla_w_out", dw)]), ExchangeComm([shards("gla_w_out")], [repl_early]),
                ExchangeComm([shards("lru_w_in")]), ExchangeComm([shards("gla_w_in")])]

    dx, d_norms[0], gw, got = b0(dh, make_comms=mla_comms)
    parts["mla_w_out"], parts["gla_w_out"], parts["lru_w_in"], parts["gla_w_in"] = got[0][0], got[1][0], got[2][0], got[3][0]
    repl_early_parts = got[1][1]
    grads.update(gw)
    grads["norm_g"] = jnp.concatenate(d_norms, axis=0)
    psmall = jnp.stack([_pack([jnp.split(grads[n], N_CHIPS, axis=_shard_axis(n))[j] for n in SMALL], F32, SMALL_ROWS) for j in range(N_CHIPS)])
    prepl = _pack([grads[n] for n in REPL_LATE], F32, SMALL_ROWS)
    late = [n for n in big_of["mla"] if n != "mla_w_out"]
    late_parts = _run_comm("exchange_last", ExchangeComm([shards(n) for n in late] + [psmall], [prepl]))
    parts.update(zip(late, late_parts))

    out = {}
    kinds = ("grad", "delta", "new_m", "new_v")
    for n in BIG:
        for kind, a in zip(kinds, _adamw("adam_" + n, parts[n], *(r[n] for r in raw), lead=True)):
            out[kind, n] = a
    for tag, names, p in (("adam_small", SMALL, late_parts[-2]), ("adam_repl_early", REPL_EARLY, repl_early_parts),
                          ("adam_repl_late", REPL_LATE, late_parts[-1])):
        shapes = [wts[n].shape for n in names]
        packed = [_pack([d[n] for n in names], F32, SMALL_ROWS) for d in (wts, ms, vs)]
        for kind, buf in zip(kinds, _adamw(tag, p, *packed)):
            for n, a in zip(names, _unpack(buf, shapes)):
                out[kind, n] = a
    loss = lax.psum(loss, ("x", "y", "c"))
    return loss, dx, out


def kernel(x, positions, norm_g, final_g, mla_w_in, mla_g_q, mla_w_uq, mla_g_kv, mla_w_ukv, mla_w_out, gla_w_in, gla_w_gk2, gla_b_gk, gla_g_o, gla_w_out, lru_w_in, lru_conv_w, lru_conv_b, lru_w_a, lru_b_a, lru_w_x, lru_b_x, lru_lam, lru_w_out, ssd_w_in, ssd_conv_w, ssd_conv_b, ssd_dt_bias, ssd_a_log, ssd_d, ssd_g_norm, ssd_w_out, loss_target, m_norm_g, m_final_g, m_mla_w_in, m_mla_g_q, m_mla_w_uq, m_mla_g_kv, m_mla_w_ukv, m_mla_w_out, m_gla_w_in, m_gla_w_gk2, m_gla_b_gk, m_gla_g_o, m_gla_w_out, m_lru_w_in, m_lru_conv_w, m_lru_conv_b, m_lru_w_a, m_lru_b_a, m_lru_w_x, m_lru_b_x, m_lru_lam, m_lru_w_out, m_ssd_w_in, m_ssd_conv_w, m_ssd_conv_b, m_ssd_dt_bias, m_ssd_a_log, m_ssd_d, m_ssd_g_norm, m_ssd_w_out, v_norm_g, v_final_g, v_mla_w_in, v_mla_g_q, v_mla_w_uq, v_mla_g_kv, v_mla_w_ukv, v_mla_w_out, v_gla_w_in, v_gla_w_gk2, v_gla_b_gk, v_gla_g_o, v_gla_w_out, v_lru_w_in, v_lru_conv_w, v_lru_conv_b, v_lru_w_a, v_lru_b_a, v_lru_w_x, v_lru_b_x, v_lru_lam, v_lru_w_out, v_ssd_w_in, v_ssd_conv_w, v_ssd_conv_b, v_ssd_dt_bias, v_ssd_a_log, v_ssd_d, v_ssd_g_norm, v_ssd_w_out):
    given = dict(locals())
    stacked = [n for n in WEIGHTS if n not in ("norm_g", "final_g")]

    def blocks(prefix):
        return {n: (given[prefix + n][0] if n in stacked else given[prefix + n]) for n in WEIGHTS}

    raw = [{n: given[prefix + n] for n in BIG} for prefix in ("", "m_", "v_")]
    loss, dx, out = _train_step(x[0], positions[0], loss_target[0], blocks(""), blocks("m_"), blocks("v_"), raw)
    res = [loss, dx[None]]
    for kind in ("grad", "delta", "new_m", "new_v"):
        res += [(out[kind, n][None] if n in stacked and n not in BIG else out[kind, n]) for n in WEIGHTS]
    return tuple(res)
```

```python
import functools
import math

import jax
import jax.numpy as jnp
from jax import lax
from jax.experimental import pallas as pl
from jax.experimental.pallas import tpu as pltpu

F32 = jnp.float32
BF16 = jnp.bfloat16

V7X_VMEM_BYTES = 64 * 1024 * 1024
VMEM_LIMIT = V7X_VMEM_BYTES - 8 * 1024 * 1024
LANE = 128

D_MODEL = 1024
NORM_EPS = 1e-6
MLA_HEADS, MLA_Q_RANK, MLA_KV_RANK, MLA_NOPE, MLA_ROPE, MLA_V = 16, 384, 256, 64, 32, 64
MLA_QK = MLA_NOPE + MLA_ROPE
ROPE_THETA = 10000.0
GLA_HEADS, GLA_DK, GLA_DV, GLA_RANK, GLA_TAU, GLA_CHUNK = 4, 128, 256, 16, 16.0, 64
LRU_WIDTH, LRU_BLOCKS, LRU_BLOCK, LRU_C, CONV_W = 1280, 10, 128, 8.0, 4
SSD_INNER, SSD_P, SSD_HEADS, SSD_GROUPS, SSD_HPG, SSD_STATE, SSD_CHUNK = 2048, 64, 32, 8, 4, 128, 64
ADAM_LR, ADAM_B1, ADAM_B2, ADAM_EPS, ADAM_WD, ADAM_STEP = 0.001, 0.9, 0.999, 1e-08, 0.01, 10

_CP = functools.partial(pltpu.CompilerParams, vmem_limit_bytes=VMEM_LIMIT)


def _bdot(a, b):
    return jnp.dot(a.astype(BF16), b.astype(BF16), preferred_element_type=F32)


def _bdot_nt(a, b):
    return lax.dot_general(a.astype(BF16), b.astype(BF16), (((1,), (1,)), ((), ())), preferred_element_type=F32)


def _bdot_tn(a, b):
    return lax.dot_general(a.astype(BF16), b.astype(BF16), (((0,), (0,)), ((), ())), preferred_element_type=F32)


def _tri(n):
    r = lax.broadcasted_iota(jnp.int32, (n, n), 0)
    c = lax.broadcasted_iota(jnp.int32, (n, n), 1)
    return r >= c


def _rms(x, g):
    return x * lax.rsqrt(jnp.mean(x * x, axis=-1, keepdims=True) + NORM_EPS) * g


def _silu(x):
    return x * jax.nn.sigmoid(x)


def _shift_rows(x, prev, j):
    if j == 0:
        return x
    t = x.shape[0]

    def fwd_impl(x, prev):
        row = lax.broadcasted_iota(jnp.int32, x.shape, 0)
        return jnp.where(row >= j, pltpu.roll(x, j, 0), pltpu.roll(prev, j, 0))

    @jax.custom_vjp
    def sh(x, prev):
        return fwd_impl(x, prev)

    def sh_fwd(x, prev):
        return fwd_impl(x, prev), None

    def sh_bwd(_, gy):
        row = lax.broadcasted_iota(jnp.int32, gy.shape, 0)
        back = pltpu.roll(gy, t - j, 0)
        return jnp.where(row < t - j, back, 0.0), jnp.where(row >= t - j, back, 0.0)

    sh.defvjp(sh_fwd, sh_bwd)
    return sh(x, prev)


def _cumsum_rows(x):
    zero = jnp.zeros_like(x)
    sh = 1
    while sh < x.shape[0]:
        x = x + _shift_rows(x, zero, sh)
        sh *= 2
    return x


def _one_minus_exp(x):
    series = -x * (1.0 + x * (0.5 + x * (1.0 / 6.0 + x * (1.0 / 24.0 + x * (1.0 / 120.0)))))
    return jnp.where(x > -0.05, series, 1.0 - jnp.exp(x))


def _tile(n, cap):
    if n <= cap:
        return n
    best = None
    for t in range(LANE, cap + 1, LANE):
        if n % t == 0:
            best = t
    assert best is not None, (n, cap)
    return best


MM_TM, MM_TN, MM_TK = 1024, 512, 2048


def _mm(name, a, b, *, ta=False, tb=False, add=None, out_dtype=F32):
    m, k = (a.shape[1], a.shape[0]) if ta else a.shape
    n, kb = (b.shape[0], b.shape[1]) if tb else (b.shape[1], b.shape[0])
    assert k == kb, (name, a.shape, b.shape, ta, tb)
    tm, tn, tk = _tile(m, MM_TM), _tile(n, MM_TN), _tile(k, MM_TK)
    nk = k // tk
    dn = (((0 if ta else 1,), (1 if tb else 0,)), ((), ()))
    has_add = add is not None

    def finish(refs, r):
        if has_add:
            r = r + refs[2][...].astype(F32)
        return r.astype(out_dtype)

    def body_one(*refs):
        a_ref, b_ref, o_ref = refs[0], refs[1], refs[-1]
        o_ref[...] = finish(refs, lax.dot_general(a_ref[...].astype(BF16), b_ref[...].astype(BF16), dn, preferred_element_type=F32))

    def body_acc(*refs):
        a_ref, b_ref = refs[0], refs[1]
        o_ref, acc = refs[-2], refs[-1]
        kk = pl.program_id(2)

        @pl.when(kk == 0)
        def _():
            acc[...] = jnp.zeros(acc.shape, F32)

        acc[...] += lax.dot_general(a_ref[...].astype(BF16), b_ref[...].astype(BF16), dn, preferred_element_type=F32)

        @pl.when(kk == nk - 1)
        def _():
            o_ref[...] = finish(refs, acc[...])

    a_spec = pl.BlockSpec((tk, tm), lambda i, j, q: (q, i)) if ta else pl.BlockSpec((tm, tk), lambda i, j, q: (i, q))
    b_spec = pl.BlockSpec((tn, tk), lambda i, j, q: (j, q)) if tb else pl.BlockSpec((tk, tn), lambda i, j, q: (q, j))
    o_spec = pl.BlockSpec((tm, tn), lambda i, j, q: (i, j))
    in_specs, args = [a_spec, b_spec], [a, b]
    if has_add:
        in_specs.append(o_spec)
        args.append(add)
    return pl.pallas_call(
        body_one if nk == 1 else body_acc, name=name, grid=(m // tm, n // tn, nk), in_specs=in_specs, out_specs=o_spec,
        out_shape=jax.ShapeDtypeStruct((m, n), out_dtype), scratch_shapes=[] if nk == 1 else [pltpu.VMEM((tm, tn), F32)],
        compiler_params=_CP(dimension_semantics=("parallel", "parallel", "arbitrary")),
    )(*args)


class In:
    def __init__(self, arr, block, imap, kind="x", per_h=False, gdtype=F32, gshape=None, gimap=None, prefixed=False):
        self.arr, self.block, self.imap, self.kind, self.per_h, self.gdtype = arr, tuple(block), imap, kind, per_h, gdtype
        self.prefixed = prefixed
        self.gshape = tuple(gshape) if gshape is not None else tuple(arr.shape)
        self.gimap = gimap if gimap is not None else imap

    def spec(self, rev_g=None):
        imap = self.imap
        if rev_g is None:
            return pl.BlockSpec(self.block, lambda h, g: imap(h, g))
        return pl.BlockSpec(self.block, lambda h, g: imap(h, rev_g - 1 - g))


class Out:
    def __init__(self, shape, dtype, block, imap):
        self.shape, self.dtype, self.block, self.imap = tuple(shape), dtype, tuple(block), imap

    def spec(self, rev_g=None):
        imap = self.imap
        if rev_g is None:
            return pl.BlockSpec(self.block, lambda h, g: imap(h, g))
        return pl.BlockSpec(self.block, lambda h, g: imap(h, rev_g - 1 - g))


def _load_f32(ref, rows=None):
    v = ref[...] if rows is None else ref[0:rows]
    return v.astype(F32) if jnp.issubdtype(v.dtype, jnp.floating) else v


def _state_out(grid, shape):
    nd = len(shape)
    return Out(tuple(grid) + tuple(shape), F32, (None, None) + tuple(shape), lambda h, g: (h, g) + (0,) * nd)


def _carry(comm, grid, refs, n_in, n_out, n_scr):
    n_c = len(comm.ops) if comm is not None else 0
    n_s = len(comm.sem_shapes) if comm is not None else 0
    p = 0
    in_refs = refs[p:p + n_in]; p += n_in
    c_src = refs[p:p + n_c]; p += n_c
    out_refs = refs[p:p + n_out]; p += n_out
    c_dst = refs[p:p + n_c]; p += n_c
    scr = refs[p:p + n_scr]; p += n_scr
    c_sem = refs[p:p + n_s]
    first = jnp.logical_and(pl.program_id(0) == 0, pl.program_id(1) == 0)
    last = jnp.logical_and(pl.program_id(0) == grid[0] - 1, pl.program_id(1) == grid[1] - 1)
    return in_refs, out_refs, scr, (c_src, c_dst, c_sem), first, last


def _carry_specs(comm):
    if comm is None:
        return [], [], [], [], []
    any_spec = pl.BlockSpec(memory_space=pl.ANY)
    n = len(comm.ops)
    return [any_spec] * n, list(comm.ops), [any_spec] * n, list(comm.out_shapes), list(comm.sem_shapes)


def _op_fwd(name, f, grid, ins, outs, state_shapes=(), comm=None, prefix_rows=None):
    assert prefix_rows is None or not state_shapes
    n_in, n_out, n_st = len(ins), len(outs), len(state_shapes)
    st_outs = [_state_out(grid, s) for s in state_shapes]

    def body(*refs):
        in_refs, o_refs, st_scr, cargs, first, last = _carry(comm, grid, refs, n_in, n_out + n_st, n_st)
        out_refs, sv_refs = o_refs[:n_out], o_refs[n_out:]
        if comm is not None:
            @pl.when(first)
            def _():
                comm.start(*cargs)
        g = pl.program_id(1)
        if n_st:
            @pl.when(g == 0)
            def _():
                for s in st_scr:
                    s[...] = jnp.zeros(s.shape, F32)

        def compute(rows):
            vals = [_load_f32(r, rows if i.prefixed else None) for r, i in zip(in_refs, ins)]
            sts = [s[...] for s in st_scr]
            o, ns = f(g, vals, sts)
            for r, v in zip(out_refs, o):
                r[...] = v.astype(r.dtype)
            for r, s in zip(sv_refs, sts):
                r[...] = s
            for s, v in zip(st_scr, ns):
                s[...] = v

        if prefix_rows is None:
            compute(None)
        else:
            per = grid[1] // len(prefix_rows)
            for lv, rows in enumerate(prefix_rows):
                pl.when(g // per == lv)(functools.partial(compute, rows))
        if comm is not None:
            @pl.when(last)
            def _():
                comm.finish(*cargs)

    all_outs = list(outs) + st_outs
    c_in_specs, c_args, c_out_specs, c_out_shapes, c_sems = _carry_specs(comm)
    res = pl.pallas_call(
        body, name=name, grid=tuple(grid), in_specs=[i.spec() for i in ins] + c_in_specs,
        out_specs=[o.spec() for o in all_outs] + c_out_specs,
        out_shape=[jax.ShapeDtypeStruct(o.shape, o.dtype) for o in all_outs] + c_out_shapes,
        scratch_shapes=[pltpu.VMEM(tuple(s), F32) for s in state_shapes] + c_sems,
        compiler_params=_CP(dimension_semantics=("arbitrary", "arbitrary")),
    )(*[i.arr for i in ins], *c_args)
    n_all = n_out + n_st
    return list(res[:n_out]), list(res[n_out:n_all]), list(res[n_all:])


def _op_bwd(name, f, grid, ins, outs, state_shapes, saved, douts, addto=None, comm=None, prefix_rows=None):
    n_in, n_out, n_st = len(ins), len(outs), len(state_shapes)
    n_g = grid[1]
    assert prefix_rows is None or all(i.prefixed and i.per_h for i in ins if i.kind == "p")
    addto = addto or {}
    diff = [k for k, i in enumerate(ins) if i.kind in ("x", "p")]
    add_idx = sorted(addto)
    st_ins = [In(s, o.block, o.imap, "c") for s, o in zip(saved, [_state_out(grid, s) for s in state_shapes])]
    dout_ins = [In(d, o.block, o.imap, "c") for d, o in zip(douts, outs)]
    add_ins = []
    for k in add_idx:
        i, a = ins[k], addto[k]
        blk = i.block if i.kind == "x" else i.block[:-2] + a.shape[-2:]
        add_ins.append(In(a, blk, i.gimap if i.kind == "x" else i.imap, "c"))
    g_outs = []
    for k in diff:
        i = ins[k]
        g_outs.append(Out(i.gshape, i.gdtype if i.kind == "x" else F32, i.block, i.gimap))

    def body(*refs):
        all_in, go_refs, ds_scr, cargs, first_step, last_step = _carry(comm, grid, refs, n_in + n_st + n_out + len(add_idx), len(diff), n_st)
        if comm is not None:
            @pl.when(first_step)
            def _():
                comm.start(*cargs)
        p = 0
        in_refs = all_in[p:p + n_in]; p += n_in
        sv_refs = all_in[p:p + n_st]; p += n_st
        do_refs = all_in[p:p + n_out]; p += n_out
        ad_refs = all_in[p:p + len(add_idx)]
        hh = pl.program_id(0)
        step = pl.program_id(1)
        g = n_g - 1 - step
        if n_st:
            @pl.when(step == 0)
            def _():
                for s in ds_scr:
                    s[...] = jnp.zeros(s.shape, F32)

        def compute(rows):
            vals = [_load_f32(r, rows if i.prefixed else None) for r, i in zip(in_refs, ins)]
            sts = [r[...] for r in sv_refs]

            def fw(dvals, states):
                full = list(vals)
                for k, v in zip(diff, dvals):
                    full[k] = v
                o, ns = f(g, full, states)
                return list(o), list(ns)

            _, vjp = jax.vjp(fw, [vals[k] for k in diff], sts)
            cts = [r[...].astype(F32) for r in do_refs]
            dns = [s[...] for s in ds_scr]
            dvals, dsts = vjp((cts, dns))
            adds = dict(zip(add_idx, ad_refs))
            for k, r, dv in zip(diff, go_refs, dvals):
                i = ins[k]
                if i.kind == "x":
                    if k in adds:
                        dv = dv + adds[k][...].astype(F32)
                    r[...] = dv.astype(r.dtype)
                elif rows is not None:
                    r[0:rows] += dv
                else:
                    first = (step == 0) if i.per_h else jnp.logical_and(step == 0, hh == 0)

                    @pl.when(first)
                    def _(r=r, dv=dv, k=k):
                        r[...] = dv
                        if k in adds:
                            lead = adds[k].shape[0]
                            r[0:lead] += adds[k][...]

                    @pl.when(jnp.logical_not(first))
                    def _(r=r, dv=dv):
                        r[...] += dv
            for s, v in zip(ds_scr, dsts):
                s[...] = v

        if prefix_rows is None:
            compute(None)
        else:
            @pl.when(step == 0)
            def _():
                for k, r in zip(diff, go_refs):
                    if ins[k].kind == "p":
                        r[...] = jnp.zeros(r.shape, F32)
            per = n_g // len(prefix_rows)
            for lv, rows in enumerate(prefix_rows):
                pl.when(g // per == lv)(functools.partial(compute, rows))
        if comm is not None:
            @pl.when(last_step)
            def _():
                comm.finish(*cargs)

    all_ins = list(ins) + st_ins + dout_ins + add_ins
    c_in_specs, c_args, c_out_specs, c_out_shapes, c_sems = _carry_specs(comm)
    res = pl.pallas_call(
        body, name=name, grid=tuple(grid), in_specs=[i.spec(n_g) for i in all_ins] + c_in_specs,
        out_specs=[o.spec(n_g) for o in g_outs] + c_out_specs,
        out_shape=[jax.ShapeDtypeStruct(o.shape, o.dtype) for o in g_outs] + c_out_shapes,
        scratch_shapes=[pltpu.VMEM(tuple(s), F32) for s in state_shapes] + c_sems,
        compiler_params=_CP(dimension_semantics=("arbitrary", "arbitrary")),
    )(*[i.arr for i in all_ins], *c_args)
    return list(res[:len(g_outs)]), list(res[len(g_outs):])


class Op:
    def __init__(self, name, f, grid, ins, outs, state_shapes=(), prefix_rows=None):
        self.name, self.f, self.grid, self.ins, self.outs, self.state_shapes = name, f, grid, ins, outs, state_shapes
        self.prefix_rows = prefix_rows
        self.saved = None

    def fwd(self, comm=None):
        res, self.saved, self.fwd_comm_out = _op_fwd(self.name + "_fwd", self.f, self.grid, self.ins, self.outs, self.state_shapes, comm,
                                                     self.prefix_rows)
        return res

    def bwd(self, douts, addto=None, comm=None):
        res, self.bwd_comm_out = _op_bwd(self.name + "_bwd", self.f, self.grid, self.ins, self.outs, self.state_shapes, self.saved, douts,
                                         addto, comm, self.prefix_rows)
        return res


def _rows(arr, t, kind="x", gdtype=F32):
    return In(arr, (t, arr.shape[1]), lambda h, g: (g, 0), kind, gdtype=gdtype)


def _whole(arr, kind="p"):
    nd = arr.ndim
    return In(arr, arr.shape, lambda h, g: (0,) * nd, kind)


def _rows_out(s, n, t, dtype):
    return Out((s, n), dtype, (t, n), lambda h, g: (g, 0))


ROW_T = 256


def _rms_op(name, x, gain, out_dtype=BF16, gdtype=F32):
    s, n = x.shape

    def f(g, vals, sts):
        return [_rms(vals[0], vals[1])], []

    return Op(name, f, (1, s // ROW_T), [_rows(x, ROW_T, gdtype=gdtype), _whole(gain.reshape(1, n))], [_rows_out(s, n, ROW_T, out_dtype)])


def _mla_prep_op(qn, q1, q2, kn, kr, v, cos, sin):
    s = qn.shape[0]
    hd, half = MLA_HEADS, MLA_ROPE // 2

    def f(g, vals, sts):
        qn, q1, q2, kn, kr, v, cos, sin = vals
        cos_h, sin_h = jnp.tile(cos, (1, hd)), jnp.tile(sin, (1, hd))
        r1 = q1 * cos_h - q2 * sin_h
        r2 = q2 * cos_h + q1 * sin_h
        k1, k2 = kr[:, 0:half], kr[:, half:2 * half]
        kr1 = k1 * cos - k2 * sin
        kr2 = k2 * cos + k1 * sin
        zpad = jnp.zeros((qn.shape[0], LANE - MLA_QK), F32)
        qs, ks, vs = [], [], []
        for h in range(hd):
            a, b = h * MLA_NOPE, (h + 1) * MLA_NOPE
            c, d = h * half, (h + 1) * half
            qs.append(jnp.concatenate([qn[:, a:b], r1[:, c:d], r2[:, c:d], zpad], axis=1))
            ks.append(jnp.concatenate([kn[:, a:b], kr1, kr2, zpad], axis=1))
            vs.append(v[:, a:b])
        return [jnp.stack(qs, 0), jnp.stack(ks, 0), jnp.stack(vs, 0)], []

    ins = [_rows(qn, ROW_T, gdtype=BF16), _rows(q1, ROW_T, gdtype=BF16), _rows(q2, ROW_T, gdtype=BF16), _rows(kn, ROW_T, gdtype=BF16),
           _rows(kr, ROW_T, gdtype=BF16), _rows(v, ROW_T, gdtype=BF16), _rows(cos, ROW_T, "c"), _rows(sin, ROW_T, "c")]
    outs = [Out((hd, s, LANE), BF16, (hd, ROW_T, LANE), lambda h, g: (0, g, 0)),
            Out((hd, s, LANE), BF16, (hd, ROW_T, LANE), lambda h, g: (0, g, 0)),
            Out((hd, s, MLA_V), BF16, (hd, ROW_T, MLA_V), lambda h, g: (0, g, 0))]
    return Op("mla_prep", f, (1, s // ROW_T), ins, outs)


ATT_TQ = 256
ATT_LEVELS = 4


def _mla_attn_op(q, k, v):
    hd, s, _ = q.shape
    scale = MLA_QK ** -0.5

    def f(g, vals, sts):
        q, k, v = vals
        sc = _bdot_nt(q, k) * scale
        r = lax.broadcasted_iota(jnp.int32, sc.shape, 0) + g * ATT_TQ
        c = lax.broadcasted_iota(jnp.int32, sc.shape, 1)
        sc = jnp.where(r >= c, sc, -1e30)
        m = lax.stop_gradient(jnp.max(sc, axis=-1, keepdims=True))
        p = jnp.exp(sc - m)
        p = p / jnp.sum(p, axis=-1, keepdims=True)
        return [_bdot(p, v)], []

    ins = [In(q, (None, ATT_TQ, LANE), lambda h, g: (h, g, 0), "x", gdtype=BF16),
           In(k, (None, s, LANE), lambda h, g: (h, 0, 0), "p", per_h=True, prefixed=True),
           In(v, (None, s, MLA_V), lambda h, g: (h, 0, 0), "p", per_h=True, prefixed=True)]
    outs = [Out((hd, s, MLA_V), F32, (None, ATT_TQ, MLA_V), lambda h, g: (h, g, 0))]
    return Op("mla_attn", f, (hd, s // ATT_TQ), ins, outs, prefix_rows=[(lv + 1) * (s // ATT_LEVELS) for lv in range(ATT_LEVELS)])


def _mla_post_op(o, gate):
    hd, s, _ = o.shape

    def f(g, vals, sts):
        o, gate = vals
        cat = jnp.concatenate([o[h] for h in range(hd)], axis=1)
        return [cat * _silu(gate)], []

    ins = [In(o, (hd, ROW_T, MLA_V), lambda h, g: (0, g, 0), "x"), _rows(gate, ROW_T, gdtype=BF16)]
    return Op("mla_post", f, (1, s // ROW_T), ins, [_rows_out(s, hd * MLA_V, ROW_T, BF16)])


def _gla_gate_op(gk, w2, b):
    s = gk.shape[0]

    def f(g, vals, sts):
        gk, w2, b = vals
        return [jax.nn.log_sigmoid(_bdot(gk, w2) + b) / GLA_TAU], []

    ins = [_rows(gk, ROW_T, gdtype=BF16), _whole(w2), _whole(b)]
    return Op("gla_gate", f, (1, s // ROW_T), ins, [_rows_out(s, GLA_HEADS * GLA_DK, ROW_T, F32)])


def _gla_core_op(q, k, v, gate, la, g_o):
    s = q.shape[0]
    c, nh = GLA_CHUNK, GLA_HEADS

    def f(g, vals, sts):
        q, k, v, gate, la, g_o = vals
        tri = _tri(c)
        b = _cumsum_rows(la)
        b_last = jnp.sum(la, axis=0, keepdims=True)
        qt = q * (GLA_DK ** -0.5) * jnp.exp(b)
        kt = k * jnp.exp(-b)
        kd = k * jnp.exp(b_last - b)
        ys, new_sts = [], []
        for h in range(nh):
            ks, vs = slice(h * GLA_DK, (h + 1) * GLA_DK), slice(h * GLA_DV, (h + 1) * GLA_DV)
            att = jnp.where(tri, _bdot_nt(qt[:, ks], kt[:, ks]), 0.0)
            o = _bdot(att, v[:, vs]) + _bdot_nt(qt[:, ks], sts[h])
            new_sts.append(jnp.exp(b_last[:, ks]) * sts[h] + _bdot_tn(v[:, vs], kd[:, ks]))
            ys.append(_rms(o, g_o) * _silu(gate[:, vs]))
        return [jnp.concatenate(ys, axis=1)], new_sts

    ins = [_rows(q, c, gdtype=BF16), _rows(k, c, gdtype=BF16), _rows(v, c, gdtype=BF16), _rows(gate, c, gdtype=BF16), _rows(la, c), _whole(g_o)]
    outs = [_rows_out(s, nh * GLA_DV, c, BF16)]
    return Op("gla_core", f, (1, s // c), ins, outs, [(GLA_DV, GLA_DK)] * nh)


LRU_T = 256


def _lru_op(gate, u, conv_w, conv_b, w_a, b_a, w_x, b_x, lam):
    s, w = u.shape
    t = LRU_T

    def f(g, vals, sts):
        gate, u, cw, cb, w_a, b_a, w_x, b_x, lam = vals
        u_prev, h_prev = sts
        uc = cb
        for kk in range(CONV_W):
            uc = uc + cw[kk] * _shift_rows(u, u_prev, CONV_W - 1 - kk)
        ra, ri = [], []
        for n in range(LRU_BLOCKS):
            blk = uc[:, n * LRU_BLOCK:(n + 1) * LRU_BLOCK]
            ra.append(_bdot(blk, w_a[n]))
            ri.append(_bdot(blk, w_x[n]))
        r = jax.nn.sigmoid(jnp.concatenate(ra, axis=1) + b_a)
        i = jax.nn.sigmoid(jnp.concatenate(ri, axis=1) + b_x)
        log_a = -LRU_C * r * jax.nn.softplus(-lam)
        a = jnp.exp(log_a)
        bb = jnp.sqrt(_one_minus_exp(2.0 * log_a)) * (i * uc)
        zero = jnp.zeros_like(a)
        sh = 1
        while sh < t:
            a_s = _shift_rows(a - 1.0, zero, sh) + 1.0
            b_s = _shift_rows(bb, zero, sh)
            bb = a * b_s + bb
            a = a * a_s
            sh *= 2
        hs = bb + a * h_prev
        last = (lax.broadcasted_iota(jnp.int32, hs.shape, 0) == t - 1).astype(F32)
        h_last = jnp.sum(hs * last, axis=0, keepdims=True)
        return [hs * _silu(gate)], [u, h_last]

    ins = [_rows(gate, t, gdtype=BF16), _rows(u, t, gdtype=BF16), _whole(conv_w), _whole(conv_b), _whole(w_a), _whole(b_a), _whole(w_x),
           _whole(b_x), _whole(lam)]
    return Op("lru_core", f, (1, s // t), ins, [_rows_out(s, w, t, BF16)], [(t, w), (1, w)])


def _ssd_conv_op(xbc, conv_w, conv_b):
    s, w = xbc.shape
    t = ROW_T
    n_x, n_b = SSD_INNER, SSD_GROUPS * SSD_STATE

    def f(g, vals, sts):
        xbc, cw, cb = vals
        acc = cb
        for kk in range(CONV_W):
            acc = acc + cw[kk] * _shift_rows(xbc, sts[0], CONV_W - 1 - kk)
        y = _silu(acc)
        return [y[:, :n_x], y[:, n_x:n_x + n_b], y[:, n_x + n_b:]], [xbc]

    ins = [_rows(xbc, t, gdtype=BF16), _whole(conv_w), _whole(conv_b)]
    outs = [_rows_out(s, n_x, t, F32), _rows_out(s, n_b, t, F32), _rows_out(s, n_b, t, F32)]
    return Op("ssd_conv", f, (1, s // t), ins, outs, [(t, w)])


SSD_L = 256


def _ssd_core_op(x, bm, cm, z, dt, dt_bias, a_log, d_skip, g_norm):
    s = x.shape[0]
    c, hg, p = SSD_L, SSD_HPG, SSD_P
    gw = hg * p

    def f(g, vals, sts):
        x, bm, cm, z, dtr, dt_bias, a_log, d_skip, g_norm = vals
        tri = _tri(c)
        dt = jax.nn.softplus(dtr + dt_bias)
        da = dt * (-jnp.exp(a_log))
        cs = _cumsum_rows(da)
        cs_last = jnp.sum(da, axis=0, keepdims=True)
        cs_t = jnp.transpose(jnp.concatenate([cs, jnp.zeros((c, LANE - hg), F32)], axis=1))
        cb = _bdot_nt(cm, bm)
        ys, new_st = [], []
        for h in range(hg):
            cs_h = cs[:, h:h + 1]
            cs_row = cs_t[h:h + 1, :]
            seg = jnp.where(tri, cs_h - cs_row, 0.0)
            lmat = jnp.where(tri, jnp.exp(seg), 0.0)
            x_h = x[:, h * p:(h + 1) * p]
            xdt = x_h * dt[:, h:h + 1]
            y_diag = _bdot(cb * lmat, xdt)
            decay = jnp.exp(cs_last[:, h:h + 1] - cs_h)
            states = _bdot_tn(xdt * decay, bm)
            y_off = _bdot_nt(cm, sts[h]) * jnp.exp(cs_h)
            new_st.append(jnp.exp(cs_last[:, h:h + 1]) * sts[h] + states)
            ys.append(y_diag + y_off + d_skip[:, h:h + 1] * x_h)
        y = jnp.concatenate(ys, axis=1) * _silu(z)
        return [_rms(y, g_norm)], new_st

    ins = [In(x, (c, gw), lambda h, g: (g, h), "x"), In(bm, (c, SSD_STATE), lambda h, g: (g, h), "x"),
           In(cm, (c, SSD_STATE), lambda h, g: (g, h), "x"), In(z, (c, gw), lambda h, g: (g, h), "x", gdtype=BF16),
           In(dt, (None, c, hg), lambda h, g: (h, g, 0), "x"),
           In(dt_bias, (None, 1, hg), lambda h, g: (h, 0, 0), "p", per_h=True),
           In(a_log, (None, 1, hg), lambda h, g: (h, 0, 0), "p", per_h=True),
           In(d_skip, (None, 1, hg), lambda h, g: (h, 0, 0), "p", per_h=True),
           In(g_norm, (1, gw), lambda h, g: (0, h), "p", per_h=True)]
    outs = [Out((s, SSD_INNER), BF16, (c, gw), lambda h, g: (g, h))]
    return Op("ssd_core", f, (SSD_GROUPS, s // c), ins, outs, [(p, SSD_STATE)] * hg)


def _loss_op(h, target, final_g):
    s, n = h.shape
    t = ROW_T
    n_g = s // t

    def body(h_ref, t_ref, g_ref, loss_ref, dh_ref, dg_ref):
        step = pl.program_id(0)

        def lossf(hv, gv):
            err = _rms(hv, gv) - t_ref[...]
            return 0.5 * jnp.sum(jnp.mean(err * err, axis=-1))

        l, (dh, dg) = jax.value_and_grad(lossf, argnums=(0, 1))(h_ref[...], g_ref[...])
        dh_ref[...] = dh

        @pl.when(step == 0)
        def _():
            loss_ref[...] = jnp.zeros(loss_ref.shape, F32)
            dg_ref[...] = jnp.zeros(dg_ref.shape, F32)

        loss_ref[...] += jnp.full(loss_ref.shape, l, F32)
        dg_ref[...] += dg

    row = pl.BlockSpec((t, n), lambda g: (g, 0))
    one = pl.BlockSpec((1, n), lambda g: (0, 0))
    return pl.pallas_call(
        body, name="loss_head", grid=(n_g,), in_specs=[row, row, one],
        out_specs=[pl.BlockSpec((1, LANE), lambda g: (0, 0)), row, one],
        out_shape=[jax.ShapeDtypeStruct((1, LANE), F32), jax.ShapeDtypeStruct((s, n), F32), jax.ShapeDtypeStruct((1, n), F32)],
        compiler_params=_CP(dimension_semantics=("arbitrary",)),
    )(h, target, final_g.reshape(1, n))


def _pad_cols(w, n):
    return jnp.pad(w, ((0, 0), (0, n - w.shape[1])))


def _pad_rows(w, n):
    return jnp.pad(w, ((0, n - w.shape[0]), (0, 0)))


def _proj_bwd(tag, u, dps, ws):
    du = None
    for i, (dp, w) in enumerate(zip(dps, ws)):
        du = _mm(f"{tag}_du{i}", dp, w, tb=True, add=du)
    dws = [_mm(f"{tag}_dw{i}", u, dp, ta=True, out_dtype=BF16) for i, dp in enumerate(dps)]
    return du, dws


def _mla_layer(h, norm_g, w, cos, sin, fwd_comm=None, late_w_out=None):
    bf = lambda a: a.astype(BF16)
    w_in, w_uq, w_ukv = w["mla_w_in"], w["mla_w_uq"], w["mla_w_ukv"]
    a0, a1, a2 = MLA_Q_RANK, MLA_Q_RANK + MLA_KV_RANK, MLA_Q_RANK + MLA_KV_RANK + MLA_ROPE
    w_cq, w_ckv, w_kr, w_g = bf(w_in[:, :a0]), bf(w_in[:, a0:a1]), bf(_pad_cols(w_in[:, a1:a2], LANE)), bf(w_in[:, a2:])
    uq = w_uq.reshape(MLA_Q_RANK, MLA_HEADS, MLA_QK)
    half = MLA_ROPE // 2
    w_qn = bf(uq[:, :, :MLA_NOPE].reshape(MLA_Q_RANK, -1))
    w_q1 = bf(uq[:, :, MLA_NOPE:MLA_NOPE + half].reshape(MLA_Q_RANK, -1))
    w_q2 = bf(uq[:, :, MLA_NOPE + half:].reshape(MLA_Q_RANK, -1))
    ukv = w_ukv.reshape(MLA_KV_RANK, MLA_HEADS, MLA_NOPE + MLA_V)
    w_kn = bf(ukv[:, :, :MLA_NOPE].reshape(MLA_KV_RANK, -1))
    w_v = bf(ukv[:, :, MLA_NOPE:].reshape(MLA_KV_RANK, -1))

    n0 = _rms_op("mla_norm", h, norm_g)
    u, = n0.fwd()
    cq, ckv, kr, gate = (_mm(f"mla_in{i}", u, wi) for i, wi in enumerate((w_cq, w_ckv, w_kr, w_g)))
    nq = _rms_op("mla_qnorm", cq, w["mla_g_q"], gdtype=BF16)
    nkv = _rms_op("mla_kvnorm", ckv, w["mla_g_kv"], gdtype=BF16)
    qn_, = nq.fwd()
    kvn_, = nkv.fwd()
    qn, q1, q2 = (_mm(f"mla_uq{i}", qn_, wi) for i, wi in enumerate((w_qn, w_q1, w_q2)))
    kn, v = (_mm(f"mla_ukv{i}", kvn_, wi) for i, wi in enumerate((w_kn, w_v)))
    prep = _mla_prep_op(qn, q1, q2, kn, kr, v, cos, sin)
    qh, kh, vh = prep.fwd()
    attn = _mla_attn_op(qh, kh, vh)
    o, = attn.fwd(fwd_comm)
    w_out = bf(w["mla_w_out"]) if late_w_out is None else late_w_out(attn.fwd_comm_out)
    post = _mla_post_op(o, gate)
    y, = post.fwd()
    h_out = _mm("mla_out", y, w_out, add=h)

    def bwd(dh, make_comm=None):
        dy = _mm("mla_out_dy", dh, w_out, tb=True, out_dtype=BF16)
        d_w_out = _mm("mla_out_dw", y, dh, ta=True, out_dtype=BF16)
        do, dgate = post.bwd([dy])
        dqh, dkh, dvh = attn.bwd([do], comm=None if make_comm is None else make_comm(d_w_out))
        dqn, dq1, dq2, dkn, dkr, dv = prep.bwd([dqh, dkh, dvh])
        dqn_, d_uq = _proj_bwd("mla_uq", qn_, (dqn, dq1, dq2), (w_qn, w_q1, w_q2))
        dkvn_, d_ukv = _proj_bwd("mla_ukv", kvn_, (dkn, dv), (w_kn, w_v))
        dcq, d_g_q = nq.bwd([dqn_])
        dckv, d_g_kv = nkv.bwd([dkvn_])
        du, d_in = _proj_bwd("mla_in", u, (dcq, dckv, dkr, dgate), (w_cq, w_ckv, w_kr, w_g))
        dh_in, d_norm = n0.bwd([du], addto={0: dh})
        shp = (MLA_Q_RANK, MLA_HEADS, -1)
        g_uq = jnp.concatenate([d_uq[0].reshape(shp), d_uq[1].reshape(shp), d_uq[2].reshape(shp)], axis=2).reshape(MLA_Q_RANK, -1)
        shp = (MLA_KV_RANK, MLA_HEADS, -1)
        g_ukv = jnp.concatenate([d_ukv[0].reshape(shp), d_ukv[1].reshape(shp)], axis=2).reshape(MLA_KV_RANK, -1)
        g_in = jnp.concatenate([d_in[0], d_in[1], d_in[2][:, :MLA_ROPE], d_in[3]], axis=1)
        return dh_in, d_norm, {"mla_w_in": g_in, "mla_g_q": d_g_q.reshape(-1), "mla_w_uq": g_uq, "mla_g_kv": d_g_kv.reshape(-1),
                               "mla_w_ukv": g_ukv, "mla_w_out": d_w_out}, attn.bwd_comm_out

    return h_out, bwd, attn.fwd_comm_out


def _gla_layer(h, norm_g, w, fwd_comm=None):
    bf = lambda a: a.astype(BF16)
    w_in = w["gla_w_in"]
    nk, nv = GLA_HEADS * GLA_DK, GLA_HEADS * GLA_DV
    cuts = (0, nk, 2 * nk, 2 * nk + nv, 2 * nk + 2 * nv)
    w_q, w_k, w_v, w_g = (bf(w_in[:, cuts[i]:cuts[i + 1]]) for i in range(4))
    w_gk = bf(_pad_cols(w_in[:, cuts[4]:], LANE))
    w2 = _pad_rows(w["gla_w_gk2"], LANE)
    b_gk = w["gla_b_gk"].reshape(1, -1)
    g_o = w["gla_g_o"].reshape(1, -1)
    w_out = bf(w["gla_w_out"])

    n0 = _rms_op("gla_norm", h, norm_g)
    u, = n0.fwd()
    q, k, v, gate, gk = (_mm(f"gla_in{i}", u, wi) for i, wi in enumerate((w_q, w_k, w_v, w_g, w_gk)))
    gop = _gla_gate_op(gk, w2, b_gk)
    la, = gop.fwd()
    core = _gla_core_op(q, k, v, gate, la, g_o)
    y, = core.fwd(fwd_comm)
    h_out = _mm("gla_out", y, w_out, add=h)

    def bwd(dh, make_comm=None):
        dy = _mm("gla_out_dy", dh, w_out, tb=True, out_dtype=BF16)
        d_w_out = _mm("gla_out_dw", y, dh, ta=True, out_dtype=BF16)
        dq, dk, dv, dgate, dla, d_g_o = core.bwd([dy], comm=None if make_comm is None else make_comm(d_w_out))
        dgk, d_w2, d_b = gop.bwd([dla])
        du, d_in = _proj_bwd("gla_in", u, (dq, dk, dv, dgate, dgk), (w_q, w_k, w_v, w_g, w_gk))
        dh_in, d_norm = n0.bwd([du], addto={0: dh})
        g_in = jnp.concatenate([d_in[0], d_in[1], d_in[2], d_in[3], d_in[4][:, :GLA_RANK]], axis=1)
        return dh_in, d_norm, {"gla_w_in": g_in, "gla_w_gk2": d_w2[:GLA_RANK], "gla_b_gk": d_b.reshape(-1), "gla_g_o": d_g_o.reshape(-1),
                               "gla_w_out": d_w_out}, core.bwd_comm_out

    return h_out, bwd, core.fwd_comm_out


def _lru_layer(h, norm_g, w, fwd_comm=None):
    bf = lambda a: a.astype(BF16)
    w_in = w["lru_w_in"]
    w_g, w_u = bf(w_in[:, :LRU_WIDTH]), bf(w_in[:, LRU_WIDTH:])
    row = lambda a: a.reshape(1, -1)
    w_out = bf(w["lru_w_out"])

    n0 = _rms_op("lru_norm", h, norm_g)
    u_, = n0.fwd()
    gate, u = (_mm(f"lru_in{i}", u_, wi) for i, wi in enumerate((w_g, w_u)))
    core = _lru_op(gate, u, w["lru_conv_w"].reshape(CONV_W, 1, -1), row(w["lru_conv_b"]), w["lru_w_a"], row(w["lru_b_a"]), w["lru_w_x"],
                   row(w["lru_b_x"]), row(w["lru_lam"]))
    y, = core.fwd(fwd_comm)
    h_out = _mm("lru_out", y, w_out, add=h)

    def bwd(dh, make_comm=None):
        dy = _mm("lru_out_dy", dh, w_out, tb=True, out_dtype=BF16)
        d_w_out = _mm("lru_out_dw", y, dh, ta=True, out_dtype=BF16)
        dgate, du, d_cw, d_cb, d_wa, d_ba, d_wx, d_bx, d_lam = core.bwd([dy], comm=None if make_comm is None else make_comm(d_w_out))
        du_, d_in = _proj_bwd("lru_in", u_, (dgate, du), (w_g, w_u))
        dh_in, d_norm = n0.bwd([du_], addto={0: dh})
        return dh_in, d_norm, {"lru_w_in": jnp.concatenate(d_in, axis=1), "lru_conv_w": d_cw.reshape(CONV_W, -1), "lru_conv_b": d_cb.reshape(-1),
                               "lru_w_a": d_wa, "lru_b_a": d_ba.reshape(-1), "lru_w_x": d_wx, "lru_b_x": d_bx.reshape(-1),
                               "lru_lam": d_lam.reshape(-1), "lru_w_out": d_w_out}, core.bwd_comm_out

    return h_out, bwd, core.fwd_comm_out


def _ssd_layer(h, norm_g, w, fwd_comm=None, late_w_out=None):
    bf = lambda a: a.astype(BF16)
    s = h.shape[0]
    w_in = w["ssd_w_in"]
    conv_dim = SSD_INNER + 2 * SSD_GROUPS * SSD_STATE
    w_z, w_xbc = bf(w_in[:, :SSD_INNER]), bf(w_in[:, SSD_INNER:SSD_INNER + conv_dim])
    w_dt = bf(_pad_cols(w_in[:, SSD_INNER + conv_dim:], LANE))
    grp = lambda a: a.reshape(SSD_GROUPS, 1, SSD_HPG)

    n0 = _rms_op("ssd_norm", h, norm_g)
    u, = n0.fwd()
    z, xbc, dtp = (_mm(f"ssd_in{i}", u, wi) for i, wi in enumerate((w_z, w_xbc, w_dt)))
    conv = _ssd_conv_op(xbc, w["ssd_conv_w"].reshape(CONV_W, 1, -1), w["ssd_conv_b"].reshape(1, -1))
    x, bm, cm = conv.fwd()
    dt = dtp[:, :SSD_HEADS].reshape(s, SSD_GROUPS, SSD_HPG).transpose(1, 0, 2)
    core = _ssd_core_op(x, bm, cm, z, dt, grp(w["ssd_dt_bias"]), grp(w["ssd_a_log"]), grp(w["ssd_d"]), w["ssd_g_norm"].reshape(1, -1))
    y, = core.fwd(fwd_comm)
    w_out = bf(w["ssd_w_out"]) if late_w_out is None else late_w_out(core.fwd_comm_out)
    h_out = _mm("ssd_out", y, w_out, add=h)

    def bwd(dh, make_comm=None):
        dy = _mm("ssd_out_dy", dh, w_out, tb=True, out_dtype=BF16)
        d_w_out = _mm("ssd_out_dw", y, dh, ta=True, out_dtype=BF16)
        dx, dbm, dcm, dz, ddt, d_dtb, d_alog, d_d, d_gn = core.bwd([dy], comm=None if make_comm is None else make_comm(d_w_out))
        dxbc, d_cw, d_cb = conv.bwd([dx, dbm, dcm])
        ddtp = _pad_cols(ddt.transpose(1, 0, 2).reshape(s, SSD_HEADS), LANE).astype(BF16)
        du, d_in = _proj_bwd("ssd_in", u, (dz, dxbc, ddtp), (w_z, w_xbc, w_dt))
        dh_in, d_norm = n0.bwd([du], addto={0: dh})
        g_in = jnp.concatenate([d_in[0], d_in[1], d_in[2][:, :SSD_HEADS]], axis=1)
        return dh_in, d_norm, {"ssd_w_in": g_in, "ssd_conv_w": d_cw.reshape(CONV_W, -1), "ssd_conv_b": d_cb.reshape(-1),
                               "ssd_dt_bias": d_dtb.reshape(-1), "ssd_a_log": d_alog.reshape(-1), "ssd_d": d_d.reshape(-1),
                               "ssd_g_norm": d_gn.reshape(-1), "ssd_w_out": d_w_out}, core.bwd_comm_out

    return h_out, bwd


def _rope_tables(positions):
    inv_freq = ROPE_THETA ** (-jnp.arange(0, MLA_ROPE, 2, dtype=F32) / MLA_ROPE)
    ang = positions.astype(F32)[:, None] * inv_freq
    return jnp.cos(ang), jnp.sin(ang)


WEIGHTS = ["norm_g", "final_g", "mla_w_in", "mla_g_q", "mla_w_uq", "mla_g_kv", "mla_w_ukv", "mla_w_out", "gla_w_in", "gla_w_gk2", "gla_b_gk",
           "gla_g_o", "gla_w_out", "lru_w_in", "lru_conv_w", "lru_conv_b", "lru_w_a", "lru_b_a", "lru_w_x", "lru_b_x", "lru_lam", "lru_w_out",
           "ssd_w_in", "ssd_conv_w", "ssd_conv_b", "ssd_dt_bias", "ssd_a_log", "ssd_d", "ssd_g_norm", "ssd_w_out"]
BIG = ["mla_w_in", "mla_w_uq", "mla_w_ukv", "mla_w_out", "gla_w_in", "gla_w_out", "lru_w_in", "lru_w_out", "ssd_w_in", "ssd_w_out"]
SMALL = ["gla_w_gk2", "gla_b_gk", "gla_g_o", "lru_conv_w", "lru_conv_b", "lru_b_a", "lru_b_x", "lru_lam", "ssd_conv_w", "ssd_conv_b", "ssd_g_norm"]
REPL = ["norm_g", "final_g", "mla_g_q", "mla_g_kv", "lru_w_a", "lru_w_x", "ssd_dt_bias", "ssd_a_log", "ssd_d"]
REPL_EARLY = ["lru_w_a", "lru_w_x"]
REPL_LATE = [n for n in REPL if n not in REPL_EARLY]
N_CHIPS, N_DEV = 4, 8
PACK_W = 1024
ADAM_ROWS = 256
SMALL_ROWS = 64


def _shard_axis(name):
    return 0 if name.endswith("_w_out") else -1


def _pack(arrs, dtype, row_mult):
    flat = jnp.concatenate([a.reshape(-1).astype(dtype) for a in arrs])
    per = PACK_W * row_mult
    total = -(-flat.shape[0] // per) * per
    return jnp.pad(flat, (0, total - flat.shape[0])).reshape(-1, PACK_W)


def _unpack(buf, shapes):
    flat = buf.reshape(-1)
    out, off = [], 0
    for s in shapes:
        n = math.prod(s)
        out.append(flat[off:off + n].reshape(s))
        off += n
    return out


def _mesh_pos():
    return lax.axis_index("x"), lax.axis_index("y"), lax.axis_index("c")


class GatherComm:
    def __init__(self, ops):
        self.ops = list(ops)
        n = len(self.ops)
        assert all(o.ndim == 2 and o.shape[0] % 32 == 0 for o in self.ops), [o.shape for o in self.ops]
        self.out_shapes = [jax.ShapeDtypeStruct((N_CHIPS,) + o.shape, o.dtype) for o in self.ops]
        self.sem_shapes = [pltpu.SemaphoreType.DMA((6 * n,)), pltpu.SemaphoreType.DMA((6 * n,)), pltpu.SemaphoreType.DMA((n,))]

    def _copies(self, srcs, dsts, sems):
        send_sems, recv_sems, local_sems = sems
        n = len(self.ops)
        x, y, c = _mesh_pos()
        me_id, sibling = (x, y, c), (x, y, 1 - c)
        chips = [(1 - x, y), (x, 1 - y), (1 - x, 1 - y)]
        mine = 2 * x + y

        def half(i, cc):
            h = self.ops[i].shape[0] // 2
            return pl.ds(cc * h, h)

        def copy(i, k, src, slot, cc, to):
            return pltpu.make_async_remote_copy(src_ref=src, dst_ref=dsts[i].at[slot, half(i, cc)], send_sem=send_sems.at[i * 6 + k],
                                                recv_sem=recv_sems.at[i * 6 + k], device_id=to, device_id_type=pl.DeviceIdType.MESH)

        local = [pltpu.make_async_copy(srcs[i], dsts[i].at[mine], local_sems.at[i]) for i in range(n)]
        first, ici_recvs, passed, sib_recvs = [], [], [], []
        for i in range(n):
            my_half = srcs[i].at[half(i, c)]
            for k, (px, py) in enumerate(chips):
                slot = 2 * px + py
                first.append(copy(i, k, my_half, mine, c, (px, py, c)))
                ici_recvs.append(copy(i, k, my_half, slot, c, me_id))
                passed.append(copy(i, 3 + k, dsts[i].at[slot, half(i, c)], slot, c, sibling))
                sib_recvs.append(copy(i, 3 + k, my_half, slot, 1 - c, me_id))
        return local, first, ici_recvs, passed, sib_recvs

    def start(self, srcs, dsts, sems):
        local, first, _, _, _ = self._copies(srcs, dsts, sems)
        for cp in local + first:
            cp.start()

    def finish(self, srcs, dsts, sems):
        local, first, ici_recvs, passed, sib_recvs = self._copies(srcs, dsts, sems)
        for rc, fw in zip(ici_recvs, passed):
            rc.wait_recv()
            fw.start()
        for cp in sib_recvs:
            cp.wait_recv()
        for cp in first + passed:
            cp.wait_send()
        for cp in local:
            cp.wait()


class ExchangeComm:
    def __init__(self, chip_ops, all_ops=()):
        self.ops = list(chip_ops) + list(all_ops)
        self.per_chip = (True,) * len(chip_ops) + (False,) * len(all_ops)
        n = len(self.ops)
        self.out_shapes = [jax.ShapeDtypeStruct((N_DEV,) + o.shape[-2:], o.dtype) for o in self.ops]
        self.sem_shapes = [pltpu.SemaphoreType.DMA((7 * n,)), pltpu.SemaphoreType.DMA((7 * n,)), pltpu.SemaphoreType.DMA((n,))]

    def _copies(self, srcs, dsts, sems):
        send_sems, recv_sems, local_sems = sems
        n, per_chip = len(self.ops), self.per_chip
        x, y, c = _mesh_pos()
        me_id, sibling = (x, y, c), (x, y, 1 - c)
        chips = [(1 - x, y), (x, 1 - y), (1 - x, 1 - y)]

        def dev(px, py, pc):
            return 4 * px + 2 * py + pc

        def part(i, px, py):
            return srcs[i].at[2 * px + py] if per_chip[i] else srcs[i]

        def copy(i, k, src, slot, to):
            return pltpu.make_async_remote_copy(src_ref=src, dst_ref=dsts[i].at[slot], send_sem=send_sems.at[i * 7 + k],
                                                recv_sem=recv_sems.at[i * 7 + k], device_id=to, device_id_type=pl.DeviceIdType.MESH)

        me = dev(x, y, c)
        local = [pltpu.make_async_copy(part(i, x, y), dsts[i].at[me], local_sems.at[i]) for i in range(n)]
        first, ici_recvs, passed, sib_recvs = [], [], [], []
        for i in range(n):
            first.append(copy(i, 0, part(i, x, y), me, sibling))
            first += [copy(i, 1 + k, part(i, px, py), me, (px, py, c)) for k, (px, py) in enumerate(chips)]
            sib_recvs.append(copy(i, 0, part(i, x, y), dev(x, y, 1 - c), me_id))
            for k, (px, py) in enumerate(chips):
                slot = dev(px, py, c)
                ici_recvs.append(copy(i, 1 + k, part(i, x, y), slot, me_id))
                passed.append(copy(i, 4 + k, dsts[i].at[slot], slot, sibling))
                sib_recvs.append(copy(i, 4 + k, part(i, x, y), dev(px, py, 1 - c), me_id))
        return local, first, ici_recvs, passed, sib_recvs

    def start(self, srcs, dsts, sems):
        local, first, _, _, _ = self._copies(srcs, dsts, sems)
        for cp in local + first:
            cp.start()

    def finish(self, srcs, dsts, sems):
        local, first, ici_recvs, passed, sib_recvs = self._copies(srcs, dsts, sems)
        for rc, fw in zip(ici_recvs, passed):
            rc.wait_recv()
            fw.start()
        for cp in sib_recvs:
            cp.wait_recv()
        for cp in first + passed:
            cp.wait_send()
        for cp in local:
            cp.wait()


def _run_comm(name, comm):
    n = len(comm.ops)

    def body(*refs):
        srcs, dsts, sems = refs[:n], refs[n:2 * n], refs[2 * n:]
        comm.start(srcs, dsts, sems)
        comm.finish(srcs, dsts, sems)

    any_spec = pl.BlockSpec(memory_space=pl.ANY)
    return pl.pallas_call(body, name=name, in_specs=[any_spec] * n, out_specs=[any_spec] * n, out_shape=comm.out_shapes,
                          scratch_shapes=comm.sem_shapes)(*comm.ops)


def _adamw(name, parts, w, m, v, lead=False):
    plist = list(parts) if isinstance(parts, (list, tuple)) else [parts]
    n_p = len(plist)
    rows, cols = w.shape[-2:]
    t = next(c for c in (ADAM_ROWS, ADAM_ROWS // 2, SMALL_ROWS) if all(p.shape[1] % c == 0 for p in plist))
    starts = [sum(p.shape[1] for p in plist[:k]) // t for k in range(n_p)]
    counts = [p.shape[1] // t for p in plist]
    assert sum(p.shape[1] for p in plist) == rows, (name, rows)
    c1 = 1.0 - ADAM_B1 ** ADAM_STEP
    c2 = 1.0 - ADAM_B2 ** ADAM_STEP

    def body(*refs):
        p_refs = refs[:n_p]
        w_ref, m_ref, v_ref, g_ref, d_ref, nm_ref, nv_ref = refs[n_p:]
        g = None
        for k, p_ref in enumerate(p_refs):
            gk = p_ref[0].astype(F32)
            for d in range(1, N_DEV):
                gk = gk + p_ref[d].astype(F32)
            g = gk if g is None else jnp.where(pl.program_id(0) >= starts[k], gk, g)
        nm = ADAM_B1 * m_ref[...] + (1.0 - ADAM_B1) * g
        nv = ADAM_B2 * v_ref[...] + (1.0 - ADAM_B2) * (g * g)
        g_ref[...] = g
        nm_ref[...] = nm
        nv_ref[...] = nv
        d_ref[...] = -ADAM_LR * ((nm / c1) / (jnp.sqrt(nv / c2) + ADAM_EPS) + ADAM_WD * w_ref[...])

    row = pl.BlockSpec((None, t, cols), lambda i: (0, i, 0)) if lead else pl.BlockSpec((t, cols), lambda i: (i, 0))
    return pl.pallas_call(
        body, name=name, grid=(rows // t,),
        in_specs=[pl.BlockSpec((N_DEV, t, cols), lambda i, lo=lo, n=n: (0, jnp.clip(i - lo, 0, n - 1), 0)) for lo, n in zip(starts, counts)]
        + [row, row, row],
        out_specs=[row] * 4, out_shape=[jax.ShapeDtypeStruct(w.shape, F32)] * 4,
        compiler_params=_CP(dimension_semantics=("parallel",)),
    )(*plist, w, m, v)


def _train_step(x, positions, target, wts, ms, vs, raw):
    small_shapes = [wts[n].shape for n in SMALL]

    big_of = {tag: [n for n in BIG if n.startswith(tag)] for tag in ("mla", "gla", "lru", "ssd")}
    full = {n: wts[n] for n in REPL}

    def gather_comm(names, extra=()):
        return GatherComm([wts[n].astype(BF16) for n in names] + list(extra))

    def assemble(names, got):
        for k, n in enumerate(names):
            full[n] = jnp.concatenate([got[k][j] for j in range(N_CHIPS)], axis=_shard_axis(n))

    first = [n for n in big_of["mla"] if n != "mla_w_out"]
    got = _run_comm("gather_first", gather_comm(first, [_pack([wts[n] for n in SMALL], F32, SMALL_ROWS)]))
    assemble(first, got)
    per_chip_small = [_unpack(got[-1][j], small_shapes) for j in range(N_CHIPS)]
    for k, n in enumerate(SMALL):
        full[n] = jnp.concatenate([per_chip_small[j][k] for j in range(N_CHIPS)], axis=_shard_axis(n))

    cos, sin = _rope_tables(positions)
    ng = full["norm_g"]
    behind_attn = ["mla_w_out"] + big_of["gla"] + big_of["lru"]

    def mla_w_out(got):
        assemble(behind_attn, got)
        return full["mla_w_out"].astype(BF16)

    h1, b0, _ = _mla_layer(x, ng[0], full, cos, sin, fwd_comm=gather_comm(behind_attn), late_w_out=mla_w_out)

    def joined(got_k, axis):
        return jnp.concatenate([got_k[j] for j in range(N_CHIPS)], axis=axis)

    ssd_in = wts["ssd_w_in"].astype(BF16)
    half = ssd_in.shape[0] // 2
    h2, b1, got = _gla_layer(h1, ng[1], full, fwd_comm=GatherComm([ssd_in[:half]]))
    top = joined(got[0], -1)
    h3, b2, got = _lru_layer(h2, ng[2], full, fwd_comm=GatherComm([ssd_in[half:]]))
    full["ssd_w_in"] = jnp.concatenate([top, joined(got[0], -1)], axis=0)
    h4, b3 = _ssd_layer(h3, ng[3], full, fwd_comm=gather_comm(["ssd_w_out"]), late_w_out=lambda got: joined(got[0], 0))
    loss, dh, d_final = _loss_op(h4, target, full["final_g"])
    loss = loss[0, 0]
    grads = {"final_g": d_final.reshape(-1)}
    d_norms = [None] * 4

    def shards_of(n, g):
        return jnp.stack(jnp.split(g.astype(BF16), N_CHIPS, axis=_shard_axis(n)))

    def shards(n):
        return shards_of(n, grads[n])

    parts = {}
    dh, d_norms[3], gw, got = b3(dh, make_comm=lambda dw: ExchangeComm([shards_of("ssd_w_out", dw)]))
    parts["ssd_w_out"] = got[0]
    grads.update(gw)
    ssd_in_g = shards("ssd_w_in")
    half = ssd_in_g.shape[1] // 2
    dh, d_norms[2], gw, got = b2(dh, make_comm=lambda dw: ExchangeComm([ssd_in_g[:, :half], shards_of("lru_w_out", dw)]))
    parts["ssd_w_in"], parts["lru_w_out"] = [got[0]], got[1]
    grads.update(gw)
    dh, d_norms[1], gw, got = b1(dh, make_comm=lambda dw: ExchangeComm([ssd_in_g[:, half:]]))
    parts["ssd_w_in"].append(got[0])
    grads.update(gw)
    repl_early = _pack([grads[n] for n in REPL_EARLY], BF16, SMALL_ROWS)

    behind_attn = ["gla_w_out", "lru_w_in", "gla_w_in"]
    dx, d_norms[0], gw, got = b0(dh, make_comm=lambda dw: ExchangeComm([shards_of("mla_w_out", dw)] + [shards(n) for n in behind_attn],
                                                                        [repl_early]))
    parts.update(zip(["mla_w_out"] + behind_attn, got))
    repl_early_parts = got[-1]
    grads.update(gw)
    grads["norm_g"] = jnp.concatenate(d_norms, axis=0)
    psmall = jnp.stack([_pack([jnp.split(grads[n], N_CHIPS, axis=_shard_axis(n))[j] for n in SMALL], F32, SMALL_ROWS) for j in range(N_CHIPS)])
    prepl = _pack([grads[n] for n in REPL_LATE], F32, SMALL_ROWS)
    late = [n for n in big_of["mla"] if n != "mla_w_out"]
    late_parts = _run_comm("exchange_last", ExchangeComm([shards(n) for n in late] + [psmall], [prepl]))
    parts.update(zip(late, late_parts))

    out = {}
    kinds = ("grad", "delta", "new_m", "new_v")
    for n in BIG:
        for kind, a in zip(kinds, _adamw("adam_" + n, parts[n], *(r[n] for r in raw), lead=True)):
            out[kind, n] = a
    for tag, names, p in (("adam_small", SMALL, late_parts[-2]), ("adam_repl_early", REPL_EARLY, repl_early_parts),
                          ("adam_repl_late", REPL_LATE, late_parts[-1])):
        shapes = [wts[n].shape for n in names]
        packed = [_pack([d[n] for n in names], F32, SMALL_ROWS) for d in (wts, ms, vs)]
        for kind, buf in zip(kinds, _adamw(tag, p, *packed)):
            for n, a in zip(names, _unpack(buf, shapes)):
                out[kind, n] = a
    loss = lax.psum(loss, ("x", "y", "c"))
    return loss, dx, out


def kernel(x, positions, norm_g, final_g, mla_w_in, mla_g_q, mla_w_uq, mla_g_kv, mla_w_ukv, mla_w_out, gla_w_in, gla_w_gk2, gla_b_gk, gla_g_o, gla_w_out, lru_w_in, lru_conv_w, lru_conv_b, lru_w_a, lru_b_a, lru_w_x, lru_b_x, lru_lam, lru_w_out, ssd_w_in, ssd_conv_w, ssd_conv_b, ssd_dt_bias, ssd_a_log, ssd_d, ssd_g_norm, ssd_w_out, loss_target, m_norm_g, m_final_g, m_mla_w_in, m_mla_g_q, m_mla_w_uq, m_mla_g_kv, m_mla_w_ukv, m_mla_w_out, m_gla_w_in, m_gla_w_gk2, m_gla_b_gk, m_gla_g_o, m_gla_w_out, m_lru_w_in, m_lru_conv_w, m_lru_conv_b, m_lru_w_a, m_lru_b_a, m_lru_w_x, m_lru_b_x, m_lru_lam, m_lru_w_out, m_ssd_w_in, m_ssd_conv_w, m_ssd_conv_b, m_ssd_dt_bias, m_ssd_a_log, m_ssd_d, m_ssd_g_norm, m_ssd_w_out, v_norm_g, v_final_g, v_mla_w_in, v_mla_g_q, v_mla_w_uq, v_mla_g_kv, v_mla_w_ukv, v_mla_w_out, v_gla_w_in, v_gla_w_gk2, v_gla_b_gk, v_gla_g_o, v_gla_w_out, v_lru_w_in, v_lru_conv_w, v_lru_conv_b, v_lru_w_a, v_lru_b_a, v_lru_w_x, v_lru_b_x, v_lru_lam, v_lru_w_out, v_ssd_w_in, v_ssd_conv_w, v_ssd_conv_b, v_ssd_dt_bias, v_ssd_a_log, v_ssd_d, v_ssd_g_norm, v_ssd_w_out):
    given = dict(locals())
    stacked = [n for n in WEIGHTS if n not in ("norm_g", "final_g")]

    def blocks(prefix):
        return {n: (given[prefix + n][0] if n in stacked else given[prefix + n]) for n in WEIGHTS}

    raw = [{n: given[prefix + n] for n in BIG} for prefix in ("", "m_", "v_")]
    loss, dx, out = _train_step(x[0], positions[0], loss_target[0], blocks(""), blocks("m_"), blocks("v_"), raw)
    res = [loss, dx[None]]
    for kind in ("grad", "delta", "new_m", "new_v"):
        res += [(out[kind, n][None] if n in stacked and n not in BIG else out[kind, n]) for n in WEIGHTS]
    return tuple(res)
```

```python
import functools
import math

import jax
import jax.numpy as jnp
from jax import lax
from jax.experimental import pallas as pl
from jax.experimental.pallas import tpu as pltpu

F32 = jnp.float32
BF16 = jnp.bfloat16

V7X_VMEM_BYTES = 64 * 1024 * 1024
VMEM_LIMIT = V7X_VMEM_BYTES - 8 * 1024 * 1024
LANE = 128

D_MODEL = 1024
NORM_EPS = 1e-6
MLA_HEADS, MLA_Q_RANK, MLA_KV_RANK, MLA_NOPE, MLA_ROPE, MLA_V = 16, 384, 256, 64, 32, 64
MLA_QK = MLA_NOPE + MLA_ROPE
ROPE_THETA = 10000.0
GLA_HEADS, GLA_DK, GLA_DV, GLA_RANK, GLA_TAU, GLA_CHUNK = 4, 128, 256, 16, 16.0, 64
LRU_WIDTH, LRU_BLOCKS, LRU_BLOCK, LRU_C, CONV_W = 1280, 10, 128, 8.0, 4
SSD_INNER, SSD_P, SSD_HEADS, SSD_GROUPS, SSD_HPG, SSD_STATE, SSD_CHUNK = 2048, 64, 32, 8, 4, 128, 64
ADAM_LR, ADAM_B1, ADAM_B2, ADAM_EPS, ADAM_WD, ADAM_STEP = 0.001, 0.9, 0.999, 1e-08, 0.01, 10

_CP = functools.partial(pltpu.CompilerParams, vmem_limit_bytes=VMEM_LIMIT)


def _bdot(a, b):
    return jnp.dot(a.astype(BF16), b.astype(BF16), preferred_element_type=F32)


def _bdot_nt(a, b):
    return lax.dot_general(a.astype(BF16), b.astype(BF16), (((1,), (1,)), ((), ())), preferred_element_type=F32)


def _bdot_tn(a, b):
    return lax.dot_general(a.astype(BF16), b.astype(BF16), (((0,), (0,)), ((), ())), preferred_element_type=F32)


def _tri(n):
    r = lax.broadcasted_iota(jnp.int32, (n, n), 0)
    c = lax.broadcasted_iota(jnp.int32, (n, n), 1)
    return r >= c


def _rms(x, g):
    return x * lax.rsqrt(jnp.mean(x * x, axis=-1, keepdims=True) + NORM_EPS) * g


def _silu(x):
    return x * jax.nn.sigmoid(x)


def _shift_rows(x, prev, j):
    if j == 0:
        return x
    t = x.shape[0]

    def fwd_impl(x, prev):
        row = lax.broadcasted_iota(jnp.int32, x.shape, 0)
        return jnp.where(row >= j, pltpu.roll(x, j, 0), pltpu.roll(prev, j, 0))

    @jax.custom_vjp
    def sh(x, prev):
        return fwd_impl(x, prev)

    def sh_fwd(x, prev):
        return fwd_impl(x, prev), None

    def sh_bwd(_, gy):
        row = lax.broadcasted_iota(jnp.int32, gy.shape, 0)
        back = pltpu.roll(gy, t - j, 0)
        return jnp.where(row < t - j, back, 0.0), jnp.where(row >= t - j, back, 0.0)

    sh.defvjp(sh_fwd, sh_bwd)
    return sh(x, prev)


def _cumsum_rows(x):
    zero = jnp.zeros_like(x)
    sh = 1
    while sh < x.shape[0]:
        x = x + _shift_rows(x, zero, sh)
        sh *= 2
    return x


def _one_minus_exp(x):
    series = -x * (1.0 + x * (0.5 + x * (1.0 / 6.0 + x * (1.0 / 24.0 + x * (1.0 / 120.0)))))
    return jnp.where(x > -0.05, series, 1.0 - jnp.exp(x))


def _tile(n, cap):
    if n <= cap:
        return n
    best = None
    for t in range(LANE, cap + 1, LANE):
        if n % t == 0:
            best = t
    assert best is not None, (n, cap)
    return best


MM_BLOCK_BYTES = 8 * 1024 * 1024
MM_ROWS, MM_KROWS = 256, 512


def _mm_tiles(m, k, n, ta):
    if ta:
        return m, _tile(n, max(LANE, MM_BLOCK_BYTES // (4 * m) // LANE * LANE)), _tile(k, MM_KROWS)
    return _tile(m, MM_ROWS), _tile(n, max(LANE, MM_BLOCK_BYTES // (2 * k) // LANE * LANE)), k


def _mm(name, a, b, *, ta=False, tb=False, add=None, out_dtype=F32):
    m, k = (a.shape[1], a.shape[0]) if ta else a.shape
    n, kb = (b.shape[0], b.shape[1]) if tb else (b.shape[1], b.shape[0])
    assert k == kb, (name, a.shape, b.shape, ta, tb)
    tm, tn, tk = _mm_tiles(m, k, n, ta)
    nk = k // tk
    dn = (((0 if ta else 1,), (1 if tb else 0,)), ((), ()))
    has_add = add is not None

    def finish(refs, r):
        if has_add:
            r = r + refs[2][...].astype(F32)
        return r.astype(out_dtype)

    def body_one(*refs):
        a_ref, b_ref, o_ref = refs[0], refs[1], refs[-1]
        o_ref[...] = finish(refs, lax.dot_general(a_ref[...].astype(BF16), b_ref[...].astype(BF16), dn, preferred_element_type=F32))

    def body_acc(*refs):
        a_ref, b_ref = refs[0], refs[1]
        o_ref, acc = refs[-2], refs[-1]
        kk = pl.program_id(2)

        @pl.when(kk == 0)
        def _():
            acc[...] = jnp.zeros(acc.shape, F32)

        acc[...] += lax.dot_general(a_ref[...].astype(BF16), b_ref[...].astype(BF16), dn, preferred_element_type=F32)

        @pl.when(kk == nk - 1)
        def _():
            o_ref[...] = finish(refs, acc[...])

    a_spec = pl.BlockSpec((tk, tm), lambda i, j, q: (q, i)) if ta else pl.BlockSpec((tm, tk), lambda i, j, q: (i, q))
    b_spec = pl.BlockSpec((tn, tk), lambda i, j, q: (j, q)) if tb else pl.BlockSpec((tk, tn), lambda i, j, q: (q, j))
    o_spec = pl.BlockSpec((tm, tn), lambda i, j, q: (i, j))
    in_specs, args = [a_spec, b_spec], [a, b]
    if has_add:
        in_specs.append(o_spec)
        args.append(add)
    return pl.pallas_call(
        body_one if nk == 1 else body_acc, name=name, grid=(m // tm, n // tn, nk), in_specs=in_specs, out_specs=o_spec,
        out_shape=jax.ShapeDtypeStruct((m, n), out_dtype), scratch_shapes=[] if nk == 1 else [pltpu.VMEM((tm, tn), F32)],
        compiler_params=_CP(dimension_semantics=("parallel", "parallel", "arbitrary")),
    )(*args)


class In:
    def __init__(self, arr, block, imap, kind="x", per_h=False, gdtype=F32, gshape=None, gimap=None, prefixed=False):
        self.arr, self.block, self.imap, self.kind, self.per_h, self.gdtype = arr, tuple(block), imap, kind, per_h, gdtype
        self.prefixed = prefixed
        self.gshape = tuple(gshape) if gshape is not None else tuple(arr.shape)
        self.gimap = gimap if gimap is not None else imap

    def spec(self, rev_g=None):
        imap = self.imap
        if rev_g is None:
            return pl.BlockSpec(self.block, lambda h, g: imap(h, g))
        return pl.BlockSpec(self.block, lambda h, g: imap(h, rev_g - 1 - g))


class Out:
    def __init__(self, shape, dtype, block, imap):
        self.shape, self.dtype, self.block, self.imap = tuple(shape), dtype, tuple(block), imap

    def spec(self, rev_g=None):
        imap = self.imap
        if rev_g is None:
            return pl.BlockSpec(self.block, lambda h, g: imap(h, g))
        return pl.BlockSpec(self.block, lambda h, g: imap(h, rev_g - 1 - g))


def _load_f32(ref, rows=None):
    v = ref[...] if rows is None else ref[0:rows]
    return v.astype(F32) if jnp.issubdtype(v.dtype, jnp.floating) else v


def _state_out(grid, shape):
    nd = len(shape)
    return Out(tuple(grid) + tuple(shape), F32, (None, None) + tuple(shape), lambda h, g: (h, g) + (0,) * nd)


def _carry(comm, grid, refs, n_in, n_out, n_scr):
    n_c = len(comm.ops) if comm is not None else 0
    n_s = len(comm.sem_shapes) if comm is not None else 0
    p = 0
    in_refs = refs[p:p + n_in]; p += n_in
    c_src = refs[p:p + n_c]; p += n_c
    out_refs = refs[p:p + n_out]; p += n_out
    c_dst = refs[p:p + n_c]; p += n_c
    scr = refs[p:p + n_scr]; p += n_scr
    c_sem = refs[p:p + n_s]
    first = jnp.logical_and(pl.program_id(0) == 0, pl.program_id(1) == 0)
    last = jnp.logical_and(pl.program_id(0) == grid[0] - 1, pl.program_id(1) == grid[1] - 1)
    return in_refs, out_refs, scr, (c_src, c_dst, c_sem), first, last


def _carry_specs(comm):
    if comm is None:
        return [], [], [], [], []
    any_spec = pl.BlockSpec(memory_space=pl.ANY)
    n = len(comm.ops)
    return [any_spec] * n, list(comm.ops), [any_spec] * n, list(comm.out_shapes), list(comm.sem_shapes)


def _op_fwd(name, f, grid, ins, outs, state_shapes=(), comm=None, prefix_rows=None):
    assert prefix_rows is None or not state_shapes
    n_in, n_out, n_st = len(ins), len(outs), len(state_shapes)
    st_outs = [_state_out(grid, s) for s in state_shapes]

    def body(*refs):
        in_refs, o_refs, st_scr, cargs, first, last = _carry(comm, grid, refs, n_in, n_out + n_st, n_st)
        out_refs, sv_refs = o_refs[:n_out], o_refs[n_out:]
        if comm is not None:
            @pl.when(first)
            def _():
                comm.start(*cargs)
        g = pl.program_id(1)
        if n_st:
            @pl.when(g == 0)
            def _():
                for s in st_scr:
                    s[...] = jnp.zeros(s.shape, F32)

        def compute(rows):
            vals = [_load_f32(r, rows if i.prefixed else None) for r, i in zip(in_refs, ins)]
            sts = [s[...] for s in st_scr]
            o, ns = f(g, vals, sts)
            for r, v in zip(out_refs, o):
                r[...] = v.astype(r.dtype)
            for r, s in zip(sv_refs, sts):
                r[...] = s
            for s, v in zip(st_scr, ns):
                s[...] = v

        if prefix_rows is None:
            compute(None)
        else:
            per = grid[1] // len(prefix_rows)
            for lv, rows in enumerate(prefix_rows):
                pl.when(g // per == lv)(functools.partial(compute, rows))
        if comm is not None:
            @pl.when(last)
            def _():
                comm.finish(*cargs)

    all_outs = list(outs) + st_outs
    c_in_specs, c_args, c_out_specs, c_out_shapes, c_sems = _carry_specs(comm)
    res = pl.pallas_call(
        body, name=name, grid=tuple(grid), in_specs=[i.spec() for i in ins] + c_in_specs,
        out_specs=[o.spec() for o in all_outs] + c_out_specs,
        out_shape=[jax.ShapeDtypeStruct(o.shape, o.dtype) for o in all_outs] + c_out_shapes,
        scratch_shapes=[pltpu.VMEM(tuple(s), F32) for s in state_shapes] + c_sems,
        compiler_params=_CP(dimension_semantics=("arbitrary", "arbitrary")),
    )(*[i.arr for i in ins], *c_args)
    n_all = n_out + n_st
    return list(res[:n_out]), list(res[n_out:n_all]), list(res[n_all:])


def _op_bwd(name, f, grid, ins, outs, state_shapes, saved, douts, addto=None, comm=None, prefix_rows=None):
    n_in, n_out, n_st = len(ins), len(outs), len(state_shapes)
    n_g = grid[1]
    assert prefix_rows is None or all(i.prefixed and i.per_h for i in ins if i.kind == "p")
    addto = addto or {}
    diff = [k for k, i in enumerate(ins) if i.kind in ("x", "p")]
    add_idx = sorted(addto)
    st_ins = [In(s, o.block, o.imap, "c") for s, o in zip(saved, [_state_out(grid, s) for s in state_shapes])]
    dout_ins = [In(d, o.block, o.imap, "c") for d, o in zip(douts, outs)]
    add_ins = []
    for k in add_idx:
        i, a = ins[k], addto[k]
        blk = i.block if i.kind == "x" else i.block[:-2] + a.shape[-2:]
        add_ins.append(In(a, blk, i.gimap if i.kind == "x" else i.imap, "c"))
    g_outs = []
    for k in diff:
        i = ins[k]
        g_outs.append(Out(i.gshape, i.gdtype if i.kind == "x" else F32, i.block, i.gimap))

    def body(*refs):
        all_in, go_refs, ds_scr, cargs, first_step, last_step = _carry(comm, grid, refs, n_in + n_st + n_out + len(add_idx), len(diff), n_st)
        if comm is not None:
            @pl.when(first_step)
            def _():
                comm.start(*cargs)
        p = 0
        in_refs = all_in[p:p + n_in]; p += n_in
        sv_refs = all_in[p:p + n_st]; p += n_st
        do_refs = all_in[p:p + n_out]; p += n_out
        ad_refs = all_in[p:p + len(add_idx)]
        hh = pl.program_id(0)
        step = pl.program_id(1)
        g = n_g - 1 - step
        if n_st:
            @pl.when(step == 0)
            def _():
                for s in ds_scr:
                    s[...] = jnp.zeros(s.shape, F32)

        def compute(rows):
            vals = [_load_f32(r, rows if i.prefixed else None) for r, i in zip(in_refs, ins)]
            sts = [r[...] for r in sv_refs]

            def fw(dvals, states):
                full = list(vals)
                for k, v in zip(diff, dvals):
                    full[k] = v
                o, ns = f(g, full, states)
                return list(o), list(ns)

            _, vjp = jax.vjp(fw, [vals[k] for k in diff], sts)
            cts = [r[...].astype(F32) for r in do_refs]
            dns = [s[...] for s in ds_scr]
            dvals, dsts = vjp((cts, dns))
            adds = dict(zip(add_idx, ad_refs))
            for k, r, dv in zip(diff, go_refs, dvals):
                i = ins[k]
                if i.kind == "x":
                    if k in adds:
                        dv = dv + adds[k][...].astype(F32)
                    r[...] = dv.astype(r.dtype)
                elif rows is not None:
                    r[0:rows] += dv
                else:
                    first = (step == 0) if i.per_h else jnp.logical_and(step == 0, hh == 0)

                    @pl.when(first)
                    def _(r=r, dv=dv, k=k):
                        r[...] = dv
                        if k in adds:
                            lead = adds[k].shape[0]
                            r[0:lead] += adds[k][...]

                    @pl.when(jnp.logical_not(first))
                    def _(r=r, dv=dv):
                        r[...] += dv
            for s, v in zip(ds_scr, dsts):
                s[...] = v

        if prefix_rows is None:
            compute(None)
        else:
            @pl.when(step == 0)
            def _():
                for k, r in zip(diff, go_refs):
                    if ins[k].kind == "p":
                        r[...] = jnp.zeros(r.shape, F32)
            per = n_g // len(prefix_rows)
            for lv, rows in enumerate(prefix_rows):
                pl.when(g // per == lv)(functools.partial(compute, rows))
        if comm is not None:
            @pl.when(last_step)
            def _():
                comm.finish(*cargs)

    all_ins = list(ins) + st_ins + dout_ins + add_ins
    c_in_specs, c_args, c_out_specs, c_out_shapes, c_sems = _carry_specs(comm)
    res = pl.pallas_call(
        body, name=name, grid=tuple(grid), in_specs=[i.spec(n_g) for i in all_ins] + c_in_specs,
        out_specs=[o.spec(n_g) for o in g_outs] + c_out_specs,
        out_shape=[jax.ShapeDtypeStruct(o.shape, o.dtype) for o in g_outs] + c_out_shapes,
        scratch_shapes=[pltpu.VMEM(tuple(s), F32) for s in state_shapes] + c_sems,
        compiler_params=_CP(dimension_semantics=("arbitrary", "arbitrary")),
    )(*[i.arr for i in all_ins], *c_args)
    return list(res[:len(g_outs)]), list(res[len(g_outs):])


class Op:
    def __init__(self, name, f, grid, ins, outs, state_shapes=(), prefix_rows=None):
        self.name, self.f, self.grid, self.ins, self.outs, self.state_shapes = name, f, grid, ins, outs, state_shapes
        self.prefix_rows = prefix_rows
        self.saved = None

    def fwd(self, comm=None):
        res, self.saved, self.fwd_comm_out = _op_fwd(self.name + "_fwd", self.f, self.grid, self.ins, self.outs, self.state_shapes, comm,
                                                     self.prefix_rows)
        return res

    def bwd(self, douts, addto=None, comm=None):
        res, self.bwd_comm_out = _op_bwd(self.name + "_bwd", self.f, self.grid, self.ins, self.outs, self.state_shapes, self.saved, douts,
                                         addto, comm, self.prefix_rows)
        return res


def _rows(arr, t, kind="x", gdtype=F32):
    return In(arr, (t, arr.shape[1]), lambda h, g: (g, 0), kind, gdtype=gdtype)


def _whole(arr, kind="p"):
    nd = arr.ndim
    return In(arr, arr.shape, lambda h, g: (0,) * nd, kind)


def _rows_out(s, n, t, dtype):
    return Out((s, n), dtype, (t, n), lambda h, g: (g, 0))


ROW_T = 256


def _rms_op(name, x, gain, out_dtype=BF16, gdtype=F32):
    s, n = x.shape

    def f(g, vals, sts):
        return [_rms(vals[0], vals[1])], []

    return Op(name, f, (1, s // ROW_T), [_rows(x, ROW_T, gdtype=gdtype), _whole(gain.reshape(1, n))], [_rows_out(s, n, ROW_T, out_dtype)])


def _mla_prep_op(qn, q1, q2, kn, kr, v, cos, sin):
    s = qn.shape[0]
    hd, half = MLA_HEADS, MLA_ROPE // 2

    def f(g, vals, sts):
        qn, q1, q2, kn, kr, v, cos, sin = vals
        cos_h, sin_h = jnp.tile(cos, (1, hd)), jnp.tile(sin, (1, hd))
        r1 = q1 * cos_h - q2 * sin_h
        r2 = q2 * cos_h + q1 * sin_h
        k1, k2 = kr[:, 0:half], kr[:, half:2 * half]
        kr1 = k1 * cos - k2 * sin
        kr2 = k2 * cos + k1 * sin
        zpad = jnp.zeros((qn.shape[0], LANE - MLA_QK), F32)
        qs, ks, vs = [], [], []
        for h in range(hd):
            a, b = h * MLA_NOPE, (h + 1) * MLA_NOPE
            c, d = h * half, (h + 1) * half
            qs.append(jnp.concatenate([qn[:, a:b], r1[:, c:d], r2[:, c:d], zpad], axis=1))
            ks.append(jnp.concatenate([kn[:, a:b], kr1, kr2, zpad], axis=1))
            vs.append(v[:, a:b])
        return [jnp.stack(qs, 0), jnp.stack(ks, 0), jnp.stack(vs, 0)], []

    ins = [_rows(qn, ROW_T, gdtype=BF16), _rows(q1, ROW_T, gdtype=BF16), _rows(q2, ROW_T, gdtype=BF16), _rows(kn, ROW_T, gdtype=BF16),
           _rows(kr, ROW_T, gdtype=BF16), _rows(v, ROW_T, gdtype=BF16), _rows(cos, ROW_T, "c"), _rows(sin, ROW_T, "c")]
    outs = [Out((hd, s, LANE), BF16, (hd, ROW_T, LANE), lambda h, g: (0, g, 0)),
            Out((hd, s, LANE), BF16, (hd, ROW_T, LANE), lambda h, g: (0, g, 0)),
            Out((hd, s, MLA_V), BF16, (hd, ROW_T, MLA_V), lambda h, g: (0, g, 0))]
    return Op("mla_prep", f, (1, s // ROW_T), ins, outs)


ATT_TQ = 256
ATT_LEVELS = 4


def _mla_attn_op(q, k, v):
    hd, s, _ = q.shape
    scale = MLA_QK ** -0.5

    def f(g, vals, sts):
        q, k, v = vals
        sc = _bdot_nt(q, k) * scale
        r = lax.broadcasted_iota(jnp.int32, sc.shape, 0) + g * ATT_TQ
        c = lax.broadcasted_iota(jnp.int32, sc.shape, 1)
        sc = jnp.where(r >= c, sc, -1e30)
        m = lax.stop_gradient(jnp.max(sc, axis=-1, keepdims=True))
        p = jnp.exp(sc - m)
        p = p / jnp.sum(p, axis=-1, keepdims=True)
        return [_bdot(p, v)], []

    ins = [In(q, (None, ATT_TQ, LANE), lambda h, g: (h, g, 0), "x", gdtype=BF16),
           In(k, (None, s, LANE), lambda h, g: (h, 0, 0), "p", per_h=True, prefixed=True),
           In(v, (None, s, MLA_V), lambda h, g: (h, 0, 0), "p", per_h=True, prefixed=True)]
    outs = [Out((hd, s, MLA_V), F32, (None, ATT_TQ, MLA_V), lambda h, g: (h, g, 0))]
    return Op("mla_attn", f, (hd, s // ATT_TQ), ins, outs, prefix_rows=[(lv + 1) * (s // ATT_LEVELS) for lv in range(ATT_LEVELS)])


def _mla_post_op(o, gate):
    hd, s, _ = o.shape

    def f(g, vals, sts):
        o, gate = vals
        cat = jnp.concatenate([o[h] for h in range(hd)], axis=1)
        return [cat * _silu(gate)], []

    ins = [In(o, (hd, ROW_T, MLA_V), lambda h, g: (0, g, 0), "x"), _rows(gate, ROW_T, gdtype=BF16)]
    return Op("mla_post", f, (1, s // ROW_T), ins, [_rows_out(s, hd * MLA_V, ROW_T, BF16)])


def _gla_gate_op(gk, w2, b):
    s = gk.shape[0]

    def f(g, vals, sts):
        gk, w2, b = vals
        return [jax.nn.log_sigmoid(_bdot(gk, w2) + b) / GLA_TAU], []

    ins = [_rows(gk, ROW_T, gdtype=BF16), _whole(w2), _whole(b)]
    return Op("gla_gate", f, (1, s // ROW_T), ins, [_rows_out(s, GLA_HEADS * GLA_DK, ROW_T, F32)])


def _gla_core_op(q, k, v, gate, la, g_o):
    s = q.shape[0]
    c, nh = GLA_CHUNK, GLA_HEADS

    def f(g, vals, sts):
        q, k, v, gate, la, g_o = vals
        tri = _tri(c)
        b = _cumsum_rows(la)
        b_last = jnp.sum(la, axis=0, keepdims=True)
        qt = q * (GLA_DK ** -0.5) * jnp.exp(b)
        kt = k * jnp.exp(-b)
        kd = k * jnp.exp(b_last - b)
        ys, new_sts = [], []
        for h in range(nh):
            ks, vs = slice(h * GLA_DK, (h + 1) * GLA_DK), slice(h * GLA_DV, (h + 1) * GLA_DV)
            att = jnp.where(tri, _bdot_nt(qt[:, ks], kt[:, ks]), 0.0)
            o = _bdot(att, v[:, vs]) + _bdot_nt(qt[:, ks], sts[h])
            new_sts.append(jnp.exp(b_last[:, ks]) * sts[h] + _bdot_tn(v[:, vs], kd[:, ks]))
            ys.append(_rms(o, g_o) * _silu(gate[:, vs]))
        return [jnp.concatenate(ys, axis=1)], new_sts

    ins = [_rows(q, c, gdtype=BF16), _rows(k, c, gdtype=BF16), _rows(v, c, gdtype=BF16), _rows(gate, c, gdtype=BF16), _rows(la, c), _whole(g_o)]
    outs = [_rows_out(s, nh * GLA_DV, c, BF16)]
    return Op("gla_core", f, (1, s // c), ins, outs, [(GLA_DV, GLA_DK)] * nh)


LRU_T = 256


def _lru_op(gate, u, conv_w, conv_b, w_a, b_a, w_x, b_x, lam):
    s, w = u.shape
    t = LRU_T

    def f(g, vals, sts):
        gate, u, cw, cb, w_a, b_a, w_x, b_x, lam = vals
        u_prev, h_prev = sts
        uc = cb
        for kk in range(CONV_W):
            uc = uc + cw[kk] * _shift_rows(u, u_prev, CONV_W - 1 - kk)
        ra, ri = [], []
        for n in range(LRU_BLOCKS):
            blk = uc[:, n * LRU_BLOCK:(n + 1) * LRU_BLOCK]
            ra.append(_bdot(blk, w_a[n]))
            ri.append(_bdot(blk, w_x[n]))
        r = jax.nn.sigmoid(jnp.concatenate(ra, axis=1) + b_a)
        i = jax.nn.sigmoid(jnp.concatenate(ri, axis=1) + b_x)
        log_a = -LRU_C * r * jax.nn.softplus(-lam)
        a = jnp.exp(log_a)
        bb = jnp.sqrt(_one_minus_exp(2.0 * log_a)) * (i * uc)
        zero = jnp.zeros_like(a)
        sh = 1
        while sh < t:
            a_s = _shift_rows(a - 1.0, zero, sh) + 1.0
            b_s = _shift_rows(bb, zero, sh)
            bb = a * b_s + bb
            a = a * a_s
            sh *= 2
        hs = bb + a * h_prev
        last = (lax.broadcasted_iota(jnp.int32, hs.shape, 0) == t - 1).astype(F32)
        h_last = jnp.sum(hs * last, axis=0, keepdims=True)
        return [hs * _silu(gate)], [u, h_last]

    ins = [_rows(gate, t, gdtype=BF16), _rows(u, t, gdtype=BF16), _whole(conv_w), _whole(conv_b), _whole(w_a), _whole(b_a), _whole(w_x),
           _whole(b_x), _whole(lam)]
    return Op("lru_core", f, (1, s // t), ins, [_rows_out(s, w, t, BF16)], [(t, w), (1, w)])


def _ssd_conv_op(xbc, conv_w, conv_b):
    s, w = xbc.shape
    t = ROW_T
    n_x, n_b = SSD_INNER, SSD_GROUPS * SSD_STATE

    def f(g, vals, sts):
        xbc, cw, cb = vals
        acc = cb
        for kk in range(CONV_W):
            acc = acc + cw[kk] * _shift_rows(xbc, sts[0], CONV_W - 1 - kk)
        y = _silu(acc)
        return [y[:, :n_x], y[:, n_x:n_x + n_b], y[:, n_x + n_b:]], [xbc]

    ins = [_rows(xbc, t, gdtype=BF16), _whole(conv_w), _whole(conv_b)]
    outs = [_rows_out(s, n_x, t, F32), _rows_out(s, n_b, t, F32), _rows_out(s, n_b, t, F32)]
    return Op("ssd_conv", f, (1, s // t), ins, outs, [(t, w)])


SSD_L = 512


def _ssd_core_op(x, bm, cm, z, dt, dt_bias, a_log, d_skip, g_norm):
    s = x.shape[0]
    c, hg, p = SSD_L, SSD_HPG, SSD_P
    gw = hg * p

    def f(g, vals, sts):
        x, bm, cm, z, dtr, dt_bias, a_log, d_skip, g_norm = vals
        tri = _tri(c)
        dt = jax.nn.softplus(dtr + dt_bias)
        da = dt * (-jnp.exp(a_log))
        cs = _cumsum_rows(da)
        cs_last = jnp.sum(da, axis=0, keepdims=True)
        cs_t = jnp.transpose(jnp.concatenate([cs, jnp.zeros((c, LANE - hg), F32)], axis=1))
        cb = _bdot_nt(cm, bm)
        ys, new_st = [], []
        for h in range(hg):
            cs_h = cs[:, h:h + 1]
            cs_row = cs_t[h:h + 1, :]
            seg = jnp.where(tri, cs_h - cs_row, 0.0)
            lmat = jnp.where(tri, jnp.exp(seg), 0.0)
            x_h = x[:, h * p:(h + 1) * p]
            xdt = x_h * dt[:, h:h + 1]
            y_diag = _bdot(cb * lmat, xdt)
            decay = jnp.exp(cs_last[:, h:h + 1] - cs_h)
            states = _bdot_tn(xdt * decay, bm)
            y_off = _bdot_nt(cm, sts[h]) * jnp.exp(cs_h)
            new_st.append(jnp.exp(cs_last[:, h:h + 1]) * sts[h] + states)
            ys.append(y_diag + y_off + d_skip[:, h:h + 1] * x_h)
        y = jnp.concatenate(ys, axis=1) * _silu(z)
        return [_rms(y, g_norm)], new_st

    ins = [In(x, (c, gw), lambda h, g: (g, h), "x"), In(bm, (c, SSD_STATE), lambda h, g: (g, h), "x"),
           In(cm, (c, SSD_STATE), lambda h, g: (g, h), "x"), In(z, (c, gw), lambda h, g: (g, h), "x", gdtype=BF16),
           In(dt, (None, c, hg), lambda h, g: (h, g, 0), "x"),
           In(dt_bias, (None, 1, hg), lambda h, g: (h, 0, 0), "p", per_h=True),
           In(a_log, (None, 1, hg), lambda h, g: (h, 0, 0), "p", per_h=True),
           In(d_skip, (None, 1, hg), lambda h, g: (h, 0, 0), "p", per_h=True),
           In(g_norm, (1, gw), lambda h, g: (0, h), "p", per_h=True)]
    outs = [Out((s, SSD_INNER), BF16, (c, gw), lambda h, g: (g, h))]
    return Op("ssd_core", f, (SSD_GROUPS, s // c), ins, outs, [(p, SSD_STATE)] * hg)


def _loss_op(h, target, final_g):
    s, n = h.shape
    t = ROW_T
    n_g = s // t

    def body(h_ref, t_ref, g_ref, loss_ref, dh_ref, dg_ref):
        step = pl.program_id(0)

        def lossf(hv, gv):
            err = _rms(hv, gv) - t_ref[...]
            return 0.5 * jnp.sum(jnp.mean(err * err, axis=-1))

        l, (dh, dg) = jax.value_and_grad(lossf, argnums=(0, 1))(h_ref[...], g_ref[...])
        dh_ref[...] = dh

        @pl.when(step == 0)
        def _():
            loss_ref[...] = jnp.zeros(loss_ref.shape, F32)
            dg_ref[...] = jnp.zeros(dg_ref.shape, F32)

        loss_ref[...] += jnp.full(loss_ref.shape, l, F32)
        dg_ref[...] += dg

    row = pl.BlockSpec((t, n), lambda g: (g, 0))
    one = pl.BlockSpec((1, n), lambda g: (0, 0))
    return pl.pallas_call(
        body, name="loss_head", grid=(n_g,), in_specs=[row, row, one],
        out_specs=[pl.BlockSpec((1, LANE), lambda g: (0, 0)), row, one],
        out_shape=[jax.ShapeDtypeStruct((1, LANE), F32), jax.ShapeDtypeStruct((s, n), F32), jax.ShapeDtypeStruct((1, n), F32)],
        compiler_params=_CP(dimension_semantics=("arbitrary",)),
    )(h, target, final_g.reshape(1, n))


def _pad_cols(w, n):
    return jnp.pad(w, ((0, 0), (0, n - w.shape[1])))


def _pad_rows(w, n):
    return jnp.pad(w, ((0, n - w.shape[0]), (0, 0)))


def _proj_bwd(tag, u, dps, ws):
    du = None
    for i, (dp, w) in enumerate(zip(dps, ws)):
        du = _mm(f"{tag}_du{i}", dp, w, tb=True, add=du)
    dws = [_mm(f"{tag}_dw{i}", u, dp, ta=True, out_dtype=BF16) for i, dp in enumerate(dps)]
    return du, dws


def _mla_layer(h, norm_g, w, cos, sin, fwd_comm=None, late_w_out=None):
    bf = lambda a: a.astype(BF16)
    w_in, w_uq, w_ukv = w["mla_w_in"], w["mla_w_uq"], w["mla_w_ukv"]
    a0, a1, a2 = MLA_Q_RANK, MLA_Q_RANK + MLA_KV_RANK, MLA_Q_RANK + MLA_KV_RANK + MLA_ROPE
    w_cq, w_ckv, w_kr, w_g = bf(w_in[:, :a0]), bf(w_in[:, a0:a1]), bf(_pad_cols(w_in[:, a1:a2], LANE)), bf(w_in[:, a2:])
    uq = w_uq.reshape(MLA_Q_RANK, MLA_HEADS, MLA_QK)
    half = MLA_ROPE // 2
    w_qn = bf(uq[:, :, :MLA_NOPE].reshape(MLA_Q_RANK, -1))
    w_q1 = bf(uq[:, :, MLA_NOPE:MLA_NOPE + half].reshape(MLA_Q_RANK, -1))
    w_q2 = bf(uq[:, :, MLA_NOPE + half:].reshape(MLA_Q_RANK, -1))
    ukv = w_ukv.reshape(MLA_KV_RANK, MLA_HEADS, MLA_NOPE + MLA_V)
    w_kn = bf(ukv[:, :, :MLA_NOPE].reshape(MLA_KV_RANK, -1))
    w_v = bf(ukv[:, :, MLA_NOPE:].reshape(MLA_KV_RANK, -1))

    n0 = _rms_op("mla_norm", h, norm_g)
    u, = n0.fwd()
    cq, ckv, kr, gate = (_mm(f"mla_in{i}", u, wi) for i, wi in enumerate((w_cq, w_ckv, w_kr, w_g)))
    nq = _rms_op("mla_qnorm", cq, w["mla_g_q"], gdtype=BF16)
    nkv = _rms_op("mla_kvnorm", ckv, w["mla_g_kv"], gdtype=BF16)
    qn_, = nq.fwd()
    kvn_, = nkv.fwd()
    qn, q1, q2 = (_mm(f"mla_uq{i}", qn_, wi) for i, wi in enumerate((w_qn, w_q1, w_q2)))
    kn, v = (_mm(f"mla_ukv{i}", kvn_, wi) for i, wi in enumerate((w_kn, w_v)))
    prep = _mla_prep_op(qn, q1, q2, kn, kr, v, cos, sin)
    qh, kh, vh = prep.fwd()
    attn = _mla_attn_op(qh, kh, vh)
    o, = attn.fwd(fwd_comm)
    w_out = bf(w["mla_w_out"]) if late_w_out is None else late_w_out(attn.fwd_comm_out)
    post = _mla_post_op(o, gate)
    y, = post.fwd()
    h_out = _mm("mla_out", y, w_out, add=h)

    def bwd(dh, make_comm=None):
        dy = _mm("mla_out_dy", dh, w_out, tb=True, out_dtype=BF16)
        d_w_out = _mm("mla_out_dw", y, dh, ta=True, out_dtype=BF16)
        do, dgate = post.bwd([dy])
        dqh, dkh, dvh = attn.bwd([do], comm=None if make_comm is None else make_comm(d_w_out))
        dqn, dq1, dq2, dkn, dkr, dv = prep.bwd([dqh, dkh, dvh])
        dqn_, d_uq = _proj_bwd("mla_uq", qn_, (dqn, dq1, dq2), (w_qn, w_q1, w_q2))
        dkvn_, d_ukv = _proj_bwd("mla_ukv", kvn_, (dkn, dv), (w_kn, w_v))
        dcq, d_g_q = nq.bwd([dqn_])
        dckv, d_g_kv = nkv.bwd([dkvn_])
        du, d_in = _proj_bwd("mla_in", u, (dcq, dckv, dkr, dgate), (w_cq, w_ckv, w_kr, w_g))
        dh_in, d_norm = n0.bwd([du], addto={0: dh})
        shp = (MLA_Q_RANK, MLA_HEADS, -1)
        g_uq = jnp.concatenate([d_uq[0].reshape(shp), d_uq[1].reshape(shp), d_uq[2].reshape(shp)], axis=2).reshape(MLA_Q_RANK, -1)
        shp = (MLA_KV_RANK, MLA_HEADS, -1)
        g_ukv = jnp.concatenate([d_ukv[0].reshape(shp), d_ukv[1].reshape(shp)], axis=2).reshape(MLA_KV_RANK, -1)
        g_in = jnp.concatenate([d_in[0], d_in[1], d_in[2][:, :MLA_ROPE], d_in[3]], axis=1)
        return dh_in, d_norm, {"mla_w_in": g_in, "mla_g_q": d_g_q.reshape(-1), "mla_w_uq": g_uq, "mla_g_kv": d_g_kv.reshape(-1),
                               "mla_w_ukv": g_ukv, "mla_w_out": d_w_out}, attn.bwd_comm_out

    return h_out, bwd, attn.fwd_comm_out


def _gla_layer(h, norm_g, w, fwd_comm=None):
    bf = lambda a: a.astype(BF16)
    w_in = w["gla_w_in"]
    nk, nv = GLA_HEADS * GLA_DK, GLA_HEADS * GLA_DV
    cuts = (0, nk, 2 * nk, 2 * nk + nv, 2 * nk + 2 * nv)
    w_q, w_k, w_v, w_g = (bf(w_in[:, cuts[i]:cuts[i + 1]]) for i in range(4))
    w_gk = bf(_pad_cols(w_in[:, cuts[4]:], LANE))
    w2 = _pad_rows(w["gla_w_gk2"], LANE)
    b_gk = w["gla_b_gk"].reshape(1, -1)
    g_o = w["gla_g_o"].reshape(1, -1)
    w_out = bf(w["gla_w_out"])

    n0 = _rms_op("gla_norm", h, norm_g)
    u, = n0.fwd()
    q, k, v, gate, gk = (_mm(f"gla_in{i}", u, wi) for i, wi in enumerate((w_q, w_k, w_v, w_g, w_gk)))
    gop = _gla_gate_op(gk, w2, b_gk)
    la, = gop.fwd()
    core = _gla_core_op(q, k, v, gate, la, g_o)
    y, = core.fwd(fwd_comm)
    h_out = _mm("gla_out", y, w_out, add=h)

    def bwd(dh, make_comm=None):
        dy = _mm("gla_out_dy", dh, w_out, tb=True, out_dtype=BF16)
        d_w_out = _mm("gla_out_dw", y, dh, ta=True, out_dtype=BF16)
        dq, dk, dv, dgate, dla, d_g_o = core.bwd([dy], comm=None if make_comm is None else make_comm(d_w_out))
        dgk, d_w2, d_b = gop.bwd([dla])
        du, d_in = _proj_bwd("gla_in", u, (dq, dk, dv, dgate, dgk), (w_q, w_k, w_v, w_g, w_gk))
        dh_in, d_norm = n0.bwd([du], addto={0: dh})
        g_in = jnp.concatenate([d_in[0], d_in[1], d_in[2], d_in[3], d_in[4][:, :GLA_RANK]], axis=1)
        return dh_in, d_norm, {"gla_w_in": g_in, "gla_w_gk2": d_w2[:GLA_RANK], "gla_b_gk": d_b.reshape(-1), "gla_g_o": d_g_o.reshape(-1),
                               "gla_w_out": d_w_out}, core.bwd_comm_out

    return h_out, bwd, core.fwd_comm_out


def _lru_layer(h, norm_g, w, fwd_comm=None):
    bf = lambda a: a.astype(BF16)
    w_in = w["lru_w_in"]
    w_g, w_u = bf(w_in[:, :LRU_WIDTH]), bf(w_in[:, LRU_WIDTH:])
    row = lambda a: a.reshape(1, -1)
    w_out = bf(w["lru_w_out"])

    n0 = _rms_op("lru_norm", h, norm_g)
    u_, = n0.fwd()
    gate, u = (_mm(f"lru_in{i}", u_, wi) for i, wi in enumerate((w_g, w_u)))
    core = _lru_op(gate, u, w["lru_conv_w"].reshape(CONV_W, 1, -1), row(w["lru_conv_b"]), w["lru_w_a"], row(w["lru_b_a"]), w["lru_w_x"],
                   row(w["lru_b_x"]), row(w["lru_lam"]))
    y, = core.fwd(fwd_comm)
    h_out = _mm("lru_out", y, w_out, add=h)

    def bwd(dh, make_comm=None):
        dy = _mm("lru_out_dy", dh, w_out, tb=True, out_dtype=BF16)
        d_w_out = _mm("lru_out_dw", y, dh, ta=True, out_dtype=BF16)
        dgate, du, d_cw, d_cb, d_wa, d_ba, d_wx, d_bx, d_lam = core.bwd([dy], comm=None if make_comm is None else make_comm(d_w_out))
        du_, d_in = _proj_bwd("lru_in", u_, (dgate, du), (w_g, w_u))
        dh_in, d_norm = n0.bwd([du_], addto={0: dh})
        return dh_in, d_norm, {"lru_w_in": jnp.concatenate(d_in, axis=1), "lru_conv_w": d_cw.reshape(CONV_W, -1), "lru_conv_b": d_cb.reshape(-1),
                               "lru_w_a": d_wa, "lru_b_a": d_ba.reshape(-1), "lru_w_x": d_wx, "lru_b_x": d_bx.reshape(-1),
                               "lru_lam": d_lam.reshape(-1), "lru_w_out": d_w_out}, core.bwd_comm_out

    return h_out, bwd, core.fwd_comm_out


def _ssd_layer(h, norm_g, w, fwd_comm=None, late_w_out=None):
    bf = lambda a: a.astype(BF16)
    s = h.shape[0]
    w_in = w["ssd_w_in"]
    conv_dim = SSD_INNER + 2 * SSD_GROUPS * SSD_STATE
    w_z, w_xbc = bf(w_in[:, :SSD_INNER]), bf(w_in[:, SSD_INNER:SSD_INNER + conv_dim])
    w_dt = bf(_pad_cols(w_in[:, SSD_INNER + conv_dim:], LANE))
    grp = lambda a: a.reshape(SSD_GROUPS, 1, SSD_HPG)

    n0 = _rms_op("ssd_norm", h, norm_g)
    u, = n0.fwd()
    z, xbc, dtp = (_mm(f"ssd_in{i}", u, wi) for i, wi in enumerate((w_z, w_xbc, w_dt)))
    conv = _ssd_conv_op(xbc, w["ssd_conv_w"].reshape(CONV_W, 1, -1), w["ssd_conv_b"].reshape(1, -1))
    x, bm, cm = conv.fwd()
    dt = dtp[:, :SSD_HEADS].reshape(s, SSD_GROUPS, SSD_HPG).transpose(1, 0, 2)
    core = _ssd_core_op(x, bm, cm, z, dt, grp(w["ssd_dt_bias"]), grp(w["ssd_a_log"]), grp(w["ssd_d"]), w["ssd_g_norm"].reshape(1, -1))
    y, = core.fwd(fwd_comm)
    w_out = bf(w["ssd_w_out"]) if late_w_out is None else late_w_out(core.fwd_comm_out)
    h_out = _mm("ssd_out", y, w_out, add=h)

    def bwd(dh, make_comm=None):
        dy = _mm("ssd_out_dy", dh, w_out, tb=True, out_dtype=BF16)
        d_w_out = _mm("ssd_out_dw", y, dh, ta=True, out_dtype=BF16)
        dx, dbm, dcm, dz, ddt, d_dtb, d_alog, d_d, d_gn = core.bwd([dy], comm=None if make_comm is None else make_comm(d_w_out))
        dxbc, d_cw, d_cb = conv.bwd([dx, dbm, dcm])
        ddtp = _pad_cols(ddt.transpose(1, 0, 2).reshape(s, SSD_HEADS), LANE).astype(BF16)
        du, d_in = _proj_bwd("ssd_in", u, (dz, dxbc, ddtp), (w_z, w_xbc, w_dt))
        dh_in, d_norm = n0.bwd([du], addto={0: dh})
        g_in = jnp.concatenate([d_in[0], d_in[1], d_in[2][:, :SSD_HEADS]], axis=1)
        return dh_in, d_norm, {"ssd_w_in": g_in, "ssd_conv_w": d_cw.reshape(CONV_W, -1), "ssd_conv_b": d_cb.reshape(-1),
                               "ssd_dt_bias": d_dtb.reshape(-1), "ssd_a_log": d_alog.reshape(-1), "ssd_d": d_d.reshape(-1),
                               "ssd_g_norm": d_gn.reshape(-1), "ssd_w_out": d_w_out}, core.bwd_comm_out

    return h_out, bwd


def _rope_tables(positions):
    inv_freq = ROPE_THETA ** (-jnp.arange(0, MLA_ROPE, 2, dtype=F32) / MLA_ROPE)
    ang = positions.astype(F32)[:, None] * inv_freq
    return jnp.cos(ang), jnp.sin(ang)


WEIGHTS = ["norm_g", "final_g", "mla_w_in", "mla_g_q", "mla_w_uq", "mla_g_kv", "mla_w_ukv", "mla_w_out", "gla_w_in", "gla_w_gk2", "gla_b_gk",
           "gla_g_o", "gla_w_out", "lru_w_in", "lru_conv_w", "lru_conv_b", "lru_w_a", "lru_b_a", "lru_w_x", "lru_b_x", "lru_lam", "lru_w_out",
           "ssd_w_in", "ssd_conv_w", "ssd_conv_b", "ssd_dt_bias", "ssd_a_log", "ssd_d", "ssd_g_norm", "ssd_w_out"]
BIG = ["mla_w_in", "mla_w_uq", "mla_w_ukv", "mla_w_out", "gla_w_in", "gla_w_out", "lru_w_in", "lru_w_out", "ssd_w_in", "ssd_w_out"]
SMALL = ["gla_w_gk2", "gla_b_gk", "gla_g_o", "lru_conv_w", "lru_conv_b", "lru_b_a", "lru_b_x", "lru_lam", "ssd_conv_w", "ssd_conv_b", "ssd_g_norm"]
REPL = ["norm_g", "final_g", "mla_g_q", "mla_g_kv", "lru_w_a", "lru_w_x", "ssd_dt_bias", "ssd_a_log", "ssd_d"]
REPL_EARLY = ["lru_w_a", "lru_w_x"]
REPL_LATE = [n for n in REPL if n not in REPL_EARLY]
N_CHIPS, N_DEV = 4, 8
PACK_W = 1024
ADAM_ROWS = 256
SMALL_ROWS = 64


def _shard_axis(name):
    return 0 if name.endswith("_w_out") else -1


def _pack(arrs, dtype, row_mult):
    flat = jnp.concatenate([a.reshape(-1).astype(dtype) for a in arrs])
    per = PACK_W * row_mult
    total = -(-flat.shape[0] // per) * per
    return jnp.pad(flat, (0, total - flat.shape[0])).reshape(-1, PACK_W)


def _unpack(buf, shapes):
    flat = buf.reshape(-1)
    out, off = [], 0
    for s in shapes:
        n = math.prod(s)
        out.append(flat[off:off + n].reshape(s))
        off += n
    return out


def _mesh_pos():
    return lax.axis_index("x"), lax.axis_index("y"), lax.axis_index("c")


class GatherComm:
    def __init__(self, ops):
        self.ops = list(ops)
        n = len(self.ops)
        assert all(o.ndim == 2 and o.shape[0] % 32 == 0 for o in self.ops), [o.shape for o in self.ops]
        self.out_shapes = [jax.ShapeDtypeStruct((N_CHIPS,) + o.shape, o.dtype) for o in self.ops]
        self.sem_shapes = [pltpu.SemaphoreType.DMA((6 * n,)), pltpu.SemaphoreType.DMA((6 * n,)), pltpu.SemaphoreType.DMA((n,))]

    def _copies(self, srcs, dsts, sems):
        send_sems, recv_sems, local_sems = sems
        n = len(self.ops)
        x, y, c = _mesh_pos()
        me_id, sibling = (x, y, c), (x, y, 1 - c)
        chips = [(1 - x, y), (x, 1 - y), (1 - x, 1 - y)]
        mine = 2 * x + y

        def half(i, cc):
            h = self.ops[i].shape[0] // 2
            return pl.ds(cc * h, h)

        def copy(i, k, src, slot, cc, to):
            return pltpu.make_async_remote_copy(src_ref=src, dst_ref=dsts[i].at[slot, half(i, cc)], send_sem=send_sems.at[i * 6 + k],
                                                recv_sem=recv_sems.at[i * 6 + k], device_id=to, device_id_type=pl.DeviceIdType.MESH)

        local = [pltpu.make_async_copy(srcs[i], dsts[i].at[mine], local_sems.at[i]) for i in range(n)]
        first, ici_recvs, passed, sib_recvs = [], [], [], []
        for i in range(n):
            my_half = srcs[i].at[half(i, c)]
            for k, (px, py) in enumerate(chips):
                slot = 2 * px + py
                first.append(copy(i, k, my_half, mine, c, (px, py, c)))
                ici_recvs.append(copy(i, k, my_half, slot, c, me_id))
                passed.append(copy(i, 3 + k, dsts[i].at[slot, half(i, c)], slot, c, sibling))
                sib_recvs.append(copy(i, 3 + k, my_half, slot, 1 - c, me_id))
        return local, first, ici_recvs, passed, sib_recvs

    def start(self, srcs, dsts, sems):
        local, first, _, _, _ = self._copies(srcs, dsts, sems)
        for cp in local + first:
            cp.start()

    def finish(self, srcs, dsts, sems):
        local, first, ici_recvs, passed, sib_recvs = self._copies(srcs, dsts, sems)
        for rc, fw in zip(ici_recvs, passed):
            rc.wait_recv()
            fw.start()
        for cp in sib_recvs:
            cp.wait_recv()
        for cp in first + passed:
            cp.wait_send()
        for cp in local:
            cp.wait()


class ExchangeComm:
    def __init__(self, chip_ops, all_ops=()):
        self.ops = list(chip_ops) + list(all_ops)
        self.per_chip = (True,) * len(chip_ops) + (False,) * len(all_ops)
        n = len(self.ops)
        self.out_shapes = [jax.ShapeDtypeStruct((N_DEV,) + o.shape[-2:], o.dtype) for o in self.ops]
        self.sem_shapes = [pltpu.SemaphoreType.DMA((7 * n,)), pltpu.SemaphoreType.DMA((7 * n,)), pltpu.SemaphoreType.DMA((n,))]

    def _copies(self, srcs, dsts, sems):
        send_sems, recv_sems, local_sems = sems
        n, per_chip = len(self.ops), self.per_chip
        x, y, c = _mesh_pos()
        me_id, sibling = (x, y, c), (x, y, 1 - c)
        chips = [(1 - x, y), (x, 1 - y), (1 - x, 1 - y)]

        def dev(px, py, pc):
            return 4 * px + 2 * py + pc

        def part(i, px, py):
            return srcs[i].at[2 * px + py] if per_chip[i] else srcs[i]

        def copy(i, k, src, slot, to):
            return pltpu.make_async_remote_copy(src_ref=src, dst_ref=dsts[i].at[slot], send_sem=send_sems.at[i * 7 + k],
                                                recv_sem=recv_sems.at[i * 7 + k], device_id=to, device_id_type=pl.DeviceIdType.MESH)

        me = dev(x, y, c)
        local = [pltpu.make_async_copy(part(i, x, y), dsts[i].at[me], local_sems.at[i]) for i in range(n)]
        first, ici_recvs, passed, sib_recvs = [], [], [], []
        for i in range(n):
            first.append(copy(i, 0, part(i, x, y), me, sibling))
            first += [copy(i, 1 + k, part(i, px, py), me, (px, py, c)) for k, (px, py) in enumerate(chips)]
            sib_recvs.append(copy(i, 0, part(i, x, y), dev(x, y, 1 - c), me_id))
            for k, (px, py) in enumerate(chips):
                slot = dev(px, py, c)
                ici_recvs.append(copy(i, 1 + k, part(i, x, y), slot, me_id))
                passed.append(copy(i, 4 + k, dsts[i].at[slot], slot, sibling))
                sib_recvs.append(copy(i, 4 + k, part(i, x, y), dev(px, py, 1 - c), me_id))
        return local, first, ici_recvs, passed, sib_recvs

    def start(self, srcs, dsts, sems):
        local, first, _, _, _ = self._copies(srcs, dsts, sems)
        for cp in local + first:
            cp.start()

    def finish(self, srcs, dsts, sems):
        local, first, ici_recvs, passed, sib_recvs = self._copies(srcs, dsts, sems)
        for rc, fw in zip(ici_recvs, passed):
            rc.wait_recv()
            fw.start()
        for cp in sib_recvs:
            cp.wait_recv()
        for cp in first + passed:
            cp.wait_send()
        for cp in local:
            cp.wait()


def _run_comm(name, comm):
    n = len(comm.ops)

    def body(*refs):
        srcs, dsts, sems = refs[:n], refs[n:2 * n], refs[2 * n:]
        comm.start(srcs, dsts, sems)
        comm.finish(srcs, dsts, sems)

    any_spec = pl.BlockSpec(memory_space=pl.ANY)
    return pl.pallas_call(body, name=name, in_specs=[any_spec] * n, out_specs=[any_spec] * n, out_shape=comm.out_shapes,
                          scratch_shapes=comm.sem_shapes)(*comm.ops)


def _adamw(name, parts, w, m, v, lead=False):
    plist = list(parts) if isinstance(parts, (list, tuple)) else [parts]
    n_p = len(plist)
    rows, cols = w.shape[-2:]
    t = next(c for c in (ADAM_ROWS, ADAM_ROWS // 2, SMALL_ROWS) if all(p.shape[1] % c == 0 for p in plist))
    starts = [sum(p.shape[1] for p in plist[:k]) // t for k in range(n_p)]
    counts = [p.shape[1] // t for p in plist]
    assert sum(p.shape[1] for p in plist) == rows, (name, rows)
    c1 = 1.0 - ADAM_B1 ** ADAM_STEP
    c2 = 1.0 - ADAM_B2 ** ADAM_STEP

    def body(*refs):
        p_refs = refs[:n_p]
        w_ref, m_ref, v_ref, g_ref, d_ref, nm_ref, nv_ref = refs[n_p:]
        g = None
        for k, p_ref in enumerate(p_refs):
            gk = p_ref[0].astype(F32)
            for d in range(1, N_DEV):
                gk = gk + p_ref[d].astype(F32)
            g = gk if g is None else jnp.where(pl.program_id(0) >= starts[k], gk, g)
        nm = ADAM_B1 * m_ref[...] + (1.0 - ADAM_B1) * g
        nv = ADAM_B2 * v_ref[...] + (1.0 - ADAM_B2) * (g * g)
        g_ref[...] = g
        nm_ref[...] = nm
        nv_ref[...] = nv
        d_ref[...] = -ADAM_LR * ((nm / c1) / (jnp.sqrt(nv / c2) + ADAM_EPS) + ADAM_WD * w_ref[...])

    row = pl.BlockSpec((None, t, cols), lambda i: (0, i, 0)) if lead else pl.BlockSpec((t, cols), lambda i: (i, 0))
    return pl.pallas_call(
        body, name=name, grid=(rows // t,),
        in_specs=[pl.BlockSpec((N_DEV, t, cols), lambda i, lo=lo, n=n: (0, jnp.clip(i - lo, 0, n - 1), 0)) for lo, n in zip(starts, counts)]
        + [row, row, row],
        out_specs=[row] * 4, out_shape=[jax.ShapeDtypeStruct(w.shape, F32)] * 4,
        compiler_params=_CP(dimension_semantics=("parallel",)),
    )(*plist, w, m, v)


def _train_step(x, positions, target, wts, ms, vs, raw):
    small_shapes = [wts[n].shape for n in SMALL]

    big_of = {tag: [n for n in BIG if n.startswith(tag)] for tag in ("mla", "gla", "lru", "ssd")}
    full = {n: wts[n] for n in REPL}

    def gather_comm(names, extra=()):
        return GatherComm([wts[n].astype(BF16) for n in names] + list(extra))

    def assemble(names, got):
        for k, n in enumerate(names):
            full[n] = jnp.concatenate([got[k][j] for j in range(N_CHIPS)], axis=_shard_axis(n))

    first = [n for n in big_of["mla"] if n != "mla_w_out"]
    got = _run_comm("gather_first", gather_comm(first, [_pack([wts[n] for n in SMALL], F32, SMALL_ROWS)]))
    assemble(first, got)
    per_chip_small = [_unpack(got[-1][j], small_shapes) for j in range(N_CHIPS)]
    for k, n in enumerate(SMALL):
        full[n] = jnp.concatenate([per_chip_small[j][k] for j in range(N_CHIPS)], axis=_shard_axis(n))

    cos, sin = _rope_tables(positions)
    ng = full["norm_g"]
    behind_attn = ["mla_w_out"] + big_of["gla"] + big_of["lru"]

    def mla_w_out(got):
        assemble(behind_attn, got)
        return full["mla_w_out"].astype(BF16)

    h1, b0, _ = _mla_layer(x, ng[0], full, cos, sin, fwd_comm=gather_comm(behind_attn), late_w_out=mla_w_out)

    def joined(got_k, axis):
        return jnp.concatenate([got_k[j] for j in range(N_CHIPS)], axis=axis)

    ssd_in = wts["ssd_w_in"].astype(BF16)
    half = ssd_in.shape[0] // 2
    h2, b1, got = _gla_layer(h1, ng[1], full, fwd_comm=GatherComm([ssd_in[:half]]))
    top = joined(got[0], -1)
    h3, b2, got = _lru_layer(h2, ng[2], full, fwd_comm=GatherComm([ssd_in[half:]]))
    full["ssd_w_in"] = jnp.concatenate([top, joined(got[0], -1)], axis=0)
    h4, b3 = _ssd_layer(h3, ng[3], full, fwd_comm=gather_comm(["ssd_w_out"]), late_w_out=lambda got: joined(got[0], 0))
    loss, dh, d_final = _loss_op(h4, target, full["final_g"])
    loss = loss[0, 0]
    grads = {"final_g": d_final.reshape(-1)}
    d_norms = [None] * 4

    def shards_of(n, g):
        return jnp.stack(jnp.split(g.astype(BF16), N_CHIPS, axis=_shard_axis(n)))

    def shards(n):
        return shards_of(n, grads[n])

    parts = {}
    dh, d_norms[3], gw, got = b3(dh, make_comm=lambda dw: ExchangeComm([shards_of("ssd_w_out", dw)]))
    parts["ssd_w_out"] = got[0]
    grads.update(gw)
    ssd_in_g = shards("ssd_w_in")
    half = ssd_in_g.shape[1] // 2
    dh, d_norms[2], gw, got = b2(dh, make_comm=lambda dw: ExchangeComm([ssd_in_g[:, :half], shards_of("lru_w_out", dw)]))
    parts["ssd_w_in"], parts["lru_w_out"] = [got[0]], got[1]
    grads.update(gw)
    dh, d_norms[1], gw, got = b1(dh, make_comm=lambda dw: ExchangeComm([ssd_in_g[:, half:]]))
    parts["ssd_w_in"].append(got[0])
    grads.update(gw)
    repl_early = _pack([grads[n] for n in REPL_EARLY], BF16, SMALL_ROWS)

    behind_attn = ["gla_w_out", "lru_w_in", "gla_w_in"]
    dx, d_norms[0], gw, got = b0(dh, make_comm=lambda dw: ExchangeComm([shards_of("mla_w_out", dw)] + [shards(n) for n in behind_attn],
                                                                        [repl_early]))
    parts.update(zip(["mla_w_out"] + behind_attn, got))
    repl_early_parts = got[-1]
    grads.update(gw)
    grads["norm_g"] = jnp.concatenate(d_norms, axis=0)
    psmall = jnp.stack([_pack([jnp.split(grads[n], N_CHIPS, axis=_shard_axis(n))[j] for n in SMALL], F32, SMALL_ROWS) for j in range(N_CHIPS)])
    prepl = _pack([grads[n] for n in REPL_LATE], F32, SMALL_ROWS)
    late = [n for n in big_of["mla"] if n != "mla_w_out"]
    late_parts = _run_comm("exchange_last", ExchangeComm([shards(n) for n in late] + [psmall], [prepl]))
    parts.update(zip(late, late_parts))

    out = {}
    kinds = ("grad", "delta", "new_m", "new_v")
    for n in BIG:
        for kind, a in zip(kinds, _adamw("adam_" + n, parts[n], *(r[n] for r in raw), lead=True)):
            out[kind, n] = a
    for tag, names, p in (("adam_small", SMALL, late_parts[-2]), ("adam_repl_early", REPL_EARLY, repl_early_parts),
                          ("adam_repl_late", REPL_LATE, late_parts[-1])):
        shapes = [wts[n].shape for n in names]
        packed = [_pack([d[n] for n in names], F32, SMALL_ROWS) for d in (wts, ms, vs)]
        for kind, buf in zip(kinds, _adamw(tag, p, *packed)):
            for n, a in zip(names, _unpack(buf, shapes)):
                out[kind, n] = a
    loss = lax.psum(loss, ("x", "y", "c"))
    return loss, dx, out


def kernel(x, positions, norm_g, final_g, mla_w_in, mla_g_q, mla_w_uq, mla_g_kv, mla_w_ukv, mla_w_out, gla_w_in, gla_w_gk2, gla_b_gk, gla_g_o, gla_w_out, lru_w_in, lru_conv_w, lru_conv_b, lru_w_a, lru_b_a, lru_w_x, lru_b_x, lru_lam, lru_w_out, ssd_w_in, ssd_conv_w, ssd_conv_b, ssd_dt_bias, ssd_a_log, ssd_d, ssd_g_norm, ssd_w_out, loss_target, m_norm_g, m_final_g, m_mla_w_in, m_mla_g_q, m_mla_w_uq, m_mla_g_kv, m_mla_w_ukv, m_mla_w_out, m_gla_w_in, m_gla_w_gk2, m_gla_b_gk, m_gla_g_o, m_gla_w_out, m_lru_w_in, m_lru_conv_w, m_lru_conv_b, m_lru_w_a, m_lru_b_a, m_lru_w_x, m_lru_b_x, m_lru_lam, m_lru_w_out, m_ssd_w_in, m_ssd_conv_w, m_ssd_conv_b, m_ssd_dt_bias, m_ssd_a_log, m_ssd_d, m_ssd_g_norm, m_ssd_w_out, v_norm_g, v_final_g, v_mla_w_in, v_mla_g_q, v_mla_w_uq, v_mla_g_kv, v_mla_w_ukv, v_mla_w_out, v_gla_w_in, v_gla_w_gk2, v_gla_b_gk, v_gla_g_o, v_gla_w_out, v_lru_w_in, v_lru_conv_w, v_lru_conv_b, v_lru_w_a, v_lru_b_a, v_lru_w_x, v_lru_b_x, v_lru_lam, v_lru_w_out, v_ssd_w_in, v_ssd_conv_w, v_ssd_conv_b, v_ssd_dt_bias, v_ssd_a_log, v_ssd_d, v_ssd_g_norm, v_ssd_w_out):
    given = dict(locals())
    stacked = [n for n in WEIGHTS if n not in ("norm_g", "final_g")]

    def blocks(prefix):
        return {n: (given[prefix + n][0] if n in stacked else given[prefix + n]) for n in WEIGHTS}

    raw = [{n: given[prefix + n] for n in BIG} for prefix in ("", "m_", "v_")]
    loss, dx, out = _train_step(x[0], positions[0], loss_target[0], blocks(""), blocks("m_"), blocks("v_"), raw)
    res = [loss, dx[None]]
    for kind in ("grad", "delta", "new_m", "new_v"):
        res += [(out[kind, n][None] if n in stacked and n not in BIG else out[kind, n]) for n in WEIGHTS]
    return tuple(res)
```

```python
import functools
import math

import jax
import jax.numpy as jnp
from jax import lax
from jax.experimental import pallas as pl
from jax.experimental.pallas import tpu as pltpu

F32 = jnp.float32
BF16 = jnp.bfloat16

V7X_VMEM_BYTES = 64 * 1024 * 1024
VMEM_LIMIT = V7X_VMEM_BYTES - 8 * 1024 * 1024
LANE = 128

D_MODEL = 1024
NORM_EPS = 1e-6
MLA_HEADS, MLA_Q_RANK, MLA_KV_RANK, MLA_NOPE, MLA_ROPE, MLA_V = 16, 384, 256, 64, 32, 64
MLA_QK = MLA_NOPE + MLA_ROPE
ROPE_THETA = 10000.0
GLA_HEADS, GLA_DK, GLA_DV, GLA_RANK, GLA_TAU, GLA_CHUNK = 4, 128, 256, 16, 16.0, 64
LRU_WIDTH, LRU_BLOCKS, LRU_BLOCK, LRU_C, CONV_W = 1280, 10, 128, 8.0, 4
SSD_INNER, SSD_P, SSD_HEADS, SSD_GROUPS, SSD_HPG, SSD_STATE, SSD_CHUNK = 2048, 64, 32, 8, 4, 128, 64
ADAM_LR, ADAM_B1, ADAM_B2, ADAM_EPS, ADAM_WD, ADAM_STEP = 0.001, 0.9, 0.999, 1e-08, 0.01, 10

_CP = functools.partial(pltpu.CompilerParams, vmem_limit_bytes=VMEM_LIMIT)


def _bdot(a, b):
    return jnp.dot(a.astype(BF16), b.astype(BF16), preferred_element_type=F32)


def _bdot_nt(a, b):
    return lax.dot_general(a.astype(BF16), b.astype(BF16), (((1,), (1,)), ((), ())), preferred_element_type=F32)


def _bdot_tn(a, b):
    return lax.dot_general(a.astype(BF16), b.astype(BF16), (((0,), (0,)), ((), ())), preferred_element_type=F32)


def _tri(n):
    r = lax.broadcasted_iota(jnp.int32, (n, n), 0)
    c = lax.broadcasted_iota(jnp.int32, (n, n), 1)
    return r >= c


def _rms(x, g):
    return x * lax.rsqrt(jnp.mean(x * x, axis=-1, keepdims=True) + NORM_EPS) * g


def _silu(x):
    return x * jax.nn.sigmoid(x)


def _shift_rows(x, prev, j):
    if j == 0:
        return x
    t = x.shape[0]

    def fwd_impl(x, prev):
        row = lax.broadcasted_iota(jnp.int32, x.shape, 0)
        return jnp.where(row >= j, pltpu.roll(x, j, 0), pltpu.roll(prev, j, 0))

    @jax.custom_vjp
    def sh(x, prev):
        return fwd_impl(x, prev)

    def sh_fwd(x, prev):
        return fwd_impl(x, prev), None

    def sh_bwd(_, gy):
        row = lax.broadcasted_iota(jnp.int32, gy.shape, 0)
        back = pltpu.roll(gy, t - j, 0)
        return jnp.where(row < t - j, back, 0.0), jnp.where(row >= t - j, back, 0.0)

    sh.defvjp(sh_fwd, sh_bwd)
    return sh(x, prev)


def _cumsum_rows(x):
    zero = jnp.zeros_like(x)
    sh = 1
    while sh < x.shape[0]:
        x = x + _shift_rows(x, zero, sh)
        sh *= 2
    return x


def _one_minus_exp(x):
    series = -x * (1.0 + x * (0.5 + x * (1.0 / 6.0 + x * (1.0 / 24.0 + x * (1.0 / 120.0)))))
    return jnp.where(x > -0.05, series, 1.0 - jnp.exp(x))


def _tile(n, cap):
    if n <= cap:
        return n
    best = None
    for t in range(LANE, cap + 1, LANE):
        if n % t == 0:
            best = t
    assert best is not None, (n, cap)
    return best


MM_BLOCK_BYTES = 8 * 1024 * 1024
MM_ROWS, MM_KROWS = 256, 512
MM_TILE_BYTES = 2 * 1024 * 1024


def _mm_tiles(m, k, n, ta):
    if ta:
        return m, _tile(n, max(LANE, MM_BLOCK_BYTES // (4 * m) // LANE * LANE)), _tile(k, MM_KROWS)
    tn = _tile(n, max(LANE, MM_BLOCK_BYTES // (2 * k) // LANE * LANE))
    rows = min(4 * MM_ROWS, max(MM_ROWS, MM_TILE_BYTES // (4 * tn) // MM_ROWS * MM_ROWS))
    return _tile(m, rows), tn, k


def _mm(name, a, b, *, ta=False, tb=False, add=None, out_dtype=F32):
    m, k = (a.shape[1], a.shape[0]) if ta else a.shape
    n, kb = (b.shape[0], b.shape[1]) if tb else (b.shape[1], b.shape[0])
    assert k == kb, (name, a.shape, b.shape, ta, tb)
    tm, tn, tk = _mm_tiles(m, k, n, ta)
    nk = k // tk
    dn = (((0 if ta else 1,), (1 if tb else 0,)), ((), ()))
    has_add = add is not None

    def finish(refs, r):
        if has_add:
            r = r + refs[2][...].astype(F32)
        return r.astype(out_dtype)

    def body_one(*refs):
        a_ref, b_ref, o_ref = refs[0], refs[1], refs[-1]
        o_ref[...] = finish(refs, lax.dot_general(a_ref[...].astype(BF16), b_ref[...].astype(BF16), dn, preferred_element_type=F32))

    def body_acc(*refs):
        a_ref, b_ref = refs[0], refs[1]
        o_ref, acc = refs[-2], refs[-1]
        kk = pl.program_id(2)

        @pl.when(kk == 0)
        def _():
            acc[...] = jnp.zeros(acc.shape, F32)

        acc[...] += lax.dot_general(a_ref[...].astype(BF16), b_ref[...].astype(BF16), dn, preferred_element_type=F32)

        @pl.when(kk == nk - 1)
        def _():
            o_ref[...] = finish(refs, acc[...])

    a_spec = pl.BlockSpec((tk, tm), lambda i, j, q: (q, i)) if ta else pl.BlockSpec((tm, tk), lambda i, j, q: (i, q))
    b_spec = pl.BlockSpec((tn, tk), lambda i, j, q: (j, q)) if tb else pl.BlockSpec((tk, tn), lambda i, j, q: (q, j))
    o_spec = pl.BlockSpec((tm, tn), lambda i, j, q: (i, j))
    in_specs, args = [a_spec, b_spec], [a, b]
    if has_add:
        in_specs.append(o_spec)
        args.append(add)
    return pl.pallas_call(
        body_one if nk == 1 else body_acc, name=name, grid=(m // tm, n // tn, nk), in_specs=in_specs, out_specs=o_spec,
        out_shape=jax.ShapeDtypeStruct((m, n), out_dtype), scratch_shapes=[] if nk == 1 else [pltpu.VMEM((tm, tn), F32)],
        compiler_params=_CP(dimension_semantics=("parallel", "parallel", "arbitrary")),
    )(*args)


class In:
    def __init__(self, arr, block, imap, kind="x", per_h=False, gdtype=F32, gshape=None, gimap=None, prefixed=False):
        self.arr, self.block, self.imap, self.kind, self.per_h, self.gdtype = arr, tuple(block), imap, kind, per_h, gdtype
        self.prefixed = prefixed
        self.gshape = tuple(gshape) if gshape is not None else tuple(arr.shape)
        self.gimap = gimap if gimap is not None else imap

    def spec(self, rev_g=None):
        imap = self.imap
        if rev_g is None:
            return pl.BlockSpec(self.block, lambda h, g: imap(h, g))
        return pl.BlockSpec(self.block, lambda h, g: imap(h, rev_g - 1 - g))


class Out:
    def __init__(self, shape, dtype, block, imap):
        self.shape, self.dtype, self.block, self.imap = tuple(shape), dtype, tuple(block), imap

    def spec(self, rev_g=None):
        imap = self.imap
        if rev_g is None:
            return pl.BlockSpec(self.block, lambda h, g: imap(h, g))
        return pl.BlockSpec(self.block, lambda h, g: imap(h, rev_g - 1 - g))


def _load_f32(ref, rows=None):
    v = ref[...] if rows is None else ref[0:rows]
    return v.astype(F32) if jnp.issubdtype(v.dtype, jnp.floating) else v


def _state_out(grid, shape):
    nd = len(shape)
    return Out(tuple(grid) + tuple(shape), F32, (None, None) + tuple(shape), lambda h, g: (h, g) + (0,) * nd)


def _carry(comm, grid, refs, n_in, n_out, n_scr):
    n_c = len(comm.ops) if comm is not None else 0
    n_s = len(comm.sem_shapes) if comm is not None else 0
    p = 0
    in_refs = refs[p:p + n_in]; p += n_in
    c_src = refs[p:p + n_c]; p += n_c
    out_refs = refs[p:p + n_out]; p += n_out
    c_dst = refs[p:p + n_c]; p += n_c
    scr = refs[p:p + n_scr]; p += n_scr
    c_sem = refs[p:p + n_s]
    first = jnp.logical_and(pl.program_id(0) == 0, pl.program_id(1) == 0)
    last = jnp.logical_and(pl.program_id(0) == grid[0] - 1, pl.program_id(1) == grid[1] - 1)
    return in_refs, out_refs, scr, (c_src, c_dst, c_sem), first, last


def _carry_specs(comm):
    if comm is None:
        return [], [], [], [], []
    any_spec = pl.BlockSpec(memory_space=pl.ANY)
    n = len(comm.ops)
    return [any_spec] * n, list(comm.ops), [any_spec] * n, list(comm.out_shapes), list(comm.sem_shapes)


def _op_fwd(name, f, grid, ins, outs, state_shapes=(), comm=None, prefix_rows=None):
    assert prefix_rows is None or not state_shapes
    n_in, n_out, n_st = len(ins), len(outs), len(state_shapes)
    st_outs = [_state_out(grid, s) for s in state_shapes]

    def body(*refs):
        in_refs, o_refs, st_scr, cargs, first, last = _carry(comm, grid, refs, n_in, n_out + n_st, n_st)
        out_refs, sv_refs = o_refs[:n_out], o_refs[n_out:]
        if comm is not None:
            @pl.when(first)
            def _():
                comm.start(*cargs)
        g = pl.program_id(1)
        if n_st:
            @pl.when(g == 0)
            def _():
                for s in st_scr:
                    s[...] = jnp.zeros(s.shape, F32)

        def compute(rows):
            vals = [_load_f32(r, rows if i.prefixed else None) for r, i in zip(in_refs, ins)]
            sts = [s[...] for s in st_scr]
            o, ns = f(g, vals, sts)
            for r, v in zip(out_refs, o):
                r[...] = v.astype(r.dtype)
            for r, s in zip(sv_refs, sts):
                r[...] = s
            for s, v in zip(st_scr, ns):
                s[...] = v

        if prefix_rows is None:
            compute(None)
        else:
            per = grid[1] // len(prefix_rows)
            for lv, rows in enumerate(prefix_rows):
                pl.when(g // per == lv)(functools.partial(compute, rows))
        if comm is not None:
            @pl.when(last)
            def _():
                comm.finish(*cargs)

    all_outs = list(outs) + st_outs
    c_in_specs, c_args, c_out_specs, c_out_shapes, c_sems = _carry_specs(comm)
    res = pl.pallas_call(
        body, name=name, grid=tuple(grid), in_specs=[i.spec() for i in ins] + c_in_specs,
        out_specs=[o.spec() for o in all_outs] + c_out_specs,
        out_shape=[jax.ShapeDtypeStruct(o.shape, o.dtype) for o in all_outs] + c_out_shapes,
        scratch_shapes=[pltpu.VMEM(tuple(s), F32) for s in state_shapes] + c_sems,
        compiler_params=_CP(dimension_semantics=("arbitrary", "arbitrary")),
    )(*[i.arr for i in ins], *c_args)
    n_all = n_out + n_st
    return list(res[:n_out]), list(res[n_out:n_all]), list(res[n_all:])


def _op_bwd(name, f, grid, ins, outs, state_shapes, saved, douts, addto=None, comm=None, prefix_rows=None):
    n_in, n_out, n_st = len(ins), len(outs), len(state_shapes)
    n_g = grid[1]
    assert prefix_rows is None or all(i.prefixed and i.per_h for i in ins if i.kind == "p")
    addto = addto or {}
    diff = [k for k, i in enumerate(ins) if i.kind in ("x", "p")]
    add_idx = sorted(addto)
    st_ins = [In(s, o.block, o.imap, "c") for s, o in zip(saved, [_state_out(grid, s) for s in state_shapes])]
    dout_ins = [In(d, o.block, o.imap, "c") for d, o in zip(douts, outs)]
    add_ins = []
    for k in add_idx:
        i, a = ins[k], addto[k]
        blk = i.block if i.kind == "x" else i.block[:-2] + a.shape[-2:]
        add_ins.append(In(a, blk, i.gimap if i.kind == "x" else i.imap, "c"))
    g_outs = []
    for k in diff:
        i = ins[k]
        g_outs.append(Out(i.gshape, i.gdtype if i.kind == "x" else F32, i.block, i.gimap))

    def body(*refs):
        all_in, go_refs, ds_scr, cargs, first_step, last_step = _carry(comm, grid, refs, n_in + n_st + n_out + len(add_idx), len(diff), n_st)
        if comm is not None:
            @pl.when(first_step)
            def _():
                comm.start(*cargs)
        p = 0
        in_refs = all_in[p:p + n_in]; p += n_in
        sv_refs = all_in[p:p + n_st]; p += n_st
        do_refs = all_in[p:p + n_out]; p += n_out
        ad_refs = all_in[p:p + len(add_idx)]
        hh = pl.program_id(0)
        step = pl.program_id(1)
        g = n_g - 1 - step
        if n_st:
            @pl.when(step == 0)
            def _():
                for s in ds_scr:
                    s[...] = jnp.zeros(s.shape, F32)

        def compute(rows):
            vals = [_load_f32(r, rows if i.prefixed else None) for r, i in zip(in_refs, ins)]
            sts = [r[...] for r in sv_refs]

            def fw(dvals, states):
                full = list(vals)
                for k, v in zip(diff, dvals):
                    full[k] = v
                o, ns = f(g, full, states)
                return list(o), list(ns)

            _, vjp = jax.vjp(fw, [vals[k] for k in diff], sts)
            cts = [r[...].astype(F32) for r in do_refs]
            dns = [s[...] for s in ds_scr]
            dvals, dsts = vjp((cts, dns))
            adds = dict(zip(add_idx, ad_refs))
            for k, r, dv in zip(diff, go_refs, dvals):
                i = ins[k]
                if i.kind == "x":
                    if k in adds:
                        dv = dv + adds[k][...].astype(F32)
                    r[...] = dv.astype(r.dtype)
                elif rows is not None:
                    r[0:rows] += dv
                else:
                    first = (step == 0) if i.per_h else jnp.logical_and(step == 0, hh == 0)

                    @pl.when(first)
                    def _(r=r, dv=dv, k=k):
                        r[...] = dv
                        if k in adds:
                            lead = adds[k].shape[0]
                            r[0:lead] += adds[k][...]

                    @pl.when(jnp.logical_not(first))
                    def _(r=r, dv=dv):
                        r[...] += dv
            for s, v in zip(ds_scr, dsts):
                s[...] = v

        if prefix_rows is None:
            compute(None)
        else:
            @pl.when(step == 0)
            def _():
                for k, r in zip(diff, go_refs):
                    if ins[k].kind == "p":
                        r[...] = jnp.zeros(r.shape, F32)
            per = n_g // len(prefix_rows)
            for lv, rows in enumerate(prefix_rows):
                pl.when(g // per == lv)(functools.partial(compute, rows))
        if comm is not None:
            @pl.when(last_step)
            def _():
                comm.finish(*cargs)

    all_ins = list(ins) + st_ins + dout_ins + add_ins
    c_in_specs, c_args, c_out_specs, c_out_shapes, c_sems = _carry_specs(comm)
    res = pl.pallas_call(
        body, name=name, grid=tuple(grid), in_specs=[i.spec(n_g) for i in all_ins] + c_in_specs,
        out_specs=[o.spec(n_g) for o in g_outs] + c_out_specs,
        out_shape=[jax.ShapeDtypeStruct(o.shape, o.dtype) for o in g_outs] + c_out_shapes,
        scratch_shapes=[pltpu.VMEM(tuple(s), F32) for s in state_shapes] + c_sems,
        compiler_params=_CP(dimension_semantics=("arbitrary", "arbitrary")),
    )(*[i.arr for i in all_ins], *c_args)
    return list(res[:len(g_outs)]), list(res[len(g_outs):])


class Op:
    def __init__(self, name, f, grid, ins, outs, state_shapes=(), prefix_rows=None):
        self.name, self.f, self.grid, self.ins, self.outs, self.state_shapes = name, f, grid, ins, outs, state_shapes
        self.prefix_rows = prefix_rows
        self.saved = None

    def fwd(self, comm=None):
        res, self.saved, self.fwd_comm_out = _op_fwd(self.name + "_fwd", self.f, self.grid, self.ins, self.outs, self.state_shapes, comm,
                                                     self.prefix_rows)
        return res

    def bwd(self, douts, addto=None, comm=None):
        res, self.bwd_comm_out = _op_bwd(self.name + "_bwd", self.f, self.grid, self.ins, self.outs, self.state_shapes, self.saved, douts,
                                         addto, comm, self.prefix_rows)
        return res


def _rows(arr, t, kind="x", gdtype=F32):
    return In(arr, (t, arr.shape[1]), lambda h, g: (g, 0), kind, gdtype=gdtype)


def _whole(arr, kind="p"):
    nd = arr.ndim
    return In(arr, arr.shape, lambda h, g: (0,) * nd, kind)


def _rows_out(s, n, t, dtype):
    return Out((s, n), dtype, (t, n), lambda h, g: (g, 0))


ROW_T = 256


def _rms_op(name, x, gain, out_dtype=BF16, gdtype=F32):
    s, n = x.shape

    def f(g, vals, sts):
        return [_rms(vals[0], vals[1])], []

    return Op(name, f, (1, s // ROW_T), [_rows(x, ROW_T, gdtype=gdtype), _whole(gain.reshape(1, n))], [_rows_out(s, n, ROW_T, out_dtype)])


def _mla_prep_op(qn, q1, q2, kn, kr, v, cos, sin):
    s = qn.shape[0]
    hd, half = MLA_HEADS, MLA_ROPE // 2

    def f(g, vals, sts):
        qn, q1, q2, kn, kr, v, cos, sin = vals
        cos_h, sin_h = jnp.tile(cos, (1, hd)), jnp.tile(sin, (1, hd))
        r1 = q1 * cos_h - q2 * sin_h
        r2 = q2 * cos_h + q1 * sin_h
        k1, k2 = kr[:, 0:half], kr[:, half:2 * half]
        kr1 = k1 * cos - k2 * sin
        kr2 = k2 * cos + k1 * sin
        zpad = jnp.zeros((qn.shape[0], LANE - MLA_QK), F32)
        qs, ks, vs = [], [], []
        for h in range(hd):
            a, b = h * MLA_NOPE, (h + 1) * MLA_NOPE
            c, d = h * half, (h + 1) * half
            qs.append(jnp.concatenate([qn[:, a:b], r1[:, c:d], r2[:, c:d], zpad], axis=1))
            ks.append(jnp.concatenate([kn[:, a:b], kr1, kr2, zpad], axis=1))
            vs.append(v[:, a:b])
        return [jnp.stack(qs, 0), jnp.stack(ks, 0), jnp.stack(vs, 0)], []

    ins = [_rows(qn, ROW_T, gdtype=BF16), _rows(q1, ROW_T, gdtype=BF16), _rows(q2, ROW_T, gdtype=BF16), _rows(kn, ROW_T, gdtype=BF16),
           _rows(kr, ROW_T, gdtype=BF16), _rows(v, ROW_T, gdtype=BF16), _rows(cos, ROW_T, "c"), _rows(sin, ROW_T, "c")]
    outs = [Out((hd, s, LANE), BF16, (hd, ROW_T, LANE), lambda h, g: (0, g, 0)),
            Out((hd, s, LANE), BF16, (hd, ROW_T, LANE), lambda h, g: (0, g, 0)),
            Out((hd, s, MLA_V), BF16, (hd, ROW_T, MLA_V), lambda h, g: (0, g, 0))]
    return Op("mla_prep", f, (1, s // ROW_T), ins, outs)


ATT_TQ = 256
ATT_LEVELS = 4


def _mla_attn_op(q, k, v):
    hd, s, _ = q.shape
    scale = MLA_QK ** -0.5

    def f(g, vals, sts):
        q, k, v = vals
        sc = _bdot_nt(q, k) * scale
        r = lax.broadcasted_iota(jnp.int32, sc.shape, 0) + g * ATT_TQ
        c = lax.broadcasted_iota(jnp.int32, sc.shape, 1)
        sc = jnp.where(r >= c, sc, -1e30)
        m = lax.stop_gradient(jnp.max(sc, axis=-1, keepdims=True))
        p = jnp.exp(sc - m)
        p = p / jnp.sum(p, axis=-1, keepdims=True)
        return [_bdot(p, v)], []

    ins = [In(q, (None, ATT_TQ, LANE), lambda h, g: (h, g, 0), "x", gdtype=BF16),
           In(k, (None, s, LANE), lambda h, g: (h, 0, 0), "p", per_h=True, prefixed=True),
           In(v, (None, s, MLA_V), lambda h, g: (h, 0, 0), "p", per_h=True, prefixed=True)]
    outs = [Out((hd, s, MLA_V), F32, (None, ATT_TQ, MLA_V), lambda h, g: (h, g, 0))]
    return Op("mla_attn", f, (hd, s // ATT_TQ), ins, outs, prefix_rows=[(lv + 1) * (s // ATT_LEVELS) for lv in range(ATT_LEVELS)])


def _mla_post_op(o, gate):
    hd, s, _ = o.shape

    def f(g, vals, sts):
        o, gate = vals
        cat = jnp.concatenate([o[h] for h in range(hd)], axis=1)
        return [cat * _silu(gate)], []

    ins = [In(o, (hd, ROW_T, MLA_V), lambda h, g: (0, g, 0), "x"), _rows(gate, ROW_T, gdtype=BF16)]
    return Op("mla_post", f, (1, s // ROW_T), ins, [_rows_out(s, hd * MLA_V, ROW_T, BF16)])


def _gla_gate_op(gk, w2, b):
    s = gk.shape[0]

    def f(g, vals, sts):
        gk, w2, b = vals
        return [jax.nn.log_sigmoid(_bdot(gk, w2) + b) / GLA_TAU], []

    ins = [_rows(gk, ROW_T, gdtype=BF16), _whole(w2), _whole(b)]
    return Op("gla_gate", f, (1, s // ROW_T), ins, [_rows_out(s, GLA_HEADS * GLA_DK, ROW_T, F32)])


def _gla_core_op(q, k, v, gate, la, g_o):
    s = q.shape[0]
    c, nh = GLA_CHUNK, GLA_HEADS

    def f(g, vals, sts):
        q, k, v, gate, la, g_o = vals
        tri = _tri(c)
        b = _cumsum_rows(la)
        b_last = jnp.sum(la, axis=0, keepdims=True)
        qt = q * (GLA_DK ** -0.5) * jnp.exp(b)
        kt = k * jnp.exp(-b)
        kd = k * jnp.exp(b_last - b)
        ys, new_sts = [], []
        for h in range(nh):
            ks, vs = slice(h * GLA_DK, (h + 1) * GLA_DK), slice(h * GLA_DV, (h + 1) * GLA_DV)
            att = jnp.where(tri, _bdot_nt(qt[:, ks], kt[:, ks]), 0.0)
            o = _bdot(att, v[:, vs]) + _bdot_nt(qt[:, ks], sts[h])
            new_sts.append(jnp.exp(b_last[:, ks]) * sts[h] + _bdot_tn(v[:, vs], kd[:, ks]))
            ys.append(_rms(o, g_o) * _silu(gate[:, vs]))
        return [jnp.concatenate(ys, axis=1)], new_sts

    ins = [_rows(q, c, gdtype=BF16), _rows(k, c, gdtype=BF16), _rows(v, c, gdtype=BF16), _rows(gate, c, gdtype=BF16), _rows(la, c), _whole(g_o)]
    outs = [_rows_out(s, nh * GLA_DV, c, BF16)]
    return Op("gla_core", f, (1, s // c), ins, outs, [(GLA_DV, GLA_DK)] * nh)


LRU_T = 256


def _lru_op(gate, u, conv_w, conv_b, w_a, b_a, w_x, b_x, lam):
    s, w = u.shape
    t = LRU_T

    def f(g, vals, sts):
        gate, u, cw, cb, w_a, b_a, w_x, b_x, lam = vals
        u_prev, h_prev = sts
        uc = cb
        for kk in range(CONV_W):
            uc = uc + cw[kk] * _shift_rows(u, u_prev, CONV_W - 1 - kk)
        ra, ri = [], []
        for n in range(LRU_BLOCKS):
            blk = uc[:, n * LRU_BLOCK:(n + 1) * LRU_BLOCK]
            ra.append(_bdot(blk, w_a[n]))
            ri.append(_bdot(blk, w_x[n]))
        r = jax.nn.sigmoid(jnp.concatenate(ra, axis=1) + b_a)
        i = jax.nn.sigmoid(jnp.concatenate(ri, axis=1) + b_x)
        log_a = -LRU_C * r * jax.nn.softplus(-lam)
        a = jnp.exp(log_a)
        bb = jnp.sqrt(_one_minus_exp(2.0 * log_a)) * (i * uc)
        zero = jnp.zeros_like(a)
        sh = 1
        while sh < t:
            a_s = _shift_rows(a - 1.0, zero, sh) + 1.0
            b_s = _shift_rows(bb, zero, sh)
            bb = a * b_s + bb
            a = a * a_s
            sh *= 2
        hs = bb + a * h_prev
        last = (lax.broadcasted_iota(jnp.int32, hs.shape, 0) == t - 1).astype(F32)
        h_last = jnp.sum(hs * last, axis=0, keepdims=True)
        return [hs * _silu(gate)], [u, h_last]

    ins = [_rows(gate, t, gdtype=BF16), _rows(u, t, gdtype=BF16), _whole(conv_w), _whole(conv_b), _whole(w_a), _whole(b_a), _whole(w_x),
           _whole(b_x), _whole(lam)]
    return Op("lru_core", f, (1, s // t), ins, [_rows_out(s, w, t, BF16)], [(t, w), (1, w)])


def _ssd_conv_op(xbc, conv_w, conv_b):
    s, w = xbc.shape
    t = ROW_T
    n_x, n_b = SSD_INNER, SSD_GROUPS * SSD_STATE

    def f(g, vals, sts):
        xbc, cw, cb = vals
        acc = cb
        for kk in range(CONV_W):
            acc = acc + cw[kk] * _shift_rows(xbc, sts[0], CONV_W - 1 - kk)
        y = _silu(acc)
        return [y[:, :n_x], y[:, n_x:n_x + n_b], y[:, n_x + n_b:]], [xbc]

    ins = [_rows(xbc, t, gdtype=BF16), _whole(conv_w), _whole(conv_b)]
    outs = [_rows_out(s, n_x, t, F32), _rows_out(s, n_b, t, F32), _rows_out(s, n_b, t, F32)]
    return Op("ssd_conv", f, (1, s // t), ins, outs, [(t, w)])


SSD_L = 512


def _ssd_core_op(x, bm, cm, z, dt, dt_bias, a_log, d_skip, g_norm):
    s = x.shape[0]
    c, hg, p = SSD_L, SSD_HPG, SSD_P
    gw = hg * p

    def f(g, vals, sts):
        x, bm, cm, z, dtr, dt_bias, a_log, d_skip, g_norm = vals
        tri = _tri(c)
        dt = jax.nn.softplus(dtr + dt_bias)
        da = dt * (-jnp.exp(a_log))
        cs = _cumsum_rows(da)
        cs_last = jnp.sum(da, axis=0, keepdims=True)
        cs_t = jnp.transpose(jnp.concatenate([cs, jnp.zeros((c, LANE - hg), F32)], axis=1))
        cb = _bdot_nt(cm, bm)
        ys, new_st = [], []
        for h in range(hg):
            cs_h = cs[:, h:h + 1]
            cs_row = cs_t[h:h + 1, :]
            seg = jnp.where(tri, cs_h - cs_row, 0.0)
            lmat = jnp.where(tri, jnp.exp(seg), 0.0)
            x_h = x[:, h * p:(h + 1) * p]
            xdt = x_h * dt[:, h:h + 1]
            y_diag = _bdot(cb * lmat, xdt)
            decay = jnp.exp(cs_last[:, h:h + 1] - cs_h)
            states = _bdot_tn(xdt * decay, bm)
            y_off = _bdot_nt(cm, sts[h]) * jnp.exp(cs_h)
            new_st.append(jnp.exp(cs_last[:, h:h + 1]) * sts[h] + states)
            ys.append(y_diag + y_off + d_skip[:, h:h + 1] * x_h)
        y = jnp.concatenate(ys, axis=1) * _silu(z)
        return [_rms(y, g_norm)], new_st

    ins = [In(x, (c, gw), lambda h, g: (g, h), "x"), In(bm, (c, SSD_STATE), lambda h, g: (g, h), "x"),
           In(cm, (c, SSD_STATE), lambda h, g: (g, h), "x"), In(z, (c, gw), lambda h, g: (g, h), "x", gdtype=BF16),
           In(dt, (None, c, hg), lambda h, g: (h, g, 0), "x"),
           In(dt_bias, (None, 1, hg), lambda h, g: (h, 0, 0), "p", per_h=True),
           In(a_log, (None, 1, hg), lambda h, g: (h, 0, 0), "p", per_h=True),
           In(d_skip, (None, 1, hg), lambda h, g: (h, 0, 0), "p", per_h=True),
           In(g_norm, (1, gw), lambda h, g: (0, h), "p", per_h=True)]
    outs = [Out((s, SSD_INNER), BF16, (c, gw), lambda h, g: (g, h))]
    return Op("ssd_core", f, (SSD_GROUPS, s // c), ins, outs, [(p, SSD_STATE)] * hg)


def _loss_op(h, target, final_g):
    s, n = h.shape
    t = ROW_T
    n_g = s // t

    def body(h_ref, t_ref, g_ref, loss_ref, dh_ref, dg_ref):
        step = pl.program_id(0)

        def lossf(hv, gv):
            err = _rms(hv, gv) - t_ref[...]
            return 0.5 * jnp.sum(jnp.mean(err * err, axis=-1))

        l, (dh, dg) = jax.value_and_grad(lossf, argnums=(0, 1))(h_ref[...], g_ref[...])
        dh_ref[...] = dh

        @pl.when(step == 0)
        def _():
            loss_ref[...] = jnp.zeros(loss_ref.shape, F32)
            dg_ref[...] = jnp.zeros(dg_ref.shape, F32)

        loss_ref[...] += jnp.full(loss_ref.shape, l, F32)
        dg_ref[...] += dg

    row = pl.BlockSpec((t, n), lambda g: (g, 0))
    one = pl.BlockSpec((1, n), lambda g: (0, 0))
    return pl.pallas_call(
        body, name="loss_head", grid=(n_g,), in_specs=[row, row, one],
        out_specs=[pl.BlockSpec((1, LANE), lambda g: (0, 0)), row, one],
        out_shape=[jax.ShapeDtypeStruct((1, LANE), F32), jax.ShapeDtypeStruct((s, n), F32), jax.ShapeDtypeStruct((1, n), F32)],
        compiler_params=_CP(dimension_semantics=("arbitrary",)),
    )(h, target, final_g.reshape(1, n))


def _pad_cols(w, n):
    return jnp.pad(w, ((0, 0), (0, n - w.shape[1])))


def _pad_rows(w, n):
    return jnp.pad(w, ((0, n - w.shape[0]), (0, 0)))


def _proj_bwd(tag, u, dps, ws):
    du = None
    for i, (dp, w) in enumerate(zip(dps, ws)):
        du = _mm(f"{tag}_du{i}", dp, w, tb=True, add=du)
    dws = [_mm(f"{tag}_dw{i}", u, dp, ta=True, out_dtype=BF16) for i, dp in enumerate(dps)]
    return du, dws


def _mla_layer(h, norm_g, w, cos, sin, fwd_comm=None, late_w_out=None):
    bf = lambda a: a.astype(BF16)
    w_in, w_uq, w_ukv = w["mla_w_in"], w["mla_w_uq"], w["mla_w_ukv"]
    a0, a1, a2 = MLA_Q_RANK, MLA_Q_RANK + MLA_KV_RANK, MLA_Q_RANK + MLA_KV_RANK + MLA_ROPE
    w_cq, w_ckv, w_kr, w_g = bf(w_in[:, :a0]), bf(w_in[:, a0:a1]), bf(_pad_cols(w_in[:, a1:a2], LANE)), bf(w_in[:, a2:])
    uq = w_uq.reshape(MLA_Q_RANK, MLA_HEADS, MLA_QK)
    half = MLA_ROPE // 2
    w_qn = bf(uq[:, :, :MLA_NOPE].reshape(MLA_Q_RANK, -1))
    w_q1 = bf(uq[:, :, MLA_NOPE:MLA_NOPE + half].reshape(MLA_Q_RANK, -1))
    w_q2 = bf(uq[:, :, MLA_NOPE + half:].reshape(MLA_Q_RANK, -1))
    ukv = w_ukv.reshape(MLA_KV_RANK, MLA_HEADS, MLA_NOPE + MLA_V)
    w_kn = bf(ukv[:, :, :MLA_NOPE].reshape(MLA_KV_RANK, -1))
    w_v = bf(ukv[:, :, MLA_NOPE:].reshape(MLA_KV_RANK, -1))

    n0 = _rms_op("mla_norm", h, norm_g)
    u, = n0.fwd()
    cq, ckv, kr, gate = (_mm(f"mla_in{i}", u, wi) for i, wi in enumerate((w_cq, w_ckv, w_kr, w_g)))
    nq = _rms_op("mla_qnorm", cq, w["mla_g_q"], gdtype=BF16)
    nkv = _rms_op("mla_kvnorm", ckv, w["mla_g_kv"], gdtype=BF16)
    qn_, = nq.fwd()
    kvn_, = nkv.fwd()
    qn, q1, q2 = (_mm(f"mla_uq{i}", qn_, wi) for i, wi in enumerate((w_qn, w_q1, w_q2)))
    kn, v = (_mm(f"mla_ukv{i}", kvn_, wi) for i, wi in enumerate((w_kn, w_v)))
    prep = _mla_prep_op(qn, q1, q2, kn, kr, v, cos, sin)
    qh, kh, vh = prep.fwd()
    attn = _mla_attn_op(qh, kh, vh)
    o, = attn.fwd(fwd_comm)
    w_out = bf(w["mla_w_out"]) if late_w_out is None else late_w_out(attn.fwd_comm_out)
    post = _mla_post_op(o, gate)
    y, = post.fwd()
    h_out = _mm("mla_out", y, w_out, add=h)

    def bwd(dh, make_comm=None):
        dy = _mm("mla_out_dy", dh, w_out, tb=True, out_dtype=BF16)
        d_w_out = _mm("mla_out_dw", y, dh, ta=True, out_dtype=BF16)
        do, dgate = post.bwd([dy])
        dqh, dkh, dvh = attn.bwd([do], comm=None if make_comm is None else make_comm(d_w_out))
        dqn, dq1, dq2, dkn, dkr, dv = prep.bwd([dqh, dkh, dvh])
        dqn_, d_uq = _proj_bwd("mla_uq", qn_, (dqn, dq1, dq2), (w_qn, w_q1, w_q2))
        dkvn_, d_ukv = _proj_bwd("mla_ukv", kvn_, (dkn, dv), (w_kn, w_v))
        dcq, d_g_q = nq.bwd([dqn_])
        dckv, d_g_kv = nkv.bwd([dkvn_])
        du, d_in = _proj_bwd("mla_in", u, (dcq, dckv, dkr, dgate), (w_cq, w_ckv, w_kr, w_g))
        dh_in, d_norm = n0.bwd([du], addto={0: dh})
        shp = (MLA_Q_RANK, MLA_HEADS, -1)
        g_uq = jnp.concatenate([d_uq[0].reshape(shp), d_uq[1].reshape(shp), d_uq[2].reshape(shp)], axis=2).reshape(MLA_Q_RANK, -1)
        shp = (MLA_KV_RANK, MLA_HEADS, -1)
        g_ukv = jnp.concatenate([d_ukv[0].reshape(shp), d_ukv[1].reshape(shp)], axis=2).reshape(MLA_KV_RANK, -1)
        g_in = jnp.concatenate([d_in[0], d_in[1], d_in[2][:, :MLA_ROPE], d_in[3]], axis=1)
        return dh_in, d_norm, {"mla_w_in": g_in, "mla_g_q": d_g_q.reshape(-1), "mla_w_uq": g_uq, "mla_g_kv": d_g_kv.reshape(-1),
                               "mla_w_ukv": g_ukv, "mla_w_out": d_w_out}, attn.bwd_comm_out

    return h_out, bwd, attn.fwd_comm_out


def _gla_layer(h, norm_g, w, fwd_comm=None):
    bf = lambda a: a.astype(BF16)
    w_in = w["gla_w_in"]
    nk, nv = GLA_HEADS * GLA_DK, GLA_HEADS * GLA_DV
    cuts = (0, nk, 2 * nk, 2 * nk + nv, 2 * nk + 2 * nv)
    w_q, w_k, w_v, w_g = (bf(w_in[:, cuts[i]:cuts[i + 1]]) for i in range(4))
    w_gk = bf(_pad_cols(w_in[:, cuts[4]:], LANE))
    w2 = _pad_rows(w["gla_w_gk2"], LANE)
    b_gk = w["gla_b_gk"].reshape(1, -1)
    g_o = w["gla_g_o"].reshape(1, -1)
    w_out = bf(w["gla_w_out"])

    n0 = _rms_op("gla_norm", h, norm_g)
    u, = n0.fwd()
    q, k, v, gate, gk = (_mm(f"gla_in{i}", u, wi) for i, wi in enumerate((w_q, w_k, w_v, w_g, w_gk)))
    gop = _gla_gate_op(gk, w2, b_gk)
    la, = gop.fwd()
    core = _gla_core_op(q, k, v, gate, la, g_o)
    y, = core.fwd(fwd_comm)
    h_out = _mm("gla_out", y, w_out, add=h)

    def bwd(dh, make_comm=None):
        dy = _mm("gla_out_dy", dh, w_out, tb=True, out_dtype=BF16)
        d_w_out = _mm("gla_out_dw", y, dh, ta=True, out_dtype=BF16)
        dq, dk, dv, dgate, dla, d_g_o = core.bwd([dy], comm=None if make_comm is None else make_comm(d_w_out))
        dgk, d_w2, d_b = gop.bwd([dla])
        du, d_in = _proj_bwd("gla_in", u, (dq, dk, dv, dgate, dgk), (w_q, w_k, w_v, w_g, w_gk))
        dh_in, d_norm = n0.bwd([du], addto={0: dh})
        g_in = jnp.concatenate([d_in[0], d_in[1], d_in[2], d_in[3], d_in[4][:, :GLA_RANK]], axis=1)
        return dh_in, d_norm, {"gla_w_in": g_in, "gla_w_gk2": d_w2[:GLA_RANK], "gla_b_gk": d_b.reshape(-1), "gla_g_o": d_g_o.reshape(-1),
                               "gla_w_out": d_w_out}, core.bwd_comm_out

    return h_out, bwd, core.fwd_comm_out


def _lru_layer(h, norm_g, w, fwd_comm=None):
    bf = lambda a: a.astype(BF16)
    w_in = w["lru_w_in"]
    w_g, w_u = bf(w_in[:, :LRU_WIDTH]), bf(w_in[:, LRU_WIDTH:])
    row = lambda a: a.reshape(1, -1)
    w_out = bf(w["lru_w_out"])

    n0 = _rms_op("lru_norm", h, norm_g)
    u_, = n0.fwd()
    gate, u = (_mm(f"lru_in{i}", u_, wi) for i, wi in enumerate((w_g, w_u)))
    core = _lru_op(gate, u, w["lru_conv_w"].reshape(CONV_W, 1, -1), row(w["lru_conv_b"]), w["lru_w_a"], row(w["lru_b_a"]), w["lru_w_x"],
                   row(w["lru_b_x"]), row(w["lru_lam"]))
    y, = core.fwd(fwd_comm)
    h_out = _mm("lru_out", y, w_out, add=h)

    def bwd(dh, make_comm=None):
        dy = _mm("lru_out_dy", dh, w_out, tb=True, out_dtype=BF16)
        d_w_out = _mm("lru_out_dw", y, dh, ta=True, out_dtype=BF16)
        dgate, du, d_cw, d_cb, d_wa, d_ba, d_wx, d_bx, d_lam = core.bwd([dy], comm=None if make_comm is None else make_comm(d_w_out))
        du_, d_in = _proj_bwd("lru_in", u_, (dgate, du), (w_g, w_u))
        dh_in, d_norm = n0.bwd([du_], addto={0: dh})
        return dh_in, d_norm, {"lru_w_in": jnp.concatenate(d_in, axis=1), "lru_conv_w": d_cw.reshape(CONV_W, -1), "lru_conv_b": d_cb.reshape(-1),
                               "lru_w_a": d_wa, "lru_b_a": d_ba.reshape(-1), "lru_w_x": d_wx, "lru_b_x": d_bx.reshape(-1),
                               "lru_lam": d_lam.reshape(-1), "lru_w_out": d_w_out}, core.bwd_comm_out

    return h_out, bwd, core.fwd_comm_out


def _ssd_layer(h, norm_g, w, fwd_comm=None, late_w_out=None):
    bf = lambda a: a.astype(BF16)
    s = h.shape[0]
    w_in = w["ssd_w_in"]
    conv_dim = SSD_INNER + 2 * SSD_GROUPS * SSD_STATE
    w_z, w_xbc = bf(w_in[:, :SSD_INNER]), bf(w_in[:, SSD_INNER:SSD_INNER + conv_dim])
    w_dt = bf(_pad_cols(w_in[:, SSD_INNER + conv_dim:], LANE))
    grp = lambda a: a.reshape(SSD_GROUPS, 1, SSD_HPG)

    n0 = _rms_op("ssd_norm", h, norm_g)
    u, = n0.fwd()
    z, xbc, dtp = (_mm(f"ssd_in{i}", u, wi) for i, wi in enumerate((w_z, w_xbc, w_dt)))
    conv = _ssd_conv_op(xbc, w["ssd_conv_w"].reshape(CONV_W, 1, -1), w["ssd_conv_b"].reshape(1, -1))
    x, bm, cm = conv.fwd()
    dt = dtp[:, :SSD_HEADS].reshape(s, SSD_GROUPS, SSD_HPG).transpose(1, 0, 2)
    core = _ssd_core_op(x, bm, cm, z, dt, grp(w["ssd_dt_bias"]), grp(w["ssd_a_log"]), grp(w["ssd_d"]), w["ssd_g_norm"].reshape(1, -1))
    y, = core.fwd(fwd_comm)
    w_out = bf(w["ssd_w_out"]) if late_w_out is None else late_w_out(core.fwd_comm_out)
    h_out = _mm("ssd_out", y, w_out, add=h)

    def bwd(dh, make_comm=None):
        dy = _mm("ssd_out_dy", dh, w_out, tb=True, out_dtype=BF16)
        d_w_out = _mm("ssd_out_dw", y, dh, ta=True, out_dtype=BF16)
        dx, dbm, dcm, dz, ddt, d_dtb, d_alog, d_d, d_gn = core.bwd([dy], comm=None if make_comm is None else make_comm(d_w_out))
        dxbc, d_cw, d_cb = conv.bwd([dx, dbm, dcm])
        ddtp = _pad_cols(ddt.transpose(1, 0, 2).reshape(s, SSD_HEADS), LANE).astype(BF16)
        du, d_in = _proj_bwd("ssd_in", u, (dz, dxbc, ddtp), (w_z, w_xbc, w_dt))
        dh_in, d_norm = n0.bwd([du], addto={0: dh})
        g_in = jnp.concatenate([d_in[0], d_in[1], d_in[2][:, :SSD_HEADS]], axis=1)
        return dh_in, d_norm, {"ssd_w_in": g_in, "ssd_conv_w": d_cw.reshape(CONV_W, -1), "ssd_conv_b": d_cb.reshape(-1),
                               "ssd_dt_bias": d_dtb.reshape(-1), "ssd_a_log": d_alog.reshape(-1), "ssd_d": d_d.reshape(-1),
                               "ssd_g_norm": d_gn.reshape(-1), "ssd_w_out": d_w_out}, core.bwd_comm_out

    return h_out, bwd


def _rope_tables(positions):
    inv_freq = ROPE_THETA ** (-jnp.arange(0, MLA_ROPE, 2, dtype=F32) / MLA_ROPE)
    ang = positions.astype(F32)[:, None] * inv_freq
    return jnp.cos(ang), jnp.sin(ang)


WEIGHTS = ["norm_g", "final_g", "mla_w_in", "mla_g_q", "mla_w_uq", "mla_g_kv", "mla_w_ukv", "mla_w_out", "gla_w_in", "gla_w_gk2", "gla_b_gk",
           "gla_g_o", "gla_w_out", "lru_w_in", "lru_conv_w", "lru_conv_b", "lru_w_a", "lru_b_a", "lru_w_x", "lru_b_x", "lru_lam", "lru_w_out",
           "ssd_w_in", "ssd_conv_w", "ssd_conv_b", "ssd_dt_bias", "ssd_a_log", "ssd_d", "ssd_g_norm", "ssd_w_out"]
BIG = ["mla_w_in", "mla_w_uq", "mla_w_ukv", "mla_w_out", "gla_w_in", "gla_w_out", "lru_w_in", "lru_w_out", "ssd_w_in", "ssd_w_out"]
SMALL = ["gla_w_gk2", "gla_b_gk", "gla_g_o", "lru_conv_w", "lru_conv_b", "lru_b_a", "lru_b_x", "lru_lam", "ssd_conv_w", "ssd_conv_b", "ssd_g_norm"]
REPL = ["norm_g", "final_g", "mla_g_q", "mla_g_kv", "lru_w_a", "lru_w_x", "ssd_dt_bias", "ssd_a_log", "ssd_d"]
REPL_EARLY = ["lru_w_a", "lru_w_x"]
REPL_LATE = [n for n in REPL if n not in REPL_EARLY]
N_CHIPS, N_DEV = 4, 8
PACK_W = 1024
ADAM_ROWS = 256
SMALL_ROWS = 64


def _shard_axis(name):
    return 0 if name.endswith("_w_out") else -1


def _pack(arrs, dtype, row_mult):
    flat = jnp.concatenate([a.reshape(-1).astype(dtype) for a in arrs])
    per = PACK_W * row_mult
    total = -(-flat.shape[0] // per) * per
    return jnp.pad(flat, (0, total - flat.shape[0])).reshape(-1, PACK_W)


def _unpack(buf, shapes):
    flat = buf.reshape(-1)
    out, off = [], 0
    for s in shapes:
        n = math.prod(s)
        out.append(flat[off:off + n].reshape(s))
        off += n
    return out


def _mesh_pos():
    return lax.axis_index("x"), lax.axis_index("y"), lax.axis_index("c")


class GatherComm:
    def __init__(self, ops):
        self.ops = list(ops)
        n = len(self.ops)
        assert all(o.ndim == 2 and o.shape[0] % 32 == 0 for o in self.ops), [o.shape for o in self.ops]
        self.out_shapes = [jax.ShapeDtypeStruct((N_CHIPS,) + o.shape, o.dtype) for o in self.ops]
        self.sem_shapes = [pltpu.SemaphoreType.DMA((6 * n,)), pltpu.SemaphoreType.DMA((6 * n,)), pltpu.SemaphoreType.DMA((n,))]

    def _copies(self, srcs, dsts, sems):
        send_sems, recv_sems, local_sems = sems
        n = len(self.ops)
        x, y, c = _mesh_pos()
        me_id, sibling = (x, y, c), (x, y, 1 - c)
        chips = [(1 - x, y), (x, 1 - y), (1 - x, 1 - y)]
        mine = 2 * x + y

        def half(i, cc):
            h = self.ops[i].shape[0] // 2
            return pl.ds(cc * h, h)

        def copy(i, k, src, slot, cc, to):
            return pltpu.make_async_remote_copy(src_ref=src, dst_ref=dsts[i].at[slot, half(i, cc)], send_sem=send_sems.at[i * 6 + k],
                                                recv_sem=recv_sems.at[i * 6 + k], device_id=to, device_id_type=pl.DeviceIdType.MESH)

        local = [pltpu.make_async_copy(srcs[i], dsts[i].at[mine], local_sems.at[i]) for i in range(n)]
        first, ici_recvs, passed, sib_recvs = [], [], [], []
        for i in range(n):
            my_half = srcs[i].at[half(i, c)]
            for k, (px, py) in enumerate(chips):
                slot = 2 * px + py
                first.append(copy(i, k, my_half, mine, c, (px, py, c)))
                ici_recvs.append(copy(i, k, my_half, slot, c, me_id))
                passed.append(copy(i, 3 + k, dsts[i].at[slot, half(i, c)], slot, c, sibling))
                sib_recvs.append(copy(i, 3 + k, my_half, slot, 1 - c, me_id))
        return local, first, ici_recvs, passed, sib_recvs

    def start(self, srcs, dsts, sems):
        local, first, _, _, _ = self._copies(srcs, dsts, sems)
        for cp in local + first:
            cp.start()

    def finish(self, srcs, dsts, sems):
        local, first, ici_recvs, passed, sib_recvs = self._copies(srcs, dsts, sems)
        for rc, fw in zip(ici_recvs, passed):
            rc.wait_recv()
            fw.start()
        for cp in sib_recvs:
            cp.wait_recv()
        for cp in first + passed:
            cp.wait_send()
        for cp in local:
            cp.wait()


class ExchangeComm:
    def __init__(self, chip_ops, all_ops=()):
        self.ops = list(chip_ops) + list(all_ops)
        self.per_chip = (True,) * len(chip_ops) + (False,) * len(all_ops)
        n = len(self.ops)
        self.out_shapes = [jax.ShapeDtypeStruct((N_DEV,) + o.shape[-2:], o.dtype) for o in self.ops]
        self.sem_shapes = [pltpu.SemaphoreType.DMA((7 * n,)), pltpu.SemaphoreType.DMA((7 * n,)), pltpu.SemaphoreType.DMA((n,))]

    def _copies(self, srcs, dsts, sems):
        send_sems, recv_sems, local_sems = sems
        n, per_chip = len(self.ops), self.per_chip
        x, y, c = _mesh_pos()
        me_id, sibling = (x, y, c), (x, y, 1 - c)
        chips = [(1 - x, y), (x, 1 - y), (1 - x, 1 - y)]

        def dev(px, py, pc):
            return 4 * px + 2 * py + pc

        def part(i, px, py):
            return srcs[i].at[2 * px + py] if per_chip[i] else srcs[i]

        def copy(i, k, src, slot, to):
            return pltpu.make_async_remote_copy(src_ref=src, dst_ref=dsts[i].at[slot], send_sem=send_sems.at[i * 7 + k],
                                                recv_sem=recv_sems.at[i * 7 + k], device_id=to, device_id_type=pl.DeviceIdType.MESH)

        me = dev(x, y, c)
        local = [pltpu.make_async_copy(part(i, x, y), dsts[i].at[me], local_sems.at[i]) for i in range(n)]
        first, ici_recvs, passed, sib_recvs = [], [], [], []
        for i in range(n):
            first.append(copy(i, 0, part(i, x, y), me, sibling))
            first += [copy(i, 1 + k, part(i, px, py), me, (px, py, c)) for k, (px, py) in enumerate(chips)]
            sib_recvs.append(copy(i, 0, part(i, x, y), dev(x, y, 1 - c), me_id))
            for k, (px, py) in enumerate(chips):
                slot = dev(px, py, c)
                ici_recvs.append(copy(i, 1 + k, part(i, x, y), slot, me_id))
                passed.append(copy(i, 4 + k, dsts[i].at[slot], slot, sibling))
                sib_recvs.append(copy(i, 4 + k, part(i, x, y), dev(px, py, 1 - c), me_id))
        return local, first, ici_recvs, passed, sib_recvs

    def start(self, srcs, dsts, sems):
        local, first, _, _, _ = self._copies(srcs, dsts, sems)
        for cp in local + first:
            cp.start()

    def finish(self, srcs, dsts, sems):
        local, first, ici_recvs, passed, sib_recvs = self._copies(srcs, dsts, sems)
        for rc, fw in zip(ici_recvs, passed):
            rc.wait_recv()
            fw.start()
        for cp in sib_recvs:
            cp.wait_recv()
        for cp in first + passed:
            cp.wait_send()
        for cp in local:
            cp.wait()


def _run_comm(name, comm):
    n = len(comm.ops)

    def body(*refs):
        srcs, dsts, sems = refs[:n], refs[n:2 * n], refs[2 * n:]
        comm.start(srcs, dsts, sems)
        comm.finish(srcs, dsts, sems)

    any_spec = pl.BlockSpec(memory_space=pl.ANY)
    return pl.pallas_call(body, name=name, in_specs=[any_spec] * n, out_specs=[any_spec] * n, out_shape=comm.out_shapes,
                          scratch_shapes=comm.sem_shapes)(*comm.ops)


def _adamw(name, parts, w, m, v, lead=False, comm=None):
    plist = list(parts) if isinstance(parts, (list, tuple)) else [parts]
    n_p = len(plist)
    rows, cols = w.shape[-2:]
    t = next(c for c in (ADAM_ROWS, ADAM_ROWS // 2, SMALL_ROWS) if all(p.shape[1] % c == 0 for p in plist))
    starts = [sum(p.shape[1] for p in plist[:k]) // t for k in range(n_p)]
    counts = [p.shape[1] // t for p in plist]
    assert sum(p.shape[1] for p in plist) == rows, (name, rows)
    c1 = 1.0 - ADAM_B1 ** ADAM_STEP
    c2 = 1.0 - ADAM_B2 ** ADAM_STEP

    n_c = len(comm.ops) if comm is not None else 0
    n_steps = rows // t

    def body(*refs):
        p_refs = refs[:n_p]
        w_ref, m_ref, v_ref = refs[n_p:n_p + 3]
        c_src = refs[n_p + 3:n_p + 3 + n_c]
        g_ref, d_ref, nm_ref, nv_ref = refs[n_p + 3 + n_c:n_p + 7 + n_c]
        cargs = (c_src, refs[n_p + 7 + n_c:n_p + 7 + 2 * n_c], refs[n_p + 7 + 2 * n_c:])
        if comm is not None:
            @pl.when(pl.program_id(0) == 0)
            def _():
                comm.start(*cargs)
        g = None
        for k, p_ref in enumerate(p_refs):
            gk = p_ref[0].astype(F32)
            for d in range(1, N_DEV):
                gk = gk + p_ref[d].astype(F32)
            g = gk if g is None else jnp.where(pl.program_id(0) >= starts[k], gk, g)
        nm = ADAM_B1 * m_ref[...] + (1.0 - ADAM_B1) * g
        nv = ADAM_B2 * v_ref[...] + (1.0 - ADAM_B2) * (g * g)
        g_ref[...] = g
        nm_ref[...] = nm
        nv_ref[...] = nv
        d_ref[...] = -ADAM_LR * ((nm / c1) / (jnp.sqrt(nv / c2) + ADAM_EPS) + ADAM_WD * w_ref[...])
        if comm is not None:
            @pl.when(pl.program_id(0) == n_steps - 1)
            def _():
                comm.finish(*cargs)

    row = pl.BlockSpec((None, t, cols), lambda i: (0, i, 0)) if lead else pl.BlockSpec((t, cols), lambda i: (i, 0))
    c_in_specs, c_args, c_out_specs, c_out_shapes, c_sems = _carry_specs(comm)
    res = pl.pallas_call(
        body, name=name, grid=(n_steps,),
        in_specs=[pl.BlockSpec((N_DEV, t, cols), lambda i, lo=lo, n=n: (0, jnp.clip(i - lo, 0, n - 1), 0)) for lo, n in zip(starts, counts)]
        + [row, row, row] + c_in_specs,
        out_specs=[row] * 4 + c_out_specs, out_shape=[jax.ShapeDtypeStruct(w.shape, F32)] * 4 + c_out_shapes, scratch_shapes=c_sems,
        compiler_params=_CP(dimension_semantics=("arbitrary" if comm is not None else "parallel",)),
    )(*plist, w, m, v, *c_args)
    return list(res[:4]), list(res[4:])


def _train_step(x, positions, target, wts, ms, vs, raw):
    small_shapes = [wts[n].shape for n in SMALL]

    big_of = {tag: [n for n in BIG if n.startswith(tag)] for tag in ("mla", "gla", "lru", "ssd")}
    full = {n: wts[n] for n in REPL}

    def gather_comm(names, extra=()):
        return GatherComm([wts[n].astype(BF16) for n in names] + list(extra))

    def assemble(names, got):
        for k, n in enumerate(names):
            full[n] = jnp.concatenate([got[k][j] for j in range(N_CHIPS)], axis=_shard_axis(n))

    first = [n for n in big_of["mla"] if n != "mla_w_out"]
    got = _run_comm("gather_first", gather_comm(first, [_pack([wts[n] for n in SMALL], F32, SMALL_ROWS)]))
    assemble(first, got)
    per_chip_small = [_unpack(got[-1][j], small_shapes) for j in range(N_CHIPS)]
    for k, n in enumerate(SMALL):
        full[n] = jnp.concatenate([per_chip_small[j][k] for j in range(N_CHIPS)], axis=_shard_axis(n))

    cos, sin = _rope_tables(positions)
    ng = full["norm_g"]
    behind_attn = ["mla_w_out"] + big_of["gla"] + big_of["lru"]

    def mla_w_out(got):
        assemble(behind_attn, got)
        return full["mla_w_out"].astype(BF16)

    h1, b0, _ = _mla_layer(x, ng[0], full, cos, sin, fwd_comm=gather_comm(behind_attn), late_w_out=mla_w_out)

    def joined(got_k, axis):
        return jnp.concatenate([got_k[j] for j in range(N_CHIPS)], axis=axis)

    ssd_in = wts["ssd_w_in"].astype(BF16)
    half = ssd_in.shape[0] // 2
    h2, b1, got = _gla_layer(h1, ng[1], full, fwd_comm=GatherComm([ssd_in[:half]]))
    top = joined(got[0], -1)
    h3, b2, got = _lru_layer(h2, ng[2], full, fwd_comm=GatherComm([ssd_in[half:]]))
    full["ssd_w_in"] = jnp.concatenate([top, joined(got[0], -1)], axis=0)
    h4, b3 = _ssd_layer(h3, ng[3], full, fwd_comm=gather_comm(["ssd_w_out"]), late_w_out=lambda got: joined(got[0], 0))
    loss, dh, d_final = _loss_op(h4, target, full["final_g"])
    loss = loss[0, 0]
    grads = {"final_g": d_final.reshape(-1)}
    d_norms = [None] * 4

    def shards_of(n, g):
        return jnp.stack(jnp.split(g.astype(BF16), N_CHIPS, axis=_shard_axis(n)))

    def shards(n):
        return shards_of(n, grads[n])

    parts = {}
    dh, d_norms[3], gw, got = b3(dh, make_comm=lambda dw: ExchangeComm([shards_of("ssd_w_out", dw)]))
    parts["ssd_w_out"] = got[0]
    grads.update(gw)
    ssd_in_g = shards("ssd_w_in")
    half = ssd_in_g.shape[1] // 2
    dh, d_norms[2], gw, got = b2(dh, make_comm=lambda dw: ExchangeComm([ssd_in_g[:, :half], shards_of("lru_w_out", dw)]))
    parts["ssd_w_in"], parts["lru_w_out"] = [got[0]], got[1]
    grads.update(gw)
    dh, d_norms[1], gw, got = b1(dh, make_comm=lambda dw: ExchangeComm([ssd_in_g[:, half:]]))
    parts["ssd_w_in"].append(got[0])
    grads.update(gw)
    repl_early = _pack([grads[n] for n in REPL_EARLY], BF16, SMALL_ROWS)

    behind_attn = ["gla_w_out", "lru_w_in", "gla_w_in"]
    dx, d_norms[0], gw, got = b0(dh, make_comm=lambda dw: ExchangeComm([shards_of("mla_w_out", dw)] + [shards(n) for n in behind_attn],
                                                                        [repl_early]))
    parts.update(zip(["mla_w_out"] + behind_attn, got))
    repl_early_parts = got[-1]
    grads.update(gw)
    grads["norm_g"] = jnp.concatenate(d_norms, axis=0)
    psmall = jnp.stack([_pack([jnp.split(grads[n], N_CHIPS, axis=_shard_axis(n))[j] for n in SMALL], F32, SMALL_ROWS) for j in range(N_CHIPS)])
    prepl = _pack([grads[n] for n in REPL_LATE], F32, SMALL_ROWS)

    out = {}
    kinds = ("grad", "delta", "new_m", "new_v")
    behind = {"ssd_w_in": ExchangeComm([shards("mla_w_in")]),
              "gla_w_in": ExchangeComm([shards("mla_w_uq"), shards("mla_w_ukv"), psmall], [prepl])}
    late = {}
    for n in ["ssd_w_in", "gla_w_in"] + [n for n in BIG if n not in behind]:
        if n in behind:
            res, late[n] = _adamw("adam_" + n, parts[n], *(r[n] for r in raw), lead=True, comm=behind[n])
            if n == "ssd_w_in":
                parts["mla_w_in"] = late[n][0]
            else:
                parts["mla_w_uq"], parts["mla_w_ukv"] = late[n][0], late[n][1]
        else:
            res, _ = _adamw("adam_" + n, parts[n], *(r[n] for r in raw), lead=True)
        for kind, a in zip(kinds, res):
            out[kind, n] = a
    late_parts = late["gla_w_in"]
    for tag, names, p in (("adam_small", SMALL, late_parts[-2]), ("adam_repl_early", REPL_EARLY, repl_early_parts),
                          ("adam_repl_late", REPL_LATE, late_parts[-1])):
        shapes = [wts[n].shape for n in names]
        packed = [_pack([d[n] for n in names], F32, SMALL_ROWS) for d in (wts, ms, vs)]
        for kind, buf in zip(kinds, _adamw(tag, p, *packed)[0]):
            for n, a in zip(names, _unpack(buf, shapes)):
                out[kind, n] = a
    loss = lax.psum(loss, ("x", "y", "c"))
    return loss, dx, out


def kernel(x, positions, norm_g, final_g, mla_w_in, mla_g_q, mla_w_uq, mla_g_kv, mla_w_ukv, mla_w_out, gla_w_in, gla_w_gk2, gla_b_gk, gla_g_o, gla_w_out, lru_w_in, lru_conv_w, lru_conv_b, lru_w_a, lru_b_a, lru_w_x, lru_b_x, lru_lam, lru_w_out, ssd_w_in, ssd_conv_w, ssd_conv_b, ssd_dt_bias, ssd_a_log, ssd_d, ssd_g_norm, ssd_w_out, loss_target, m_norm_g, m_final_g, m_mla_w_in, m_mla_g_q, m_mla_w_uq, m_mla_g_kv, m_mla_w_ukv, m_mla_w_out, m_gla_w_in, m_gla_w_gk2, m_gla_b_gk, m_gla_g_o, m_gla_w_out, m_lru_w_in, m_lru_conv_w, m_lru_conv_b, m_lru_w_a, m_lru_b_a, m_lru_w_x, m_lru_b_x, m_lru_lam, m_lru_w_out, m_ssd_w_in, m_ssd_conv_w, m_ssd_conv_b, m_ssd_dt_bias, m_ssd_a_log, m_ssd_d, m_ssd_g_norm, m_ssd_w_out, v_norm_g, v_final_g, v_mla_w_in, v_mla_g_q, v_mla_w_uq, v_mla_g_kv, v_mla_w_ukv, v_mla_w_out, v_gla_w_in, v_gla_w_gk2, v_gla_b_gk, v_gla_g_o, v_gla_w_out, v_lru_w_in, v_lru_conv_w, v_lru_conv_b, v_lru_w_a, v_lru_b_a, v_lru_w_x, v_lru_b_x, v_lru_lam, v_lru_w_out, v_ssd_w_in, v_ssd_conv_w, v_ssd_conv_b, v_ssd_dt_bias, v_ssd_a_log, v_ssd_d, v_ssd_g_norm, v_ssd_w_out):
    given = dict(locals())
    stacked = [n for n in WEIGHTS if n not in ("norm_g", "final_g")]

    def blocks(prefix):
        return {n: (given[prefix + n][0] if n in stacked else given[prefix + n]) for n in WEIGHTS}

    raw = [{n: given[prefix + n] for n in BIG} for prefix in ("", "m_", "v_")]
    loss, dx, out = _train_step(x[0], positions[0], loss_target[0], blocks(""), blocks("m_"), blocks("v_"), raw)
    res = [loss, dx[None]]
    for kind in ("grad", "delta", "new_m", "new_v"):
        res += [(out[kind, n][None] if n in stacked and n not in BIG else out[kind, n]) for n in WEIGHTS]
    return tuple(res)
```

```python
import functools
import math

import jax
import jax.numpy as jnp
from jax import lax
from jax.experimental import pallas as pl
from jax.experimental.pallas import tpu as pltpu

F32 = jnp.float32
BF16 = jnp.bfloat16

V7X_VMEM_BYTES = 64 * 1024 * 1024
VMEM_LIMIT = V7X_VMEM_BYTES - 8 * 1024 * 1024
LANE = 128

D_MODEL = 1024
NORM_EPS = 1e-6
MLA_HEADS, MLA_Q_RANK, MLA_KV_RANK, MLA_NOPE, MLA_ROPE, MLA_V = 16, 384, 256, 64, 32, 64
MLA_QK = MLA_NOPE + MLA_ROPE
ROPE_THETA = 10000.0
GLA_HEADS, GLA_DK, GLA_DV, GLA_RANK, GLA_TAU, GLA_CHUNK = 4, 128, 256, 16, 16.0, 64
LRU_WIDTH, LRU_BLOCKS, LRU_BLOCK, LRU_C, CONV_W = 1280, 10, 128, 8.0, 4
SSD_INNER, SSD_P, SSD_HEADS, SSD_GROUPS, SSD_HPG, SSD_STATE, SSD_CHUNK = 2048, 64, 32, 8, 4, 128, 64
ADAM_LR, ADAM_B1, ADAM_B2, ADAM_EPS, ADAM_WD, ADAM_STEP = 0.001, 0.9, 0.999, 1e-08, 0.01, 10

_CP = functools.partial(pltpu.CompilerParams, vmem_limit_bytes=VMEM_LIMIT)


def _bdot(a, b):
    return jnp.dot(a.astype(BF16), b.astype(BF16), preferred_element_type=F32)


def _bdot_nt(a, b):
    return lax.dot_general(a.astype(BF16), b.astype(BF16), (((1,), (1,)), ((), ())), preferred_element_type=F32)


def _bdot_tn(a, b):
    return lax.dot_general(a.astype(BF16), b.astype(BF16), (((0,), (0,)), ((), ())), preferred_element_type=F32)


def _tri(n):
    r = lax.broadcasted_iota(jnp.int32, (n, n), 0)
    c = lax.broadcasted_iota(jnp.int32, (n, n), 1)
    return r >= c


def _rms(x, g):
    return x * lax.rsqrt(jnp.mean(x * x, axis=-1, keepdims=True) + NORM_EPS) * g


def _silu(x):
    return x * jax.nn.sigmoid(x)


def _shift_rows(x, prev, j):
    if j == 0:
        return x
    t = x.shape[0]

    def fwd_impl(x, prev):
        row = lax.broadcasted_iota(jnp.int32, x.shape, 0)
        return jnp.where(row >= j, pltpu.roll(x, j, 0), pltpu.roll(prev, j, 0))

    @jax.custom_vjp
    def sh(x, prev):
        return fwd_impl(x, prev)

    def sh_fwd(x, prev):
        return fwd_impl(x, prev), None

    def sh_bwd(_, gy):
        row = lax.broadcasted_iota(jnp.int32, gy.shape, 0)
        back = pltpu.roll(gy, t - j, 0)
        return jnp.where(row < t - j, back, 0.0), jnp.where(row >= t - j, back, 0.0)

    sh.defvjp(sh_fwd, sh_bwd)
    return sh(x, prev)


def _cumsum_rows(x):
    zero = jnp.zeros_like(x)
    sh = 1
    while sh < x.shape[0]:
        x = x + _shift_rows(x, zero, sh)
        sh *= 2
    return x


def _one_minus_exp(x):
    series = -x * (1.0 + x * (0.5 + x * (1.0 / 6.0 + x * (1.0 / 24.0 + x * (1.0 / 120.0)))))
    return jnp.where(x > -0.05, series, 1.0 - jnp.exp(x))


def _tile(n, cap):
    if n <= cap:
        return n
    best = None
    for t in range(LANE, cap + 1, LANE):
        if n % t == 0:
            best = t
    assert best is not None, (n, cap)
    return best


MM_BLOCK_BYTES = 8 * 1024 * 1024
MM_ROWS, MM_KROWS = 256, 512
MM_TILE_BYTES = 2 * 1024 * 1024


def _mm_tiles(m, k, n, ta):
    if ta:
        return m, _tile(n, max(LANE, MM_BLOCK_BYTES // (4 * m) // LANE * LANE)), _tile(k, MM_KROWS)
    tn = _tile(n, max(LANE, MM_BLOCK_BYTES // (2 * k) // LANE * LANE))
    rows = min(4 * MM_ROWS, max(MM_ROWS, MM_TILE_BYTES // (4 * tn) // MM_ROWS * MM_ROWS))
    return _tile(m, rows), tn, k


def _mm(name, a, b, *, ta=False, tb=False, add=None, out_dtype=F32):
    m, k = (a.shape[1], a.shape[0]) if ta else a.shape
    n, kb = (b.shape[0], b.shape[1]) if tb else (b.shape[1], b.shape[0])
    assert k == kb, (name, a.shape, b.shape, ta, tb)
    tm, tn, tk = _mm_tiles(m, k, n, ta)
    nk = k // tk
    dn = (((0 if ta else 1,), (1 if tb else 0,)), ((), ()))
    has_add = add is not None

    def finish(refs, r):
        if has_add:
            r = r + refs[2][...].astype(F32)
        return r.astype(out_dtype)

    def body_one(*refs):
        a_ref, b_ref, o_ref = refs[0], refs[1], refs[-1]
        o_ref[...] = finish(refs, lax.dot_general(a_ref[...].astype(BF16), b_ref[...].astype(BF16), dn, preferred_element_type=F32))

    def body_acc(*refs):
        a_ref, b_ref = refs[0], refs[1]
        o_ref, acc = refs[-2], refs[-1]
        kk = pl.program_id(2)

        @pl.when(kk == 0)
        def _():
            acc[...] = jnp.zeros(acc.shape, F32)

        acc[...] += lax.dot_general(a_ref[...].astype(BF16), b_ref[...].astype(BF16), dn, preferred_element_type=F32)

        @pl.when(kk == nk - 1)
        def _():
            o_ref[...] = finish(refs, acc[...])

    a_spec = pl.BlockSpec((tk, tm), lambda i, j, q: (q, i)) if ta else pl.BlockSpec((tm, tk), lambda i, j, q: (i, q))
    b_spec = pl.BlockSpec((tn, tk), lambda i, j, q: (j, q)) if tb else pl.BlockSpec((tk, tn), lambda i, j, q: (q, j))
    o_spec = pl.BlockSpec((tm, tn), lambda i, j, q: (i, j))
    in_specs, args = [a_spec, b_spec], [a, b]
    if has_add:
        in_specs.append(o_spec)
        args.append(add)
    return pl.pallas_call(
        body_one if nk == 1 else body_acc, name=name, grid=(m // tm, n // tn, nk), in_specs=in_specs, out_specs=o_spec,
        out_shape=jax.ShapeDtypeStruct((m, n), out_dtype), scratch_shapes=[] if nk == 1 else [pltpu.VMEM((tm, tn), F32)],
        compiler_params=_CP(dimension_semantics=("parallel", "parallel", "arbitrary")),
    )(*args)


class In:
    def __init__(self, arr, block, imap, kind="x", per_h=False, gdtype=F32, gshape=None, gimap=None, prefixed=False):
        self.arr, self.block, self.imap, self.kind, self.per_h, self.gdtype = arr, tuple(block), imap, kind, per_h, gdtype
        self.prefixed = prefixed
        self.gshape = tuple(gshape) if gshape is not None else tuple(arr.shape)
        self.gimap = gimap if gimap is not None else imap

    def spec(self, rev_g=None):
        imap = self.imap
        if rev_g is None:
            return pl.BlockSpec(self.block, lambda h, g: imap(h, g))
        return pl.BlockSpec(self.block, lambda h, g: imap(h, rev_g - 1 - g))


class Out:
    def __init__(self, shape, dtype, block, imap):
        self.shape, self.dtype, self.block, self.imap = tuple(shape), dtype, tuple(block), imap

    def spec(self, rev_g=None):
        imap = self.imap
        if rev_g is None:
            return pl.BlockSpec(self.block, lambda h, g: imap(h, g))
        return pl.BlockSpec(self.block, lambda h, g: imap(h, rev_g - 1 - g))


def _load_f32(ref, rows=None):
    v = ref[...] if rows is None else ref[0:rows]
    return v.astype(F32) if jnp.issubdtype(v.dtype, jnp.floating) else v


def _state_out(grid, shape):
    nd = len(shape)
    return Out(tuple(grid) + tuple(shape), F32, (None, None) + tuple(shape), lambda h, g: (h, g) + (0,) * nd)


def _carry(comm, grid, refs, n_in, n_out, n_scr):
    n_c = len(comm.ops) if comm is not None else 0
    n_s = len(comm.sem_shapes) if comm is not None else 0
    p = 0
    in_refs = refs[p:p + n_in]; p += n_in
    c_src = refs[p:p + n_c]; p += n_c
    out_refs = refs[p:p + n_out]; p += n_out
    c_dst = refs[p:p + n_c]; p += n_c
    scr = refs[p:p + n_scr]; p += n_scr
    c_sem = refs[p:p + n_s]
    step = pl.program_id(0) * grid[1] + pl.program_id(1)
    n_steps = grid[0] * grid[1]
    when = (step == 0, step == _forward_step(n_steps), step == n_steps - 1)
    return in_refs, out_refs, scr, (c_src, c_dst, c_sem), when


def _forward_step(n_steps):
    return max(0, min(n_steps - 2, (3 * n_steps) // 4))


def _carry_specs(comm):
    if comm is None:
        return [], [], [], [], []
    any_spec = pl.BlockSpec(memory_space=pl.ANY)
    n = len(comm.ops)
    return [any_spec] * n, list(comm.ops), [any_spec] * n, list(comm.out_shapes), list(comm.sem_shapes)


def _op_fwd(name, f, grid, ins, outs, state_shapes=(), comm=None, prefix_rows=None):
    assert prefix_rows is None or not state_shapes
    n_in, n_out, n_st = len(ins), len(outs), len(state_shapes)
    st_outs = [_state_out(grid, s) for s in state_shapes]

    def body(*refs):
        in_refs, o_refs, st_scr, cargs, (first, fwd_step, last) = _carry(comm, grid, refs, n_in, n_out + n_st, n_st)
        out_refs, sv_refs = o_refs[:n_out], o_refs[n_out:]
        if comm is not None:
            @pl.when(first)
            def _():
                comm.start(*cargs)

            @pl.when(fwd_step)
            def _():
                comm.forward(*cargs)
        g = pl.program_id(1)
        if n_st:
            @pl.when(g == 0)
            def _():
                for s in st_scr:
                    s[...] = jnp.zeros(s.shape, F32)

        def compute(rows):
            vals = [_load_f32(r, rows if i.prefixed else None) for r, i in zip(in_refs, ins)]
            sts = [s[...] for s in st_scr]
            o, ns = f(g, vals, sts)
            for r, v in zip(out_refs, o):
                r[...] = v.astype(r.dtype)
            for r, s in zip(sv_refs, sts):
                r[...] = s
            for s, v in zip(st_scr, ns):
                s[...] = v

        if prefix_rows is None:
            compute(None)
        else:
            per = grid[1] // len(prefix_rows)
            for lv, rows in enumerate(prefix_rows):
                pl.when(g // per == lv)(functools.partial(compute, rows))
        if comm is not None:
            @pl.when(last)
            def _():
                comm.finish(*cargs)

    all_outs = list(outs) + st_outs
    c_in_specs, c_args, c_out_specs, c_out_shapes, c_sems = _carry_specs(comm)
    res = pl.pallas_call(
        body, name=name, grid=tuple(grid), in_specs=[i.spec() for i in ins] + c_in_specs,
        out_specs=[o.spec() for o in all_outs] + c_out_specs,
        out_shape=[jax.ShapeDtypeStruct(o.shape, o.dtype) for o in all_outs] + c_out_shapes,
        scratch_shapes=[pltpu.VMEM(tuple(s), F32) for s in state_shapes] + c_sems,
        compiler_params=_CP(dimension_semantics=("arbitrary", "arbitrary")),
    )(*[i.arr for i in ins], *c_args)
    n_all = n_out + n_st
    return list(res[:n_out]), list(res[n_out:n_all]), list(res[n_all:])


def _op_bwd(name, f, grid, ins, outs, state_shapes, saved, douts, addto=None, comm=None, prefix_rows=None):
    n_in, n_out, n_st = len(ins), len(outs), len(state_shapes)
    n_g = grid[1]
    assert prefix_rows is None or all(i.prefixed and i.per_h for i in ins if i.kind == "p")
    addto = addto or {}
    diff = [k for k, i in enumerate(ins) if i.kind in ("x", "p")]
    add_idx = sorted(addto)
    st_ins = [In(s, o.block, o.imap, "c") for s, o in zip(saved, [_state_out(grid, s) for s in state_shapes])]
    dout_ins = [In(d, o.block, o.imap, "c") for d, o in zip(douts, outs)]
    add_ins = []
    for k in add_idx:
        i, a = ins[k], addto[k]
        blk = i.block if i.kind == "x" else i.block[:-2] + a.shape[-2:]
        add_ins.append(In(a, blk, i.gimap if i.kind == "x" else i.imap, "c"))
    g_outs = []
    for k in diff:
        i = ins[k]
        g_outs.append(Out(i.gshape, i.gdtype if i.kind == "x" else F32, i.block, i.gimap))

    def body(*refs):
        all_in, go_refs, ds_scr, cargs, (first_step, fwd_step, last_step) = _carry(comm, grid, refs, n_in + n_st + n_out + len(add_idx),
                                                                                    len(diff), n_st)
        if comm is not None:
            @pl.when(first_step)
            def _():
                comm.start(*cargs)

            @pl.when(fwd_step)
            def _():
                comm.forward(*cargs)
        p = 0
        in_refs = all_in[p:p + n_in]; p += n_in
        sv_refs = all_in[p:p + n_st]; p += n_st
        do_refs = all_in[p:p + n_out]; p += n_out
        ad_refs = all_in[p:p + len(add_idx)]
        hh = pl.program_id(0)
        step = pl.program_id(1)
        g = n_g - 1 - step
        if n_st:
            @pl.when(step == 0)
            def _():
                for s in ds_scr:
                    s[...] = jnp.zeros(s.shape, F32)

        def compute(rows):
            vals = [_load_f32(r, rows if i.prefixed else None) for r, i in zip(in_refs, ins)]
            sts = [r[...] for r in sv_refs]

            def fw(dvals, states):
                full = list(vals)
                for k, v in zip(diff, dvals):
                    full[k] = v
                o, ns = f(g, full, states)
                return list(o), list(ns)

            _, vjp = jax.vjp(fw, [vals[k] for k in diff], sts)
            cts = [r[...].astype(F32) for r in do_refs]
            dns = [s[...] for s in ds_scr]
            dvals, dsts = vjp((cts, dns))
            adds = dict(zip(add_idx, ad_refs))
            for k, r, dv in zip(diff, go_refs, dvals):
                i = ins[k]
                if i.kind == "x":
                    if k in adds:
                        dv = dv + adds[k][...].astype(F32)
                    r[...] = dv.astype(r.dtype)
                elif rows is not None:
                    r[0:rows] += dv
                else:
                    first = (step == 0) if i.per_h else jnp.logical_and(step == 0, hh == 0)

                    @pl.when(first)
                    def _(r=r, dv=dv, k=k):
                        r[...] = dv
                        if k in adds:
                            lead = adds[k].shape[0]
                            r[0:lead] += adds[k][...]

                    @pl.when(jnp.logical_not(first))
                    def _(r=r, dv=dv):
                        r[...] += dv
            for s, v in zip(ds_scr, dsts):
                s[...] = v

        if prefix_rows is None:
            compute(None)
        else:
            @pl.when(step == 0)
            def _():
                for k, r in zip(diff, go_refs):
                    if ins[k].kind == "p":
                        r[...] = jnp.zeros(r.shape, F32)
            per = n_g // len(prefix_rows)
            for lv, rows in enumerate(prefix_rows):
                pl.when(g // per == lv)(functools.partial(compute, rows))
        if comm is not None:
            @pl.when(last_step)
            def _():
                comm.finish(*cargs)

    all_ins = list(ins) + st_ins + dout_ins + add_ins
    c_in_specs, c_args, c_out_specs, c_out_shapes, c_sems = _carry_specs(comm)
    res = pl.pallas_call(
        body, name=name, grid=tuple(grid), in_specs=[i.spec(n_g) for i in all_ins] + c_in_specs,
        out_specs=[o.spec(n_g) for o in g_outs] + c_out_specs,
        out_shape=[jax.ShapeDtypeStruct(o.shape, o.dtype) for o in g_outs] + c_out_shapes,
        scratch_shapes=[pltpu.VMEM(tuple(s), F32) for s in state_shapes] + c_sems,
        compiler_params=_CP(dimension_semantics=("arbitrary", "arbitrary")),
    )(*[i.arr for i in all_ins], *c_args)
    return list(res[:len(g_outs)]), list(res[len(g_outs):])


class Op:
    def __init__(self, name, f, grid, ins, outs, state_shapes=(), prefix_rows=None):
        self.name, self.f, self.grid, self.ins, self.outs, self.state_shapes = name, f, grid, ins, outs, state_shapes
        self.prefix_rows = prefix_rows
        self.saved = None

    def fwd(self, comm=None):
        res, self.saved, self.fwd_comm_out = _op_fwd(self.name + "_fwd", self.f, self.grid, self.ins, self.outs, self.state_shapes, comm,
                                                     self.prefix_rows)
        return res

    def bwd(self, douts, addto=None, comm=None):
        res, self.bwd_comm_out = _op_bwd(self.name + "_bwd", self.f, self.grid, self.ins, self.outs, self.state_shapes, self.saved, douts,
                                         addto, comm, self.prefix_rows)
        return res


def _rows(arr, t, kind="x", gdtype=F32):
    return In(arr, (t, arr.shape[1]), lambda h, g: (g, 0), kind, gdtype=gdtype)


def _whole(arr, kind="p"):
    nd = arr.ndim
    return In(arr, arr.shape, lambda h, g: (0,) * nd, kind)


def _rows_out(s, n, t, dtype):
    return Out((s, n), dtype, (t, n), lambda h, g: (g, 0))


ROW_T = 256


def _rms_op(name, x, gain, out_dtype=BF16, gdtype=F32):
    s, n = x.shape

    def f(g, vals, sts):
        return [_rms(vals[0], vals[1])], []

    return Op(name, f, (1, s // ROW_T), [_rows(x, ROW_T, gdtype=gdtype), _whole(gain.reshape(1, n))], [_rows_out(s, n, ROW_T, out_dtype)])


def _mla_prep_op(qn, q1, q2, kn, kr, v, cos, sin):
    s = qn.shape[0]
    hd, half = MLA_HEADS, MLA_ROPE // 2

    def f(g, vals, sts):
        qn, q1, q2, kn, kr, v, cos, sin = vals
        cos_h, sin_h = jnp.tile(cos, (1, hd)), jnp.tile(sin, (1, hd))
        r1 = q1 * cos_h - q2 * sin_h
        r2 = q2 * cos_h + q1 * sin_h
        k1, k2 = kr[:, 0:half], kr[:, half:2 * half]
        kr1 = k1 * cos - k2 * sin
        kr2 = k2 * cos + k1 * sin
        zpad = jnp.zeros((qn.shape[0], LANE - MLA_QK), F32)
        qs, ks, vs = [], [], []
        for h in range(hd):
            a, b = h * MLA_NOPE, (h + 1) * MLA_NOPE
            c, d = h * half, (h + 1) * half
            qs.append(jnp.concatenate([qn[:, a:b], r1[:, c:d], r2[:, c:d], zpad], axis=1))
            ks.append(jnp.concatenate([kn[:, a:b], kr1, kr2, zpad], axis=1))
            vs.append(v[:, a:b])
        return [jnp.stack(qs, 0), jnp.stack(ks, 0), jnp.stack(vs, 0)], []

    ins = [_rows(qn, ROW_T, gdtype=BF16), _rows(q1, ROW_T, gdtype=BF16), _rows(q2, ROW_T, gdtype=BF16), _rows(kn, ROW_T, gdtype=BF16),
           _rows(kr, ROW_T, gdtype=BF16), _rows(v, ROW_T, gdtype=BF16), _rows(cos, ROW_T, "c"), _rows(sin, ROW_T, "c")]
    outs = [Out((hd, s, LANE), BF16, (hd, ROW_T, LANE), lambda h, g: (0, g, 0)),
            Out((hd, s, LANE), BF16, (hd, ROW_T, LANE), lambda h, g: (0, g, 0)),
            Out((hd, s, MLA_V), BF16, (hd, ROW_T, MLA_V), lambda h, g: (0, g, 0))]
    return Op("mla_prep", f, (1, s // ROW_T), ins, outs)


ATT_TQ = 256
ATT_LEVELS = 4


def _mla_attn_op(q, k, v):
    hd, s, _ = q.shape
    scale = MLA_QK ** -0.5

    def f(g, vals, sts):
        q, k, v = vals
        sc = _bdot_nt(q, k) * scale
        r = lax.broadcasted_iota(jnp.int32, sc.shape, 0) + g * ATT_TQ
        c = lax.broadcasted_iota(jnp.int32, sc.shape, 1)
        sc = jnp.where(r >= c, sc, -1e30)
        m = lax.stop_gradient(jnp.max(sc, axis=-1, keepdims=True))
        p = jnp.exp(sc - m)
        p = p / jnp.sum(p, axis=-1, keepdims=True)
        return [_bdot(p, v)], []

    ins = [In(q, (None, ATT_TQ, LANE), lambda h, g: (h, g, 0), "x", gdtype=BF16),
           In(k, (None, s, LANE), lambda h, g: (h, 0, 0), "p", per_h=True, prefixed=True),
           In(v, (None, s, MLA_V), lambda h, g: (h, 0, 0), "p", per_h=True, prefixed=True)]
    outs = [Out((hd, s, MLA_V), F32, (None, ATT_TQ, MLA_V), lambda h, g: (h, g, 0))]
    return Op("mla_attn", f, (hd, s // ATT_TQ), ins, outs, prefix_rows=[(lv + 1) * (s // ATT_LEVELS) for lv in range(ATT_LEVELS)])


def _mla_post_op(o, gate):
    hd, s, _ = o.shape

    def f(g, vals, sts):
        o, gate = vals
        cat = jnp.concatenate([o[h] for h in range(hd)], axis=1)
        return [cat * _silu(gate)], []

    ins = [In(o, (hd, ROW_T, MLA_V), lambda h, g: (0, g, 0), "x"), _rows(gate, ROW_T, gdtype=BF16)]
    return Op("mla_post", f, (1, s // ROW_T), ins, [_rows_out(s, hd * MLA_V, ROW_T, BF16)])


def _gla_gate_op(gk, w2, b):
    s = gk.shape[0]

    def f(g, vals, sts):
        gk, w2, b = vals
        return [jax.nn.log_sigmoid(_bdot(gk, w2) + b) / GLA_TAU], []

    ins = [_rows(gk, ROW_T, gdtype=BF16), _whole(w2), _whole(b)]
    return Op("gla_gate", f, (1, s // ROW_T), ins, [_rows_out(s, GLA_HEADS * GLA_DK, ROW_T, F32)])


def _gla_core_op(q, k, v, gate, la, g_o):
    s = q.shape[0]
    c, nh = GLA_CHUNK, GLA_HEADS

    def f(g, vals, sts):
        q, k, v, gate, la, g_o = vals
        tri = _tri(c)
        b = _cumsum_rows(la)
        b_last = jnp.sum(la, axis=0, keepdims=True)
        qt = q * (GLA_DK ** -0.5) * jnp.exp(b)
        kt = k * jnp.exp(-b)
        kd = k * jnp.exp(b_last - b)
        ys, new_sts = [], []
        for h in range(nh):
            ks, vs = slice(h * GLA_DK, (h + 1) * GLA_DK), slice(h * GLA_DV, (h + 1) * GLA_DV)
            att = jnp.where(tri, _bdot_nt(qt[:, ks], kt[:, ks]), 0.0)
            o = _bdot(att, v[:, vs]) + _bdot_nt(qt[:, ks], sts[h])
            new_sts.append(jnp.exp(b_last[:, ks]) * sts[h] + _bdot_tn(v[:, vs], kd[:, ks]))
            ys.append(_rms(o, g_o) * _silu(gate[:, vs]))
        return [jnp.concatenate(ys, axis=1)], new_sts

    ins = [_rows(q, c, gdtype=BF16), _rows(k, c, gdtype=BF16), _rows(v, c, gdtype=BF16), _rows(gate, c, gdtype=BF16), _rows(la, c), _whole(g_o)]
    outs = [_rows_out(s, nh * GLA_DV, c, BF16)]
    return Op("gla_core", f, (1, s // c), ins, outs, [(GLA_DV, GLA_DK)] * nh)


LRU_T = 256


def _lru_op(gate, u, conv_w, conv_b, w_a, b_a, w_x, b_x, lam):
    s, w = u.shape
    t = LRU_T

    def f(g, vals, sts):
        gate, u, cw, cb, w_a, b_a, w_x, b_x, lam = vals
        u_prev, h_prev = sts
        uc = cb
        for kk in range(CONV_W):
            uc = uc + cw[kk] * _shift_rows(u, u_prev, CONV_W - 1 - kk)
        ra, ri = [], []
        for n in range(LRU_BLOCKS):
            blk = uc[:, n * LRU_BLOCK:(n + 1) * LRU_BLOCK]
            ra.append(_bdot(blk, w_a[n]))
            ri.append(_bdot(blk, w_x[n]))
        r = jax.nn.sigmoid(jnp.concatenate(ra, axis=1) + b_a)
        i = jax.nn.sigmoid(jnp.concatenate(ri, axis=1) + b_x)
        log_a = -LRU_C * r * jax.nn.softplus(-lam)
        a = jnp.exp(log_a)
        bb = jnp.sqrt(_one_minus_exp(2.0 * log_a)) * (i * uc)
        zero = jnp.zeros_like(a)
        sh = 1
        while sh < t:
            a_s = _shift_rows(a - 1.0, zero, sh) + 1.0
            b_s = _shift_rows(bb, zero, sh)
            bb = a * b_s + bb
            a = a * a_s
            sh *= 2
        hs = bb + a * h_prev
        last = (lax.broadcasted_iota(jnp.int32, hs.shape, 0) == t - 1).astype(F32)
        h_last = jnp.sum(hs * last, axis=0, keepdims=True)
        return [hs * _silu(gate)], [u, h_last]

    ins = [_rows(gate, t, gdtype=BF16), _rows(u, t, gdtype=BF16), _whole(conv_w), _whole(conv_b), _whole(w_a), _whole(b_a), _whole(w_x),
           _whole(b_x), _whole(lam)]
    return Op("lru_core", f, (1, s // t), ins, [_rows_out(s, w, t, BF16)], [(t, w), (1, w)])


def _ssd_conv_op(xbc, conv_w, conv_b):
    s, w = xbc.shape
    t = ROW_T
    n_x, n_b = SSD_INNER, SSD_GROUPS * SSD_STATE

    def f(g, vals, sts):
        xbc, cw, cb = vals
        acc = cb
        for kk in range(CONV_W):
            acc = acc + cw[kk] * _shift_rows(xbc, sts[0], CONV_W - 1 - kk)
        y = _silu(acc)
        return [y[:, :n_x], y[:, n_x:n_x + n_b], y[:, n_x + n_b:]], [xbc]

    ins = [_rows(xbc, t, gdtype=BF16), _whole(conv_w), _whole(conv_b)]
    outs = [_rows_out(s, n_x, t, F32), _rows_out(s, n_b, t, F32), _rows_out(s, n_b, t, F32)]
    return Op("ssd_conv", f, (1, s // t), ins, outs, [(t, w)])


SSD_L = 512


def _ssd_core_op(x, bm, cm, z, dt, dt_bias, a_log, d_skip, g_norm):
    s = x.shape[0]
    c, hg, p = SSD_L, SSD_HPG, SSD_P
    gw = hg * p

    def f(g, vals, sts):
        x, bm, cm, z, dtr, dt_bias, a_log, d_skip, g_norm = vals
        tri = _tri(c)
        dt = jax.nn.softplus(dtr + dt_bias)
        da = dt * (-jnp.exp(a_log))
        cs = _cumsum_rows(da)
        cs_last = jnp.sum(da, axis=0, keepdims=True)
        cs_t = jnp.transpose(jnp.concatenate([cs, jnp.zeros((c, LANE - hg), F32)], axis=1))
        cb = _bdot_nt(cm, bm)
        ys, new_st = [], []
        for h in range(hg):
            cs_h = cs[:, h:h + 1]
            cs_row = cs_t[h:h + 1, :]
            seg = jnp.where(tri, cs_h - cs_row, 0.0)
            lmat = jnp.where(tri, jnp.exp(seg), 0.0)
            x_h = x[:, h * p:(h + 1) * p]
            xdt = x_h * dt[:, h:h + 1]
            y_diag = _bdot(cb * lmat, xdt)
            decay = jnp.exp(cs_last[:, h:h + 1] - cs_h)
            states = _bdot_tn(xdt * decay, bm)
            y_off = _bdot_nt(cm, sts[h]) * jnp.exp(cs_h)
            new_st.append(jnp.exp(cs_last[:, h:h + 1]) * sts[h] + states)
            ys.append(y_diag + y_off + d_skip[:, h:h + 1] * x_h)
        y = jnp.concatenate(ys, axis=1) * _silu(z)
        return [_rms(y, g_norm)], new_st

    ins = [In(x, (c, gw), lambda h, g: (g, h), "x"), In(bm, (c, SSD_STATE), lambda h, g: (g, h), "x"),
           In(cm, (c, SSD_STATE), lambda h, g: (g, h), "x"), In(z, (c, gw), lambda h, g: (g, h), "x", gdtype=BF16),
           In(dt, (None, c, hg), lambda h, g: (h, g, 0), "x"),
           In(dt_bias, (None, 1, hg), lambda h, g: (h, 0, 0), "p", per_h=True),
           In(a_log, (None, 1, hg), lambda h, g: (h, 0, 0), "p", per_h=True),
           In(d_skip, (None, 1, hg), lambda h, g: (h, 0, 0), "p", per_h=True),
           In(g_norm, (1, gw), lambda h, g: (0, h), "p", per_h=True)]
    outs = [Out((s, SSD_INNER), BF16, (c, gw), lambda h, g: (g, h))]
    return Op("ssd_core", f, (SSD_GROUPS, s // c), ins, outs, [(p, SSD_STATE)] * hg)


def _loss_op(h, target, final_g):
    s, n = h.shape
    t = ROW_T
    n_g = s // t

    def body(h_ref, t_ref, g_ref, loss_ref, dh_ref, dg_ref):
        step = pl.program_id(0)

        def lossf(hv, gv):
            err = _rms(hv, gv) - t_ref[...]
            return 0.5 * jnp.sum(jnp.mean(err * err, axis=-1))

        l, (dh, dg) = jax.value_and_grad(lossf, argnums=(0, 1))(h_ref[...], g_ref[...])
        dh_ref[...] = dh

        @pl.when(step == 0)
        def _():
            loss_ref[...] = jnp.zeros(loss_ref.shape, F32)
            dg_ref[...] = jnp.zeros(dg_ref.shape, F32)

        loss_ref[...] += jnp.full(loss_ref.shape, l, F32)
        dg_ref[...] += dg

    row = pl.BlockSpec((t, n), lambda g: (g, 0))
    one = pl.BlockSpec((1, n), lambda g: (0, 0))
    return pl.pallas_call(
        body, name="loss_head", grid=(n_g,), in_specs=[row, row, one],
        out_specs=[pl.BlockSpec((1, LANE), lambda g: (0, 0)), row, one],
        out_shape=[jax.ShapeDtypeStruct((1, LANE), F32), jax.ShapeDtypeStruct((s, n), F32), jax.ShapeDtypeStruct((1, n), F32)],
        compiler_params=_CP(dimension_semantics=("arbitrary",)),
    )(h, target, final_g.reshape(1, n))


def _pad_cols(w, n):
    return jnp.pad(w, ((0, 0), (0, n - w.shape[1])))


def _pad_rows(w, n):
    return jnp.pad(w, ((0, n - w.shape[0]), (0, 0)))


def _proj_bwd(tag, u, dps, ws):
    du = None
    for i, (dp, w) in enumerate(zip(dps, ws)):
        du = _mm(f"{tag}_du{i}", dp, w, tb=True, add=du)
    dws = [_mm(f"{tag}_dw{i}", u, dp, ta=True, out_dtype=BF16) for i, dp in enumerate(dps)]
    return du, dws


def _mla_layer(h, norm_g, w, cos, sin, fwd_comm=None, late_w_out=None):
    bf = lambda a: a.astype(BF16)
    w_in, w_uq, w_ukv = w["mla_w_in"], w["mla_w_uq"], w["mla_w_ukv"]
    a0, a1, a2 = MLA_Q_RANK, MLA_Q_RANK + MLA_KV_RANK, MLA_Q_RANK + MLA_KV_RANK + MLA_ROPE
    w_cq, w_ckv, w_kr, w_g = bf(w_in[:, :a0]), bf(w_in[:, a0:a1]), bf(_pad_cols(w_in[:, a1:a2], LANE)), bf(w_in[:, a2:])
    uq = w_uq.reshape(MLA_Q_RANK, MLA_HEADS, MLA_QK)
    half = MLA_ROPE // 2
    w_qn = bf(uq[:, :, :MLA_NOPE].reshape(MLA_Q_RANK, -1))
    w_q1 = bf(uq[:, :, MLA_NOPE:MLA_NOPE + half].reshape(MLA_Q_RANK, -1))
    w_q2 = bf(uq[:, :, MLA_NOPE + half:].reshape(MLA_Q_RANK, -1))
    ukv = w_ukv.reshape(MLA_KV_RANK, MLA_HEADS, MLA_NOPE + MLA_V)
    w_kn = bf(ukv[:, :, :MLA_NOPE].reshape(MLA_KV_RANK, -1))
    w_v = bf(ukv[:, :, MLA_NOPE:].reshape(MLA_KV_RANK, -1))

    n0 = _rms_op("mla_norm", h, norm_g)
    u, = n0.fwd()
    cq, ckv, kr, gate = (_mm(f"mla_in{i}", u, wi) for i, wi in enumerate((w_cq, w_ckv, w_kr, w_g)))
    nq = _rms_op("mla_qnorm", cq, w["mla_g_q"], gdtype=BF16)
    nkv = _rms_op("mla_kvnorm", ckv, w["mla_g_kv"], gdtype=BF16)
    qn_, = nq.fwd()
    kvn_, = nkv.fwd()
    qn, q1, q2 = (_mm(f"mla_uq{i}", qn_, wi) for i, wi in enumerate((w_qn, w_q1, w_q2)))
    kn, v = (_mm(f"mla_ukv{i}", kvn_, wi) for i, wi in enumerate((w_kn, w_v)))
    prep = _mla_prep_op(qn, q1, q2, kn, kr, v, cos, sin)
    qh, kh, vh = prep.fwd()
    attn = _mla_attn_op(qh, kh, vh)
    o, = attn.fwd(fwd_comm)
    w_out = bf(w["mla_w_out"]) if late_w_out is None else late_w_out(attn.fwd_comm_out)
    post = _mla_post_op(o, gate)
    y, = post.fwd()
    h_out = _mm("mla_out", y, w_out, add=h)

    def bwd(dh, make_comm=None):
        dy = _mm("mla_out_dy", dh, w_out, tb=True, out_dtype=BF16)
        d_w_out = _mm("mla_out_dw", y, dh, ta=True, out_dtype=BF16)
        do, dgate = post.bwd([dy])
        dqh, dkh, dvh = attn.bwd([do], comm=None if make_comm is None else make_comm(d_w_out))
        dqn, dq1, dq2, dkn, dkr, dv = prep.bwd([dqh, dkh, dvh])
        dqn_, d_uq = _proj_bwd("mla_uq", qn_, (dqn, dq1, dq2), (w_qn, w_q1, w_q2))
        dkvn_, d_ukv = _proj_bwd("mla_ukv", kvn_, (dkn, dv), (w_kn, w_v))
        dcq, d_g_q = nq.bwd([dqn_])
        dckv, d_g_kv = nkv.bwd([dkvn_])
        du, d_in = _proj_bwd("mla_in", u, (dcq, dckv, dkr, dgate), (w_cq, w_ckv, w_kr, w_g))
        dh_in, d_norm = n0.bwd([du], addto={0: dh})
        shp = (MLA_Q_RANK, MLA_HEADS, -1)
        g_uq = jnp.concatenate([d_uq[0].reshape(shp), d_uq[1].reshape(shp), d_uq[2].reshape(shp)], axis=2).reshape(MLA_Q_RANK, -1)
        shp = (MLA_KV_RANK, MLA_HEADS, -1)
        g_ukv = jnp.concatenate([d_ukv[0].reshape(shp), d_ukv[1].reshape(shp)], axis=2).reshape(MLA_KV_RANK, -1)
        g_in = jnp.concatenate([d_in[0], d_in[1], d_in[2][:, :MLA_ROPE], d_in[3]], axis=1)
        return dh_in, d_norm, {"mla_w_in": g_in, "mla_g_q": d_g_q.reshape(-1), "mla_w_uq": g_uq, "mla_g_kv": d_g_kv.reshape(-1),
                               "mla_w_ukv": g_ukv, "mla_w_out": d_w_out}, attn.bwd_comm_out

    return h_out, bwd, attn.fwd_comm_out


def _gla_layer(h, norm_g, w, fwd_comm=None):
    bf = lambda a: a.astype(BF16)
    w_in = w["gla_w_in"]
    nk, nv = GLA_HEADS * GLA_DK, GLA_HEADS * GLA_DV
    cuts = (0, nk, 2 * nk, 2 * nk + nv, 2 * nk + 2 * nv)
    w_q, w_k, w_v, w_g = (bf(w_in[:, cuts[i]:cuts[i + 1]]) for i in range(4))
    w_gk = bf(_pad_cols(w_in[:, cuts[4]:], LANE))
    w2 = _pad_rows(w["gla_w_gk2"], LANE)
    b_gk = w["gla_b_gk"].reshape(1, -1)
    g_o = w["gla_g_o"].reshape(1, -1)
    w_out = bf(w["gla_w_out"])

    n0 = _rms_op("gla_norm", h, norm_g)
    u, = n0.fwd()
    q, k, v, gate, gk = (_mm(f"gla_in{i}", u, wi) for i, wi in enumerate((w_q, w_k, w_v, w_g, w_gk)))
    gop = _gla_gate_op(gk, w2, b_gk)
    la, = gop.fwd()
    core = _gla_core_op(q, k, v, gate, la, g_o)
    y, = core.fwd(fwd_comm)
    h_out = _mm("gla_out", y, w_out, add=h)

    def bwd(dh, make_comm=None):
        dy = _mm("gla_out_dy", dh, w_out, tb=True, out_dtype=BF16)
        d_w_out = _mm("gla_out_dw", y, dh, ta=True, out_dtype=BF16)
        dq, dk, dv, dgate, dla, d_g_o = core.bwd([dy], comm=None if make_comm is None else make_comm(d_w_out))
        dgk, d_w2, d_b = gop.bwd([dla])
        du, d_in = _proj_bwd("gla_in", u, (dq, dk, dv, dgate, dgk), (w_q, w_k, w_v, w_g, w_gk))
        dh_in, d_norm = n0.bwd([du], addto={0: dh})
        g_in = jnp.concatenate([d_in[0], d_in[1], d_in[2], d_in[3], d_in[4][:, :GLA_RANK]], axis=1)
        return dh_in, d_norm, {"gla_w_in": g_in, "gla_w_gk2": d_w2[:GLA_RANK], "gla_b_gk": d_b.reshape(-1), "gla_g_o": d_g_o.reshape(-1),
                               "gla_w_out": d_w_out}, core.bwd_comm_out

    return h_out, bwd, core.fwd_comm_out


def _lru_layer(h, norm_g, w, fwd_comm=None):
    bf = lambda a: a.astype(BF16)
    w_in = w["lru_w_in"]
    w_g, w_u = bf(w_in[:, :LRU_WIDTH]), bf(w_in[:, LRU_WIDTH:])
    row = lambda a: a.reshape(1, -1)
    w_out = bf(w["lru_w_out"])

    n0 = _rms_op("lru_norm", h, norm_g)
    u_, = n0.fwd()
    gate, u = (_mm(f"lru_in{i}", u_, wi) for i, wi in enumerate((w_g, w_u)))
    core = _lru_op(gate, u, w["lru_conv_w"].reshape(CONV_W, 1, -1), row(w["lru_conv_b"]), w["lru_w_a"], row(w["lru_b_a"]), w["lru_w_x"],
                   row(w["lru_b_x"]), row(w["lru_lam"]))
    y, = core.fwd(fwd_comm)
    h_out = _mm("lru_out", y, w_out, add=h)

    def bwd(dh, make_comm=None):
        dy = _mm("lru_out_dy", dh, w_out, tb=True, out_dtype=BF16)
        d_w_out = _mm("lru_out_dw", y, dh, ta=True, out_dtype=BF16)
        dgate, du, d_cw, d_cb, d_wa, d_ba, d_wx, d_bx, d_lam = core.bwd([dy], comm=None if make_comm is None else make_comm(d_w_out))
        du_, d_in = _proj_bwd("lru_in", u_, (dgate, du), (w_g, w_u))
        dh_in, d_norm = n0.bwd([du_], addto={0: dh})
        return dh_in, d_norm, {"lru_w_in": jnp.concatenate(d_in, axis=1), "lru_conv_w": d_cw.reshape(CONV_W, -1), "lru_conv_b": d_cb.reshape(-1),
                               "lru_w_a": d_wa, "lru_b_a": d_ba.reshape(-1), "lru_w_x": d_wx, "lru_b_x": d_bx.reshape(-1),
                               "lru_lam": d_lam.reshape(-1), "lru_w_out": d_w_out}, core.bwd_comm_out

    return h_out, bwd, core.fwd_comm_out


def _ssd_layer(h, norm_g, w, fwd_comm=None, late_w_out=None):
    bf = lambda a: a.astype(BF16)
    s = h.shape[0]
    w_in = w["ssd_w_in"]
    conv_dim = SSD_INNER + 2 * SSD_GROUPS * SSD_STATE
    w_z, w_xbc = bf(w_in[:, :SSD_INNER]), bf(w_in[:, SSD_INNER:SSD_INNER + conv_dim])
    w_dt = bf(_pad_cols(w_in[:, SSD_INNER + conv_dim:], LANE))
    grp = lambda a: a.reshape(SSD_GROUPS, 1, SSD_HPG)

    n0 = _rms_op("ssd_norm", h, norm_g)
    u, = n0.fwd()
    z, xbc, dtp = (_mm(f"ssd_in{i}", u, wi) for i, wi in enumerate((w_z, w_xbc, w_dt)))
    conv = _ssd_conv_op(xbc, w["ssd_conv_w"].reshape(CONV_W, 1, -1), w["ssd_conv_b"].reshape(1, -1))
    x, bm, cm = conv.fwd()
    dt = dtp[:, :SSD_HEADS].reshape(s, SSD_GROUPS, SSD_HPG).transpose(1, 0, 2)
    core = _ssd_core_op(x, bm, cm, z, dt, grp(w["ssd_dt_bias"]), grp(w["ssd_a_log"]), grp(w["ssd_d"]), w["ssd_g_norm"].reshape(1, -1))
    y, = core.fwd(fwd_comm)
    w_out = bf(w["ssd_w_out"]) if late_w_out is None else late_w_out(core.fwd_comm_out)
    h_out = _mm("ssd_out", y, w_out, add=h)

    def bwd(dh, make_comm=None):
        dy = _mm("ssd_out_dy", dh, w_out, tb=True, out_dtype=BF16)
        d_w_out = _mm("ssd_out_dw", y, dh, ta=True, out_dtype=BF16)
        dx, dbm, dcm, dz, ddt, d_dtb, d_alog, d_d, d_gn = core.bwd([dy], comm=None if make_comm is None else make_comm(d_w_out))
        dxbc, d_cw, d_cb = conv.bwd([dx, dbm, dcm])
        ddtp = _pad_cols(ddt.transpose(1, 0, 2).reshape(s, SSD_HEADS), LANE).astype(BF16)
        du, d_in = _proj_bwd("ssd_in", u, (dz, dxbc, ddtp), (w_z, w_xbc, w_dt))
        dh_in, d_norm = n0.bwd([du], addto={0: dh})
        g_in = jnp.concatenate([d_in[0], d_in[1], d_in[2][:, :SSD_HEADS]], axis=1)
        return dh_in, d_norm, {"ssd_w_in": g_in, "ssd_conv_w": d_cw.reshape(CONV_W, -1), "ssd_conv_b": d_cb.reshape(-1),
                               "ssd_dt_bias": d_dtb.reshape(-1), "ssd_a_log": d_alog.reshape(-1), "ssd_d": d_d.reshape(-1),
                               "ssd_g_norm": d_gn.reshape(-1), "ssd_w_out": d_w_out}, core.bwd_comm_out

    return h_out, bwd


def _rope_tables(positions):
    inv_freq = ROPE_THETA ** (-jnp.arange(0, MLA_ROPE, 2, dtype=F32) / MLA_ROPE)
    ang = positions.astype(F32)[:, None] * inv_freq
    return jnp.cos(ang), jnp.sin(ang)


WEIGHTS = ["norm_g", "final_g", "mla_w_in", "mla_g_q", "mla_w_uq", "mla_g_kv", "mla_w_ukv", "mla_w_out", "gla_w_in", "gla_w_gk2", "gla_b_gk",
           "gla_g_o", "gla_w_out", "lru_w_in", "lru_conv_w", "lru_conv_b", "lru_w_a", "lru_b_a", "lru_w_x", "lru_b_x", "lru_lam", "lru_w_out",
           "ssd_w_in", "ssd_conv_w", "ssd_conv_b", "ssd_dt_bias", "ssd_a_log", "ssd_d", "ssd_g_norm", "ssd_w_out"]
BIG = ["mla_w_in", "mla_w_uq", "mla_w_ukv", "mla_w_out", "gla_w_in", "gla_w_out", "lru_w_in", "lru_w_out", "ssd_w_in", "ssd_w_out"]
SMALL = ["gla_w_gk2", "gla_b_gk", "gla_g_o", "lru_conv_w", "lru_conv_b", "lru_b_a", "lru_b_x", "lru_lam", "ssd_conv_w", "ssd_conv_b", "ssd_g_norm"]
REPL = ["norm_g", "final_g", "mla_g_q", "mla_g_kv", "lru_w_a", "lru_w_x", "ssd_dt_bias", "ssd_a_log", "ssd_d"]
REPL_EARLY = ["lru_w_a", "lru_w_x"]
REPL_LATE = [n for n in REPL if n not in REPL_EARLY]
N_CHIPS, N_DEV = 4, 8
PACK_W = 1024
ADAM_ROWS = 256
SMALL_ROWS = 64


def _shard_axis(name):
    return 0 if name.endswith("_w_out") else -1


def _pack(arrs, dtype, row_mult):
    flat = jnp.concatenate([a.reshape(-1).astype(dtype) for a in arrs])
    per = PACK_W * row_mult
    total = -(-flat.shape[0] // per) * per
    return jnp.pad(flat, (0, total - flat.shape[0])).reshape(-1, PACK_W)


def _unpack(buf, shapes):
    flat = buf.reshape(-1)
    out, off = [], 0
    for s in shapes:
        n = math.prod(s)
        out.append(flat[off:off + n].reshape(s))
        off += n
    return out


def _mesh_pos():
    return lax.axis_index("x"), lax.axis_index("y"), lax.axis_index("c")


class GatherComm:
    def __init__(self, ops):
        self.ops = list(ops)
        n = len(self.ops)
        assert all(o.ndim == 2 and o.shape[0] % 32 == 0 for o in self.ops), [o.shape for o in self.ops]
        self.out_shapes = [jax.ShapeDtypeStruct((N_CHIPS,) + o.shape, o.dtype) for o in self.ops]
        self.sem_shapes = [pltpu.SemaphoreType.DMA((6 * n,)), pltpu.SemaphoreType.DMA((6 * n,)), pltpu.SemaphoreType.DMA((n,))]

    def _copies(self, srcs, dsts, sems):
        send_sems, recv_sems, local_sems = sems
        n = len(self.ops)
        x, y, c = _mesh_pos()
        me_id, sibling = (x, y, c), (x, y, 1 - c)
        chips = [(1 - x, y), (x, 1 - y), (1 - x, 1 - y)]
        mine = 2 * x + y

        def half(i, cc):
            h = self.ops[i].shape[0] // 2
            return pl.ds(cc * h, h)

        def copy(i, k, src, slot, cc, to):
            return pltpu.make_async_remote_copy(src_ref=src, dst_ref=dsts[i].at[slot, half(i, cc)], send_sem=send_sems.at[i * 6 + k],
                                                recv_sem=recv_sems.at[i * 6 + k], device_id=to, device_id_type=pl.DeviceIdType.MESH)

        local = [pltpu.make_async_copy(srcs[i], dsts[i].at[mine], local_sems.at[i]) for i in range(n)]
        first, ici_recvs, passed, sib_recvs = [], [], [], []
        for i in range(n):
            my_half = srcs[i].at[half(i, c)]
            for k, (px, py) in enumerate(chips):
                slot = 2 * px + py
                first.append(copy(i, k, my_half, mine, c, (px, py, c)))
                ici_recvs.append(copy(i, k, my_half, slot, c, me_id))
                passed.append(copy(i, 3 + k, dsts[i].at[slot, half(i, c)], slot, c, sibling))
                sib_recvs.append(copy(i, 3 + k, my_half, slot, 1 - c, me_id))
        return local, first, ici_recvs, passed, sib_recvs

    def start(self, srcs, dsts, sems):
        local, first, _, _, _ = self._copies(srcs, dsts, sems)
        for cp in local + first:
            cp.start()

    def forward(self, srcs, dsts, sems):
        _, _, ici_recvs, passed, _ = self._copies(srcs, dsts, sems)
        for rc, fw in zip(ici_recvs, passed):
            rc.wait_recv()
            fw.start()

    def finish(self, srcs, dsts, sems):
        local, first, _, passed, sib_recvs = self._copies(srcs, dsts, sems)
        for cp in sib_recvs:
            cp.wait_recv()
        for cp in first + passed:
            cp.wait_send()
        for cp in local:
            cp.wait()


class ExchangeComm:
    def __init__(self, chip_ops, all_ops=()):
        self.ops = list(chip_ops) + list(all_ops)
        self.per_chip = (True,) * len(chip_ops) + (False,) * len(all_ops)
        n = len(self.ops)
        self.out_shapes = [jax.ShapeDtypeStruct((N_DEV,) + o.shape[-2:], o.dtype) for o in self.ops]
        self.sem_shapes = [pltpu.SemaphoreType.DMA((7 * n,)), pltpu.SemaphoreType.DMA((7 * n,)), pltpu.SemaphoreType.DMA((n,))]

    def _copies(self, srcs, dsts, sems):
        send_sems, recv_sems, local_sems = sems
        n, per_chip = len(self.ops), self.per_chip
        x, y, c = _mesh_pos()
        me_id, sibling = (x, y, c), (x, y, 1 - c)
        chips = [(1 - x, y), (x, 1 - y), (1 - x, 1 - y)]

        def dev(px, py, pc):
            return 4 * px + 2 * py + pc

        def part(i, px, py):
            return srcs[i].at[2 * px + py] if per_chip[i] else srcs[i]

        def copy(i, k, src, slot, to):
            return pltpu.make_async_remote_copy(src_ref=src, dst_ref=dsts[i].at[slot], send_sem=send_sems.at[i * 7 + k],
                                                recv_sem=recv_sems.at[i * 7 + k], device_id=to, device_id_type=pl.DeviceIdType.MESH)

        me = dev(x, y, c)
        local = [pltpu.make_async_copy(part(i, x, y), dsts[i].at[me], local_sems.at[i]) for i in range(n)]
        first, ici_recvs, passed, sib_recvs = [], [], [], []
        for i in range(n):
            first.append(copy(i, 0, part(i, x, y), me, sibling))
            first += [copy(i, 1 + k, part(i, px, py), me, (px, py, c)) for k, (px, py) in enumerate(chips)]
            sib_recvs.append(copy(i, 0, part(i, x, y), dev(x, y, 1 - c), me_id))
            for k, (px, py) in enumerate(chips):
                slot = dev(px, py, c)
                ici_recvs.append(copy(i, 1 + k, part(i, x, y), slot, me_id))
                passed.append(copy(i, 4 + k, dsts[i].at[slot], slot, sibling))
                sib_recvs.append(copy(i, 4 + k, part(i, x, y), dev(px, py, 1 - c), me_id))
        return local, first, ici_recvs, passed, sib_recvs

    def start(self, srcs, dsts, sems):
        local, first, _, _, _ = self._copies(srcs, dsts, sems)
        for cp in local + first:
            cp.start()

    def forward(self, srcs, dsts, sems):
        _, _, ici_recvs, passed, _ = self._copies(srcs, dsts, sems)
        for rc, fw in zip(ici_recvs, passed):
            rc.wait_recv()
            fw.start()

    def finish(self, srcs, dsts, sems):
        local, first, _, passed, sib_recvs = self._copies(srcs, dsts, sems)
        for cp in sib_recvs:
            cp.wait_recv()
        for cp in first + passed:
            cp.wait_send()
        for cp in local:
            cp.wait()


def _run_comm(name, comm):
    n = len(comm.ops)

    def body(*refs):
        srcs, dsts, sems = refs[:n], refs[n:2 * n], refs[2 * n:]
        comm.start(srcs, dsts, sems)
        comm.forward(srcs, dsts, sems)
        comm.finish(srcs, dsts, sems)

    any_spec = pl.BlockSpec(memory_space=pl.ANY)
    return pl.pallas_call(body, name=name, in_specs=[any_spec] * n, out_specs=[any_spec] * n, out_shape=comm.out_shapes,
                          scratch_shapes=comm.sem_shapes)(*comm.ops)


def _adamw(name, parts, w, m, v, lead=False, comm=None):
    plist = list(parts) if isinstance(parts, (list, tuple)) else [parts]
    n_p = len(plist)
    rows, cols = w.shape[-2:]
    t = next(c for c in (ADAM_ROWS, ADAM_ROWS // 2, SMALL_ROWS) if all(p.shape[1] % c == 0 for p in plist))
    starts = [sum(p.shape[1] for p in plist[:k]) // t for k in range(n_p)]
    counts = [p.shape[1] // t for p in plist]
    assert sum(p.shape[1] for p in plist) == rows, (name, rows)
    c1 = 1.0 - ADAM_B1 ** ADAM_STEP
    c2 = 1.0 - ADAM_B2 ** ADAM_STEP

    n_c = len(comm.ops) if comm is not None else 0
    n_steps = rows // t

    def body(*refs):
        p_refs = refs[:n_p]
        w_ref, m_ref, v_ref = refs[n_p:n_p + 3]
        c_src = refs[n_p + 3:n_p + 3 + n_c]
        g_ref, d_ref, nm_ref, nv_ref = refs[n_p + 3 + n_c:n_p + 7 + n_c]
        cargs = (c_src, refs[n_p + 7 + n_c:n_p + 7 + 2 * n_c], refs[n_p + 7 + 2 * n_c:])
        if comm is not None:
            @pl.when(pl.program_id(0) == 0)
            def _():
                comm.start(*cargs)

            @pl.when(pl.program_id(0) == _forward_step(n_steps))
            def _():
                comm.forward(*cargs)
        g = None
        for k, p_ref in enumerate(p_refs):
            gk = p_ref[0].astype(F32)
            for d in range(1, N_DEV):
                gk = gk + p_ref[d].astype(F32)
            g = gk if g is None else jnp.where(pl.program_id(0) >= starts[k], gk, g)
        nm = ADAM_B1 * m_ref[...] + (1.0 - ADAM_B1) * g
        nv = ADAM_B2 * v_ref[...] + (1.0 - ADAM_B2) * (g * g)
        g_ref[...] = g
        nm_ref[...] = nm
        nv_ref[...] = nv
        d_ref[...] = -ADAM_LR * ((nm / c1) / (jnp.sqrt(nv / c2) + ADAM_EPS) + ADAM_WD * w_ref[...])
        if comm is not None:
            @pl.when(pl.program_id(0) == n_steps - 1)
            def _():
                comm.finish(*cargs)

    row = pl.BlockSpec((None, t, cols), lambda i: (0, i, 0)) if lead else pl.BlockSpec((t, cols), lambda i: (i, 0))
    c_in_specs, c_args, c_out_specs, c_out_shapes, c_sems = _carry_specs(comm)
    res = pl.pallas_call(
        body, name=name, grid=(n_steps,),
        in_specs=[pl.BlockSpec((N_DEV, t, cols), lambda i, lo=lo, n=n: (0, jnp.clip(i - lo, 0, n - 1), 0)) for lo, n in zip(starts, counts)]
        + [row, row, row] + c_in_specs,
        out_specs=[row] * 4 + c_out_specs, out_shape=[jax.ShapeDtypeStruct(w.shape, F32)] * 4 + c_out_shapes, scratch_shapes=c_sems,
        compiler_params=_CP(dimension_semantics=("arbitrary" if comm is not None else "parallel",)),
    )(*plist, w, m, v, *c_args)
    return list(res[:4]), list(res[4:])


def _train_step(x, positions, target, wts, ms, vs, raw):
    small_shapes = [wts[n].shape for n in SMALL]

    big_of = {tag: [n for n in BIG if n.startswith(tag)] for tag in ("mla", "gla", "lru", "ssd")}
    full = {n: wts[n] for n in REPL}

    def gather_comm(names, extra=()):
        return GatherComm([wts[n].astype(BF16) for n in names] + list(extra))

    def assemble(names, got):
        for k, n in enumerate(names):
            full[n] = jnp.concatenate([got[k][j] for j in range(N_CHIPS)], axis=_shard_axis(n))

    first = [n for n in big_of["mla"] if n != "mla_w_out"]
    got = _run_comm("gather_first", gather_comm(first, [_pack([wts[n] for n in SMALL], F32, SMALL_ROWS)]))
    assemble(first, got)
    per_chip_small = [_unpack(got[-1][j], small_shapes) for j in range(N_CHIPS)]
    for k, n in enumerate(SMALL):
        full[n] = jnp.concatenate([per_chip_small[j][k] for j in range(N_CHIPS)], axis=_shard_axis(n))

    cos, sin = _rope_tables(positions)
    ng = full["norm_g"]
    behind_attn = ["mla_w_out"] + big_of["gla"] + big_of["lru"]

    def mla_w_out(got):
        assemble(behind_attn, got)
        return full["mla_w_out"].astype(BF16)

    h1, b0, _ = _mla_layer(x, ng[0], full, cos, sin, fwd_comm=gather_comm(behind_attn), late_w_out=mla_w_out)

    def joined(got_k, axis):
        return jnp.concatenate([got_k[j] for j in range(N_CHIPS)], axis=axis)

    ssd_in = wts["ssd_w_in"].astype(BF16)
    half = ssd_in.shape[0] // 2
    h2, b1, got = _gla_layer(h1, ng[1], full, fwd_comm=GatherComm([ssd_in[:half]]))
    top = joined(got[0], -1)
    h3, b2, got = _lru_layer(h2, ng[2], full, fwd_comm=GatherComm([ssd_in[half:]]))
    full["ssd_w_in"] = jnp.concatenate([top, joined(got[0], -1)], axis=0)
    h4, b3 = _ssd_layer(h3, ng[3], full, fwd_comm=gather_comm(["ssd_w_out"]), late_w_out=lambda got: joined(got[0], 0))
    loss, dh, d_final = _loss_op(h4, target, full["final_g"])
    loss = loss[0, 0]
    grads = {"final_g": d_final.reshape(-1)}
    d_norms = [None] * 4

    def shards_of(n, g):
        return jnp.stack(jnp.split(g.astype(BF16), N_CHIPS, axis=_shard_axis(n)))

    def shards(n):
        return shards_of(n, grads[n])

    parts = {}
    dh, d_norms[3], gw, got = b3(dh, make_comm=lambda dw: ExchangeComm([shards_of("ssd_w_out", dw)]))
    parts["ssd_w_out"] = got[0]
    grads.update(gw)
    ssd_in_g = shards("ssd_w_in")
    half = ssd_in_g.shape[1] // 2
    dh, d_norms[2], gw, got = b2(dh, make_comm=lambda dw: ExchangeComm([ssd_in_g[:, :half], shards_of("lru_w_out", dw)]))
    parts["ssd_w_in"], parts["lru_w_out"] = [got[0]], got[1]
    grads.update(gw)
    dh, d_norms[1], gw, got = b1(dh, make_comm=lambda dw: ExchangeComm([ssd_in_g[:, half:]]))
    parts["ssd_w_in"].append(got[0])
    grads.update(gw)
    repl_early = _pack([grads[n] for n in REPL_EARLY], BF16, SMALL_ROWS)

    behind_attn = ["gla_w_out", "lru_w_in", "gla_w_in"]
    dx, d_norms[0], gw, got = b0(dh, make_comm=lambda dw: ExchangeComm([shards_of("mla_w_out", dw)] + [shards(n) for n in behind_attn],
                                                                        [repl_early]))
    parts.update(zip(["mla_w_out"] + behind_attn, got))
    repl_early_parts = got[-1]
    grads.update(gw)
    grads["norm_g"] = jnp.concatenate(d_norms, axis=0)
    psmall = jnp.stack([_pack([jnp.split(grads[n], N_CHIPS, axis=_shard_axis(n))[j] for n in SMALL], F32, SMALL_ROWS) for j in range(N_CHIPS)])
    prepl = _pack([grads[n] for n in REPL_LATE], F32, SMALL_ROWS)

    out = {}
    kinds = ("grad", "delta", "new_m", "new_v")
    late = [n for n in big_of["mla"] if n != "mla_w_out"]
    late_comm = ExchangeComm([shards(n) for n in late] + [psmall], [prepl])
    for n in ["ssd_w_in"] + [n for n in BIG if n != "ssd_w_in"]:
        if n == "ssd_w_in":
            res, late_parts = _adamw("adam_" + n, parts[n], *(r[n] for r in raw), lead=True, comm=late_comm)
            parts.update(zip(late, late_parts))
        else:
            res, _ = _adamw("adam_" + n, parts[n], *(r[n] for r in raw), lead=True)
        for kind, a in zip(kinds, res):
            out[kind, n] = a
    for tag, names, p in (("adam_small", SMALL, late_parts[-2]), ("adam_repl_early", REPL_EARLY, repl_early_parts),
                          ("adam_repl_late", REPL_LATE, late_parts[-1])):
        shapes = [wts[n].shape for n in names]
        packed = [_pack([d[n] for n in names], F32, SMALL_ROWS) for d in (wts, ms, vs)]
        for kind, buf in zip(kinds, _adamw(tag, p, *packed)[0]):
            for n, a in zip(names, _unpack(buf, shapes)):
                out[kind, n] = a
    loss = lax.psum(loss, ("x", "y", "c"))
    return loss, dx, out


def kernel(x, positions, norm_g, final_g, mla_w_in, mla_g_q, mla_w_uq, mla_g_kv, mla_w_ukv, mla_w_out, gla_w_in, gla_w_gk2, gla_b_gk, gla_g_o, gla_w_out, lru_w_in, lru_conv_w, lru_conv_b, lru_w_a, lru_b_a, lru_w_x, lru_b_x, lru_lam, lru_w_out, ssd_w_in, ssd_conv_w, ssd_conv_b, ssd_dt_bias, ssd_a_log, ssd_d, ssd_g_norm, ssd_w_out, loss_target, m_norm_g, m_final_g, m_mla_w_in, m_mla_g_q, m_mla_w_uq, m_mla_g_kv, m_mla_w_ukv, m_mla_w_out, m_gla_w_in, m_gla_w_gk2, m_gla_b_gk, m_gla_g_o, m_gla_w_out, m_lru_w_in, m_lru_conv_w, m_lru_conv_b, m_lru_w_a, m_lru_b_a, m_lru_w_x, m_lru_b_x, m_lru_lam, m_lru_w_out, m_ssd_w_in, m_ssd_conv_w, m_ssd_conv_b, m_ssd_dt_bias, m_ssd_a_log, m_ssd_d, m_ssd_g_norm, m_ssd_w_out, v_norm_g, v_final_g, v_mla_w_in, v_mla_g_q, v_mla_w_uq, v_mla_g_kv, v_mla_w_ukv, v_mla_w_out, v_gla_w_in, v_gla_w_gk2, v_gla_b_gk, v_gla_g_o, v_gla_w_out, v_lru_w_in, v_lru_conv_w, v_lru_conv_b, v_lru_w_a, v_lru_b_a, v_lru_w_x, v_lru_b_x, v_lru_lam, v_lru_w_out, v_ssd_w_in, v_ssd_conv_w, v_ssd_conv_b, v_ssd_dt_bias, v_ssd_a_log, v_ssd_d, v_ssd_g_norm, v_ssd_w_out):
    given = dict(locals())
    stacked = [n for n in WEIGHTS if n not in ("norm_g", "final_g")]

    def blocks(prefix):
        return {n: (given[prefix + n][0] if n in stacked else given[prefix + n]) for n in WEIGHTS}

    raw = [{n: given[prefix + n] for n in BIG} for prefix in ("", "m_", "v_")]
    loss, dx, out = _train_step(x[0], positions[0], loss_target[0], blocks(""), blocks("m_"), blocks("v_"), raw)
    res = [loss, dx[None]]
    for kind in ("grad", "delta", "new_m", "new_v"):
        res += [(out[kind, n][None] if n in stacked and n not in BIG else out[kind, n]) for n in WEIGHTS]
    return tuple(res)
```

```python
import functools
import math

import jax
import jax.numpy as jnp
from jax import lax
from jax.experimental import pallas as pl
from jax.experimental.pallas import tpu as pltpu

F32 = jnp.float32
BF16 = jnp.bfloat16

V7X_VMEM_BYTES = 64 * 1024 * 1024
VMEM_LIMIT = V7X_VMEM_BYTES - 8 * 1024 * 1024
LANE = 128

D_MODEL = 1024
NORM_EPS = 1e-6
MLA_HEADS, MLA_Q_RANK, MLA_KV_RANK, MLA_NOPE, MLA_ROPE, MLA_V = 16, 384, 256, 64, 32, 64
MLA_QK = MLA_NOPE + MLA_ROPE
ROPE_THETA = 10000.0
GLA_HEADS, GLA_DK, GLA_DV, GLA_RANK, GLA_TAU, GLA_CHUNK = 4, 128, 256, 16, 16.0, 64
LRU_WIDTH, LRU_BLOCKS, LRU_BLOCK, LRU_C, CONV_W = 1280, 10, 128, 8.0, 4
SSD_INNER, SSD_P, SSD_HEADS, SSD_GROUPS, SSD_HPG, SSD_STATE, SSD_CHUNK = 2048, 64, 32, 8, 4, 128, 64
ADAM_LR, ADAM_B1, ADAM_B2, ADAM_EPS, ADAM_WD, ADAM_STEP = 0.001, 0.9, 0.999, 1e-08, 0.01, 10

_CP = functools.partial(pltpu.CompilerParams, vmem_limit_bytes=VMEM_LIMIT)


def _bdot(a, b):
    return jnp.dot(a.astype(BF16), b.astype(BF16), preferred_element_type=F32)


def _bdot_nt(a, b):
    return lax.dot_general(a.astype(BF16), b.astype(BF16), (((1,), (1,)), ((), ())), preferred_element_type=F32)


def _bdot_tn(a, b):
    return lax.dot_general(a.astype(BF16), b.astype(BF16), (((0,), (0,)), ((), ())), preferred_element_type=F32)


def _tri(n):
    r = lax.broadcasted_iota(jnp.int32, (n, n), 0)
    c = lax.broadcasted_iota(jnp.int32, (n, n), 1)
    return r >= c


def _rms(x, g):
    return x * lax.rsqrt(jnp.mean(x * x, axis=-1, keepdims=True) + NORM_EPS) * g


def _silu(x):
    return x * jax.nn.sigmoid(x)


def _shift_rows(x, prev, j):
    if j == 0:
        return x
    t = x.shape[0]

    def fwd_impl(x, prev):
        row = lax.broadcasted_iota(jnp.int32, x.shape, 0)
        return jnp.where(row >= j, pltpu.roll(x, j, 0), pltpu.roll(prev, j, 0))

    @jax.custom_vjp
    def sh(x, prev):
        return fwd_impl(x, prev)

    def sh_fwd(x, prev):
        return fwd_impl(x, prev), None

    def sh_bwd(_, gy):
        row = lax.broadcasted_iota(jnp.int32, gy.shape, 0)
        back = pltpu.roll(gy, t - j, 0)
        return jnp.where(row < t - j, back, 0.0), jnp.where(row >= t - j, back, 0.0)

    sh.defvjp(sh_fwd, sh_bwd)
    return sh(x, prev)


def _cumsum_rows(x):
    zero = jnp.zeros_like(x)
    sh = 1
    while sh < x.shape[0]:
        x = x + _shift_rows(x, zero, sh)
        sh *= 2
    return x


def _one_minus_exp(x):
    series = -x * (1.0 + x * (0.5 + x * (1.0 / 6.0 + x * (1.0 / 24.0 + x * (1.0 / 120.0)))))
    return jnp.where(x > -0.05, series, 1.0 - jnp.exp(x))


def _tile(n, cap):
    if n <= cap:
        return n
    best = None
    for t in range(LANE, cap + 1, LANE):
        if n % t == 0:
            best = t
    assert best is not None, (n, cap)
    return best


MM_BLOCK_BYTES = 8 * 1024 * 1024
MM_ROWS, MM_KROWS = 256, 512
MM_TILE_BYTES = 2 * 1024 * 1024


def _mm_tiles(m, k, n, ta):
    if ta:
        return m, _tile(n, max(LANE, MM_BLOCK_BYTES // (4 * m) // LANE * LANE)), _tile(k, MM_KROWS)
    tn = _tile(n, max(LANE, MM_BLOCK_BYTES // (2 * k) // LANE * LANE))
    rows = min(4 * MM_ROWS, max(MM_ROWS, MM_TILE_BYTES // (4 * tn) // MM_ROWS * MM_ROWS))
    return _tile(m, rows), tn, k


def _mm(name, a, b, *, ta=False, tb=False, add=None, out_dtype=F32):
    m, k = (a.shape[1], a.shape[0]) if ta else a.shape
    n, kb = (b.shape[0], b.shape[1]) if tb else (b.shape[1], b.shape[0])
    assert k == kb, (name, a.shape, b.shape, ta, tb)
    tm, tn, tk = _mm_tiles(m, k, n, ta)
    nk = k // tk
    dn = (((0 if ta else 1,), (1 if tb else 0,)), ((), ()))
    has_add = add is not None

    def finish(refs, r):
        if has_add:
            r = r + refs[2][...].astype(F32)
        return r.astype(out_dtype)

    def body_one(*refs):
        a_ref, b_ref, o_ref = refs[0], refs[1], refs[-1]
        o_ref[...] = finish(refs, lax.dot_general(a_ref[...].astype(BF16), b_ref[...].astype(BF16), dn, preferred_element_type=F32))

    def body_acc(*refs):
        a_ref, b_ref = refs[0], refs[1]
        o_ref, acc = refs[-2], refs[-1]
        kk = pl.program_id(2)

        @pl.when(kk == 0)
        def _():
            acc[...] = jnp.zeros(acc.shape, F32)

        acc[...] += lax.dot_general(a_ref[...].astype(BF16), b_ref[...].astype(BF16), dn, preferred_element_type=F32)

        @pl.when(kk == nk - 1)
        def _():
            o_ref[...] = finish(refs, acc[...])

    a_spec = pl.BlockSpec((tk, tm), lambda i, j, q: (q, i)) if ta else pl.BlockSpec((tm, tk), lambda i, j, q: (i, q))
    b_spec = pl.BlockSpec((tn, tk), lambda i, j, q: (j, q)) if tb else pl.BlockSpec((tk, tn), lambda i, j, q: (q, j))
    o_spec = pl.BlockSpec((tm, tn), lambda i, j, q: (i, j))
    in_specs, args = [a_spec, b_spec], [a, b]
    if has_add:
        in_specs.append(o_spec)
        args.append(add)
    return pl.pallas_call(
        body_one if nk == 1 else body_acc, name=name, grid=(m // tm, n // tn, nk), in_specs=in_specs, out_specs=o_spec,
        out_shape=jax.ShapeDtypeStruct((m, n), out_dtype), scratch_shapes=[] if nk == 1 else [pltpu.VMEM((tm, tn), F32)],
        compiler_params=_CP(dimension_semantics=("parallel", "parallel", "arbitrary")),
    )(*args)


class In:
    def __init__(self, arr, block, imap, kind="x", per_h=False, gdtype=F32, gshape=None, gimap=None, prefixed=False):
        self.arr, self.block, self.imap, self.kind, self.per_h, self.gdtype = arr, tuple(block), imap, kind, per_h, gdtype
        self.prefixed = prefixed
        self.gshape = tuple(gshape) if gshape is not None else tuple(arr.shape)
        self.gimap = gimap if gimap is not None else imap

    def spec(self, rev_g=None):
        imap = self.imap
        if rev_g is None:
            return pl.BlockSpec(self.block, lambda h, g: imap(h, g))
        return pl.BlockSpec(self.block, lambda h, g: imap(h, rev_g - 1 - g))


class Out:
    def __init__(self, shape, dtype, block, imap):
        self.shape, self.dtype, self.block, self.imap = tuple(shape), dtype, tuple(block), imap

    def spec(self, rev_g=None):
        imap = self.imap
        if rev_g is None:
            return pl.BlockSpec(self.block, lambda h, g: imap(h, g))
        return pl.BlockSpec(self.block, lambda h, g: imap(h, rev_g - 1 - g))


def _load_f32(ref, rows=None):
    v = ref[...] if rows is None else ref[0:rows]
    return v.astype(F32) if jnp.issubdtype(v.dtype, jnp.floating) else v


def _state_out(grid, shape):
    nd = len(shape)
    return Out(tuple(grid) + tuple(shape), F32, (None, None) + tuple(shape), lambda h, g: (h, g) + (0,) * nd)


def _carry(comm, grid, refs, n_in, n_out, n_scr):
    n_c = len(comm.ops) if comm is not None else 0
    n_s = len(comm.sem_shapes) if comm is not None else 0
    p = 0
    in_refs = refs[p:p + n_in]; p += n_in
    c_src = refs[p:p + n_c]; p += n_c
    out_refs = refs[p:p + n_out]; p += n_out
    c_dst = refs[p:p + n_c]; p += n_c
    scr = refs[p:p + n_scr]; p += n_scr
    c_sem = refs[p:p + n_s]
    step = pl.program_id(0) * grid[1] + pl.program_id(1)
    n_steps = grid[0] * grid[1]
    when = (step == 0, step == _forward_step(n_steps), step == n_steps - 1)
    return in_refs, out_refs, scr, (c_src, c_dst, c_sem), when


def _forward_step(n_steps):
    return max(0, min(n_steps - 2, (3 * n_steps) // 4))


def _carry_specs(comm):
    if comm is None:
        return [], [], [], [], []
    any_spec = pl.BlockSpec(memory_space=pl.ANY)
    n = len(comm.ops)
    return [any_spec] * n, list(comm.ops), [any_spec] * n, list(comm.out_shapes), list(comm.sem_shapes)


def _op_fwd(name, f, grid, ins, outs, state_shapes=(), comm=None, prefix_rows=None):
    assert prefix_rows is None or not state_shapes
    n_in, n_out, n_st = len(ins), len(outs), len(state_shapes)
    st_outs = [_state_out(grid, s) for s in state_shapes]

    def body(*refs):
        in_refs, o_refs, st_scr, cargs, (first, fwd_step, last) = _carry(comm, grid, refs, n_in, n_out + n_st, n_st)
        out_refs, sv_refs = o_refs[:n_out], o_refs[n_out:]
        if comm is not None:
            @pl.when(first)
            def _():
                comm.start(*cargs)

            @pl.when(fwd_step)
            def _():
                comm.forward(*cargs)
        g = pl.program_id(1)
        if n_st:
            @pl.when(g == 0)
            def _():
                for s in st_scr:
                    s[...] = jnp.zeros(s.shape, F32)

        def compute(rows, g=g):
            vals = [_load_f32(r, rows if i.prefixed else None) for r, i in zip(in_refs, ins)]
            sts = [s[...] for s in st_scr]
            o, ns = f(g, vals, sts)
            for r, v in zip(out_refs, o):
                r[...] = v.astype(r.dtype)
            for r, s in zip(sv_refs, sts):
                r[...] = s
            for s, v in zip(st_scr, ns):
                s[...] = v

        if prefix_rows is None:
            compute(None)
        else:
            per = grid[1] // len(prefix_rows)
            for lv, rows in enumerate(prefix_rows):
                pl.when(g // per == lv)(functools.partial(compute, rows, lv) if per == 1 else functools.partial(compute, rows))
        if comm is not None:
            @pl.when(last)
            def _():
                comm.finish(*cargs)

    all_outs = list(outs) + st_outs
    c_in_specs, c_args, c_out_specs, c_out_shapes, c_sems = _carry_specs(comm)
    res = pl.pallas_call(
        body, name=name, grid=tuple(grid), in_specs=[i.spec() for i in ins] + c_in_specs,
        out_specs=[o.spec() for o in all_outs] + c_out_specs,
        out_shape=[jax.ShapeDtypeStruct(o.shape, o.dtype) for o in all_outs] + c_out_shapes,
        scratch_shapes=[pltpu.VMEM(tuple(s), F32) for s in state_shapes] + c_sems,
        compiler_params=_CP(dimension_semantics=("arbitrary", "arbitrary")),
    )(*[i.arr for i in ins], *c_args)
    n_all = n_out + n_st
    return list(res[:n_out]), list(res[n_out:n_all]), list(res[n_all:])


def _op_bwd(name, f, grid, ins, outs, state_shapes, saved, douts, addto=None, comm=None, prefix_rows=None):
    n_in, n_out, n_st = len(ins), len(outs), len(state_shapes)
    n_g = grid[1]
    assert prefix_rows is None or all(i.prefixed and i.per_h for i in ins if i.kind == "p")
    addto = addto or {}
    diff = [k for k, i in enumerate(ins) if i.kind in ("x", "p")]
    add_idx = sorted(addto)
    st_ins = [In(s, o.block, o.imap, "c") for s, o in zip(saved, [_state_out(grid, s) for s in state_shapes])]
    dout_ins = [In(d, o.block, o.imap, "c") for d, o in zip(douts, outs)]
    add_ins = []
    for k in add_idx:
        i, a = ins[k], addto[k]
        blk = i.block if i.kind == "x" else i.block[:-2] + a.shape[-2:]
        add_ins.append(In(a, blk, i.gimap if i.kind == "x" else i.imap, "c"))
    g_outs = []
    for k in diff:
        i = ins[k]
        g_outs.append(Out(i.gshape, i.gdtype if i.kind == "x" else F32, i.block, i.gimap))

    def body(*refs):
        all_in, go_refs, ds_scr, cargs, (first_step, fwd_step, last_step) = _carry(comm, grid, refs, n_in + n_st + n_out + len(add_idx),
                                                                                    len(diff), n_st)
        if comm is not None:
            @pl.when(first_step)
            def _():
                comm.start(*cargs)

            @pl.when(fwd_step)
            def _():
                comm.forward(*cargs)
        p = 0
        in_refs = all_in[p:p + n_in]; p += n_in
        sv_refs = all_in[p:p + n_st]; p += n_st
        do_refs = all_in[p:p + n_out]; p += n_out
        ad_refs = all_in[p:p + len(add_idx)]
        hh = pl.program_id(0)
        step = pl.program_id(1)
        g = n_g - 1 - step
        if n_st:
            @pl.when(step == 0)
            def _():
                for s in ds_scr:
                    s[...] = jnp.zeros(s.shape, F32)

        def compute(rows, g=g):
            vals = [_load_f32(r, rows if i.prefixed else None) for r, i in zip(in_refs, ins)]
            sts = [r[...] for r in sv_refs]

            def fw(dvals, states):
                full = list(vals)
                for k, v in zip(diff, dvals):
                    full[k] = v
                o, ns = f(g, full, states)
                return list(o), list(ns)

            _, vjp = jax.vjp(fw, [vals[k] for k in diff], sts)
            cts = [r[...].astype(F32) for r in do_refs]
            dns = [s[...] for s in ds_scr]
            dvals, dsts = vjp((cts, dns))
            adds = dict(zip(add_idx, ad_refs))
            for k, r, dv in zip(diff, go_refs, dvals):
                i = ins[k]
                if i.kind == "x":
                    if k in adds:
                        dv = dv + adds[k][...].astype(F32)
                    r[...] = dv.astype(r.dtype)
                elif rows is not None:
                    r[0:rows] += dv
                else:
                    first = (step == 0) if i.per_h else jnp.logical_and(step == 0, hh == 0)

                    @pl.when(first)
                    def _(r=r, dv=dv, k=k):
                        r[...] = dv
                        if k in adds:
                            lead = adds[k].shape[0]
                            r[0:lead] += adds[k][...]

                    @pl.when(jnp.logical_not(first))
                    def _(r=r, dv=dv):
                        r[...] += dv
            for s, v in zip(ds_scr, dsts):
                s[...] = v

        if prefix_rows is None:
            compute(None)
        else:
            @pl.when(step == 0)
            def _():
                for k, r in zip(diff, go_refs):
                    if ins[k].kind == "p":
                        r[...] = jnp.zeros(r.shape, F32)
            per = n_g // len(prefix_rows)
            for lv, rows in enumerate(prefix_rows):
                pl.when(g // per == lv)(functools.partial(compute, rows, lv) if per == 1 else functools.partial(compute, rows))
        if comm is not None:
            @pl.when(last_step)
            def _():
                comm.finish(*cargs)

    all_ins = list(ins) + st_ins + dout_ins + add_ins
    c_in_specs, c_args, c_out_specs, c_out_shapes, c_sems = _carry_specs(comm)
    res = pl.pallas_call(
        body, name=name, grid=tuple(grid), in_specs=[i.spec(n_g) for i in all_ins] + c_in_specs,
        out_specs=[o.spec(n_g) for o in g_outs] + c_out_specs,
        out_shape=[jax.ShapeDtypeStruct(o.shape, o.dtype) for o in g_outs] + c_out_shapes,
        scratch_shapes=[pltpu.VMEM(tuple(s), F32) for s in state_shapes] + c_sems,
        compiler_params=_CP(dimension_semantics=("arbitrary", "arbitrary")),
    )(*[i.arr for i in all_ins], *c_args)
    return list(res[:len(g_outs)]), list(res[len(g_outs):])


class Op:
    def __init__(self, name, f, grid, ins, outs, state_shapes=(), prefix_rows=None):
        self.name, self.f, self.grid, self.ins, self.outs, self.state_shapes = name, f, grid, ins, outs, state_shapes
        self.prefix_rows = prefix_rows
        self.saved = None

    def fwd(self, comm=None):
        res, self.saved, self.fwd_comm_out = _op_fwd(self.name + "_fwd", self.f, self.grid, self.ins, self.outs, self.state_shapes, comm,
                                                     self.prefix_rows)
        return res

    def bwd(self, douts, addto=None, comm=None):
        res, self.bwd_comm_out = _op_bwd(self.name + "_bwd", self.f, self.grid, self.ins, self.outs, self.state_shapes, self.saved, douts,
                                         addto, comm, self.prefix_rows)
        return res


def _rows(arr, t, kind="x", gdtype=F32):
    return In(arr, (t, arr.shape[1]), lambda h, g: (g, 0), kind, gdtype=gdtype)


def _whole(arr, kind="p"):
    nd = arr.ndim
    return In(arr, arr.shape, lambda h, g: (0,) * nd, kind)


def _rows_out(s, n, t, dtype):
    return Out((s, n), dtype, (t, n), lambda h, g: (g, 0))


ROW_T = 256


def _rms_op(name, x, gain, out_dtype=BF16, gdtype=F32):
    s, n = x.shape

    def f(g, vals, sts):
        return [_rms(vals[0], vals[1])], []

    return Op(name, f, (1, s // ROW_T), [_rows(x, ROW_T, gdtype=gdtype), _whole(gain.reshape(1, n))], [_rows_out(s, n, ROW_T, out_dtype)])


def _mla_prep_op(qn, q1, q2, kn, kr, v, cos, sin):
    s = qn.shape[0]
    hd, half = MLA_HEADS, MLA_ROPE // 2

    def f(g, vals, sts):
        qn, q1, q2, kn, kr, v, cos, sin = vals
        cos_h, sin_h = jnp.tile(cos, (1, hd)), jnp.tile(sin, (1, hd))
        r1 = q1 * cos_h - q2 * sin_h
        r2 = q2 * cos_h + q1 * sin_h
        k1, k2 = kr[:, 0:half], kr[:, half:2 * half]
        kr1 = k1 * cos - k2 * sin
        kr2 = k2 * cos + k1 * sin
        zpad = jnp.zeros((qn.shape[0], LANE - MLA_QK), F32)
        qs, ks, vs = [], [], []
        for h in range(hd):
            a, b = h * MLA_NOPE, (h + 1) * MLA_NOPE
            c, d = h * half, (h + 1) * half
            qs.append(jnp.concatenate([qn[:, a:b], r1[:, c:d], r2[:, c:d], zpad], axis=1))
            ks.append(jnp.concatenate([kn[:, a:b], kr1, kr2, zpad], axis=1))
            vs.append(v[:, a:b])
        return [jnp.stack(qs, 0), jnp.stack(ks, 0), jnp.stack(vs, 0)], []

    ins = [_rows(qn, ROW_T, gdtype=BF16), _rows(q1, ROW_T, gdtype=BF16), _rows(q2, ROW_T, gdtype=BF16), _rows(kn, ROW_T, gdtype=BF16),
           _rows(kr, ROW_T, gdtype=BF16), _rows(v, ROW_T, gdtype=BF16), _rows(cos, ROW_T, "c"), _rows(sin, ROW_T, "c")]
    outs = [Out((hd, s, LANE), BF16, (hd, ROW_T, LANE), lambda h, g: (0, g, 0)),
            Out((hd, s, LANE), BF16, (hd, ROW_T, LANE), lambda h, g: (0, g, 0)),
            Out((hd, s, MLA_V), BF16, (hd, ROW_T, MLA_V), lambda h, g: (0, g, 0))]
    return Op("mla_prep", f, (1, s // ROW_T), ins, outs)


ATT_TQ = 256
ATT_LEVELS = 8


def _mla_attn_op(q, k, v):
    hd, s, _ = q.shape
    scale = MLA_QK ** -0.5

    def f(g, vals, sts):
        q, k, v = vals
        sc = _bdot_nt(q, k) * scale
        r = lax.broadcasted_iota(jnp.int32, sc.shape, 0) + g * ATT_TQ
        c = lax.broadcasted_iota(jnp.int32, sc.shape, 1)
        sc = jnp.where(r >= c, sc, -1e30)
        m = lax.stop_gradient(jnp.max(sc, axis=-1, keepdims=True))
        p = jnp.exp(sc - m)
        p = p * (1.0 / jnp.sum(p, axis=-1, keepdims=True))
        return [_bdot(p, v)], []

    ins = [In(q, (None, ATT_TQ, LANE), lambda h, g: (h, g, 0), "x", gdtype=BF16),
           In(k, (None, s, LANE), lambda h, g: (h, 0, 0), "p", per_h=True, prefixed=True),
           In(v, (None, s, MLA_V), lambda h, g: (h, 0, 0), "p", per_h=True, prefixed=True)]
    outs = [Out((hd, s, MLA_V), F32, (None, ATT_TQ, MLA_V), lambda h, g: (h, g, 0))]
    return Op("mla_attn", f, (hd, s // ATT_TQ), ins, outs, prefix_rows=[(lv + 1) * (s // ATT_LEVELS) for lv in range(ATT_LEVELS)])


def _mla_post_op(o, gate):
    hd, s, _ = o.shape

    def f(g, vals, sts):
        o, gate = vals
        cat = jnp.concatenate([o[h] for h in range(hd)], axis=1)
        return [cat * _silu(gate)], []

    ins = [In(o, (hd, ROW_T, MLA_V), lambda h, g: (0, g, 0), "x"), _rows(gate, ROW_T, gdtype=BF16)]
    return Op("mla_post", f, (1, s // ROW_T), ins, [_rows_out(s, hd * MLA_V, ROW_T, BF16)])


def _gla_gate_op(gk, w2, b):
    s = gk.shape[0]

    def f(g, vals, sts):
        gk, w2, b = vals
        return [jax.nn.log_sigmoid(_bdot(gk, w2) + b) / GLA_TAU], []

    ins = [_rows(gk, ROW_T, gdtype=BF16), _whole(w2), _whole(b)]
    return Op("gla_gate", f, (1, s // ROW_T), ins, [_rows_out(s, GLA_HEADS * GLA_DK, ROW_T, F32)])


def _gla_core_op(q, k, v, gate, la, g_o):
    s = q.shape[0]
    c, nh = GLA_CHUNK, GLA_HEADS

    def f(g, vals, sts):
        q, k, v, gate, la, g_o = vals
        tri = _tri(c)
        b = _cumsum_rows(la)
        b_last = jnp.sum(la, axis=0, keepdims=True)
        qt = q * (GLA_DK ** -0.5) * jnp.exp(b)
        kt = k * jnp.exp(-b)
        kd = k * jnp.exp(b_last - b)
        ys, new_sts = [], []
        for h in range(nh):
            ks, vs = slice(h * GLA_DK, (h + 1) * GLA_DK), slice(h * GLA_DV, (h + 1) * GLA_DV)
            att = jnp.where(tri, _bdot_nt(qt[:, ks], kt[:, ks]), 0.0)
            o = _bdot(att, v[:, vs]) + _bdot_nt(qt[:, ks], sts[h])
            new_sts.append(jnp.exp(b_last[:, ks]) * sts[h] + _bdot_tn(v[:, vs], kd[:, ks]))
            ys.append(_rms(o, g_o) * _silu(gate[:, vs]))
        return [jnp.concatenate(ys, axis=1)], new_sts

    ins = [_rows(q, c, gdtype=BF16), _rows(k, c, gdtype=BF16), _rows(v, c, gdtype=BF16), _rows(gate, c, gdtype=BF16), _rows(la, c), _whole(g_o)]
    outs = [_rows_out(s, nh * GLA_DV, c, BF16)]
    return Op("gla_core", f, (1, s // c), ins, outs, [(GLA_DV, GLA_DK)] * nh)


LRU_T = 256


def _lru_op(gate, u, conv_w, conv_b, w_a, b_a, w_x, b_x, lam):
    s, w = u.shape
    t = LRU_T

    def f(g, vals, sts):
        gate, u, cw, cb, w_a, b_a, w_x, b_x, lam = vals
        u_prev, h_prev = sts
        uc = cb
        for kk in range(CONV_W):
            uc = uc + cw[kk] * _shift_rows(u, u_prev, CONV_W - 1 - kk)
        ra, ri = [], []
        for n in range(LRU_BLOCKS):
            blk = uc[:, n * LRU_BLOCK:(n + 1) * LRU_BLOCK]
            ra.append(_bdot(blk, w_a[n]))
            ri.append(_bdot(blk, w_x[n]))
        r = jax.nn.sigmoid(jnp.concatenate(ra, axis=1) + b_a)
        i = jax.nn.sigmoid(jnp.concatenate(ri, axis=1) + b_x)
        log_a = -LRU_C * r * jax.nn.softplus(-lam)
        a = jnp.exp(log_a)
        bb = jnp.sqrt(_one_minus_exp(2.0 * log_a)) * (i * uc)
        zero = jnp.zeros_like(a)
        sh = 1
        while sh < t:
            a_s = _shift_rows(a - 1.0, zero, sh) + 1.0
            b_s = _shift_rows(bb, zero, sh)
            bb = a * b_s + bb
            a = a * a_s
            sh *= 2
        hs = bb + a * h_prev
        last = (lax.broadcasted_iota(jnp.int32, hs.shape, 0) == t - 1).astype(F32)
        h_last = jnp.sum(hs * last, axis=0, keepdims=True)
        return [hs * _silu(gate)], [u, h_last]

    ins = [_rows(gate, t, gdtype=BF16), _rows(u, t, gdtype=BF16), _whole(conv_w), _whole(conv_b), _whole(w_a), _whole(b_a), _whole(w_x),
           _whole(b_x), _whole(lam)]
    return Op("lru_core", f, (1, s // t), ins, [_rows_out(s, w, t, BF16)], [(t, w), (1, w)])


def _ssd_conv_op(xbc, conv_w, conv_b):
    s, w = xbc.shape
    t = ROW_T
    n_x, n_b = SSD_INNER, SSD_GROUPS * SSD_STATE

    def f(g, vals, sts):
        xbc, cw, cb = vals
        acc = cb
        for kk in range(CONV_W):
            acc = acc + cw[kk] * _shift_rows(xbc, sts[0], CONV_W - 1 - kk)
        y = _silu(acc)
        return [y[:, :n_x], y[:, n_x:n_x + n_b], y[:, n_x + n_b:]], [xbc]

    ins = [_rows(xbc, t, gdtype=BF16), _whole(conv_w), _whole(conv_b)]
    outs = [_rows_out(s, n_x, t, F32), _rows_out(s, n_b, t, F32), _rows_out(s, n_b, t, F32)]
    return Op("ssd_conv", f, (1, s // t), ins, outs, [(t, w)])


SSD_L = 512


def _ssd_core_op(x, bm, cm, z, dt, dt_bias, a_log, d_skip, g_norm):
    s = x.shape[0]
    c, hg, p = SSD_L, SSD_HPG, SSD_P
    gw = hg * p

    def f(g, vals, sts):
        x, bm, cm, z, dtr, dt_bias, a_log, d_skip, g_norm = vals
        tri = _tri(c)
        dt = jax.nn.softplus(dtr + dt_bias)
        da = dt * (-jnp.exp(a_log))
        cs = _cumsum_rows(da)
        cs_last = jnp.sum(da, axis=0, keepdims=True)
        cs_t = jnp.transpose(jnp.concatenate([cs, jnp.zeros((c, LANE - hg), F32)], axis=1))
        cb = _bdot_nt(cm, bm)
        ys, new_st = [], []
        for h in range(hg):
            cs_h = cs[:, h:h + 1]
            cs_row = cs_t[h:h + 1, :]
            seg = jnp.where(tri, cs_h - cs_row, 0.0)
            lmat = jnp.where(tri, jnp.exp(seg), 0.0)
            x_h = x[:, h * p:(h + 1) * p]
            xdt = x_h * dt[:, h:h + 1]
            y_diag = _bdot(cb * lmat, xdt)
            decay = jnp.exp(cs_last[:, h:h + 1] - cs_h)
            states = _bdot_tn(xdt * decay, bm)
            y_off = _bdot_nt(cm, sts[h]) * jnp.exp(cs_h)
            new_st.append(jnp.exp(cs_last[:, h:h + 1]) * sts[h] + states)
            ys.append(y_diag + y_off + d_skip[:, h:h + 1] * x_h)
        y = jnp.concatenate(ys, axis=1) * _silu(z)
        return [_rms(y, g_norm)], new_st

    ins = [In(x, (c, gw), lambda h, g: (g, h), "x"), In(bm, (c, SSD_STATE), lambda h, g: (g, h), "x"),
           In(cm, (c, SSD_STATE), lambda h, g: (g, h), "x"), In(z, (c, gw), lambda h, g: (g, h), "x", gdtype=BF16),
           In(dt, (None, c, hg), lambda h, g: (h, g, 0), "x"),
           In(dt_bias, (None, 1, hg), lambda h, g: (h, 0, 0), "p", per_h=True),
           In(a_log, (None, 1, hg), lambda h, g: (h, 0, 0), "p", per_h=True),
           In(d_skip, (None, 1, hg), lambda h, g: (h, 0, 0), "p", per_h=True),
           In(g_norm, (1, gw), lambda h, g: (0, h), "p", per_h=True)]
    outs = [Out((s, SSD_INNER), BF16, (c, gw), lambda h, g: (g, h))]
    return Op("ssd_core", f, (SSD_GROUPS, s // c), ins, outs, [(p, SSD_STATE)] * hg)


def _loss_op(h, target, final_g):
    s, n = h.shape
    t = ROW_T
    n_g = s // t

    def body(h_ref, t_ref, g_ref, loss_ref, dh_ref, dg_ref):
        step = pl.program_id(0)

        def lossf(hv, gv):
            err = _rms(hv, gv) - t_ref[...]
            return 0.5 * jnp.sum(jnp.mean(err * err, axis=-1))

        l, (dh, dg) = jax.value_and_grad(lossf, argnums=(0, 1))(h_ref[...], g_ref[...])
        dh_ref[...] = dh

        @pl.when(step == 0)
        def _():
            loss_ref[...] = jnp.zeros(loss_ref.shape, F32)
            dg_ref[...] = jnp.zeros(dg_ref.shape, F32)

        loss_ref[...] += jnp.full(loss_ref.shape, l, F32)
        dg_ref[...] += dg

    row = pl.BlockSpec((t, n), lambda g: (g, 0))
    one = pl.BlockSpec((1, n), lambda g: (0, 0))
    return pl.pallas_call(
        body, name="loss_head", grid=(n_g,), in_specs=[row, row, one],
        out_specs=[pl.BlockSpec((1, LANE), lambda g: (0, 0)), row, one],
        out_shape=[jax.ShapeDtypeStruct((1, LANE), F32), jax.ShapeDtypeStruct((s, n), F32), jax.ShapeDtypeStruct((1, n), F32)],
        compiler_params=_CP(dimension_semantics=("arbitrary",)),
    )(h, target, final_g.reshape(1, n))


def _pad_cols(w, n):
    return jnp.pad(w, ((0, 0), (0, n - w.shape[1])))


def _pad_rows(w, n):
    return jnp.pad(w, ((0, n - w.shape[0]), (0, 0)))


def _proj_bwd(tag, u, dps, ws):
    du = None
    for i, (dp, w) in enumerate(zip(dps, ws)):
        du = _mm(f"{tag}_du{i}", dp, w, tb=True, add=du)
    dws = [_mm(f"{tag}_dw{i}", u, dp, ta=True, out_dtype=BF16) for i, dp in enumerate(dps)]
    return du, dws


def _mla_layer(h, norm_g, w, cos, sin, fwd_comm=None, late_w_out=None):
    bf = lambda a: a.astype(BF16)
    w_in, w_uq, w_ukv = w["mla_w_in"], w["mla_w_uq"], w["mla_w_ukv"]
    a0, a1, a2 = MLA_Q_RANK, MLA_Q_RANK + MLA_KV_RANK, MLA_Q_RANK + MLA_KV_RANK + MLA_ROPE
    w_cq, w_ckv, w_kr, w_g = bf(w_in[:, :a0]), bf(w_in[:, a0:a1]), bf(_pad_cols(w_in[:, a1:a2], LANE)), bf(w_in[:, a2:])
    uq = w_uq.reshape(MLA_Q_RANK, MLA_HEADS, MLA_QK)
    half = MLA_ROPE // 2
    w_qn = bf(uq[:, :, :MLA_NOPE].reshape(MLA_Q_RANK, -1))
    w_q1 = bf(uq[:, :, MLA_NOPE:MLA_NOPE + half].reshape(MLA_Q_RANK, -1))
    w_q2 = bf(uq[:, :, MLA_NOPE + half:].reshape(MLA_Q_RANK, -1))
    ukv = w_ukv.reshape(MLA_KV_RANK, MLA_HEADS, MLA_NOPE + MLA_V)
    w_kn = bf(ukv[:, :, :MLA_NOPE].reshape(MLA_KV_RANK, -1))
    w_v = bf(ukv[:, :, MLA_NOPE:].reshape(MLA_KV_RANK, -1))

    n0 = _rms_op("mla_norm", h, norm_g)
    u, = n0.fwd()
    cq, ckv, kr, gate = (_mm(f"mla_in{i}", u, wi) for i, wi in enumerate((w_cq, w_ckv, w_kr, w_g)))
    nq = _rms_op("mla_qnorm", cq, w["mla_g_q"], gdtype=BF16)
    nkv = _rms_op("mla_kvnorm", ckv, w["mla_g_kv"], gdtype=BF16)
    qn_, = nq.fwd()
    kvn_, = nkv.fwd()
    qn, q1, q2 = (_mm(f"mla_uq{i}", qn_, wi) for i, wi in enumerate((w_qn, w_q1, w_q2)))
    kn, v = (_mm(f"mla_ukv{i}", kvn_, wi) for i, wi in enumerate((w_kn, w_v)))
    prep = _mla_prep_op(qn, q1, q2, kn, kr, v, cos, sin)
    qh, kh, vh = prep.fwd()
    attn = _mla_attn_op(qh, kh, vh)
    o, = attn.fwd(fwd_comm)
    w_out = bf(w["mla_w_out"]) if late_w_out is None else late_w_out(attn.fwd_comm_out)
    post = _mla_post_op(o, gate)
    y, = post.fwd()
    h_out = _mm("mla_out", y, w_out, add=h)

    def bwd(dh, make_comm=None):
        dy = _mm("mla_out_dy", dh, w_out, tb=True, out_dtype=BF16)
        d_w_out = _mm("mla_out_dw", y, dh, ta=True, out_dtype=BF16)
        do, dgate = post.bwd([dy])
        dqh, dkh, dvh = attn.bwd([do], comm=None if make_comm is None else make_comm(d_w_out))
        dqn, dq1, dq2, dkn, dkr, dv = prep.bwd([dqh, dkh, dvh])
        dqn_, d_uq = _proj_bwd("mla_uq", qn_, (dqn, dq1, dq2), (w_qn, w_q1, w_q2))
        dkvn_, d_ukv = _proj_bwd("mla_ukv", kvn_, (dkn, dv), (w_kn, w_v))
        dcq, d_g_q = nq.bwd([dqn_])
        dckv, d_g_kv = nkv.bwd([dkvn_])
        du, d_in = _proj_bwd("mla_in", u, (dcq, dckv, dkr, dgate), (w_cq, w_ckv, w_kr, w_g))
        dh_in, d_norm = n0.bwd([du], addto={0: dh})
        shp = (MLA_Q_RANK, MLA_HEADS, -1)
        g_uq = jnp.concatenate([d_uq[0].reshape(shp), d_uq[1].reshape(shp), d_uq[2].reshape(shp)], axis=2).reshape(MLA_Q_RANK, -1)
        shp = (MLA_KV_RANK, MLA_HEADS, -1)
        g_ukv = jnp.concatenate([d_ukv[0].reshape(shp), d_ukv[1].reshape(shp)], axis=2).reshape(MLA_KV_RANK, -1)
        g_in = jnp.concatenate([d_in[0], d_in[1], d_in[2][:, :MLA_ROPE], d_in[3]], axis=1)
        return dh_in, d_norm, {"mla_w_in": g_in, "mla_g_q": d_g_q.reshape(-1), "mla_w_uq": g_uq, "mla_g_kv": d_g_kv.reshape(-1),
                               "mla_w_ukv": g_ukv, "mla_w_out": d_w_out}, attn.bwd_comm_out

    return h_out, bwd, attn.fwd_comm_out


def _gla_layer(h, norm_g, w, fwd_comm=None):
    bf = lambda a: a.astype(BF16)
    w_in = w["gla_w_in"]
    nk, nv = GLA_HEADS * GLA_DK, GLA_HEADS * GLA_DV
    cuts = (0, nk, 2 * nk, 2 * nk + nv, 2 * nk + 2 * nv)
    w_q, w_k, w_v, w_g = (bf(w_in[:, cuts[i]:cuts[i + 1]]) for i in range(4))
    w_gk = bf(_pad_cols(w_in[:, cuts[4]:], LANE))
    w2 = _pad_rows(w["gla_w_gk2"], LANE)
    b_gk = w["gla_b_gk"].reshape(1, -1)
    g_o = w["gla_g_o"].reshape(1, -1)
    w_out = bf(w["gla_w_out"])

    n0 = _rms_op("gla_norm", h, norm_g)
    u, = n0.fwd()
    q, k, v, gate, gk = (_mm(f"gla_in{i}", u, wi) for i, wi in enumerate((w_q, w_k, w_v, w_g, w_gk)))
    gop = _gla_gate_op(gk, w2, b_gk)
    la, = gop.fwd()
    core = _gla_core_op(q, k, v, gate, la, g_o)
    y, = core.fwd(fwd_comm)
    h_out = _mm("gla_out", y, w_out, add=h)

    def bwd(dh, make_comm=None):
        dy = _mm("gla_out_dy", dh, w_out, tb=True, out_dtype=BF16)
        d_w_out = _mm("gla_out_dw", y, dh, ta=True, out_dtype=BF16)
        dq, dk, dv, dgate, dla, d_g_o = core.bwd([dy], comm=None if make_comm is None else make_comm(d_w_out))
        dgk, d_w2, d_b = gop.bwd([dla])
        du, d_in = _proj_bwd("gla_in", u, (dq, dk, dv, dgate, dgk), (w_q, w_k, w_v, w_g, w_gk))
        dh_in, d_norm = n0.bwd([du], addto={0: dh})
        g_in = jnp.concatenate([d_in[0], d_in[1], d_in[2], d_in[3], d_in[4][:, :GLA_RANK]], axis=1)
        return dh_in, d_norm, {"gla_w_in": g_in, "gla_w_gk2": d_w2[:GLA_RANK], "gla_b_gk": d_b.reshape(-1), "gla_g_o": d_g_o.reshape(-1),
                               "gla_w_out": d_w_out}, core.bwd_comm_out

    return h_out, bwd, core.fwd_comm_out


def _lru_layer(h, norm_g, w, fwd_comm=None):
    bf = lambda a: a.astype(BF16)
    w_in = w["lru_w_in"]
    w_g, w_u = bf(w_in[:, :LRU_WIDTH]), bf(w_in[:, LRU_WIDTH:])
    row = lambda a: a.reshape(1, -1)
    w_out = bf(w["lru_w_out"])

    n0 = _rms_op("lru_norm", h, norm_g)
    u_, = n0.fwd()
    gate, u = (_mm(f"lru_in{i}", u_, wi) for i, wi in enumerate((w_g, w_u)))
    core = _lru_op(gate, u, w["lru_conv_w"].reshape(CONV_W, 1, -1), row(w["lru_conv_b"]), w["lru_w_a"], row(w["lru_b_a"]), w["lru_w_x"],
                   row(w["lru_b_x"]), row(w["lru_lam"]))
    y, = core.fwd(fwd_comm)
    h_out = _mm("lru_out", y, w_out, add=h)

    def bwd(dh, make_comm=None):
        dy = _mm("lru_out_dy", dh, w_out, tb=True, out_dtype=BF16)
        d_w_out = _mm("lru_out_dw", y, dh, ta=True, out_dtype=BF16)
        dgate, du, d_cw, d_cb, d_wa, d_ba, d_wx, d_bx, d_lam = core.bwd([dy], comm=None if make_comm is None else make_comm(d_w_out))
        du_, d_in = _proj_bwd("lru_in", u_, (dgate, du), (w_g, w_u))
        dh_in, d_norm = n0.bwd([du_], addto={0: dh})
        return dh_in, d_norm, {"lru_w_in": jnp.concatenate(d_in, axis=1), "lru_conv_w": d_cw.reshape(CONV_W, -1), "lru_conv_b": d_cb.reshape(-1),
                               "lru_w_a": d_wa, "lru_b_a": d_ba.reshape(-1), "lru_w_x": d_wx, "lru_b_x": d_bx.reshape(-1),
                               "lru_lam": d_lam.reshape(-1), "lru_w_out": d_w_out}, core.bwd_comm_out

    return h_out, bwd, core.fwd_comm_out


def _ssd_layer(h, norm_g, w, fwd_comm=None, late_w_out=None):
    bf = lambda a: a.astype(BF16)
    s = h.shape[0]
    w_in = w["ssd_w_in"]
    conv_dim = SSD_INNER + 2 * SSD_GROUPS * SSD_STATE
    w_z, w_xbc = bf(w_in[:, :SSD_INNER]), bf(w_in[:, SSD_INNER:SSD_INNER + conv_dim])
    w_dt = bf(_pad_cols(w_in[:, SSD_INNER + conv_dim:], LANE))
    grp = lambda a: a.reshape(SSD_GROUPS, 1, SSD_HPG)

    n0 = _rms_op("ssd_norm", h, norm_g)
    u, = n0.fwd()
    z, xbc, dtp = (_mm(f"ssd_in{i}", u, wi) for i, wi in enumerate((w_z, w_xbc, w_dt)))
    conv = _ssd_conv_op(xbc, w["ssd_conv_w"].reshape(CONV_W, 1, -1), w["ssd_conv_b"].reshape(1, -1))
    x, bm, cm = conv.fwd()
    dt = dtp[:, :SSD_HEADS].reshape(s, SSD_GROUPS, SSD_HPG).transpose(1, 0, 2)
    core = _ssd_core_op(x, bm, cm, z, dt, grp(w["ssd_dt_bias"]), grp(w["ssd_a_log"]), grp(w["ssd_d"]), w["ssd_g_norm"].reshape(1, -1))
    y, = core.fwd(fwd_comm)
    w_out = bf(w["ssd_w_out"]) if late_w_out is None else late_w_out(core.fwd_comm_out)
    h_out = _mm("ssd_out", y, w_out, add=h)

    def bwd(dh, make_comm=None):
        dy = _mm("ssd_out_dy", dh, w_out, tb=True, out_dtype=BF16)
        d_w_out = _mm("ssd_out_dw", y, dh, ta=True, out_dtype=BF16)
        dx, dbm, dcm, dz, ddt, d_dtb, d_alog, d_d, d_gn = core.bwd([dy], comm=None if make_comm is None else make_comm(d_w_out))
        dxbc, d_cw, d_cb = conv.bwd([dx, dbm, dcm])
        ddtp = _pad_cols(ddt.transpose(1, 0, 2).reshape(s, SSD_HEADS), LANE).astype(BF16)
        du, d_in = _proj_bwd("ssd_in", u, (dz, dxbc, ddtp), (w_z, w_xbc, w_dt))
        dh_in, d_norm = n0.bwd([du], addto={0: dh})
        g_in = jnp.concatenate([d_in[0], d_in[1], d_in[2][:, :SSD_HEADS]], axis=1)
        return dh_in, d_norm, {"ssd_w_in": g_in, "ssd_conv_w": d_cw.reshape(CONV_W, -1), "ssd_conv_b": d_cb.reshape(-1),
                               "ssd_dt_bias": d_dtb.reshape(-1), "ssd_a_log": d_alog.reshape(-1), "ssd_d": d_d.reshape(-1),
                               "ssd_g_norm": d_gn.reshape(-1), "ssd_w_out": d_w_out}, core.bwd_comm_out

    return h_out, bwd


def _rope_tables(positions):
    inv_freq = ROPE_THETA ** (-jnp.arange(0, MLA_ROPE, 2, dtype=F32) / MLA_ROPE)
    ang = positions.astype(F32)[:, None] * inv_freq
    return jnp.cos(ang), jnp.sin(ang)


WEIGHTS = ["norm_g", "final_g", "mla_w_in", "mla_g_q", "mla_w_uq", "mla_g_kv", "mla_w_ukv", "mla_w_out", "gla_w_in", "gla_w_gk2", "gla_b_gk",
           "gla_g_o", "gla_w_out", "lru_w_in", "lru_conv_w", "lru_conv_b", "lru_w_a", "lru_b_a", "lru_w_x", "lru_b_x", "lru_lam", "lru_w_out",
           "ssd_w_in", "ssd_conv_w", "ssd_conv_b", "ssd_dt_bias", "ssd_a_log", "ssd_d", "ssd_g_norm", "ssd_w_out"]
BIG = ["mla_w_in", "mla_w_uq", "mla_w_ukv", "mla_w_out", "gla_w_in", "gla_w_out", "lru_w_in", "lru_w_out", "ssd_w_in", "ssd_w_out"]
SMALL = ["gla_w_gk2", "gla_b_gk", "gla_g_o", "lru_conv_w", "lru_conv_b", "lru_b_a", "lru_b_x", "lru_lam", "ssd_conv_w", "ssd_conv_b", "ssd_g_norm"]
REPL = ["norm_g", "final_g", "mla_g_q", "mla_g_kv", "lru_w_a", "lru_w_x", "ssd_dt_bias", "ssd_a_log", "ssd_d"]
REPL_EARLY = ["lru_w_a", "lru_w_x"]
REPL_LATE = [n for n in REPL if n not in REPL_EARLY]
N_CHIPS, N_DEV = 4, 8
PACK_W = 1024
ADAM_ROWS = 256
SMALL_ROWS = 64


def _shard_axis(name):
    return 0 if name.endswith("_w_out") else -1


def _pack(arrs, dtype, row_mult):
    flat = jnp.concatenate([a.reshape(-1).astype(dtype) for a in arrs])
    per = PACK_W * row_mult
    total = -(-flat.shape[0] // per) * per
    return jnp.pad(flat, (0, total - flat.shape[0])).reshape(-1, PACK_W)


def _unpack(buf, shapes):
    flat = buf.reshape(-1)
    out, off = [], 0
    for s in shapes:
        n = math.prod(s)
        out.append(flat[off:off + n].reshape(s))
        off += n
    return out


def _mesh_pos():
    return lax.axis_index("x"), lax.axis_index("y"), lax.axis_index("c")


class GatherComm:
    def __init__(self, ops):
        self.ops = list(ops)
        n = len(self.ops)
        assert all(o.ndim == 2 and o.shape[0] % 32 == 0 for o in self.ops), [o.shape for o in self.ops]
        self.out_shapes = [jax.ShapeDtypeStruct((N_CHIPS,) + o.shape, o.dtype) for o in self.ops]
        self.sem_shapes = [pltpu.SemaphoreType.DMA((6 * n,)), pltpu.SemaphoreType.DMA((6 * n,)), pltpu.SemaphoreType.DMA((n,))]

    def _copies(self, srcs, dsts, sems):
        send_sems, recv_sems, local_sems = sems
        n = len(self.ops)
        x, y, c = _mesh_pos()
        me_id, sibling = (x, y, c), (x, y, 1 - c)
        chips = [(1 - x, y), (x, 1 - y), (1 - x, 1 - y)]
        mine = 2 * x + y

        def half(i, cc):
            h = self.ops[i].shape[0] // 2
            return pl.ds(cc * h, h)

        def copy(i, k, src, slot, cc, to):
            return pltpu.make_async_remote_copy(src_ref=src, dst_ref=dsts[i].at[slot, half(i, cc)], send_sem=send_sems.at[i * 6 + k],
                                                recv_sem=recv_sems.at[i * 6 + k], device_id=to, device_id_type=pl.DeviceIdType.MESH)

        local = [pltpu.make_async_copy(srcs[i], dsts[i].at[mine], local_sems.at[i]) for i in range(n)]
        first, ici_recvs, passed, sib_recvs = [], [], [], []
        for i in range(n):
            my_half = srcs[i].at[half(i, c)]
            for k, (px, py) in enumerate(chips):
                slot = 2 * px + py
                first.append(copy(i, k, my_half, mine, c, (px, py, c)))
                ici_recvs.append(copy(i, k, my_half, slot, c, me_id))
                passed.append(copy(i, 3 + k, dsts[i].at[slot, half(i, c)], slot, c, sibling))
                sib_recvs.append(copy(i, 3 + k, my_half, slot, 1 - c, me_id))
        return local, first, ici_recvs, passed, sib_recvs

    def start(self, srcs, dsts, sems):
        local, first, _, _, _ = self._copies(srcs, dsts, sems)
        for cp in local + first:
            cp.start()

    def forward(self, srcs, dsts, sems):
        _, _, ici_recvs, passed, _ = self._copies(srcs, dsts, sems)
        for rc, fw in zip(ici_recvs, passed):
            rc.wait_recv()
            fw.start()

    def finish(self, srcs, dsts, sems):
        local, first, _, passed, sib_recvs = self._copies(srcs, dsts, sems)
        for cp in sib_recvs:
            cp.wait_recv()
        for cp in first + passed:
            cp.wait_send()
        for cp in local:
            cp.wait()


class ExchangeComm:
    def __init__(self, chip_ops, all_ops=()):
        self.ops = list(chip_ops) + list(all_ops)
        self.per_chip = (True,) * len(chip_ops) + (False,) * len(all_ops)
        n = len(self.ops)
        self.out_shapes = [jax.ShapeDtypeStruct((N_DEV,) + o.shape[-2:], o.dtype) for o in self.ops]
        self.sem_shapes = [pltpu.SemaphoreType.DMA((7 * n,)), pltpu.SemaphoreType.DMA((7 * n,)), pltpu.SemaphoreType.DMA((n,))]

    def _copies(self, srcs, dsts, sems):
        send_sems, recv_sems, local_sems = sems
        n, per_chip = len(self.ops), self.per_chip
        x, y, c = _mesh_pos()
        me_id, sibling = (x, y, c), (x, y, 1 - c)
        chips = [(1 - x, y), (x, 1 - y), (1 - x, 1 - y)]

        def dev(px, py, pc):
            return 4 * px + 2 * py + pc

        def part(i, px, py):
            return srcs[i].at[2 * px + py] if per_chip[i] else srcs[i]

        def copy(i, k, src, slot, to):
            return pltpu.make_async_remote_copy(src_ref=src, dst_ref=dsts[i].at[slot], send_sem=send_sems.at[i * 7 + k],
                                                recv_sem=recv_sems.at[i * 7 + k], device_id=to, device_id_type=pl.DeviceIdType.MESH)

        me = dev(x, y, c)
        local = [pltpu.make_async_copy(part(i, x, y), dsts[i].at[me], local_sems.at[i]) for i in range(n)]
        first, ici_recvs, passed, sib_recvs = [], [], [], []
        for i in range(n):
            first.append(copy(i, 0, part(i, x, y), me, sibling))
            first += [copy(i, 1 + k, part(i, px, py), me, (px, py, c)) for k, (px, py) in enumerate(chips)]
            sib_recvs.append(copy(i, 0, part(i, x, y), dev(x, y, 1 - c), me_id))
            for k, (px, py) in enumerate(chips):
                slot = dev(px, py, c)
                ici_recvs.append(copy(i, 1 + k, part(i, x, y), slot, me_id))
                passed.append(copy(i, 4 + k, dsts[i].at[slot], slot, sibling))
                sib_recvs.append(copy(i, 4 + k, part(i, x, y), dev(px, py, 1 - c), me_id))
        return local, first, ici_recvs, passed, sib_recvs

    def start(self, srcs, dsts, sems):
        local, first, _, _, _ = self._copies(srcs, dsts, sems)
        for cp in local + first:
            cp.start()

    def forward(self, srcs, dsts, sems):
        _, _, ici_recvs, passed, _ = self._copies(srcs, dsts, sems)
        for rc, fw in zip(ici_recvs, passed):
            rc.wait_recv()
            fw.start()

    def finish(self, srcs, dsts, sems):
        local, first, _, passed, sib_recvs = self._copies(srcs, dsts, sems)
        for cp in sib_recvs:
            cp.wait_recv()
        for cp in first + passed:
            cp.wait_send()
        for cp in local:
            cp.wait()


def _run_comm(name, comm):
    n = len(comm.ops)

    def body(*refs):
        srcs, dsts, sems = refs[:n], refs[n:2 * n], refs[2 * n:]
        comm.start(srcs, dsts, sems)
        comm.forward(srcs, dsts, sems)
        comm.finish(srcs, dsts, sems)

    any_spec = pl.BlockSpec(memory_space=pl.ANY)
    return pl.pallas_call(body, name=name, in_specs=[any_spec] * n, out_specs=[any_spec] * n, out_shape=comm.out_shapes,
                          scratch_shapes=comm.sem_shapes)(*comm.ops)


def _adamw(name, parts, w, m, v, lead=False, comm=None):
    plist = list(parts) if isinstance(parts, (list, tuple)) else [parts]
    n_p = len(plist)
    rows, cols = w.shape[-2:]
    t = next(c for c in (ADAM_ROWS, ADAM_ROWS // 2, SMALL_ROWS) if all(p.shape[1] % c == 0 for p in plist))
    starts = [sum(p.shape[1] for p in plist[:k]) // t for k in range(n_p)]
    counts = [p.shape[1] // t for p in plist]
    assert sum(p.shape[1] for p in plist) == rows, (name, rows)
    c1 = 1.0 - ADAM_B1 ** ADAM_STEP
    c2 = 1.0 - ADAM_B2 ** ADAM_STEP

    n_c = len(comm.ops) if comm is not None else 0
    n_steps = rows // t

    def body(*refs):
        p_refs = refs[:n_p]
        w_ref, m_ref, v_ref = refs[n_p:n_p + 3]
        c_src = refs[n_p + 3:n_p + 3 + n_c]
        g_ref, d_ref, nm_ref, nv_ref = refs[n_p + 3 + n_c:n_p + 7 + n_c]
        cargs = (c_src, refs[n_p + 7 + n_c:n_p + 7 + 2 * n_c], refs[n_p + 7 + 2 * n_c:])
        if comm is not None:
            @pl.when(pl.program_id(0) == 0)
            def _():
                comm.start(*cargs)

            @pl.when(pl.program_id(0) == _forward_step(n_steps))
            def _():
                comm.forward(*cargs)
        g = None
        for k, p_ref in enumerate(p_refs):
            gk = p_ref[0].astype(F32)
            for d in range(1, N_DEV):
                gk = gk + p_ref[d].astype(F32)
            g = gk if g is None else jnp.where(pl.program_id(0) >= starts[k], gk, g)
        nm = ADAM_B1 * m_ref[...] + (1.0 - ADAM_B1) * g
        nv = ADAM_B2 * v_ref[...] + (1.0 - ADAM_B2) * (g * g)
        g_ref[...] = g
        nm_ref[...] = nm
        nv_ref[...] = nv
        d_ref[...] = -ADAM_LR * ((nm / c1) / (jnp.sqrt(nv / c2) + ADAM_EPS) + ADAM_WD * w_ref[...])
        if comm is not None:
            @pl.when(pl.program_id(0) == n_steps - 1)
            def _():
                comm.finish(*cargs)

    row = pl.BlockSpec((None, t, cols), lambda i: (0, i, 0)) if lead else pl.BlockSpec((t, cols), lambda i: (i, 0))
    c_in_specs, c_args, c_out_specs, c_out_shapes, c_sems = _carry_specs(comm)
    res = pl.pallas_call(
        body, name=name, grid=(n_steps,),
        in_specs=[pl.BlockSpec((N_DEV, t, cols), lambda i, lo=lo, n=n: (0, jnp.clip(i - lo, 0, n - 1), 0)) for lo, n in zip(starts, counts)]
        + [row, row, row] + c_in_specs,
        out_specs=[row] * 4 + c_out_specs, out_shape=[jax.ShapeDtypeStruct(w.shape, F32)] * 4 + c_out_shapes, scratch_shapes=c_sems,
        compiler_params=_CP(dimension_semantics=("arbitrary" if comm is not None else "parallel",)),
    )(*plist, w, m, v, *c_args)
    return list(res[:4]), list(res[4:])


def _train_step(x, positions, target, wts, ms, vs, raw):
    small_shapes = [wts[n].shape for n in SMALL]

    big_of = {tag: [n for n in BIG if n.startswith(tag)] for tag in ("mla", "gla", "lru", "ssd")}
    full = {n: wts[n] for n in REPL}

    def gather_comm(names, extra=()):
        return GatherComm([wts[n].astype(BF16) for n in names] + list(extra))

    def assemble(names, got):
        for k, n in enumerate(names):
            full[n] = jnp.concatenate([got[k][j] for j in range(N_CHIPS)], axis=_shard_axis(n))

    first = [n for n in big_of["mla"] if n != "mla_w_out"]
    got = _run_comm("gather_first", gather_comm(first, [_pack([wts[n] for n in SMALL], F32, SMALL_ROWS)]))
    assemble(first, got)
    per_chip_small = [_unpack(got[-1][j], small_shapes) for j in range(N_CHIPS)]
    for k, n in enumerate(SMALL):
        full[n] = jnp.concatenate([per_chip_small[j][k] for j in range(N_CHIPS)], axis=_shard_axis(n))

    cos, sin = _rope_tables(positions)
    ng = full["norm_g"]
    behind_attn = ["mla_w_out"] + big_of["gla"] + big_of["lru"]

    def mla_w_out(got):
        assemble(behind_attn, got)
        return full["mla_w_out"].astype(BF16)

    h1, b0, _ = _mla_layer(x, ng[0], full, cos, sin, fwd_comm=gather_comm(behind_attn), late_w_out=mla_w_out)

    def joined(got_k, axis):
        return jnp.concatenate([got_k[j] for j in range(N_CHIPS)], axis=axis)

    ssd_in = wts["ssd_w_in"].astype(BF16)
    half = ssd_in.shape[0] // 2
    h2, b1, got = _gla_layer(h1, ng[1], full, fwd_comm=GatherComm([ssd_in[:half]]))
    top = joined(got[0], -1)
    h3, b2, got = _lru_layer(h2, ng[2], full, fwd_comm=GatherComm([ssd_in[half:]]))
    full["ssd_w_in"] = jnp.concatenate([top, joined(got[0], -1)], axis=0)
    h4, b3 = _ssd_layer(h3, ng[3], full, fwd_comm=gather_comm(["ssd_w_out"]), late_w_out=lambda got: joined(got[0], 0))
    loss, dh, d_final = _loss_op(h4, target, full["final_g"])
    loss = loss[0, 0]
    grads = {"final_g": d_final.reshape(-1)}
    d_norms = [None] * 4

    def shards_of(n, g):
        return jnp.stack(jnp.split(g.astype(BF16), N_CHIPS, axis=_shard_axis(n)))

    def shards(n):
        return shards_of(n, grads[n])

    parts = {}
    dh, d_norms[3], gw, got = b3(dh, make_comm=lambda dw: ExchangeComm([shards_of("ssd_w_out", dw)]))
    parts["ssd_w_out"] = got[0]
    grads.update(gw)
    ssd_in_g = shards("ssd_w_in")
    half = ssd_in_g.shape[1] // 2
    dh, d_norms[2], gw, got = b2(dh, make_comm=lambda dw: ExchangeComm([ssd_in_g[:, :half], shards_of("lru_w_out", dw)]))
    parts["ssd_w_in"], parts["lru_w_out"] = [got[0]], got[1]
    grads.update(gw)
    dh, d_norms[1], gw, got = b1(dh, make_comm=lambda dw: ExchangeComm([ssd_in_g[:, half:]]))
    parts["ssd_w_in"].append(got[0])
    grads.update(gw)
    repl_early = _pack([grads[n] for n in REPL_EARLY], BF16, SMALL_ROWS)

    behind_attn = ["gla_w_out", "lru_w_in", "gla_w_in"]
    dx, d_norms[0], gw, got = b0(dh, make_comm=lambda dw: ExchangeComm([shards_of("mla_w_out", dw)] + [shards(n) for n in behind_attn],
                                                                        [repl_early]))
    parts.update(zip(["mla_w_out"] + behind_attn, got))
    repl_early_parts = got[-1]
    grads.update(gw)
    grads["norm_g"] = jnp.concatenate(d_norms, axis=0)
    psmall = jnp.stack([_pack([jnp.split(grads[n], N_CHIPS, axis=_shard_axis(n))[j] for n in SMALL], F32, SMALL_ROWS) for j in range(N_CHIPS)])
    prepl = _pack([grads[n] for n in REPL_LATE], F32, SMALL_ROWS)

    out = {}
    kinds = ("grad", "delta", "new_m", "new_v")
    late = [n for n in big_of["mla"] if n != "mla_w_out"]
    late_comm = ExchangeComm([shards(n) for n in late] + [psmall], [prepl])
    for n in ["ssd_w_in"] + [n for n in BIG if n != "ssd_w_in"]:
        if n == "ssd_w_in":
            res, late_parts = _adamw("adam_" + n, parts[n], *(r[n] for r in raw), lead=True, comm=late_comm)
            parts.update(zip(late, late_parts))
        else:
            res, _ = _adamw("adam_" + n, parts[n], *(r[n] for r in raw), lead=True)
        for kind, a in zip(kinds, res):
            out[kind, n] = a
    for tag, names, p in (("adam_small", SMALL, late_parts[-2]), ("adam_repl_early", REPL_EARLY, repl_early_parts),
                          ("adam_repl_late", REPL_LATE, late_parts[-1])):
        shapes = [wts[n].shape for n in names]
        packed = [_pack([d[n] for n in names], F32, SMALL_ROWS) for d in (wts, ms, vs)]
        for kind, buf in zip(kinds, _adamw(tag, p, *packed)[0]):
            for n, a in zip(names, _unpack(buf, shapes)):
                out[kind, n] = a
    loss = lax.psum(loss, ("x", "y", "c"))
    return loss, dx, out


def kernel(x, positions, norm_g, final_g, mla_w_in, mla_g_q, mla_w_uq, mla_g_kv, mla_w_ukv, mla_w_out, gla_w_in, gla_w_gk2, gla_b_gk, gla_g_o, gla_w_out, lru_w_in, lru_conv_w, lru_conv_b, lru_w_a, lru_b_a, lru_w_x, lru_b_x, lru_lam, lru_w_out, ssd_w_in, ssd_conv_w, ssd_conv_b, ssd_dt_bias, ssd_a_log, ssd_d, ssd_g_norm, ssd_w_out, loss_target, m_norm_g, m_final_g, m_mla_w_in, m_mla_g_q, m_mla_w_uq, m_mla_g_kv, m_mla_w_ukv, m_mla_w_out, m_gla_w_in, m_gla_w_gk2, m_gla_b_gk, m_gla_g_o, m_gla_w_out, m_lru_w_in, m_lru_conv_w, m_lru_conv_b, m_lru_w_a, m_lru_b_a, m_lru_w_x, m_lru_b_x, m_lru_lam, m_lru_w_out, m_ssd_w_in, m_ssd_conv_w, m_ssd_conv_b, m_ssd_dt_bias, m_ssd_a_log, m_ssd_d, m_ssd_g_norm, m_ssd_w_out, v_norm_g, v_final_g, v_mla_w_in, v_mla_g_q, v_mla_w_uq, v_mla_g_kv, v_mla_w_ukv, v_mla_w_out, v_gla_w_in, v_gla_w_gk2, v_gla_b_gk, v_gla_g_o, v_gla_w_out, v_lru_w_in, v_lru_conv_w, v_lru_conv_b, v_lru_w_a, v_lru_b_a, v_lru_w_x, v_lru_b_x, v_lru_lam, v_lru_w_out, v_ssd_w_in, v_ssd_conv_w, v_ssd_conv_b, v_ssd_dt_bias, v_ssd_a_log, v_ssd_d, v_ssd_g_norm, v_ssd_w_out):
    given = dict(locals())
    stacked = [n for n in WEIGHTS if n not in ("norm_g", "final_g")]

    def blocks(prefix):
        return {n: (given[prefix + n][0] if n in stacked else given[prefix + n]) for n in WEIGHTS}

    raw = [{n: given[prefix + n] for n in BIG} for prefix in ("", "m_", "v_")]
    loss, dx, out = _train_step(x[0], positions[0], loss_target[0], blocks(""), blocks("m_"), blocks("v_"), raw)
    res = [loss, dx[None]]
    for kind in ("grad", "delta", "new_m", "new_v"):
        res += [(out[kind, n][None] if n in stacked and n not in BIG else out[kind, n]) for n in WEIGHTS]
    return tuple(res)
```

```python
import functools
import math

import jax
import jax.numpy as jnp
from jax import lax
from jax.experimental import pallas as pl
from jax.experimental.pallas import tpu as pltpu

F32 = jnp.float32
BF16 = jnp.bfloat16

V7X_VMEM_BYTES = 64 * 1024 * 1024
VMEM_LIMIT = V7X_VMEM_BYTES - 8 * 1024 * 1024
LANE = 128

D_MODEL = 1024
NORM_EPS = 1e-6
MLA_HEADS, MLA_Q_RANK, MLA_KV_RANK, MLA_NOPE, MLA_ROPE, MLA_V = 16, 384, 256, 64, 32, 64
MLA_QK = MLA_NOPE + MLA_ROPE
ROPE_THETA = 10000.0
GLA_HEADS, GLA_DK, GLA_DV, GLA_RANK, GLA_TAU, GLA_CHUNK = 4, 128, 256, 16, 16.0, 64
LRU_WIDTH, LRU_BLOCKS, LRU_BLOCK, LRU_C, CONV_W = 1280, 10, 128, 8.0, 4
SSD_INNER, SSD_P, SSD_HEADS, SSD_GROUPS, SSD_HPG, SSD_STATE, SSD_CHUNK = 2048, 64, 32, 8, 4, 128, 64
ADAM_LR, ADAM_B1, ADAM_B2, ADAM_EPS, ADAM_WD, ADAM_STEP = 0.001, 0.9, 0.999, 1e-08, 0.01, 10

_CP = functools.partial(pltpu.CompilerParams, vmem_limit_bytes=VMEM_LIMIT)


def _bdot(a, b):
    return jnp.dot(a.astype(BF16), b.astype(BF16), preferred_element_type=F32)


def _bdot_nt(a, b):
    return lax.dot_general(a.astype(BF16), b.astype(BF16), (((1,), (1,)), ((), ())), preferred_element_type=F32)


def _bdot_tn(a, b):
    return lax.dot_general(a.astype(BF16), b.astype(BF16), (((0,), (0,)), ((), ())), preferred_element_type=F32)


def _tri(n):
    r = lax.broadcasted_iota(jnp.int32, (n, n), 0)
    c = lax.broadcasted_iota(jnp.int32, (n, n), 1)
    return r >= c


def _rms(x, g):
    return x * lax.rsqrt(jnp.mean(x * x, axis=-1, keepdims=True) + NORM_EPS) * g


def _silu(x):
    return x * jax.nn.sigmoid(x)


def _shift_rows(x, prev, j):
    if j == 0:
        return x
    t = x.shape[0]

    def fwd_impl(x, prev):
        row = lax.broadcasted_iota(jnp.int32, x.shape, 0)
        return jnp.where(row >= j, pltpu.roll(x, j, 0), pltpu.roll(prev, j, 0))

    @jax.custom_vjp
    def sh(x, prev):
        return fwd_impl(x, prev)

    def sh_fwd(x, prev):
        return fwd_impl(x, prev), None

    def sh_bwd(_, gy):
        row = lax.broadcasted_iota(jnp.int32, gy.shape, 0)
        back = pltpu.roll(gy, t - j, 0)
        return jnp.where(row < t - j, back, 0.0), jnp.where(row >= t - j, back, 0.0)

    sh.defvjp(sh_fwd, sh_bwd)
    return sh(x, prev)


def _cumsum_rows(x):
    zero = jnp.zeros_like(x)
    sh = 1
    while sh < x.shape[0]:
        x = x + _shift_rows(x, zero, sh)
        sh *= 2
    return x


def _one_minus_exp(x):
    series = -x * (1.0 + x * (0.5 + x * (1.0 / 6.0 + x * (1.0 / 24.0 + x * (1.0 / 120.0)))))
    return jnp.where(x > -0.05, series, 1.0 - jnp.exp(x))


def _tile(n, cap):
    if n <= cap:
        return n
    best = None
    for t in range(LANE, cap + 1, LANE):
        if n % t == 0:
            best = t
    assert best is not None, (n, cap)
    return best


MM_BLOCK_BYTES = 8 * 1024 * 1024
MM_ROWS, MM_KROWS = 256, 512
MM_TILE_BYTES = 2 * 1024 * 1024


def _mm_tiles(m, k, n, ta):
    if ta:
        return m, _tile(n, max(LANE, MM_BLOCK_BYTES // (4 * m) // LANE * LANE)), _tile(k, MM_KROWS)
    tn = _tile(n, max(LANE, MM_BLOCK_BYTES // (2 * k) // LANE * LANE))
    rows = min(4 * MM_ROWS, max(MM_ROWS, MM_TILE_BYTES // (4 * tn) // MM_ROWS * MM_ROWS))
    return _tile(m, rows), tn, k


def _mm(name, a, b, *, ta=False, tb=False, add=None, out_dtype=F32):
    m, k = (a.shape[1], a.shape[0]) if ta else a.shape
    n, kb = (b.shape[0], b.shape[1]) if tb else (b.shape[1], b.shape[0])
    assert k == kb, (name, a.shape, b.shape, ta, tb)
    tm, tn, tk = _mm_tiles(m, k, n, ta)
    nk = k // tk
    dn = (((0 if ta else 1,), (1 if tb else 0,)), ((), ()))
    has_add = add is not None

    def finish(refs, r):
        if has_add:
            r = r + refs[2][...].astype(F32)
        return r.astype(out_dtype)

    def body_one(*refs):
        a_ref, b_ref, o_ref = refs[0], refs[1], refs[-1]
        o_ref[...] = finish(refs, lax.dot_general(a_ref[...].astype(BF16), b_ref[...].astype(BF16), dn, preferred_element_type=F32))

    def body_acc(*refs):
        a_ref, b_ref = refs[0], refs[1]
        o_ref, acc = refs[-2], refs[-1]
        kk = pl.program_id(2)

        @pl.when(kk == 0)
        def _():
            acc[...] = jnp.zeros(acc.shape, F32)

        acc[...] += lax.dot_general(a_ref[...].astype(BF16), b_ref[...].astype(BF16), dn, preferred_element_type=F32)

        @pl.when(kk == nk - 1)
        def _():
            o_ref[...] = finish(refs, acc[...])

    a_spec = pl.BlockSpec((tk, tm), lambda i, j, q: (q, i)) if ta else pl.BlockSpec((tm, tk), lambda i, j, q: (i, q))
    b_spec = pl.BlockSpec((tn, tk), lambda i, j, q: (j, q)) if tb else pl.BlockSpec((tk, tn), lambda i, j, q: (q, j))
    o_spec = pl.BlockSpec((tm, tn), lambda i, j, q: (i, j))
    in_specs, args = [a_spec, b_spec], [a, b]
    if has_add:
        in_specs.append(o_spec)
        args.append(add)
    return pl.pallas_call(
        body_one if nk == 1 else body_acc, name=name, grid=(m // tm, n // tn, nk), in_specs=in_specs, out_specs=o_spec,
        out_shape=jax.ShapeDtypeStruct((m, n), out_dtype), scratch_shapes=[] if nk == 1 else [pltpu.VMEM((tm, tn), F32)],
        compiler_params=_CP(dimension_semantics=("parallel", "parallel", "arbitrary")),
    )(*args)


def _mm_pieces(name, a, bs):
    m, k = a.shape
    n_b = len(bs)
    tm = _tile(m, MM_ROWS)

    def body(*refs):
        av = refs[0][...].astype(BF16)
        for b_ref, o_ref in zip(refs[1:1 + n_b], refs[1 + n_b:]):
            o_ref[...] = jnp.dot(av, b_ref[...].astype(BF16), preferred_element_type=F32)

    return pl.pallas_call(
        body, name=name, grid=(m // tm,),
        in_specs=[pl.BlockSpec((tm, k), lambda i: (i, 0))] + [pl.BlockSpec(b.shape, lambda i: (0, 0)) for b in bs],
        out_specs=[pl.BlockSpec((tm, b.shape[1]), lambda i: (i, 0)) for b in bs],
        out_shape=[jax.ShapeDtypeStruct((m, b.shape[1]), F32) for b in bs],
        compiler_params=_CP(dimension_semantics=("parallel",)),
    )(a, *bs)


def _mm_pieces_t(name, dps, ws):
    m, k = dps[0].shape[0], ws[0].shape[0]
    n_b = len(ws)
    tm = _tile(m, MM_ROWS)
    dn = (((1,), (1,)), ((), ()))

    def body(*refs):
        acc = None
        for d_ref, w_ref in zip(refs[:n_b], refs[n_b:2 * n_b]):
            part = lax.dot_general(d_ref[...].astype(BF16), w_ref[...].astype(BF16), dn, preferred_element_type=F32)
            acc = part if acc is None else acc + part
        refs[-1][...] = acc

    return pl.pallas_call(
        body, name=name, grid=(m // tm,),
        in_specs=[pl.BlockSpec((tm, d.shape[1]), lambda i: (i, 0)) for d in dps] + [pl.BlockSpec(w.shape, lambda i: (0, 0)) for w in ws],
        out_specs=pl.BlockSpec((tm, k), lambda i: (i, 0)), out_shape=jax.ShapeDtypeStruct((m, k), F32),
        compiler_params=_CP(dimension_semantics=("parallel",)),
    )(*dps, *ws)


class In:
    def __init__(self, arr, block, imap, kind="x", per_h=False, gdtype=F32, gshape=None, gimap=None, prefixed=False):
        self.arr, self.block, self.imap, self.kind, self.per_h, self.gdtype = arr, tuple(block), imap, kind, per_h, gdtype
        self.prefixed = prefixed
        self.gshape = tuple(gshape) if gshape is not None else tuple(arr.shape)
        self.gimap = gimap if gimap is not None else imap

    def spec(self, rev_g=None):
        imap = self.imap
        if rev_g is None:
            return pl.BlockSpec(self.block, lambda h, g: imap(h, g))
        return pl.BlockSpec(self.block, lambda h, g: imap(h, rev_g - 1 - g))


class Out:
    def __init__(self, shape, dtype, block, imap):
        self.shape, self.dtype, self.block, self.imap = tuple(shape), dtype, tuple(block), imap

    def spec(self, rev_g=None):
        imap = self.imap
        if rev_g is None:
            return pl.BlockSpec(self.block, lambda h, g: imap(h, g))
        return pl.BlockSpec(self.block, lambda h, g: imap(h, rev_g - 1 - g))


def _load_f32(ref, rows=None):
    v = ref[...] if rows is None else ref[0:rows]
    return v.astype(F32) if jnp.issubdtype(v.dtype, jnp.floating) else v


def _state_out(grid, shape):
    nd = len(shape)
    return Out(tuple(grid) + tuple(shape), F32, (None, None) + tuple(shape), lambda h, g: (h, g) + (0,) * nd)


def _carry(comm, grid, refs, n_in, n_out, n_scr):
    n_c = len(comm.ops) if comm is not None else 0
    n_s = len(comm.sem_shapes) if comm is not None else 0
    p = 0
    in_refs = refs[p:p + n_in]; p += n_in
    c_src = refs[p:p + n_c]; p += n_c
    out_refs = refs[p:p + n_out]; p += n_out
    c_dst = refs[p:p + n_c]; p += n_c
    scr = refs[p:p + n_scr]; p += n_scr
    c_sem = refs[p:p + n_s]
    step = pl.program_id(0) * grid[1] + pl.program_id(1)
    n_steps = grid[0] * grid[1]
    when = (step == 0, step == _forward_step(n_steps), step == n_steps - 1)
    return in_refs, out_refs, scr, (c_src, c_dst, c_sem), when


def _forward_step(n_steps):
    return max(0, min(n_steps - 2, (3 * n_steps) // 4))


def _carry_specs(comm):
    if comm is None:
        return [], [], [], [], []
    any_spec = pl.BlockSpec(memory_space=pl.ANY)
    n = len(comm.ops)
    return [any_spec] * n, list(comm.ops), [any_spec] * n, list(comm.out_shapes), list(comm.sem_shapes)


def _op_fwd(name, f, grid, ins, outs, state_shapes=(), comm=None, prefix_rows=None):
    assert prefix_rows is None or not state_shapes
    n_in, n_out, n_st = len(ins), len(outs), len(state_shapes)
    st_outs = [_state_out(grid, s) for s in state_shapes]

    def body(*refs):
        in_refs, o_refs, st_scr, cargs, (first, fwd_step, last) = _carry(comm, grid, refs, n_in, n_out + n_st, n_st)
        out_refs, sv_refs = o_refs[:n_out], o_refs[n_out:]
        if comm is not None:
            @pl.when(first)
            def _():
                comm.start(*cargs)

            @pl.when(fwd_step)
            def _():
                comm.forward(*cargs)
        g = pl.program_id(1)
        if n_st:
            @pl.when(g == 0)
            def _():
                for s in st_scr:
                    s[...] = jnp.zeros(s.shape, F32)

        def compute(rows, g=g):
            vals = [_load_f32(r, rows if i.prefixed else None) for r, i in zip(in_refs, ins)]
            sts = [s[...] for s in st_scr]
            o, ns = f(g, vals, sts)
            for r, v in zip(out_refs, o):
                r[...] = v.astype(r.dtype)
            for r, s in zip(sv_refs, sts):
                r[...] = s
            for s, v in zip(st_scr, ns):
                s[...] = v

        if prefix_rows is None:
            compute(None)
        else:
            per = grid[1] // len(prefix_rows)
            for lv, rows in enumerate(prefix_rows):
                pl.when(g // per == lv)(functools.partial(compute, rows, lv) if per == 1 else functools.partial(compute, rows))
        if comm is not None:
            @pl.when(last)
            def _():
                comm.finish(*cargs)

    all_outs = list(outs) + st_outs
    c_in_specs, c_args, c_out_specs, c_out_shapes, c_sems = _carry_specs(comm)
    res = pl.pallas_call(
        body, name=name, grid=tuple(grid), in_specs=[i.spec() for i in ins] + c_in_specs,
        out_specs=[o.spec() for o in all_outs] + c_out_specs,
        out_shape=[jax.ShapeDtypeStruct(o.shape, o.dtype) for o in all_outs] + c_out_shapes,
        scratch_shapes=[pltpu.VMEM(tuple(s), F32) for s in state_shapes] + c_sems,
        compiler_params=_CP(dimension_semantics=("arbitrary", "arbitrary")),
    )(*[i.arr for i in ins], *c_args)
    n_all = n_out + n_st
    return list(res[:n_out]), list(res[n_out:n_all]), list(res[n_all:])


def _op_bwd(name, f, grid, ins, outs, state_shapes, saved, douts, addto=None, comm=None, prefix_rows=None):
    n_in, n_out, n_st = len(ins), len(outs), len(state_shapes)
    n_g = grid[1]
    assert prefix_rows is None or all(i.prefixed and i.per_h for i in ins if i.kind == "p")
    addto = addto or {}
    diff = [k for k, i in enumerate(ins) if i.kind in ("x", "p")]
    add_idx = sorted(addto)
    st_ins = [In(s, o.block, o.imap, "c") for s, o in zip(saved, [_state_out(grid, s) for s in state_shapes])]
    dout_ins = [In(d, o.block, o.imap, "c") for d, o in zip(douts, outs)]
    add_ins = []
    for k in add_idx:
        i, a = ins[k], addto[k]
        blk = i.block if i.kind == "x" else i.block[:-2] + a.shape[-2:]
        add_ins.append(In(a, blk, i.gimap if i.kind == "x" else i.imap, "c"))
    g_outs = []
    for k in diff:
        i = ins[k]
        g_outs.append(Out(i.gshape, i.gdtype if i.kind == "x" else F32, i.block, i.gimap))

    def body(*refs):
        all_in, go_refs, ds_scr, cargs, (first_step, fwd_step, last_step) = _carry(comm, grid, refs, n_in + n_st + n_out + len(add_idx),
                                                                                    len(diff), n_st)
        if comm is not None:
            @pl.when(first_step)
            def _():
                comm.start(*cargs)

            @pl.when(fwd_step)
            def _():
                comm.forward(*cargs)
        p = 0
        in_refs = all_in[p:p + n_in]; p += n_in
        sv_refs = all_in[p:p + n_st]; p += n_st
        do_refs = all_in[p:p + n_out]; p += n_out
        ad_refs = all_in[p:p + len(add_idx)]
        hh = pl.program_id(0)
        step = pl.program_id(1)
        g = n_g - 1 - step
        if n_st:
            @pl.when(step == 0)
            def _():
                for s in ds_scr:
                    s[...] = jnp.zeros(s.shape, F32)

        def compute(rows, g=g):
            vals = [_load_f32(r, rows if i.prefixed else None) for r, i in zip(in_refs, ins)]
            sts = [r[...] for r in sv_refs]

            def fw(dvals, states):
                full = list(vals)
                for k, v in zip(diff, dvals):
                    full[k] = v
                o, ns = f(g, full, states)
                return list(o), list(ns)

            _, vjp = jax.vjp(fw, [vals[k] for k in diff], sts)
            cts = [r[...].astype(F32) for r in do_refs]
            dns = [s[...] for s in ds_scr]
            dvals, dsts = vjp((cts, dns))
            adds = dict(zip(add_idx, ad_refs))
            for k, r, dv in zip(diff, go_refs, dvals):
                i = ins[k]
                if i.kind == "x":
                    if k in adds:
                        dv = dv + adds[k][...].astype(F32)
                    r[...] = dv.astype(r.dtype)
                elif rows is not None:
                    r[0:rows] += dv
                else:
                    first = (step == 0) if i.per_h else jnp.logical_and(step == 0, hh == 0)

                    @pl.when(first)
                    def _(r=r, dv=dv, k=k):
                        r[...] = dv
                        if k in adds:
                            lead = adds[k].shape[0]
                            r[0:lead] += adds[k][...]

                    @pl.when(jnp.logical_not(first))
                    def _(r=r, dv=dv):
                        r[...] += dv
            for s, v in zip(ds_scr, dsts):
                s[...] = v

        if prefix_rows is None:
            compute(None)
        else:
            @pl.when(step == 0)
            def _():
                for k, r in zip(diff, go_refs):
                    if ins[k].kind == "p":
                        r[...] = jnp.zeros(r.shape, F32)
            per = n_g // len(prefix_rows)
            for lv, rows in enumerate(prefix_rows):
                pl.when(g // per == lv)(functools.partial(compute, rows, lv) if per == 1 else functools.partial(compute, rows))
        if comm is not None:
            @pl.when(last_step)
            def _():
                comm.finish(*cargs)

    all_ins = list(ins) + st_ins + dout_ins + add_ins
    c_in_specs, c_args, c_out_specs, c_out_shapes, c_sems = _carry_specs(comm)
    res = pl.pallas_call(
        body, name=name, grid=tuple(grid), in_specs=[i.spec(n_g) for i in all_ins] + c_in_specs,
        out_specs=[o.spec(n_g) for o in g_outs] + c_out_specs,
        out_shape=[jax.ShapeDtypeStruct(o.shape, o.dtype) for o in g_outs] + c_out_shapes,
        scratch_shapes=[pltpu.VMEM(tuple(s), F32) for s in state_shapes] + c_sems,
        compiler_params=_CP(dimension_semantics=("arbitrary", "arbitrary")),
    )(*[i.arr for i in all_ins], *c_args)
    return list(res[:len(g_outs)]), list(res[len(g_outs):])


class Op:
    def __init__(self, name, f, grid, ins, outs, state_shapes=(), prefix_rows=None):
        self.name, self.f, self.grid, self.ins, self.outs, self.state_shapes = name, f, grid, ins, outs, state_shapes
        self.prefix_rows = prefix_rows
        self.saved = None

    def fwd(self, comm=None):
        res, self.saved, self.fwd_comm_out = _op_fwd(self.name + "_fwd", self.f, self.grid, self.ins, self.outs, self.state_shapes, comm,
                                                     self.prefix_rows)
        return res

    def bwd(self, douts, addto=None, comm=None):
        res, self.bwd_comm_out = _op_bwd(self.name + "_bwd", self.f, self.grid, self.ins, self.outs, self.state_shapes, self.saved, douts,
                                         addto, comm, self.prefix_rows)
        return res


def _rows(arr, t, kind="x", gdtype=F32):
    return In(arr, (t, arr.shape[1]), lambda h, g: (g, 0), kind, gdtype=gdtype)


def _whole(arr, kind="p"):
    nd = arr.ndim
    return In(arr, arr.shape, lambda h, g: (0,) * nd, kind)


def _rows_out(s, n, t, dtype):
    return Out((s, n), dtype, (t, n), lambda h, g: (g, 0))


ROW_T = 256


def _rms_op(name, x, gain, out_dtype=BF16, gdtype=F32):
    s, n = x.shape

    def f(g, vals, sts):
        return [_rms(vals[0], vals[1])], []

    return Op(name, f, (1, s // ROW_T), [_rows(x, ROW_T, gdtype=gdtype), _whole(gain.reshape(1, n))], [_rows_out(s, n, ROW_T, out_dtype)])


def _mla_prep_op(qn, q1, q2, kn, kr, v, cos, sin):
    s = qn.shape[0]
    hd, half = MLA_HEADS, MLA_ROPE // 2

    def f(g, vals, sts):
        qn, q1, q2, kn, kr, v, cos, sin = vals
        cos_h, sin_h = jnp.tile(cos, (1, hd)), jnp.tile(sin, (1, hd))
        r1 = q1 * cos_h - q2 * sin_h
        r2 = q2 * cos_h + q1 * sin_h
        k1, k2 = kr[:, 0:half], kr[:, half:2 * half]
        kr1 = k1 * cos - k2 * sin
        kr2 = k2 * cos + k1 * sin
        zpad = jnp.zeros((qn.shape[0], LANE - MLA_QK), F32)
        qs, ks, vs = [], [], []
        for h in range(hd):
            a, b = h * MLA_NOPE, (h + 1) * MLA_NOPE
            c, d = h * half, (h + 1) * half
            qs.append(jnp.concatenate([qn[:, a:b], r1[:, c:d], r2[:, c:d], zpad], axis=1))
            ks.append(jnp.concatenate([kn[:, a:b], kr1, kr2, zpad], axis=1))
            vs.append(v[:, a:b])
        return [jnp.stack(qs, 0), jnp.stack(ks, 0), jnp.stack(vs, 0)], []

    ins = [_rows(qn, ROW_T, gdtype=BF16), _rows(q1, ROW_T, gdtype=BF16), _rows(q2, ROW_T, gdtype=BF16), _rows(kn, ROW_T, gdtype=BF16),
           _rows(kr, ROW_T, gdtype=BF16), _rows(v, ROW_T, gdtype=BF16), _rows(cos, ROW_T, "c"), _rows(sin, ROW_T, "c")]
    outs = [Out((hd, s, LANE), BF16, (hd, ROW_T, LANE), lambda h, g: (0, g, 0)),
            Out((hd, s, LANE), BF16, (hd, ROW_T, LANE), lambda h, g: (0, g, 0)),
            Out((hd, s, MLA_V), BF16, (hd, ROW_T, MLA_V), lambda h, g: (0, g, 0))]
    return Op("mla_prep", f, (1, s // ROW_T), ins, outs)


ATT_TQ = 256
ATT_LEVELS = 8


def _mla_attn_op(q, k, v):
    hd, s, _ = q.shape
    scale = MLA_QK ** -0.5

    def f(g, vals, sts):
        q, k, v = vals
        sc = _bdot_nt(q, k) * scale
        r = lax.broadcasted_iota(jnp.int32, sc.shape, 0) + g * ATT_TQ
        c = lax.broadcasted_iota(jnp.int32, sc.shape, 1)
        sc = jnp.where(r >= c, sc, -1e30)
        m = lax.stop_gradient(jnp.max(sc, axis=-1, keepdims=True))
        p = jnp.exp(sc - m)
        p = p * (1.0 / jnp.sum(p, axis=-1, keepdims=True))
        return [_bdot(p, v)], []

    ins = [In(q, (None, ATT_TQ, LANE), lambda h, g: (h, g, 0), "x", gdtype=BF16),
           In(k, (None, s, LANE), lambda h, g: (h, 0, 0), "p", per_h=True, prefixed=True),
           In(v, (None, s, MLA_V), lambda h, g: (h, 0, 0), "p", per_h=True, prefixed=True)]
    outs = [Out((hd, s, MLA_V), F32, (None, ATT_TQ, MLA_V), lambda h, g: (h, g, 0))]
    return Op("mla_attn", f, (hd, s // ATT_TQ), ins, outs, prefix_rows=[(lv + 1) * (s // ATT_LEVELS) for lv in range(ATT_LEVELS)])


def _mla_post_op(o, gate):
    hd, s, _ = o.shape

    def f(g, vals, sts):
        o, gate = vals
        cat = jnp.concatenate([o[h] for h in range(hd)], axis=1)
        return [cat * _silu(gate)], []

    ins = [In(o, (hd, ROW_T, MLA_V), lambda h, g: (0, g, 0), "x"), _rows(gate, ROW_T, gdtype=BF16)]
    return Op("mla_post", f, (1, s // ROW_T), ins, [_rows_out(s, hd * MLA_V, ROW_T, BF16)])


def _gla_gate_op(gk, w2, b):
    s = gk.shape[0]

    def f(g, vals, sts):
        gk, w2, b = vals
        return [jax.nn.log_sigmoid(_bdot(gk, w2) + b) / GLA_TAU], []

    ins = [_rows(gk, ROW_T, gdtype=BF16), _whole(w2), _whole(b)]
    return Op("gla_gate", f, (1, s // ROW_T), ins, [_rows_out(s, GLA_HEADS * GLA_DK, ROW_T, F32)])


def _gla_core_op(q, k, v, gate, la, g_o):
    s = q.shape[0]
    c, nh = GLA_CHUNK, GLA_HEADS

    def f(g, vals, sts):
        q, k, v, gate, la, g_o = vals
        tri = _tri(c)
        b = _cumsum_rows(la)
        b_last = jnp.sum(la, axis=0, keepdims=True)
        qt = q * (GLA_DK ** -0.5) * jnp.exp(b)
        kt = k * jnp.exp(-b)
        kd = k * jnp.exp(b_last - b)
        ys, new_sts = [], []
        for h in range(nh):
            ks, vs = slice(h * GLA_DK, (h + 1) * GLA_DK), slice(h * GLA_DV, (h + 1) * GLA_DV)
            att = jnp.where(tri, _bdot_nt(qt[:, ks], kt[:, ks]), 0.0)
            o = _bdot(att, v[:, vs]) + _bdot_nt(qt[:, ks], sts[h])
            new_sts.append(jnp.exp(b_last[:, ks]) * sts[h] + _bdot_tn(v[:, vs], kd[:, ks]))
            ys.append(_rms(o, g_o) * _silu(gate[:, vs]))
        return [jnp.concatenate(ys, axis=1)], new_sts

    ins = [_rows(q, c, gdtype=BF16), _rows(k, c, gdtype=BF16), _rows(v, c, gdtype=BF16), _rows(gate, c, gdtype=BF16), _rows(la, c), _whole(g_o)]
    outs = [_rows_out(s, nh * GLA_DV, c, BF16)]
    return Op("gla_core", f, (1, s // c), ins, outs, [(GLA_DV, GLA_DK)] * nh)


LRU_T = 256


def _lru_op(gate, u, conv_w, conv_b, w_a, b_a, w_x, b_x, lam):
    s, w = u.shape
    t = LRU_T

    def f(g, vals, sts):
        gate, u, cw, cb, w_a, b_a, w_x, b_x, lam = vals
        u_prev, h_prev = sts
        uc = cb
        for kk in range(CONV_W):
            uc = uc + cw[kk] * _shift_rows(u, u_prev, CONV_W - 1 - kk)
        ra, ri = [], []
        for n in range(LRU_BLOCKS):
            blk = uc[:, n * LRU_BLOCK:(n + 1) * LRU_BLOCK]
            ra.append(_bdot(blk, w_a[n]))
            ri.append(_bdot(blk, w_x[n]))
        r = jax.nn.sigmoid(jnp.concatenate(ra, axis=1) + b_a)
        i = jax.nn.sigmoid(jnp.concatenate(ri, axis=1) + b_x)
        log_a = -LRU_C * r * jax.nn.softplus(-lam)
        a = jnp.exp(log_a)
        bb = jnp.sqrt(_one_minus_exp(2.0 * log_a)) * (i * uc)
        zero = jnp.zeros_like(a)
        sh = 1
        while sh < t:
            a_s = _shift_rows(a - 1.0, zero, sh) + 1.0
            b_s = _shift_rows(bb, zero, sh)
            bb = a * b_s + bb
            a = a * a_s
            sh *= 2
        hs = bb + a * h_prev
        last = (lax.broadcasted_iota(jnp.int32, hs.shape, 0) == t - 1).astype(F32)
        h_last = jnp.sum(hs * last, axis=0, keepdims=True)
        return [hs * _silu(gate)], [u, h_last]

    ins = [_rows(gate, t, gdtype=BF16), _rows(u, t, gdtype=BF16), _whole(conv_w), _whole(conv_b), _whole(w_a), _whole(b_a), _whole(w_x),
           _whole(b_x), _whole(lam)]
    return Op("lru_core", f, (1, s // t), ins, [_rows_out(s, w, t, BF16)], [(t, w), (1, w)])


def _ssd_conv_op(xbc, conv_w, conv_b):
    s, w = xbc.shape
    t = ROW_T
    n_x, n_b = SSD_INNER, SSD_GROUPS * SSD_STATE

    def f(g, vals, sts):
        xbc, cw, cb = vals
        acc = cb
        for kk in range(CONV_W):
            acc = acc + cw[kk] * _shift_rows(xbc, sts[0], CONV_W - 1 - kk)
        y = _silu(acc)
        return [y[:, :n_x], y[:, n_x:n_x + n_b], y[:, n_x + n_b:]], [xbc]

    ins = [_rows(xbc, t, gdtype=BF16), _whole(conv_w), _whole(conv_b)]
    outs = [_rows_out(s, n_x, t, F32), _rows_out(s, n_b, t, F32), _rows_out(s, n_b, t, F32)]
    return Op("ssd_conv", f, (1, s // t), ins, outs, [(t, w)])


SSD_L = 512


def _ssd_core_op(x, bm, cm, z, dt, dt_bias, a_log, d_skip, g_norm):
    s = x.shape[0]
    c, hg, p = SSD_L, SSD_HPG, SSD_P
    gw = hg * p

    def f(g, vals, sts):
        x, bm, cm, z, dtr, dt_bias, a_log, d_skip, g_norm = vals
        tri = _tri(c)
        dt = jax.nn.softplus(dtr + dt_bias)
        da = dt * (-jnp.exp(a_log))
        cs = _cumsum_rows(da)
        cs_last = jnp.sum(da, axis=0, keepdims=True)
        cs_t = jnp.transpose(jnp.concatenate([cs, jnp.zeros((c, LANE - hg), F32)], axis=1))
        cb = _bdot_nt(cm, bm)
        ys, new_st = [], []
        for h in range(hg):
            cs_h = cs[:, h:h + 1]
            cs_row = cs_t[h:h + 1, :]
            seg = jnp.where(tri, cs_h - cs_row, 0.0)
            lmat = jnp.where(tri, jnp.exp(seg), 0.0)
            x_h = x[:, h * p:(h + 1) * p]
            xdt = x_h * dt[:, h:h + 1]
            y_diag = _bdot(cb * lmat, xdt)
            decay = jnp.exp(cs_last[:, h:h + 1] - cs_h)
            states = _bdot_tn(xdt * decay, bm)
            y_off = _bdot_nt(cm, sts[h]) * jnp.exp(cs_h)
            new_st.append(jnp.exp(cs_last[:, h:h + 1]) * sts[h] + states)
            ys.append(y_diag + y_off + d_skip[:, h:h + 1] * x_h)
        y = jnp.concatenate(ys, axis=1) * _silu(z)
        return [_rms(y, g_norm)], new_st

    ins = [In(x, (c, gw), lambda h, g: (g, h), "x"), In(bm, (c, SSD_STATE), lambda h, g: (g, h), "x"),
           In(cm, (c, SSD_STATE), lambda h, g: (g, h), "x"), In(z, (c, gw), lambda h, g: (g, h), "x", gdtype=BF16),
           In(dt, (None, c, hg), lambda h, g: (h, g, 0), "x"),
           In(dt_bias, (None, 1, hg), lambda h, g: (h, 0, 0), "p", per_h=True),
           In(a_log, (None, 1, hg), lambda h, g: (h, 0, 0), "p", per_h=True),
           In(d_skip, (None, 1, hg), lambda h, g: (h, 0, 0), "p", per_h=True),
           In(g_norm, (1, gw), lambda h, g: (0, h), "p", per_h=True)]
    outs = [Out((s, SSD_INNER), BF16, (c, gw), lambda h, g: (g, h))]
    return Op("ssd_core", f, (SSD_GROUPS, s // c), ins, outs, [(p, SSD_STATE)] * hg)


def _loss_op(h, target, final_g):
    s, n = h.shape
    t = ROW_T
    n_g = s // t

    def body(h_ref, t_ref, g_ref, loss_ref, dh_ref, dg_ref):
        step = pl.program_id(0)

        def lossf(hv, gv):
            err = _rms(hv, gv) - t_ref[...]
            return 0.5 * jnp.sum(jnp.mean(err * err, axis=-1))

        l, (dh, dg) = jax.value_and_grad(lossf, argnums=(0, 1))(h_ref[...], g_ref[...])
        dh_ref[...] = dh

        @pl.when(step == 0)
        def _():
            loss_ref[...] = jnp.zeros(loss_ref.shape, F32)
            dg_ref[...] = jnp.zeros(dg_ref.shape, F32)

        loss_ref[...] += jnp.full(loss_ref.shape, l, F32)
        dg_ref[...] += dg

    row = pl.BlockSpec((t, n), lambda g: (g, 0))
    one = pl.BlockSpec((1, n), lambda g: (0, 0))
    return pl.pallas_call(
        body, name="loss_head", grid=(n_g,), in_specs=[row, row, one],
        out_specs=[pl.BlockSpec((1, LANE), lambda g: (0, 0)), row, one],
        out_shape=[jax.ShapeDtypeStruct((1, LANE), F32), jax.ShapeDtypeStruct((s, n), F32), jax.ShapeDtypeStruct((1, n), F32)],
        compiler_params=_CP(dimension_semantics=("arbitrary",)),
    )(h, target, final_g.reshape(1, n))


def _pad_cols(w, n):
    return jnp.pad(w, ((0, 0), (0, n - w.shape[1])))


def _pad_rows(w, n):
    return jnp.pad(w, ((0, n - w.shape[0]), (0, 0)))


def _proj_bwd(tag, u, dps, ws):
    du = _mm_pieces_t(f"{tag}_du", dps, ws)
    dws = [_mm(f"{tag}_dw{i}", u, dp, ta=True, out_dtype=BF16) for i, dp in enumerate(dps)]
    return du, dws


def _mla_layer(h, norm_g, w, cos, sin, fwd_comm=None, late_w_out=None):
    bf = lambda a: a.astype(BF16)
    w_in, w_uq, w_ukv = w["mla_w_in"], w["mla_w_uq"], w["mla_w_ukv"]
    a0, a1, a2 = MLA_Q_RANK, MLA_Q_RANK + MLA_KV_RANK, MLA_Q_RANK + MLA_KV_RANK + MLA_ROPE
    w_cq, w_ckv, w_kr, w_g = bf(w_in[:, :a0]), bf(w_in[:, a0:a1]), bf(_pad_cols(w_in[:, a1:a2], LANE)), bf(w_in[:, a2:])
    uq = w_uq.reshape(MLA_Q_RANK, MLA_HEADS, MLA_QK)
    half = MLA_ROPE // 2
    w_qn = bf(uq[:, :, :MLA_NOPE].reshape(MLA_Q_RANK, -1))
    w_q1 = bf(uq[:, :, MLA_NOPE:MLA_NOPE + half].reshape(MLA_Q_RANK, -1))
    w_q2 = bf(uq[:, :, MLA_NOPE + half:].reshape(MLA_Q_RANK, -1))
    ukv = w_ukv.reshape(MLA_KV_RANK, MLA_HEADS, MLA_NOPE + MLA_V)
    w_kn = bf(ukv[:, :, :MLA_NOPE].reshape(MLA_KV_RANK, -1))
    w_v = bf(ukv[:, :, MLA_NOPE:].reshape(MLA_KV_RANK, -1))

    n0 = _rms_op("mla_norm", h, norm_g)
    u, = n0.fwd()
    cq, ckv, kr, gate = _mm_pieces("mla_in", u, (w_cq, w_ckv, w_kr, w_g))
    nq = _rms_op("mla_qnorm", cq, w["mla_g_q"], gdtype=BF16)
    nkv = _rms_op("mla_kvnorm", ckv, w["mla_g_kv"], gdtype=BF16)
    qn_, = nq.fwd()
    kvn_, = nkv.fwd()
    qn, q1, q2 = _mm_pieces("mla_uq", qn_, (w_qn, w_q1, w_q2))
    kn, v = _mm_pieces("mla_ukv", kvn_, (w_kn, w_v))
    prep = _mla_prep_op(qn, q1, q2, kn, kr, v, cos, sin)
    qh, kh, vh = prep.fwd()
    attn = _mla_attn_op(qh, kh, vh)
    o, = attn.fwd(fwd_comm)
    w_out = bf(w["mla_w_out"]) if late_w_out is None else late_w_out(attn.fwd_comm_out)
    post = _mla_post_op(o, gate)
    y, = post.fwd()
    h_out = _mm("mla_out", y, w_out, add=h)

    def bwd(dh, make_comm=None):
        dy = _mm("mla_out_dy", dh, w_out, tb=True, out_dtype=BF16)
        d_w_out = _mm("mla_out_dw", y, dh, ta=True, out_dtype=BF16)
        do, dgate = post.bwd([dy])
        dqh, dkh, dvh = attn.bwd([do], comm=None if make_comm is None else make_comm(d_w_out))
        dqn, dq1, dq2, dkn, dkr, dv = prep.bwd([dqh, dkh, dvh])
        dqn_, d_uq = _proj_bwd("mla_uq", qn_, (dqn, dq1, dq2), (w_qn, w_q1, w_q2))
        dkvn_, d_ukv = _proj_bwd("mla_ukv", kvn_, (dkn, dv), (w_kn, w_v))
        dcq, d_g_q = nq.bwd([dqn_])
        dckv, d_g_kv = nkv.bwd([dkvn_])
        du, d_in = _proj_bwd("mla_in", u, (dcq, dckv, dkr, dgate), (w_cq, w_ckv, w_kr, w_g))
        dh_in, d_norm = n0.bwd([du], addto={0: dh})
        shp = (MLA_Q_RANK, MLA_HEADS, -1)
        g_uq = jnp.concatenate([d_uq[0].reshape(shp), d_uq[1].reshape(shp), d_uq[2].reshape(shp)], axis=2).reshape(MLA_Q_RANK, -1)
        shp = (MLA_KV_RANK, MLA_HEADS, -1)
        g_ukv = jnp.concatenate([d_ukv[0].reshape(shp), d_ukv[1].reshape(shp)], axis=2).reshape(MLA_KV_RANK, -1)
        g_in = jnp.concatenate([d_in[0], d_in[1], d_in[2][:, :MLA_ROPE], d_in[3]], axis=1)
        return dh_in, d_norm, {"mla_w_in": g_in, "mla_g_q": d_g_q.reshape(-1), "mla_w_uq": g_uq, "mla_g_kv": d_g_kv.reshape(-1),
                               "mla_w_ukv": g_ukv, "mla_w_out": d_w_out}, attn.bwd_comm_out

    return h_out, bwd, attn.fwd_comm_out


def _gla_layer(h, norm_g, w, fwd_comm=None):
    bf = lambda a: a.astype(BF16)
    w_in = w["gla_w_in"]
    nk, nv = GLA_HEADS * GLA_DK, GLA_HEADS * GLA_DV
    cuts = (0, nk, 2 * nk, 2 * nk + nv, 2 * nk + 2 * nv)
    w_q, w_k, w_v, w_g = (bf(w_in[:, cuts[i]:cuts[i + 1]]) for i in range(4))
    w_gk = bf(_pad_cols(w_in[:, cuts[4]:], LANE))
    w2 = _pad_rows(w["gla_w_gk2"], LANE)
    b_gk = w["gla_b_gk"].reshape(1, -1)
    g_o = w["gla_g_o"].reshape(1, -1)
    w_out = bf(w["gla_w_out"])

    n0 = _rms_op("gla_norm", h, norm_g)
    u, = n0.fwd()
    q, k, v, gate, gk = _mm_pieces("gla_in", u, (w_q, w_k, w_v, w_g, w_gk))
    gop = _gla_gate_op(gk, w2, b_gk)
    la, = gop.fwd()
    core = _gla_core_op(q, k, v, gate, la, g_o)
    y, = core.fwd(fwd_comm)
    h_out = _mm("gla_out", y, w_out, add=h)

    def bwd(dh, make_comm=None):
        dy = _mm("gla_out_dy", dh, w_out, tb=True, out_dtype=BF16)
        d_w_out = _mm("gla_out_dw", y, dh, ta=True, out_dtype=BF16)
        dq, dk, dv, dgate, dla, d_g_o = core.bwd([dy], comm=None if make_comm is None else make_comm(d_w_out))
        dgk, d_w2, d_b = gop.bwd([dla])
        du, d_in = _proj_bwd("gla_in", u, (dq, dk, dv, dgate, dgk), (w_q, w_k, w_v, w_g, w_gk))
        dh_in, d_norm = n0.bwd([du], addto={0: dh})
        g_in = jnp.concatenate([d_in[0], d_in[1], d_in[2], d_in[3], d_in[4][:, :GLA_RANK]], axis=1)
        return dh_in, d_norm, {"gla_w_in": g_in, "gla_w_gk2": d_w2[:GLA_RANK], "gla_b_gk": d_b.reshape(-1), "gla_g_o": d_g_o.reshape(-1),
                               "gla_w_out": d_w_out}, core.bwd_comm_out

    return h_out, bwd, core.fwd_comm_out


def _lru_layer(h, norm_g, w, fwd_comm=None):
    bf = lambda a: a.astype(BF16)
    w_in = w["lru_w_in"]
    w_g, w_u = bf(w_in[:, :LRU_WIDTH]), bf(w_in[:, LRU_WIDTH:])
    row = lambda a: a.reshape(1, -1)
    w_out = bf(w["lru_w_out"])

    n0 = _rms_op("lru_norm", h, norm_g)
    u_, = n0.fwd()
    gate, u = _mm_pieces("lru_in", u_, (w_g, w_u))
    core = _lru_op(gate, u, w["lru_conv_w"].reshape(CONV_W, 1, -1), row(w["lru_conv_b"]), w["lru_w_a"], row(w["lru_b_a"]), w["lru_w_x"],
                   row(w["lru_b_x"]), row(w["lru_lam"]))
    y, = core.fwd(fwd_comm)
    h_out = _mm("lru_out", y, w_out, add=h)

    def bwd(dh, make_comm=None):
        dy = _mm("lru_out_dy", dh, w_out, tb=True, out_dtype=BF16)
        d_w_out = _mm("lru_out_dw", y, dh, ta=True, out_dtype=BF16)
        dgate, du, d_cw, d_cb, d_wa, d_ba, d_wx, d_bx, d_lam = core.bwd([dy], comm=None if make_comm is None else make_comm(d_w_out))
        du_, d_in = _proj_bwd("lru_in", u_, (dgate, du), (w_g, w_u))
        dh_in, d_norm = n0.bwd([du_], addto={0: dh})
        return dh_in, d_norm, {"lru_w_in": jnp.concatenate(d_in, axis=1), "lru_conv_w": d_cw.reshape(CONV_W, -1), "lru_conv_b": d_cb.reshape(-1),
                               "lru_w_a": d_wa, "lru_b_a": d_ba.reshape(-1), "lru_w_x": d_wx, "lru_b_x": d_bx.reshape(-1),
                               "lru_lam": d_lam.reshape(-1), "lru_w_out": d_w_out}, core.bwd_comm_out

    return h_out, bwd, core.fwd_comm_out


def _ssd_layer(h, norm_g, w, fwd_comm=None, late_w_out=None):
    bf = lambda a: a.astype(BF16)
    s = h.shape[0]
    w_in = w["ssd_w_in"]
    conv_dim = SSD_INNER + 2 * SSD_GROUPS * SSD_STATE
    w_z, w_xbc = bf(w_in[:, :SSD_INNER]), bf(w_in[:, SSD_INNER:SSD_INNER + conv_dim])
    w_dt = bf(_pad_cols(w_in[:, SSD_INNER + conv_dim:], LANE))
    grp = lambda a: a.reshape(SSD_GROUPS, 1, SSD_HPG)

    n0 = _rms_op("ssd_norm", h, norm_g)
    u, = n0.fwd()
    z, xbc, dtp = _mm_pieces("ssd_in", u, (w_z, w_xbc, w_dt))
    conv = _ssd_conv_op(xbc, w["ssd_conv_w"].reshape(CONV_W, 1, -1), w["ssd_conv_b"].reshape(1, -1))
    x, bm, cm = conv.fwd()
    dt = dtp[:, :SSD_HEADS].reshape(s, SSD_GROUPS, SSD_HPG).transpose(1, 0, 2)
    core = _ssd_core_op(x, bm, cm, z, dt, grp(w["ssd_dt_bias"]), grp(w["ssd_a_log"]), grp(w["ssd_d"]), w["ssd_g_norm"].reshape(1, -1))
    y, = core.fwd(fwd_comm)
    w_out = bf(w["ssd_w_out"]) if late_w_out is None else late_w_out(core.fwd_comm_out)
    h_out = _mm("ssd_out", y, w_out, add=h)

    def bwd(dh, make_comm=None):
        dy = _mm("ssd_out_dy", dh, w_out, tb=True, out_dtype=BF16)
        d_w_out = _mm("ssd_out_dw", y, dh, ta=True, out_dtype=BF16)
        dx, dbm, dcm, dz, ddt, d_dtb, d_alog, d_d, d_gn = core.bwd([dy], comm=None if make_comm is None else make_comm(d_w_out))
        dxbc, d_cw, d_cb = conv.bwd([dx, dbm, dcm])
        ddtp = _pad_cols(ddt.transpose(1, 0, 2).reshape(s, SSD_HEADS), LANE).astype(BF16)
        du, d_in = _proj_bwd("ssd_in", u, (dz, dxbc, ddtp), (w_z, w_xbc, w_dt))
        dh_in, d_norm = n0.bwd([du], addto={0: dh})
        g_in = jnp.concatenate([d_in[0], d_in[1], d_in[2][:, :SSD_HEADS]], axis=1)
        return dh_in, d_norm, {"ssd_w_in": g_in, "ssd_conv_w": d_cw.reshape(CONV_W, -1), "ssd_conv_b": d_cb.reshape(-1),
                               "ssd_dt_bias": d_dtb.reshape(-1), "ssd_a_log": d_alog.reshape(-1), "ssd_d": d_d.reshape(-1),
                               "ssd_g_norm": d_gn.reshape(-1), "ssd_w_out": d_w_out}, core.bwd_comm_out

    return h_out, bwd


def _rope_tables(positions):
    inv_freq = ROPE_THETA ** (-jnp.arange(0, MLA_ROPE, 2, dtype=F32) / MLA_ROPE)
    ang = positions.astype(F32)[:, None] * inv_freq
    return jnp.cos(ang), jnp.sin(ang)


WEIGHTS = ["norm_g", "final_g", "mla_w_in", "mla_g_q", "mla_w_uq", "mla_g_kv", "mla_w_ukv", "mla_w_out", "gla_w_in", "gla_w_gk2", "gla_b_gk",
           "gla_g_o", "gla_w_out", "lru_w_in", "lru_conv_w", "lru_conv_b", "lru_w_a", "lru_b_a", "lru_w_x", "lru_b_x", "lru_lam", "lru_w_out",
           "ssd_w_in", "ssd_conv_w", "ssd_conv_b", "ssd_dt_bias", "ssd_a_log", "ssd_d", "ssd_g_norm", "ssd_w_out"]
BIG = ["mla_w_in", "mla_w_uq", "mla_w_ukv", "mla_w_out", "gla_w_in", "gla_w_out", "lru_w_in", "lru_w_out", "ssd_w_in", "ssd_w_out"]
SMALL = ["gla_w_gk2", "gla_b_gk", "gla_g_o", "lru_conv_w", "lru_conv_b", "lru_b_a", "lru_b_x", "lru_lam", "ssd_conv_w", "ssd_conv_b", "ssd_g_norm"]
REPL = ["norm_g", "final_g", "mla_g_q", "mla_g_kv", "lru_w_a", "lru_w_x", "ssd_dt_bias", "ssd_a_log", "ssd_d"]
REPL_EARLY = ["lru_w_a", "lru_w_x"]
REPL_LATE = [n for n in REPL if n not in REPL_EARLY]
N_CHIPS, N_DEV = 4, 8
PACK_W = 1024
ADAM_ROWS = 256
SMALL_ROWS = 64


def _shard_axis(name):
    return 0 if name.endswith("_w_out") else -1


def _pack(arrs, dtype, row_mult):
    flat = jnp.concatenate([a.reshape(-1).astype(dtype) for a in arrs])
    per = PACK_W * row_mult
    total = -(-flat.shape[0] // per) * per
    return jnp.pad(flat, (0, total - flat.shape[0])).reshape(-1, PACK_W)


def _unpack(buf, shapes):
    flat = buf.reshape(-1)
    out, off = [], 0
    for s in shapes:
        n = math.prod(s)
        out.append(flat[off:off + n].reshape(s))
        off += n
    return out


def _mesh_pos():
    return lax.axis_index("x"), lax.axis_index("y"), lax.axis_index("c")


class GatherComm:
    def __init__(self, ops):
        self.ops = list(ops)
        n = len(self.ops)
        assert all(o.ndim == 2 and o.shape[0] % 32 == 0 for o in self.ops), [o.shape for o in self.ops]
        self.out_shapes = [jax.ShapeDtypeStruct((N_CHIPS,) + o.shape, o.dtype) for o in self.ops]
        self.sem_shapes = [pltpu.SemaphoreType.DMA((6 * n,)), pltpu.SemaphoreType.DMA((6 * n,)), pltpu.SemaphoreType.DMA((n,))]

    def _copies(self, srcs, dsts, sems):
        send_sems, recv_sems, local_sems = sems
        n = len(self.ops)
        x, y, c = _mesh_pos()
        me_id, sibling = (x, y, c), (x, y, 1 - c)
        chips = [(1 - x, y), (x, 1 - y), (1 - x, 1 - y)]
        mine = 2 * x + y

        def half(i, cc):
            h = self.ops[i].shape[0] // 2
            return pl.ds(cc * h, h)

        def copy(i, k, src, slot, cc, to):
            return pltpu.make_async_remote_copy(src_ref=src, dst_ref=dsts[i].at[slot, half(i, cc)], send_sem=send_sems.at[i * 6 + k],
                                                recv_sem=recv_sems.at[i * 6 + k], device_id=to, device_id_type=pl.DeviceIdType.MESH)

        local = [pltpu.make_async_copy(srcs[i], dsts[i].at[mine], local_sems.at[i]) for i in range(n)]
        first, ici_recvs, passed, sib_recvs = [], [], [], []
        for i in range(n):
            my_half = srcs[i].at[half(i, c)]
            for k, (px, py) in enumerate(chips):
                slot = 2 * px + py
                first.append(copy(i, k, my_half, mine, c, (px, py, c)))
                ici_recvs.append(copy(i, k, my_half, slot, c, me_id))
                passed.append(copy(i, 3 + k, dsts[i].at[slot, half(i, c)], slot, c, sibling))
                sib_recvs.append(copy(i, 3 + k, my_half, slot, 1 - c, me_id))
        return local, first, ici_recvs, passed, sib_recvs

    def start(self, srcs, dsts, sems):
        local, first, _, _, _ = self._copies(srcs, dsts, sems)
        for cp in local + first:
            cp.start()

    def forward(self, srcs, dsts, sems):
        _, _, ici_recvs, passed, _ = self._copies(srcs, dsts, sems)
        for rc, fw in zip(ici_recvs, passed):
            rc.wait_recv()
            fw.start()

    def finish(self, srcs, dsts, sems):
        local, first, _, passed, sib_recvs = self._copies(srcs, dsts, sems)
        for cp in sib_recvs:
            cp.wait_recv()
        for cp in first + passed:
            cp.wait_send()
        for cp in local:
            cp.wait()


class ExchangeComm:
    def __init__(self, chip_ops, all_ops=()):
        self.ops = list(chip_ops) + list(all_ops)
        self.per_chip = (True,) * len(chip_ops) + (False,) * len(all_ops)
        n = len(self.ops)
        self.out_shapes = [jax.ShapeDtypeStruct((N_DEV,) + o.shape[-2:], o.dtype) for o in self.ops]
        self.sem_shapes = [pltpu.SemaphoreType.DMA((7 * n,)), pltpu.SemaphoreType.DMA((7 * n,)), pltpu.SemaphoreType.DMA((n,))]

    def _copies(self, srcs, dsts, sems):
        send_sems, recv_sems, local_sems = sems
        n, per_chip = len(self.ops), self.per_chip
        x, y, c = _mesh_pos()
        me_id, sibling = (x, y, c), (x, y, 1 - c)
        chips = [(1 - x, y), (x, 1 - y), (1 - x, 1 - y)]

        def dev(px, py, pc):
            return 4 * px + 2 * py + pc

        def part(i, px, py):
            return srcs[i].at[2 * px + py] if per_chip[i] else srcs[i]

        def copy(i, k, src, slot, to):
            return pltpu.make_async_remote_copy(src_ref=src, dst_ref=dsts[i].at[slot], send_sem=send_sems.at[i * 7 + k],
                                                recv_sem=recv_sems.at[i * 7 + k], device_id=to, device_id_type=pl.DeviceIdType.MESH)

        me = dev(x, y, c)
        local = [pltpu.make_async_copy(part(i, x, y), dsts[i].at[me], local_sems.at[i]) for i in range(n)]
        first, ici_recvs, passed, sib_recvs = [], [], [], []
        for i in range(n):
            first.append(copy(i, 0, part(i, x, y), me, sibling))
            first += [copy(i, 1 + k, part(i, px, py), me, (px, py, c)) for k, (px, py) in enumerate(chips)]
            sib_recvs.append(copy(i, 0, part(i, x, y), dev(x, y, 1 - c), me_id))
            for k, (px, py) in enumerate(chips):
                slot = dev(px, py, c)
                ici_recvs.append(copy(i, 1 + k, part(i, x, y), slot, me_id))
                passed.append(copy(i, 4 + k, dsts[i].at[slot], slot, sibling))
                sib_recvs.append(copy(i, 4 + k, part(i, x, y), dev(px, py, 1 - c), me_id))
        return local, first, ici_recvs, passed, sib_recvs

    def start(self, srcs, dsts, sems):
        local, first, _, _, _ = self._copies(srcs, dsts, sems)
        for cp in local + first:
            cp.start()

    def forward(self, srcs, dsts, sems):
        _, _, ici_recvs, passed, _ = self._copies(srcs, dsts, sems)
        for rc, fw in zip(ici_recvs, passed):
            rc.wait_recv()
            fw.start()

    def finish(self, srcs, dsts, sems):
        local, first, _, passed, sib_recvs = self._copies(srcs, dsts, sems)
        for cp in sib_recvs:
            cp.wait_recv()
        for cp in first + passed:
            cp.wait_send()
        for cp in local:
            cp.wait()


def _run_comm(name, comm):
    n = len(comm.ops)

    def body(*refs):
        srcs, dsts, sems = refs[:n], refs[n:2 * n], refs[2 * n:]
        comm.start(srcs, dsts, sems)
        comm.forward(srcs, dsts, sems)
        comm.finish(srcs, dsts, sems)

    any_spec = pl.BlockSpec(memory_space=pl.ANY)
    return pl.pallas_call(body, name=name, in_specs=[any_spec] * n, out_specs=[any_spec] * n, out_shape=comm.out_shapes,
                          scratch_shapes=comm.sem_shapes)(*comm.ops)


def _adamw(name, parts, w, m, v, lead=False, comm=None):
    plist = list(parts) if isinstance(parts, (list, tuple)) else [parts]
    n_p = len(plist)
    rows, cols = w.shape[-2:]
    t = next(c for c in (ADAM_ROWS, ADAM_ROWS // 2, SMALL_ROWS) if all(p.shape[1] % c == 0 for p in plist))
    starts = [sum(p.shape[1] for p in plist[:k]) // t for k in range(n_p)]
    counts = [p.shape[1] // t for p in plist]
    assert sum(p.shape[1] for p in plist) == rows, (name, rows)
    c1 = 1.0 - ADAM_B1 ** ADAM_STEP
    c2 = 1.0 - ADAM_B2 ** ADAM_STEP

    n_c = len(comm.ops) if comm is not None else 0
    n_steps = rows // t

    def body(*refs):
        p_refs = refs[:n_p]
        w_ref, m_ref, v_ref = refs[n_p:n_p + 3]
        c_src = refs[n_p + 3:n_p + 3 + n_c]
        g_ref, d_ref, nm_ref, nv_ref = refs[n_p + 3 + n_c:n_p + 7 + n_c]
        cargs = (c_src, refs[n_p + 7 + n_c:n_p + 7 + 2 * n_c], refs[n_p + 7 + 2 * n_c:])
        if comm is not None:
            @pl.when(pl.program_id(0) == 0)
            def _():
                comm.start(*cargs)

            @pl.when(pl.program_id(0) == _forward_step(n_steps))
            def _():
                comm.forward(*cargs)
        g = None
        for k, p_ref in enumerate(p_refs):
            gk = p_ref[0].astype(F32)
            for d in range(1, N_DEV):
                gk = gk + p_ref[d].astype(F32)
            g = gk if g is None else jnp.where(pl.program_id(0) >= starts[k], gk, g)
        nm = ADAM_B1 * m_ref[...] + (1.0 - ADAM_B1) * g
        nv = ADAM_B2 * v_ref[...] + (1.0 - ADAM_B2) * (g * g)
        g_ref[...] = g
        nm_ref[...] = nm
        nv_ref[...] = nv
        d_ref[...] = -ADAM_LR * ((nm / c1) / (jnp.sqrt(nv / c2) + ADAM_EPS) + ADAM_WD * w_ref[...])
        if comm is not None:
            @pl.when(pl.program_id(0) == n_steps - 1)
            def _():
                comm.finish(*cargs)

    row = pl.BlockSpec((None, t, cols), lambda i: (0, i, 0)) if lead else pl.BlockSpec((t, cols), lambda i: (i, 0))
    c_in_specs, c_args, c_out_specs, c_out_shapes, c_sems = _carry_specs(comm)
    res = pl.pallas_call(
        body, name=name, grid=(n_steps,),
        in_specs=[pl.BlockSpec((N_DEV, t, cols), lambda i, lo=lo, n=n: (0, jnp.clip(i - lo, 0, n - 1), 0)) for lo, n in zip(starts, counts)]
        + [row, row, row] + c_in_specs,
        out_specs=[row] * 4 + c_out_specs, out_shape=[jax.ShapeDtypeStruct(w.shape, F32)] * 4 + c_out_shapes, scratch_shapes=c_sems,
        compiler_params=_CP(dimension_semantics=("arbitrary" if comm is not None else "parallel",)),
    )(*plist, w, m, v, *c_args)
    return list(res[:4]), list(res[4:])


def _train_step(x, positions, target, wts, ms, vs, raw):
    small_shapes = [wts[n].shape for n in SMALL]

    big_of = {tag: [n for n in BIG if n.startswith(tag)] for tag in ("mla", "gla", "lru", "ssd")}
    full = {n: wts[n] for n in REPL}

    def gather_comm(names, extra=()):
        return GatherComm([wts[n].astype(BF16) for n in names] + list(extra))

    def assemble(names, got):
        for k, n in enumerate(names):
            full[n] = jnp.concatenate([got[k][j] for j in range(N_CHIPS)], axis=_shard_axis(n))

    first = [n for n in big_of["mla"] if n != "mla_w_out"]
    got = _run_comm("gather_first", gather_comm(first, [_pack([wts[n] for n in SMALL], F32, SMALL_ROWS)]))
    assemble(first, got)
    per_chip_small = [_unpack(got[-1][j], small_shapes) for j in range(N_CHIPS)]
    for k, n in enumerate(SMALL):
        full[n] = jnp.concatenate([per_chip_small[j][k] for j in range(N_CHIPS)], axis=_shard_axis(n))

    cos, sin = _rope_tables(positions)
    ng = full["norm_g"]
    behind_attn = ["mla_w_out"] + big_of["gla"] + big_of["lru"]

    def mla_w_out(got):
        assemble(behind_attn, got)
        return full["mla_w_out"].astype(BF16)

    h1, b0, _ = _mla_layer(x, ng[0], full, cos, sin, fwd_comm=gather_comm(behind_attn), late_w_out=mla_w_out)

    def joined(got_k, axis):
        return jnp.concatenate([got_k[j] for j in range(N_CHIPS)], axis=axis)

    ssd_in = wts["ssd_w_in"].astype(BF16)
    half = ssd_in.shape[0] // 2
    h2, b1, got = _gla_layer(h1, ng[1], full, fwd_comm=GatherComm([ssd_in[:half]]))
    top = joined(got[0], -1)
    h3, b2, got = _lru_layer(h2, ng[2], full, fwd_comm=GatherComm([ssd_in[half:]]))
    full["ssd_w_in"] = jnp.concatenate([top, joined(got[0], -1)], axis=0)
    h4, b3 = _ssd_layer(h3, ng[3], full, fwd_comm=gather_comm(["ssd_w_out"]), late_w_out=lambda got: joined(got[0], 0))
    loss, dh, d_final = _loss_op(h4, target, full["final_g"])
    loss = loss[0, 0]
    grads = {"final_g": d_final.reshape(-1)}
    d_norms = [None] * 4

    def shards_of(n, g):
        return jnp.stack(jnp.split(g.astype(BF16), N_CHIPS, axis=_shard_axis(n)))

    def shards(n):
        return shards_of(n, grads[n])

    parts = {}
    dh, d_norms[3], gw, got = b3(dh, make_comm=lambda dw: ExchangeComm([shards_of("ssd_w_out", dw)]))
    parts["ssd_w_out"] = got[0]
    grads.update(gw)
    ssd_in_g = shards("ssd_w_in")
    half = ssd_in_g.shape[1] // 2
    dh, d_norms[2], gw, got = b2(dh, make_comm=lambda dw: ExchangeComm([ssd_in_g[:, :half], shards_of("lru_w_out", dw)]))
    parts["ssd_w_in"], parts["lru_w_out"] = [got[0]], got[1]
    grads.update(gw)
    dh, d_norms[1], gw, got = b1(dh, make_comm=lambda dw: ExchangeComm([ssd_in_g[:, half:]]))
    parts["ssd_w_in"].append(got[0])
    grads.update(gw)
    repl_early = _pack([grads[n] for n in REPL_EARLY], BF16, SMALL_ROWS)

    behind_attn = ["gla_w_out", "lru_w_in", "gla_w_in"]
    dx, d_norms[0], gw, got = b0(dh, make_comm=lambda dw: ExchangeComm([shards_of("mla_w_out", dw)] + [shards(n) for n in behind_attn],
                                                                        [repl_early]))
    parts.update(zip(["mla_w_out"] + behind_attn, got))
    repl_early_parts = got[-1]
    grads.update(gw)
    grads["norm_g"] = jnp.concatenate(d_norms, axis=0)
    psmall = jnp.stack([_pack([jnp.split(grads[n], N_CHIPS, axis=_shard_axis(n))[j] for n in SMALL], F32, SMALL_ROWS) for j in range(N_CHIPS)])
    prepl = _pack([grads[n] for n in REPL_LATE], F32, SMALL_ROWS)

    out = {}
    kinds = ("grad", "delta", "new_m", "new_v")
    late = [n for n in big_of["mla"] if n != "mla_w_out"]
    late_comm = ExchangeComm([shards(n) for n in late] + [psmall], [prepl])
    for n in ["ssd_w_in"] + [n for n in BIG if n != "ssd_w_in"]:
        if n == "ssd_w_in":
            res, late_parts = _adamw("adam_" + n, parts[n], *(r[n] for r in raw), lead=True, comm=late_comm)
            parts.update(zip(late, late_parts))
        else:
            res, _ = _adamw("adam_" + n, parts[n], *(r[n] for r in raw), lead=True)
        for kind, a in zip(kinds, res):
            out[kind, n] = a
    for tag, names, p in (("adam_small", SMALL, late_parts[-2]), ("adam_repl_early", REPL_EARLY, repl_early_parts),
                          ("adam_repl_late", REPL_LATE, late_parts[-1])):
        shapes = [wts[n].shape for n in names]
        packed = [_pack([d[n] for n in names], F32, SMALL_ROWS) for d in (wts, ms, vs)]
        for kind, buf in zip(kinds, _adamw(tag, p, *packed)[0]):
            for n, a in zip(names, _unpack(buf, shapes)):
                out[kind, n] = a
    loss = lax.psum(loss, ("x", "y", "c"))
    return loss, dx, out


def kernel(x, positions, norm_g, final_g, mla_w_in, mla_g_q, mla_w_uq, mla_g_kv, mla_w_ukv, mla_w_out, gla_w_in, gla_w_gk2, gla_b_gk, gla_g_o, gla_w_out, lru_w_in, lru_conv_w, lru_conv_b, lru_w_a, lru_b_a, lru_w_x, lru_b_x, lru_lam, lru_w_out, ssd_w_in, ssd_conv_w, ssd_conv_b, ssd_dt_bias, ssd_a_log, ssd_d, ssd_g_norm, ssd_w_out, loss_target, m_norm_g, m_final_g, m_mla_w_in, m_mla_g_q, m_mla_w_uq, m_mla_g_kv, m_mla_w_ukv, m_mla_w_out, m_gla_w_in, m_gla_w_gk2, m_gla_b_gk, m_gla_g_o, m_gla_w_out, m_lru_w_in, m_lru_conv_w, m_lru_conv_b, m_lru_w_a, m_lru_b_a, m_lru_w_x, m_lru_b_x, m_lru_lam, m_lru_w_out, m_ssd_w_in, m_ssd_conv_w, m_ssd_conv_b, m_ssd_dt_bias, m_ssd_a_log, m_ssd_d, m_ssd_g_norm, m_ssd_w_out, v_norm_g, v_final_g, v_mla_w_in, v_mla_g_q, v_mla_w_uq, v_mla_g_kv, v_mla_w_ukv, v_mla_w_out, v_gla_w_in, v_gla_w_gk2, v_gla_b_gk, v_gla_g_o, v_gla_w_out, v_lru_w_in, v_lru_conv_w, v_lru_conv_b, v_lru_w_a, v_lru_b_a, v_lru_w_x, v_lru_b_x, v_lru_lam, v_lru_w_out, v_ssd_w_in, v_ssd_conv_w, v_ssd_conv_b, v_ssd_dt_bias, v_ssd_a_log, v_ssd_d, v_ssd_g_norm, v_ssd_w_out):
    given = dict(locals())
    stacked = [n for n in WEIGHTS if n not in ("norm_g", "final_g")]

    def blocks(prefix):
        return {n: (given[prefix + n][0] if n in stacked else given[prefix + n]) for n in WEIGHTS}

    raw = [{n: given[prefix + n] for n in BIG} for prefix in ("", "m_", "v_")]
    loss, dx, out = _train_step(x[0], positions[0], loss_target[0], blocks(""), blocks("m_"), blocks("v_"), raw)
    res = [loss, dx[None]]
    for kind in ("grad", "delta", "new_m", "new_v"):
        res += [(out[kind, n][None] if n in stacked and n not in BIG else out[kind, n]) for n in WEIGHTS]
    return tuple(res)
```

```python
import functools
import math

import jax
import jax.numpy as jnp
from jax import lax
from jax.experimental import pallas as pl
from jax.experimental.pallas import tpu as pltpu

F32 = jnp.float32
BF16 = jnp.bfloat16

V7X_VMEM_BYTES = 64 * 1024 * 1024
VMEM_LIMIT = V7X_VMEM_BYTES - 8 * 1024 * 1024
LANE = 128

D_MODEL = 1024
NORM_EPS = 1e-6
MLA_HEADS, MLA_Q_RANK, MLA_KV_RANK, MLA_NOPE, MLA_ROPE, MLA_V = 16, 384, 256, 64, 32, 64
MLA_QK = MLA_NOPE + MLA_ROPE
ROPE_THETA = 10000.0
GLA_HEADS, GLA_DK, GLA_DV, GLA_RANK, GLA_TAU, GLA_CHUNK = 4, 128, 256, 16, 16.0, 64
LRU_WIDTH, LRU_BLOCKS, LRU_BLOCK, LRU_C, CONV_W = 1280, 10, 128, 8.0, 4
SSD_INNER, SSD_P, SSD_HEADS, SSD_GROUPS, SSD_HPG, SSD_STATE, SSD_CHUNK = 2048, 64, 32, 8, 4, 128, 64
ADAM_LR, ADAM_B1, ADAM_B2, ADAM_EPS, ADAM_WD, ADAM_STEP = 0.001, 0.9, 0.999, 1e-08, 0.01, 10

_CP = functools.partial(pltpu.CompilerParams, vmem_limit_bytes=VMEM_LIMIT)


def _bdot(a, b):
    return jnp.dot(a.astype(BF16), b.astype(BF16), preferred_element_type=F32)


def _bdot_nt(a, b):
    return lax.dot_general(a.astype(BF16), b.astype(BF16), (((1,), (1,)), ((), ())), preferred_element_type=F32)


def _bdot_tn(a, b):
    return lax.dot_general(a.astype(BF16), b.astype(BF16), (((0,), (0,)), ((), ())), preferred_element_type=F32)


def _tri(n):
    r = lax.broadcasted_iota(jnp.int32, (n, n), 0)
    c = lax.broadcasted_iota(jnp.int32, (n, n), 1)
    return r >= c


def _rms(x, g):
    return x * lax.rsqrt(jnp.mean(x * x, axis=-1, keepdims=True) + NORM_EPS) * g


def _silu(x):
    return x * jax.nn.sigmoid(x)


def _shift_rows(x, prev, j):
    if j == 0:
        return x
    t = x.shape[0]

    def fwd_impl(x, prev):
        row = lax.broadcasted_iota(jnp.int32, x.shape, 0)
        return jnp.where(row >= j, pltpu.roll(x, j, 0), pltpu.roll(prev, j, 0))

    @jax.custom_vjp
    def sh(x, prev):
        return fwd_impl(x, prev)

    def sh_fwd(x, prev):
        return fwd_impl(x, prev), None

    def sh_bwd(_, gy):
        row = lax.broadcasted_iota(jnp.int32, gy.shape, 0)
        back = pltpu.roll(gy, t - j, 0)
        return jnp.where(row < t - j, back, 0.0), jnp.where(row >= t - j, back, 0.0)

    sh.defvjp(sh_fwd, sh_bwd)
    return sh(x, prev)


def _cumsum_rows(x):
    zero = jnp.zeros_like(x)
    sh = 1
    while sh < x.shape[0]:
        x = x + _shift_rows(x, zero, sh)
        sh *= 2
    return x


def _one_minus_exp(x):
    series = -x * (1.0 + x * (0.5 + x * (1.0 / 6.0 + x * (1.0 / 24.0 + x * (1.0 / 120.0)))))
    return jnp.where(x > -0.05, series, 1.0 - jnp.exp(x))


def _tile(n, cap):
    if n <= cap:
        return n
    best = None
    for t in range(LANE, cap + 1, LANE):
        if n % t == 0:
            best = t
    assert best is not None, (n, cap)
    return best


MM_BLOCK_BYTES = 8 * 1024 * 1024
MM_ROWS, MM_KROWS = 256, 512
MM_TILE_BYTES = 2 * 1024 * 1024


def _mm_tiles(m, k, n, ta):
    if ta:
        return m, _tile(n, max(LANE, MM_BLOCK_BYTES // (4 * m) // LANE * LANE)), _tile(k, MM_KROWS)
    tn = _tile(n, max(LANE, MM_BLOCK_BYTES // (2 * k) // LANE * LANE))
    rows = min(4 * MM_ROWS, max(MM_ROWS, MM_TILE_BYTES // (4 * tn) // MM_ROWS * MM_ROWS))
    return _tile(m, rows), tn, k


def _mm(name, a, b, *, ta=False, tb=False, add=None, out_dtype=F32):
    m, k = (a.shape[1], a.shape[0]) if ta else a.shape
    n, kb = (b.shape[0], b.shape[1]) if tb else (b.shape[1], b.shape[0])
    assert k == kb, (name, a.shape, b.shape, ta, tb)
    tm, tn, tk = _mm_tiles(m, k, n, ta)
    nk = k // tk
    dn = (((0 if ta else 1,), (1 if tb else 0,)), ((), ()))
    has_add = add is not None

    def finish(refs, r):
        if has_add:
            r = r + refs[2][...].astype(F32)
        return r.astype(out_dtype)

    def body_one(*refs):
        a_ref, b_ref, o_ref = refs[0], refs[1], refs[-1]
        o_ref[...] = finish(refs, lax.dot_general(a_ref[...].astype(BF16), b_ref[...].astype(BF16), dn, preferred_element_type=F32))

    def body_acc(*refs):
        a_ref, b_ref = refs[0], refs[1]
        o_ref, acc = refs[-2], refs[-1]
        kk = pl.program_id(2)

        @pl.when(kk == 0)
        def _():
            acc[...] = jnp.zeros(acc.shape, F32)

        acc[...] += lax.dot_general(a_ref[...].astype(BF16), b_ref[...].astype(BF16), dn, preferred_element_type=F32)

        @pl.when(kk == nk - 1)
        def _():
            o_ref[...] = finish(refs, acc[...])

    a_spec = pl.BlockSpec((tk, tm), lambda i, j, q: (q, i)) if ta else pl.BlockSpec((tm, tk), lambda i, j, q: (i, q))
    b_spec = pl.BlockSpec((tn, tk), lambda i, j, q: (j, q)) if tb else pl.BlockSpec((tk, tn), lambda i, j, q: (q, j))
    o_spec = pl.BlockSpec((tm, tn), lambda i, j, q: (i, j))
    in_specs, args = [a_spec, b_spec], [a, b]
    if has_add:
        in_specs.append(o_spec)
        args.append(add)
    return pl.pallas_call(
        body_one if nk == 1 else body_acc, name=name, grid=(m // tm, n // tn, nk), in_specs=in_specs, out_specs=o_spec,
        out_shape=jax.ShapeDtypeStruct((m, n), out_dtype), scratch_shapes=[] if nk == 1 else [pltpu.VMEM((tm, tn), F32)],
        compiler_params=_CP(dimension_semantics=("parallel", "parallel", "arbitrary")),
    )(*args)


def _mm_pieces(name, a, bs):
    m, k = a.shape
    n_b = len(bs)
    tm = _tile(m, MM_ROWS)

    def body(*refs):
        av = refs[0][...].astype(BF16)
        for b_ref, o_ref in zip(refs[1:1 + n_b], refs[1 + n_b:]):
            o_ref[...] = jnp.dot(av, b_ref[...].astype(BF16), preferred_element_type=F32)

    return pl.pallas_call(
        body, name=name, grid=(m // tm,),
        in_specs=[pl.BlockSpec((tm, k), lambda i: (i, 0))] + [pl.BlockSpec(b.shape, lambda i: (0, 0)) for b in bs],
        out_specs=[pl.BlockSpec((tm, b.shape[1]), lambda i: (i, 0)) for b in bs],
        out_shape=[jax.ShapeDtypeStruct((m, b.shape[1]), F32) for b in bs],
        compiler_params=_CP(dimension_semantics=("parallel",)),
    )(a, *bs)


def _mm_pieces_t(name, dps, ws):
    m, k = dps[0].shape[0], ws[0].shape[0]
    n_b = len(ws)
    tm = _tile(m, MM_ROWS)
    dn = (((1,), (1,)), ((), ()))

    def body(*refs):
        acc = None
        for d_ref, w_ref in zip(refs[:n_b], refs[n_b:2 * n_b]):
            part = lax.dot_general(d_ref[...].astype(BF16), w_ref[...].astype(BF16), dn, preferred_element_type=F32)
            acc = part if acc is None else acc + part
        refs[-1][...] = acc

    return pl.pallas_call(
        body, name=name, grid=(m // tm,),
        in_specs=[pl.BlockSpec((tm, d.shape[1]), lambda i: (i, 0)) for d in dps] + [pl.BlockSpec(w.shape, lambda i: (0, 0)) for w in ws],
        out_specs=pl.BlockSpec((tm, k), lambda i: (i, 0)), out_shape=jax.ShapeDtypeStruct((m, k), F32),
        compiler_params=_CP(dimension_semantics=("parallel",)),
    )(*dps, *ws)


def _mm_pieces_dw(name, u, dps):
    s, k = u.shape
    n_b = len(dps)
    tk = _tile(s, MM_KROWS)
    n_steps = s // tk
    dn = (((0,), (0,)), ((), ()))

    def body(*refs):
        u_ref, d_refs, o_refs, accs = refs[0], refs[1:1 + n_b], refs[1 + n_b:1 + 2 * n_b], refs[1 + 2 * n_b:]
        step = pl.program_id(0)
        uv = u_ref[...].astype(BF16)
        for d_ref, o_ref, acc in zip(d_refs, o_refs, accs):
            part = lax.dot_general(uv, d_ref[...].astype(BF16), dn, preferred_element_type=F32)

            @pl.when(step == 0)
            def _(acc=acc, part=part):
                acc[...] = part

            @pl.when(step > 0)
            def _(acc=acc, part=part):
                acc[...] += part

            @pl.when(step == n_steps - 1)
            def _(acc=acc, o_ref=o_ref):
                o_ref[...] = acc[...].astype(o_ref.dtype)

    return pl.pallas_call(
        body, name=name, grid=(n_steps,),
        in_specs=[pl.BlockSpec((tk, k), lambda i: (i, 0))] + [pl.BlockSpec((tk, d.shape[1]), lambda i: (i, 0)) for d in dps],
        out_specs=[pl.BlockSpec((k, d.shape[1]), lambda i: (0, 0)) for d in dps],
        out_shape=[jax.ShapeDtypeStruct((k, d.shape[1]), BF16) for d in dps],
        scratch_shapes=[pltpu.VMEM((k, d.shape[1]), F32) for d in dps],
        compiler_params=_CP(dimension_semantics=("arbitrary",)),
    )(u, *dps)


MM_DW_MERGE_COLS = 4096


class In:
    def __init__(self, arr, block, imap, kind="x", per_h=False, gdtype=F32, gshape=None, gimap=None, prefixed=False):
        self.arr, self.block, self.imap, self.kind, self.per_h, self.gdtype = arr, tuple(block), imap, kind, per_h, gdtype
        self.prefixed = prefixed
        self.gshape = tuple(gshape) if gshape is not None else tuple(arr.shape)
        self.gimap = gimap if gimap is not None else imap

    def spec(self, rev_g=None):
        imap = self.imap
        if rev_g is None:
            return pl.BlockSpec(self.block, lambda h, g: imap(h, g))
        return pl.BlockSpec(self.block, lambda h, g: imap(h, rev_g - 1 - g))


class Out:
    def __init__(self, shape, dtype, block, imap):
        self.shape, self.dtype, self.block, self.imap = tuple(shape), dtype, tuple(block), imap

    def spec(self, rev_g=None):
        imap = self.imap
        if rev_g is None:
            return pl.BlockSpec(self.block, lambda h, g: imap(h, g))
        return pl.BlockSpec(self.block, lambda h, g: imap(h, rev_g - 1 - g))


def _load_f32(ref, rows=None):
    v = ref[...] if rows is None else ref[0:rows]
    return v.astype(F32) if jnp.issubdtype(v.dtype, jnp.floating) else v


def _state_out(grid, shape):
    nd = len(shape)
    return Out(tuple(grid) + tuple(shape), F32, (None, None) + tuple(shape), lambda h, g: (h, g) + (0,) * nd)


def _carry(comm, grid, refs, n_in, n_out, n_scr):
    n_c = len(comm.ops) if comm is not None else 0
    n_s = len(comm.sem_shapes) if comm is not None else 0
    p = 0
    in_refs = refs[p:p + n_in]; p += n_in
    c_src = refs[p:p + n_c]; p += n_c
    out_refs = refs[p:p + n_out]; p += n_out
    c_dst = refs[p:p + n_c]; p += n_c
    scr = refs[p:p + n_scr]; p += n_scr
    c_sem = refs[p:p + n_s]
    step = pl.program_id(0) * grid[1] + pl.program_id(1)
    n_steps = grid[0] * grid[1]
    when = (step == 0, step == _forward_step(n_steps), step == n_steps - 1)
    return in_refs, out_refs, scr, (c_src, c_dst, c_sem), when


def _forward_step(n_steps):
    return max(0, min(n_steps - 2, (3 * n_steps) // 4))


def _carry_specs(comm):
    if comm is None:
        return [], [], [], [], []
    any_spec = pl.BlockSpec(memory_space=pl.ANY)
    n = len(comm.ops)
    return [any_spec] * n, list(comm.ops), [any_spec] * n, list(comm.out_shapes), list(comm.sem_shapes)


def _op_fwd(name, f, grid, ins, outs, state_shapes=(), comm=None, prefix_rows=None):
    assert prefix_rows is None or not state_shapes
    n_in, n_out, n_st = len(ins), len(outs), len(state_shapes)
    st_outs = [_state_out(grid, s) for s in state_shapes]

    def body(*refs):
        in_refs, o_refs, st_scr, cargs, (first, fwd_step, last) = _carry(comm, grid, refs, n_in, n_out + n_st, n_st)
        out_refs, sv_refs = o_refs[:n_out], o_refs[n_out:]
        if comm is not None:
            @pl.when(first)
            def _():
                comm.start(*cargs)

            @pl.when(fwd_step)
            def _():
                comm.forward(*cargs)
        g = pl.program_id(1)
        if n_st:
            @pl.when(g == 0)
            def _():
                for s in st_scr:
                    s[...] = jnp.zeros(s.shape, F32)

        def compute(rows, g=g):
            vals = [_load_f32(r, rows if i.prefixed else None) for r, i in zip(in_refs, ins)]
            sts = [s[...] for s in st_scr]
            o, ns = f(g, vals, sts)
            for r, v in zip(out_refs, o):
                r[...] = v.astype(r.dtype)
            for r, s in zip(sv_refs, sts):
                r[...] = s
            for s, v in zip(st_scr, ns):
                s[...] = v

        if prefix_rows is None:
            compute(None)
        else:
            per = grid[1] // len(prefix_rows)
            for lv, rows in enumerate(prefix_rows):
                pl.when(g // per == lv)(functools.partial(compute, rows, lv) if per == 1 else functools.partial(compute, rows))
        if comm is not None:
            @pl.when(last)
            def _():
                comm.finish(*cargs)

    all_outs = list(outs) + st_outs
    c_in_specs, c_args, c_out_specs, c_out_shapes, c_sems = _carry_specs(comm)
    res = pl.pallas_call(
        body, name=name, grid=tuple(grid), in_specs=[i.spec() for i in ins] + c_in_specs,
        out_specs=[o.spec() for o in all_outs] + c_out_specs,
        out_shape=[jax.ShapeDtypeStruct(o.shape, o.dtype) for o in all_outs] + c_out_shapes,
        scratch_shapes=[pltpu.VMEM(tuple(s), F32) for s in state_shapes] + c_sems,
        compiler_params=_CP(dimension_semantics=("arbitrary", "arbitrary")),
    )(*[i.arr for i in ins], *c_args)
    n_all = n_out + n_st
    return list(res[:n_out]), list(res[n_out:n_all]), list(res[n_all:])


def _op_bwd(name, f, grid, ins, outs, state_shapes, saved, douts, addto=None, comm=None, prefix_rows=None):
    n_in, n_out, n_st = len(ins), len(outs), len(state_shapes)
    n_g = grid[1]
    assert prefix_rows is None or all(i.prefixed and i.per_h for i in ins if i.kind == "p")
    addto = addto or {}
    diff = [k for k, i in enumerate(ins) if i.kind in ("x", "p")]
    add_idx = sorted(addto)
    st_ins = [In(s, o.block, o.imap, "c") for s, o in zip(saved, [_state_out(grid, s) for s in state_shapes])]
    dout_ins = [In(d, o.block, o.imap, "c") for d, o in zip(douts, outs)]
    add_ins = []
    for k in add_idx:
        i, a = ins[k], addto[k]
        blk = i.block if i.kind == "x" else i.block[:-2] + a.shape[-2:]
        add_ins.append(In(a, blk, i.gimap if i.kind == "x" else i.imap, "c"))
    g_outs = []
    for k in diff:
        i = ins[k]
        g_outs.append(Out(i.gshape, i.gdtype if i.kind == "x" else F32, i.block, i.gimap))

    def body(*refs):
        all_in, go_refs, ds_scr, cargs, (first_step, fwd_step, last_step) = _carry(comm, grid, refs, n_in + n_st + n_out + len(add_idx),
                                                                                    len(diff), n_st)
        if comm is not None:
            @pl.when(first_step)
            def _():
                comm.start(*cargs)

            @pl.when(fwd_step)
            def _():
                comm.forward(*cargs)
        p = 0
        in_refs = all_in[p:p + n_in]; p += n_in
        sv_refs = all_in[p:p + n_st]; p += n_st
        do_refs = all_in[p:p + n_out]; p += n_out
        ad_refs = all_in[p:p + len(add_idx)]
        hh = pl.program_id(0)
        step = pl.program_id(1)
        g = n_g - 1 - step
        if n_st:
            @pl.when(step == 0)
            def _():
                for s in ds_scr:
                    s[...] = jnp.zeros(s.shape, F32)

        def compute(rows, g=g):
            vals = [_load_f32(r, rows if i.prefixed else None) for r, i in zip(in_refs, ins)]
            sts = [r[...] for r in sv_refs]

            def fw(dvals, states):
                full = list(vals)
                for k, v in zip(diff, dvals):
                    full[k] = v
                o, ns = f(g, full, states)
                return list(o), list(ns)

            _, vjp = jax.vjp(fw, [vals[k] for k in diff], sts)
            cts = [r[...].astype(F32) for r in do_refs]
            dns = [s[...] for s in ds_scr]
            dvals, dsts = vjp((cts, dns))
            adds = dict(zip(add_idx, ad_refs))
            for k, r, dv in zip(diff, go_refs, dvals):
                i = ins[k]
                if i.kind == "x":
                    if k in adds:
                        dv = dv + adds[k][...].astype(F32)
                    r[...] = dv.astype(r.dtype)
                elif rows is not None:
                    r[0:rows] += dv
                else:
                    first = (step == 0) if i.per_h else jnp.logical_and(step == 0, hh == 0)

                    @pl.when(first)
                    def _(r=r, dv=dv, k=k):
                        r[...] = dv
                        if k in adds:
                            lead = adds[k].shape[0]
                            r[0:lead] += adds[k][...]

                    @pl.when(jnp.logical_not(first))
                    def _(r=r, dv=dv):
                        r[...] += dv
            for s, v in zip(ds_scr, dsts):
                s[...] = v

        if prefix_rows is None:
            compute(None)
        else:
            @pl.when(step == 0)
            def _():
                for k, r in zip(diff, go_refs):
                    if ins[k].kind == "p":
                        r[...] = jnp.zeros(r.shape, F32)
            per = n_g // len(prefix_rows)
            for lv, rows in enumerate(prefix_rows):
                pl.when(g // per == lv)(functools.partial(compute, rows, lv) if per == 1 else functools.partial(compute, rows))
        if comm is not None:
            @pl.when(last_step)
            def _():
                comm.finish(*cargs)

    all_ins = list(ins) + st_ins + dout_ins + add_ins
    c_in_specs, c_args, c_out_specs, c_out_shapes, c_sems = _carry_specs(comm)
    res = pl.pallas_call(
        body, name=name, grid=tuple(grid), in_specs=[i.spec(n_g) for i in all_ins] + c_in_specs,
        out_specs=[o.spec(n_g) for o in g_outs] + c_out_specs,
        out_shape=[jax.ShapeDtypeStruct(o.shape, o.dtype) for o in g_outs] + c_out_shapes,
        scratch_shapes=[pltpu.VMEM(tuple(s), F32) for s in state_shapes] + c_sems,
        compiler_params=_CP(dimension_semantics=("arbitrary", "arbitrary")),
    )(*[i.arr for i in all_ins], *c_args)
    return list(res[:len(g_outs)]), list(res[len(g_outs):])


class Op:
    def __init__(self, name, f, grid, ins, outs, state_shapes=(), prefix_rows=None):
        self.name, self.f, self.grid, self.ins, self.outs, self.state_shapes = name, f, grid, ins, outs, state_shapes
        self.prefix_rows = prefix_rows
        self.saved = None

    def fwd(self, comm=None):
        res, self.saved, self.fwd_comm_out = _op_fwd(self.name + "_fwd", self.f, self.grid, self.ins, self.outs, self.state_shapes, comm,
                                                     self.prefix_rows)
        return res

    def bwd(self, douts, addto=None, comm=None):
        res, self.bwd_comm_out = _op_bwd(self.name + "_bwd", self.f, self.grid, self.ins, self.outs, self.state_shapes, self.saved, douts,
                                         addto, comm, self.prefix_rows)
        return res


def _rows(arr, t, kind="x", gdtype=F32):
    return In(arr, (t, arr.shape[1]), lambda h, g: (g, 0), kind, gdtype=gdtype)


def _whole(arr, kind="p"):
    nd = arr.ndim
    return In(arr, arr.shape, lambda h, g: (0,) * nd, kind)


def _rows_out(s, n, t, dtype):
    return Out((s, n), dtype, (t, n), lambda h, g: (g, 0))


ROW_T = 256


def _rms_op(name, x, gain, out_dtype=BF16, gdtype=F32):
    s, n = x.shape

    def f(g, vals, sts):
        return [_rms(vals[0], vals[1])], []

    return Op(name, f, (1, s // ROW_T), [_rows(x, ROW_T, gdtype=gdtype), _whole(gain.reshape(1, n))], [_rows_out(s, n, ROW_T, out_dtype)])


def _mla_prep_op(qn, q1, q2, kn, kr, v, cos, sin):
    s = qn.shape[0]
    hd, half = MLA_HEADS, MLA_ROPE // 2

    def f(g, vals, sts):
        qn, q1, q2, kn, kr, v, cos, sin = vals
        cos_h, sin_h = jnp.tile(cos, (1, hd)), jnp.tile(sin, (1, hd))
        r1 = q1 * cos_h - q2 * sin_h
        r2 = q2 * cos_h + q1 * sin_h
        k1, k2 = kr[:, 0:half], kr[:, half:2 * half]
        kr1 = k1 * cos - k2 * sin
        kr2 = k2 * cos + k1 * sin
        zpad = jnp.zeros((qn.shape[0], LANE - MLA_QK), F32)
        qs, ks, vs = [], [], []
        for h in range(hd):
            a, b = h * MLA_NOPE, (h + 1) * MLA_NOPE
            c, d = h * half, (h + 1) * half
            qs.append(jnp.concatenate([qn[:, a:b], r1[:, c:d], r2[:, c:d], zpad], axis=1))
            ks.append(jnp.concatenate([kn[:, a:b], kr1, kr2, zpad], axis=1))
            vs.append(v[:, a:b])
        return [jnp.stack(qs, 0), jnp.stack(ks, 0), jnp.stack(vs, 0)], []

    ins = [_rows(qn, ROW_T, gdtype=BF16), _rows(q1, ROW_T, gdtype=BF16), _rows(q2, ROW_T, gdtype=BF16), _rows(kn, ROW_T, gdtype=BF16),
           _rows(kr, ROW_T, gdtype=BF16), _rows(v, ROW_T, gdtype=BF16), _rows(cos, ROW_T, "c"), _rows(sin, ROW_T, "c")]
    outs = [Out((hd, s, LANE), BF16, (hd, ROW_T, LANE), lambda h, g: (0, g, 0)),
            Out((hd, s, LANE), BF16, (hd, ROW_T, LANE), lambda h, g: (0, g, 0)),
            Out((hd, s, MLA_V), BF16, (hd, ROW_T, MLA_V), lambda h, g: (0, g, 0))]
    return Op("mla_prep", f, (1, s // ROW_T), ins, outs)


ATT_TQ = 256
ATT_LEVELS = 8


def _mla_attn_op(q, k, v):
    hd, s, _ = q.shape
    scale = MLA_QK ** -0.5

    def f(g, vals, sts):
        q, k, v = vals
        sc = _bdot_nt(q, k) * scale
        r = lax.broadcasted_iota(jnp.int32, sc.shape, 0) + g * ATT_TQ
        c = lax.broadcasted_iota(jnp.int32, sc.shape, 1)
        sc = jnp.where(r >= c, sc, -1e30)
        m = lax.stop_gradient(jnp.max(sc, axis=-1, keepdims=True))
        p = jnp.exp(sc - m)
        p = p * (1.0 / jnp.sum(p, axis=-1, keepdims=True))
        return [_bdot(p, v)], []

    ins = [In(q, (None, ATT_TQ, LANE), lambda h, g: (h, g, 0), "x", gdtype=BF16),
           In(k, (None, s, LANE), lambda h, g: (h, 0, 0), "p", per_h=True, prefixed=True),
           In(v, (None, s, MLA_V), lambda h, g: (h, 0, 0), "p", per_h=True, prefixed=True)]
    outs = [Out((hd, s, MLA_V), F32, (None, ATT_TQ, MLA_V), lambda h, g: (h, g, 0))]
    return Op("mla_attn", f, (hd, s // ATT_TQ), ins, outs, prefix_rows=[(lv + 1) * (s // ATT_LEVELS) for lv in range(ATT_LEVELS)])


def _mla_post_op(o, gate):
    hd, s, _ = o.shape

    def f(g, vals, sts):
        o, gate = vals
        cat = jnp.concatenate([o[h] for h in range(hd)], axis=1)
        return [cat * _silu(gate)], []

    ins = [In(o, (hd, ROW_T, MLA_V), lambda h, g: (0, g, 0), "x"), _rows(gate, ROW_T, gdtype=BF16)]
    return Op("mla_post", f, (1, s // ROW_T), ins, [_rows_out(s, hd * MLA_V, ROW_T, BF16)])


def _gla_gate_op(gk, w2, b):
    s = gk.shape[0]

    def f(g, vals, sts):
        gk, w2, b = vals
        return [jax.nn.log_sigmoid(_bdot(gk, w2) + b) / GLA_TAU], []

    ins = [_rows(gk, ROW_T, gdtype=BF16), _whole(w2), _whole(b)]
    return Op("gla_gate", f, (1, s // ROW_T), ins, [_rows_out(s, GLA_HEADS * GLA_DK, ROW_T, F32)])


def _gla_core_op(q, k, v, gate, la, g_o):
    s = q.shape[0]
    c, nh = GLA_CHUNK, GLA_HEADS

    def f(g, vals, sts):
        q, k, v, gate, la, g_o = vals
        tri = _tri(c)
        b = _cumsum_rows(la)
        b_last = jnp.sum(la, axis=0, keepdims=True)
        qt = q * (GLA_DK ** -0.5) * jnp.exp(b)
        kt = k * jnp.exp(-b)
        kd = k * jnp.exp(b_last - b)
        ys, new_sts = [], []
        for h in range(nh):
            ks, vs = slice(h * GLA_DK, (h + 1) * GLA_DK), slice(h * GLA_DV, (h + 1) * GLA_DV)
            att = jnp.where(tri, _bdot_nt(qt[:, ks], kt[:, ks]), 0.0)
            o = _bdot(att, v[:, vs]) + _bdot_nt(qt[:, ks], sts[h])
            new_sts.append(jnp.exp(b_last[:, ks]) * sts[h] + _bdot_tn(v[:, vs], kd[:, ks]))
            ys.append(_rms(o, g_o) * _silu(gate[:, vs]))
        return [jnp.concatenate(ys, axis=1)], new_sts

    ins = [_rows(q, c, gdtype=BF16), _rows(k, c, gdtype=BF16), _rows(v, c, gdtype=BF16), _rows(gate, c, gdtype=BF16), _rows(la, c), _whole(g_o)]
    outs = [_rows_out(s, nh * GLA_DV, c, BF16)]
    return Op("gla_core", f, (1, s // c), ins, outs, [(GLA_DV, GLA_DK)] * nh)


LRU_T = 256


def _lru_op(gate, u, conv_w, conv_b, w_a, b_a, w_x, b_x, lam):
    s, w = u.shape
    t = LRU_T

    def f(g, vals, sts):
        gate, u, cw, cb, w_a, b_a, w_x, b_x, lam = vals
        u_prev, h_prev = sts
        uc = cb
        for kk in range(CONV_W):
            uc = uc + cw[kk] * _shift_rows(u, u_prev, CONV_W - 1 - kk)
        ra, ri = [], []
        for n in range(LRU_BLOCKS):
            blk = uc[:, n * LRU_BLOCK:(n + 1) * LRU_BLOCK]
            ra.append(_bdot(blk, w_a[n]))
            ri.append(_bdot(blk, w_x[n]))
        r = jax.nn.sigmoid(jnp.concatenate(ra, axis=1) + b_a)
        i = jax.nn.sigmoid(jnp.concatenate(ri, axis=1) + b_x)
        log_a = -LRU_C * r * jax.nn.softplus(-lam)
        a = jnp.exp(log_a)
        bb = jnp.sqrt(_one_minus_exp(2.0 * log_a)) * (i * uc)
        zero = jnp.zeros_like(a)
        sh = 1
        while sh < t:
            a_s = _shift_rows(a - 1.0, zero, sh) + 1.0
            b_s = _shift_rows(bb, zero, sh)
            bb = a * b_s + bb
            a = a * a_s
            sh *= 2
        hs = bb + a * h_prev
        last = (lax.broadcasted_iota(jnp.int32, hs.shape, 0) == t - 1).astype(F32)
        h_last = jnp.sum(hs * last, axis=0, keepdims=True)
        return [hs * _silu(gate)], [u, h_last]

    ins = [_rows(gate, t, gdtype=BF16), _rows(u, t, gdtype=BF16), _whole(conv_w), _whole(conv_b), _whole(w_a), _whole(b_a), _whole(w_x),
           _whole(b_x), _whole(lam)]
    return Op("lru_core", f, (1, s // t), ins, [_rows_out(s, w, t, BF16)], [(t, w), (1, w)])


def _ssd_conv_op(xbc, conv_w, conv_b):
    s, w = xbc.shape
    t = ROW_T
    n_x, n_b = SSD_INNER, SSD_GROUPS * SSD_STATE

    def f(g, vals, sts):
        xbc, cw, cb = vals
        acc = cb
        for kk in range(CONV_W):
            acc = acc + cw[kk] * _shift_rows(xbc, sts[0], CONV_W - 1 - kk)
        y = _silu(acc)
        return [y[:, :n_x], y[:, n_x:n_x + n_b], y[:, n_x + n_b:]], [xbc]

    ins = [_rows(xbc, t, gdtype=BF16), _whole(conv_w), _whole(conv_b)]
    outs = [_rows_out(s, n_x, t, F32), _rows_out(s, n_b, t, BF16), _rows_out(s, n_b, t, BF16)]
    return Op("ssd_conv", f, (1, s // t), ins, outs, [(t, w)])


SSD_L = 512


def _ssd_core_op(x, bm, cm, z, dt, dt_bias, a_log, d_skip, g_norm):
    s = x.shape[0]
    c, hg, p = SSD_L, SSD_HPG, SSD_P
    gw = hg * p

    def f(g, vals, sts):
        x, bm, cm, z, dtr, dt_bias, a_log, d_skip, g_norm = vals
        tri = _tri(c)
        dt = jax.nn.softplus(dtr + dt_bias)
        da = dt * (-jnp.exp(a_log))
        cs = _cumsum_rows(da)
        cs_last = jnp.sum(da, axis=0, keepdims=True)
        cs_t = jnp.transpose(jnp.concatenate([cs, jnp.zeros((c, LANE - hg), F32)], axis=1))
        cb = _bdot_nt(cm, bm)
        ys, new_st = [], []
        for h in range(hg):
            cs_h = cs[:, h:h + 1]
            cs_row = cs_t[h:h + 1, :]
            seg = jnp.where(tri, cs_h - cs_row, 0.0)
            lmat = jnp.where(tri, jnp.exp(seg), 0.0)
            x_h = x[:, h * p:(h + 1) * p]
            xdt = x_h * dt[:, h:h + 1]
            y_diag = _bdot(cb * lmat, xdt)
            decay = jnp.exp(cs_last[:, h:h + 1] - cs_h)
            states = _bdot_tn(xdt * decay, bm)
            y_off = _bdot_nt(cm, sts[h]) * jnp.exp(cs_h)
            new_st.append(jnp.exp(cs_last[:, h:h + 1]) * sts[h] + states)
            ys.append(y_diag + y_off + d_skip[:, h:h + 1] * x_h)
        y = jnp.concatenate(ys, axis=1) * _silu(z)
        return [_rms(y, g_norm)], new_st

    ins = [In(x, (c, gw), lambda h, g: (g, h), "x"), In(bm, (c, SSD_STATE), lambda h, g: (g, h), "x", gdtype=BF16),
           In(cm, (c, SSD_STATE), lambda h, g: (g, h), "x", gdtype=BF16), In(z, (c, gw), lambda h, g: (g, h), "x", gdtype=BF16),
           In(dt, (None, c, hg), lambda h, g: (h, g, 0), "x"),
           In(dt_bias, (None, 1, hg), lambda h, g: (h, 0, 0), "p", per_h=True),
           In(a_log, (None, 1, hg), lambda h, g: (h, 0, 0), "p", per_h=True),
           In(d_skip, (None, 1, hg), lambda h, g: (h, 0, 0), "p", per_h=True),
           In(g_norm, (1, gw), lambda h, g: (0, h), "p", per_h=True)]
    outs = [Out((s, SSD_INNER), BF16, (c, gw), lambda h, g: (g, h))]
    return Op("ssd_core", f, (SSD_GROUPS, s // c), ins, outs, [(p, SSD_STATE)] * hg)


def _loss_op(h, target, final_g):
    s, n = h.shape
    t = ROW_T
    n_g = s // t

    def body(h_ref, t_ref, g_ref, loss_ref, dh_ref, dg_ref):
        step = pl.program_id(0)

        def lossf(hv, gv):
            err = _rms(hv, gv) - t_ref[...]
            return 0.5 * jnp.sum(jnp.mean(err * err, axis=-1))

        l, (dh, dg) = jax.value_and_grad(lossf, argnums=(0, 1))(h_ref[...], g_ref[...])
        dh_ref[...] = dh

        @pl.when(step == 0)
        def _():
            loss_ref[...] = jnp.zeros(loss_ref.shape, F32)
            dg_ref[...] = jnp.zeros(dg_ref.shape, F32)

        loss_ref[...] += jnp.full(loss_ref.shape, l, F32)
        dg_ref[...] += dg

    row = pl.BlockSpec((t, n), lambda g: (g, 0))
    one = pl.BlockSpec((1, n), lambda g: (0, 0))
    return pl.pallas_call(
        body, name="loss_head", grid=(n_g,), in_specs=[row, row, one],
        out_specs=[pl.BlockSpec((1, LANE), lambda g: (0, 0)), row, one],
        out_shape=[jax.ShapeDtypeStruct((1, LANE), F32), jax.ShapeDtypeStruct((s, n), F32), jax.ShapeDtypeStruct((1, n), F32)],
        compiler_params=_CP(dimension_semantics=("arbitrary",)),
    )(h, target, final_g.reshape(1, n))


def _pad_cols(w, n):
    return jnp.pad(w, ((0, 0), (0, n - w.shape[1])))


def _pad_rows(w, n):
    return jnp.pad(w, ((0, n - w.shape[0]), (0, 0)))


def _proj_bwd(tag, u, dps, ws):
    du = _mm_pieces_t(f"{tag}_du", dps, ws)
    if sum(dp.shape[1] for dp in dps) <= MM_DW_MERGE_COLS:
        dws = _mm_pieces_dw(f"{tag}_dw", u, dps)
    else:
        dws = [_mm(f"{tag}_dw{i}", u, dp, ta=True, out_dtype=BF16) for i, dp in enumerate(dps)]
    return du, dws


def _mla_layer(h, norm_g, w, cos, sin, fwd_comm=None, late_w_out=None):
    bf = lambda a: a.astype(BF16)
    w_in, w_uq, w_ukv = w["mla_w_in"], w["mla_w_uq"], w["mla_w_ukv"]
    a0, a1, a2 = MLA_Q_RANK, MLA_Q_RANK + MLA_KV_RANK, MLA_Q_RANK + MLA_KV_RANK + MLA_ROPE
    w_cq, w_ckv, w_kr, w_g = bf(w_in[:, :a0]), bf(w_in[:, a0:a1]), bf(_pad_cols(w_in[:, a1:a2], LANE)), bf(w_in[:, a2:])
    uq = w_uq.reshape(MLA_Q_RANK, MLA_HEADS, MLA_QK)
    half = MLA_ROPE // 2
    w_qn = bf(uq[:, :, :MLA_NOPE].reshape(MLA_Q_RANK, -1))
    w_q1 = bf(uq[:, :, MLA_NOPE:MLA_NOPE + half].reshape(MLA_Q_RANK, -1))
    w_q2 = bf(uq[:, :, MLA_NOPE + half:].reshape(MLA_Q_RANK, -1))
    ukv = w_ukv.reshape(MLA_KV_RANK, MLA_HEADS, MLA_NOPE + MLA_V)
    w_kn = bf(ukv[:, :, :MLA_NOPE].reshape(MLA_KV_RANK, -1))
    w_v = bf(ukv[:, :, MLA_NOPE:].reshape(MLA_KV_RANK, -1))

    n0 = _rms_op("mla_norm", h, norm_g)
    u, = n0.fwd()
    cq, ckv, kr, gate = _mm_pieces("mla_in", u, (w_cq, w_ckv, w_kr, w_g))
    nq = _rms_op("mla_qnorm", cq, w["mla_g_q"], gdtype=BF16)
    nkv = _rms_op("mla_kvnorm", ckv, w["mla_g_kv"], gdtype=BF16)
    qn_, = nq.fwd()
    kvn_, = nkv.fwd()
    qn, q1, q2 = _mm_pieces("mla_uq", qn_, (w_qn, w_q1, w_q2))
    kn, v = _mm_pieces("mla_ukv", kvn_, (w_kn, w_v))
    prep = _mla_prep_op(qn, q1, q2, kn, kr, v, cos, sin)
    qh, kh, vh = prep.fwd()
    attn = _mla_attn_op(qh, kh, vh)
    o, = attn.fwd(fwd_comm)
    w_out = bf(w["mla_w_out"]) if late_w_out is None else late_w_out(attn.fwd_comm_out)
    post = _mla_post_op(o, gate)
    y, = post.fwd()
    h_out = _mm("mla_out", y, w_out, add=h)

    def bwd(dh, make_comm=None):
        dy = _mm("mla_out_dy", dh, w_out, tb=True, out_dtype=BF16)
        d_w_out = _mm("mla_out_dw", y, dh, ta=True, out_dtype=BF16)
        do, dgate = post.bwd([dy])
        dqh, dkh, dvh = attn.bwd([do], comm=None if make_comm is None else make_comm(d_w_out))
        dqn, dq1, dq2, dkn, dkr, dv = prep.bwd([dqh, dkh, dvh])
        dqn_, d_uq = _proj_bwd("mla_uq", qn_, (dqn, dq1, dq2), (w_qn, w_q1, w_q2))
        dkvn_, d_ukv = _proj_bwd("mla_ukv", kvn_, (dkn, dv), (w_kn, w_v))
        dcq, d_g_q = nq.bwd([dqn_])
        dckv, d_g_kv = nkv.bwd([dkvn_])
        du, d_in = _proj_bwd("mla_in", u, (dcq, dckv, dkr, dgate), (w_cq, w_ckv, w_kr, w_g))
        dh_in, d_norm = n0.bwd([du], addto={0: dh})
        shp = (MLA_Q_RANK, MLA_HEADS, -1)
        g_uq = jnp.concatenate([d_uq[0].reshape(shp), d_uq[1].reshape(shp), d_uq[2].reshape(shp)], axis=2).reshape(MLA_Q_RANK, -1)
        shp = (MLA_KV_RANK, MLA_HEADS, -1)
        g_ukv = jnp.concatenate([d_ukv[0].reshape(shp), d_ukv[1].reshape(shp)], axis=2).reshape(MLA_KV_RANK, -1)
        g_in = jnp.concatenate([d_in[0], d_in[1], d_in[2][:, :MLA_ROPE], d_in[3]], axis=1)
        return dh_in, d_norm, {"mla_w_in": g_in, "mla_g_q": d_g_q.reshape(-1), "mla_w_uq": g_uq, "mla_g_kv": d_g_kv.reshape(-1),
                               "mla_w_ukv": g_ukv, "mla_w_out": d_w_out}, attn.bwd_comm_out

    return h_out, bwd, attn.fwd_comm_out


def _gla_layer(h, norm_g, w, fwd_comm=None):
    bf = lambda a: a.astype(BF16)
    w_in = w["gla_w_in"]
    nk, nv = GLA_HEADS * GLA_DK, GLA_HEADS * GLA_DV
    cuts = (0, nk, 2 * nk, 2 * nk + nv, 2 * nk + 2 * nv)
    w_q, w_k, w_v, w_g = (bf(w_in[:, cuts[i]:cuts[i + 1]]) for i in range(4))
    w_gk = bf(_pad_cols(w_in[:, cuts[4]:], LANE))
    w2 = _pad_rows(w["gla_w_gk2"], LANE)
    b_gk = w["gla_b_gk"].reshape(1, -1)
    g_o = w["gla_g_o"].reshape(1, -1)
    w_out = bf(w["gla_w_out"])

    n0 = _rms_op("gla_norm", h, norm_g)
    u, = n0.fwd()
    q, k, v, gate, gk = _mm_pieces("gla_in", u, (w_q, w_k, w_v, w_g, w_gk))
    gop = _gla_gate_op(gk, w2, b_gk)
    la, = gop.fwd()
    core = _gla_core_op(q, k, v, gate, la, g_o)
    y, = core.fwd(fwd_comm)
    h_out = _mm("gla_out", y, w_out, add=h)

    def bwd(dh, make_comm=None):
        dy = _mm("gla_out_dy", dh, w_out, tb=True, out_dtype=BF16)
        d_w_out = _mm("gla_out_dw", y, dh, ta=True, out_dtype=BF16)
        dq, dk, dv, dgate, dla, d_g_o = core.bwd([dy], comm=None if make_comm is None else make_comm(d_w_out))
        dgk, d_w2, d_b = gop.bwd([dla])
        du, d_in = _proj_bwd("gla_in", u, (dq, dk, dv, dgate, dgk), (w_q, w_k, w_v, w_g, w_gk))
        dh_in, d_norm = n0.bwd([du], addto={0: dh})
        g_in = jnp.concatenate([d_in[0], d_in[1], d_in[2], d_in[3], d_in[4][:, :GLA_RANK]], axis=1)
        return dh_in, d_norm, {"gla_w_in": g_in, "gla_w_gk2": d_w2[:GLA_RANK], "gla_b_gk": d_b.reshape(-1), "gla_g_o": d_g_o.reshape(-1),
                               "gla_w_out": d_w_out}, core.bwd_comm_out

    return h_out, bwd, core.fwd_comm_out


def _lru_layer(h, norm_g, w, fwd_comm=None):
    bf = lambda a: a.astype(BF16)
    w_in = w["lru_w_in"]
    w_g, w_u = bf(w_in[:, :LRU_WIDTH]), bf(w_in[:, LRU_WIDTH:])
    row = lambda a: a.reshape(1, -1)
    w_out = bf(w["lru_w_out"])

    n0 = _rms_op("lru_norm", h, norm_g)
    u_, = n0.fwd()
    gate, u = _mm_pieces("lru_in", u_, (w_g, w_u))
    core = _lru_op(gate, u, w["lru_conv_w"].reshape(CONV_W, 1, -1), row(w["lru_conv_b"]), w["lru_w_a"], row(w["lru_b_a"]), w["lru_w_x"],
                   row(w["lru_b_x"]), row(w["lru_lam"]))
    y, = core.fwd(fwd_comm)
    h_out = _mm("lru_out", y, w_out, add=h)

    def bwd(dh, make_comm=None):
        dy = _mm("lru_out_dy", dh, w_out, tb=True, out_dtype=BF16)
        d_w_out = _mm("lru_out_dw", y, dh, ta=True, out_dtype=BF16)
        dgate, du, d_cw, d_cb, d_wa, d_ba, d_wx, d_bx, d_lam = core.bwd([dy], comm=None if make_comm is None else make_comm(d_w_out))
        du_, d_in = _proj_bwd("lru_in", u_, (dgate, du), (w_g, w_u))
        dh_in, d_norm = n0.bwd([du_], addto={0: dh})
        return dh_in, d_norm, {"lru_w_in": jnp.concatenate(d_in, axis=1), "lru_conv_w": d_cw.reshape(CONV_W, -1), "lru_conv_b": d_cb.reshape(-1),
                               "lru_w_a": d_wa, "lru_b_a": d_ba.reshape(-1), "lru_w_x": d_wx, "lru_b_x": d_bx.reshape(-1),
                               "lru_lam": d_lam.reshape(-1), "lru_w_out": d_w_out}, core.bwd_comm_out

    return h_out, bwd, core.fwd_comm_out


def _ssd_layer(h, norm_g, w, fwd_comm=None, late_w_out=None):
    bf = lambda a: a.astype(BF16)
    s = h.shape[0]
    w_in = w["ssd_w_in"]
    conv_dim = SSD_INNER + 2 * SSD_GROUPS * SSD_STATE
    w_z, w_xbc = bf(w_in[:, :SSD_INNER]), bf(w_in[:, SSD_INNER:SSD_INNER + conv_dim])
    w_dt = bf(_pad_cols(w_in[:, SSD_INNER + conv_dim:], LANE))
    grp = lambda a: a.reshape(SSD_GROUPS, 1, SSD_HPG)

    n0 = _rms_op("ssd_norm", h, norm_g)
    u, = n0.fwd()
    z, xbc, dtp = _mm_pieces("ssd_in", u, (w_z, w_xbc, w_dt))
    conv = _ssd_conv_op(xbc, w["ssd_conv_w"].reshape(CONV_W, 1, -1), w["ssd_conv_b"].reshape(1, -1))
    x, bm, cm = conv.fwd()
    dt = dtp[:, :SSD_HEADS].reshape(s, SSD_GROUPS, SSD_HPG).transpose(1, 0, 2)
    core = _ssd_core_op(x, bm, cm, z, dt, grp(w["ssd_dt_bias"]), grp(w["ssd_a_log"]), grp(w["ssd_d"]), w["ssd_g_norm"].reshape(1, -1))
    y, = core.fwd(fwd_comm)
    w_out = bf(w["ssd_w_out"]) if late_w_out is None else late_w_out(core.fwd_comm_out)
    h_out = _mm("ssd_out", y, w_out, add=h)

    def bwd(dh, make_comm=None):
        dy = _mm("ssd_out_dy", dh, w_out, tb=True, out_dtype=BF16)
        d_w_out = _mm("ssd_out_dw", y, dh, ta=True, out_dtype=BF16)
        dx, dbm, dcm, dz, ddt, d_dtb, d_alog, d_d, d_gn = core.bwd([dy], comm=None if make_comm is None else make_comm(d_w_out))
        dxbc, d_cw, d_cb = conv.bwd([dx, dbm, dcm])
        ddtp = _pad_cols(ddt.transpose(1, 0, 2).reshape(s, SSD_HEADS), LANE).astype(BF16)
        du, d_in = _proj_bwd("ssd_in", u, (dz, dxbc, ddtp), (w_z, w_xbc, w_dt))
        dh_in, d_norm = n0.bwd([du], addto={0: dh})
        g_in = jnp.concatenate([d_in[0], d_in[1], d_in[2][:, :SSD_HEADS]], axis=1)
        return dh_in, d_norm, {"ssd_w_in": g_in, "ssd_conv_w": d_cw.reshape(CONV_W, -1), "ssd_conv_b": d_cb.reshape(-1),
                               "ssd_dt_bias": d_dtb.reshape(-1), "ssd_a_log": d_alog.reshape(-1), "ssd_d": d_d.reshape(-1),
                               "ssd_g_norm": d_gn.reshape(-1), "ssd_w_out": d_w_out}, core.bwd_comm_out

    return h_out, bwd


def _rope_tables(positions):
    inv_freq = ROPE_THETA ** (-jnp.arange(0, MLA_ROPE, 2, dtype=F32) / MLA_ROPE)
    ang = positions.astype(F32)[:, None] * inv_freq
    return jnp.cos(ang), jnp.sin(ang)


WEIGHTS = ["norm_g", "final_g", "mla_w_in", "mla_g_q", "mla_w_uq", "mla_g_kv", "mla_w_ukv", "mla_w_out", "gla_w_in", "gla_w_gk2", "gla_b_gk",
           "gla_g_o", "gla_w_out", "lru_w_in", "lru_conv_w", "lru_conv_b", "lru_w_a", "lru_b_a", "lru_w_x", "lru_b_x", "lru_lam", "lru_w_out",
           "ssd_w_in", "ssd_conv_w", "ssd_conv_b", "ssd_dt_bias", "ssd_a_log", "ssd_d", "ssd_g_norm", "ssd_w_out"]
BIG = ["mla_w_in", "mla_w_uq", "mla_w_ukv", "mla_w_out", "gla_w_in", "gla_w_out", "lru_w_in", "lru_w_out", "ssd_w_in", "ssd_w_out"]
SMALL = ["gla_w_gk2", "gla_b_gk", "gla_g_o", "lru_conv_w", "lru_conv_b", "lru_b_a", "lru_b_x", "lru_lam", "ssd_conv_w", "ssd_conv_b", "ssd_g_norm"]
REPL = ["norm_g", "final_g", "mla_g_q", "mla_g_kv", "lru_w_a", "lru_w_x", "ssd_dt_bias", "ssd_a_log", "ssd_d"]
REPL_EARLY = ["lru_w_a", "lru_w_x"]
REPL_LATE = [n for n in REPL if n not in REPL_EARLY]
N_CHIPS, N_DEV = 4, 8
PACK_W = 1024
ADAM_ROWS = 256
SMALL_ROWS = 64


def _shard_axis(name):
    return 0 if name.endswith("_w_out") else -1


def _pack(arrs, dtype, row_mult):
    flat = jnp.concatenate([a.reshape(-1).astype(dtype) for a in arrs])
    per = PACK_W * row_mult
    total = -(-flat.shape[0] // per) * per
    return jnp.pad(flat, (0, total - flat.shape[0])).reshape(-1, PACK_W)


def _unpack(buf, shapes):
    flat = buf.reshape(-1)
    out, off = [], 0
    for s in shapes:
        n = math.prod(s)
        out.append(flat[off:off + n].reshape(s))
        off += n
    return out


def _mesh_pos():
    return lax.axis_index("x"), lax.axis_index("y"), lax.axis_index("c")


class GatherComm:
    def __init__(self, ops):
        self.ops = list(ops)
        n = len(self.ops)
        assert all(o.ndim == 2 and o.shape[0] % 32 == 0 for o in self.ops), [o.shape for o in self.ops]
        self.out_shapes = [jax.ShapeDtypeStruct((N_CHIPS,) + o.shape, o.dtype) for o in self.ops]
        self.sem_shapes = [pltpu.SemaphoreType.DMA((6 * n,)), pltpu.SemaphoreType.DMA((6 * n,)), pltpu.SemaphoreType.DMA((n,))]

    def _copies(self, srcs, dsts, sems):
        send_sems, recv_sems, local_sems = sems
        n = len(self.ops)
        x, y, c = _mesh_pos()
        me_id, sibling = (x, y, c), (x, y, 1 - c)
        chips = [(1 - x, y), (x, 1 - y), (1 - x, 1 - y)]
        mine = 2 * x + y

        def half(i, cc):
            h = self.ops[i].shape[0] // 2
            return pl.ds(cc * h, h)

        def copy(i, k, src, slot, cc, to):
            return pltpu.make_async_remote_copy(src_ref=src, dst_ref=dsts[i].at[slot, half(i, cc)], send_sem=send_sems.at[i * 6 + k],
                                                recv_sem=recv_sems.at[i * 6 + k], device_id=to, device_id_type=pl.DeviceIdType.MESH)

        local = [pltpu.make_async_copy(srcs[i], dsts[i].at[mine], local_sems.at[i]) for i in range(n)]
        first, ici_recvs, passed, sib_recvs = [], [], [], []
        for i in range(n):
            my_half = srcs[i].at[half(i, c)]
            for k, (px, py) in enumerate(chips):
                slot = 2 * px + py
                first.append(copy(i, k, my_half, mine, c, (px, py, c)))
                ici_recvs.append(copy(i, k, my_half, slot, c, me_id))
                passed.append(copy(i, 3 + k, dsts[i].at[slot, half(i, c)], slot, c, sibling))
                sib_recvs.append(copy(i, 3 + k, my_half, slot, 1 - c, me_id))
        return local, first, ici_recvs, passed, sib_recvs

    def start(self, srcs, dsts, sems):
        local, first, _, _, _ = self._copies(srcs, dsts, sems)
        for cp in local + first:
            cp.start()

    def forward(self, srcs, dsts, sems):
        _, _, ici_recvs, passed, _ = self._copies(srcs, dsts, sems)
        for rc, fw in zip(ici_recvs, passed):
            rc.wait_recv()
            fw.start()

    def finish(self, srcs, dsts, sems):
        local, first, _, passed, sib_recvs = self._copies(srcs, dsts, sems)
        for cp in sib_recvs:
            cp.wait_recv()
        for cp in first + passed:
            cp.wait_send()
        for cp in local:
            cp.wait()


class ExchangeComm:
    def __init__(self, chip_ops, all_ops=()):
        self.ops = list(chip_ops) + list(all_ops)
        self.per_chip = (True,) * len(chip_ops) + (False,) * len(all_ops)
        n = len(self.ops)
        self.out_shapes = [jax.ShapeDtypeStruct((N_DEV,) + o.shape[-2:], o.dtype) for o in self.ops]
        self.sem_shapes = [pltpu.SemaphoreType.DMA((7 * n,)), pltpu.SemaphoreType.DMA((7 * n,)), pltpu.SemaphoreType.DMA((n,))]

    def _copies(self, srcs, dsts, sems):
        send_sems, recv_sems, local_sems = sems
        n, per_chip = len(self.ops), self.per_chip
        x, y, c = _mesh_pos()
        me_id, sibling = (x, y, c), (x, y, 1 - c)
        chips = [(1 - x, y), (x, 1 - y), (1 - x, 1 - y)]

        def dev(px, py, pc):
            return 4 * px + 2 * py + pc

        def part(i, px, py):
            return srcs[i].at[2 * px + py] if per_chip[i] else srcs[i]

        def copy(i, k, src, slot, to):
            return pltpu.make_async_remote_copy(src_ref=src, dst_ref=dsts[i].at[slot], send_sem=send_sems.at[i * 7 + k],
                                                recv_sem=recv_sems.at[i * 7 + k], device_id=to, device_id_type=pl.DeviceIdType.MESH)

        me = dev(x, y, c)
        local = [pltpu.make_async_copy(part(i, x, y), dsts[i].at[me], local_sems.at[i]) for i in range(n)]
        first, ici_recvs, passed, sib_recvs = [], [], [], []
        for i in range(n):
            first.append(copy(i, 0, part(i, x, y), me, sibling))
            first += [copy(i, 1 + k, part(i, px, py), me, (px, py, c)) for k, (px, py) in enumerate(chips)]
            sib_recvs.append(copy(i, 0, part(i, x, y), dev(x, y, 1 - c), me_id))
            for k, (px, py) in enumerate(chips):
                slot = dev(px, py, c)
                ici_recvs.append(copy(i, 1 + k, part(i, x, y), slot, me_id))
                passed.append(copy(i, 4 + k, dsts[i].at[slot], slot, sibling))
                sib_recvs.append(copy(i, 4 + k, part(i, x, y), dev(px, py, 1 - c), me_id))
        return local, first, ici_recvs, passed, sib_recvs

    def start(self, srcs, dsts, sems):
        local, first, _, _, _ = self._copies(srcs, dsts, sems)
        for cp in local + first:
            cp.start()

    def forward(self, srcs, dsts, sems):
        _, _, ici_recvs, passed, _ = self._copies(srcs, dsts, sems)
        for rc, fw in zip(ici_recvs, passed):
            rc.wait_recv()
            fw.start()

    def finish(self, srcs, dsts, sems):
        local, first, _, passed, sib_recvs = self._copies(srcs, dsts, sems)
        for cp in sib_recvs:
            cp.wait_recv()
        for cp in first + passed:
            cp.wait_send()
        for cp in local:
            cp.wait()


def _run_comm(name, comm):
    n = len(comm.ops)

    def body(*refs):
        srcs, dsts, sems = refs[:n], refs[n:2 * n], refs[2 * n:]
        comm.start(srcs, dsts, sems)
        comm.forward(srcs, dsts, sems)
        comm.finish(srcs, dsts, sems)

    any_spec = pl.BlockSpec(memory_space=pl.ANY)
    return pl.pallas_call(body, name=name, in_specs=[any_spec] * n, out_specs=[any_spec] * n, out_shape=comm.out_shapes,
                          scratch_shapes=comm.sem_shapes)(*comm.ops)


def _adamw(name, parts, w, m, v, lead=False, comm=None):
    plist = list(parts) if isinstance(parts, (list, tuple)) else [parts]
    n_p = len(plist)
    rows, cols = w.shape[-2:]
    t = next(c for c in (ADAM_ROWS, ADAM_ROWS // 2, SMALL_ROWS) if all(p.shape[1] % c == 0 for p in plist))
    starts = [sum(p.shape[1] for p in plist[:k]) // t for k in range(n_p)]
    counts = [p.shape[1] // t for p in plist]
    assert sum(p.shape[1] for p in plist) == rows, (name, rows)
    c1 = 1.0 - ADAM_B1 ** ADAM_STEP
    c2 = 1.0 - ADAM_B2 ** ADAM_STEP

    n_c = len(comm.ops) if comm is not None else 0
    n_steps = rows // t

    def body(*refs):
        p_refs = refs[:n_p]
        w_ref, m_ref, v_ref = refs[n_p:n_p + 3]
        c_src = refs[n_p + 3:n_p + 3 + n_c]
        g_ref, d_ref, nm_ref, nv_ref = refs[n_p + 3 + n_c:n_p + 7 + n_c]
        cargs = (c_src, refs[n_p + 7 + n_c:n_p + 7 + 2 * n_c], refs[n_p + 7 + 2 * n_c:])
        if comm is not None:
            @pl.when(pl.program_id(0) == 0)
            def _():
                comm.start(*cargs)

            @pl.when(pl.program_id(0) == _forward_step(n_steps))
            def _():
                comm.forward(*cargs)
        g = None
        for k, p_ref in enumerate(p_refs):
            gk = p_ref[0].astype(F32)
            for d in range(1, N_DEV):
                gk = gk + p_ref[d].astype(F32)
            g = gk if g is None else jnp.where(pl.program_id(0) >= starts[k], gk, g)
        nm = ADAM_B1 * m_ref[...] + (1.0 - ADAM_B1) * g
        nv = ADAM_B2 * v_ref[...] + (1.0 - ADAM_B2) * (g * g)
        g_ref[...] = g
        nm_ref[...] = nm
        nv_ref[...] = nv
        d_ref[...] = -ADAM_LR * ((nm / c1) / (jnp.sqrt(nv / c2) + ADAM_EPS) + ADAM_WD * w_ref[...])
        if comm is not None:
            @pl.when(pl.program_id(0) == n_steps - 1)
            def _():
                comm.finish(*cargs)

    row = pl.BlockSpec((None, t, cols), lambda i: (0, i, 0)) if lead else pl.BlockSpec((t, cols), lambda i: (i, 0))
    c_in_specs, c_args, c_out_specs, c_out_shapes, c_sems = _carry_specs(comm)
    res = pl.pallas_call(
        body, name=name, grid=(n_steps,),
        in_specs=[pl.BlockSpec((N_DEV, t, cols), lambda i, lo=lo, n=n: (0, jnp.clip(i - lo, 0, n - 1), 0)) for lo, n in zip(starts, counts)]
        + [row, row, row] + c_in_specs,
        out_specs=[row] * 4 + c_out_specs, out_shape=[jax.ShapeDtypeStruct(w.shape, F32)] * 4 + c_out_shapes, scratch_shapes=c_sems,
        compiler_params=_CP(dimension_semantics=("arbitrary" if comm is not None else "parallel",)),
    )(*plist, w, m, v, *c_args)
    return list(res[:4]), list(res[4:])


def _train_step(x, positions, target, wts, ms, vs, raw):
    small_shapes = [wts[n].shape for n in SMALL]

    big_of = {tag: [n for n in BIG if n.startswith(tag)] for tag in ("mla", "gla", "lru", "ssd")}
    full = {n: wts[n] for n in REPL}

    def gather_comm(names, extra=()):
        return GatherComm([wts[n].astype(BF16) for n in names] + list(extra))

    def assemble(names, got):
        for k, n in enumerate(names):
            full[n] = jnp.concatenate([got[k][j] for j in range(N_CHIPS)], axis=_shard_axis(n))

    first = [n for n in big_of["mla"] if n != "mla_w_out"]
    got = _run_comm("gather_first", gather_comm(first, [_pack([wts[n] for n in SMALL], F32, SMALL_ROWS)]))
    assemble(first, got)
    per_chip_small = [_unpack(got[-1][j], small_shapes) for j in range(N_CHIPS)]
    for k, n in enumerate(SMALL):
        full[n] = jnp.concatenate([per_chip_small[j][k] for j in range(N_CHIPS)], axis=_shard_axis(n))

    cos, sin = _rope_tables(positions)
    ng = full["norm_g"]
    behind_attn = ["mla_w_out"] + big_of["gla"] + big_of["lru"]

    def mla_w_out(got):
        assemble(behind_attn, got)
        return full["mla_w_out"].astype(BF16)

    h1, b0, _ = _mla_layer(x, ng[0], full, cos, sin, fwd_comm=gather_comm(behind_attn), late_w_out=mla_w_out)

    def joined(got_k, axis):
        return jnp.concatenate([got_k[j] for j in range(N_CHIPS)], axis=axis)

    ssd_in = wts["ssd_w_in"].astype(BF16)
    half = ssd_in.shape[0] // 2
    h2, b1, got = _gla_layer(h1, ng[1], full, fwd_comm=GatherComm([ssd_in[:half]]))
    top = joined(got[0], -1)
    h3, b2, got = _lru_layer(h2, ng[2], full, fwd_comm=GatherComm([ssd_in[half:]]))
    full["ssd_w_in"] = jnp.concatenate([top, joined(got[0], -1)], axis=0)
    h4, b3 = _ssd_layer(h3, ng[3], full, fwd_comm=gather_comm(["ssd_w_out"]), late_w_out=lambda got: joined(got[0], 0))
    loss, dh, d_final = _loss_op(h4, target, full["final_g"])
    loss = loss[0, 0]
    grads = {"final_g": d_final.reshape(-1)}
    d_norms = [None] * 4

    def shards_of(n, g):
        return jnp.stack(jnp.split(g.astype(BF16), N_CHIPS, axis=_shard_axis(n)))

    def shards(n):
        return shards_of(n, grads[n])

    parts = {}
    dh, d_norms[3], gw, got = b3(dh, make_comm=lambda dw: ExchangeComm([shards_of("ssd_w_out", dw)]))
    parts["ssd_w_out"] = got[0]
    grads.update(gw)
    ssd_in_g = shards("ssd_w_in")
    half = ssd_in_g.shape[1] // 2
    dh, d_norms[2], gw, got = b2(dh, make_comm=lambda dw: ExchangeComm([ssd_in_g[:, :half], shards_of("lru_w_out", dw)]))
    parts["ssd_w_in"], parts["lru_w_out"] = [got[0]], got[1]
    grads.update(gw)
    dh, d_norms[1], gw, got = b1(dh, make_comm=lambda dw: ExchangeComm([ssd_in_g[:, half:]]))
    parts["ssd_w_in"].append(got[0])
    grads.update(gw)
    repl_early = _pack([grads[n] for n in REPL_EARLY], BF16, SMALL_ROWS)

    behind_attn = ["gla_w_out", "lru_w_in", "gla_w_in"]
    dx, d_norms[0], gw, got = b0(dh, make_comm=lambda dw: ExchangeComm([shards_of("mla_w_out", dw)] + [shards(n) for n in behind_attn],
                                                                        [repl_early]))
    parts.update(zip(["mla_w_out"] + behind_attn, got))
    repl_early_parts = got[-1]
    grads.update(gw)
    grads["norm_g"] = jnp.concatenate(d_norms, axis=0)
    psmall = jnp.stack([_pack([jnp.split(grads[n], N_CHIPS, axis=_shard_axis(n))[j] for n in SMALL], F32, SMALL_ROWS) for j in range(N_CHIPS)])
    prepl = _pack([grads[n] for n in REPL_LATE], F32, SMALL_ROWS)

    out = {}
    kinds = ("grad", "delta", "new_m", "new_v")
    late = [n for n in big_of["mla"] if n != "mla_w_out"]
    late_comm = ExchangeComm([shards(n) for n in late] + [psmall], [prepl])
    for n in ["ssd_w_in"] + [n for n in BIG if n != "ssd_w_in"]:
        if n == "ssd_w_in":
            res, late_parts = _adamw("adam_" + n, parts[n], *(r[n] for r in raw), lead=True, comm=late_comm)
            parts.update(zip(late, late_parts))
        else:
            res, _ = _adamw("adam_" + n, parts[n], *(r[n] for r in raw), lead=True)
        for kind, a in zip(kinds, res):
            out[kind, n] = a
    for tag, names, p in (("adam_small", SMALL, late_parts[-2]), ("adam_repl_early", REPL_EARLY, repl_early_parts),
                          ("adam_repl_late", REPL_LATE, late_parts[-1])):
        shapes = [wts[n].shape for n in names]
        packed = [_pack([d[n] for n in names], F32, SMALL_ROWS) for d in (wts, ms, vs)]
        for kind, buf in zip(kinds, _adamw(tag, p, *packed)[0]):
            for n, a in zip(names, _unpack(buf, shapes)):
                out[kind, n] = a
    loss = lax.psum(loss, ("x", "y", "c"))
    return loss, dx, out


def kernel(x, positions, norm_g, final_g, mla_w_in, mla_g_q, mla_w_uq, mla_g_kv, mla_w_ukv, mla_w_out, gla_w_in, gla_w_gk2, gla_b_gk, gla_g_o, gla_w_out, lru_w_in, lru_conv_w, lru_conv_b, lru_w_a, lru_b_a, lru_w_x, lru_b_x, lru_lam, lru_w_out, ssd_w_in, ssd_conv_w, ssd_conv_b, ssd_dt_bias, ssd_a_log, ssd_d, ssd_g_norm, ssd_w_out, loss_target, m_norm_g, m_final_g, m_mla_w_in, m_mla_g_q, m_mla_w_uq, m_mla_g_kv, m_mla_w_ukv, m_mla_w_out, m_gla_w_in, m_gla_w_gk2, m_gla_b_gk, m_gla_g_o, m_gla_w_out, m_lru_w_in, m_lru_conv_w, m_lru_conv_b, m_lru_w_a, m_lru_b_a, m_lru_w_x, m_lru_b_x, m_lru_lam, m_lru_w_out, m_ssd_w_in, m_ssd_conv_w, m_ssd_conv_b, m_ssd_dt_bias, m_ssd_a_log, m_ssd_d, m_ssd_g_norm, m_ssd_w_out, v_norm_g, v_final_g, v_mla_w_in, v_mla_g_q, v_mla_w_uq, v_mla_g_kv, v_mla_w_ukv, v_mla_w_out, v_gla_w_in, v_gla_w_gk2, v_gla_b_gk, v_gla_g_o, v_gla_w_out, v_lru_w_in, v_lru_conv_w, v_lru_conv_b, v_lru_w_a, v_lru_b_a, v_lru_w_x, v_lru_b_x, v_lru_lam, v_lru_w_out, v_ssd_w_in, v_ssd_conv_w, v_ssd_conv_b, v_ssd_dt_bias, v_ssd_a_log, v_ssd_d, v_ssd_g_norm, v_ssd_w_out):
    given = dict(locals())
    stacked = [n for n in WEIGHTS if n not in ("norm_g", "final_g")]

    def blocks(prefix):
        return {n: (given[prefix + n][0] if n in stacked else given[prefix + n]) for n in WEIGHTS}

    raw = [{n: given[prefix + n] for n in BIG} for prefix in ("", "m_", "v_")]
    loss, dx, out = _train_step(x[0], positions[0], loss_target[0], blocks(""), blocks("m_"), blocks("v_"), raw)
    res = [loss, dx[None]]
    for kind in ("grad", "delta", "new_m", "new_v"):
        res += [(out[kind, n][None] if n in stacked and n not in BIG else out[kind, n]) for n in WEIGHTS]
    return tuple(res)
```

```python
import functools
import math

import jax
import jax.numpy as jnp
from jax import lax
from jax.experimental import pallas as pl
from jax.experimental.pallas import tpu as pltpu

F32 = jnp.float32
BF16 = jnp.bfloat16

V7X_VMEM_BYTES = 64 * 1024 * 1024
VMEM_LIMIT = V7X_VMEM_BYTES - 8 * 1024 * 1024
LANE = 128

D_MODEL = 1024
NORM_EPS = 1e-6
MLA_HEADS, MLA_Q_RANK, MLA_KV_RANK, MLA_NOPE, MLA_ROPE, MLA_V = 16, 384, 256, 64, 32, 64
MLA_QK = MLA_NOPE + MLA_ROPE
ROPE_THETA = 10000.0
GLA_HEADS, GLA_DK, GLA_DV, GLA_RANK, GLA_TAU, GLA_CHUNK = 4, 128, 256, 16, 16.0, 64
LRU_WIDTH, LRU_BLOCKS, LRU_BLOCK, LRU_C, CONV_W = 1280, 10, 128, 8.0, 4
SSD_INNER, SSD_P, SSD_HEADS, SSD_GROUPS, SSD_HPG, SSD_STATE, SSD_CHUNK = 2048, 64, 32, 8, 4, 128, 64
ADAM_LR, ADAM_B1, ADAM_B2, ADAM_EPS, ADAM_WD, ADAM_STEP = 0.001, 0.9, 0.999, 1e-08, 0.01, 10

_CP = functools.partial(pltpu.CompilerParams, vmem_limit_bytes=VMEM_LIMIT)


def _bdot(a, b):
    return jnp.dot(a.astype(BF16), b.astype(BF16), preferred_element_type=F32)


def _bdot_nt(a, b):
    return lax.dot_general(a.astype(BF16), b.astype(BF16), (((1,), (1,)), ((), ())), preferred_element_type=F32)


def _bdot_tn(a, b):
    return lax.dot_general(a.astype(BF16), b.astype(BF16), (((0,), (0,)), ((), ())), preferred_element_type=F32)


def _tri(n):
    r = lax.broadcasted_iota(jnp.int32, (n, n), 0)
    c = lax.broadcasted_iota(jnp.int32, (n, n), 1)
    return r >= c


def _rms(x, g):
    return x * lax.rsqrt(jnp.mean(x * x, axis=-1, keepdims=True) + NORM_EPS) * g


def _silu(x):
    return x * jax.nn.sigmoid(x)


def _shift_rows(x, prev, j):
    if j == 0:
        return x
    t = x.shape[0]

    def fwd_impl(x, prev):
        row = lax.broadcasted_iota(jnp.int32, x.shape, 0)
        return jnp.where(row >= j, pltpu.roll(x, j, 0), pltpu.roll(prev, j, 0))

    @jax.custom_vjp
    def sh(x, prev):
        return fwd_impl(x, prev)

    def sh_fwd(x, prev):
        return fwd_impl(x, prev), None

    def sh_bwd(_, gy):
        row = lax.broadcasted_iota(jnp.int32, gy.shape, 0)
        back = pltpu.roll(gy, t - j, 0)
        return jnp.where(row < t - j, back, 0.0), jnp.where(row >= t - j, back, 0.0)

    sh.defvjp(sh_fwd, sh_bwd)
    return sh(x, prev)


def _cumsum_rows(x):
    zero = jnp.zeros_like(x)
    sh = 1
    while sh < x.shape[0]:
        x = x + _shift_rows(x, zero, sh)
        sh *= 2
    return x


def _one_minus_exp(x):
    series = -x * (1.0 + x * (0.5 + x * (1.0 / 6.0 + x * (1.0 / 24.0 + x * (1.0 / 120.0)))))
    return jnp.where(x > -0.05, series, 1.0 - jnp.exp(x))


def _tile(n, cap):
    if n <= cap:
        return n
    best = None
    for t in range(LANE, cap + 1, LANE):
        if n % t == 0:
            best = t
    assert best is not None, (n, cap)
    return best


MM_BLOCK_BYTES = 8 * 1024 * 1024
MM_ROWS, MM_KROWS = 256, 512
MM_TILE_BYTES = 2 * 1024 * 1024


def _mm_tiles(m, k, n, ta):
    if ta:
        return m, _tile(n, max(LANE, MM_BLOCK_BYTES // (4 * m) // LANE * LANE)), _tile(k, MM_KROWS)
    tn = _tile(n, max(LANE, MM_BLOCK_BYTES // (2 * k) // LANE * LANE))
    rows = min(4 * MM_ROWS, max(MM_ROWS, MM_TILE_BYTES // (4 * tn) // MM_ROWS * MM_ROWS))
    return _tile(m, rows), tn, k


def _mm(name, a, b, *, ta=False, tb=False, add=None, out_dtype=F32):
    m, k = (a.shape[1], a.shape[0]) if ta else a.shape
    n, kb = (b.shape[0], b.shape[1]) if tb else (b.shape[1], b.shape[0])
    assert k == kb, (name, a.shape, b.shape, ta, tb)
    tm, tn, tk = _mm_tiles(m, k, n, ta)
    nk = k // tk
    dn = (((0 if ta else 1,), (1 if tb else 0,)), ((), ()))
    has_add = add is not None

    def finish(refs, r):
        if has_add:
            r = r + refs[2][...].astype(F32)
        return r.astype(out_dtype)

    def body_one(*refs):
        a_ref, b_ref, o_ref = refs[0], refs[1], refs[-1]
        o_ref[...] = finish(refs, lax.dot_general(a_ref[...].astype(BF16), b_ref[...].astype(BF16), dn, preferred_element_type=F32))

    def body_acc(*refs):
        a_ref, b_ref = refs[0], refs[1]
        o_ref, acc = refs[-2], refs[-1]
        kk = pl.program_id(2)

        @pl.when(kk == 0)
        def _():
            acc[...] = jnp.zeros(acc.shape, F32)

        acc[...] += lax.dot_general(a_ref[...].astype(BF16), b_ref[...].astype(BF16), dn, preferred_element_type=F32)

        @pl.when(kk == nk - 1)
        def _():
            o_ref[...] = finish(refs, acc[...])

    a_spec = pl.BlockSpec((tk, tm), lambda i, j, q: (q, i)) if ta else pl.BlockSpec((tm, tk), lambda i, j, q: (i, q))
    b_spec = pl.BlockSpec((tn, tk), lambda i, j, q: (j, q)) if tb else pl.BlockSpec((tk, tn), lambda i, j, q: (q, j))
    o_spec = pl.BlockSpec((tm, tn), lambda i, j, q: (i, j))
    in_specs, args = [a_spec, b_spec], [a, b]
    if has_add:
        in_specs.append(o_spec)
        args.append(add)
    return pl.pallas_call(
        body_one if nk == 1 else body_acc, name=name, grid=(m // tm, n // tn, nk), in_specs=in_specs, out_specs=o_spec,
        out_shape=jax.ShapeDtypeStruct((m, n), out_dtype), scratch_shapes=[] if nk == 1 else [pltpu.VMEM((tm, tn), F32)],
        compiler_params=_CP(dimension_semantics=("parallel", "parallel", "arbitrary")),
    )(*args)


def _mm_pieces(name, a, bs):
    m, k = a.shape
    n_b = len(bs)
    tm = _tile(m, MM_ROWS)

    def body(*refs):
        av = refs[0][...].astype(BF16)
        for b_ref, o_ref in zip(refs[1:1 + n_b], refs[1 + n_b:]):
            o_ref[...] = jnp.dot(av, b_ref[...].astype(BF16), preferred_element_type=F32)

    return pl.pallas_call(
        body, name=name, grid=(m // tm,),
        in_specs=[pl.BlockSpec((tm, k), lambda i: (i, 0))] + [pl.BlockSpec(b.shape, lambda i: (0, 0)) for b in bs],
        out_specs=[pl.BlockSpec((tm, b.shape[1]), lambda i: (i, 0)) for b in bs],
        out_shape=[jax.ShapeDtypeStruct((m, b.shape[1]), F32) for b in bs],
        compiler_params=_CP(dimension_semantics=("parallel",)),
    )(a, *bs)


def _mm_pieces_t(name, dps, ws):
    m, k = dps[0].shape[0], ws[0].shape[0]
    n_b = len(ws)
    tm = _tile(m, MM_ROWS)
    dn = (((1,), (1,)), ((), ()))

    def body(*refs):
        acc = None
        for d_ref, w_ref in zip(refs[:n_b], refs[n_b:2 * n_b]):
            part = lax.dot_general(d_ref[...].astype(BF16), w_ref[...].astype(BF16), dn, preferred_element_type=F32)
            acc = part if acc is None else acc + part
        refs[-1][...] = acc

    return pl.pallas_call(
        body, name=name, grid=(m // tm,),
        in_specs=[pl.BlockSpec((tm, d.shape[1]), lambda i: (i, 0)) for d in dps] + [pl.BlockSpec(w.shape, lambda i: (0, 0)) for w in ws],
        out_specs=pl.BlockSpec((tm, k), lambda i: (i, 0)), out_shape=jax.ShapeDtypeStruct((m, k), F32),
        compiler_params=_CP(dimension_semantics=("parallel",)),
    )(*dps, *ws)


def _mm_pieces_dw(name, u, dps, comm=None):
    s, k = u.shape
    n_b = len(dps)
    tk = _tile(s, MM_KROWS)
    n_steps = s // tk
    dn = (((0,), (0,)), ((), ()))

    n_c = len(comm.ops) if comm is not None else 0

    def body(*refs):
        u_ref, d_refs = refs[0], refs[1:1 + n_b]
        p = 1 + n_b
        c_src = refs[p:p + n_c]; p += n_c
        o_refs = refs[p:p + n_b]; p += n_b
        c_dst = refs[p:p + n_c]; p += n_c
        accs = refs[p:p + n_b]; p += n_b
        cargs = (c_src, c_dst, refs[p:])
        step = pl.program_id(0)
        if comm is not None:
            @pl.when(step == 0)
            def _():
                comm.start(*cargs)

            @pl.when(step == _forward_step(n_steps))
            def _():
                comm.forward(*cargs)
        uv = u_ref[...].astype(BF16)
        for d_ref, o_ref, acc in zip(d_refs, o_refs, accs):
            part = lax.dot_general(uv, d_ref[...].astype(BF16), dn, preferred_element_type=F32)

            @pl.when(step == 0)
            def _(acc=acc, part=part):
                acc[...] = part

            @pl.when(step > 0)
            def _(acc=acc, part=part):
                acc[...] += part

            @pl.when(step == n_steps - 1)
            def _(acc=acc, o_ref=o_ref):
                o_ref[...] = acc[...].astype(o_ref.dtype)
        if comm is not None:
            @pl.when(step == n_steps - 1)
            def _():
                comm.finish(*cargs)

    c_in_specs, c_args, c_out_specs, c_out_shapes, c_sems = _carry_specs(comm)
    res = pl.pallas_call(
        body, name=name, grid=(n_steps,),
        in_specs=[pl.BlockSpec((tk, k), lambda i: (i, 0))] + [pl.BlockSpec((tk, d.shape[1]), lambda i: (i, 0)) for d in dps] + c_in_specs,
        out_specs=[pl.BlockSpec((k, d.shape[1]), lambda i: (0, 0)) for d in dps] + c_out_specs,
        out_shape=[jax.ShapeDtypeStruct((k, d.shape[1]), BF16) for d in dps] + c_out_shapes,
        scratch_shapes=[pltpu.VMEM((k, d.shape[1]), F32) for d in dps] + c_sems,
        compiler_params=_CP(dimension_semantics=("arbitrary",)),
    )(u, *dps, *c_args)
    return list(res[:n_b]), list(res[n_b:])


MM_DW_MERGE_COLS = 4096


class In:
    def __init__(self, arr, block, imap, kind="x", per_h=False, gdtype=F32, gshape=None, gimap=None, prefixed=False):
        self.arr, self.block, self.imap, self.kind, self.per_h, self.gdtype = arr, tuple(block), imap, kind, per_h, gdtype
        self.prefixed = prefixed
        self.gshape = tuple(gshape) if gshape is not None else tuple(arr.shape)
        self.gimap = gimap if gimap is not None else imap

    def spec(self, rev_g=None):
        imap = self.imap
        if rev_g is None:
            return pl.BlockSpec(self.block, lambda h, g: imap(h, g))
        return pl.BlockSpec(self.block, lambda h, g: imap(h, rev_g - 1 - g))


class Out:
    def __init__(self, shape, dtype, block, imap):
        self.shape, self.dtype, self.block, self.imap = tuple(shape), dtype, tuple(block), imap

    def spec(self, rev_g=None):
        imap = self.imap
        if rev_g is None:
            return pl.BlockSpec(self.block, lambda h, g: imap(h, g))
        return pl.BlockSpec(self.block, lambda h, g: imap(h, rev_g - 1 - g))


def _load_f32(ref, rows=None):
    v = ref[...] if rows is None else ref[0:rows]
    return v.astype(F32) if jnp.issubdtype(v.dtype, jnp.floating) else v


def _state_out(grid, shape):
    nd = len(shape)
    return Out(tuple(grid) + tuple(shape), F32, (None, None) + tuple(shape), lambda h, g: (h, g) + (0,) * nd)


def _carry(comm, grid, refs, n_in, n_out, n_scr):
    n_c = len(comm.ops) if comm is not None else 0
    n_s = len(comm.sem_shapes) if comm is not None else 0
    p = 0
    in_refs = refs[p:p + n_in]; p += n_in
    c_src = refs[p:p + n_c]; p += n_c
    out_refs = refs[p:p + n_out]; p += n_out
    c_dst = refs[p:p + n_c]; p += n_c
    scr = refs[p:p + n_scr]; p += n_scr
    c_sem = refs[p:p + n_s]
    step = pl.program_id(0) * grid[1] + pl.program_id(1)
    n_steps = grid[0] * grid[1]
    when = (step == 0, step == _forward_step(n_steps), step == n_steps - 1)
    return in_refs, out_refs, scr, (c_src, c_dst, c_sem), when


def _forward_step(n_steps):
    return max(0, min(n_steps - 2, (3 * n_steps) // 4))


def _carry_specs(comm):
    if comm is None:
        return [], [], [], [], []
    any_spec = pl.BlockSpec(memory_space=pl.ANY)
    n = len(comm.ops)
    return [any_spec] * n, list(comm.ops), [any_spec] * n, list(comm.out_shapes), list(comm.sem_shapes)


def _op_fwd(name, f, grid, ins, outs, state_shapes=(), comm=None, prefix_rows=None):
    assert prefix_rows is None or not state_shapes
    n_in, n_out, n_st = len(ins), len(outs), len(state_shapes)
    st_outs = [_state_out(grid, s) for s in state_shapes]

    def body(*refs):
        in_refs, o_refs, st_scr, cargs, (first, fwd_step, last) = _carry(comm, grid, refs, n_in, n_out + n_st, n_st)
        out_refs, sv_refs = o_refs[:n_out], o_refs[n_out:]
        if comm is not None:
            @pl.when(first)
            def _():
                comm.start(*cargs)

            @pl.when(fwd_step)
            def _():
                comm.forward(*cargs)
        g = pl.program_id(1)
        if n_st:
            @pl.when(g == 0)
            def _():
                for s in st_scr:
                    s[...] = jnp.zeros(s.shape, F32)

        def compute(rows, g=g):
            vals = [_load_f32(r, rows if i.prefixed else None) for r, i in zip(in_refs, ins)]
            sts = [s[...] for s in st_scr]
            o, ns = f(g, vals, sts)
            for r, v in zip(out_refs, o):
                r[...] = v.astype(r.dtype)
            for r, s in zip(sv_refs, sts):
                r[...] = s
            for s, v in zip(st_scr, ns):
                s[...] = v

        if prefix_rows is None:
            compute(None)
        else:
            per = grid[1] // len(prefix_rows)
            for lv, rows in enumerate(prefix_rows):
                pl.when(g // per == lv)(functools.partial(compute, rows, lv) if per == 1 else functools.partial(compute, rows))
        if comm is not None:
            @pl.when(last)
            def _():
                comm.finish(*cargs)

    all_outs = list(outs) + st_outs
    c_in_specs, c_args, c_out_specs, c_out_shapes, c_sems = _carry_specs(comm)
    res = pl.pallas_call(
        body, name=name, grid=tuple(grid), in_specs=[i.spec() for i in ins] + c_in_specs,
        out_specs=[o.spec() for o in all_outs] + c_out_specs,
        out_shape=[jax.ShapeDtypeStruct(o.shape, o.dtype) for o in all_outs] + c_out_shapes,
        scratch_shapes=[pltpu.VMEM(tuple(s), F32) for s in state_shapes] + c_sems,
        compiler_params=_CP(dimension_semantics=("arbitrary", "arbitrary")),
    )(*[i.arr for i in ins], *c_args)
    n_all = n_out + n_st
    return list(res[:n_out]), list(res[n_out:n_all]), list(res[n_all:])


def _op_bwd(name, f, grid, ins, outs, state_shapes, saved, douts, addto=None, comm=None, prefix_rows=None):
    n_in, n_out, n_st = len(ins), len(outs), len(state_shapes)
    n_g = grid[1]
    assert prefix_rows is None or all(i.prefixed and i.per_h for i in ins if i.kind == "p")
    addto = addto or {}
    diff = [k for k, i in enumerate(ins) if i.kind in ("x", "p")]
    add_idx = sorted(addto)
    st_ins = [In(s, o.block, o.imap, "c") for s, o in zip(saved, [_state_out(grid, s) for s in state_shapes])]
    dout_ins = [In(d, o.block, o.imap, "c") for d, o in zip(douts, outs)]
    add_ins = []
    for k in add_idx:
        i, a = ins[k], addto[k]
        blk = i.block if i.kind == "x" else i.block[:-2] + a.shape[-2:]
        add_ins.append(In(a, blk, i.gimap if i.kind == "x" else i.imap, "c"))
    g_outs = []
    for k in diff:
        i = ins[k]
        g_outs.append(Out(i.gshape, i.gdtype if i.kind == "x" else F32, i.block, i.gimap))

    def body(*refs):
        all_in, go_refs, ds_scr, cargs, (first_step, fwd_step, last_step) = _carry(comm, grid, refs, n_in + n_st + n_out + len(add_idx),
                                                                                    len(diff), n_st)
        if comm is not None:
            @pl.when(first_step)
            def _():
                comm.start(*cargs)

            @pl.when(fwd_step)
            def _():
                comm.forward(*cargs)
        p = 0
        in_refs = all_in[p:p + n_in]; p += n_in
        sv_refs = all_in[p:p + n_st]; p += n_st
        do_refs = all_in[p:p + n_out]; p += n_out
        ad_refs = all_in[p:p + len(add_idx)]
        hh = pl.program_id(0)
        step = pl.program_id(1)
        g = n_g - 1 - step
        if n_st:
            @pl.when(step == 0)
            def _():
                for s in ds_scr:
                    s[...] = jnp.zeros(s.shape, F32)

        def compute(rows, g=g):
            vals = [_load_f32(r, rows if i.prefixed else None) for r, i in zip(in_refs, ins)]
            sts = [r[...] for r in sv_refs]

            def fw(dvals, states):
                full = list(vals)
                for k, v in zip(diff, dvals):
                    full[k] = v
                o, ns = f(g, full, states)
                return list(o), list(ns)

            _, vjp = jax.vjp(fw, [vals[k] for k in diff], sts)
            cts = [r[...].astype(F32) for r in do_refs]
            dns = [s[...] for s in ds_scr]
            dvals, dsts = vjp((cts, dns))
            adds = dict(zip(add_idx, ad_refs))
            for k, r, dv in zip(diff, go_refs, dvals):
                i = ins[k]
                if i.kind == "x":
                    if k in adds:
                        dv = dv + adds[k][...].astype(F32)
                    r[...] = dv.astype(r.dtype)
                elif rows is not None:
                    r[0:rows] += dv
                else:
                    first = (step == 0) if i.per_h else jnp.logical_and(step == 0, hh == 0)

                    @pl.when(first)
                    def _(r=r, dv=dv, k=k):
                        r[...] = dv
                        if k in adds:
                            lead = adds[k].shape[0]
                            r[0:lead] += adds[k][...]

                    @pl.when(jnp.logical_not(first))
                    def _(r=r, dv=dv):
                        r[...] += dv
            for s, v in zip(ds_scr, dsts):
                s[...] = v

        if prefix_rows is None:
            compute(None)
        else:
            @pl.when(step == 0)
            def _():
                for k, r in zip(diff, go_refs):
                    if ins[k].kind == "p":
                        r[...] = jnp.zeros(r.shape, F32)
            per = n_g // len(prefix_rows)
            for lv, rows in enumerate(prefix_rows):
                pl.when(g // per == lv)(functools.partial(compute, rows, lv) if per == 1 else functools.partial(compute, rows))
        if comm is not None:
            @pl.when(last_step)
            def _():
                comm.finish(*cargs)

    all_ins = list(ins) + st_ins + dout_ins + add_ins
    c_in_specs, c_args, c_out_specs, c_out_shapes, c_sems = _carry_specs(comm)
    res = pl.pallas_call(
        body, name=name, grid=tuple(grid), in_specs=[i.spec(n_g) for i in all_ins] + c_in_specs,
        out_specs=[o.spec(n_g) for o in g_outs] + c_out_specs,
        out_shape=[jax.ShapeDtypeStruct(o.shape, o.dtype) for o in g_outs] + c_out_shapes,
        scratch_shapes=[pltpu.VMEM(tuple(s), F32) for s in state_shapes] + c_sems,
        compiler_params=_CP(dimension_semantics=("arbitrary", "arbitrary")),
    )(*[i.arr for i in all_ins], *c_args)
    return list(res[:len(g_outs)]), list(res[len(g_outs):])


class Op:
    def __init__(self, name, f, grid, ins, outs, state_shapes=(), prefix_rows=None):
        self.name, self.f, self.grid, self.ins, self.outs, self.state_shapes = name, f, grid, ins, outs, state_shapes
        self.prefix_rows = prefix_rows
        self.saved = None

    def fwd(self, comm=None):
        res, self.saved, self.fwd_comm_out = _op_fwd(self.name + "_fwd", self.f, self.grid, self.ins, self.outs, self.state_shapes, comm,
                                                     self.prefix_rows)
        return res

    def bwd(self, douts, addto=None, comm=None):
        res, self.bwd_comm_out = _op_bwd(self.name + "_bwd", self.f, self.grid, self.ins, self.outs, self.state_shapes, self.saved, douts,
                                         addto, comm, self.prefix_rows)
        return res


def _rows(arr, t, kind="x", gdtype=F32):
    return In(arr, (t, arr.shape[1]), lambda h, g: (g, 0), kind, gdtype=gdtype)


def _whole(arr, kind="p"):
    nd = arr.ndim
    return In(arr, arr.shape, lambda h, g: (0,) * nd, kind)


def _rows_out(s, n, t, dtype):
    return Out((s, n), dtype, (t, n), lambda h, g: (g, 0))


ROW_T = 256


def _rms_op(name, x, gain, out_dtype=BF16, gdtype=F32):
    s, n = x.shape

    def f(g, vals, sts):
        return [_rms(vals[0], vals[1])], []

    return Op(name, f, (1, s // ROW_T), [_rows(x, ROW_T, gdtype=gdtype), _whole(gain.reshape(1, n))], [_rows_out(s, n, ROW_T, out_dtype)])


def _mla_prep_op(qn, q1, q2, kn, kr, v, cos, sin):
    s = qn.shape[0]
    hd, half = MLA_HEADS, MLA_ROPE // 2

    def f(g, vals, sts):
        qn, q1, q2, kn, kr, v, cos, sin = vals
        cos_h, sin_h = jnp.tile(cos, (1, hd)), jnp.tile(sin, (1, hd))
        r1 = q1 * cos_h - q2 * sin_h
        r2 = q2 * cos_h + q1 * sin_h
        k1, k2 = kr[:, 0:half], kr[:, half:2 * half]
        kr1 = k1 * cos - k2 * sin
        kr2 = k2 * cos + k1 * sin
        zpad = jnp.zeros((qn.shape[0], LANE - MLA_QK), F32)
        qs, ks, vs = [], [], []
        for h in range(hd):
            a, b = h * MLA_NOPE, (h + 1) * MLA_NOPE
            c, d = h * half, (h + 1) * half
            qs.append(jnp.concatenate([qn[:, a:b], r1[:, c:d], r2[:, c:d], zpad], axis=1))
            ks.append(jnp.concatenate([kn[:, a:b], kr1, kr2, zpad], axis=1))
            vs.append(v[:, a:b])
        return [jnp.stack(qs, 0), jnp.stack(ks, 0), jnp.stack(vs, 0)], []

    ins = [_rows(qn, ROW_T, gdtype=BF16), _rows(q1, ROW_T, gdtype=BF16), _rows(q2, ROW_T, gdtype=BF16), _rows(kn, ROW_T, gdtype=BF16),
           _rows(kr, ROW_T, gdtype=BF16), _rows(v, ROW_T, gdtype=BF16), _rows(cos, ROW_T, "c"), _rows(sin, ROW_T, "c")]
    outs = [Out((hd, s, LANE), BF16, (hd, ROW_T, LANE), lambda h, g: (0, g, 0)),
            Out((hd, s, LANE), BF16, (hd, ROW_T, LANE), lambda h, g: (0, g, 0)),
            Out((hd, s, MLA_V), BF16, (hd, ROW_T, MLA_V), lambda h, g: (0, g, 0))]
    return Op("mla_prep", f, (1, s // ROW_T), ins, outs)


ATT_TQ = 256
ATT_LEVELS = 8


def _mla_attn_op(q, k, v):
    hd, s, _ = q.shape
    scale = MLA_QK ** -0.5

    def f(g, vals, sts):
        q, k, v = vals
        sc = _bdot_nt(q, k) * scale
        r = lax.broadcasted_iota(jnp.int32, sc.shape, 0) + g * ATT_TQ
        c = lax.broadcasted_iota(jnp.int32, sc.shape, 1)
        sc = jnp.where(r >= c, sc, -1e30)
        m = lax.stop_gradient(jnp.max(sc, axis=-1, keepdims=True))
        p = jnp.exp(sc - m)
        p = p * (1.0 / jnp.sum(p, axis=-1, keepdims=True))
        return [_bdot(p, v)], []

    ins = [In(q, (None, ATT_TQ, LANE), lambda h, g: (h, g, 0), "x", gdtype=BF16),
           In(k, (None, s, LANE), lambda h, g: (h, 0, 0), "p", per_h=True, prefixed=True),
           In(v, (None, s, MLA_V), lambda h, g: (h, 0, 0), "p", per_h=True, prefixed=True)]
    outs = [Out((hd, s, MLA_V), F32, (None, ATT_TQ, MLA_V), lambda h, g: (h, g, 0))]
    return Op("mla_attn", f, (hd, s // ATT_TQ), ins, outs, prefix_rows=[(lv + 1) * (s // ATT_LEVELS) for lv in range(ATT_LEVELS)])


def _mla_post_op(o, gate):
    hd, s, _ = o.shape

    def f(g, vals, sts):
        o, gate = vals
        cat = jnp.concatenate([o[h] for h in range(hd)], axis=1)
        return [cat * _silu(gate)], []

    ins = [In(o, (hd, ROW_T, MLA_V), lambda h, g: (0, g, 0), "x"), _rows(gate, ROW_T, gdtype=BF16)]
    return Op("mla_post", f, (1, s // ROW_T), ins, [_rows_out(s, hd * MLA_V, ROW_T, BF16)])


def _gla_gate_op(gk, w2, b):
    s = gk.shape[0]

    def f(g, vals, sts):
        gk, w2, b = vals
        return [jax.nn.log_sigmoid(_bdot(gk, w2) + b) / GLA_TAU], []

    ins = [_rows(gk, ROW_T, gdtype=BF16), _whole(w2), _whole(b)]
    return Op("gla_gate", f, (1, s // ROW_T), ins, [_rows_out(s, GLA_HEADS * GLA_DK, ROW_T, F32)])


def _gla_core_op(q, k, v, gate, la, g_o):
    s = q.shape[0]
    c, nh = GLA_CHUNK, GLA_HEADS

    def f(g, vals, sts):
        q, k, v, gate, la, g_o = vals
        tri = _tri(c)
        b = _cumsum_rows(la)
        b_last = jnp.sum(la, axis=0, keepdims=True)
        qt = q * (GLA_DK ** -0.5) * jnp.exp(b)
        kt = k * jnp.exp(-b)
        kd = k * jnp.exp(b_last - b)
        ys, new_sts = [], []
        for h in range(nh):
            ks, vs = slice(h * GLA_DK, (h + 1) * GLA_DK), slice(h * GLA_DV, (h + 1) * GLA_DV)
            att = jnp.where(tri, _bdot_nt(qt[:, ks], kt[:, ks]), 0.0)
            o = _bdot(att, v[:, vs]) + _bdot_nt(qt[:, ks], sts[h])
            new_sts.append(jnp.exp(b_last[:, ks]) * sts[h] + _bdot_tn(v[:, vs], kd[:, ks]))
            ys.append(_rms(o, g_o) * _silu(gate[:, vs]))
        return [jnp.concatenate(ys, axis=1)], new_sts

    ins = [_rows(q, c, gdtype=BF16), _rows(k, c, gdtype=BF16), _rows(v, c, gdtype=BF16), _rows(gate, c, gdtype=BF16), _rows(la, c), _whole(g_o)]
    outs = [_rows_out(s, nh * GLA_DV, c, BF16)]
    return Op("gla_core", f, (1, s // c), ins, outs, [(GLA_DV, GLA_DK)] * nh)


LRU_T = 256


def _lru_op(gate, u, conv_w, conv_b, w_a, b_a, w_x, b_x, lam):
    s, w = u.shape
    t = LRU_T

    def f(g, vals, sts):
        gate, u, cw, cb, w_a, b_a, w_x, b_x, lam = vals
        u_prev, h_prev = sts
        uc = cb
        for kk in range(CONV_W):
            uc = uc + cw[kk] * _shift_rows(u, u_prev, CONV_W - 1 - kk)
        ra, ri = [], []
        for n in range(LRU_BLOCKS):
            blk = uc[:, n * LRU_BLOCK:(n + 1) * LRU_BLOCK]
            ra.append(_bdot(blk, w_a[n]))
            ri.append(_bdot(blk, w_x[n]))
        r = jax.nn.sigmoid(jnp.concatenate(ra, axis=1) + b_a)
        i = jax.nn.sigmoid(jnp.concatenate(ri, axis=1) + b_x)
        log_a = -LRU_C * r * jax.nn.softplus(-lam)
        a = jnp.exp(log_a)
        bb = jnp.sqrt(_one_minus_exp(2.0 * log_a)) * (i * uc)
        zero = jnp.zeros_like(a)
        sh = 1
        while sh < t:
            a_s = _shift_rows(a - 1.0, zero, sh) + 1.0
            b_s = _shift_rows(bb, zero, sh)
            bb = a * b_s + bb
            a = a * a_s
            sh *= 2
        hs = bb + a * h_prev
        last = (lax.broadcasted_iota(jnp.int32, hs.shape, 0) == t - 1).astype(F32)
        h_last = jnp.sum(hs * last, axis=0, keepdims=True)
        return [hs * _silu(gate)], [u, h_last]

    ins = [_rows(gate, t, gdtype=BF16), _rows(u, t, gdtype=BF16), _whole(conv_w), _whole(conv_b), _whole(w_a), _whole(b_a), _whole(w_x),
           _whole(b_x), _whole(lam)]
    return Op("lru_core", f, (1, s // t), ins, [_rows_out(s, w, t, BF16)], [(t, w), (1, w)])


def _ssd_conv_op(xbc, conv_w, conv_b):
    s, w = xbc.shape
    t = ROW_T
    n_x, n_b = SSD_INNER, SSD_GROUPS * SSD_STATE

    def f(g, vals, sts):
        xbc, cw, cb = vals
        acc = cb
        for kk in range(CONV_W):
            acc = acc + cw[kk] * _shift_rows(xbc, sts[0], CONV_W - 1 - kk)
        y = _silu(acc)
        return [y[:, :n_x], y[:, n_x:n_x + n_b], y[:, n_x + n_b:]], [xbc]

    ins = [_rows(xbc, t, gdtype=BF16), _whole(conv_w), _whole(conv_b)]
    outs = [_rows_out(s, n_x, t, F32), _rows_out(s, n_b, t, BF16), _rows_out(s, n_b, t, BF16)]
    return Op("ssd_conv", f, (1, s // t), ins, outs, [(t, w)])


SSD_L = 512


def _ssd_core_op(x, bm, cm, z, dt, dt_bias, a_log, d_skip, g_norm):
    s = x.shape[0]
    c, hg, p = SSD_L, SSD_HPG, SSD_P
    gw = hg * p

    def f(g, vals, sts):
        x, bm, cm, z, dtr, dt_bias, a_log, d_skip, g_norm = vals
        tri = _tri(c)
        dt = jax.nn.softplus(dtr + dt_bias)
        da = dt * (-jnp.exp(a_log))
        cs = _cumsum_rows(da)
        cs_last = jnp.sum(da, axis=0, keepdims=True)
        cs_t = jnp.transpose(jnp.concatenate([cs, jnp.zeros((c, LANE - hg), F32)], axis=1))
        cb = _bdot_nt(cm, bm)
        ys, new_st = [], []
        for h in range(hg):
            cs_h = cs[:, h:h + 1]
            cs_row = cs_t[h:h + 1, :]
            seg = jnp.where(tri, cs_h - cs_row, 0.0)
            lmat = jnp.where(tri, jnp.exp(seg), 0.0)
            x_h = x[:, h * p:(h + 1) * p]
            xdt = x_h * dt[:, h:h + 1]
            y_diag = _bdot(cb * lmat, xdt)
            decay = jnp.exp(cs_last[:, h:h + 1] - cs_h)
            states = _bdot_tn(xdt * decay, bm)
            y_off = _bdot_nt(cm, sts[h]) * jnp.exp(cs_h)
            new_st.append(jnp.exp(cs_last[:, h:h + 1]) * sts[h] + states)
            ys.append(y_diag + y_off + d_skip[:, h:h + 1] * x_h)
        y = jnp.concatenate(ys, axis=1) * _silu(z)
        return [_rms(y, g_norm)], new_st

    ins = [In(x, (c, gw), lambda h, g: (g, h), "x"), In(bm, (c, SSD_STATE), lambda h, g: (g, h), "x", gdtype=BF16),
           In(cm, (c, SSD_STATE), lambda h, g: (g, h), "x", gdtype=BF16), In(z, (c, gw), lambda h, g: (g, h), "x", gdtype=BF16),
           In(dt, (None, c, hg), lambda h, g: (h, g, 0), "x"),
           In(dt_bias, (None, 1, hg), lambda h, g: (h, 0, 0), "p", per_h=True),
           In(a_log, (None, 1, hg), lambda h, g: (h, 0, 0), "p", per_h=True),
           In(d_skip, (None, 1, hg), lambda h, g: (h, 0, 0), "p", per_h=True),
           In(g_norm, (1, gw), lambda h, g: (0, h), "p", per_h=True)]
    outs = [Out((s, SSD_INNER), BF16, (c, gw), lambda h, g: (g, h))]
    return Op("ssd_core", f, (SSD_GROUPS, s // c), ins, outs, [(p, SSD_STATE)] * hg)


def _loss_op(h, target, final_g):
    s, n = h.shape
    t = ROW_T
    n_g = s // t

    def body(h_ref, t_ref, g_ref, loss_ref, dh_ref, dg_ref):
        step = pl.program_id(0)

        def lossf(hv, gv):
            err = _rms(hv, gv) - t_ref[...]
            return 0.5 * jnp.sum(jnp.mean(err * err, axis=-1))

        l, (dh, dg) = jax.value_and_grad(lossf, argnums=(0, 1))(h_ref[...], g_ref[...])
        dh_ref[...] = dh

        @pl.when(step == 0)
        def _():
            loss_ref[...] = jnp.zeros(loss_ref.shape, F32)
            dg_ref[...] = jnp.zeros(dg_ref.shape, F32)

        loss_ref[...] += jnp.full(loss_ref.shape, l, F32)
        dg_ref[...] += dg

    row = pl.BlockSpec((t, n), lambda g: (g, 0))
    one = pl.BlockSpec((1, n), lambda g: (0, 0))
    return pl.pallas_call(
        body, name="loss_head", grid=(n_g,), in_specs=[row, row, one],
        out_specs=[pl.BlockSpec((1, LANE), lambda g: (0, 0)), row, one],
        out_shape=[jax.ShapeDtypeStruct((1, LANE), F32), jax.ShapeDtypeStruct((s, n), F32), jax.ShapeDtypeStruct((1, n), F32)],
        compiler_params=_CP(dimension_semantics=("arbitrary",)),
    )(h, target, final_g.reshape(1, n))


def _pad_cols(w, n):
    return jnp.pad(w, ((0, 0), (0, n - w.shape[1])))


def _pad_rows(w, n):
    return jnp.pad(w, ((0, n - w.shape[0]), (0, 0)))


def _proj_bwd(tag, u, dps, ws):
    du = _mm_pieces_t(f"{tag}_du", dps, ws)
    if sum(dp.shape[1] for dp in dps) <= MM_DW_MERGE_COLS:
        dws, _ = _mm_pieces_dw(f"{tag}_dw", u, dps)
    else:
        dws = [_mm(f"{tag}_dw{i}", u, dp, ta=True, out_dtype=BF16) for i, dp in enumerate(dps)]
    return du, dws


def _mla_layer(h, norm_g, w, cos, sin, fwd_comm=None, late_w_out=None):
    bf = lambda a: a.astype(BF16)
    w_in, w_uq, w_ukv = w["mla_w_in"], w["mla_w_uq"], w["mla_w_ukv"]
    a0, a1, a2 = MLA_Q_RANK, MLA_Q_RANK + MLA_KV_RANK, MLA_Q_RANK + MLA_KV_RANK + MLA_ROPE
    w_cq, w_ckv, w_kr, w_g = bf(w_in[:, :a0]), bf(w_in[:, a0:a1]), bf(_pad_cols(w_in[:, a1:a2], LANE)), bf(w_in[:, a2:])
    uq = w_uq.reshape(MLA_Q_RANK, MLA_HEADS, MLA_QK)
    half = MLA_ROPE // 2
    w_qn = bf(uq[:, :, :MLA_NOPE].reshape(MLA_Q_RANK, -1))
    w_q1 = bf(uq[:, :, MLA_NOPE:MLA_NOPE + half].reshape(MLA_Q_RANK, -1))
    w_q2 = bf(uq[:, :, MLA_NOPE + half:].reshape(MLA_Q_RANK, -1))
    ukv = w_ukv.reshape(MLA_KV_RANK, MLA_HEADS, MLA_NOPE + MLA_V)
    w_kn = bf(ukv[:, :, :MLA_NOPE].reshape(MLA_KV_RANK, -1))
    w_v = bf(ukv[:, :, MLA_NOPE:].reshape(MLA_KV_RANK, -1))

    n0 = _rms_op("mla_norm", h, norm_g)
    u, = n0.fwd()
    cq, ckv, kr, gate = _mm_pieces("mla_in", u, (w_cq, w_ckv, w_kr, w_g))
    nq = _rms_op("mla_qnorm", cq, w["mla_g_q"], gdtype=BF16)
    nkv = _rms_op("mla_kvnorm", ckv, w["mla_g_kv"], gdtype=BF16)
    qn_, = nq.fwd()
    kvn_, = nkv.fwd()
    qn, q1, q2 = _mm_pieces("mla_uq", qn_, (w_qn, w_q1, w_q2))
    kn, v = _mm_pieces("mla_ukv", kvn_, (w_kn, w_v))
    prep = _mla_prep_op(qn, q1, q2, kn, kr, v, cos, sin)
    qh, kh, vh = prep.fwd()
    attn = _mla_attn_op(qh, kh, vh)
    o, = attn.fwd(fwd_comm)
    w_out = bf(w["mla_w_out"]) if late_w_out is None else late_w_out(attn.fwd_comm_out)
    post = _mla_post_op(o, gate)
    y, = post.fwd()
    h_out = _mm("mla_out", y, w_out, add=h)

    def bwd(dh, make_comm=None, make_late=None):
        dy = _mm("mla_out_dy", dh, w_out, tb=True, out_dtype=BF16)
        d_w_out = _mm("mla_out_dw", y, dh, ta=True, out_dtype=BF16)
        do, dgate = post.bwd([dy])
        dqh, dkh, dvh = attn.bwd([do], comm=None if make_comm is None else make_comm(d_w_out))
        dqn, dq1, dq2, dkn, dkr, dv = prep.bwd([dqh, dkh, dvh])
        dqn_, d_uq = _proj_bwd("mla_uq", qn_, (dqn, dq1, dq2), (w_qn, w_q1, w_q2))
        dkvn_, d_ukv = _proj_bwd("mla_ukv", kvn_, (dkn, dv), (w_kn, w_v))
        shp = (MLA_Q_RANK, MLA_HEADS, -1)
        g_uq = jnp.concatenate([d_uq[0].reshape(shp), d_uq[1].reshape(shp), d_uq[2].reshape(shp)], axis=2).reshape(MLA_Q_RANK, -1)
        shp = (MLA_KV_RANK, MLA_HEADS, -1)
        g_ukv = jnp.concatenate([d_ukv[0].reshape(shp), d_ukv[1].reshape(shp)], axis=2).reshape(MLA_KV_RANK, -1)
        dcq, d_g_q = nq.bwd([dqn_])
        dckv, d_g_kv = nkv.bwd([dkvn_])
        dps = (dcq, dckv, dkr, dgate)
        du = _mm_pieces_t("mla_in_du", dps, (w_cq, w_ckv, w_kr, w_g))
        d_in, late_out = _mm_pieces_dw("mla_in_dw", u, dps, comm=None if make_late is None else make_late(g_uq, g_ukv))
        dh_in, d_norm = n0.bwd([du], addto={0: dh})
        g_in = jnp.concatenate([d_in[0], d_in[1], d_in[2][:, :MLA_ROPE], d_in[3]], axis=1)
        return dh_in, d_norm, {"mla_w_in": g_in, "mla_g_q": d_g_q.reshape(-1), "mla_w_uq": g_uq, "mla_g_kv": d_g_kv.reshape(-1),
                               "mla_w_ukv": g_ukv, "mla_w_out": d_w_out}, attn.bwd_comm_out, late_out

    return h_out, bwd, attn.fwd_comm_out


def _gla_layer(h, norm_g, w, fwd_comm=None):
    bf = lambda a: a.astype(BF16)
    w_in = w["gla_w_in"]
    nk, nv = GLA_HEADS * GLA_DK, GLA_HEADS * GLA_DV
    cuts = (0, nk, 2 * nk, 2 * nk + nv, 2 * nk + 2 * nv)
    w_q, w_k, w_v, w_g = (bf(w_in[:, cuts[i]:cuts[i + 1]]) for i in range(4))
    w_gk = bf(_pad_cols(w_in[:, cuts[4]:], LANE))
    w2 = _pad_rows(w["gla_w_gk2"], LANE)
    b_gk = w["gla_b_gk"].reshape(1, -1)
    g_o = w["gla_g_o"].reshape(1, -1)
    w_out = bf(w["gla_w_out"])

    n0 = _rms_op("gla_norm", h, norm_g)
    u, = n0.fwd()
    q, k, v, gate, gk = _mm_pieces("gla_in", u, (w_q, w_k, w_v, w_g, w_gk))
    gop = _gla_gate_op(gk, w2, b_gk)
    la, = gop.fwd()
    core = _gla_core_op(q, k, v, gate, la, g_o)
    y, = core.fwd(fwd_comm)
    h_out = _mm("gla_out", y, w_out, add=h)

    def bwd(dh, make_comm=None):
        dy = _mm("gla_out_dy", dh, w_out, tb=True, out_dtype=BF16)
        d_w_out = _mm("gla_out_dw", y, dh, ta=True, out_dtype=BF16)
        dq, dk, dv, dgate, dla, d_g_o = core.bwd([dy], comm=None if make_comm is None else make_comm(d_w_out))
        dgk, d_w2, d_b = gop.bwd([dla])
        du, d_in = _proj_bwd("gla_in", u, (dq, dk, dv, dgate, dgk), (w_q, w_k, w_v, w_g, w_gk))
        dh_in, d_norm = n0.bwd([du], addto={0: dh})
        g_in = jnp.concatenate([d_in[0], d_in[1], d_in[2], d_in[3], d_in[4][:, :GLA_RANK]], axis=1)
        return dh_in, d_norm, {"gla_w_in": g_in, "gla_w_gk2": d_w2[:GLA_RANK], "gla_b_gk": d_b.reshape(-1), "gla_g_o": d_g_o.reshape(-1),
                               "gla_w_out": d_w_out}, core.bwd_comm_out

    return h_out, bwd, core.fwd_comm_out


def _lru_layer(h, norm_g, w, fwd_comm=None):
    bf = lambda a: a.astype(BF16)
    w_in = w["lru_w_in"]
    w_g, w_u = bf(w_in[:, :LRU_WIDTH]), bf(w_in[:, LRU_WIDTH:])
    row = lambda a: a.reshape(1, -1)
    w_out = bf(w["lru_w_out"])

    n0 = _rms_op("lru_norm", h, norm_g)
    u_, = n0.fwd()
    gate, u = _mm_pieces("lru_in", u_, (w_g, w_u))
    core = _lru_op(gate, u, w["lru_conv_w"].reshape(CONV_W, 1, -1), row(w["lru_conv_b"]), w["lru_w_a"], row(w["lru_b_a"]), w["lru_w_x"],
                   row(w["lru_b_x"]), row(w["lru_lam"]))
    y, = core.fwd(fwd_comm)
    h_out = _mm("lru_out", y, w_out, add=h)

    def bwd(dh, make_comm=None):
        dy = _mm("lru_out_dy", dh, w_out, tb=True, out_dtype=BF16)
        d_w_out = _mm("lru_out_dw", y, dh, ta=True, out_dtype=BF16)
        dgate, du, d_cw, d_cb, d_wa, d_ba, d_wx, d_bx, d_lam = core.bwd([dy], comm=None if make_comm is None else make_comm(d_w_out))
        du_, d_in = _proj_bwd("lru_in", u_, (dgate, du), (w_g, w_u))
        dh_in, d_norm = n0.bwd([du_], addto={0: dh})
        return dh_in, d_norm, {"lru_w_in": jnp.concatenate(d_in, axis=1), "lru_conv_w": d_cw.reshape(CONV_W, -1), "lru_conv_b": d_cb.reshape(-1),
                               "lru_w_a": d_wa, "lru_b_a": d_ba.reshape(-1), "lru_w_x": d_wx, "lru_b_x": d_bx.reshape(-1),
                               "lru_lam": d_lam.reshape(-1), "lru_w_out": d_w_out}, core.bwd_comm_out

    return h_out, bwd, core.fwd_comm_out


def _ssd_layer(h, norm_g, w, fwd_comm=None, late_w_out=None):
    bf = lambda a: a.astype(BF16)
    s = h.shape[0]
    w_in = w["ssd_w_in"]
    conv_dim = SSD_INNER + 2 * SSD_GROUPS * SSD_STATE
    w_z, w_xbc = bf(w_in[:, :SSD_INNER]), bf(w_in[:, SSD_INNER:SSD_INNER + conv_dim])
    w_dt = bf(_pad_cols(w_in[:, SSD_INNER + conv_dim:], LANE))
    grp = lambda a: a.reshape(SSD_GROUPS, 1, SSD_HPG)

    n0 = _rms_op("ssd_norm", h, norm_g)
    u, = n0.fwd()
    z, xbc, dtp = _mm_pieces("ssd_in", u, (w_z, w_xbc, w_dt))
    conv = _ssd_conv_op(xbc, w["ssd_conv_w"].reshape(CONV_W, 1, -1), w["ssd_conv_b"].reshape(1, -1))
    x, bm, cm = conv.fwd()
    dt = dtp[:, :SSD_HEADS].reshape(s, SSD_GROUPS, SSD_HPG).transpose(1, 0, 2)
    core = _ssd_core_op(x, bm, cm, z, dt, grp(w["ssd_dt_bias"]), grp(w["ssd_a_log"]), grp(w["ssd_d"]), w["ssd_g_norm"].reshape(1, -1))
    y, = core.fwd(fwd_comm)
    w_out = bf(w["ssd_w_out"]) if late_w_out is None else late_w_out(core.fwd_comm_out)
    h_out = _mm("ssd_out", y, w_out, add=h)

    def bwd(dh, make_comm=None):
        dy = _mm("ssd_out_dy", dh, w_out, tb=True, out_dtype=BF16)
        d_w_out = _mm("ssd_out_dw", y, dh, ta=True, out_dtype=BF16)
        dx, dbm, dcm, dz, ddt, d_dtb, d_alog, d_d, d_gn = core.bwd([dy], comm=None if make_comm is None else make_comm(d_w_out))
        dxbc, d_cw, d_cb = conv.bwd([dx, dbm, dcm])
        ddtp = _pad_cols(ddt.transpose(1, 0, 2).reshape(s, SSD_HEADS), LANE).astype(BF16)
        du, d_in = _proj_bwd("ssd_in", u, (dz, dxbc, ddtp), (w_z, w_xbc, w_dt))
        dh_in, d_norm = n0.bwd([du], addto={0: dh})
        g_in = jnp.concatenate([d_in[0], d_in[1], d_in[2][:, :SSD_HEADS]], axis=1)
        return dh_in, d_norm, {"ssd_w_in": g_in, "ssd_conv_w": d_cw.reshape(CONV_W, -1), "ssd_conv_b": d_cb.reshape(-1),
                               "ssd_dt_bias": d_dtb.reshape(-1), "ssd_a_log": d_alog.reshape(-1), "ssd_d": d_d.reshape(-1),
                               "ssd_g_norm": d_gn.reshape(-1), "ssd_w_out": d_w_out}, core.bwd_comm_out

    return h_out, bwd


def _rope_tables(positions):
    inv_freq = ROPE_THETA ** (-jnp.arange(0, MLA_ROPE, 2, dtype=F32) / MLA_ROPE)
    ang = positions.astype(F32)[:, None] * inv_freq
    return jnp.cos(ang), jnp.sin(ang)


WEIGHTS = ["norm_g", "final_g", "mla_w_in", "mla_g_q", "mla_w_uq", "mla_g_kv", "mla_w_ukv", "mla_w_out", "gla_w_in", "gla_w_gk2", "gla_b_gk",
           "gla_g_o", "gla_w_out", "lru_w_in", "lru_conv_w", "lru_conv_b", "lru_w_a", "lru_b_a", "lru_w_x", "lru_b_x", "lru_lam", "lru_w_out",
           "ssd_w_in", "ssd_conv_w", "ssd_conv_b", "ssd_dt_bias", "ssd_a_log", "ssd_d", "ssd_g_norm", "ssd_w_out"]
BIG = ["mla_w_in", "mla_w_uq", "mla_w_ukv", "mla_w_out", "gla_w_in", "gla_w_out", "lru_w_in", "lru_w_out", "ssd_w_in", "ssd_w_out"]
SMALL = ["gla_w_gk2", "gla_b_gk", "gla_g_o", "lru_conv_w", "lru_conv_b", "lru_b_a", "lru_b_x", "lru_lam", "ssd_conv_w", "ssd_conv_b", "ssd_g_norm"]
REPL = ["norm_g", "final_g", "mla_g_q", "mla_g_kv", "lru_w_a", "lru_w_x", "ssd_dt_bias", "ssd_a_log", "ssd_d"]
REPL_EARLY = ["lru_w_a", "lru_w_x"]
REPL_LATE = [n for n in REPL if n not in REPL_EARLY]
N_CHIPS, N_DEV = 4, 8
PACK_W = 1024
ADAM_ROWS = 256
SMALL_ROWS = 64


def _shard_axis(name):
    return 0 if name.endswith("_w_out") else -1


def _pack(arrs, dtype, row_mult):
    flat = jnp.concatenate([a.reshape(-1).astype(dtype) for a in arrs])
    per = PACK_W * row_mult
    total = -(-flat.shape[0] // per) * per
    return jnp.pad(flat, (0, total - flat.shape[0])).reshape(-1, PACK_W)


def _unpack(buf, shapes):
    flat = buf.reshape(-1)
    out, off = [], 0
    for s in shapes:
        n = math.prod(s)
        out.append(flat[off:off + n].reshape(s))
        off += n
    return out


def _mesh_pos():
    return lax.axis_index("x"), lax.axis_index("y"), lax.axis_index("c")


class GatherComm:
    def __init__(self, ops):
        self.ops = list(ops)
        n = len(self.ops)
        assert all(o.ndim == 2 and o.shape[0] % 32 == 0 for o in self.ops), [o.shape for o in self.ops]
        self.out_shapes = [jax.ShapeDtypeStruct((N_CHIPS,) + o.shape, o.dtype) for o in self.ops]
        self.sem_shapes = [pltpu.SemaphoreType.DMA((6 * n,)), pltpu.SemaphoreType.DMA((6 * n,)), pltpu.SemaphoreType.DMA((n,))]

    def _copies(self, srcs, dsts, sems):
        send_sems, recv_sems, local_sems = sems
        n = len(self.ops)
        x, y, c = _mesh_pos()
        me_id, sibling = (x, y, c), (x, y, 1 - c)
        chips = [(1 - x, y), (x, 1 - y), (1 - x, 1 - y)]
        mine = 2 * x + y

        def half(i, cc):
            h = self.ops[i].shape[0] // 2
            return pl.ds(cc * h, h)

        def copy(i, k, src, slot, cc, to):
            return pltpu.make_async_remote_copy(src_ref=src, dst_ref=dsts[i].at[slot, half(i, cc)], send_sem=send_sems.at[i * 6 + k],
                                                recv_sem=recv_sems.at[i * 6 + k], device_id=to, device_id_type=pl.DeviceIdType.MESH)

        local = [pltpu.make_async_copy(srcs[i], dsts[i].at[mine], local_sems.at[i]) for i in range(n)]
        first, ici_recvs, passed, sib_recvs = [], [], [], []
        for i in range(n):
            my_half = srcs[i].at[half(i, c)]
            for k, (px, py) in enumerate(chips):
                slot = 2 * px + py
                first.append(copy(i, k, my_half, mine, c, (px, py, c)))
                ici_recvs.append(copy(i, k, my_half, slot, c, me_id))
                passed.append(copy(i, 3 + k, dsts[i].at[slot, half(i, c)], slot, c, sibling))
                sib_recvs.append(copy(i, 3 + k, my_half, slot, 1 - c, me_id))
        return local, first, ici_recvs, passed, sib_recvs

    def start(self, srcs, dsts, sems):
        local, first, _, _, _ = self._copies(srcs, dsts, sems)
        for cp in local + first:
            cp.start()

    def forward(self, srcs, dsts, sems):
        _, _, ici_recvs, passed, _ = self._copies(srcs, dsts, sems)
        for rc, fw in zip(ici_recvs, passed):
            rc.wait_recv()
            fw.start()

    def finish(self, srcs, dsts, sems):
        local, first, _, passed, sib_recvs = self._copies(srcs, dsts, sems)
        for cp in sib_recvs:
            cp.wait_recv()
        for cp in first + passed:
            cp.wait_send()
        for cp in local:
            cp.wait()


class ExchangeComm:
    def __init__(self, chip_ops, all_ops=()):
        self.ops = list(chip_ops) + list(all_ops)
        self.per_chip = (True,) * len(chip_ops) + (False,) * len(all_ops)
        n = len(self.ops)
        self.out_shapes = [jax.ShapeDtypeStruct((N_DEV,) + o.shape[-2:], o.dtype) for o in self.ops]
        self.sem_shapes = [pltpu.SemaphoreType.DMA((7 * n,)), pltpu.SemaphoreType.DMA((7 * n,)), pltpu.SemaphoreType.DMA((n,))]

    def _copies(self, srcs, dsts, sems):
        send_sems, recv_sems, local_sems = sems
        n, per_chip = len(self.ops), self.per_chip
        x, y, c = _mesh_pos()
        me_id, sibling = (x, y, c), (x, y, 1 - c)
        chips = [(1 - x, y), (x, 1 - y), (1 - x, 1 - y)]

        def dev(px, py, pc):
            return 4 * px + 2 * py + pc

        def part(i, px, py):
            return srcs[i].at[2 * px + py] if per_chip[i] else srcs[i]

        def copy(i, k, src, slot, to):
            return pltpu.make_async_remote_copy(src_ref=src, dst_ref=dsts[i].at[slot], send_sem=send_sems.at[i * 7 + k],
                                                recv_sem=recv_sems.at[i * 7 + k], device_id=to, device_id_type=pl.DeviceIdType.MESH)

        me = dev(x, y, c)
        local = [pltpu.make_async_copy(part(i, x, y), dsts[i].at[me], local_sems.at[i]) for i in range(n)]
        first, ici_recvs, passed, sib_recvs = [], [], [], []
        for i in range(n):
            first.append(copy(i, 0, part(i, x, y), me, sibling))
            first += [copy(i, 1 + k, part(i, px, py), me, (px, py, c)) for k, (px, py) in enumerate(chips)]
            sib_recvs.append(copy(i, 0, part(i, x, y), dev(x, y, 1 - c), me_id))
            for k, (px, py) in enumerate(chips):
                slot = dev(px, py, c)
                ici_recvs.append(copy(i, 1 + k, part(i, x, y), slot, me_id))
                passed.append(copy(i, 4 + k, dsts[i].at[slot], slot, sibling))
                sib_recvs.append(copy(i, 4 + k, part(i, x, y), dev(px, py, 1 - c), me_id))
        return local, first, ici_recvs, passed, sib_recvs

    def start(self, srcs, dsts, sems):
        local, first, _, _, _ = self._copies(srcs, dsts, sems)
        for cp in local + first:
            cp.start()

    def forward(self, srcs, dsts, sems):
        _, _, ici_recvs, passed, _ = self._copies(srcs, dsts, sems)
        for rc, fw in zip(ici_recvs, passed):
            rc.wait_recv()
            fw.start()

    def finish(self, srcs, dsts, sems):
        local, first, _, passed, sib_recvs = self._copies(srcs, dsts, sems)
        for cp in sib_recvs:
            cp.wait_recv()
        for cp in first + passed:
            cp.wait_send()
        for cp in local:
            cp.wait()


def _run_comm(name, comm):
    n = len(comm.ops)

    def body(*refs):
        srcs, dsts, sems = refs[:n], refs[n:2 * n], refs[2 * n:]
        comm.start(srcs, dsts, sems)
        comm.forward(srcs, dsts, sems)
        comm.finish(srcs, dsts, sems)

    any_spec = pl.BlockSpec(memory_space=pl.ANY)
    return pl.pallas_call(body, name=name, in_specs=[any_spec] * n, out_specs=[any_spec] * n, out_shape=comm.out_shapes,
                          scratch_shapes=comm.sem_shapes)(*comm.ops)


def _adamw(name, parts, w, m, v, lead=False, comm=None):
    plist = list(parts) if isinstance(parts, (list, tuple)) else [parts]
    n_p = len(plist)
    rows, cols = w.shape[-2:]
    t = next(c for c in (ADAM_ROWS, ADAM_ROWS // 2, SMALL_ROWS) if all(p.shape[1] % c == 0 for p in plist))
    starts = [sum(p.shape[1] for p in plist[:k]) // t for k in range(n_p)]
    counts = [p.shape[1] // t for p in plist]
    assert sum(p.shape[1] for p in plist) == rows, (name, rows)
    c1 = 1.0 - ADAM_B1 ** ADAM_STEP
    c2 = 1.0 - ADAM_B2 ** ADAM_STEP

    n_c = len(comm.ops) if comm is not None else 0
    n_steps = rows // t

    def body(*refs):
        p_refs = refs[:n_p]
        w_ref, m_ref, v_ref = refs[n_p:n_p + 3]
        c_src = refs[n_p + 3:n_p + 3 + n_c]
        g_ref, d_ref, nm_ref, nv_ref = refs[n_p + 3 + n_c:n_p + 7 + n_c]
        cargs = (c_src, refs[n_p + 7 + n_c:n_p + 7 + 2 * n_c], refs[n_p + 7 + 2 * n_c:])
        if comm is not None:
            @pl.when(pl.program_id(0) == 0)
            def _():
                comm.start(*cargs)

            @pl.when(pl.program_id(0) == _forward_step(n_steps))
            def _():
                comm.forward(*cargs)
        g = None
        for k, p_ref in enumerate(p_refs):
            gk = p_ref[0].astype(F32)
            for d in range(1, N_DEV):
                gk = gk + p_ref[d].astype(F32)
            g = gk if g is None else jnp.where(pl.program_id(0) >= starts[k], gk, g)
        nm = ADAM_B1 * m_ref[...] + (1.0 - ADAM_B1) * g
        nv = ADAM_B2 * v_ref[...] + (1.0 - ADAM_B2) * (g * g)
        g_ref[...] = g
        nm_ref[...] = nm
        nv_ref[...] = nv
        d_ref[...] = -ADAM_LR * ((nm / c1) / (jnp.sqrt(nv / c2) + ADAM_EPS) + ADAM_WD * w_ref[...])
        if comm is not None:
            @pl.when(pl.program_id(0) == n_steps - 1)
            def _():
                comm.finish(*cargs)

    row = pl.BlockSpec((None, t, cols), lambda i: (0, i, 0)) if lead else pl.BlockSpec((t, cols), lambda i: (i, 0))
    c_in_specs, c_args, c_out_specs, c_out_shapes, c_sems = _carry_specs(comm)
    res = pl.pallas_call(
        body, name=name, grid=(n_steps,),
        in_specs=[pl.BlockSpec((N_DEV, t, cols), lambda i, lo=lo, n=n: (0, jnp.clip(i - lo, 0, n - 1), 0)) for lo, n in zip(starts, counts)]
        + [row, row, row] + c_in_specs,
        out_specs=[row] * 4 + c_out_specs, out_shape=[jax.ShapeDtypeStruct(w.shape, F32)] * 4 + c_out_shapes, scratch_shapes=c_sems,
        compiler_params=_CP(dimension_semantics=("arbitrary" if comm is not None else "parallel",)),
    )(*plist, w, m, v, *c_args)
    return list(res[:4]), list(res[4:])


def _train_step(x, positions, target, wts, ms, vs, raw):
    small_shapes = [wts[n].shape for n in SMALL]

    big_of = {tag: [n for n in BIG if n.startswith(tag)] for tag in ("mla", "gla", "lru", "ssd")}
    full = {n: wts[n] for n in REPL}

    def gather_comm(names, extra=()):
        return GatherComm([wts[n].astype(BF16) for n in names] + list(extra))

    def assemble(names, got):
        for k, n in enumerate(names):
            full[n] = jnp.concatenate([got[k][j] for j in range(N_CHIPS)], axis=_shard_axis(n))

    first = [n for n in big_of["mla"] if n != "mla_w_out"]
    got = _run_comm("gather_first", gather_comm(first, [_pack([wts[n] for n in SMALL], F32, SMALL_ROWS)]))
    assemble(first, got)
    per_chip_small = [_unpack(got[-1][j], small_shapes) for j in range(N_CHIPS)]
    for k, n in enumerate(SMALL):
        full[n] = jnp.concatenate([per_chip_small[j][k] for j in range(N_CHIPS)], axis=_shard_axis(n))

    cos, sin = _rope_tables(positions)
    ng = full["norm_g"]
    behind_attn = ["mla_w_out"] + big_of["gla"] + big_of["lru"]

    def mla_w_out(got):
        assemble(behind_attn, got)
        return full["mla_w_out"].astype(BF16)

    h1, b0, _ = _mla_layer(x, ng[0], full, cos, sin, fwd_comm=gather_comm(behind_attn), late_w_out=mla_w_out)

    def joined(got_k, axis):
        return jnp.concatenate([got_k[j] for j in range(N_CHIPS)], axis=axis)

    ssd_in = wts["ssd_w_in"].astype(BF16)
    half = ssd_in.shape[0] // 2
    h2, b1, got = _gla_layer(h1, ng[1], full, fwd_comm=GatherComm([ssd_in[:half]]))
    top = joined(got[0], -1)
    h3, b2, got = _lru_layer(h2, ng[2], full, fwd_comm=GatherComm([ssd_in[half:]]))
    full["ssd_w_in"] = jnp.concatenate([top, joined(got[0], -1)], axis=0)
    h4, b3 = _ssd_layer(h3, ng[3], full, fwd_comm=gather_comm(["ssd_w_out"]), late_w_out=lambda got: joined(got[0], 0))
    loss, dh, d_final = _loss_op(h4, target, full["final_g"])
    loss = loss[0, 0]
    grads = {"final_g": d_final.reshape(-1)}
    d_norms = [None] * 4

    def shards_of(n, g):
        return jnp.stack(jnp.split(g.astype(BF16), N_CHIPS, axis=_shard_axis(n)))

    def shards(n):
        return shards_of(n, grads[n])

    parts = {}
    dh, d_norms[3], gw, got = b3(dh, make_comm=lambda dw: ExchangeComm([shards_of("ssd_w_out", dw)]))
    parts["ssd_w_out"] = got[0]
    grads.update(gw)
    ssd_in_g = shards("ssd_w_in")
    half = ssd_in_g.shape[1] // 2
    dh, d_norms[2], gw, got = b2(dh, make_comm=lambda dw: ExchangeComm([ssd_in_g[:, :half], shards_of("lru_w_out", dw)]))
    parts["ssd_w_in"], parts["lru_w_out"] = [got[0]], got[1]
    grads.update(gw)
    dh, d_norms[1], gw, got = b1(dh, make_comm=lambda dw: ExchangeComm([ssd_in_g[:, half:], shards_of("gla_w_out", dw)]))
    parts["ssd_w_in"].append(got[0])
    parts["gla_w_out"] = got[1]
    grads.update(gw)
    repl_early = _pack([grads[n] for n in REPL_EARLY], BF16, SMALL_ROWS)

    behind_attn = ["lru_w_in", "gla_w_in"]
    psmall = jnp.stack([_pack([jnp.split(grads[n], N_CHIPS, axis=_shard_axis(n))[j] for n in SMALL], F32, SMALL_ROWS) for j in range(N_CHIPS)])
    dx, d_norms[0], gw, got, got_late = b0(
        dh, make_comm=lambda dw: ExchangeComm([shards_of("mla_w_out", dw)] + [shards(n) for n in behind_attn], [repl_early]),
        make_late=lambda g_uq, g_ukv: ExchangeComm([shards_of("mla_w_uq", g_uq), shards_of("mla_w_ukv", g_ukv), psmall]))
    parts.update(zip(["mla_w_out"] + behind_attn, got))
    repl_early_parts = got[-1]
    parts["mla_w_uq"], parts["mla_w_ukv"], small_parts = got_late
    grads.update(gw)
    grads["norm_g"] = jnp.concatenate(d_norms, axis=0)
    prepl = _pack([grads[n] for n in REPL_LATE], F32, SMALL_ROWS)

    out = {}
    kinds = ("grad", "delta", "new_m", "new_v")
    late_comm = ExchangeComm([shards("mla_w_in")], [prepl])
    for n in ["ssd_w_in"] + [n for n in BIG if n != "ssd_w_in"]:
        if n == "ssd_w_in":
            res, late_parts = _adamw("adam_" + n, parts[n], *(r[n] for r in raw), lead=True, comm=late_comm)
            parts["mla_w_in"] = late_parts[0]
        else:
            res, _ = _adamw("adam_" + n, parts[n], *(r[n] for r in raw), lead=True)
        for kind, a in zip(kinds, res):
            out[kind, n] = a
    for tag, names, p in (("adam_small", SMALL, small_parts), ("adam_repl_early", REPL_EARLY, repl_early_parts),
                          ("adam_repl_late", REPL_LATE, late_parts[-1])):
        shapes = [wts[n].shape for n in names]
        packed = [_pack([d[n] for n in names], F32, SMALL_ROWS) for d in (wts, ms, vs)]
        for kind, buf in zip(kinds, _adamw(tag, p, *packed)[0]):
            for n, a in zip(names, _unpack(buf, shapes)):
                out[kind, n] = a
    loss = lax.psum(loss, ("x", "y", "c"))
    return loss, dx, out


def kernel(x, positions, norm_g, final_g, mla_w_in, mla_g_q, mla_w_uq, mla_g_kv, mla_w_ukv, mla_w_out, gla_w_in, gla_w_gk2, gla_b_gk, gla_g_o, gla_w_out, lru_w_in, lru_conv_w, lru_conv_b, lru_w_a, lru_b_a, lru_w_x, lru_b_x, lru_lam, lru_w_out, ssd_w_in, ssd_conv_w, ssd_conv_b, ssd_dt_bias, ssd_a_log, ssd_d, ssd_g_norm, ssd_w_out, loss_target, m_norm_g, m_final_g, m_mla_w_in, m_mla_g_q, m_mla_w_uq, m_mla_g_kv, m_mla_w_ukv, m_mla_w_out, m_gla_w_in, m_gla_w_gk2, m_gla_b_gk, m_gla_g_o, m_gla_w_out, m_lru_w_in, m_lru_conv_w, m_lru_conv_b, m_lru_w_a, m_lru_b_a, m_lru_w_x, m_lru_b_x, m_lru_lam, m_lru_w_out, m_ssd_w_in, m_ssd_conv_w, m_ssd_conv_b, m_ssd_dt_bias, m_ssd_a_log, m_ssd_d, m_ssd_g_norm, m_ssd_w_out, v_norm_g, v_final_g, v_mla_w_in, v_mla_g_q, v_mla_w_uq, v_mla_g_kv, v_mla_w_ukv, v_mla_w_out, v_gla_w_in, v_gla_w_gk2, v_gla_b_gk, v_gla_g_o, v_gla_w_out, v_lru_w_in, v_lru_conv_w, v_lru_conv_b, v_lru_w_a, v_lru_b_a, v_lru_w_x, v_lru_b_x, v_lru_lam, v_lru_w_out, v_ssd_w_in, v_ssd_conv_w, v_ssd_conv_b, v_ssd_dt_bias, v_ssd_a_log, v_ssd_d, v_ssd_g_norm, v_ssd_w_out):
    given = dict(locals())
    stacked = [n for n in WEIGHTS if n not in ("norm_g", "final_g")]

    def blocks(prefix):
        return {n: (given[prefix + n][0] if n in stacked else given[prefix + n]) for n in WEIGHTS}

    raw = [{n: given[prefix + n] for n in BIG} for prefix in ("", "m_", "v_")]
    loss, dx, out = _train_step(x[0], positions[0], loss_target[0], blocks(""), blocks("m_"), blocks("v_"), raw)
    res = [loss, dx[None]]
    for kind in ("grad", "delta", "new_m", "new_v"):
        res += [(out[kind, n][None] if n in stacked and n not in BIG else out[kind, n]) for n in WEIGHTS]
    return tuple(res)
```

```python
import functools
import math

import jax
import jax.numpy as jnp
from jax import lax
from jax.experimental import pallas as pl
from jax.experimental.pallas import tpu as pltpu

F32 = jnp.float32
BF16 = jnp.bfloat16

V7X_VMEM_BYTES = 64 * 1024 * 1024
VMEM_LIMIT = V7X_VMEM_BYTES - 8 * 1024 * 1024
LANE = 128

D_MODEL = 1024
NORM_EPS = 1e-6
MLA_HEADS, MLA_Q_RANK, MLA_KV_RANK, MLA_NOPE, MLA_ROPE, MLA_V = 16, 384, 256, 64, 32, 64
MLA_QK = MLA_NOPE + MLA_ROPE
ROPE_THETA = 10000.0
GLA_HEADS, GLA_DK, GLA_DV, GLA_RANK, GLA_TAU, GLA_CHUNK = 4, 128, 256, 16, 16.0, 64
LRU_WIDTH, LRU_BLOCKS, LRU_BLOCK, LRU_C, CONV_W = 1280, 10, 128, 8.0, 4
SSD_INNER, SSD_P, SSD_HEADS, SSD_GROUPS, SSD_HPG, SSD_STATE, SSD_CHUNK = 2048, 64, 32, 8, 4, 128, 64
ADAM_LR, ADAM_B1, ADAM_B2, ADAM_EPS, ADAM_WD, ADAM_STEP = 0.001, 0.9, 0.999, 1e-08, 0.01, 10

_CP = functools.partial(pltpu.CompilerParams, vmem_limit_bytes=VMEM_LIMIT)


def _bdot(a, b):
    return jnp.dot(a.astype(BF16), b.astype(BF16), preferred_element_type=F32)


def _bdot_nt(a, b):
    return lax.dot_general(a.astype(BF16), b.astype(BF16), (((1,), (1,)), ((), ())), preferred_element_type=F32)


def _bdot_tn(a, b):
    return lax.dot_general(a.astype(BF16), b.astype(BF16), (((0,), (0,)), ((), ())), preferred_element_type=F32)


def _tri(n):
    r = lax.broadcasted_iota(jnp.int32, (n, n), 0)
    c = lax.broadcasted_iota(jnp.int32, (n, n), 1)
    return r >= c


def _rms(x, g):
    return x * lax.rsqrt(jnp.mean(x * x, axis=-1, keepdims=True) + NORM_EPS) * g


def _silu(x):
    return x * jax.nn.sigmoid(x)


def _shift_rows(x, prev, j):
    if j == 0:
        return x
    t = x.shape[0]

    def fwd_impl(x, prev):
        row = lax.broadcasted_iota(jnp.int32, x.shape, 0)
        return jnp.where(row >= j, pltpu.roll(x, j, 0), pltpu.roll(prev, j, 0))

    @jax.custom_vjp
    def sh(x, prev):
        return fwd_impl(x, prev)

    def sh_fwd(x, prev):
        return fwd_impl(x, prev), None

    def sh_bwd(_, gy):
        row = lax.broadcasted_iota(jnp.int32, gy.shape, 0)
        back = pltpu.roll(gy, t - j, 0)
        return jnp.where(row < t - j, back, 0.0), jnp.where(row >= t - j, back, 0.0)

    sh.defvjp(sh_fwd, sh_bwd)
    return sh(x, prev)


def _cumsum_rows(x):
    zero = jnp.zeros_like(x)
    sh = 1
    while sh < x.shape[0]:
        x = x + _shift_rows(x, zero, sh)
        sh *= 2
    return x


def _one_minus_exp(x):
    series = -x * (1.0 + x * (0.5 + x * (1.0 / 6.0 + x * (1.0 / 24.0 + x * (1.0 / 120.0)))))
    return jnp.where(x > -0.05, series, 1.0 - jnp.exp(x))


def _tile(n, cap):
    if n <= cap:
        return n
    best = None
    for t in range(LANE, cap + 1, LANE):
        if n % t == 0:
            best = t
    assert best is not None, (n, cap)
    return best


MM_BLOCK_BYTES = 8 * 1024 * 1024
MM_ROWS, MM_KROWS = 256, 512
MM_TILE_BYTES = 2 * 1024 * 1024


def _mm_tiles(m, k, n, ta):
    if ta:
        return m, _tile(n, max(LANE, MM_BLOCK_BYTES // (4 * m) // LANE * LANE)), _tile(k, MM_KROWS)
    tn = _tile(n, max(LANE, MM_BLOCK_BYTES // (2 * k) // LANE * LANE))
    rows = min(4 * MM_ROWS, max(MM_ROWS, MM_TILE_BYTES // (4 * tn) // MM_ROWS * MM_ROWS))
    return _tile(m, rows), tn, k


def _mm(name, a, b, *, ta=False, tb=False, add=None, out_dtype=F32):
    m, k = (a.shape[1], a.shape[0]) if ta else a.shape
    n, kb = (b.shape[0], b.shape[1]) if tb else (b.shape[1], b.shape[0])
    assert k == kb, (name, a.shape, b.shape, ta, tb)
    tm, tn, tk = _mm_tiles(m, k, n, ta)
    nk = k // tk
    dn = (((0 if ta else 1,), (1 if tb else 0,)), ((), ()))
    has_add = add is not None

    def finish(refs, r):
        if has_add:
            r = r + refs[2][...].astype(F32)
        return r.astype(out_dtype)

    def body_one(*refs):
        a_ref, b_ref, o_ref = refs[0], refs[1], refs[-1]
        o_ref[...] = finish(refs, lax.dot_general(a_ref[...].astype(BF16), b_ref[...].astype(BF16), dn, preferred_element_type=F32))

    def body_acc(*refs):
        a_ref, b_ref = refs[0], refs[1]
        o_ref, acc = refs[-2], refs[-1]
        kk = pl.program_id(2)

        @pl.when(kk == 0)
        def _():
            acc[...] = jnp.zeros(acc.shape, F32)

        acc[...] += lax.dot_general(a_ref[...].astype(BF16), b_ref[...].astype(BF16), dn, preferred_element_type=F32)

        @pl.when(kk == nk - 1)
        def _():
            o_ref[...] = finish(refs, acc[...])

    a_spec = pl.BlockSpec((tk, tm), lambda i, j, q: (q, i)) if ta else pl.BlockSpec((tm, tk), lambda i, j, q: (i, q))
    b_spec = pl.BlockSpec((tn, tk), lambda i, j, q: (j, q)) if tb else pl.BlockSpec((tk, tn), lambda i, j, q: (q, j))
    o_spec = pl.BlockSpec((tm, tn), lambda i, j, q: (i, j))
    in_specs, args = [a_spec, b_spec], [a, b]
    if has_add:
        in_specs.append(o_spec)
        args.append(add)
    return pl.pallas_call(
        body_one if nk == 1 else body_acc, name=name, grid=(m // tm, n // tn, nk), in_specs=in_specs, out_specs=o_spec,
        out_shape=jax.ShapeDtypeStruct((m, n), out_dtype), scratch_shapes=[] if nk == 1 else [pltpu.VMEM((tm, tn), F32)],
        compiler_params=_CP(dimension_semantics=("parallel", "parallel", "arbitrary")),
    )(*args)


def _mm_pieces(name, a, bs):
    m, k = a.shape
    n_b = len(bs)
    tm = _tile(m, MM_ROWS)

    def body(*refs):
        av = refs[0][...].astype(BF16)
        for b_ref, o_ref in zip(refs[1:1 + n_b], refs[1 + n_b:]):
            o_ref[...] = jnp.dot(av, b_ref[...].astype(BF16), preferred_element_type=F32)

    return pl.pallas_call(
        body, name=name, grid=(m // tm,),
        in_specs=[pl.BlockSpec((tm, k), lambda i: (i, 0))] + [pl.BlockSpec(b.shape, lambda i: (0, 0)) for b in bs],
        out_specs=[pl.BlockSpec((tm, b.shape[1]), lambda i: (i, 0)) for b in bs],
        out_shape=[jax.ShapeDtypeStruct((m, b.shape[1]), F32) for b in bs],
        compiler_params=_CP(dimension_semantics=("parallel",)),
    )(a, *bs)


def _mm_pieces_t(name, dps, ws):
    m, k = dps[0].shape[0], ws[0].shape[0]
    n_b = len(ws)
    tm = _tile(m, MM_ROWS)
    dn = (((1,), (1,)), ((), ()))

    def body(*refs):
        acc = None
        for d_ref, w_ref in zip(refs[:n_b], refs[n_b:2 * n_b]):
            part = lax.dot_general(d_ref[...].astype(BF16), w_ref[...].astype(BF16), dn, preferred_element_type=F32)
            acc = part if acc is None else acc + part
        refs[-1][...] = acc

    return pl.pallas_call(
        body, name=name, grid=(m // tm,),
        in_specs=[pl.BlockSpec((tm, d.shape[1]), lambda i: (i, 0)) for d in dps] + [pl.BlockSpec(w.shape, lambda i: (0, 0)) for w in ws],
        out_specs=pl.BlockSpec((tm, k), lambda i: (i, 0)), out_shape=jax.ShapeDtypeStruct((m, k), F32),
        compiler_params=_CP(dimension_semantics=("parallel",)),
    )(*dps, *ws)


def _mm_pieces_dw(name, u, dps):
    s, k = u.shape
    n_b = len(dps)
    tk = _tile(s, MM_KROWS)
    n_steps = s // tk
    dn = (((0,), (0,)), ((), ()))

    def body(*refs):
        u_ref, d_refs, o_refs, accs = refs[0], refs[1:1 + n_b], refs[1 + n_b:1 + 2 * n_b], refs[1 + 2 * n_b:]
        step = pl.program_id(0)
        uv = u_ref[...].astype(BF16)
        for d_ref, o_ref, acc in zip(d_refs, o_refs, accs):
            part = lax.dot_general(uv, d_ref[...].astype(BF16), dn, preferred_element_type=F32)

            @pl.when(step == 0)
            def _(acc=acc, part=part):
                acc[...] = part

            @pl.when(step > 0)
            def _(acc=acc, part=part):
                acc[...] += part

            @pl.when(step == n_steps - 1)
            def _(acc=acc, o_ref=o_ref):
                o_ref[...] = acc[...].astype(o_ref.dtype)

    return pl.pallas_call(
        body, name=name, grid=(n_steps,),
        in_specs=[pl.BlockSpec((tk, k), lambda i: (i, 0))] + [pl.BlockSpec((tk, d.shape[1]), lambda i: (i, 0)) for d in dps],
        out_specs=[pl.BlockSpec((k, d.shape[1]), lambda i: (0, 0)) for d in dps],
        out_shape=[jax.ShapeDtypeStruct((k, d.shape[1]), BF16) for d in dps],
        scratch_shapes=[pltpu.VMEM((k, d.shape[1]), F32) for d in dps],
        compiler_params=_CP(dimension_semantics=("arbitrary",)),
    )(u, *dps)


MM_DW_MERGE_COLS = 4096


class In:
    def __init__(self, arr, block, imap, kind="x", per_h=False, gdtype=F32, gshape=None, gimap=None, prefixed=False):
        self.arr, self.block, self.imap, self.kind, self.per_h, self.gdtype = arr, tuple(block), imap, kind, per_h, gdtype
        self.prefixed = prefixed
        self.gshape = tuple(gshape) if gshape is not None else tuple(arr.shape)
        self.gimap = gimap if gimap is not None else imap

    def spec(self, rev_g=None):
        imap = self.imap
        if rev_g is None:
            return pl.BlockSpec(self.block, lambda h, g: imap(h, g))
        return pl.BlockSpec(self.block, lambda h, g: imap(h, rev_g - 1 - g))


class Out:
    def __init__(self, shape, dtype, block, imap):
        self.shape, self.dtype, self.block, self.imap = tuple(shape), dtype, tuple(block), imap

    def spec(self, rev_g=None):
        imap = self.imap
        if rev_g is None:
            return pl.BlockSpec(self.block, lambda h, g: imap(h, g))
        return pl.BlockSpec(self.block, lambda h, g: imap(h, rev_g - 1 - g))


def _load_f32(ref, rows=None):
    v = ref[...] if rows is None else ref[0:rows]
    return v.astype(F32) if jnp.issubdtype(v.dtype, jnp.floating) else v


def _state_out(grid, shape):
    nd = len(shape)
    return Out(tuple(grid) + tuple(shape), F32, (None, None) + tuple(shape), lambda h, g: (h, g) + (0,) * nd)


def _carry(comm, grid, refs, n_in, n_out, n_scr):
    n_c = len(comm.ops) if comm is not None else 0
    n_s = len(comm.sem_shapes) if comm is not None else 0
    p = 0
    in_refs = refs[p:p + n_in]; p += n_in
    c_src = refs[p:p + n_c]; p += n_c
    out_refs = refs[p:p + n_out]; p += n_out
    c_dst = refs[p:p + n_c]; p += n_c
    scr = refs[p:p + n_scr]; p += n_scr
    c_sem = refs[p:p + n_s]
    step = pl.program_id(0) * grid[1] + pl.program_id(1)
    n_steps = grid[0] * grid[1]
    when = (step == 0, step == _forward_step(n_steps), step == n_steps - 1)
    return in_refs, out_refs, scr, (c_src, c_dst, c_sem), when


def _forward_step(n_steps):
    return max(0, min(n_steps - 2, (3 * n_steps) // 4))


def _carry_specs(comm):
    if comm is None:
        return [], [], [], [], []
    any_spec = pl.BlockSpec(memory_space=pl.ANY)
    n = len(comm.ops)
    return [any_spec] * n, list(comm.ops), [any_spec] * n, list(comm.out_shapes), list(comm.sem_shapes)


def _op_fwd(name, f, grid, ins, outs, state_shapes=(), comm=None, prefix_rows=None):
    assert prefix_rows is None or not state_shapes
    n_in, n_out, n_st = len(ins), len(outs), len(state_shapes)
    st_outs = [_state_out(grid, s) for s in state_shapes]

    def body(*refs):
        in_refs, o_refs, st_scr, cargs, (first, fwd_step, last) = _carry(comm, grid, refs, n_in, n_out + n_st, n_st)
        out_refs, sv_refs = o_refs[:n_out], o_refs[n_out:]
        if comm is not None:
            @pl.when(first)
            def _():
                comm.start(*cargs)

            @pl.when(fwd_step)
            def _():
                comm.forward(*cargs)
        g = pl.program_id(1)
        if n_st:
            @pl.when(g == 0)
            def _():
                for s in st_scr:
                    s[...] = jnp.zeros(s.shape, F32)

        def compute(rows, g=g):
            vals = [_load_f32(r, rows if i.prefixed else None) for r, i in zip(in_refs, ins)]
            sts = [s[...] for s in st_scr]
            o, ns = f(g, vals, sts)
            for r, v in zip(out_refs, o):
                r[...] = v.astype(r.dtype)
            for r, s in zip(sv_refs, sts):
                r[...] = s
            for s, v in zip(st_scr, ns):
                s[...] = v

        if prefix_rows is None:
            compute(None)
        else:
            per = grid[1] // len(prefix_rows)
            for lv, rows in enumerate(prefix_rows):
                pl.when(g // per == lv)(functools.partial(compute, rows, lv) if per == 1 else functools.partial(compute, rows))
        if comm is not None:
            @pl.when(last)
            def _():
                comm.finish(*cargs)

    all_outs = list(outs) + st_outs
    c_in_specs, c_args, c_out_specs, c_out_shapes, c_sems = _carry_specs(comm)
    res = pl.pallas_call(
        body, name=name, grid=tuple(grid), in_specs=[i.spec() for i in ins] + c_in_specs,
        out_specs=[o.spec() for o in all_outs] + c_out_specs,
        out_shape=[jax.ShapeDtypeStruct(o.shape, o.dtype) for o in all_outs] + c_out_shapes,
        scratch_shapes=[pltpu.VMEM(tuple(s), F32) for s in state_shapes] + c_sems,
        compiler_params=_CP(dimension_semantics=("arbitrary", "arbitrary")),
    )(*[i.arr for i in ins], *c_args)
    n_all = n_out + n_st
    return list(res[:n_out]), list(res[n_out:n_all]), list(res[n_all:])


def _op_bwd(name, f, grid, ins, outs, state_shapes, saved, douts, addto=None, comm=None, prefix_rows=None):
    n_in, n_out, n_st = len(ins), len(outs), len(state_shapes)
    n_g = grid[1]
    assert prefix_rows is None or all(i.prefixed and i.per_h for i in ins if i.kind == "p")
    addto = addto or {}
    diff = [k for k, i in enumerate(ins) if i.kind in ("x", "p")]
    add_idx = sorted(addto)
    st_ins = [In(s, o.block, o.imap, "c") for s, o in zip(saved, [_state_out(grid, s) for s in state_shapes])]
    dout_ins = [In(d, o.block, o.imap, "c") for d, o in zip(douts, outs)]
    add_ins = []
    for k in add_idx:
        i, a = ins[k], addto[k]
        blk = i.block if i.kind == "x" else i.block[:-2] + a.shape[-2:]
        add_ins.append(In(a, blk, i.gimap if i.kind == "x" else i.imap, "c"))
    g_outs = []
    for k in diff:
        i = ins[k]
        g_outs.append(Out(i.gshape, i.gdtype if i.kind == "x" else F32, i.block, i.gimap))

    def body(*refs):
        all_in, go_refs, ds_scr, cargs, (first_step, fwd_step, last_step) = _carry(comm, grid, refs, n_in + n_st + n_out + len(add_idx),
                                                                                    len(diff), n_st)
        if comm is not None:
            @pl.when(first_step)
            def _():
                comm.start(*cargs)

            @pl.when(fwd_step)
            def _():
                comm.forward(*cargs)
        p = 0
        in_refs = all_in[p:p + n_in]; p += n_in
        sv_refs = all_in[p:p + n_st]; p += n_st
        do_refs = all_in[p:p + n_out]; p += n_out
        ad_refs = all_in[p:p + len(add_idx)]
        hh = pl.program_id(0)
        step = pl.program_id(1)
        g = n_g - 1 - step
        if n_st:
            @pl.when(step == 0)
            def _():
                for s in ds_scr:
                    s[...] = jnp.zeros(s.shape, F32)

        def compute(rows, g=g):
            vals = [_load_f32(r, rows if i.prefixed else None) for r, i in zip(in_refs, ins)]
            sts = [r[...] for r in sv_refs]

            def fw(dvals, states):
                full = list(vals)
                for k, v in zip(diff, dvals):
                    full[k] = v
                o, ns = f(g, full, states)
                return list(o), list(ns)

            _, vjp = jax.vjp(fw, [vals[k] for k in diff], sts)
            cts = [r[...].astype(F32) for r in do_refs]
            dns = [s[...] for s in ds_scr]
            dvals, dsts = vjp((cts, dns))
            adds = dict(zip(add_idx, ad_refs))
            for k, r, dv in zip(diff, go_refs, dvals):
                i = ins[k]
                if i.kind == "x":
                    if k in adds:
                        dv = dv + adds[k][...].astype(F32)
                    r[...] = dv.astype(r.dtype)
                elif rows is not None:
                    r[0:rows] += dv
                else:
                    first = (step == 0) if i.per_h else jnp.logical_and(step == 0, hh == 0)

                    @pl.when(first)
                    def _(r=r, dv=dv, k=k):
                        r[...] = dv
                        if k in adds:
                            lead = adds[k].shape[0]
                            r[0:lead] += adds[k][...]

                    @pl.when(jnp.logical_not(first))
                    def _(r=r, dv=dv):
                        r[...] += dv
            for s, v in zip(ds_scr, dsts):
                s[...] = v

        if prefix_rows is None:
            compute(None)
        else:
            @pl.when(step == 0)
            def _():
                for k, r in zip(diff, go_refs):
                    if ins[k].kind == "p":
                        r[...] = jnp.zeros(r.shape, F32)
            per = n_g // len(prefix_rows)
            for lv, rows in enumerate(prefix_rows):
                pl.when(g // per == lv)(functools.partial(compute, rows, lv) if per == 1 else functools.partial(compute, rows))
        if comm is not None:
            @pl.when(last_step)
            def _():
                comm.finish(*cargs)

    all_ins = list(ins) + st_ins + dout_ins + add_ins
    c_in_specs, c_args, c_out_specs, c_out_shapes, c_sems = _carry_specs(comm)
    res = pl.pallas_call(
        body, name=name, grid=tuple(grid), in_specs=[i.spec(n_g) for i in all_ins] + c_in_specs,
        out_specs=[o.spec(n_g) for o in g_outs] + c_out_specs,
        out_shape=[jax.ShapeDtypeStruct(o.shape, o.dtype) for o in g_outs] + c_out_shapes,
        scratch_shapes=[pltpu.VMEM(tuple(s), F32) for s in state_shapes] + c_sems,
        compiler_params=_CP(dimension_semantics=("arbitrary", "arbitrary")),
    )(*[i.arr for i in all_ins], *c_args)
    return list(res[:len(g_outs)]), list(res[len(g_outs):])


class Op:
    def __init__(self, name, f, grid, ins, outs, state_shapes=(), prefix_rows=None):
        self.name, self.f, self.grid, self.ins, self.outs, self.state_shapes = name, f, grid, ins, outs, state_shapes
        self.prefix_rows = prefix_rows
        self.saved = None

    def fwd(self, comm=None):
        res, self.saved, self.fwd_comm_out = _op_fwd(self.name + "_fwd", self.f, self.grid, self.ins, self.outs, self.state_shapes, comm,
                                                     self.prefix_rows)
        return res

    def bwd(self, douts, addto=None, comm=None):
        res, self.bwd_comm_out = _op_bwd(self.name + "_bwd", self.f, self.grid, self.ins, self.outs, self.state_shapes, self.saved, douts,
                                         addto, comm, self.prefix_rows)
        return res


def _rows(arr, t, kind="x", gdtype=F32):
    return In(arr, (t, arr.shape[1]), lambda h, g: (g, 0), kind, gdtype=gdtype)


def _whole(arr, kind="p"):
    nd = arr.ndim
    return In(arr, arr.shape, lambda h, g: (0,) * nd, kind)


def _rows_out(s, n, t, dtype):
    return Out((s, n), dtype, (t, n), lambda h, g: (g, 0))


ROW_T = 256


def _rms_op(name, x, gain, out_dtype=BF16, gdtype=F32):
    s, n = x.shape

    def f(g, vals, sts):
        return [_rms(vals[0], vals[1])], []

    return Op(name, f, (1, s // ROW_T), [_rows(x, ROW_T, gdtype=gdtype), _whole(gain.reshape(1, n))], [_rows_out(s, n, ROW_T, out_dtype)])


def _mla_prep_op(qn, q1, q2, kn, kr, v, cos, sin):
    s = qn.shape[0]
    hd, half = MLA_HEADS, MLA_ROPE // 2

    def f(g, vals, sts):
        qn, q1, q2, kn, kr, v, cos, sin = vals
        cos_h, sin_h = jnp.tile(cos, (1, hd)), jnp.tile(sin, (1, hd))
        r1 = q1 * cos_h - q2 * sin_h
        r2 = q2 * cos_h + q1 * sin_h
        k1, k2 = kr[:, 0:half], kr[:, half:2 * half]
        kr1 = k1 * cos - k2 * sin
        kr2 = k2 * cos + k1 * sin
        zpad = jnp.zeros((qn.shape[0], LANE - MLA_QK), F32)
        qs, ks, vs = [], [], []
        for h in range(hd):
            a, b = h * MLA_NOPE, (h + 1) * MLA_NOPE
            c, d = h * half, (h + 1) * half
            qs.append(jnp.concatenate([qn[:, a:b], r1[:, c:d], r2[:, c:d], zpad], axis=1))
            ks.append(jnp.concatenate([kn[:, a:b], kr1, kr2, zpad], axis=1))
            vs.append(v[:, a:b])
        return [jnp.stack(qs, 0), jnp.stack(ks, 0), jnp.stack(vs, 0)], []

    ins = [_rows(qn, ROW_T, gdtype=BF16), _rows(q1, ROW_T, gdtype=BF16), _rows(q2, ROW_T, gdtype=BF16), _rows(kn, ROW_T, gdtype=BF16),
           _rows(kr, ROW_T, gdtype=BF16), _rows(v, ROW_T, gdtype=BF16), _rows(cos, ROW_T, "c"), _rows(sin, ROW_T, "c")]
    outs = [Out((hd, s, LANE), BF16, (hd, ROW_T, LANE), lambda h, g: (0, g, 0)),
            Out((hd, s, LANE), BF16, (hd, ROW_T, LANE), lambda h, g: (0, g, 0)),
            Out((hd, s, MLA_V), BF16, (hd, ROW_T, MLA_V), lambda h, g: (0, g, 0))]
    return Op("mla_prep", f, (1, s // ROW_T), ins, outs)


ATT_TQ = 256
ATT_LEVELS = 8


def _mla_attn_op(q, k, v):
    hd, s, _ = q.shape
    scale = MLA_QK ** -0.5

    def f(g, vals, sts):
        q, k, v = vals
        sc = _bdot_nt(q, k) * scale
        r = lax.broadcasted_iota(jnp.int32, sc.shape, 0) + g * ATT_TQ
        c = lax.broadcasted_iota(jnp.int32, sc.shape, 1)
        sc = jnp.where(r >= c, sc, -1e30)
        m = lax.stop_gradient(jnp.max(sc, axis=-1, keepdims=True))
        p = jnp.exp(sc - m)
        p = p * (1.0 / jnp.sum(p, axis=-1, keepdims=True))
        return [_bdot(p, v)], []

    ins = [In(q, (None, ATT_TQ, LANE), lambda h, g: (h, g, 0), "x", gdtype=BF16),
           In(k, (None, s, LANE), lambda h, g: (h, 0, 0), "p", per_h=True, prefixed=True),
           In(v, (None, s, MLA_V), lambda h, g: (h, 0, 0), "p", per_h=True, prefixed=True)]
    outs = [Out((hd, s, MLA_V), F32, (None, ATT_TQ, MLA_V), lambda h, g: (h, g, 0))]
    return Op("mla_attn", f, (hd, s // ATT_TQ), ins, outs, prefix_rows=[(lv + 1) * (s // ATT_LEVELS) for lv in range(ATT_LEVELS)])


def _mla_post_op(o, gate):
    hd, s, _ = o.shape

    def f(g, vals, sts):
        o, gate = vals
        cat = jnp.concatenate([o[h] for h in range(hd)], axis=1)
        return [cat * _silu(gate)], []

    ins = [In(o, (hd, ROW_T, MLA_V), lambda h, g: (0, g, 0), "x"), _rows(gate, ROW_T, gdtype=BF16)]
    return Op("mla_post", f, (1, s // ROW_T), ins, [_rows_out(s, hd * MLA_V, ROW_T, BF16)])


def _gla_gate_op(gk, w2, b):
    s = gk.shape[0]

    def f(g, vals, sts):
        gk, w2, b = vals
        return [jax.nn.log_sigmoid(_bdot(gk, w2) + b) / GLA_TAU], []

    ins = [_rows(gk, ROW_T, gdtype=BF16), _whole(w2), _whole(b)]
    return Op("gla_gate", f, (1, s // ROW_T), ins, [_rows_out(s, GLA_HEADS * GLA_DK, ROW_T, F32)])


def _gla_core_op(q, k, v, gate, la, g_o):
    s = q.shape[0]
    c, nh = GLA_CHUNK, GLA_HEADS

    def f(g, vals, sts):
        q, k, v, gate, la, g_o = vals
        tri = _tri(c)
        b = _cumsum_rows(la)
        b_last = jnp.sum(la, axis=0, keepdims=True)
        qt = q * (GLA_DK ** -0.5) * jnp.exp(b)
        kt = k * jnp.exp(-b)
        kd = k * jnp.exp(b_last - b)
        ys, new_sts = [], []
        for h in range(nh):
            ks, vs = slice(h * GLA_DK, (h + 1) * GLA_DK), slice(h * GLA_DV, (h + 1) * GLA_DV)
            att = jnp.where(tri, _bdot_nt(qt[:, ks], kt[:, ks]), 0.0)
            o = _bdot(att, v[:, vs]) + _bdot_nt(qt[:, ks], sts[h])
            new_sts.append(jnp.exp(b_last[:, ks]) * sts[h] + _bdot_tn(v[:, vs], kd[:, ks]))
            ys.append(_rms(o, g_o) * _silu(gate[:, vs]))
        return [jnp.concatenate(ys, axis=1)], new_sts

    ins = [_rows(q, c, gdtype=BF16), _rows(k, c, gdtype=BF16), _rows(v, c, gdtype=BF16), _rows(gate, c, gdtype=BF16), _rows(la, c), _whole(g_o)]
    outs = [_rows_out(s, nh * GLA_DV, c, BF16)]
    return Op("gla_core", f, (1, s // c), ins, outs, [(GLA_DV, GLA_DK)] * nh)


LRU_T = 256


def _lru_op(gate, u, conv_w, conv_b, w_a, b_a, w_x, b_x, lam):
    s, w = u.shape
    t = LRU_T

    def f(g, vals, sts):
        gate, u, cw, cb, w_a, b_a, w_x, b_x, lam = vals
        u_prev, h_prev = sts
        uc = cb
        for kk in range(CONV_W):
            uc = uc + cw[kk] * _shift_rows(u, u_prev, CONV_W - 1 - kk)
        ra, ri = [], []
        for n in range(LRU_BLOCKS):
            blk = uc[:, n * LRU_BLOCK:(n + 1) * LRU_BLOCK]
            ra.append(_bdot(blk, w_a[n]))
            ri.append(_bdot(blk, w_x[n]))
        r = jax.nn.sigmoid(jnp.concatenate(ra, axis=1) + b_a)
        i = jax.nn.sigmoid(jnp.concatenate(ri, axis=1) + b_x)
        log_a = -LRU_C * r * jax.nn.softplus(-lam)
        a = jnp.exp(log_a)
        bb = jnp.sqrt(_one_minus_exp(2.0 * log_a)) * (i * uc)
        zero = jnp.zeros_like(a)
        sh = 1
        while sh < t:
            a_s = _shift_rows(a - 1.0, zero, sh) + 1.0
            b_s = _shift_rows(bb, zero, sh)
            bb = a * b_s + bb
            a = a * a_s
            sh *= 2
        hs = bb + a * h_prev
        last = (lax.broadcasted_iota(jnp.int32, hs.shape, 0) == t - 1).astype(F32)
        h_last = jnp.sum(hs * last, axis=0, keepdims=True)
        return [hs * _silu(gate)], [u, h_last]

    ins = [_rows(gate, t, gdtype=BF16), _rows(u, t, gdtype=BF16), _whole(conv_w), _whole(conv_b), _whole(w_a), _whole(b_a), _whole(w_x),
           _whole(b_x), _whole(lam)]
    return Op("lru_core", f, (1, s // t), ins, [_rows_out(s, w, t, BF16)], [(t, w), (1, w)])


def _ssd_conv_op(xbc, conv_w, conv_b):
    s, w = xbc.shape
    t = ROW_T
    n_x, n_b = SSD_INNER, SSD_GROUPS * SSD_STATE

    def f(g, vals, sts):
        xbc, cw, cb = vals
        acc = cb
        for kk in range(CONV_W):
            acc = acc + cw[kk] * _shift_rows(xbc, sts[0], CONV_W - 1 - kk)
        y = _silu(acc)
        return [y[:, :n_x], y[:, n_x:n_x + n_b], y[:, n_x + n_b:]], [xbc]

    ins = [_rows(xbc, t, gdtype=BF16), _whole(conv_w), _whole(conv_b)]
    outs = [_rows_out(s, n_x, t, F32), _rows_out(s, n_b, t, BF16), _rows_out(s, n_b, t, BF16)]
    return Op("ssd_conv", f, (1, s // t), ins, outs, [(t, w)])


SSD_L = 512


def _ssd_core_op(x, bm, cm, z, dt, dt_bias, a_log, d_skip, g_norm):
    s = x.shape[0]
    c, hg, p = SSD_L, SSD_HPG, SSD_P
    gw = hg * p

    def f(g, vals, sts):
        x, bm, cm, z, dtr, dt_bias, a_log, d_skip, g_norm = vals
        tri = _tri(c)
        dt = jax.nn.softplus(dtr + dt_bias)
        da = dt * (-jnp.exp(a_log))
        cs = _cumsum_rows(da)
        cs_last = jnp.sum(da, axis=0, keepdims=True)
        cs_t = jnp.transpose(jnp.concatenate([cs, jnp.zeros((c, LANE - hg), F32)], axis=1))
        cb = _bdot_nt(cm, bm)
        ys, new_st = [], []
        for h in range(hg):
            cs_h = cs[:, h:h + 1]
            cs_row = cs_t[h:h + 1, :]
            seg = jnp.where(tri, cs_h - cs_row, 0.0)
            lmat = jnp.where(tri, jnp.exp(seg), 0.0)
            x_h = x[:, h * p:(h + 1) * p]
            xdt = x_h * dt[:, h:h + 1]
            y_diag = _bdot(cb * lmat, xdt)
            decay = jnp.exp(cs_last[:, h:h + 1] - cs_h)
            states = _bdot_tn(xdt * decay, bm)
            y_off = _bdot_nt(cm, sts[h]) * jnp.exp(cs_h)
            new_st.append(jnp.exp(cs_last[:, h:h + 1]) * sts[h] + states)
            ys.append(y_diag + y_off + d_skip[:, h:h + 1] * x_h)
        y = jnp.concatenate(ys, axis=1) * _silu(z)
        return [_rms(y, g_norm)], new_st

    ins = [In(x, (c, gw), lambda h, g: (g, h), "x"), In(bm, (c, SSD_STATE), lambda h, g: (g, h), "x", gdtype=BF16),
           In(cm, (c, SSD_STATE), lambda h, g: (g, h), "x", gdtype=BF16), In(z, (c, gw), lambda h, g: (g, h), "x", gdtype=BF16),
           In(dt, (None, c, hg), lambda h, g: (h, g, 0), "x"),
           In(dt_bias, (None, 1, hg), lambda h, g: (h, 0, 0), "p", per_h=True),
           In(a_log, (None, 1, hg), lambda h, g: (h, 0, 0), "p", per_h=True),
           In(d_skip, (None, 1, hg), lambda h, g: (h, 0, 0), "p", per_h=True),
           In(g_norm, (1, gw), lambda h, g: (0, h), "p", per_h=True)]
    outs = [Out((s, SSD_INNER), BF16, (c, gw), lambda h, g: (g, h))]
    return Op("ssd_core", f, (SSD_GROUPS, s // c), ins, outs, [(p, SSD_STATE)] * hg)


def _loss_op(h, target, final_g):
    s, n = h.shape
    t = ROW_T
    n_g = s // t

    def body(h_ref, t_ref, g_ref, loss_ref, dh_ref, dg_ref):
        step = pl.program_id(0)

        def lossf(hv, gv):
            err = _rms(hv, gv) - t_ref[...]
            return 0.5 * jnp.sum(jnp.mean(err * err, axis=-1))

        l, (dh, dg) = jax.value_and_grad(lossf, argnums=(0, 1))(h_ref[...], g_ref[...])
        dh_ref[...] = dh

        @pl.when(step == 0)
        def _():
            loss_ref[...] = jnp.zeros(loss_ref.shape, F32)
            dg_ref[...] = jnp.zeros(dg_ref.shape, F32)

        loss_ref[...] += jnp.full(loss_ref.shape, l, F32)
        dg_ref[...] += dg

    row = pl.BlockSpec((t, n), lambda g: (g, 0))
    one = pl.BlockSpec((1, n), lambda g: (0, 0))
    return pl.pallas_call(
        body, name="loss_head", grid=(n_g,), in_specs=[row, row, one],
        out_specs=[pl.BlockSpec((1, LANE), lambda g: (0, 0)), row, one],
        out_shape=[jax.ShapeDtypeStruct((1, LANE), F32), jax.ShapeDtypeStruct((s, n), F32), jax.ShapeDtypeStruct((1, n), F32)],
        compiler_params=_CP(dimension_semantics=("arbitrary",)),
    )(h, target, final_g.reshape(1, n))


def _pad_cols(w, n):
    return jnp.pad(w, ((0, 0), (0, n - w.shape[1])))


def _pad_rows(w, n):
    return jnp.pad(w, ((0, n - w.shape[0]), (0, 0)))


def _proj_bwd(tag, u, dps, ws):
    du = _mm_pieces_t(f"{tag}_du", dps, ws)
    if sum(dp.shape[1] for dp in dps) <= MM_DW_MERGE_COLS:
        dws = _mm_pieces_dw(f"{tag}_dw", u, dps)
    else:
        dws = [_mm(f"{tag}_dw{i}", u, dp, ta=True, out_dtype=BF16) for i, dp in enumerate(dps)]
    return du, dws


def _mla_layer(h, norm_g, w, cos, sin, fwd_comm=None, late_w_out=None):
    bf = lambda a: a.astype(BF16)
    w_in, w_uq, w_ukv = w["mla_w_in"], w["mla_w_uq"], w["mla_w_ukv"]
    a0, a1, a2 = MLA_Q_RANK, MLA_Q_RANK + MLA_KV_RANK, MLA_Q_RANK + MLA_KV_RANK + MLA_ROPE
    w_cq, w_ckv, w_kr, w_g = bf(w_in[:, :a0]), bf(w_in[:, a0:a1]), bf(_pad_cols(w_in[:, a1:a2], LANE)), bf(w_in[:, a2:])
    uq = w_uq.reshape(MLA_Q_RANK, MLA_HEADS, MLA_QK)
    half = MLA_ROPE // 2
    w_qn = bf(uq[:, :, :MLA_NOPE].reshape(MLA_Q_RANK, -1))
    w_q1 = bf(uq[:, :, MLA_NOPE:MLA_NOPE + half].reshape(MLA_Q_RANK, -1))
    w_q2 = bf(uq[:, :, MLA_NOPE + half:].reshape(MLA_Q_RANK, -1))
    ukv = w_ukv.reshape(MLA_KV_RANK, MLA_HEADS, MLA_NOPE + MLA_V)
    w_kn = bf(ukv[:, :, :MLA_NOPE].reshape(MLA_KV_RANK, -1))
    w_v = bf(ukv[:, :, MLA_NOPE:].reshape(MLA_KV_RANK, -1))

    n0 = _rms_op("mla_norm", h, norm_g)
    u, = n0.fwd()
    cq, ckv, kr, gate = _mm_pieces("mla_in", u, (w_cq, w_ckv, w_kr, w_g))
    nq = _rms_op("mla_qnorm", cq, w["mla_g_q"], gdtype=BF16)
    nkv = _rms_op("mla_kvnorm", ckv, w["mla_g_kv"], gdtype=BF16)
    qn_, = nq.fwd()
    kvn_, = nkv.fwd()
    qn, q1, q2 = _mm_pieces("mla_uq", qn_, (w_qn, w_q1, w_q2))
    kn, v = _mm_pieces("mla_ukv", kvn_, (w_kn, w_v))
    prep = _mla_prep_op(qn, q1, q2, kn, kr, v, cos, sin)
    qh, kh, vh = prep.fwd()
    attn = _mla_attn_op(qh, kh, vh)
    o, = attn.fwd(fwd_comm)
    w_out = bf(w["mla_w_out"]) if late_w_out is None else late_w_out(attn.fwd_comm_out)
    post = _mla_post_op(o, gate)
    y, = post.fwd()
    h_out = _mm("mla_out", y, w_out, add=h)

    def bwd(dh, make_comm=None):
        dy = _mm("mla_out_dy", dh, w_out, tb=True, out_dtype=BF16)
        d_w_out = _mm("mla_out_dw", y, dh, ta=True, out_dtype=BF16)
        do, dgate = post.bwd([dy])
        dqh, dkh, dvh = attn.bwd([do], comm=None if make_comm is None else make_comm(d_w_out))
        dqn, dq1, dq2, dkn, dkr, dv = prep.bwd([dqh, dkh, dvh])
        dqn_, d_uq = _proj_bwd("mla_uq", qn_, (dqn, dq1, dq2), (w_qn, w_q1, w_q2))
        dkvn_, d_ukv = _proj_bwd("mla_ukv", kvn_, (dkn, dv), (w_kn, w_v))
        dcq, d_g_q = nq.bwd([dqn_])
        dckv, d_g_kv = nkv.bwd([dkvn_])
        du, d_in = _proj_bwd("mla_in", u, (dcq, dckv, dkr, dgate), (w_cq, w_ckv, w_kr, w_g))
        dh_in, d_norm = n0.bwd([du], addto={0: dh})
        shp = (MLA_Q_RANK, MLA_HEADS, -1)
        g_uq = jnp.concatenate([d_uq[0].reshape(shp), d_uq[1].reshape(shp), d_uq[2].reshape(shp)], axis=2).reshape(MLA_Q_RANK, -1)
        shp = (MLA_KV_RANK, MLA_HEADS, -1)
        g_ukv = jnp.concatenate([d_ukv[0].reshape(shp), d_ukv[1].reshape(shp)], axis=2).reshape(MLA_KV_RANK, -1)
        g_in = jnp.concatenate([d_in[0], d_in[1], d_in[2][:, :MLA_ROPE], d_in[3]], axis=1)
        return dh_in, d_norm, {"mla_w_in": g_in, "mla_g_q": d_g_q.reshape(-1), "mla_w_uq": g_uq, "mla_g_kv": d_g_kv.reshape(-1),
                               "mla_w_ukv": g_ukv, "mla_w_out": d_w_out}, attn.bwd_comm_out

    return h_out, bwd, attn.fwd_comm_out


def _gla_layer(h, norm_g, w, fwd_comm=None):
    bf = lambda a: a.astype(BF16)
    w_in = w["gla_w_in"]
    nk, nv = GLA_HEADS * GLA_DK, GLA_HEADS * GLA_DV
    cuts = (0, nk, 2 * nk, 2 * nk + nv, 2 * nk + 2 * nv)
    w_q, w_k, w_v, w_g = (bf(w_in[:, cuts[i]:cuts[i + 1]]) for i in range(4))
    w_gk = bf(_pad_cols(w_in[:, cuts[4]:], LANE))
    w2 = _pad_rows(w["gla_w_gk2"], LANE)
    b_gk = w["gla_b_gk"].reshape(1, -1)
    g_o = w["gla_g_o"].reshape(1, -1)
    w_out = bf(w["gla_w_out"])

    n0 = _rms_op("gla_norm", h, norm_g)
    u, = n0.fwd()
    q, k, v, gate, gk = _mm_pieces("gla_in", u, (w_q, w_k, w_v, w_g, w_gk))
    gop = _gla_gate_op(gk, w2, b_gk)
    la, = gop.fwd()
    core = _gla_core_op(q, k, v, gate, la, g_o)
    y, = core.fwd(fwd_comm)
    h_out = _mm("gla_out", y, w_out, add=h)

    def bwd(dh, make_comm=None):
        dy = _mm("gla_out_dy", dh, w_out, tb=True, out_dtype=BF16)
        d_w_out = _mm("gla_out_dw", y, dh, ta=True, out_dtype=BF16)
        dq, dk, dv, dgate, dla, d_g_o = core.bwd([dy], comm=None if make_comm is None else make_comm(d_w_out))
        dgk, d_w2, d_b = gop.bwd([dla])
        du, d_in = _proj_bwd("gla_in", u, (dq, dk, dv, dgate, dgk), (w_q, w_k, w_v, w_g, w_gk))
        dh_in, d_norm = n0.bwd([du], addto={0: dh})
        g_in = jnp.concatenate([d_in[0], d_in[1], d_in[2], d_in[3], d_in[4][:, :GLA_RANK]], axis=1)
        return dh_in, d_norm, {"gla_w_in": g_in, "gla_w_gk2": d_w2[:GLA_RANK], "gla_b_gk": d_b.reshape(-1), "gla_g_o": d_g_o.reshape(-1),
                               "gla_w_out": d_w_out}, core.bwd_comm_out

    return h_out, bwd, core.fwd_comm_out


def _lru_layer(h, norm_g, w, fwd_comm=None):
    bf = lambda a: a.astype(BF16)
    w_in = w["lru_w_in"]
    w_g, w_u = bf(w_in[:, :LRU_WIDTH]), bf(w_in[:, LRU_WIDTH:])
    row = lambda a: a.reshape(1, -1)
    w_out = bf(w["lru_w_out"])

    n0 = _rms_op("lru_norm", h, norm_g)
    u_, = n0.fwd()
    gate, u = _mm_pieces("lru_in", u_, (w_g, w_u))
    core = _lru_op(gate, u, w["lru_conv_w"].reshape(CONV_W, 1, -1), row(w["lru_conv_b"]), w["lru_w_a"], row(w["lru_b_a"]), w["lru_w_x"],
                   row(w["lru_b_x"]), row(w["lru_lam"]))
    y, = core.fwd(fwd_comm)
    h_out = _mm("lru_out", y, w_out, add=h)

    def bwd(dh, make_comm=None):
        dy = _mm("lru_out_dy", dh, w_out, tb=True, out_dtype=BF16)
        d_w_out = _mm("lru_out_dw", y, dh, ta=True, out_dtype=BF16)
        dgate, du, d_cw, d_cb, d_wa, d_ba, d_wx, d_bx, d_lam = core.bwd([dy], comm=None if make_comm is None else make_comm(d_w_out))
        du_, d_in = _proj_bwd("lru_in", u_, (dgate, du), (w_g, w_u))
        dh_in, d_norm = n0.bwd([du_], addto={0: dh})
        return dh_in, d_norm, {"lru_w_in": jnp.concatenate(d_in, axis=1), "lru_conv_w": d_cw.reshape(CONV_W, -1), "lru_conv_b": d_cb.reshape(-1),
                               "lru_w_a": d_wa, "lru_b_a": d_ba.reshape(-1), "lru_w_x": d_wx, "lru_b_x": d_bx.reshape(-1),
                               "lru_lam": d_lam.reshape(-1), "lru_w_out": d_w_out}, core.bwd_comm_out

    return h_out, bwd, core.fwd_comm_out


def _ssd_layer(h, norm_g, w, fwd_comm=None, late_w_out=None):
    bf = lambda a: a.astype(BF16)
    s = h.shape[0]
    w_in = w["ssd_w_in"]
    conv_dim = SSD_INNER + 2 * SSD_GROUPS * SSD_STATE
    w_z, w_xbc = bf(w_in[:, :SSD_INNER]), bf(w_in[:, SSD_INNER:SSD_INNER + conv_dim])
    w_dt = bf(_pad_cols(w_in[:, SSD_INNER + conv_dim:], LANE))
    grp = lambda a: a.reshape(SSD_GROUPS, 1, SSD_HPG)

    n0 = _rms_op("ssd_norm", h, norm_g)
    u, = n0.fwd()
    z, xbc, dtp = _mm_pieces("ssd_in", u, (w_z, w_xbc, w_dt))
    conv = _ssd_conv_op(xbc, w["ssd_conv_w"].reshape(CONV_W, 1, -1), w["ssd_conv_b"].reshape(1, -1))
    x, bm, cm = conv.fwd()
    dt = dtp[:, :SSD_HEADS].reshape(s, SSD_GROUPS, SSD_HPG).transpose(1, 0, 2)
    core = _ssd_core_op(x, bm, cm, z, dt, grp(w["ssd_dt_bias"]), grp(w["ssd_a_log"]), grp(w["ssd_d"]), w["ssd_g_norm"].reshape(1, -1))
    y, = core.fwd(fwd_comm)
    w_out = bf(w["ssd_w_out"]) if late_w_out is None else late_w_out(core.fwd_comm_out)
    h_out = _mm("ssd_out", y, w_out, add=h)

    def bwd(dh, make_comm=None):
        dy = _mm("ssd_out_dy", dh, w_out, tb=True, out_dtype=BF16)
        d_w_out = _mm("ssd_out_dw", y, dh, ta=True, out_dtype=BF16)
        dx, dbm, dcm, dz, ddt, d_dtb, d_alog, d_d, d_gn = core.bwd([dy], comm=None if make_comm is None else make_comm(d_w_out))
        dxbc, d_cw, d_cb = conv.bwd([dx, dbm, dcm])
        ddtp = _pad_cols(ddt.transpose(1, 0, 2).reshape(s, SSD_HEADS), LANE).astype(BF16)
        du, d_in = _proj_bwd("ssd_in", u, (dz, dxbc, ddtp), (w_z, w_xbc, w_dt))
        dh_in, d_norm = n0.bwd([du], addto={0: dh})
        g_in = jnp.concatenate([d_in[0], d_in[1], d_in[2][:, :SSD_HEADS]], axis=1)
        return dh_in, d_norm, {"ssd_w_in": g_in, "ssd_conv_w": d_cw.reshape(CONV_W, -1), "ssd_conv_b": d_cb.reshape(-1),
                               "ssd_dt_bias": d_dtb.reshape(-1), "ssd_a_log": d_alog.reshape(-1), "ssd_d": d_d.reshape(-1),
                               "ssd_g_norm": d_gn.reshape(-1), "ssd_w_out": d_w_out}, core.bwd_comm_out

    return h_out, bwd


def _rope_tables(positions):
    inv_freq = ROPE_THETA ** (-jnp.arange(0, MLA_ROPE, 2, dtype=F32) / MLA_ROPE)
    ang = positions.astype(F32)[:, None] * inv_freq
    return jnp.cos(ang), jnp.sin(ang)


WEIGHTS = ["norm_g", "final_g", "mla_w_in", "mla_g_q", "mla_w_uq", "mla_g_kv", "mla_w_ukv", "mla_w_out", "gla_w_in", "gla_w_gk2", "gla_b_gk",
           "gla_g_o", "gla_w_out", "lru_w_in", "lru_conv_w", "lru_conv_b", "lru_w_a", "lru_b_a", "lru_w_x", "lru_b_x", "lru_lam", "lru_w_out",
           "ssd_w_in", "ssd_conv_w", "ssd_conv_b", "ssd_dt_bias", "ssd_a_log", "ssd_d", "ssd_g_norm", "ssd_w_out"]
BIG = ["mla_w_in", "mla_w_uq", "mla_w_ukv", "mla_w_out", "gla_w_in", "gla_w_out", "lru_w_in", "lru_w_out", "ssd_w_in", "ssd_w_out"]
SMALL = ["gla_w_gk2", "gla_b_gk", "gla_g_o", "lru_conv_w", "lru_conv_b", "lru_b_a", "lru_b_x", "lru_lam", "ssd_conv_w", "ssd_conv_b", "ssd_g_norm"]
REPL = ["norm_g", "final_g", "mla_g_q", "mla_g_kv", "lru_w_a", "lru_w_x", "ssd_dt_bias", "ssd_a_log", "ssd_d"]
REPL_EARLY = ["lru_w_a", "lru_w_x"]
REPL_LATE = [n for n in REPL if n not in REPL_EARLY]
N_CHIPS, N_DEV = 4, 8
PACK_W = 1024
ADAM_ROWS = 256
SMALL_ROWS = 64


def _shard_axis(name):
    return 0 if name.endswith("_w_out") else -1


def _pack(arrs, dtype, row_mult):
    flat = jnp.concatenate([a.reshape(-1).astype(dtype) for a in arrs])
    per = PACK_W * row_mult
    total = -(-flat.shape[0] // per) * per
    return jnp.pad(flat, (0, total - flat.shape[0])).reshape(-1, PACK_W)


def _unpack(buf, shapes):
    flat = buf.reshape(-1)
    out, off = [], 0
    for s in shapes:
        n = math.prod(s)
        out.append(flat[off:off + n].reshape(s))
        off += n
    return out


def _mesh_pos():
    return lax.axis_index("x"), lax.axis_index("y"), lax.axis_index("c")


class GatherComm:
    def __init__(self, ops):
        self.ops = list(ops)
        n = len(self.ops)
        assert all(o.ndim == 2 and o.shape[0] % 32 == 0 for o in self.ops), [o.shape for o in self.ops]
        self.out_shapes = [jax.ShapeDtypeStruct((N_CHIPS,) + o.shape, o.dtype) for o in self.ops]
        self.sem_shapes = [pltpu.SemaphoreType.DMA((6 * n,)), pltpu.SemaphoreType.DMA((6 * n,)), pltpu.SemaphoreType.DMA((n,))]

    def _copies(self, srcs, dsts, sems):
        send_sems, recv_sems, local_sems = sems
        n = len(self.ops)
        x, y, c = _mesh_pos()
        me_id, sibling = (x, y, c), (x, y, 1 - c)
        chips = [(1 - x, y), (x, 1 - y), (1 - x, 1 - y)]
        mine = 2 * x + y

        def half(i, cc):
            h = self.ops[i].shape[0] // 2
            return pl.ds(cc * h, h)

        def copy(i, k, src, slot, cc, to):
            return pltpu.make_async_remote_copy(src_ref=src, dst_ref=dsts[i].at[slot, half(i, cc)], send_sem=send_sems.at[i * 6 + k],
                                                recv_sem=recv_sems.at[i * 6 + k], device_id=to, device_id_type=pl.DeviceIdType.MESH)

        local = [pltpu.make_async_copy(srcs[i], dsts[i].at[mine], local_sems.at[i]) for i in range(n)]
        first, ici_recvs, passed, sib_recvs = [], [], [], []
        for i in range(n):
            my_half = srcs[i].at[half(i, c)]
            for k, (px, py) in enumerate(chips):
                slot = 2 * px + py
                first.append(copy(i, k, my_half, mine, c, (px, py, c)))
                ici_recvs.append(copy(i, k, my_half, slot, c, me_id))
                passed.append(copy(i, 3 + k, dsts[i].at[slot, half(i, c)], slot, c, sibling))
                sib_recvs.append(copy(i, 3 + k, my_half, slot, 1 - c, me_id))
        return local, first, ici_recvs, passed, sib_recvs

    def start(self, srcs, dsts, sems):
        local, first, _, _, _ = self._copies(srcs, dsts, sems)
        for cp in local + first:
            cp.start()

    def forward(self, srcs, dsts, sems):
        _, _, ici_recvs, passed, _ = self._copies(srcs, dsts, sems)
        for rc, fw in zip(ici_recvs, passed):
            rc.wait_recv()
            fw.start()

    def finish(self, srcs, dsts, sems):
        local, first, _, passed, sib_recvs = self._copies(srcs, dsts, sems)
        for cp in sib_recvs:
            cp.wait_recv()
        for cp in first + passed:
            cp.wait_send()
        for cp in local:
            cp.wait()


class ExchangeComm:
    def __init__(self, chip_ops, all_ops=()):
        self.ops = list(chip_ops) + list(all_ops)
        self.per_chip = (True,) * len(chip_ops) + (False,) * len(all_ops)
        n = len(self.ops)
        self.out_shapes = [jax.ShapeDtypeStruct((N_DEV,) + o.shape[-2:], o.dtype) for o in self.ops]
        self.sem_shapes = [pltpu.SemaphoreType.DMA((7 * n,)), pltpu.SemaphoreType.DMA((7 * n,)), pltpu.SemaphoreType.DMA((n,))]

    def _copies(self, srcs, dsts, sems):
        send_sems, recv_sems, local_sems = sems
        n, per_chip = len(self.ops), self.per_chip
        x, y, c = _mesh_pos()
        me_id, sibling = (x, y, c), (x, y, 1 - c)
        chips = [(1 - x, y), (x, 1 - y), (1 - x, 1 - y)]

        def dev(px, py, pc):
            return 4 * px + 2 * py + pc

        def part(i, px, py):
            return srcs[i].at[2 * px + py] if per_chip[i] else srcs[i]

        def copy(i, k, src, slot, to):
            return pltpu.make_async_remote_copy(src_ref=src, dst_ref=dsts[i].at[slot], send_sem=send_sems.at[i * 7 + k],
                                                recv_sem=recv_sems.at[i * 7 + k], device_id=to, device_id_type=pl.DeviceIdType.MESH)

        me = dev(x, y, c)
        local = [pltpu.make_async_copy(part(i, x, y), dsts[i].at[me], local_sems.at[i]) for i in range(n)]
        first, ici_recvs, passed, sib_recvs = [], [], [], []
        for i in range(n):
            first.append(copy(i, 0, part(i, x, y), me, sibling))
            first += [copy(i, 1 + k, part(i, px, py), me, (px, py, c)) for k, (px, py) in enumerate(chips)]
            sib_recvs.append(copy(i, 0, part(i, x, y), dev(x, y, 1 - c), me_id))
            for k, (px, py) in enumerate(chips):
                slot = dev(px, py, c)
                ici_recvs.append(copy(i, 1 + k, part(i, x, y), slot, me_id))
                passed.append(copy(i, 4 + k, dsts[i].at[slot], slot, sibling))
                sib_recvs.append(copy(i, 4 + k, part(i, x, y), dev(px, py, 1 - c), me_id))
        return local, first, ici_recvs, passed, sib_recvs

    def start(self, srcs, dsts, sems):
        local, first, _, _, _ = self._copies(srcs, dsts, sems)
        for cp in local + first:
            cp.start()

    def forward(self, srcs, dsts, sems):
        _, _, ici_recvs, passed, _ = self._copies(srcs, dsts, sems)
        for rc, fw in zip(ici_recvs, passed):
            rc.wait_recv()
            fw.start()

    def finish(self, srcs, dsts, sems):
        local, first, _, passed, sib_recvs = self._copies(srcs, dsts, sems)
        for cp in sib_recvs:
            cp.wait_recv()
        for cp in first + passed:
            cp.wait_send()
        for cp in local:
            cp.wait()


def _run_comm(name, comm):
    n = len(comm.ops)

    def body(*refs):
        srcs, dsts, sems = refs[:n], refs[n:2 * n], refs[2 * n:]
        comm.start(srcs, dsts, sems)
        comm.forward(srcs, dsts, sems)
        comm.finish(srcs, dsts, sems)

    any_spec = pl.BlockSpec(memory_space=pl.ANY)
    return pl.pallas_call(body, name=name, in_specs=[any_spec] * n, out_specs=[any_spec] * n, out_shape=comm.out_shapes,
                          scratch_shapes=comm.sem_shapes)(*comm.ops)


def _late_copies(n, per_chip, srcs, lands, send_sems, recv_sems):
    x, y, c = _mesh_pos()
    me = 4 * x + 2 * y + c
    sends, recvs = [], []
    for i in range(n):
        for k in range(1, N_DEV):
            flip = lambda v, bit: v + bit - 2 * v * bit
            px, py, pc = flip(x, (k >> 2) & 1), flip(y, (k >> 1) & 1), flip(c, k & 1)
            sem = i * (N_DEV - 1) + k - 1

            def copy(src, slot, to, sem=sem, i=i):
                return pltpu.make_async_remote_copy(src_ref=src, dst_ref=lands[i].at[slot], send_sem=send_sems.at[sem],
                                                    recv_sem=recv_sems.at[sem], device_id=to, device_id_type=pl.DeviceIdType.MESH)

            mine = srcs[i].at[2 * px + py] if per_chip[i] else srcs[i]
            sends.append(copy(mine, me, (px, py, pc)))
            recvs.append(copy(mine, 4 * px + 2 * py + pc, (x, y, c)))
    return sends, recvs


def _late_start(chip_ops, all_ops):
    ops = list(chip_ops) + list(all_ops)
    per_chip = [True] * len(chip_ops) + [False] * len(all_ops)
    n = len(ops)
    hbm = lambda a: pltpu.with_memory_space_constraint(a, pltpu.HBM)
    lands = [lax.empty((N_DEV,) + o.shape[-2:], o.dtype) for o in ops]

    def body(*refs):
        srcs, land = refs[:n], refs[n:2 * n]
        send_sems, recv_sems, token = refs[2 * n], refs[2 * n + 1], refs[-1]
        sends, _ = _late_copies(n, per_chip, srcs, land, send_sems, recv_sems)
        for cp in sends:
            cp.start()
        token[...] = jnp.zeros(token.shape, F32)

    hbm_spec, sem_spec = pl.BlockSpec(memory_space=pltpu.HBM), pl.BlockSpec(memory_space=pltpu.SEMAPHORE)
    n_sem = n * (N_DEV - 1)
    res = pl.pallas_call(
        body, name="late_exchange_start",
        out_shape=(pltpu.SemaphoreType.DMA((n_sem,)), pltpu.SemaphoreType.DMA((n_sem,)), *[pltpu.HBM(a.shape, a.dtype) for a in ops + lands],
                   jax.ShapeDtypeStruct((8, LANE), F32)),
        in_specs=[hbm_spec] * (2 * n), out_specs=(sem_spec, sem_spec, *[hbm_spec] * (2 * n), pl.BlockSpec(memory_space=pltpu.VMEM)),
        input_output_aliases={i: 2 + i for i in range(2 * n)},
        compiler_params=pltpu.CompilerParams(has_side_effects=pltpu.SideEffectType.DATAFLOW_SIDE_EFFECTING),
    )(*[hbm(a) for a in ops], *[hbm(a) for a in lands])
    return per_chip, res[0], res[1], list(res[2:2 + n]), list(res[2 + n:2 + 2 * n])


def _late_wait(handles, after):
    per_chip, send_sems, recv_sems, srcs, lands = handles
    n = len(srcs)

    def body(*refs):
        src_refs, land = refs[:n], refs[n:2 * n]
        sends, recvs = _late_copies(n, per_chip, src_refs, land, refs[2 * n], refs[2 * n + 1])
        for cp in sends:
            cp.wait_send()
        for cp in recvs:
            cp.wait_recv()

    hbm_spec, sem_spec = pl.BlockSpec(memory_space=pltpu.HBM), pl.BlockSpec(memory_space=pltpu.SEMAPHORE)
    res = pl.pallas_call(
        body, name="late_exchange_wait", out_shape=tuple(pltpu.HBM(a.shape, a.dtype) for a in srcs + lands),
        in_specs=[hbm_spec] * (2 * n) + [sem_spec, sem_spec, pl.BlockSpec(memory_space=pl.ANY)], out_specs=tuple([hbm_spec] * (2 * n)),
        input_output_aliases={i: i for i in range(2 * n)},
        compiler_params=pltpu.CompilerParams(has_side_effects=pltpu.SideEffectType.DATAFLOW_SIDE_EFFECTING),
    )(*srcs, *lands, send_sems, recv_sems, after)
    x, y, c = _mesh_pos()
    me = 4 * x + 2 * y + c
    out = []
    for i in range(n):
        own = res[i][2 * x + y] if per_chip[i] else res[i]
        out.append(lax.dynamic_update_slice(res[n + i], own[None], (me, 0, 0)))
    return out


def _adamw(name, parts, w, m, v, lead=False, comm=None):
    plist = list(parts) if isinstance(parts, (list, tuple)) else [parts]
    n_p = len(plist)
    rows, cols = w.shape[-2:]
    t = next(c for c in (ADAM_ROWS, ADAM_ROWS // 2, SMALL_ROWS) if all(p.shape[1] % c == 0 for p in plist))
    starts = [sum(p.shape[1] for p in plist[:k]) // t for k in range(n_p)]
    counts = [p.shape[1] // t for p in plist]
    assert sum(p.shape[1] for p in plist) == rows, (name, rows)
    c1 = 1.0 - ADAM_B1 ** ADAM_STEP
    c2 = 1.0 - ADAM_B2 ** ADAM_STEP

    n_c = len(comm.ops) if comm is not None else 0
    n_steps = rows // t

    def body(*refs):
        p_refs = refs[:n_p]
        w_ref, m_ref, v_ref = refs[n_p:n_p + 3]
        c_src = refs[n_p + 3:n_p + 3 + n_c]
        g_ref, d_ref, nm_ref, nv_ref = refs[n_p + 3 + n_c:n_p + 7 + n_c]
        cargs = (c_src, refs[n_p + 7 + n_c:n_p + 7 + 2 * n_c], refs[n_p + 7 + 2 * n_c:])
        if comm is not None:
            @pl.when(pl.program_id(0) == 0)
            def _():
                comm.start(*cargs)

            @pl.when(pl.program_id(0) == _forward_step(n_steps))
            def _():
                comm.forward(*cargs)
        g = None
        for k, p_ref in enumerate(p_refs):
            gk = p_ref[0].astype(F32)
            for d in range(1, N_DEV):
                gk = gk + p_ref[d].astype(F32)
            g = gk if g is None else jnp.where(pl.program_id(0) >= starts[k], gk, g)
        nm = ADAM_B1 * m_ref[...] + (1.0 - ADAM_B1) * g
        nv = ADAM_B2 * v_ref[...] + (1.0 - ADAM_B2) * (g * g)
        g_ref[...] = g
        nm_ref[...] = nm
        nv_ref[...] = nv
        d_ref[...] = -ADAM_LR * ((nm / c1) / (jnp.sqrt(nv / c2) + ADAM_EPS) + ADAM_WD * w_ref[...])
        if comm is not None:
            @pl.when(pl.program_id(0) == n_steps - 1)
            def _():
                comm.finish(*cargs)

    row = pl.BlockSpec((None, t, cols), lambda i: (0, i, 0)) if lead else pl.BlockSpec((t, cols), lambda i: (i, 0))
    c_in_specs, c_args, c_out_specs, c_out_shapes, c_sems = _carry_specs(comm)
    res = pl.pallas_call(
        body, name=name, grid=(n_steps,),
        in_specs=[pl.BlockSpec((N_DEV, t, cols), lambda i, lo=lo, n=n: (0, jnp.clip(i - lo, 0, n - 1), 0)) for lo, n in zip(starts, counts)]
        + [row, row, row] + c_in_specs,
        out_specs=[row] * 4 + c_out_specs, out_shape=[jax.ShapeDtypeStruct(w.shape, F32)] * 4 + c_out_shapes, scratch_shapes=c_sems,
        compiler_params=_CP(dimension_semantics=("arbitrary" if comm is not None else "parallel",)),
    )(*plist, w, m, v, *c_args)
    return list(res[:4]), list(res[4:])


def _train_step(x, positions, target, wts, ms, vs, raw):
    small_shapes = [wts[n].shape for n in SMALL]

    big_of = {tag: [n for n in BIG if n.startswith(tag)] for tag in ("mla", "gla", "lru", "ssd")}
    full = {n: wts[n] for n in REPL}

    def gather_comm(names, extra=()):
        return GatherComm([wts[n].astype(BF16) for n in names] + list(extra))

    def assemble(names, got):
        for k, n in enumerate(names):
            full[n] = jnp.concatenate([got[k][j] for j in range(N_CHIPS)], axis=_shard_axis(n))

    first = [n for n in big_of["mla"] if n != "mla_w_out"]
    got = _run_comm("gather_first", gather_comm(first, [_pack([wts[n] for n in SMALL], F32, SMALL_ROWS)]))
    assemble(first, got)
    per_chip_small = [_unpack(got[-1][j], small_shapes) for j in range(N_CHIPS)]
    for k, n in enumerate(SMALL):
        full[n] = jnp.concatenate([per_chip_small[j][k] for j in range(N_CHIPS)], axis=_shard_axis(n))

    cos, sin = _rope_tables(positions)
    ng = full["norm_g"]
    behind_attn = ["mla_w_out"] + big_of["gla"] + big_of["lru"]

    def mla_w_out(got):
        assemble(behind_attn, got)
        return full["mla_w_out"].astype(BF16)

    h1, b0, _ = _mla_layer(x, ng[0], full, cos, sin, fwd_comm=gather_comm(behind_attn), late_w_out=mla_w_out)

    def joined(got_k, axis):
        return jnp.concatenate([got_k[j] for j in range(N_CHIPS)], axis=axis)

    ssd_in = wts["ssd_w_in"].astype(BF16)
    half = ssd_in.shape[0] // 2
    h2, b1, got = _gla_layer(h1, ng[1], full, fwd_comm=GatherComm([ssd_in[:half]]))
    top = joined(got[0], -1)
    h3, b2, got = _lru_layer(h2, ng[2], full, fwd_comm=GatherComm([ssd_in[half:]]))
    full["ssd_w_in"] = jnp.concatenate([top, joined(got[0], -1)], axis=0)
    h4, b3 = _ssd_layer(h3, ng[3], full, fwd_comm=gather_comm(["ssd_w_out"]), late_w_out=lambda got: joined(got[0], 0))
    loss, dh, d_final = _loss_op(h4, target, full["final_g"])
    loss = loss[0, 0]
    grads = {"final_g": d_final.reshape(-1)}
    d_norms = [None] * 4

    def shards_of(n, g):
        return jnp.stack(jnp.split(g.astype(BF16), N_CHIPS, axis=_shard_axis(n)))

    def shards(n):
        return shards_of(n, grads[n])

    parts = {}
    dh, d_norms[3], gw, got = b3(dh, make_comm=lambda dw: ExchangeComm([shards_of("ssd_w_out", dw)]))
    parts["ssd_w_out"] = got[0]
    grads.update(gw)
    ssd_in_g = shards("ssd_w_in")
    half = ssd_in_g.shape[1] // 2
    dh, d_norms[2], gw, got = b2(dh, make_comm=lambda dw: ExchangeComm([ssd_in_g[:, :half], shards_of("lru_w_out", dw)]))
    parts["ssd_w_in"], parts["lru_w_out"] = [got[0]], got[1]
    grads.update(gw)
    dh, d_norms[1], gw, got = b1(dh, make_comm=lambda dw: ExchangeComm([ssd_in_g[:, half:]]))
    parts["ssd_w_in"].append(got[0])
    grads.update(gw)
    repl_early = _pack([grads[n] for n in REPL_EARLY], BF16, SMALL_ROWS)

    behind_attn = ["gla_w_out", "lru_w_in", "gla_w_in"]
    dx, d_norms[0], gw, got = b0(dh, make_comm=lambda dw: ExchangeComm([shards_of("mla_w_out", dw)] + [shards(n) for n in behind_attn],
                                                                        [repl_early]))
    parts.update(zip(["mla_w_out"] + behind_attn, got))
    repl_early_parts = got[-1]
    grads.update(gw)
    grads["norm_g"] = jnp.concatenate(d_norms, axis=0)
    psmall = jnp.stack([_pack([jnp.split(grads[n], N_CHIPS, axis=_shard_axis(n))[j] for n in SMALL], F32, SMALL_ROWS) for j in range(N_CHIPS)])
    prepl = _pack([grads[n] for n in REPL_LATE], F32, SMALL_ROWS)

    out = {}
    kinds = ("grad", "delta", "new_m", "new_v")
    late = [n for n in big_of["mla"] if n != "mla_w_out"]
    handles = _late_start([shards(n) for n in late] + [psmall], [prepl])

    def adam_big(n):
        for kind, a in zip(kinds, _adamw("adam_" + n, parts[n], *(r[n] for r in raw), lead=True)[0]):
            out[kind, n] = a

    def adam_packed(tag, names, p):
        shapes = [wts[n].shape for n in names]
        packed = [_pack([d[n] for n in names], F32, SMALL_ROWS) for d in (wts, ms, vs)]
        for kind, buf in zip(kinds, _adamw(tag, p, *packed)[0]):
            for n, a in zip(names, _unpack(buf, shapes)):
                out[kind, n] = a

    early = [n for n in BIG if n not in late]
    for n in early:
        adam_big(n)
    adam_packed("adam_repl_early", REPL_EARLY, repl_early_parts)
    late_parts = _late_wait(handles, after=out["grad", early[-1]])
    parts.update(zip(late, late_parts))
    for n in late:
        adam_big(n)
    adam_packed("adam_small", SMALL, late_parts[-2])
    adam_packed("adam_repl_late", REPL_LATE, late_parts[-1])
    loss = lax.psum(loss, ("x", "y", "c"))
    return loss, dx, out


def kernel(x, positions, norm_g, final_g, mla_w_in, mla_g_q, mla_w_uq, mla_g_kv, mla_w_ukv, mla_w_out, gla_w_in, gla_w_gk2, gla_b_gk, gla_g_o, gla_w_out, lru_w_in, lru_conv_w, lru_conv_b, lru_w_a, lru_b_a, lru_w_x, lru_b_x, lru_lam, lru_w_out, ssd_w_in, ssd_conv_w, ssd_conv_b, ssd_dt_bias, ssd_a_log, ssd_d, ssd_g_norm, ssd_w_out, loss_target, m_norm_g, m_final_g, m_mla_w_in, m_mla_g_q, m_mla_w_uq, m_mla_g_kv, m_mla_w_ukv, m_mla_w_out, m_gla_w_in, m_gla_w_gk2, m_gla_b_gk, m_gla_g_o, m_gla_w_out, m_lru_w_in, m_lru_conv_w, m_lru_conv_b, m_lru_w_a, m_lru_b_a, m_lru_w_x, m_lru_b_x, m_lru_lam, m_lru_w_out, m_ssd_w_in, m_ssd_conv_w, m_ssd_conv_b, m_ssd_dt_bias, m_ssd_a_log, m_ssd_d, m_ssd_g_norm, m_ssd_w_out, v_norm_g, v_final_g, v_mla_w_in, v_mla_g_q, v_mla_w_uq, v_mla_g_kv, v_mla_w_ukv, v_mla_w_out, v_gla_w_in, v_gla_w_gk2, v_gla_b_gk, v_gla_g_o, v_gla_w_out, v_lru_w_in, v_lru_conv_w, v_lru_conv_b, v_lru_w_a, v_lru_b_a, v_lru_w_x, v_lru_b_x, v_lru_lam, v_lru_w_out, v_ssd_w_in, v_ssd_conv_w, v_ssd_conv_b, v_ssd_dt_bias, v_ssd_a_log, v_ssd_d, v_ssd_g_norm, v_ssd_w_out):
    given = dict(locals())
    stacked = [n for n in WEIGHTS if n not in ("norm_g", "final_g")]

    def blocks(prefix):
        return {n: (given[prefix + n][0] if n in stacked else given[prefix + n]) for n in WEIGHTS}

    raw = [{n: given[prefix + n] for n in BIG} for prefix in ("", "m_", "v_")]
    loss, dx, out = _train_step(x[0], positions[0], loss_target[0], blocks(""), blocks("m_"), blocks("v_"), raw)
    res = [loss, dx[None]]
    for kind in ("grad", "delta", "new_m", "new_v"):
        res += [(out[kind, n][None] if n in stacked and n not in BIG else out[kind, n]) for n in WEIGHTS]
    return tuple(res)
```

```python
import functools
import math

import jax
import jax.numpy as jnp
from jax import lax
from jax.experimental import pallas as pl
from jax.experimental.pallas import tpu as pltpu

F32 = jnp.float32
BF16 = jnp.bfloat16

V7X_VMEM_BYTES = 64 * 1024 * 1024
VMEM_LIMIT = V7X_VMEM_BYTES - 8 * 1024 * 1024
LANE = 128

D_MODEL = 1024
NORM_EPS = 1e-6
MLA_HEADS, MLA_Q_RANK, MLA_KV_RANK, MLA_NOPE, MLA_ROPE, MLA_V = 16, 384, 256, 64, 32, 64
MLA_QK = MLA_NOPE + MLA_ROPE
ROPE_THETA = 10000.0
GLA_HEADS, GLA_DK, GLA_DV, GLA_RANK, GLA_TAU, GLA_CHUNK = 4, 128, 256, 16, 16.0, 64
LRU_WIDTH, LRU_BLOCKS, LRU_BLOCK, LRU_C, CONV_W = 1280, 10, 128, 8.0, 4
SSD_INNER, SSD_P, SSD_HEADS, SSD_GROUPS, SSD_HPG, SSD_STATE, SSD_CHUNK = 2048, 64, 32, 8, 4, 128, 64
ADAM_LR, ADAM_B1, ADAM_B2, ADAM_EPS, ADAM_WD, ADAM_STEP = 0.001, 0.9, 0.999, 1e-08, 0.01, 10

_CP = functools.partial(pltpu.CompilerParams, vmem_limit_bytes=VMEM_LIMIT)


def _bdot(a, b):
    return jnp.dot(a.astype(BF16), b.astype(BF16), preferred_element_type=F32)


def _bdot_nt(a, b):
    return lax.dot_general(a.astype(BF16), b.astype(BF16), (((1,), (1,)), ((), ())), preferred_element_type=F32)


def _bdot_tn(a, b):
    return lax.dot_general(a.astype(BF16), b.astype(BF16), (((0,), (0,)), ((), ())), preferred_element_type=F32)


def _tri(n):
    r = lax.broadcasted_iota(jnp.int32, (n, n), 0)
    c = lax.broadcasted_iota(jnp.int32, (n, n), 1)
    return r >= c


def _rms(x, g):
    return x * lax.rsqrt(jnp.mean(x * x, axis=-1, keepdims=True) + NORM_EPS) * g


def _silu(x):
    return x * jax.nn.sigmoid(x)


def _shift_rows(x, prev, j):
    if j == 0:
        return x
    t = x.shape[0]

    def fwd_impl(x, prev):
        row = lax.broadcasted_iota(jnp.int32, x.shape, 0)
        return jnp.where(row >= j, pltpu.roll(x, j, 0), pltpu.roll(prev, j, 0))

    @jax.custom_vjp
    def sh(x, prev):
        return fwd_impl(x, prev)

    def sh_fwd(x, prev):
        return fwd_impl(x, prev), None

    def sh_bwd(_, gy):
        row = lax.broadcasted_iota(jnp.int32, gy.shape, 0)
        back = pltpu.roll(gy, t - j, 0)
        return jnp.where(row < t - j, back, 0.0), jnp.where(row >= t - j, back, 0.0)

    sh.defvjp(sh_fwd, sh_bwd)
    return sh(x, prev)


def _cumsum_rows(x):
    zero = jnp.zeros_like(x)
    sh = 1
    while sh < x.shape[0]:
        x = x + _shift_rows(x, zero, sh)
        sh *= 2
    return x


def _one_minus_exp(x):
    series = -x * (1.0 + x * (0.5 + x * (1.0 / 6.0 + x * (1.0 / 24.0 + x * (1.0 / 120.0)))))
    return jnp.where(x > -0.05, series, 1.0 - jnp.exp(x))


def _tile(n, cap):
    if n <= cap:
        return n
    best = None
    for t in range(LANE, cap + 1, LANE):
        if n % t == 0:
            best = t
    assert best is not None, (n, cap)
    return best


MM_BLOCK_BYTES = 8 * 1024 * 1024
MM_ROWS, MM_KROWS = 256, 512
MM_TILE_BYTES = 2 * 1024 * 1024


def _mm_tiles(m, k, n, ta):
    if ta:
        return m, _tile(n, max(LANE, MM_BLOCK_BYTES // (4 * m) // LANE * LANE)), _tile(k, MM_KROWS)
    tn = _tile(n, max(LANE, MM_BLOCK_BYTES // (2 * k) // LANE * LANE))
    rows = min(4 * MM_ROWS, max(MM_ROWS, MM_TILE_BYTES // (4 * tn) // MM_ROWS * MM_ROWS))
    return _tile(m, rows), tn, k


def _mm(name, a, b, *, ta=False, tb=False, add=None, out_dtype=F32):
    m, k = (a.shape[1], a.shape[0]) if ta else a.shape
    n, kb = (b.shape[0], b.shape[1]) if tb else (b.shape[1], b.shape[0])
    assert k == kb, (name, a.shape, b.shape, ta, tb)
    tm, tn, tk = _mm_tiles(m, k, n, ta)
    nk = k // tk
    dn = (((0 if ta else 1,), (1 if tb else 0,)), ((), ()))
    has_add = add is not None

    def finish(refs, r):
        if has_add:
            r = r + refs[2][...].astype(F32)
        return r.astype(out_dtype)

    def body_one(*refs):
        a_ref, b_ref, o_ref = refs[0], refs[1], refs[-1]
        o_ref[...] = finish(refs, lax.dot_general(a_ref[...].astype(BF16), b_ref[...].astype(BF16), dn, preferred_element_type=F32))

    def body_acc(*refs):
        a_ref, b_ref = refs[0], refs[1]
        o_ref, acc = refs[-2], refs[-1]
        kk = pl.program_id(2)

        @pl.when(kk == 0)
        def _():
            acc[...] = jnp.zeros(acc.shape, F32)

        acc[...] += lax.dot_general(a_ref[...].astype(BF16), b_ref[...].astype(BF16), dn, preferred_element_type=F32)

        @pl.when(kk == nk - 1)
        def _():
            o_ref[...] = finish(refs, acc[...])

    a_spec = pl.BlockSpec((tk, tm), lambda i, j, q: (q, i)) if ta else pl.BlockSpec((tm, tk), lambda i, j, q: (i, q))
    b_spec = pl.BlockSpec((tn, tk), lambda i, j, q: (j, q)) if tb else pl.BlockSpec((tk, tn), lambda i, j, q: (q, j))
    o_spec = pl.BlockSpec((tm, tn), lambda i, j, q: (i, j))
    in_specs, args = [a_spec, b_spec], [a, b]
    if has_add:
        in_specs.append(o_spec)
        args.append(add)
    return pl.pallas_call(
        body_one if nk == 1 else body_acc, name=name, grid=(m // tm, n // tn, nk), in_specs=in_specs, out_specs=o_spec,
        out_shape=jax.ShapeDtypeStruct((m, n), out_dtype), scratch_shapes=[] if nk == 1 else [pltpu.VMEM((tm, tn), F32)],
        compiler_params=_CP(dimension_semantics=("parallel", "parallel", "arbitrary")),
    )(*args)


def _mm_pieces(name, a, bs):
    m, k = a.shape
    n_b = len(bs)
    tm = _tile(m, MM_ROWS)

    def body(*refs):
        av = refs[0][...].astype(BF16)
        for b_ref, o_ref in zip(refs[1:1 + n_b], refs[1 + n_b:]):
            o_ref[...] = jnp.dot(av, b_ref[...].astype(BF16), preferred_element_type=F32)

    return pl.pallas_call(
        body, name=name, grid=(m // tm,),
        in_specs=[pl.BlockSpec((tm, k), lambda i: (i, 0))] + [pl.BlockSpec(b.shape, lambda i: (0, 0)) for b in bs],
        out_specs=[pl.BlockSpec((tm, b.shape[1]), lambda i: (i, 0)) for b in bs],
        out_shape=[jax.ShapeDtypeStruct((m, b.shape[1]), F32) for b in bs],
        compiler_params=_CP(dimension_semantics=("parallel",)),
    )(a, *bs)


def _mm_pieces_t(name, dps, ws):
    m, k = dps[0].shape[0], ws[0].shape[0]
    n_b = len(ws)
    tm = _tile(m, MM_ROWS)
    dn = (((1,), (1,)), ((), ()))

    def body(*refs):
        acc = None
        for d_ref, w_ref in zip(refs[:n_b], refs[n_b:2 * n_b]):
            part = lax.dot_general(d_ref[...].astype(BF16), w_ref[...].astype(BF16), dn, preferred_element_type=F32)
            acc = part if acc is None else acc + part
        refs[-1][...] = acc

    return pl.pallas_call(
        body, name=name, grid=(m // tm,),
        in_specs=[pl.BlockSpec((tm, d.shape[1]), lambda i: (i, 0)) for d in dps] + [pl.BlockSpec(w.shape, lambda i: (0, 0)) for w in ws],
        out_specs=pl.BlockSpec((tm, k), lambda i: (i, 0)), out_shape=jax.ShapeDtypeStruct((m, k), F32),
        compiler_params=_CP(dimension_semantics=("parallel",)),
    )(*dps, *ws)


def _mm_pieces_dw(name, u, dps):
    s, k = u.shape
    n_b = len(dps)
    tk = _tile(s, MM_KROWS)
    n_steps = s // tk
    dn = (((0,), (0,)), ((), ()))

    def body(*refs):
        u_ref, d_refs, o_refs, accs = refs[0], refs[1:1 + n_b], refs[1 + n_b:1 + 2 * n_b], refs[1 + 2 * n_b:]
        step = pl.program_id(0)
        uv = u_ref[...].astype(BF16)
        for d_ref, o_ref, acc in zip(d_refs, o_refs, accs):
            part = lax.dot_general(uv, d_ref[...].astype(BF16), dn, preferred_element_type=F32)

            @pl.when(step == 0)
            def _(acc=acc, part=part):
                acc[...] = part

            @pl.when(step > 0)
            def _(acc=acc, part=part):
                acc[...] += part

            @pl.when(step == n_steps - 1)
            def _(acc=acc, o_ref=o_ref):
                o_ref[...] = acc[...].astype(o_ref.dtype)

    return pl.pallas_call(
        body, name=name, grid=(n_steps,),
        in_specs=[pl.BlockSpec((tk, k), lambda i: (i, 0))] + [pl.BlockSpec((tk, d.shape[1]), lambda i: (i, 0)) for d in dps],
        out_specs=[pl.BlockSpec((k, d.shape[1]), lambda i: (0, 0)) for d in dps],
        out_shape=[jax.ShapeDtypeStruct((k, d.shape[1]), BF16) for d in dps],
        scratch_shapes=[pltpu.VMEM((k, d.shape[1]), F32) for d in dps],
        compiler_params=_CP(dimension_semantics=("arbitrary",)),
    )(u, *dps)


MM_DW_MERGE_COLS = 4096


class In:
    def __init__(self, arr, block, imap, kind="x", per_h=False, gdtype=F32, gshape=None, gimap=None, prefixed=False):
        self.arr, self.block, self.imap, self.kind, self.per_h, self.gdtype = arr, tuple(block), imap, kind, per_h, gdtype
        self.prefixed = prefixed
        self.gshape = tuple(gshape) if gshape is not None else tuple(arr.shape)
        self.gimap = gimap if gimap is not None else imap

    def spec(self, rev_g=None):
        imap = self.imap
        if rev_g is None:
            return pl.BlockSpec(self.block, lambda h, g: imap(h, g))
        return pl.BlockSpec(self.block, lambda h, g: imap(h, rev_g - 1 - g))


class Out:
    def __init__(self, shape, dtype, block, imap):
        self.shape, self.dtype, self.block, self.imap = tuple(shape), dtype, tuple(block), imap

    def spec(self, rev_g=None):
        imap = self.imap
        if rev_g is None:
            return pl.BlockSpec(self.block, lambda h, g: imap(h, g))
        return pl.BlockSpec(self.block, lambda h, g: imap(h, rev_g - 1 - g))


def _load_f32(ref, rows=None):
    v = ref[...] if rows is None else ref[0:rows]
    return v.astype(F32) if jnp.issubdtype(v.dtype, jnp.floating) else v


def _state_out(grid, shape):
    nd = len(shape)
    return Out(tuple(grid) + tuple(shape), F32, (None, None) + tuple(shape), lambda h, g: (h, g) + (0,) * nd)


def _carry(comm, grid, refs, n_in, n_out, n_scr):
    n_c = len(comm.ops) if comm is not None else 0
    n_s = len(comm.sem_shapes) if comm is not None else 0
    p = 0
    in_refs = refs[p:p + n_in]; p += n_in
    c_src = refs[p:p + n_c]; p += n_c
    out_refs = refs[p:p + n_out]; p += n_out
    c_dst = refs[p:p + n_c]; p += n_c
    scr = refs[p:p + n_scr]; p += n_scr
    c_sem = refs[p:p + n_s]
    step = pl.program_id(0) * grid[1] + pl.program_id(1)
    n_steps = grid[0] * grid[1]
    when = (step == 0, step == _forward_step(n_steps), step == n_steps - 1)
    return in_refs, out_refs, scr, (c_src, c_dst, c_sem), when


def _forward_step(n_steps):
    return max(0, min(n_steps - 2, (3 * n_steps) // 4))


def _carry_specs(comm):
    if comm is None:
        return [], [], [], [], []
    any_spec = pl.BlockSpec(memory_space=pl.ANY)
    n = len(comm.ops)
    return [any_spec] * n, list(comm.ops), [any_spec] * n, list(comm.out_shapes), list(comm.sem_shapes)


def _op_fwd(name, f, grid, ins, outs, state_shapes=(), comm=None, prefix_rows=None):
    assert prefix_rows is None or not state_shapes
    n_in, n_out, n_st = len(ins), len(outs), len(state_shapes)
    st_outs = [_state_out(grid, s) for s in state_shapes]

    def body(*refs):
        in_refs, o_refs, st_scr, cargs, (first, fwd_step, last) = _carry(comm, grid, refs, n_in, n_out + n_st, n_st)
        out_refs, sv_refs = o_refs[:n_out], o_refs[n_out:]
        if comm is not None:
            @pl.when(first)
            def _():
                comm.start(*cargs)

            @pl.when(fwd_step)
            def _():
                comm.forward(*cargs)
        g = pl.program_id(1)
        if n_st:
            @pl.when(g == 0)
            def _():
                for s in st_scr:
                    s[...] = jnp.zeros(s.shape, F32)

        def compute(rows, g=g):
            vals = [_load_f32(r, rows if i.prefixed else None) for r, i in zip(in_refs, ins)]
            sts = [s[...] for s in st_scr]
            o, ns = f(g, vals, sts)
            for r, v in zip(out_refs, o):
                r[...] = v.astype(r.dtype)
            for r, s in zip(sv_refs, sts):
                r[...] = s
            for s, v in zip(st_scr, ns):
                s[...] = v

        if prefix_rows is None:
            compute(None)
        else:
            per = grid[1] // len(prefix_rows)
            for lv, rows in enumerate(prefix_rows):
                pl.when(g // per == lv)(functools.partial(compute, rows, lv) if per == 1 else functools.partial(compute, rows))
        if comm is not None:
            @pl.when(last)
            def _():
                comm.finish(*cargs)

    all_outs = list(outs) + st_outs
    c_in_specs, c_args, c_out_specs, c_out_shapes, c_sems = _carry_specs(comm)
    res = pl.pallas_call(
        body, name=name, grid=tuple(grid), in_specs=[i.spec() for i in ins] + c_in_specs,
        out_specs=[o.spec() for o in all_outs] + c_out_specs,
        out_shape=[jax.ShapeDtypeStruct(o.shape, o.dtype) for o in all_outs] + c_out_shapes,
        scratch_shapes=[pltpu.VMEM(tuple(s), F32) for s in state_shapes] + c_sems,
        compiler_params=_CP(dimension_semantics=("arbitrary", "arbitrary")),
    )(*[i.arr for i in ins], *c_args)
    n_all = n_out + n_st
    return list(res[:n_out]), list(res[n_out:n_all]), list(res[n_all:])


def _op_bwd(name, f, grid, ins, outs, state_shapes, saved, douts, addto=None, comm=None, prefix_rows=None):
    n_in, n_out, n_st = len(ins), len(outs), len(state_shapes)
    n_g = grid[1]
    assert prefix_rows is None or all(i.prefixed and i.per_h for i in ins if i.kind == "p")
    addto = addto or {}
    diff = [k for k, i in enumerate(ins) if i.kind in ("x", "p")]
    add_idx = sorted(addto)
    st_ins = [In(s, o.block, o.imap, "c") for s, o in zip(saved, [_state_out(grid, s) for s in state_shapes])]
    dout_ins = [In(d, o.block, o.imap, "c") for d, o in zip(douts, outs)]
    add_ins = []
    for k in add_idx:
        i, a = ins[k], addto[k]
        blk = i.block if i.kind == "x" else i.block[:-2] + a.shape[-2:]
        add_ins.append(In(a, blk, i.gimap if i.kind == "x" else i.imap, "c"))
    g_outs = []
    for k in diff:
        i = ins[k]
        g_outs.append(Out(i.gshape, i.gdtype if i.kind == "x" else F32, i.block, i.gimap))

    def body(*refs):
        all_in, go_refs, ds_scr, cargs, (first_step, fwd_step, last_step) = _carry(comm, grid, refs, n_in + n_st + n_out + len(add_idx),
                                                                                    len(diff), n_st)
        if comm is not None:
            @pl.when(first_step)
            def _():
                comm.start(*cargs)

            @pl.when(fwd_step)
            def _():
                comm.forward(*cargs)
        p = 0
        in_refs = all_in[p:p + n_in]; p += n_in
        sv_refs = all_in[p:p + n_st]; p += n_st
        do_refs = all_in[p:p + n_out]; p += n_out
        ad_refs = all_in[p:p + len(add_idx)]
        hh = pl.program_id(0)
        step = pl.program_id(1)
        g = n_g - 1 - step
        if n_st:
            @pl.when(step == 0)
            def _():
                for s in ds_scr:
                    s[...] = jnp.zeros(s.shape, F32)

        def compute(rows, g=g):
            vals = [_load_f32(r, rows if i.prefixed else None) for r, i in zip(in_refs, ins)]
            sts = [r[...] for r in sv_refs]

            def fw(dvals, states):
                full = list(vals)
                for k, v in zip(diff, dvals):
                    full[k] = v
                o, ns = f(g, full, states)
                return list(o), list(ns)

            _, vjp = jax.vjp(fw, [vals[k] for k in diff], sts)
            cts = [r[...].astype(F32) for r in do_refs]
            dns = [s[...] for s in ds_scr]
            dvals, dsts = vjp((cts, dns))
            adds = dict(zip(add_idx, ad_refs))
            for k, r, dv in zip(diff, go_refs, dvals):
                i = ins[k]
                if i.kind == "x":
                    if k in adds:
                        dv = dv + adds[k][...].astype(F32)
                    r[...] = dv.astype(r.dtype)
                elif rows is not None:
                    r[0:rows] += dv
                else:
                    first = (step == 0) if i.per_h else jnp.logical_and(step == 0, hh == 0)

                    @pl.when(first)
                    def _(r=r, dv=dv, k=k):
                        r[...] = dv
                        if k in adds:
                            lead = adds[k].shape[0]
                            r[0:lead] += adds[k][...]

                    @pl.when(jnp.logical_not(first))
                    def _(r=r, dv=dv):
                        r[...] += dv
            for s, v in zip(ds_scr, dsts):
                s[...] = v

        if prefix_rows is None:
            compute(None)
        else:
            @pl.when(step == 0)
            def _():
                for k, r in zip(diff, go_refs):
                    if ins[k].kind == "p":
                        r[...] = jnp.zeros(r.shape, F32)
            per = n_g // len(prefix_rows)
            for lv, rows in enumerate(prefix_rows):
                pl.when(g // per == lv)(functools.partial(compute, rows, lv) if per == 1 else functools.partial(compute, rows))
        if comm is not None:
            @pl.when(last_step)
            def _():
                comm.finish(*cargs)

    all_ins = list(ins) + st_ins + dout_ins + add_ins
    c_in_specs, c_args, c_out_specs, c_out_shapes, c_sems = _carry_specs(comm)
    res = pl.pallas_call(
        body, name=name, grid=tuple(grid), in_specs=[i.spec(n_g) for i in all_ins] + c_in_specs,
        out_specs=[o.spec(n_g) for o in g_outs] + c_out_specs,
        out_shape=[jax.ShapeDtypeStruct(o.shape, o.dtype) for o in g_outs] + c_out_shapes,
        scratch_shapes=[pltpu.VMEM(tuple(s), F32) for s in state_shapes] + c_sems,
        compiler_params=_CP(dimension_semantics=("arbitrary", "arbitrary")),
    )(*[i.arr for i in all_ins], *c_args)
    return list(res[:len(g_outs)]), list(res[len(g_outs):])


class Op:
    def __init__(self, name, f, grid, ins, outs, state_shapes=(), prefix_rows=None):
        self.name, self.f, self.grid, self.ins, self.outs, self.state_shapes = name, f, grid, ins, outs, state_shapes
        self.prefix_rows = prefix_rows
        self.saved = None

    def fwd(self, comm=None):
        res, self.saved, self.fwd_comm_out = _op_fwd(self.name + "_fwd", self.f, self.grid, self.ins, self.outs, self.state_shapes, comm,
                                                     self.prefix_rows)
        return res

    def bwd(self, douts, addto=None, comm=None):
        res, self.bwd_comm_out = _op_bwd(self.name + "_bwd", self.f, self.grid, self.ins, self.outs, self.state_shapes, self.saved, douts,
                                         addto, comm, self.prefix_rows)
        return res


def _rows(arr, t, kind="x", gdtype=F32):
    return In(arr, (t, arr.shape[1]), lambda h, g: (g, 0), kind, gdtype=gdtype)


def _whole(arr, kind="p"):
    nd = arr.ndim
    return In(arr, arr.shape, lambda h, g: (0,) * nd, kind)


def _rows_out(s, n, t, dtype):
    return Out((s, n), dtype, (t, n), lambda h, g: (g, 0))


ROW_T = 256


def _rms_op(name, x, gain, out_dtype=BF16, gdtype=F32):
    s, n = x.shape

    def f(g, vals, sts):
        return [_rms(vals[0], vals[1])], []

    return Op(name, f, (1, s // ROW_T), [_rows(x, ROW_T, gdtype=gdtype), _whole(gain.reshape(1, n))], [_rows_out(s, n, ROW_T, out_dtype)])


def _mla_prep_op(qn, q1, q2, kn, kr, v, cos, sin):
    s = qn.shape[0]
    hd, half = MLA_HEADS, MLA_ROPE // 2

    def f(g, vals, sts):
        qn, q1, q2, kn, kr, v, cos, sin = vals
        cos_h, sin_h = jnp.tile(cos, (1, hd)), jnp.tile(sin, (1, hd))
        r1 = q1 * cos_h - q2 * sin_h
        r2 = q2 * cos_h + q1 * sin_h
        k1, k2 = kr[:, 0:half], kr[:, half:2 * half]
        kr1 = k1 * cos - k2 * sin
        kr2 = k2 * cos + k1 * sin
        zpad = jnp.zeros((qn.shape[0], LANE - MLA_QK), F32)
        qs, ks, vs = [], [], []
        for h in range(hd):
            a, b = h * MLA_NOPE, (h + 1) * MLA_NOPE
            c, d = h * half, (h + 1) * half
            qs.append(jnp.concatenate([qn[:, a:b], r1[:, c:d], r2[:, c:d], zpad], axis=1))
            ks.append(jnp.concatenate([kn[:, a:b], kr1, kr2, zpad], axis=1))
            vs.append(v[:, a:b])
        return [jnp.stack(qs, 0), jnp.stack(ks, 0), jnp.stack(vs, 0)], []

    ins = [_rows(qn, ROW_T, gdtype=BF16), _rows(q1, ROW_T, gdtype=BF16), _rows(q2, ROW_T, gdtype=BF16), _rows(kn, ROW_T, gdtype=BF16),
           _rows(kr, ROW_T, gdtype=BF16), _rows(v, ROW_T, gdtype=BF16), _rows(cos, ROW_T, "c"), _rows(sin, ROW_T, "c")]
    outs = [Out((hd, s, LANE), BF16, (hd, ROW_T, LANE), lambda h, g: (0, g, 0)),
            Out((hd, s, LANE), BF16, (hd, ROW_T, LANE), lambda h, g: (0, g, 0)),
            Out((hd, s, MLA_V), BF16, (hd, ROW_T, MLA_V), lambda h, g: (0, g, 0))]
    return Op("mla_prep", f, (1, s // ROW_T), ins, outs)


ATT_TQ = 256
ATT_LEVELS = 8


def _mla_attn_op(q, k, v):
    hd, s, _ = q.shape
    scale = MLA_QK ** -0.5

    def f(g, vals, sts):
        q, k, v = vals
        sc = _bdot_nt(q, k) * scale
        r = lax.broadcasted_iota(jnp.int32, sc.shape, 0) + g * ATT_TQ
        c = lax.broadcasted_iota(jnp.int32, sc.shape, 1)
        sc = jnp.where(r >= c, sc, -1e30)
        m = lax.stop_gradient(jnp.max(sc, axis=-1, keepdims=True))
        p = jnp.exp(sc - m)
        p = p * (1.0 / jnp.sum(p, axis=-1, keepdims=True))
        return [_bdot(p, v)], []

    ins = [In(q, (None, ATT_TQ, LANE), lambda h, g: (h, g, 0), "x", gdtype=BF16),
           In(k, (None, s, LANE), lambda h, g: (h, 0, 0), "p", per_h=True, prefixed=True),
           In(v, (None, s, MLA_V), lambda h, g: (h, 0, 0), "p", per_h=True, prefixed=True)]
    outs = [Out((hd, s, MLA_V), F32, (None, ATT_TQ, MLA_V), lambda h, g: (h, g, 0))]
    return Op("mla_attn", f, (hd, s // ATT_TQ), ins, outs, prefix_rows=[(lv + 1) * (s // ATT_LEVELS) for lv in range(ATT_LEVELS)])


def _mla_post_op(o, gate):
    hd, s, _ = o.shape

    def f(g, vals, sts):
        o, gate = vals
        cat = jnp.concatenate([o[h] for h in range(hd)], axis=1)
        return [cat * _silu(gate)], []

    ins = [In(o, (hd, ROW_T, MLA_V), lambda h, g: (0, g, 0), "x"), _rows(gate, ROW_T, gdtype=BF16)]
    return Op("mla_post", f, (1, s // ROW_T), ins, [_rows_out(s, hd * MLA_V, ROW_T, BF16)])


def _gla_gate_op(gk, w2, b):
    s = gk.shape[0]

    def f(g, vals, sts):
        gk, w2, b = vals
        return [jax.nn.log_sigmoid(_bdot(gk, w2) + b) / GLA_TAU], []

    ins = [_rows(gk, ROW_T, gdtype=BF16), _whole(w2), _whole(b)]
    return Op("gla_gate", f, (1, s // ROW_T), ins, [_rows_out(s, GLA_HEADS * GLA_DK, ROW_T, F32)])


def _gla_core_op(q, k, v, gate, la, g_o):
    s = q.shape[0]
    c, nh = GLA_CHUNK, GLA_HEADS

    def f(g, vals, sts):
        q, k, v, gate, la, g_o = vals
        tri = _tri(c)
        b = _cumsum_rows(la)
        b_last = jnp.sum(la, axis=0, keepdims=True)
        qt = q * (GLA_DK ** -0.5) * jnp.exp(b)
        kt = k * jnp.exp(-b)
        kd = k * jnp.exp(b_last - b)
        ys, new_sts = [], []
        for h in range(nh):
            ks, vs = slice(h * GLA_DK, (h + 1) * GLA_DK), slice(h * GLA_DV, (h + 1) * GLA_DV)
            att = jnp.where(tri, _bdot_nt(qt[:, ks], kt[:, ks]), 0.0)
            o = _bdot(att, v[:, vs]) + _bdot_nt(qt[:, ks], sts[h])
            new_sts.append(jnp.exp(b_last[:, ks]) * sts[h] + _bdot_tn(v[:, vs], kd[:, ks]))
            ys.append(_rms(o, g_o) * _silu(gate[:, vs]))
        return [jnp.concatenate(ys, axis=1)], new_sts

    ins = [_rows(q, c, gdtype=BF16), _rows(k, c, gdtype=BF16), _rows(v, c, gdtype=BF16), _rows(gate, c, gdtype=BF16), _rows(la, c), _whole(g_o)]
    outs = [_rows_out(s, nh * GLA_DV, c, BF16)]
    return Op("gla_core", f, (1, s // c), ins, outs, [(GLA_DV, GLA_DK)] * nh)


LRU_T = 256


def _lru_op(gate, u, conv_w, conv_b, w_a, b_a, w_x, b_x, lam):
    s, w = u.shape
    t = LRU_T

    def f(g, vals, sts):
        gate, u, cw, cb, w_a, b_a, w_x, b_x, lam = vals
        u_prev, h_prev = sts
        uc = cb
        for kk in range(CONV_W):
            uc = uc + cw[kk] * _shift_rows(u, u_prev, CONV_W - 1 - kk)
        ra, ri = [], []
        for n in range(LRU_BLOCKS):
            blk = uc[:, n * LRU_BLOCK:(n + 1) * LRU_BLOCK]
            ra.append(_bdot(blk, w_a[n]))
            ri.append(_bdot(blk, w_x[n]))
        r = jax.nn.sigmoid(jnp.concatenate(ra, axis=1) + b_a)
        i = jax.nn.sigmoid(jnp.concatenate(ri, axis=1) + b_x)
        log_a = -LRU_C * r * jax.nn.softplus(-lam)
        a = jnp.exp(log_a)
        bb = jnp.sqrt(_one_minus_exp(2.0 * log_a)) * (i * uc)
        zero = jnp.zeros_like(a)
        sh = 1
        while sh < t:
            a_s = _shift_rows(a - 1.0, zero, sh) + 1.0
            b_s = _shift_rows(bb, zero, sh)
            bb = a * b_s + bb
            a = a * a_s
            sh *= 2
        hs = bb + a * h_prev
        last = (lax.broadcasted_iota(jnp.int32, hs.shape, 0) == t - 1).astype(F32)
        h_last = jnp.sum(hs * last, axis=0, keepdims=True)
        return [hs * _silu(gate)], [u, h_last]

    ins = [_rows(gate, t, gdtype=BF16), _rows(u, t, gdtype=BF16), _whole(conv_w), _whole(conv_b), _whole(w_a), _whole(b_a), _whole(w_x),
           _whole(b_x), _whole(lam)]
    return Op("lru_core", f, (1, s // t), ins, [_rows_out(s, w, t, BF16)], [(t, w), (1, w)])


def _ssd_conv_op(xbc, conv_w, conv_b):
    s, w = xbc.shape
    t = ROW_T
    n_x, n_b = SSD_INNER, SSD_GROUPS * SSD_STATE

    def f(g, vals, sts):
        xbc, cw, cb = vals
        acc = cb
        for kk in range(CONV_W):
            acc = acc + cw[kk] * _shift_rows(xbc, sts[0], CONV_W - 1 - kk)
        y = _silu(acc)
        return [y[:, :n_x], y[:, n_x:n_x + n_b], y[:, n_x + n_b:]], [xbc]

    ins = [_rows(xbc, t, gdtype=BF16), _whole(conv_w), _whole(conv_b)]
    outs = [_rows_out(s, n_x, t, F32), _rows_out(s, n_b, t, BF16), _rows_out(s, n_b, t, BF16)]
    return Op("ssd_conv", f, (1, s // t), ins, outs, [(t, w)])


SSD_L = 512


def _ssd_core_op(x, bm, cm, z, dt, dt_bias, a_log, d_skip, g_norm):
    s = x.shape[0]
    c, hg, p = SSD_L, SSD_HPG, SSD_P
    gw = hg * p

    def f(g, vals, sts):
        x, bm, cm, z, dtr, dt_bias, a_log, d_skip, g_norm = vals
        tri = _tri(c)
        dt = jax.nn.softplus(dtr + dt_bias)
        da = dt * (-jnp.exp(a_log))
        cs = _cumsum_rows(da)
        cs_last = jnp.sum(da, axis=0, keepdims=True)
        cs_t = jnp.transpose(jnp.concatenate([cs, jnp.zeros((c, LANE - hg), F32)], axis=1))
        cb = _bdot_nt(cm, bm)
        ys, new_st = [], []
        for h in range(hg):
            cs_h = cs[:, h:h + 1]
            cs_row = cs_t[h:h + 1, :]
            seg = jnp.where(tri, cs_h - cs_row, 0.0)
            lmat = jnp.where(tri, jnp.exp(seg), 0.0)
            x_h = x[:, h * p:(h + 1) * p]
            xdt = x_h * dt[:, h:h + 1]
            y_diag = _bdot(cb * lmat, xdt)
            decay = jnp.exp(cs_last[:, h:h + 1] - cs_h)
            states = _bdot_tn(xdt * decay, bm)
            y_off = _bdot_nt(cm, sts[h]) * jnp.exp(cs_h)
            new_st.append(jnp.exp(cs_last[:, h:h + 1]) * sts[h] + states)
            ys.append(y_diag + y_off + d_skip[:, h:h + 1] * x_h)
        y = jnp.concatenate(ys, axis=1) * _silu(z)
        return [_rms(y, g_norm)], new_st

    ins = [In(x, (c, gw), lambda h, g: (g, h), "x"), In(bm, (c, SSD_STATE), lambda h, g: (g, h), "x", gdtype=BF16),
           In(cm, (c, SSD_STATE), lambda h, g: (g, h), "x", gdtype=BF16), In(z, (c, gw), lambda h, g: (g, h), "x", gdtype=BF16),
           In(dt, (None, c, hg), lambda h, g: (h, g, 0), "x"),
           In(dt_bias, (None, 1, hg), lambda h, g: (h, 0, 0), "p", per_h=True),
           In(a_log, (None, 1, hg), lambda h, g: (h, 0, 0), "p", per_h=True),
           In(d_skip, (None, 1, hg), lambda h, g: (h, 0, 0), "p", per_h=True),
           In(g_norm, (1, gw), lambda h, g: (0, h), "p", per_h=True)]
    outs = [Out((s, SSD_INNER), BF16, (c, gw), lambda h, g: (g, h))]
    return Op("ssd_core", f, (SSD_GROUPS, s // c), ins, outs, [(p, SSD_STATE)] * hg)


def _loss_op(h, target, final_g):
    s, n = h.shape
    t = ROW_T
    n_g = s // t

    def body(h_ref, t_ref, g_ref, loss_ref, dh_ref, dg_ref):
        step = pl.program_id(0)

        def lossf(hv, gv):
            err = _rms(hv, gv) - t_ref[...]
            return 0.5 * jnp.sum(jnp.mean(err * err, axis=-1))

        l, (dh, dg) = jax.value_and_grad(lossf, argnums=(0, 1))(h_ref[...], g_ref[...])
        dh_ref[...] = dh

        @pl.when(step == 0)
        def _():
            loss_ref[...] = jnp.zeros(loss_ref.shape, F32)
            dg_ref[...] = jnp.zeros(dg_ref.shape, F32)

        loss_ref[...] += jnp.full(loss_ref.shape, l, F32)
        dg_ref[...] += dg

    row = pl.BlockSpec((t, n), lambda g: (g, 0))
    one = pl.BlockSpec((1, n), lambda g: (0, 0))
    return pl.pallas_call(
        body, name="loss_head", grid=(n_g,), in_specs=[row, row, one],
        out_specs=[pl.BlockSpec((1, LANE), lambda g: (0, 0)), row, one],
        out_shape=[jax.ShapeDtypeStruct((1, LANE), F32), jax.ShapeDtypeStruct((s, n), F32), jax.ShapeDtypeStruct((1, n), F32)],
        compiler_params=_CP(dimension_semantics=("arbitrary",)),
    )(h, target, final_g.reshape(1, n))


def _pad_cols(w, n):
    return jnp.pad(w, ((0, 0), (0, n - w.shape[1])))


def _pad_rows(w, n):
    return jnp.pad(w, ((0, n - w.shape[0]), (0, 0)))


def _proj_bwd(tag, u, dps, ws):
    du = _mm_pieces_t(f"{tag}_du", dps, ws)
    if sum(dp.shape[1] for dp in dps) <= MM_DW_MERGE_COLS:
        dws = _mm_pieces_dw(f"{tag}_dw", u, dps)
    else:
        dws = [_mm(f"{tag}_dw{i}", u, dp, ta=True, out_dtype=BF16) for i, dp in enumerate(dps)]
    return du, dws


def _mla_layer(h, norm_g, w, cos, sin, fwd_comm=None, late_w_out=None):
    bf = lambda a: a.astype(BF16)
    w_in, w_uq, w_ukv = w["mla_w_in"], w["mla_w_uq"], w["mla_w_ukv"]
    a0, a1, a2 = MLA_Q_RANK, MLA_Q_RANK + MLA_KV_RANK, MLA_Q_RANK + MLA_KV_RANK + MLA_ROPE
    w_cq, w_ckv, w_kr, w_g = bf(w_in[:, :a0]), bf(w_in[:, a0:a1]), bf(_pad_cols(w_in[:, a1:a2], LANE)), bf(w_in[:, a2:])
    uq = w_uq.reshape(MLA_Q_RANK, MLA_HEADS, MLA_QK)
    half = MLA_ROPE // 2
    w_qn = bf(uq[:, :, :MLA_NOPE].reshape(MLA_Q_RANK, -1))
    w_q1 = bf(uq[:, :, MLA_NOPE:MLA_NOPE + half].reshape(MLA_Q_RANK, -1))
    w_q2 = bf(uq[:, :, MLA_NOPE + half:].reshape(MLA_Q_RANK, -1))
    ukv = w_ukv.reshape(MLA_KV_RANK, MLA_HEADS, MLA_NOPE + MLA_V)
    w_kn = bf(ukv[:, :, :MLA_NOPE].reshape(MLA_KV_RANK, -1))
    w_v = bf(ukv[:, :, MLA_NOPE:].reshape(MLA_KV_RANK, -1))

    n0 = _rms_op("mla_norm", h, norm_g)
    u, = n0.fwd()
    cq, ckv, kr, gate = _mm_pieces("mla_in", u, (w_cq, w_ckv, w_kr, w_g))
    nq = _rms_op("mla_qnorm", cq, w["mla_g_q"], gdtype=BF16)
    nkv = _rms_op("mla_kvnorm", ckv, w["mla_g_kv"], gdtype=BF16)
    qn_, = nq.fwd()
    kvn_, = nkv.fwd()
    qn, q1, q2 = _mm_pieces("mla_uq", qn_, (w_qn, w_q1, w_q2))
    kn, v = _mm_pieces("mla_ukv", kvn_, (w_kn, w_v))
    prep = _mla_prep_op(qn, q1, q2, kn, kr, v, cos, sin)
    qh, kh, vh = prep.fwd()
    attn = _mla_attn_op(qh, kh, vh)
    o, = attn.fwd(fwd_comm)
    w_out = bf(w["mla_w_out"]) if late_w_out is None else late_w_out(attn.fwd_comm_out)
    post = _mla_post_op(o, gate)
    y, = post.fwd()
    h_out = _mm("mla_out", y, w_out, add=h)

    def bwd(dh, make_comm=None):
        dy = _mm("mla_out_dy", dh, w_out, tb=True, out_dtype=BF16)
        d_w_out = _mm("mla_out_dw", y, dh, ta=True, out_dtype=BF16)
        do, dgate = post.bwd([dy])
        dqh, dkh, dvh = attn.bwd([do], comm=None if make_comm is None else make_comm(d_w_out))
        dqn, dq1, dq2, dkn, dkr, dv = prep.bwd([dqh, dkh, dvh])
        dqn_, d_uq = _proj_bwd("mla_uq", qn_, (dqn, dq1, dq2), (w_qn, w_q1, w_q2))
        dkvn_, d_ukv = _proj_bwd("mla_ukv", kvn_, (dkn, dv), (w_kn, w_v))
        dcq, d_g_q = nq.bwd([dqn_])
        dckv, d_g_kv = nkv.bwd([dkvn_])
        du, d_in = _proj_bwd("mla_in", u, (dcq, dckv, dkr, dgate), (w_cq, w_ckv, w_kr, w_g))
        dh_in, d_norm = n0.bwd([du], addto={0: dh})
        shp = (MLA_Q_RANK, MLA_HEADS, -1)
        g_uq = jnp.concatenate([d_uq[0].reshape(shp), d_uq[1].reshape(shp), d_uq[2].reshape(shp)], axis=2).reshape(MLA_Q_RANK, -1)
        shp = (MLA_KV_RANK, MLA_HEADS, -1)
        g_ukv = jnp.concatenate([d_ukv[0].reshape(shp), d_ukv[1].reshape(shp)], axis=2).reshape(MLA_KV_RANK, -1)
        g_in = jnp.concatenate([d_in[0], d_in[1], d_in[2][:, :MLA_ROPE], d_in[3]], axis=1)
        return dh_in, d_norm, {"mla_w_in": g_in, "mla_g_q": d_g_q.reshape(-1), "mla_w_uq": g_uq, "mla_g_kv": d_g_kv.reshape(-1),
                               "mla_w_ukv": g_ukv, "mla_w_out": d_w_out}, attn.bwd_comm_out

    return h_out, bwd, attn.fwd_comm_out


def _gla_layer(h, norm_g, w, fwd_comm=None):
    bf = lambda a: a.astype(BF16)
    w_in = w["gla_w_in"]
    nk, nv = GLA_HEADS * GLA_DK, GLA_HEADS * GLA_DV
    cuts = (0, nk, 2 * nk, 2 * nk + nv, 2 * nk + 2 * nv)
    w_q, w_k, w_v, w_g = (bf(w_in[:, cuts[i]:cuts[i + 1]]) for i in range(4))
    w_gk = bf(_pad_cols(w_in[:, cuts[4]:], LANE))
    w2 = _pad_rows(w["gla_w_gk2"], LANE)
    b_gk = w["gla_b_gk"].reshape(1, -1)
    g_o = w["gla_g_o"].reshape(1, -1)
    w_out = bf(w["gla_w_out"])

    n0 = _rms_op("gla_norm", h, norm_g)
    u, = n0.fwd()
    q, k, v, gate, gk = _mm_pieces("gla_in", u, (w_q, w_k, w_v, w_g, w_gk))
    gop = _gla_gate_op(gk, w2, b_gk)
    la, = gop.fwd()
    core = _gla_core_op(q, k, v, gate, la, g_o)
    y, = core.fwd(fwd_comm)
    h_out = _mm("gla_out", y, w_out, add=h)

    def bwd(dh, make_comm=None):
        dy = _mm("gla_out_dy", dh, w_out, tb=True, out_dtype=BF16)
        d_w_out = _mm("gla_out_dw", y, dh, ta=True, out_dtype=BF16)
        dq, dk, dv, dgate, dla, d_g_o = core.bwd([dy], comm=None if make_comm is None else make_comm(d_w_out))
        dgk, d_w2, d_b = gop.bwd([dla])
        du, d_in = _proj_bwd("gla_in", u, (dq, dk, dv, dgate, dgk), (w_q, w_k, w_v, w_g, w_gk))
        dh_in, d_norm = n0.bwd([du], addto={0: dh})
        g_in = jnp.concatenate([d_in[0], d_in[1], d_in[2], d_in[3], d_in[4][:, :GLA_RANK]], axis=1)
        return dh_in, d_norm, {"gla_w_in": g_in, "gla_w_gk2": d_w2[:GLA_RANK], "gla_b_gk": d_b.reshape(-1), "gla_g_o": d_g_o.reshape(-1),
                               "gla_w_out": d_w_out}, core.bwd_comm_out

    return h_out, bwd, core.fwd_comm_out


def _lru_layer(h, norm_g, w, fwd_comm=None):
    bf = lambda a: a.astype(BF16)
    w_in = w["lru_w_in"]
    w_g, w_u = bf(w_in[:, :LRU_WIDTH]), bf(w_in[:, LRU_WIDTH:])
    row = lambda a: a.reshape(1, -1)
    w_out = bf(w["lru_w_out"])

    n0 = _rms_op("lru_norm", h, norm_g)
    u_, = n0.fwd()
    gate, u = _mm_pieces("lru_in", u_, (w_g, w_u))
    core = _lru_op(gate, u, w["lru_conv_w"].reshape(CONV_W, 1, -1), row(w["lru_conv_b"]), w["lru_w_a"], row(w["lru_b_a"]), w["lru_w_x"],
                   row(w["lru_b_x"]), row(w["lru_lam"]))
    y, = core.fwd(fwd_comm)
    h_out = _mm("lru_out", y, w_out, add=h)

    def bwd(dh, make_comm=None):
        dy = _mm("lru_out_dy", dh, w_out, tb=True, out_dtype=BF16)
        d_w_out = _mm("lru_out_dw", y, dh, ta=True, out_dtype=BF16)
        dgate, du, d_cw, d_cb, d_wa, d_ba, d_wx, d_bx, d_lam = core.bwd([dy], comm=None if make_comm is None else make_comm(d_w_out))
        du_, d_in = _proj_bwd("lru_in", u_, (dgate, du), (w_g, w_u))
        dh_in, d_norm = n0.bwd([du_], addto={0: dh})
        return dh_in, d_norm, {"lru_w_in": jnp.concatenate(d_in, axis=1), "lru_conv_w": d_cw.reshape(CONV_W, -1), "lru_conv_b": d_cb.reshape(-1),
                               "lru_w_a": d_wa, "lru_b_a": d_ba.reshape(-1), "lru_w_x": d_wx, "lru_b_x": d_bx.reshape(-1),
                               "lru_lam": d_lam.reshape(-1), "lru_w_out": d_w_out}, core.bwd_comm_out

    return h_out, bwd, core.fwd_comm_out


def _ssd_layer(h, norm_g, w, fwd_comm=None, late_w_out=None):
    bf = lambda a: a.astype(BF16)
    s = h.shape[0]
    w_in = w["ssd_w_in"]
    conv_dim = SSD_INNER + 2 * SSD_GROUPS * SSD_STATE
    w_z, w_xbc = bf(w_in[:, :SSD_INNER]), bf(w_in[:, SSD_INNER:SSD_INNER + conv_dim])
    w_dt = bf(_pad_cols(w_in[:, SSD_INNER + conv_dim:], LANE))
    grp = lambda a: a.reshape(SSD_GROUPS, 1, SSD_HPG)

    n0 = _rms_op("ssd_norm", h, norm_g)
    u, = n0.fwd()
    z, xbc, dtp = _mm_pieces("ssd_in", u, (w_z, w_xbc, w_dt))
    conv = _ssd_conv_op(xbc, w["ssd_conv_w"].reshape(CONV_W, 1, -1), w["ssd_conv_b"].reshape(1, -1))
    x, bm, cm = conv.fwd()
    dt = dtp[:, :SSD_HEADS].reshape(s, SSD_GROUPS, SSD_HPG).transpose(1, 0, 2)
    core = _ssd_core_op(x, bm, cm, z, dt, grp(w["ssd_dt_bias"]), grp(w["ssd_a_log"]), grp(w["ssd_d"]), w["ssd_g_norm"].reshape(1, -1))
    y, = core.fwd(fwd_comm)
    w_out = bf(w["ssd_w_out"]) if late_w_out is None else late_w_out(core.fwd_comm_out)
    h_out = _mm("ssd_out", y, w_out, add=h)

    def bwd(dh, make_comm=None):
        dy = _mm("ssd_out_dy", dh, w_out, tb=True, out_dtype=BF16)
        d_w_out = _mm("ssd_out_dw", y, dh, ta=True, out_dtype=BF16)
        dx, dbm, dcm, dz, ddt, d_dtb, d_alog, d_d, d_gn = core.bwd([dy], comm=None if make_comm is None else make_comm(d_w_out))
        dxbc, d_cw, d_cb = conv.bwd([dx, dbm, dcm])
        ddtp = _pad_cols(ddt.transpose(1, 0, 2).reshape(s, SSD_HEADS), LANE).astype(BF16)
        du, d_in = _proj_bwd("ssd_in", u, (dz, dxbc, ddtp), (w_z, w_xbc, w_dt))
        dh_in, d_norm = n0.bwd([du], addto={0: dh})
        g_in = jnp.concatenate([d_in[0], d_in[1], d_in[2][:, :SSD_HEADS]], axis=1)
        return dh_in, d_norm, {"ssd_w_in": g_in, "ssd_conv_w": d_cw.reshape(CONV_W, -1), "ssd_conv_b": d_cb.reshape(-1),
                               "ssd_dt_bias": d_dtb.reshape(-1), "ssd_a_log": d_alog.reshape(-1), "ssd_d": d_d.reshape(-1),
                               "ssd_g_norm": d_gn.reshape(-1), "ssd_w_out": d_w_out}, core.bwd_comm_out

    return h_out, bwd


def _rope_tables(positions):
    inv_freq = ROPE_THETA ** (-jnp.arange(0, MLA_ROPE, 2, dtype=F32) / MLA_ROPE)
    ang = positions.astype(F32)[:, None] * inv_freq
    return jnp.cos(ang), jnp.sin(ang)


WEIGHTS = ["norm_g", "final_g", "mla_w_in", "mla_g_q", "mla_w_uq", "mla_g_kv", "mla_w_ukv", "mla_w_out", "gla_w_in", "gla_w_gk2", "gla_b_gk",
           "gla_g_o", "gla_w_out", "lru_w_in", "lru_conv_w", "lru_conv_b", "lru_w_a", "lru_b_a", "lru_w_x", "lru_b_x", "lru_lam", "lru_w_out",
           "ssd_w_in", "ssd_conv_w", "ssd_conv_b", "ssd_dt_bias", "ssd_a_log", "ssd_d", "ssd_g_norm", "ssd_w_out"]
BIG = ["mla_w_in", "mla_w_uq", "mla_w_ukv", "mla_w_out", "gla_w_in", "gla_w_out", "lru_w_in", "lru_w_out", "ssd_w_in", "ssd_w_out"]
SMALL = ["gla_w_gk2", "gla_b_gk", "gla_g_o", "lru_conv_w", "lru_conv_b", "lru_b_a", "lru_b_x", "lru_lam", "ssd_conv_w", "ssd_conv_b", "ssd_g_norm"]
REPL = ["norm_g", "final_g", "mla_g_q", "mla_g_kv", "lru_w_a", "lru_w_x", "ssd_dt_bias", "ssd_a_log", "ssd_d"]
REPL_EARLY = ["lru_w_a", "lru_w_x"]
REPL_LATE = [n for n in REPL if n not in REPL_EARLY]
N_CHIPS, N_DEV = 4, 8
PACK_W = 1024
ADAM_ROWS = 256
SMALL_ROWS = 64


def _shard_axis(name):
    return 0 if name.endswith("_w_out") else -1


def _pack(arrs, dtype, row_mult):
    flat = jnp.concatenate([a.reshape(-1).astype(dtype) for a in arrs])
    per = PACK_W * row_mult
    total = -(-flat.shape[0] // per) * per
    return jnp.pad(flat, (0, total - flat.shape[0])).reshape(-1, PACK_W)


def _unpack(buf, shapes):
    flat = buf.reshape(-1)
    out, off = [], 0
    for s in shapes:
        n = math.prod(s)
        out.append(flat[off:off + n].reshape(s))
        off += n
    return out


def _mesh_pos():
    return lax.axis_index("x"), lax.axis_index("y"), lax.axis_index("c")


class GatherComm:
    def __init__(self, ops):
        self.ops = list(ops)
        n = len(self.ops)
        assert all(o.ndim == 2 and o.shape[0] % 32 == 0 for o in self.ops), [o.shape for o in self.ops]
        self.out_shapes = [jax.ShapeDtypeStruct((N_CHIPS,) + o.shape, o.dtype) for o in self.ops]
        self.sem_shapes = [pltpu.SemaphoreType.DMA((6 * n,)), pltpu.SemaphoreType.DMA((6 * n,)), pltpu.SemaphoreType.DMA((n,))]

    def _copies(self, srcs, dsts, sems):
        send_sems, recv_sems, local_sems = sems
        n = len(self.ops)
        x, y, c = _mesh_pos()
        me_id, sibling = (x, y, c), (x, y, 1 - c)
        chips = [(1 - x, y), (x, 1 - y), (1 - x, 1 - y)]
        mine = 2 * x + y

        def half(i, cc):
            h = self.ops[i].shape[0] // 2
            return pl.ds(cc * h, h)

        def copy(i, k, src, slot, cc, to):
            return pltpu.make_async_remote_copy(src_ref=src, dst_ref=dsts[i].at[slot, half(i, cc)], send_sem=send_sems.at[i * 6 + k],
                                                recv_sem=recv_sems.at[i * 6 + k], device_id=to, device_id_type=pl.DeviceIdType.MESH)

        local = [pltpu.make_async_copy(srcs[i], dsts[i].at[mine], local_sems.at[i]) for i in range(n)]
        first, ici_recvs, passed, sib_recvs = [], [], [], []
        for i in range(n):
            my_half = srcs[i].at[half(i, c)]
            for k, (px, py) in enumerate(chips):
                slot = 2 * px + py
                first.append(copy(i, k, my_half, mine, c, (px, py, c)))
                ici_recvs.append(copy(i, k, my_half, slot, c, me_id))
                passed.append(copy(i, 3 + k, dsts[i].at[slot, half(i, c)], slot, c, sibling))
                sib_recvs.append(copy(i, 3 + k, my_half, slot, 1 - c, me_id))
        return local, first, ici_recvs, passed, sib_recvs

    def start(self, srcs, dsts, sems):
        local, first, _, _, _ = self._copies(srcs, dsts, sems)
        for cp in local + first:
            cp.start()

    def forward(self, srcs, dsts, sems):
        _, _, ici_recvs, passed, _ = self._copies(srcs, dsts, sems)
        for rc, fw in zip(ici_recvs, passed):
            rc.wait_recv()
            fw.start()

    def finish(self, srcs, dsts, sems):
        local, first, _, passed, sib_recvs = self._copies(srcs, dsts, sems)
        for cp in sib_recvs:
            cp.wait_recv()
        for cp in first + passed:
            cp.wait_send()
        for cp in local:
            cp.wait()


class ExchangeComm:
    def __init__(self, chip_ops, all_ops=()):
        self.ops = list(chip_ops) + list(all_ops)
        self.per_chip = (True,) * len(chip_ops) + (False,) * len(all_ops)
        n = len(self.ops)
        self.out_shapes = [jax.ShapeDtypeStruct((N_DEV,) + o.shape[-2:], o.dtype) for o in self.ops]
        self.sem_shapes = [pltpu.SemaphoreType.DMA((7 * n,)), pltpu.SemaphoreType.DMA((7 * n,)), pltpu.SemaphoreType.DMA((n,))]

    def _copies(self, srcs, dsts, sems):
        send_sems, recv_sems, local_sems = sems
        n, per_chip = len(self.ops), self.per_chip
        x, y, c = _mesh_pos()
        me_id, sibling = (x, y, c), (x, y, 1 - c)
        chips = [(1 - x, y), (x, 1 - y), (1 - x, 1 - y)]

        def dev(px, py, pc):
            return 4 * px + 2 * py + pc

        def part(i, px, py):
            return srcs[i].at[2 * px + py] if per_chip[i] else srcs[i]

        def copy(i, k, src, slot, to):
            return pltpu.make_async_remote_copy(src_ref=src, dst_ref=dsts[i].at[slot], send_sem=send_sems.at[i * 7 + k],
                                                recv_sem=recv_sems.at[i * 7 + k], device_id=to, device_id_type=pl.DeviceIdType.MESH)

        me = dev(x, y, c)
        local = [pltpu.make_async_copy(part(i, x, y), dsts[i].at[me], local_sems.at[i]) for i in range(n)]
        first, ici_recvs, passed, sib_recvs = [], [], [], []
        for i in range(n):
            first.append(copy(i, 0, part(i, x, y), me, sibling))
            first += [copy(i, 1 + k, part(i, px, py), me, (px, py, c)) for k, (px, py) in enumerate(chips)]
            sib_recvs.append(copy(i, 0, part(i, x, y), dev(x, y, 1 - c), me_id))
            for k, (px, py) in enumerate(chips):
                slot = dev(px, py, c)
                ici_recvs.append(copy(i, 1 + k, part(i, x, y), slot, me_id))
                passed.append(copy(i, 4 + k, dsts[i].at[slot], slot, sibling))
                sib_recvs.append(copy(i, 4 + k, part(i, x, y), dev(px, py, 1 - c), me_id))
        return local, first, ici_recvs, passed, sib_recvs

    def start(self, srcs, dsts, sems):
        local, first, _, _, _ = self._copies(srcs, dsts, sems)
        for cp in local + first:
            cp.start()

    def forward(self, srcs, dsts, sems):
        _, _, ici_recvs, passed, _ = self._copies(srcs, dsts, sems)
        for rc, fw in zip(ici_recvs, passed):
            rc.wait_recv()
            fw.start()

    def finish(self, srcs, dsts, sems):
        local, first, _, passed, sib_recvs = self._copies(srcs, dsts, sems)
        for cp in sib_recvs:
            cp.wait_recv()
        for cp in first + passed:
            cp.wait_send()
        for cp in local:
            cp.wait()


def _run_comm(name, comm):
    n = len(comm.ops)

    def body(*refs):
        srcs, dsts, sems = refs[:n], refs[n:2 * n], refs[2 * n:]
        comm.start(srcs, dsts, sems)
        comm.forward(srcs, dsts, sems)
        comm.finish(srcs, dsts, sems)

    any_spec = pl.BlockSpec(memory_space=pl.ANY)
    return pl.pallas_call(body, name=name, in_specs=[any_spec] * n, out_specs=[any_spec] * n, out_shape=comm.out_shapes,
                          scratch_shapes=comm.sem_shapes)(*comm.ops)


def _adamw(name, parts, w, m, v, lead=False, comm=None):
    plist = list(parts) if isinstance(parts, (list, tuple)) else [parts]
    n_p = len(plist)
    rows, cols = w.shape[-2:]
    t = next(c for c in (ADAM_ROWS, ADAM_ROWS // 2, SMALL_ROWS) if all(p.shape[1] % c == 0 for p in plist))
    starts = [sum(p.shape[1] for p in plist[:k]) // t for k in range(n_p)]
    counts = [p.shape[1] // t for p in plist]
    assert sum(p.shape[1] for p in plist) == rows, (name, rows)
    c1 = 1.0 - ADAM_B1 ** ADAM_STEP
    c2 = 1.0 - ADAM_B2 ** ADAM_STEP

    n_c = len(comm.ops) if comm is not None else 0
    n_steps = rows // t

    def body(*refs):
        p_refs = refs[:n_p]
        w_ref, m_ref, v_ref = refs[n_p:n_p + 3]
        c_src = refs[n_p + 3:n_p + 3 + n_c]
        g_ref, d_ref, nm_ref, nv_ref = refs[n_p + 3 + n_c:n_p + 7 + n_c]
        cargs = (c_src, refs[n_p + 7 + n_c:n_p + 7 + 2 * n_c], refs[n_p + 7 + 2 * n_c:])
        if comm is not None:
            @pl.when(pl.program_id(0) == 0)
            def _():
                comm.start(*cargs)

            @pl.when(pl.program_id(0) == _forward_step(n_steps))
            def _():
                comm.forward(*cargs)
        g = None
        for k, p_ref in enumerate(p_refs):
            gk = p_ref[0].astype(F32)
            for d in range(1, N_DEV):
                gk = gk + p_ref[d].astype(F32)
            g = gk if g is None else jnp.where(pl.program_id(0) >= starts[k], gk, g)
        nm = ADAM_B1 * m_ref[...] + (1.0 - ADAM_B1) * g
        nv = ADAM_B2 * v_ref[...] + (1.0 - ADAM_B2) * (g * g)
        g_ref[...] = g
        nm_ref[...] = nm
        nv_ref[...] = nv
        d_ref[...] = -ADAM_LR * ((nm / c1) / (jnp.sqrt(nv / c2) + ADAM_EPS) + ADAM_WD * w_ref[...])
        if comm is not None:
            @pl.when(pl.program_id(0) == n_steps - 1)
            def _():
                comm.finish(*cargs)

    row = pl.BlockSpec((None, t, cols), lambda i: (0, i, 0)) if lead else pl.BlockSpec((t, cols), lambda i: (i, 0))
    c_in_specs, c_args, c_out_specs, c_out_shapes, c_sems = _carry_specs(comm)
    res = pl.pallas_call(
        body, name=name, grid=(n_steps,),
        in_specs=[pl.BlockSpec((N_DEV, t, cols), lambda i, lo=lo, n=n: (0, jnp.clip(i - lo, 0, n - 1), 0)) for lo, n in zip(starts, counts)]
        + [row, row, row] + c_in_specs,
        out_specs=[row] * 4 + c_out_specs, out_shape=[jax.ShapeDtypeStruct(w.shape, F32)] * 4 + c_out_shapes, scratch_shapes=c_sems,
        compiler_params=_CP(dimension_semantics=("arbitrary" if comm is not None else "parallel",)),
    )(*plist, w, m, v, *c_args)
    return list(res[:4]), list(res[4:])


def _train_step(x, positions, target, wts, ms, vs, raw):
    small_shapes = [wts[n].shape for n in SMALL]

    big_of = {tag: [n for n in BIG if n.startswith(tag)] for tag in ("mla", "gla", "lru", "ssd")}
    full = {n: wts[n] for n in REPL}

    def gather_comm(names, extra=()):
        return GatherComm([wts[n].astype(BF16) for n in names] + list(extra))

    def assemble(names, got):
        for k, n in enumerate(names):
            full[n] = jnp.concatenate([got[k][j] for j in range(N_CHIPS)], axis=_shard_axis(n))

    first = [n for n in big_of["mla"] if n != "mla_w_out"]
    got = _run_comm("gather_first", gather_comm(first, [_pack([wts[n] for n in SMALL], F32, SMALL_ROWS)]))
    assemble(first, got)
    per_chip_small = [_unpack(got[-1][j], small_shapes) for j in range(N_CHIPS)]
    for k, n in enumerate(SMALL):
        full[n] = jnp.concatenate([per_chip_small[j][k] for j in range(N_CHIPS)], axis=_shard_axis(n))

    cos, sin = _rope_tables(positions)
    ng = full["norm_g"]
    behind_attn = ["mla_w_out"] + big_of["gla"] + big_of["lru"]

    def mla_w_out(got):
        assemble(behind_attn, got)
        return full["mla_w_out"].astype(BF16)

    h1, b0, _ = _mla_layer(x, ng[0], full, cos, sin, fwd_comm=gather_comm(behind_attn), late_w_out=mla_w_out)

    def joined(got_k, axis):
        return jnp.concatenate([got_k[j] for j in range(N_CHIPS)], axis=axis)

    ssd_in = wts["ssd_w_in"].astype(BF16)
    half = ssd_in.shape[0] // 2
    h2, b1, got = _gla_layer(h1, ng[1], full, fwd_comm=GatherComm([ssd_in[:half]]))
    top = joined(got[0], -1)
    h3, b2, got = _lru_layer(h2, ng[2], full, fwd_comm=GatherComm([ssd_in[half:]]))
    full["ssd_w_in"] = jnp.concatenate([top, joined(got[0], -1)], axis=0)
    h4, b3 = _ssd_layer(h3, ng[3], full, fwd_comm=gather_comm(["ssd_w_out"]), late_w_out=lambda got: joined(got[0], 0))
    loss, dh, d_final = _loss_op(h4, target, full["final_g"])
    loss = loss[0, 0]
    grads = {"final_g": d_final.reshape(-1)}
    d_norms = [None] * 4

    def shards_of(n, g):
        return jnp.stack(jnp.split(g.astype(BF16), N_CHIPS, axis=_shard_axis(n)))

    def shards(n):
        return shards_of(n, grads[n])

    parts = {}
    dh, d_norms[3], gw, got = b3(dh, make_comm=lambda dw: ExchangeComm([shards_of("ssd_w_out", dw)]))
    parts["ssd_w_out"] = got[0]
    grads.update(gw)
    ssd_in_g = shards("ssd_w_in")
    half = ssd_in_g.shape[1] * 11 // 16
    dh, d_norms[2], gw, got = b2(dh, make_comm=lambda dw: ExchangeComm([ssd_in_g[:, :half], shards_of("lru_w_out", dw)]))
    parts["ssd_w_in"], parts["lru_w_out"] = [got[0]], got[1]
    grads.update(gw)
    dh, d_norms[1], gw, got = b1(dh, make_comm=lambda dw: ExchangeComm([ssd_in_g[:, half:]]))
    parts["ssd_w_in"].append(got[0])
    grads.update(gw)
    repl_early = _pack([grads[n] for n in REPL_EARLY], BF16, SMALL_ROWS)

    behind_attn = ["gla_w_out", "lru_w_in", "gla_w_in"]
    dx, d_norms[0], gw, got = b0(dh, make_comm=lambda dw: ExchangeComm([shards_of("mla_w_out", dw)] + [shards(n) for n in behind_attn],
                                                                        [repl_early]))
    parts.update(zip(["mla_w_out"] + behind_attn, got))
    repl_early_parts = got[-1]
    grads.update(gw)
    grads["norm_g"] = jnp.concatenate(d_norms, axis=0)
    psmall = jnp.stack([_pack([jnp.split(grads[n], N_CHIPS, axis=_shard_axis(n))[j] for n in SMALL], F32, SMALL_ROWS) for j in range(N_CHIPS)])
    prepl = _pack([grads[n] for n in REPL_LATE], F32, SMALL_ROWS)

    out = {}
    kinds = ("grad", "delta", "new_m", "new_v")
    late = [n for n in big_of["mla"] if n != "mla_w_out"]
    late_comm = ExchangeComm([shards(n) for n in late] + [psmall], [prepl])
    for n in ["ssd_w_in"] + [n for n in BIG if n != "ssd_w_in"]:
        if n == "ssd_w_in":
            res, late_parts = _adamw("adam_" + n, parts[n], *(r[n] for r in raw), lead=True, comm=late_comm)
            parts.update(zip(late, late_parts))
        else:
            res, _ = _adamw("adam_" + n, parts[n], *(r[n] for r in raw), lead=True)
        for kind, a in zip(kinds, res):
            out[kind, n] = a
    for tag, names, p in (("adam_small", SMALL, late_parts[-2]), ("adam_repl_early", REPL_EARLY, repl_early_parts),
                          ("adam_repl_late", REPL_LATE, late_parts[-1])):
        shapes = [wts[n].shape for n in names]
        packed = [_pack([d[n] for n in names], F32, SMALL_ROWS) for d in (wts, ms, vs)]
        for kind, buf in zip(kinds, _adamw(tag, p, *packed)[0]):
            for n, a in zip(names, _unpack(buf, shapes)):
                out[kind, n] = a
    loss = lax.psum(loss, ("x", "y", "c"))
    return loss, dx, out


def kernel(x, positions, norm_g, final_g, mla_w_in, mla_g_q, mla_w_uq, mla_g_kv, mla_w_ukv, mla_w_out, gla_w_in, gla_w_gk2, gla_b_gk, gla_g_o, gla_w_out, lru_w_in, lru_conv_w, lru_conv_b, lru_w_a, lru_b_a, lru_w_x, lru_b_x, lru_lam, lru_w_out, ssd_w_in, ssd_conv_w, ssd_conv_b, ssd_dt_bias, ssd_a_log, ssd_d, ssd_g_norm, ssd_w_out, loss_target, m_norm_g, m_final_g, m_mla_w_in, m_mla_g_q, m_mla_w_uq, m_mla_g_kv, m_mla_w_ukv, m_mla_w_out, m_gla_w_in, m_gla_w_gk2, m_gla_b_gk, m_gla_g_o, m_gla_w_out, m_lru_w_in, m_lru_conv_w, m_lru_conv_b, m_lru_w_a, m_lru_b_a, m_lru_w_x, m_lru_b_x, m_lru_lam, m_lru_w_out, m_ssd_w_in, m_ssd_conv_w, m_ssd_conv_b, m_ssd_dt_bias, m_ssd_a_log, m_ssd_d, m_ssd_g_norm, m_ssd_w_out, v_norm_g, v_final_g, v_mla_w_in, v_mla_g_q, v_mla_w_uq, v_mla_g_kv, v_mla_w_ukv, v_mla_w_out, v_gla_w_in, v_gla_w_gk2, v_gla_b_gk, v_gla_g_o, v_gla_w_out, v_lru_w_in, v_lru_conv_w, v_lru_conv_b, v_lru_w_a, v_lru_b_a, v_lru_w_x, v_lru_b_x, v_lru_lam, v_lru_w_out, v_ssd_w_in, v_ssd_conv_w, v_ssd_conv_b, v_ssd_dt_bias, v_ssd_a_log, v_ssd_d, v_ssd_g_norm, v_ssd_w_out):
    given = dict(locals())
    stacked = [n for n in WEIGHTS if n not in ("norm_g", "final_g")]

    def blocks(prefix):
        return {n: (given[prefix + n][0] if n in stacked else given[prefix + n]) for n in WEIGHTS}

    raw = [{n: given[prefix + n] for n in BIG} for prefix in ("", "m_", "v_")]
    loss, dx, out = _train_step(x[0], positions[0], loss_target[0], blocks(""), blocks("m_"), blocks("v_"), raw)
    res = [loss, dx[None]]
    for kind in ("grad", "delta", "new_m", "new_v"):
        res += [(out[kind, n][None] if n in stacked and n not in BIG else out[kind, n]) for n in WEIGHTS]
    return tuple(res)
```

```python
import functools
import math

import jax
import jax.numpy as jnp
from jax import lax
from jax.experimental import pallas as pl
from jax.experimental.pallas import tpu as pltpu

F32 = jnp.float32
BF16 = jnp.bfloat16

V7X_VMEM_BYTES = 64 * 1024 * 1024
VMEM_LIMIT = V7X_VMEM_BYTES - 8 * 1024 * 1024
LANE = 128

D_MODEL = 1024
NORM_EPS = 1e-6
MLA_HEADS, MLA_Q_RANK, MLA_KV_RANK, MLA_NOPE, MLA_ROPE, MLA_V = 16, 384, 256, 64, 32, 64
MLA_QK = MLA_NOPE + MLA_ROPE
ROPE_THETA = 10000.0
GLA_HEADS, GLA_DK, GLA_DV, GLA_RANK, GLA_TAU, GLA_CHUNK = 4, 128, 256, 16, 16.0, 64
LRU_WIDTH, LRU_BLOCKS, LRU_BLOCK, LRU_C, CONV_W = 1280, 10, 128, 8.0, 4
SSD_INNER, SSD_P, SSD_HEADS, SSD_GROUPS, SSD_HPG, SSD_STATE, SSD_CHUNK = 2048, 64, 32, 8, 4, 128, 64
ADAM_LR, ADAM_B1, ADAM_B2, ADAM_EPS, ADAM_WD, ADAM_STEP = 0.001, 0.9, 0.999, 1e-08, 0.01, 10

_CP = functools.partial(pltpu.CompilerParams, vmem_limit_bytes=VMEM_LIMIT)


def _bdot(a, b):
    return jnp.dot(a.astype(BF16), b.astype(BF16), preferred_element_type=F32)


def _bdot_nt(a, b):
    return lax.dot_general(a.astype(BF16), b.astype(BF16), (((1,), (1,)), ((), ())), preferred_element_type=F32)


def _bdot_tn(a, b):
    return lax.dot_general(a.astype(BF16), b.astype(BF16), (((0,), (0,)), ((), ())), preferred_element_type=F32)


def _tri(n):
    r = lax.broadcasted_iota(jnp.int32, (n, n), 0)
    c = lax.broadcasted_iota(jnp.int32, (n, n), 1)
    return r >= c


def _rms(x, g):
    return x * lax.rsqrt(jnp.mean(x * x, axis=-1, keepdims=True) + NORM_EPS) * g


def _silu(x):
    return x * jax.nn.sigmoid(x)


def _shift_rows(x, prev, j):
    if j == 0:
        return x
    t = x.shape[0]

    def fwd_impl(x, prev):
        row = lax.broadcasted_iota(jnp.int32, x.shape, 0)
        return jnp.where(row >= j, pltpu.roll(x, j, 0), pltpu.roll(prev, j, 0))

    @jax.custom_vjp
    def sh(x, prev):
        return fwd_impl(x, prev)

    def sh_fwd(x, prev):
        return fwd_impl(x, prev), None

    def sh_bwd(_, gy):
        row = lax.broadcasted_iota(jnp.int32, gy.shape, 0)
        back = pltpu.roll(gy, t - j, 0)
        return jnp.where(row < t - j, back, 0.0), jnp.where(row >= t - j, back, 0.0)

    sh.defvjp(sh_fwd, sh_bwd)
    return sh(x, prev)


def _cumsum_rows(x):
    zero = jnp.zeros_like(x)
    sh = 1
    while sh < x.shape[0]:
        x = x + _shift_rows(x, zero, sh)
        sh *= 2
    return x


def _one_minus_exp(x):
    series = -x * (1.0 + x * (0.5 + x * (1.0 / 6.0 + x * (1.0 / 24.0 + x * (1.0 / 120.0)))))
    return jnp.where(x > -0.05, series, 1.0 - jnp.exp(x))


def _tile(n, cap):
    if n <= cap:
        return n
    best = None
    for t in range(LANE, cap + 1, LANE):
        if n % t == 0:
            best = t
    assert best is not None, (n, cap)
    return best


MM_BLOCK_BYTES = 8 * 1024 * 1024
MM_ROWS, MM_KROWS = 256, 512
MM_TILE_BYTES = 2 * 1024 * 1024


def _mm_tiles(m, k, n, ta):
    if ta:
        return m, _tile(n, max(LANE, MM_BLOCK_BYTES // (4 * m) // LANE * LANE)), _tile(k, MM_KROWS)
    tn = _tile(n, max(LANE, MM_BLOCK_BYTES // (2 * k) // LANE * LANE))
    rows = min(4 * MM_ROWS, max(MM_ROWS, MM_TILE_BYTES // (4 * tn) // MM_ROWS * MM_ROWS))
    return _tile(m, rows), tn, k


def _mm(name, a, b, *, ta=False, tb=False, add=None, out_dtype=F32):
    m, k = (a.shape[1], a.shape[0]) if ta else a.shape
    n, kb = (b.shape[0], b.shape[1]) if tb else (b.shape[1], b.shape[0])
    assert k == kb, (name, a.shape, b.shape, ta, tb)
    tm, tn, tk = _mm_tiles(m, k, n, ta)
    nk = k // tk
    dn = (((0 if ta else 1,), (1 if tb else 0,)), ((), ()))
    has_add = add is not None

    def finish(refs, r):
        if has_add:
            r = r + refs[2][...].astype(F32)
        return r.astype(out_dtype)

    def body_one(*refs):
        a_ref, b_ref, o_ref = refs[0], refs[1], refs[-1]
        o_ref[...] = finish(refs, lax.dot_general(a_ref[...].astype(BF16), b_ref[...].astype(BF16), dn, preferred_element_type=F32))

    def body_acc(*refs):
        a_ref, b_ref = refs[0], refs[1]
        o_ref, acc = refs[-2], refs[-1]
        kk = pl.program_id(2)

        @pl.when(kk == 0)
        def _():
            acc[...] = jnp.zeros(acc.shape, F32)

        acc[...] += lax.dot_general(a_ref[...].astype(BF16), b_ref[...].astype(BF16), dn, preferred_element_type=F32)

        @pl.when(kk == nk - 1)
        def _():
            o_ref[...] = finish(refs, acc[...])

    a_spec = pl.BlockSpec((tk, tm), lambda i, j, q: (q, i)) if ta else pl.BlockSpec((tm, tk), lambda i, j, q: (i, q))
    b_spec = pl.BlockSpec((tn, tk), lambda i, j, q: (j, q)) if tb else pl.BlockSpec((tk, tn), lambda i, j, q: (q, j))
    o_spec = pl.BlockSpec((tm, tn), lambda i, j, q: (i, j))
    in_specs, args = [a_spec, b_spec], [a, b]
    if has_add:
        in_specs.append(o_spec)
        args.append(add)
    return pl.pallas_call(
        body_one if nk == 1 else body_acc, name=name, grid=(m // tm, n // tn, nk), in_specs=in_specs, out_specs=o_spec,
        out_shape=jax.ShapeDtypeStruct((m, n), out_dtype), scratch_shapes=[] if nk == 1 else [pltpu.VMEM((tm, tn), F32)],
        compiler_params=_CP(dimension_semantics=("parallel", "parallel", "arbitrary")),
    )(*args)


def _mm_pieces(name, a, bs):
    m, k = a.shape
    n_b = len(bs)
    tm = _tile(m, MM_ROWS)

    def body(*refs):
        av = refs[0][...].astype(BF16)
        for b_ref, o_ref in zip(refs[1:1 + n_b], refs[1 + n_b:]):
            o_ref[...] = jnp.dot(av, b_ref[...].astype(BF16), preferred_element_type=F32)

    return pl.pallas_call(
        body, name=name, grid=(m // tm,),
        in_specs=[pl.BlockSpec((tm, k), lambda i: (i, 0))] + [pl.BlockSpec(b.shape, lambda i: (0, 0)) for b in bs],
        out_specs=[pl.BlockSpec((tm, b.shape[1]), lambda i: (i, 0)) for b in bs],
        out_shape=[jax.ShapeDtypeStruct((m, b.shape[1]), F32) for b in bs],
        compiler_params=_CP(dimension_semantics=("parallel",)),
    )(a, *bs)


def _mm_pieces_t(name, dps, ws):
    m, k = dps[0].shape[0], ws[0].shape[0]
    n_b = len(ws)
    tm = _tile(m, MM_ROWS)
    dn = (((1,), (1,)), ((), ()))

    def body(*refs):
        acc = None
        for d_ref, w_ref in zip(refs[:n_b], refs[n_b:2 * n_b]):
            part = lax.dot_general(d_ref[...].astype(BF16), w_ref[...].astype(BF16), dn, preferred_element_type=F32)
            acc = part if acc is None else acc + part
        refs[-1][...] = acc

    return pl.pallas_call(
        body, name=name, grid=(m // tm,),
        in_specs=[pl.BlockSpec((tm, d.shape[1]), lambda i: (i, 0)) for d in dps] + [pl.BlockSpec(w.shape, lambda i: (0, 0)) for w in ws],
        out_specs=pl.BlockSpec((tm, k), lambda i: (i, 0)), out_shape=jax.ShapeDtypeStruct((m, k), F32),
        compiler_params=_CP(dimension_semantics=("parallel",)),
    )(*dps, *ws)


def _mm_pieces_dw(name, u, dps):
    s, k = u.shape
    n_b = len(dps)
    tk = _tile(s, MM_KROWS)
    n_steps = s // tk
    dn = (((0,), (0,)), ((), ()))

    def body(*refs):
        u_ref, d_refs, o_refs, accs = refs[0], refs[1:1 + n_b], refs[1 + n_b:1 + 2 * n_b], refs[1 + 2 * n_b:]
        step = pl.program_id(0)
        uv = u_ref[...].astype(BF16)
        for d_ref, o_ref, acc in zip(d_refs, o_refs, accs):
            part = lax.dot_general(uv, d_ref[...].astype(BF16), dn, preferred_element_type=F32)

            @pl.when(step == 0)
            def _(acc=acc, part=part):
                acc[...] = part

            @pl.when(step > 0)
            def _(acc=acc, part=part):
                acc[...] += part

            @pl.when(step == n_steps - 1)
            def _(acc=acc, o_ref=o_ref):
                o_ref[...] = acc[...].astype(o_ref.dtype)

    return pl.pallas_call(
        body, name=name, grid=(n_steps,),
        in_specs=[pl.BlockSpec((tk, k), lambda i: (i, 0))] + [pl.BlockSpec((tk, d.shape[1]), lambda i: (i, 0)) for d in dps],
        out_specs=[pl.BlockSpec((k, d.shape[1]), lambda i: (0, 0)) for d in dps],
        out_shape=[jax.ShapeDtypeStruct((k, d.shape[1]), BF16) for d in dps],
        scratch_shapes=[pltpu.VMEM((k, d.shape[1]), F32) for d in dps],
        compiler_params=_CP(dimension_semantics=("arbitrary",)),
    )(u, *dps)


MM_DW_MERGE_COLS = 4096


class In:
    def __init__(self, arr, block, imap, kind="x", per_h=False, gdtype=F32, gshape=None, gimap=None, prefixed=False):
        self.arr, self.block, self.imap, self.kind, self.per_h, self.gdtype = arr, tuple(block), imap, kind, per_h, gdtype
        self.prefixed = prefixed
        self.gshape = tuple(gshape) if gshape is not None else tuple(arr.shape)
        self.gimap = gimap if gimap is not None else imap

    def spec(self, rev_g=None):
        imap = self.imap
        if rev_g is None:
            return pl.BlockSpec(self.block, lambda h, g: imap(h, g))
        return pl.BlockSpec(self.block, lambda h, g: imap(h, rev_g - 1 - g))


class Out:
    def __init__(self, shape, dtype, block, imap):
        self.shape, self.dtype, self.block, self.imap = tuple(shape), dtype, tuple(block), imap

    def spec(self, rev_g=None):
        imap = self.imap
        if rev_g is None:
            return pl.BlockSpec(self.block, lambda h, g: imap(h, g))
        return pl.BlockSpec(self.block, lambda h, g: imap(h, rev_g - 1 - g))


def _load_f32(ref, rows=None):
    v = ref[...] if rows is None else ref[0:rows]
    return v.astype(F32) if jnp.issubdtype(v.dtype, jnp.floating) else v


def _state_out(grid, shape):
    nd = len(shape)
    return Out(tuple(grid) + tuple(shape), F32, (None, None) + tuple(shape), lambda h, g: (h, g) + (0,) * nd)


def _carry(comm, grid, refs, n_in, n_out, n_scr):
    n_c = len(comm.ops) if comm is not None else 0
    n_s = len(comm.sem_shapes) if comm is not None else 0
    p = 0
    in_refs = refs[p:p + n_in]; p += n_in
    c_src = refs[p:p + n_c]; p += n_c
    out_refs = refs[p:p + n_out]; p += n_out
    c_dst = refs[p:p + n_c]; p += n_c
    scr = refs[p:p + n_scr]; p += n_scr
    c_sem = refs[p:p + n_s]
    step = pl.program_id(0) * grid[1] + pl.program_id(1)
    n_steps = grid[0] * grid[1]
    when = (step == 0, step == _forward_step(n_steps), step == n_steps - 1)
    return in_refs, out_refs, scr, (c_src, c_dst, c_sem), when


def _forward_step(n_steps):
    return max(0, min(n_steps - 2, (3 * n_steps) // 4))


def _carry_specs(comm):
    if comm is None:
        return [], [], [], [], []
    any_spec = pl.BlockSpec(memory_space=pl.ANY)
    n = len(comm.ops)
    return [any_spec] * n, list(comm.ops), [any_spec] * n, list(comm.out_shapes), list(comm.sem_shapes)


def _op_fwd(name, f, grid, ins, outs, state_shapes=(), comm=None, prefix_rows=None):
    assert prefix_rows is None or not state_shapes
    n_in, n_out, n_st = len(ins), len(outs), len(state_shapes)
    st_outs = [_state_out(grid, s) for s in state_shapes]

    def body(*refs):
        in_refs, o_refs, st_scr, cargs, (first, fwd_step, last) = _carry(comm, grid, refs, n_in, n_out + n_st, n_st)
        out_refs, sv_refs = o_refs[:n_out], o_refs[n_out:]
        if comm is not None:
            @pl.when(first)
            def _():
                comm.start(*cargs)

            @pl.when(fwd_step)
            def _():
                comm.forward(*cargs)
        g = pl.program_id(1)
        if n_st:
            @pl.when(g == 0)
            def _():
                for s in st_scr:
                    s[...] = jnp.zeros(s.shape, F32)

        def compute(rows, g=g):
            vals = [_load_f32(r, rows if i.prefixed else None) for r, i in zip(in_refs, ins)]
            sts = [s[...] for s in st_scr]
            o, ns = f(g, vals, sts)
            for r, v in zip(out_refs, o):
                r[...] = v.astype(r.dtype)
            for r, s in zip(sv_refs, sts):
                r[...] = s
            for s, v in zip(st_scr, ns):
                s[...] = v

        if prefix_rows is None:
            compute(None)
        else:
            per = grid[1] // len(prefix_rows)
            for lv, rows in enumerate(prefix_rows):
                pl.when(g // per == lv)(functools.partial(compute, rows, lv) if per == 1 else functools.partial(compute, rows))
        if comm is not None:
            @pl.when(last)
            def _():
                comm.finish(*cargs)

    all_outs = list(outs) + st_outs
    c_in_specs, c_args, c_out_specs, c_out_shapes, c_sems = _carry_specs(comm)
    res = pl.pallas_call(
        body, name=name, grid=tuple(grid), in_specs=[i.spec() for i in ins] + c_in_specs,
        out_specs=[o.spec() for o in all_outs] + c_out_specs,
        out_shape=[jax.ShapeDtypeStruct(o.shape, o.dtype) for o in all_outs] + c_out_shapes,
        scratch_shapes=[pltpu.VMEM(tuple(s), F32) for s in state_shapes] + c_sems,
        compiler_params=_CP(dimension_semantics=("arbitrary", "arbitrary")),
    )(*[i.arr for i in ins], *c_args)
    n_all = n_out + n_st
    return list(res[:n_out]), list(res[n_out:n_all]), list(res[n_all:])


def _op_bwd(name, f, grid, ins, outs, state_shapes, saved, douts, addto=None, comm=None, prefix_rows=None):
    n_in, n_out, n_st = len(ins), len(outs), len(state_shapes)
    n_g = grid[1]
    assert prefix_rows is None or all(i.prefixed and i.per_h for i in ins if i.kind == "p")
    addto = addto or {}
    diff = [k for k, i in enumerate(ins) if i.kind in ("x", "p")]
    add_idx = sorted(addto)
    st_ins = [In(s, o.block, o.imap, "c") for s, o in zip(saved, [_state_out(grid, s) for s in state_shapes])]
    dout_ins = [In(d, o.block, o.imap, "c") for d, o in zip(douts, outs)]
    add_ins = []
    for k in add_idx:
        i, a = ins[k], addto[k]
        blk = i.block if i.kind == "x" else i.block[:-2] + a.shape[-2:]
        add_ins.append(In(a, blk, i.gimap if i.kind == "x" else i.imap, "c"))
    g_outs = []
    for k in diff:
        i = ins[k]
        g_outs.append(Out(i.gshape, i.gdtype if i.kind == "x" else F32, i.block, i.gimap))

    def body(*refs):
        all_in, go_refs, ds_scr, cargs, (first_step, fwd_step, last_step) = _carry(comm, grid, refs, n_in + n_st + n_out + len(add_idx),
                                                                                    len(diff), n_st)
        if comm is not None:
            @pl.when(first_step)
            def _():
                comm.start(*cargs)

            @pl.when(fwd_step)
            def _():
                comm.forward(*cargs)
        p = 0
        in_refs = all_in[p:p + n_in]; p += n_in
        sv_refs = all_in[p:p + n_st]; p += n_st
        do_refs = all_in[p:p + n_out]; p += n_out
        ad_refs = all_in[p:p + len(add_idx)]
        hh = pl.program_id(0)
        step = pl.program_id(1)
        g = n_g - 1 - step
        if n_st:
            @pl.when(step == 0)
            def _():
                for s in ds_scr:
                    s[...] = jnp.zeros(s.shape, F32)

        def compute(rows, g=g):
            vals = [_load_f32(r, rows if i.prefixed else None) for r, i in zip(in_refs, ins)]
            sts = [r[...] for r in sv_refs]

            def fw(dvals, states):
                full = list(vals)
                for k, v in zip(diff, dvals):
                    full[k] = v
                o, ns = f(g, full, states)
                return list(o), list(ns)

            _, vjp = jax.vjp(fw, [vals[k] for k in diff], sts)
            cts = [r[...].astype(F32) for r in do_refs]
            dns = [s[...] for s in ds_scr]
            dvals, dsts = vjp((cts, dns))
            adds = dict(zip(add_idx, ad_refs))
            for k, r, dv in zip(diff, go_refs, dvals):
                i = ins[k]
                if i.kind == "x":
                    if k in adds:
                        dv = dv + adds[k][...].astype(F32)
                    r[...] = dv.astype(r.dtype)
                elif rows is not None:
                    r[0:rows] += dv
                else:
                    first = (step == 0) if i.per_h else jnp.logical_and(step == 0, hh == 0)

                    @pl.when(first)
                    def _(r=r, dv=dv, k=k):
                        r[...] = dv
                        if k in adds:
                            lead = adds[k].shape[0]
                            r[0:lead] += adds[k][...]

                    @pl.when(jnp.logical_not(first))
                    def _(r=r, dv=dv):
                        r[...] += dv
            for s, v in zip(ds_scr, dsts):
                s[...] = v

        if prefix_rows is None:
            compute(None)
        else:
            @pl.when(step == 0)
            def _():
                for k, r in zip(diff, go_refs):
                    if ins[k].kind == "p":
                        r[...] = jnp.zeros(r.shape, F32)
            per = n_g // len(prefix_rows)
            for lv, rows in enumerate(prefix_rows):
                pl.when(g // per == lv)(functools.partial(compute, rows, lv) if per == 1 else functools.partial(compute, rows))
        if comm is not None:
            @pl.when(last_step)
            def _():
                comm.finish(*cargs)

    all_ins = list(ins) + st_ins + dout_ins + add_ins
    c_in_specs, c_args, c_out_specs, c_out_shapes, c_sems = _carry_specs(comm)
    res = pl.pallas_call(
        body, name=name, grid=tuple(grid), in_specs=[i.spec(n_g) for i in all_ins] + c_in_specs,
        out_specs=[o.spec(n_g) for o in g_outs] + c_out_specs,
        out_shape=[jax.ShapeDtypeStruct(o.shape, o.dtype) for o in g_outs] + c_out_shapes,
        scratch_shapes=[pltpu.VMEM(tuple(s), F32) for s in state_shapes] + c_sems,
        compiler_params=_CP(dimension_semantics=("arbitrary", "arbitrary")),
    )(*[i.arr for i in all_ins], *c_args)
    return list(res[:len(g_outs)]), list(res[len(g_outs):])


class Op:
    def __init__(self, name, f, grid, ins, outs, state_shapes=(), prefix_rows=None):
        self.name, self.f, self.grid, self.ins, self.outs, self.state_shapes = name, f, grid, ins, outs, state_shapes
        self.prefix_rows = prefix_rows
        self.saved = None

    def fwd(self, comm=None):
        res, self.saved, self.fwd_comm_out = _op_fwd(self.name + "_fwd", self.f, self.grid, self.ins, self.outs, self.state_shapes, comm,
                                                     self.prefix_rows)
        return res

    def bwd(self, douts, addto=None, comm=None):
        res, self.bwd_comm_out = _op_bwd(self.name + "_bwd", self.f, self.grid, self.ins, self.outs, self.state_shapes, self.saved, douts,
                                         addto, comm, self.prefix_rows)
        return res


def _rows(arr, t, kind="x", gdtype=F32):
    return In(arr, (t, arr.shape[1]), lambda h, g: (g, 0), kind, gdtype=gdtype)


def _whole(arr, kind="p"):
    nd = arr.ndim
    return In(arr, arr.shape, lambda h, g: (0,) * nd, kind)


def _rows_out(s, n, t, dtype):
    return Out((s, n), dtype, (t, n), lambda h, g: (g, 0))


ROW_T = 512
CONV_T = 256


def _rms_op(name, x, gain, out_dtype=BF16, gdtype=F32):
    s, n = x.shape

    def f(g, vals, sts):
        return [_rms(vals[0], vals[1])], []

    return Op(name, f, (1, s // ROW_T), [_rows(x, ROW_T, gdtype=gdtype), _whole(gain.reshape(1, n))], [_rows_out(s, n, ROW_T, out_dtype)])


def _mla_prep_op(qn, q1, q2, kn, kr, v, cos, sin):
    s = qn.shape[0]
    hd, half = MLA_HEADS, MLA_ROPE // 2

    def f(g, vals, sts):
        qn, q1, q2, kn, kr, v, cos, sin = vals
        cos_h, sin_h = jnp.tile(cos, (1, hd)), jnp.tile(sin, (1, hd))
        r1 = q1 * cos_h - q2 * sin_h
        r2 = q2 * cos_h + q1 * sin_h
        k1, k2 = kr[:, 0:half], kr[:, half:2 * half]
        kr1 = k1 * cos - k2 * sin
        kr2 = k2 * cos + k1 * sin
        zpad = jnp.zeros((qn.shape[0], LANE - MLA_QK), F32)
        qs, ks, vs = [], [], []
        for h in range(hd):
            a, b = h * MLA_NOPE, (h + 1) * MLA_NOPE
            c, d = h * half, (h + 1) * half
            qs.append(jnp.concatenate([qn[:, a:b], r1[:, c:d], r2[:, c:d], zpad], axis=1))
            ks.append(jnp.concatenate([kn[:, a:b], kr1, kr2, zpad], axis=1))
            vs.append(v[:, a:b])
        return [jnp.stack(qs, 0), jnp.stack(ks, 0), jnp.stack(vs, 0)], []

    ins = [_rows(qn, ROW_T, gdtype=BF16), _rows(q1, ROW_T, gdtype=BF16), _rows(q2, ROW_T, gdtype=BF16), _rows(kn, ROW_T, gdtype=BF16),
           _rows(kr, ROW_T, gdtype=BF16), _rows(v, ROW_T, gdtype=BF16), _rows(cos, ROW_T, "c"), _rows(sin, ROW_T, "c")]
    outs = [Out((hd, s, LANE), BF16, (hd, ROW_T, LANE), lambda h, g: (0, g, 0)),
            Out((hd, s, LANE), BF16, (hd, ROW_T, LANE), lambda h, g: (0, g, 0)),
            Out((hd, s, MLA_V), BF16, (hd, ROW_T, MLA_V), lambda h, g: (0, g, 0))]
    return Op("mla_prep", f, (1, s // ROW_T), ins, outs)


ATT_TQ = 256
ATT_LEVELS = 8


def _mla_attn_op(q, k, v):
    hd, s, _ = q.shape
    scale = MLA_QK ** -0.5

    def f(g, vals, sts):
        q, k, v = vals
        sc = _bdot_nt(q, k) * scale
        r = lax.broadcasted_iota(jnp.int32, sc.shape, 0) + g * ATT_TQ
        c = lax.broadcasted_iota(jnp.int32, sc.shape, 1)
        sc = jnp.where(r >= c, sc, -1e30)
        m = lax.stop_gradient(jnp.max(sc, axis=-1, keepdims=True))
        p = jnp.exp(sc - m)
        p = p * (1.0 / jnp.sum(p, axis=-1, keepdims=True))
        return [_bdot(p, v)], []

    ins = [In(q, (None, ATT_TQ, LANE), lambda h, g: (h, g, 0), "x", gdtype=BF16),
           In(k, (None, s, LANE), lambda h, g: (h, 0, 0), "p", per_h=True, prefixed=True),
           In(v, (None, s, MLA_V), lambda h, g: (h, 0, 0), "p", per_h=True, prefixed=True)]
    outs = [Out((hd, s, MLA_V), F32, (None, ATT_TQ, MLA_V), lambda h, g: (h, g, 0))]
    return Op("mla_attn", f, (hd, s // ATT_TQ), ins, outs, prefix_rows=[(lv + 1) * (s // ATT_LEVELS) for lv in range(ATT_LEVELS)])


def _mla_post_op(o, gate):
    hd, s, _ = o.shape

    def f(g, vals, sts):
        o, gate = vals
        cat = jnp.concatenate([o[h] for h in range(hd)], axis=1)
        return [cat * _silu(gate)], []

    ins = [In(o, (hd, ROW_T, MLA_V), lambda h, g: (0, g, 0), "x"), _rows(gate, ROW_T, gdtype=BF16)]
    return Op("mla_post", f, (1, s // ROW_T), ins, [_rows_out(s, hd * MLA_V, ROW_T, BF16)])


def _gla_gate_op(gk, w2, b):
    s = gk.shape[0]

    def f(g, vals, sts):
        gk, w2, b = vals
        return [jax.nn.log_sigmoid(_bdot(gk, w2) + b) / GLA_TAU], []

    ins = [_rows(gk, ROW_T, gdtype=BF16), _whole(w2), _whole(b)]
    return Op("gla_gate", f, (1, s // ROW_T), ins, [_rows_out(s, GLA_HEADS * GLA_DK, ROW_T, F32)])


def _gla_core_op(q, k, v, gate, la, g_o):
    s = q.shape[0]
    c, nh = GLA_CHUNK, GLA_HEADS

    def f(g, vals, sts):
        q, k, v, gate, la, g_o = vals
        tri = _tri(c)
        b = _cumsum_rows(la)
        b_last = jnp.sum(la, axis=0, keepdims=True)
        qt = q * (GLA_DK ** -0.5) * jnp.exp(b)
        kt = k * jnp.exp(-b)
        kd = k * jnp.exp(b_last - b)
        ys, new_sts = [], []
        for h in range(nh):
            ks, vs = slice(h * GLA_DK, (h + 1) * GLA_DK), slice(h * GLA_DV, (h + 1) * GLA_DV)
            att = jnp.where(tri, _bdot_nt(qt[:, ks], kt[:, ks]), 0.0)
            o = _bdot(att, v[:, vs]) + _bdot_nt(qt[:, ks], sts[h])
            new_sts.append(jnp.exp(b_last[:, ks]) * sts[h] + _bdot_tn(v[:, vs], kd[:, ks]))
            ys.append(_rms(o, g_o) * _silu(gate[:, vs]))
        return [jnp.concatenate(ys, axis=1)], new_sts

    ins = [_rows(q, c, gdtype=BF16), _rows(k, c, gdtype=BF16), _rows(v, c, gdtype=BF16), _rows(gate, c, gdtype=BF16), _rows(la, c), _whole(g_o)]
    outs = [_rows_out(s, nh * GLA_DV, c, BF16)]
    return Op("gla_core", f, (1, s // c), ins, outs, [(GLA_DV, GLA_DK)] * nh)


LRU_T = 256


def _lru_op(gate, u, conv_w, conv_b, w_a, b_a, w_x, b_x, lam):
    s, w = u.shape
    t = LRU_T

    def f(g, vals, sts):
        gate, u, cw, cb, w_a, b_a, w_x, b_x, lam = vals
        u_prev, h_prev = sts
        uc = cb
        for kk in range(CONV_W):
            uc = uc + cw[kk] * _shift_rows(u, u_prev, CONV_W - 1 - kk)
        ra, ri = [], []
        for n in range(LRU_BLOCKS):
            blk = uc[:, n * LRU_BLOCK:(n + 1) * LRU_BLOCK]
            ra.append(_bdot(blk, w_a[n]))
            ri.append(_bdot(blk, w_x[n]))
        r = jax.nn.sigmoid(jnp.concatenate(ra, axis=1) + b_a)
        i = jax.nn.sigmoid(jnp.concatenate(ri, axis=1) + b_x)
        log_a = -LRU_C * r * jax.nn.softplus(-lam)
        a = jnp.exp(log_a)
        bb = jnp.sqrt(_one_minus_exp(2.0 * log_a)) * (i * uc)
        zero = jnp.zeros_like(a)
        sh = 1
        while sh < t:
            a_s = _shift_rows(a - 1.0, zero, sh) + 1.0
            b_s = _shift_rows(bb, zero, sh)
            bb = a * b_s + bb
            a = a * a_s
            sh *= 2
        hs = bb + a * h_prev
        last = (lax.broadcasted_iota(jnp.int32, hs.shape, 0) == t - 1).astype(F32)
        h_last = jnp.sum(hs * last, axis=0, keepdims=True)
        return [hs * _silu(gate)], [u, h_last]

    ins = [_rows(gate, t, gdtype=BF16), _rows(u, t, gdtype=BF16), _whole(conv_w), _whole(conv_b), _whole(w_a), _whole(b_a), _whole(w_x),
           _whole(b_x), _whole(lam)]
    return Op("lru_core", f, (1, s // t), ins, [_rows_out(s, w, t, BF16)], [(t, w), (1, w)])


def _ssd_conv_op(xbc, conv_w, conv_b):
    s, w = xbc.shape
    t = CONV_T
    n_x, n_b = SSD_INNER, SSD_GROUPS * SSD_STATE

    def f(g, vals, sts):
        xbc, cw, cb = vals
        acc = cb
        for kk in range(CONV_W):
            acc = acc + cw[kk] * _shift_rows(xbc, sts[0], CONV_W - 1 - kk)
        y = _silu(acc)
        return [y[:, :n_x], y[:, n_x:n_x + n_b], y[:, n_x + n_b:]], [xbc]

    ins = [_rows(xbc, t, gdtype=BF16), _whole(conv_w), _whole(conv_b)]
    outs = [_rows_out(s, n_x, t, F32), _rows_out(s, n_b, t, BF16), _rows_out(s, n_b, t, BF16)]
    return Op("ssd_conv", f, (1, s // t), ins, outs, [(t, w)])


SSD_L = 512


def _ssd_core_op(x, bm, cm, z, dt, dt_bias, a_log, d_skip, g_norm):
    s = x.shape[0]
    c, hg, p = SSD_L, SSD_HPG, SSD_P
    gw = hg * p

    def f(g, vals, sts):
        x, bm, cm, z, dtr, dt_bias, a_log, d_skip, g_norm = vals
        tri = _tri(c)
        dt = jax.nn.softplus(dtr + dt_bias)
        da = dt * (-jnp.exp(a_log))
        cs = _cumsum_rows(da)
        cs_last = jnp.sum(da, axis=0, keepdims=True)
        cs_t = jnp.transpose(jnp.concatenate([cs, jnp.zeros((c, LANE - hg), F32)], axis=1))
        cb = _bdot_nt(cm, bm)
        ys, new_st = [], []
        for h in range(hg):
            cs_h = cs[:, h:h + 1]
            cs_row = cs_t[h:h + 1, :]
            seg = jnp.where(tri, cs_h - cs_row, 0.0)
            lmat = jnp.where(tri, jnp.exp(seg), 0.0)
            x_h = x[:, h * p:(h + 1) * p]
            xdt = x_h * dt[:, h:h + 1]
            y_diag = _bdot(cb * lmat, xdt)
            decay = jnp.exp(cs_last[:, h:h + 1] - cs_h)
            states = _bdot_tn(xdt * decay, bm)
            y_off = _bdot_nt(cm, sts[h]) * jnp.exp(cs_h)
            new_st.append(jnp.exp(cs_last[:, h:h + 1]) * sts[h] + states)
            ys.append(y_diag + y_off + d_skip[:, h:h + 1] * x_h)
        y = jnp.concatenate(ys, axis=1) * _silu(z)
        return [_rms(y, g_norm)], new_st

    ins = [In(x, (c, gw), lambda h, g: (g, h), "x"), In(bm, (c, SSD_STATE), lambda h, g: (g, h), "x", gdtype=BF16),
           In(cm, (c, SSD_STATE), lambda h, g: (g, h), "x", gdtype=BF16), In(z, (c, gw), lambda h, g: (g, h), "x", gdtype=BF16),
           In(dt, (None, c, hg), lambda h, g: (h, g, 0), "x"),
           In(dt_bias, (None, 1, hg), lambda h, g: (h, 0, 0), "p", per_h=True),
           In(a_log, (None, 1, hg), lambda h, g: (h, 0, 0), "p", per_h=True),
           In(d_skip, (None, 1, hg), lambda h, g: (h, 0, 0), "p", per_h=True),
           In(g_norm, (1, gw), lambda h, g: (0, h), "p", per_h=True)]
    outs = [Out((s, SSD_INNER), BF16, (c, gw), lambda h, g: (g, h))]
    return Op("ssd_core", f, (SSD_GROUPS, s // c), ins, outs, [(p, SSD_STATE)] * hg)


def _loss_op(h, target, final_g):
    s, n = h.shape
    t = ROW_T
    n_g = s // t

    def body(h_ref, t_ref, g_ref, loss_ref, dh_ref, dg_ref):
        step = pl.program_id(0)

        def lossf(hv, gv):
            err = _rms(hv, gv) - t_ref[...]
            return 0.5 * jnp.sum(jnp.mean(err * err, axis=-1))

        l, (dh, dg) = jax.value_and_grad(lossf, argnums=(0, 1))(h_ref[...], g_ref[...])
        dh_ref[...] = dh

        @pl.when(step == 0)
        def _():
            loss_ref[...] = jnp.zeros(loss_ref.shape, F32)
            dg_ref[...] = jnp.zeros(dg_ref.shape, F32)

        loss_ref[...] += jnp.full(loss_ref.shape, l, F32)
        dg_ref[...] += dg

    row = pl.BlockSpec((t, n), lambda g: (g, 0))
    one = pl.BlockSpec((1, n), lambda g: (0, 0))
    return pl.pallas_call(
        body, name="loss_head", grid=(n_g,), in_specs=[row, row, one],
        out_specs=[pl.BlockSpec((1, LANE), lambda g: (0, 0)), row, one],
        out_shape=[jax.ShapeDtypeStruct((1, LANE), F32), jax.ShapeDtypeStruct((s, n), F32), jax.ShapeDtypeStruct((1, n), F32)],
        compiler_params=_CP(dimension_semantics=("arbitrary",)),
    )(h, target, final_g.reshape(1, n))


def _pad_cols(w, n):
    return jnp.pad(w, ((0, 0), (0, n - w.shape[1])))


def _pad_rows(w, n):
    return jnp.pad(w, ((0, n - w.shape[0]), (0, 0)))


def _proj_bwd(tag, u, dps, ws):
    du = _mm_pieces_t(f"{tag}_du", dps, ws)
    if sum(dp.shape[1] for dp in dps) <= MM_DW_MERGE_COLS:
        dws = _mm_pieces_dw(f"{tag}_dw", u, dps)
    else:
        dws = [_mm(f"{tag}_dw{i}", u, dp, ta=True, out_dtype=BF16) for i, dp in enumerate(dps)]
    return du, dws


def _mla_layer(h, norm_g, w, cos, sin, fwd_comm=None, late_w_out=None):
    bf = lambda a: a.astype(BF16)
    w_in, w_uq, w_ukv = w["mla_w_in"], w["mla_w_uq"], w["mla_w_ukv"]
    a0, a1, a2 = MLA_Q_RANK, MLA_Q_RANK + MLA_KV_RANK, MLA_Q_RANK + MLA_KV_RANK + MLA_ROPE
    w_cq, w_ckv, w_kr, w_g = bf(w_in[:, :a0]), bf(w_in[:, a0:a1]), bf(_pad_cols(w_in[:, a1:a2], LANE)), bf(w_in[:, a2:])
    uq = w_uq.reshape(MLA_Q_RANK, MLA_HEADS, MLA_QK)
    half = MLA_ROPE // 2
    w_qn = bf(uq[:, :, :MLA_NOPE].reshape(MLA_Q_RANK, -1))
    w_q1 = bf(uq[:, :, MLA_NOPE:MLA_NOPE + half].reshape(MLA_Q_RANK, -1))
    w_q2 = bf(uq[:, :, MLA_NOPE + half:].reshape(MLA_Q_RANK, -1))
    ukv = w_ukv.reshape(MLA_KV_RANK, MLA_HEADS, MLA_NOPE + MLA_V)
    w_kn = bf(ukv[:, :, :MLA_NOPE].reshape(MLA_KV_RANK, -1))
    w_v = bf(ukv[:, :, MLA_NOPE:].reshape(MLA_KV_RANK, -1))

    n0 = _rms_op("mla_norm", h, norm_g)
    u, = n0.fwd()
    cq, ckv, kr, gate = _mm_pieces("mla_in", u, (w_cq, w_ckv, w_kr, w_g))
    nq = _rms_op("mla_qnorm", cq, w["mla_g_q"], gdtype=BF16)
    nkv = _rms_op("mla_kvnorm", ckv, w["mla_g_kv"], gdtype=BF16)
    qn_, = nq.fwd()
    kvn_, = nkv.fwd()
    qn, q1, q2 = _mm_pieces("mla_uq", qn_, (w_qn, w_q1, w_q2))
    kn, v = _mm_pieces("mla_ukv", kvn_, (w_kn, w_v))
    prep = _mla_prep_op(qn, q1, q2, kn, kr, v, cos, sin)
    qh, kh, vh = prep.fwd()
    attn = _mla_attn_op(qh, kh, vh)
    o, = attn.fwd(fwd_comm)
    w_out = bf(w["mla_w_out"]) if late_w_out is None else late_w_out(attn.fwd_comm_out)
    post = _mla_post_op(o, gate)
    y, = post.fwd()
    h_out = _mm("mla_out", y, w_out, add=h)

    def bwd(dh, make_comm=None):
        dy = _mm("mla_out_dy", dh, w_out, tb=True, out_dtype=BF16)
        d_w_out = _mm("mla_out_dw", y, dh, ta=True, out_dtype=BF16)
        do, dgate = post.bwd([dy])
        dqh, dkh, dvh = attn.bwd([do], comm=None if make_comm is None else make_comm(d_w_out))
        dqn, dq1, dq2, dkn, dkr, dv = prep.bwd([dqh, dkh, dvh])
        dqn_, d_uq = _proj_bwd("mla_uq", qn_, (dqn, dq1, dq2), (w_qn, w_q1, w_q2))
        dkvn_, d_ukv = _proj_bwd("mla_ukv", kvn_, (dkn, dv), (w_kn, w_v))
        dcq, d_g_q = nq.bwd([dqn_])
        dckv, d_g_kv = nkv.bwd([dkvn_])
        du, d_in = _proj_bwd("mla_in", u, (dcq, dckv, dkr, dgate), (w_cq, w_ckv, w_kr, w_g))
        dh_in, d_norm = n0.bwd([du], addto={0: dh})
        shp = (MLA_Q_RANK, MLA_HEADS, -1)
        g_uq = jnp.concatenate([d_uq[0].reshape(shp), d_uq[1].reshape(shp), d_uq[2].reshape(shp)], axis=2).reshape(MLA_Q_RANK, -1)
        shp = (MLA_KV_RANK, MLA_HEADS, -1)
        g_ukv = jnp.concatenate([d_ukv[0].reshape(shp), d_ukv[1].reshape(shp)], axis=2).reshape(MLA_KV_RANK, -1)
        g_in = jnp.concatenate([d_in[0], d_in[1], d_in[2][:, :MLA_ROPE], d_in[3]], axis=1)
        return dh_in, d_norm, {"mla_w_in": g_in, "mla_g_q": d_g_q.reshape(-1), "mla_w_uq": g_uq, "mla_g_kv": d_g_kv.reshape(-1),
                               "mla_w_ukv": g_ukv, "mla_w_out": d_w_out}, attn.bwd_comm_out

    return h_out, bwd, attn.fwd_comm_out


def _gla_layer(h, norm_g, w, fwd_comm=None):
    bf = lambda a: a.astype(BF16)
    w_in = w["gla_w_in"]
    nk, nv = GLA_HEADS * GLA_DK, GLA_HEADS * GLA_DV
    cuts = (0, nk, 2 * nk, 2 * nk + nv, 2 * nk + 2 * nv)
    w_q, w_k, w_v, w_g = (bf(w_in[:, cuts[i]:cuts[i + 1]]) for i in range(4))
    w_gk = bf(_pad_cols(w_in[:, cuts[4]:], LANE))
    w2 = _pad_rows(w["gla_w_gk2"], LANE)
    b_gk = w["gla_b_gk"].reshape(1, -1)
    g_o = w["gla_g_o"].reshape(1, -1)
    w_out = bf(w["gla_w_out"])

    n0 = _rms_op("gla_norm", h, norm_g)
    u, = n0.fwd()
    q, k, v, gate, gk = _mm_pieces("gla_in", u, (w_q, w_k, w_v, w_g, w_gk))
    gop = _gla_gate_op(gk, w2, b_gk)
    la, = gop.fwd()
    core = _gla_core_op(q, k, v, gate, la, g_o)
    y, = core.fwd(fwd_comm)
    h_out = _mm("gla_out", y, w_out, add=h)

    def bwd(dh, make_comm=None):
        dy = _mm("gla_out_dy", dh, w_out, tb=True, out_dtype=BF16)
        d_w_out = _mm("gla_out_dw", y, dh, ta=True, out_dtype=BF16)
        dq, dk, dv, dgate, dla, d_g_o = core.bwd([dy], comm=None if make_comm is None else make_comm(d_w_out))
        dgk, d_w2, d_b = gop.bwd([dla])
        du, d_in = _proj_bwd("gla_in", u, (dq, dk, dv, dgate, dgk), (w_q, w_k, w_v, w_g, w_gk))
        dh_in, d_norm = n0.bwd([du], addto={0: dh})
        g_in = jnp.concatenate([d_in[0], d_in[1], d_in[2], d_in[3], d_in[4][:, :GLA_RANK]], axis=1)
        return dh_in, d_norm, {"gla_w_in": g_in, "gla_w_gk2": d_w2[:GLA_RANK], "gla_b_gk": d_b.reshape(-1), "gla_g_o": d_g_o.reshape(-1),
                               "gla_w_out": d_w_out}, core.bwd_comm_out

    return h_out, bwd, core.fwd_comm_out


def _lru_layer(h, norm_g, w, fwd_comm=None):
    bf = lambda a: a.astype(BF16)
    w_in = w["lru_w_in"]
    w_g, w_u = bf(w_in[:, :LRU_WIDTH]), bf(w_in[:, LRU_WIDTH:])
    row = lambda a: a.reshape(1, -1)
    w_out = bf(w["lru_w_out"])

    n0 = _rms_op("lru_norm", h, norm_g)
    u_, = n0.fwd()
    gate, u = _mm_pieces("lru_in", u_, (w_g, w_u))
    core = _lru_op(gate, u, w["lru_conv_w"].reshape(CONV_W, 1, -1), row(w["lru_conv_b"]), w["lru_w_a"], row(w["lru_b_a"]), w["lru_w_x"],
                   row(w["lru_b_x"]), row(w["lru_lam"]))
    y, = core.fwd(fwd_comm)
    h_out = _mm("lru_out", y, w_out, add=h)

    def bwd(dh, make_comm=None):
        dy = _mm("lru_out_dy", dh, w_out, tb=True, out_dtype=BF16)
        d_w_out = _mm("lru_out_dw", y, dh, ta=True, out_dtype=BF16)
        dgate, du, d_cw, d_cb, d_wa, d_ba, d_wx, d_bx, d_lam = core.bwd([dy], comm=None if make_comm is None else make_comm(d_w_out))
        du_, d_in = _proj_bwd("lru_in", u_, (dgate, du), (w_g, w_u))
        dh_in, d_norm = n0.bwd([du_], addto={0: dh})
        return dh_in, d_norm, {"lru_w_in": jnp.concatenate(d_in, axis=1), "lru_conv_w": d_cw.reshape(CONV_W, -1), "lru_conv_b": d_cb.reshape(-1),
                               "lru_w_a": d_wa, "lru_b_a": d_ba.reshape(-1), "lru_w_x": d_wx, "lru_b_x": d_bx.reshape(-1),
                               "lru_lam": d_lam.reshape(-1), "lru_w_out": d_w_out}, core.bwd_comm_out

    return h_out, bwd, core.fwd_comm_out


def _ssd_layer(h, norm_g, w, fwd_comm=None, late_w_out=None):
    bf = lambda a: a.astype(BF16)
    s = h.shape[0]
    w_in = w["ssd_w_in"]
    conv_dim = SSD_INNER + 2 * SSD_GROUPS * SSD_STATE
    w_z, w_xbc = bf(w_in[:, :SSD_INNER]), bf(w_in[:, SSD_INNER:SSD_INNER + conv_dim])
    w_dt = bf(_pad_cols(w_in[:, SSD_INNER + conv_dim:], LANE))
    grp = lambda a: a.reshape(SSD_GROUPS, 1, SSD_HPG)

    n0 = _rms_op("ssd_norm", h, norm_g)
    u, = n0.fwd()
    z, xbc, dtp = _mm_pieces("ssd_in", u, (w_z, w_xbc, w_dt))
    conv = _ssd_conv_op(xbc, w["ssd_conv_w"].reshape(CONV_W, 1, -1), w["ssd_conv_b"].reshape(1, -1))
    x, bm, cm = conv.fwd()
    dt = dtp[:, :SSD_HEADS].reshape(s, SSD_GROUPS, SSD_HPG).transpose(1, 0, 2)
    core = _ssd_core_op(x, bm, cm, z, dt, grp(w["ssd_dt_bias"]), grp(w["ssd_a_log"]), grp(w["ssd_d"]), w["ssd_g_norm"].reshape(1, -1))
    y, = core.fwd(fwd_comm)
    w_out = bf(w["ssd_w_out"]) if late_w_out is None else late_w_out(core.fwd_comm_out)
    h_out = _mm("ssd_out", y, w_out, add=h)

    def bwd(dh, make_comm=None):
        dy = _mm("ssd_out_dy", dh, w_out, tb=True, out_dtype=BF16)
        d_w_out = _mm("ssd_out_dw", y, dh, ta=True, out_dtype=BF16)
        dx, dbm, dcm, dz, ddt, d_dtb, d_alog, d_d, d_gn = core.bwd([dy], comm=None if make_comm is None else make_comm(d_w_out))
        dxbc, d_cw, d_cb = conv.bwd([dx, dbm, dcm])
        ddtp = _pad_cols(ddt.transpose(1, 0, 2).reshape(s, SSD_HEADS), LANE).astype(BF16)
        du, d_in = _proj_bwd("ssd_in", u, (dz, dxbc, ddtp), (w_z, w_xbc, w_dt))
        dh_in, d_norm = n0.bwd([du], addto={0: dh})
        g_in = jnp.concatenate([d_in[0], d_in[1], d_in[2][:, :SSD_HEADS]], axis=1)
        return dh_in, d_norm, {"ssd_w_in": g_in, "ssd_conv_w": d_cw.reshape(CONV_W, -1), "ssd_conv_b": d_cb.reshape(-1),
                               "ssd_dt_bias": d_dtb.reshape(-1), "ssd_a_log": d_alog.reshape(-1), "ssd_d": d_d.reshape(-1),
                               "ssd_g_norm": d_gn.reshape(-1), "ssd_w_out": d_w_out}, core.bwd_comm_out

    return h_out, bwd


def _rope_tables(positions):
    inv_freq = ROPE_THETA ** (-jnp.arange(0, MLA_ROPE, 2, dtype=F32) / MLA_ROPE)
    ang = positions.astype(F32)[:, None] * inv_freq
    return jnp.cos(ang), jnp.sin(ang)


WEIGHTS = ["norm_g", "final_g", "mla_w_in", "mla_g_q", "mla_w_uq", "mla_g_kv", "mla_w_ukv", "mla_w_out", "gla_w_in", "gla_w_gk2", "gla_b_gk",
           "gla_g_o", "gla_w_out", "lru_w_in", "lru_conv_w", "lru_conv_b", "lru_w_a", "lru_b_a", "lru_w_x", "lru_b_x", "lru_lam", "lru_w_out",
           "ssd_w_in", "ssd_conv_w", "ssd_conv_b", "ssd_dt_bias", "ssd_a_log", "ssd_d", "ssd_g_norm", "ssd_w_out"]
BIG = ["mla_w_in", "mla_w_uq", "mla_w_ukv", "mla_w_out", "gla_w_in", "gla_w_out", "lru_w_in", "lru_w_out", "ssd_w_in", "ssd_w_out"]
SMALL = ["gla_w_gk2", "gla_b_gk", "gla_g_o", "lru_conv_w", "lru_conv_b", "lru_b_a", "lru_b_x", "lru_lam", "ssd_conv_w", "ssd_conv_b", "ssd_g_norm"]
REPL = ["norm_g", "final_g", "mla_g_q", "mla_g_kv", "lru_w_a", "lru_w_x", "ssd_dt_bias", "ssd_a_log", "ssd_d"]
REPL_EARLY = ["lru_w_a", "lru_w_x"]
REPL_LATE = [n for n in REPL if n not in REPL_EARLY]
N_CHIPS, N_DEV = 4, 8
PACK_W = 1024
ADAM_ROWS = 256
SMALL_ROWS = 64


def _shard_axis(name):
    return 0 if name.endswith("_w_out") else -1


def _pack(arrs, dtype, row_mult):
    flat = jnp.concatenate([a.reshape(-1).astype(dtype) for a in arrs])
    per = PACK_W * row_mult
    total = -(-flat.shape[0] // per) * per
    return jnp.pad(flat, (0, total - flat.shape[0])).reshape(-1, PACK_W)


def _unpack(buf, shapes):
    flat = buf.reshape(-1)
    out, off = [], 0
    for s in shapes:
        n = math.prod(s)
        out.append(flat[off:off + n].reshape(s))
        off += n
    return out


def _mesh_pos():
    return lax.axis_index("x"), lax.axis_index("y"), lax.axis_index("c")


class GatherComm:
    def __init__(self, ops):
        self.ops = list(ops)
        n = len(self.ops)
        assert all(o.ndim == 2 and o.shape[0] % 32 == 0 for o in self.ops), [o.shape for o in self.ops]
        self.out_shapes = [jax.ShapeDtypeStruct((N_CHIPS,) + o.shape, o.dtype) for o in self.ops]
        self.sem_shapes = [pltpu.SemaphoreType.DMA((6 * n,)), pltpu.SemaphoreType.DMA((6 * n,)), pltpu.SemaphoreType.DMA((n,))]

    def _copies(self, srcs, dsts, sems):
        send_sems, recv_sems, local_sems = sems
        n = len(self.ops)
        x, y, c = _mesh_pos()
        me_id, sibling = (x, y, c), (x, y, 1 - c)
        chips = [(1 - x, y), (x, 1 - y), (1 - x, 1 - y)]
        mine = 2 * x + y

        def half(i, cc):
            h = self.ops[i].shape[0] // 2
            return pl.ds(cc * h, h)

        def copy(i, k, src, slot, cc, to):
            return pltpu.make_async_remote_copy(src_ref=src, dst_ref=dsts[i].at[slot, half(i, cc)], send_sem=send_sems.at[i * 6 + k],
                                                recv_sem=recv_sems.at[i * 6 + k], device_id=to, device_id_type=pl.DeviceIdType.MESH)

        local = [pltpu.make_async_copy(srcs[i], dsts[i].at[mine], local_sems.at[i]) for i in range(n)]
        first, ici_recvs, passed, sib_recvs = [], [], [], []
        for i in range(n):
            my_half = srcs[i].at[half(i, c)]
            for k, (px, py) in enumerate(chips):
                slot = 2 * px + py
                first.append(copy(i, k, my_half, mine, c, (px, py, c)))
                ici_recvs.append(copy(i, k, my_half, slot, c, me_id))
                passed.append(copy(i, 3 + k, dsts[i].at[slot, half(i, c)], slot, c, sibling))
                sib_recvs.append(copy(i, 3 + k, my_half, slot, 1 - c, me_id))
        return local, first, ici_recvs, passed, sib_recvs

    def start(self, srcs, dsts, sems):
        local, first, _, _, _ = self._copies(srcs, dsts, sems)
        for cp in local + first:
            cp.start()

    def forward(self, srcs, dsts, sems):
        _, _, ici_recvs, passed, _ = self._copies(srcs, dsts, sems)
        for rc, fw in zip(ici_recvs, passed):
            rc.wait_recv()
            fw.start()

    def finish(self, srcs, dsts, sems):
        local, first, _, passed, sib_recvs = self._copies(srcs, dsts, sems)
        for cp in sib_recvs:
            cp.wait_recv()
        for cp in first + passed:
            cp.wait_send()
        for cp in local:
            cp.wait()


class ExchangeComm:
    def __init__(self, chip_ops, all_ops=()):
        self.ops = list(chip_ops) + list(all_ops)
        self.per_chip = (True,) * len(chip_ops) + (False,) * len(all_ops)
        n = len(self.ops)
        self.out_shapes = [jax.ShapeDtypeStruct((N_DEV,) + o.shape[-2:], o.dtype) for o in self.ops]
        self.sem_shapes = [pltpu.SemaphoreType.DMA((7 * n,)), pltpu.SemaphoreType.DMA((7 * n,)), pltpu.SemaphoreType.DMA((n,))]

    def _copies(self, srcs, dsts, sems):
        send_sems, recv_sems, local_sems = sems
        n, per_chip = len(self.ops), self.per_chip
        x, y, c = _mesh_pos()
        me_id, sibling = (x, y, c), (x, y, 1 - c)
        chips = [(1 - x, y), (x, 1 - y), (1 - x, 1 - y)]

        def dev(px, py, pc):
            return 4 * px + 2 * py + pc

        def part(i, px, py):
            return srcs[i].at[2 * px + py] if per_chip[i] else srcs[i]

        def copy(i, k, src, slot, to):
            return pltpu.make_async_remote_copy(src_ref=src, dst_ref=dsts[i].at[slot], send_sem=send_sems.at[i * 7 + k],
                                                recv_sem=recv_sems.at[i * 7 + k], device_id=to, device_id_type=pl.DeviceIdType.MESH)

        me = dev(x, y, c)
        local = [pltpu.make_async_copy(part(i, x, y), dsts[i].at[me], local_sems.at[i]) for i in range(n)]
        first, ici_recvs, passed, sib_recvs = [], [], [], []
        for i in range(n):
            first.append(copy(i, 0, part(i, x, y), me, sibling))
            first += [copy(i, 1 + k, part(i, px, py), me, (px, py, c)) for k, (px, py) in enumerate(chips)]
            sib_recvs.append(copy(i, 0, part(i, x, y), dev(x, y, 1 - c), me_id))
            for k, (px, py) in enumerate(chips):
                slot = dev(px, py, c)
                ici_recvs.append(copy(i, 1 + k, part(i, x, y), slot, me_id))
                passed.append(copy(i, 4 + k, dsts[i].at[slot], slot, sibling))
                sib_recvs.append(copy(i, 4 + k, part(i, x, y), dev(px, py, 1 - c), me_id))
        return local, first, ici_recvs, passed, sib_recvs

    def start(self, srcs, dsts, sems):
        local, first, _, _, _ = self._copies(srcs, dsts, sems)
        for cp in local + first:
            cp.start()

    def forward(self, srcs, dsts, sems):
        _, _, ici_recvs, passed, _ = self._copies(srcs, dsts, sems)
        for rc, fw in zip(ici_recvs, passed):
            rc.wait_recv()
            fw.start()

    def finish(self, srcs, dsts, sems):
        local, first, _, passed, sib_recvs = self._copies(srcs, dsts, sems)
        for cp in sib_recvs:
            cp.wait_recv()
        for cp in first + passed:
            cp.wait_send()
        for cp in local:
            cp.wait()


def _run_comm(name, comm):
    n = len(comm.ops)

    def body(*refs):
        srcs, dsts, sems = refs[:n], refs[n:2 * n], refs[2 * n:]
        comm.start(srcs, dsts, sems)
        comm.forward(srcs, dsts, sems)
        comm.finish(srcs, dsts, sems)

    any_spec = pl.BlockSpec(memory_space=pl.ANY)
    return pl.pallas_call(body, name=name, in_specs=[any_spec] * n, out_specs=[any_spec] * n, out_shape=comm.out_shapes,
                          scratch_shapes=comm.sem_shapes)(*comm.ops)


def _adamw(name, parts, w, m, v, lead=False, comm=None):
    plist = list(parts) if isinstance(parts, (list, tuple)) else [parts]
    n_p = len(plist)
    rows, cols = w.shape[-2:]
    t = next(c for c in (ADAM_ROWS, ADAM_ROWS // 2, SMALL_ROWS) if all(p.shape[1] % c == 0 for p in plist))
    starts = [sum(p.shape[1] for p in plist[:k]) // t for k in range(n_p)]
    counts = [p.shape[1] // t for p in plist]
    assert sum(p.shape[1] for p in plist) == rows, (name, rows)
    c1 = 1.0 - ADAM_B1 ** ADAM_STEP
    c2 = 1.0 - ADAM_B2 ** ADAM_STEP

    n_c = len(comm.ops) if comm is not None else 0
    n_steps = rows // t

    def body(*refs):
        p_refs = refs[:n_p]
        w_ref, m_ref, v_ref = refs[n_p:n_p + 3]
        c_src = refs[n_p + 3:n_p + 3 + n_c]
        g_ref, d_ref, nm_ref, nv_ref = refs[n_p + 3 + n_c:n_p + 7 + n_c]
        cargs = (c_src, refs[n_p + 7 + n_c:n_p + 7 + 2 * n_c], refs[n_p + 7 + 2 * n_c:])
        if comm is not None:
            @pl.when(pl.program_id(0) == 0)
            def _():
                comm.start(*cargs)

            @pl.when(pl.program_id(0) == _forward_step(n_steps))
            def _():
                comm.forward(*cargs)
        g = None
        for k, p_ref in enumerate(p_refs):
            gk = p_ref[0].astype(F32)
            for d in range(1, N_DEV):
                gk = gk + p_ref[d].astype(F32)
            g = gk if g is None else jnp.where(pl.program_id(0) >= starts[k], gk, g)
        nm = ADAM_B1 * m_ref[...] + (1.0 - ADAM_B1) * g
        nv = ADAM_B2 * v_ref[...] + (1.0 - ADAM_B2) * (g * g)
        g_ref[...] = g
        nm_ref[...] = nm
        nv_ref[...] = nv
        d_ref[...] = -ADAM_LR * ((nm / c1) / (jnp.sqrt(nv / c2) + ADAM_EPS) + ADAM_WD * w_ref[...])
        if comm is not None:
            @pl.when(pl.program_id(0) == n_steps - 1)
            def _():
                comm.finish(*cargs)

    row = pl.BlockSpec((None, t, cols), lambda i: (0, i, 0)) if lead else pl.BlockSpec((t, cols), lambda i: (i, 0))
    c_in_specs, c_args, c_out_specs, c_out_shapes, c_sems = _carry_specs(comm)
    res = pl.pallas_call(
        body, name=name, grid=(n_steps,),
        in_specs=[pl.BlockSpec((N_DEV, t, cols), lambda i, lo=lo, n=n: (0, jnp.clip(i - lo, 0, n - 1), 0)) for lo, n in zip(starts, counts)]
        + [row, row, row] + c_in_specs,
        out_specs=[row] * 4 + c_out_specs, out_shape=[jax.ShapeDtypeStruct(w.shape, F32)] * 4 + c_out_shapes, scratch_shapes=c_sems,
        compiler_params=_CP(dimension_semantics=("arbitrary" if comm is not None else "parallel",)),
    )(*plist, w, m, v, *c_args)
    return list(res[:4]), list(res[4:])


def _train_step(x, positions, target, wts, ms, vs, raw):
    small_shapes = [wts[n].shape for n in SMALL]

    big_of = {tag: [n for n in BIG if n.startswith(tag)] for tag in ("mla", "gla", "lru", "ssd")}
    full = {n: wts[n] for n in REPL}

    def gather_comm(names, extra=()):
        return GatherComm([wts[n].astype(BF16) for n in names] + list(extra))

    def assemble(names, got):
        for k, n in enumerate(names):
            full[n] = jnp.concatenate([got[k][j] for j in range(N_CHIPS)], axis=_shard_axis(n))

    first = [n for n in big_of["mla"] if n != "mla_w_out"]
    got = _run_comm("gather_first", gather_comm(first, [_pack([wts[n] for n in SMALL], F32, SMALL_ROWS)]))
    assemble(first, got)
    per_chip_small = [_unpack(got[-1][j], small_shapes) for j in range(N_CHIPS)]
    for k, n in enumerate(SMALL):
        full[n] = jnp.concatenate([per_chip_small[j][k] for j in range(N_CHIPS)], axis=_shard_axis(n))

    cos, sin = _rope_tables(positions)
    ng = full["norm_g"]
    behind_attn = ["mla_w_out"] + big_of["gla"] + big_of["lru"]

    def mla_w_out(got):
        assemble(behind_attn, got)
        return full["mla_w_out"].astype(BF16)

    h1, b0, _ = _mla_layer(x, ng[0], full, cos, sin, fwd_comm=gather_comm(behind_attn), late_w_out=mla_w_out)

    def joined(got_k, axis):
        return jnp.concatenate([got_k[j] for j in range(N_CHIPS)], axis=axis)

    ssd_in = wts["ssd_w_in"].astype(BF16)
    half = ssd_in.shape[0] // 2
    h2, b1, got = _gla_layer(h1, ng[1], full, fwd_comm=GatherComm([ssd_in[:half]]))
    top = joined(got[0], -1)
    h3, b2, got = _lru_layer(h2, ng[2], full, fwd_comm=GatherComm([ssd_in[half:]]))
    full["ssd_w_in"] = jnp.concatenate([top, joined(got[0], -1)], axis=0)
    h4, b3 = _ssd_layer(h3, ng[3], full, fwd_comm=gather_comm(["ssd_w_out"]), late_w_out=lambda got: joined(got[0], 0))
    loss, dh, d_final = _loss_op(h4, target, full["final_g"])
    loss = loss[0, 0]
    grads = {"final_g": d_final.reshape(-1)}
    d_norms = [None] * 4

    def shards_of(n, g):
        return jnp.stack(jnp.split(g.astype(BF16), N_CHIPS, axis=_shard_axis(n)))

    def shards(n):
        return shards_of(n, grads[n])

    parts = {}
    dh, d_norms[3], gw, got = b3(dh, make_comm=lambda dw: ExchangeComm([shards_of("ssd_w_out", dw)]))
    parts["ssd_w_out"] = got[0]
    grads.update(gw)
    ssd_in_g = shards("ssd_w_in")
    half = ssd_in_g.shape[1] // 2
    dh, d_norms[2], gw, got = b2(dh, make_comm=lambda dw: ExchangeComm([ssd_in_g[:, :half], shards_of("lru_w_out", dw)]))
    parts["ssd_w_in"], parts["lru_w_out"] = [got[0]], got[1]
    grads.update(gw)
    dh, d_norms[1], gw, got = b1(dh, make_comm=lambda dw: ExchangeComm([ssd_in_g[:, half:]]))
    parts["ssd_w_in"].append(got[0])
    grads.update(gw)
    repl_early = _pack([grads[n] for n in REPL_EARLY], BF16, SMALL_ROWS)

    behind_attn = ["gla_w_out", "lru_w_in", "gla_w_in"]
    dx, d_norms[0], gw, got = b0(dh, make_comm=lambda dw: ExchangeComm([shards_of("mla_w_out", dw)] + [shards(n) for n in behind_attn],
                                                                        [repl_early]))
    parts.update(zip(["mla_w_out"] + behind_attn, got))
    repl_early_parts = got[-1]
    grads.update(gw)
    grads["norm_g"] = jnp.concatenate(d_norms, axis=0)
    psmall = jnp.stack([_pack([jnp.split(grads[n], N_CHIPS, axis=_shard_axis(n))[j] for n in SMALL], F32, SMALL_ROWS) for j in range(N_CHIPS)])
    prepl = _pack([grads[n] for n in REPL_LATE], F32, SMALL_ROWS)

    out = {}
    kinds = ("grad", "delta", "new_m", "new_v")
    late = [n for n in big_of["mla"] if n != "mla_w_out"]
    late_comm = ExchangeComm([shards(n) for n in late] + [psmall], [prepl])
    for n in ["ssd_w_in"] + [n for n in BIG if n != "ssd_w_in"]:
        if n == "ssd_w_in":
            res, late_parts = _adamw("adam_" + n, parts[n], *(r[n] for r in raw), lead=True, comm=late_comm)
            parts.update(zip(late, late_parts))
        else:
            res, _ = _adamw("adam_" + n, parts[n], *(r[n] for r in raw), lead=True)
        for kind, a in zip(kinds, res):
            out[kind, n] = a
    for tag, names, p in (("adam_small", SMALL, late_parts[-2]), ("adam_repl_early", REPL_EARLY, repl_early_parts),
                          ("adam_repl_late", REPL_LATE, late_parts[-1])):
        shapes = [wts[n].shape for n in names]
        packed = [_pack([d[n] for n in names], F32, SMALL_ROWS) for d in (wts, ms, vs)]
        for kind, buf in zip(kinds, _adamw(tag, p, *packed)[0]):
            for n, a in zip(names, _unpack(buf, shapes)):
                out[kind, n] = a
    loss = lax.psum(loss, ("x", "y", "c"))
    return loss, dx, out


def kernel(x, positions, norm_g, final_g, mla_w_in, mla_g_q, mla_w_uq, mla_g_kv, mla_w_ukv, mla_w_out, gla_w_in, gla_w_gk2, gla_b_gk, gla_g_o, gla_w_out, lru_w_in, lru_conv_w, lru_conv_b, lru_w_a, lru_b_a, lru_w_x, lru_b_x, lru_lam, lru_w_out, ssd_w_in, ssd_conv_w, ssd_conv_b, ssd_dt_bias, ssd_a_log, ssd_d, ssd_g_norm, ssd_w_out, loss_target, m_norm_g, m_final_g, m_mla_w_in, m_mla_g_q, m_mla_w_uq, m_mla_g_kv, m_mla_w_ukv, m_mla_w_out, m_gla_w_in, m_gla_w_gk2, m_gla_b_gk, m_gla_g_o, m_gla_w_out, m_lru_w_in, m_lru_conv_w, m_lru_conv_b, m_lru_w_a, m_lru_b_a, m_lru_w_x, m_lru_b_x, m_lru_lam, m_lru_w_out, m_ssd_w_in, m_ssd_conv_w, m_ssd_conv_b, m_ssd_dt_bias, m_ssd_a_log, m_ssd_d, m_ssd_g_norm, m_ssd_w_out, v_norm_g, v_final_g, v_mla_w_in, v_mla_g_q, v_mla_w_uq, v_mla_g_kv, v_mla_w_ukv, v_mla_w_out, v_gla_w_in, v_gla_w_gk2, v_gla_b_gk, v_gla_g_o, v_gla_w_out, v_lru_w_in, v_lru_conv_w, v_lru_conv_b, v_lru_w_a, v_lru_b_a, v_lru_w_x, v_lru_b_x, v_lru_lam, v_lru_w_out, v_ssd_w_in, v_ssd_conv_w, v_ssd_conv_b, v_ssd_dt_bias, v_ssd_a_log, v_ssd_d, v_ssd_g_norm, v_ssd_w_out):
    given = dict(locals())
    stacked = [n for n in WEIGHTS if n not in ("norm_g", "final_g")]

    def blocks(prefix):
        return {n: (given[prefix + n][0] if n in stacked else given[prefix + n]) for n in WEIGHTS}

    raw = [{n: given[prefix + n] for n in BIG} for prefix in ("", "m_", "v_")]
    loss, dx, out = _train_step(x[0], positions[0], loss_target[0], blocks(""), blocks("m_"), blocks("v_"), raw)
    res = [loss, dx[None]]
    for kind in ("grad", "delta", "new_m", "new_v"):
        res += [(out[kind, n][None] if n in stacked and n not in BIG else out[kind, n]) for n in WEIGHTS]
    return tuple(res)
```

```python
import functools
import math

import jax
import jax.numpy as jnp
from jax import lax
from jax.experimental import pallas as pl
from jax.experimental.pallas import tpu as pltpu

F32 = jnp.float32
BF16 = jnp.bfloat16

V7X_VMEM_BYTES = 64 * 1024 * 1024
VMEM_LIMIT = V7X_VMEM_BYTES - 8 * 1024 * 1024
LANE = 128

D_MODEL = 1024
NORM_EPS = 1e-6
MLA_HEADS, MLA_Q_RANK, MLA_KV_RANK, MLA_NOPE, MLA_ROPE, MLA_V = 16, 384, 256, 64, 32, 64
MLA_QK = MLA_NOPE + MLA_ROPE
ROPE_THETA = 10000.0
GLA_HEADS, GLA_DK, GLA_DV, GLA_RANK, GLA_TAU, GLA_CHUNK = 4, 128, 256, 16, 16.0, 64
LRU_WIDTH, LRU_BLOCKS, LRU_BLOCK, LRU_C, CONV_W = 1280, 10, 128, 8.0, 4
SSD_INNER, SSD_P, SSD_HEADS, SSD_GROUPS, SSD_HPG, SSD_STATE, SSD_CHUNK = 2048, 64, 32, 8, 4, 128, 64
ADAM_LR, ADAM_B1, ADAM_B2, ADAM_EPS, ADAM_WD, ADAM_STEP = 0.001, 0.9, 0.999, 1e-08, 0.01, 10

_CP = functools.partial(pltpu.CompilerParams, vmem_limit_bytes=VMEM_LIMIT)


def _bdot(a, b):
    return jnp.dot(a.astype(BF16), b.astype(BF16), preferred_element_type=F32)


def _bdot_nt(a, b):
    return lax.dot_general(a.astype(BF16), b.astype(BF16), (((1,), (1,)), ((), ())), preferred_element_type=F32)


def _bdot_tn(a, b):
    return lax.dot_general(a.astype(BF16), b.astype(BF16), (((0,), (0,)), ((), ())), preferred_element_type=F32)


def _tri(n):
    r = lax.broadcasted_iota(jnp.int32, (n, n), 0)
    c = lax.broadcasted_iota(jnp.int32, (n, n), 1)
    return r >= c


def _rms(x, g):
    return x * lax.rsqrt(jnp.mean(x * x, axis=-1, keepdims=True) + NORM_EPS) * g


def _silu(x):
    return x * jax.nn.sigmoid(x)


def _shift_rows(x, prev, j):
    if j == 0:
        return x
    t = x.shape[0]

    def fwd_impl(x, prev):
        row = lax.broadcasted_iota(jnp.int32, x.shape, 0)
        return jnp.where(row >= j, pltpu.roll(x, j, 0), pltpu.roll(prev, j, 0))

    @jax.custom_vjp
    def sh(x, prev):
        return fwd_impl(x, prev)

    def sh_fwd(x, prev):
        return fwd_impl(x, prev), None

    def sh_bwd(_, gy):
        row = lax.broadcasted_iota(jnp.int32, gy.shape, 0)
        back = pltpu.roll(gy, t - j, 0)
        return jnp.where(row < t - j, back, 0.0), jnp.where(row >= t - j, back, 0.0)

    sh.defvjp(sh_fwd, sh_bwd)
    return sh(x, prev)


def _cumsum_rows(x):
    zero = jnp.zeros_like(x)
    sh = 1
    while sh < x.shape[0]:
        x = x + _shift_rows(x, zero, sh)
        sh *= 2
    return x


def _one_minus_exp(x):
    series = -x * (1.0 + x * (0.5 + x * (1.0 / 6.0 + x * (1.0 / 24.0 + x * (1.0 / 120.0)))))
    return jnp.where(x > -0.05, series, 1.0 - jnp.exp(x))


def _tile(n, cap):
    if n <= cap:
        return n
    best = None
    for t in range(LANE, cap + 1, LANE):
        if n % t == 0:
            best = t
    assert best is not None, (n, cap)
    return best


MM_BLOCK_BYTES = 8 * 1024 * 1024
MM_ROWS, MM_KROWS = 256, 512
MM_TILE_BYTES = 2 * 1024 * 1024


def _mm_tiles(m, k, n, ta):
    if ta:
        return m, _tile(n, max(LANE, MM_BLOCK_BYTES // (4 * m) // LANE * LANE)), _tile(k, MM_KROWS)
    tn = _tile(n, max(LANE, MM_BLOCK_BYTES // (2 * k) // LANE * LANE))
    rows = min(4 * MM_ROWS, max(MM_ROWS, MM_TILE_BYTES // (4 * tn) // MM_ROWS * MM_ROWS))
    return _tile(m, rows), tn, k


def _mm(name, a, b, *, ta=False, tb=False, add=None, out_dtype=F32):
    m, k = (a.shape[1], a.shape[0]) if ta else a.shape
    n, kb = (b.shape[0], b.shape[1]) if tb else (b.shape[1], b.shape[0])
    assert k == kb, (name, a.shape, b.shape, ta, tb)
    tm, tn, tk = _mm_tiles(m, k, n, ta)
    nk = k // tk
    dn = (((0 if ta else 1,), (1 if tb else 0,)), ((), ()))
    has_add = add is not None

    def finish(refs, r):
        if has_add:
            r = r + refs[2][...].astype(F32)
        return r.astype(out_dtype)

    def body_one(*refs):
        a_ref, b_ref, o_ref = refs[0], refs[1], refs[-1]
        o_ref[...] = finish(refs, lax.dot_general(a_ref[...].astype(BF16), b_ref[...].astype(BF16), dn, preferred_element_type=F32))

    def body_acc(*refs):
        a_ref, b_ref = refs[0], refs[1]
        o_ref, acc = refs[-2], refs[-1]
        kk = pl.program_id(2)

        @pl.when(kk == 0)
        def _():
            acc[...] = jnp.zeros(acc.shape, F32)

        acc[...] += lax.dot_general(a_ref[...].astype(BF16), b_ref[...].astype(BF16), dn, preferred_element_type=F32)

        @pl.when(kk == nk - 1)
        def _():
            o_ref[...] = finish(refs, acc[...])

    a_spec = pl.BlockSpec((tk, tm), lambda i, j, q: (q, i)) if ta else pl.BlockSpec((tm, tk), lambda i, j, q: (i, q))
    b_spec = pl.BlockSpec((tn, tk), lambda i, j, q: (j, q)) if tb else pl.BlockSpec((tk, tn), lambda i, j, q: (q, j))
    o_spec = pl.BlockSpec((tm, tn), lambda i, j, q: (i, j))
    in_specs, args = [a_spec, b_spec], [a, b]
    if has_add:
        in_specs.append(o_spec)
        args.append(add)
    return pl.pallas_call(
        body_one if nk == 1 else body_acc, name=name, grid=(m // tm, n // tn, nk), in_specs=in_specs, out_specs=o_spec,
        out_shape=jax.ShapeDtypeStruct((m, n), out_dtype), scratch_shapes=[] if nk == 1 else [pltpu.VMEM((tm, tn), F32)],
        compiler_params=_CP(dimension_semantics=("parallel", "parallel", "arbitrary")),
    )(*args)


def _mm_pieces(name, a, bs, gain=None):
    m, k = a.shape
    n_b = len(bs)
    tm = _tile(m, MM_ROWS)
    fused = gain is not None
    n_lead = 2 if fused else 1

    def body(*refs):
        if fused:
            av = _rms(refs[0][...].astype(F32), refs[1][...]).astype(BF16)
            refs[-1][...] = av
        else:
            av = refs[0][...].astype(BF16)
        for b_ref, o_ref in zip(refs[n_lead:n_lead + n_b], refs[n_lead + n_b:n_lead + 2 * n_b]):
            o_ref[...] = jnp.dot(av, b_ref[...].astype(BF16), preferred_element_type=F32)

    row = pl.BlockSpec((tm, k), lambda i: (i, 0))
    res = pl.pallas_call(
        body, name=name, grid=(m // tm,),
        in_specs=[row] + ([pl.BlockSpec((1, k), lambda i: (0, 0))] if fused else []) + [pl.BlockSpec(b.shape, lambda i: (0, 0)) for b in bs],
        out_specs=[pl.BlockSpec((tm, b.shape[1]), lambda i: (i, 0)) for b in bs] + ([row] if fused else []),
        out_shape=[jax.ShapeDtypeStruct((m, b.shape[1]), F32) for b in bs] + ([jax.ShapeDtypeStruct((m, k), BF16)] if fused else []),
        compiler_params=_CP(dimension_semantics=("parallel",)),
    )(a, *([gain.reshape(1, k)] if fused else []), *bs)
    return (list(res[:n_b]), res[n_b]) if fused else list(res)


def _mm_pieces_t(name, dps, ws):
    m, k = dps[0].shape[0], ws[0].shape[0]
    n_b = len(ws)
    tm = _tile(m, MM_ROWS)
    dn = (((1,), (1,)), ((), ()))

    def body(*refs):
        acc = None
        for d_ref, w_ref in zip(refs[:n_b], refs[n_b:2 * n_b]):
            part = lax.dot_general(d_ref[...].astype(BF16), w_ref[...].astype(BF16), dn, preferred_element_type=F32)
            acc = part if acc is None else acc + part
        refs[-1][...] = acc

    return pl.pallas_call(
        body, name=name, grid=(m // tm,),
        in_specs=[pl.BlockSpec((tm, d.shape[1]), lambda i: (i, 0)) for d in dps] + [pl.BlockSpec(w.shape, lambda i: (0, 0)) for w in ws],
        out_specs=pl.BlockSpec((tm, k), lambda i: (i, 0)), out_shape=jax.ShapeDtypeStruct((m, k), F32),
        compiler_params=_CP(dimension_semantics=("parallel",)),
    )(*dps, *ws)


def _mm_pieces_dw(name, u, dps):
    s, k = u.shape
    n_b = len(dps)
    tk = _tile(s, MM_KROWS)
    n_steps = s // tk
    dn = (((0,), (0,)), ((), ()))

    def body(*refs):
        u_ref, d_refs, o_refs, accs = refs[0], refs[1:1 + n_b], refs[1 + n_b:1 + 2 * n_b], refs[1 + 2 * n_b:]
        step = pl.program_id(0)
        uv = u_ref[...].astype(BF16)
        for d_ref, o_ref, acc in zip(d_refs, o_refs, accs):
            part = lax.dot_general(uv, d_ref[...].astype(BF16), dn, preferred_element_type=F32)

            @pl.when(step == 0)
            def _(acc=acc, part=part):
                acc[...] = part

            @pl.when(step > 0)
            def _(acc=acc, part=part):
                acc[...] += part

            @pl.when(step == n_steps - 1)
            def _(acc=acc, o_ref=o_ref):
                o_ref[...] = acc[...].astype(o_ref.dtype)

    return pl.pallas_call(
        body, name=name, grid=(n_steps,),
        in_specs=[pl.BlockSpec((tk, k), lambda i: (i, 0))] + [pl.BlockSpec((tk, d.shape[1]), lambda i: (i, 0)) for d in dps],
        out_specs=[pl.BlockSpec((k, d.shape[1]), lambda i: (0, 0)) for d in dps],
        out_shape=[jax.ShapeDtypeStruct((k, d.shape[1]), BF16) for d in dps],
        scratch_shapes=[pltpu.VMEM((k, d.shape[1]), F32) for d in dps],
        compiler_params=_CP(dimension_semantics=("arbitrary",)),
    )(u, *dps)


MM_DW_MERGE_COLS = 4096


class In:
    def __init__(self, arr, block, imap, kind="x", per_h=False, gdtype=F32, gshape=None, gimap=None, prefixed=False):
        self.arr, self.block, self.imap, self.kind, self.per_h, self.gdtype = arr, tuple(block), imap, kind, per_h, gdtype
        self.prefixed = prefixed
        self.gshape = tuple(gshape) if gshape is not None else tuple(arr.shape)
        self.gimap = gimap if gimap is not None else imap

    def spec(self, rev_g=None):
        imap = self.imap
        if rev_g is None:
            return pl.BlockSpec(self.block, lambda h, g: imap(h, g))
        return pl.BlockSpec(self.block, lambda h, g: imap(h, rev_g - 1 - g))


class Out:
    def __init__(self, shape, dtype, block, imap):
        self.shape, self.dtype, self.block, self.imap = tuple(shape), dtype, tuple(block), imap

    def spec(self, rev_g=None):
        imap = self.imap
        if rev_g is None:
            return pl.BlockSpec(self.block, lambda h, g: imap(h, g))
        return pl.BlockSpec(self.block, lambda h, g: imap(h, rev_g - 1 - g))


def _load_f32(ref, rows=None):
    v = ref[...] if rows is None else ref[0:rows]
    return v.astype(F32) if jnp.issubdtype(v.dtype, jnp.floating) else v


def _state_out(grid, shape):
    nd = len(shape)
    return Out(tuple(grid) + tuple(shape), F32, (None, None) + tuple(shape), lambda h, g: (h, g) + (0,) * nd)


def _carry(comm, grid, refs, n_in, n_out, n_scr):
    n_c = len(comm.ops) if comm is not None else 0
    n_s = len(comm.sem_shapes) if comm is not None else 0
    p = 0
    in_refs = refs[p:p + n_in]; p += n_in
    c_src = refs[p:p + n_c]; p += n_c
    out_refs = refs[p:p + n_out]; p += n_out
    c_dst = refs[p:p + n_c]; p += n_c
    scr = refs[p:p + n_scr]; p += n_scr
    c_sem = refs[p:p + n_s]
    step = pl.program_id(0) * grid[1] + pl.program_id(1)
    n_steps = grid[0] * grid[1]
    when = (step == 0, step == _forward_step(n_steps), step == n_steps - 1)
    return in_refs, out_refs, scr, (c_src, c_dst, c_sem), when


def _forward_step(n_steps):
    return max(0, min(n_steps - 2, (3 * n_steps) // 4))


def _carry_specs(comm):
    if comm is None:
        return [], [], [], [], []
    any_spec = pl.BlockSpec(memory_space=pl.ANY)
    n = len(comm.ops)
    return [any_spec] * n, list(comm.ops), [any_spec] * n, list(comm.out_shapes), list(comm.sem_shapes)


def _op_fwd(name, f, grid, ins, outs, state_shapes=(), comm=None, prefix_rows=None):
    assert prefix_rows is None or not state_shapes
    n_in, n_out, n_st = len(ins), len(outs), len(state_shapes)
    st_outs = [_state_out(grid, s) for s in state_shapes]

    def body(*refs):
        in_refs, o_refs, st_scr, cargs, (first, fwd_step, last) = _carry(comm, grid, refs, n_in, n_out + n_st, n_st)
        out_refs, sv_refs = o_refs[:n_out], o_refs[n_out:]
        if comm is not None:
            @pl.when(first)
            def _():
                comm.start(*cargs)

            @pl.when(fwd_step)
            def _():
                comm.forward(*cargs)
        g = pl.program_id(1)
        if n_st:
            @pl.when(g == 0)
            def _():
                for s in st_scr:
                    s[...] = jnp.zeros(s.shape, F32)

        def compute(rows, g=g):
            vals = [_load_f32(r, rows if i.prefixed else None) for r, i in zip(in_refs, ins)]
            sts = [s[...] for s in st_scr]
            o, ns = f(g, vals, sts)
            for r, v in zip(out_refs, o):
                r[...] = v.astype(r.dtype)
            for r, s in zip(sv_refs, sts):
                r[...] = s
            for s, v in zip(st_scr, ns):
                s[...] = v

        if prefix_rows is None:
            compute(None)
        else:
            per = grid[1] // len(prefix_rows)
            for lv, rows in enumerate(prefix_rows):
                pl.when(g // per == lv)(functools.partial(compute, rows, lv) if per == 1 else functools.partial(compute, rows))
        if comm is not None:
            @pl.when(last)
            def _():
                comm.finish(*cargs)

    all_outs = list(outs) + st_outs
    c_in_specs, c_args, c_out_specs, c_out_shapes, c_sems = _carry_specs(comm)
    res = pl.pallas_call(
        body, name=name, grid=tuple(grid), in_specs=[i.spec() for i in ins] + c_in_specs,
        out_specs=[o.spec() for o in all_outs] + c_out_specs,
        out_shape=[jax.ShapeDtypeStruct(o.shape, o.dtype) for o in all_outs] + c_out_shapes,
        scratch_shapes=[pltpu.VMEM(tuple(s), F32) for s in state_shapes] + c_sems,
        compiler_params=_CP(dimension_semantics=("arbitrary", "arbitrary")),
    )(*[i.arr for i in ins], *c_args)
    n_all = n_out + n_st
    return list(res[:n_out]), list(res[n_out:n_all]), list(res[n_all:])


def _op_bwd(name, f, grid, ins, outs, state_shapes, saved, douts, addto=None, comm=None, prefix_rows=None):
    n_in, n_out, n_st = len(ins), len(outs), len(state_shapes)
    n_g = grid[1]
    assert prefix_rows is None or all(i.prefixed and i.per_h for i in ins if i.kind == "p")
    addto = addto or {}
    diff = [k for k, i in enumerate(ins) if i.kind in ("x", "p")]
    add_idx = sorted(addto)
    st_ins = [In(s, o.block, o.imap, "c") for s, o in zip(saved, [_state_out(grid, s) for s in state_shapes])]
    dout_ins = [In(d, o.block, o.imap, "c") for d, o in zip(douts, outs)]
    add_ins = []
    for k in add_idx:
        i, a = ins[k], addto[k]
        blk = i.block if i.kind == "x" else i.block[:-2] + a.shape[-2:]
        add_ins.append(In(a, blk, i.gimap if i.kind == "x" else i.imap, "c"))
    g_outs = []
    for k in diff:
        i = ins[k]
        g_outs.append(Out(i.gshape, i.gdtype if i.kind == "x" else F32, i.block, i.gimap))

    def body(*refs):
        all_in, go_refs, ds_scr, cargs, (first_step, fwd_step, last_step) = _carry(comm, grid, refs, n_in + n_st + n_out + len(add_idx),
                                                                                    len(diff), n_st)
        if comm is not None:
            @pl.when(first_step)
            def _():
                comm.start(*cargs)

            @pl.when(fwd_step)
            def _():
                comm.forward(*cargs)
        p = 0
        in_refs = all_in[p:p + n_in]; p += n_in
        sv_refs = all_in[p:p + n_st]; p += n_st
        do_refs = all_in[p:p + n_out]; p += n_out
        ad_refs = all_in[p:p + len(add_idx)]
        hh = pl.program_id(0)
        step = pl.program_id(1)
        g = n_g - 1 - step
        if n_st:
            @pl.when(step == 0)
            def _():
                for s in ds_scr:
                    s[...] = jnp.zeros(s.shape, F32)

        def compute(rows, g=g):
            vals = [_load_f32(r, rows if i.prefixed else None) for r, i in zip(in_refs, ins)]
            sts = [r[...] for r in sv_refs]

            def fw(dvals, states):
                full = list(vals)
                for k, v in zip(diff, dvals):
                    full[k] = v
                o, ns = f(g, full, states)
                return list(o), list(ns)

            _, vjp = jax.vjp(fw, [vals[k] for k in diff], sts)
            cts = [r[...].astype(F32) for r in do_refs]
            dns = [s[...] for s in ds_scr]
            dvals, dsts = vjp((cts, dns))
            adds = dict(zip(add_idx, ad_refs))
            for k, r, dv in zip(diff, go_refs, dvals):
                i = ins[k]
                if i.kind == "x":
                    if k in adds:
                        dv = dv + adds[k][...].astype(F32)
                    r[...] = dv.astype(r.dtype)
                elif rows is not None:
                    r[0:rows] += dv
                else:
                    first = (step == 0) if i.per_h else jnp.logical_and(step == 0, hh == 0)

                    @pl.when(first)
                    def _(r=r, dv=dv, k=k):
                        r[...] = dv
                        if k in adds:
                            lead = adds[k].shape[0]
                            r[0:lead] += adds[k][...]

                    @pl.when(jnp.logical_not(first))
                    def _(r=r, dv=dv):
                        r[...] += dv
            for s, v in zip(ds_scr, dsts):
                s[...] = v

        if prefix_rows is None:
            compute(None)
        else:
            @pl.when(step == 0)
            def _():
                for k, r in zip(diff, go_refs):
                    if ins[k].kind == "p":
                        r[...] = jnp.zeros(r.shape, F32)
            per = n_g // len(prefix_rows)
            for lv, rows in enumerate(prefix_rows):
                pl.when(g // per == lv)(functools.partial(compute, rows, lv) if per == 1 else functools.partial(compute, rows))
        if comm is not None:
            @pl.when(last_step)
            def _():
                comm.finish(*cargs)

    all_ins = list(ins) + st_ins + dout_ins + add_ins
    c_in_specs, c_args, c_out_specs, c_out_shapes, c_sems = _carry_specs(comm)
    res = pl.pallas_call(
        body, name=name, grid=tuple(grid), in_specs=[i.spec(n_g) for i in all_ins] + c_in_specs,
        out_specs=[o.spec(n_g) for o in g_outs] + c_out_specs,
        out_shape=[jax.ShapeDtypeStruct(o.shape, o.dtype) for o in g_outs] + c_out_shapes,
        scratch_shapes=[pltpu.VMEM(tuple(s), F32) for s in state_shapes] + c_sems,
        compiler_params=_CP(dimension_semantics=("arbitrary", "arbitrary")),
    )(*[i.arr for i in all_ins], *c_args)
    return list(res[:len(g_outs)]), list(res[len(g_outs):])


class Op:
    def __init__(self, name, f, grid, ins, outs, state_shapes=(), prefix_rows=None):
        self.name, self.f, self.grid, self.ins, self.outs, self.state_shapes = name, f, grid, ins, outs, state_shapes
        self.prefix_rows = prefix_rows
        self.saved = []

    def fwd(self, comm=None):
        res, self.saved, self.fwd_comm_out = _op_fwd(self.name + "_fwd", self.f, self.grid, self.ins, self.outs, self.state_shapes, comm,
                                                     self.prefix_rows)
        return res

    def bwd(self, douts, addto=None, comm=None):
        res, self.bwd_comm_out = _op_bwd(self.name + "_bwd", self.f, self.grid, self.ins, self.outs, self.state_shapes, self.saved, douts,
                                         addto, comm, self.prefix_rows)
        return res


def _rows(arr, t, kind="x", gdtype=F32):
    return In(arr, (t, arr.shape[1]), lambda h, g: (g, 0), kind, gdtype=gdtype)


def _whole(arr, kind="p"):
    nd = arr.ndim
    return In(arr, arr.shape, lambda h, g: (0,) * nd, kind)


def _rows_out(s, n, t, dtype):
    return Out((s, n), dtype, (t, n), lambda h, g: (g, 0))


ROW_T = 512
CONV_T = 256


def _rms_op(name, x, gain, out_dtype=BF16, gdtype=F32):
    s, n = x.shape

    def f(g, vals, sts):
        return [_rms(vals[0], vals[1])], []

    return Op(name, f, (1, s // ROW_T), [_rows(x, ROW_T, gdtype=gdtype), _whole(gain.reshape(1, n))], [_rows_out(s, n, ROW_T, out_dtype)])


def _mla_prep_op(qn, q1, q2, kn, kr, v, cos, sin):
    s = qn.shape[0]
    hd, half = MLA_HEADS, MLA_ROPE // 2

    def f(g, vals, sts):
        qn, q1, q2, kn, kr, v, cos, sin = vals
        cos_h, sin_h = jnp.tile(cos, (1, hd)), jnp.tile(sin, (1, hd))
        r1 = q1 * cos_h - q2 * sin_h
        r2 = q2 * cos_h + q1 * sin_h
        k1, k2 = kr[:, 0:half], kr[:, half:2 * half]
        kr1 = k1 * cos - k2 * sin
        kr2 = k2 * cos + k1 * sin
        zpad = jnp.zeros((qn.shape[0], LANE - MLA_QK), F32)
        qs, ks, vs = [], [], []
        for h in range(hd):
            a, b = h * MLA_NOPE, (h + 1) * MLA_NOPE
            c, d = h * half, (h + 1) * half
            qs.append(jnp.concatenate([qn[:, a:b], r1[:, c:d], r2[:, c:d], zpad], axis=1))
            ks.append(jnp.concatenate([kn[:, a:b], kr1, kr2, zpad], axis=1))
            vs.append(v[:, a:b])
        return [jnp.stack(qs, 0), jnp.stack(ks, 0), jnp.stack(vs, 0)], []

    ins = [_rows(qn, ROW_T, gdtype=BF16), _rows(q1, ROW_T, gdtype=BF16), _rows(q2, ROW_T, gdtype=BF16), _rows(kn, ROW_T, gdtype=BF16),
           _rows(kr, ROW_T, gdtype=BF16), _rows(v, ROW_T, gdtype=BF16), _rows(cos, ROW_T, "c"), _rows(sin, ROW_T, "c")]
    outs = [Out((hd, s, LANE), BF16, (hd, ROW_T, LANE), lambda h, g: (0, g, 0)),
            Out((hd, s, LANE), BF16, (hd, ROW_T, LANE), lambda h, g: (0, g, 0)),
            Out((hd, s, MLA_V), BF16, (hd, ROW_T, MLA_V), lambda h, g: (0, g, 0))]
    return Op("mla_prep", f, (1, s // ROW_T), ins, outs)


ATT_TQ = 256
ATT_LEVELS = 8


def _mla_attn_op(q, k, v):
    hd, s, _ = q.shape
    scale = MLA_QK ** -0.5

    def f(g, vals, sts):
        q, k, v = vals
        sc = _bdot_nt(q, k) * scale
        r = lax.broadcasted_iota(jnp.int32, sc.shape, 0) + g * ATT_TQ
        c = lax.broadcasted_iota(jnp.int32, sc.shape, 1)
        sc = jnp.where(r >= c, sc, -1e30)
        m = lax.stop_gradient(jnp.max(sc, axis=-1, keepdims=True))
        p = jnp.exp(sc - m)
        p = p * (1.0 / jnp.sum(p, axis=-1, keepdims=True))
        return [_bdot(p, v)], []

    ins = [In(q, (None, ATT_TQ, LANE), lambda h, g: (h, g, 0), "x", gdtype=BF16),
           In(k, (None, s, LANE), lambda h, g: (h, 0, 0), "p", per_h=True, prefixed=True),
           In(v, (None, s, MLA_V), lambda h, g: (h, 0, 0), "p", per_h=True, prefixed=True)]
    outs = [Out((hd, s, MLA_V), F32, (None, ATT_TQ, MLA_V), lambda h, g: (h, g, 0))]
    return Op("mla_attn", f, (hd, s // ATT_TQ), ins, outs, prefix_rows=[(lv + 1) * (s // ATT_LEVELS) for lv in range(ATT_LEVELS)])


def _mla_post_op(o, gate):
    hd, s, _ = o.shape

    def f(g, vals, sts):
        o, gate = vals
        cat = jnp.concatenate([o[h] for h in range(hd)], axis=1)
        return [cat * _silu(gate)], []

    ins = [In(o, (hd, ROW_T, MLA_V), lambda h, g: (0, g, 0), "x"), _rows(gate, ROW_T, gdtype=BF16)]
    return Op("mla_post", f, (1, s // ROW_T), ins, [_rows_out(s, hd * MLA_V, ROW_T, BF16)])


def _gla_gate_op(gk, w2, b):
    s = gk.shape[0]

    def f(g, vals, sts):
        gk, w2, b = vals
        return [jax.nn.log_sigmoid(_bdot(gk, w2) + b) / GLA_TAU], []

    ins = [_rows(gk, ROW_T, gdtype=BF16), _whole(w2), _whole(b)]
    return Op("gla_gate", f, (1, s // ROW_T), ins, [_rows_out(s, GLA_HEADS * GLA_DK, ROW_T, F32)])


def _gla_core_op(q, k, v, gate, la, g_o):
    s = q.shape[0]
    c, nh = GLA_CHUNK, GLA_HEADS

    def f(g, vals, sts):
        q, k, v, gate, la, g_o = vals
        tri = _tri(c)
        b = _cumsum_rows(la)
        b_last = jnp.sum(la, axis=0, keepdims=True)
        qt = q * (GLA_DK ** -0.5) * jnp.exp(b)
        kt = k * jnp.exp(-b)
        kd = k * jnp.exp(b_last - b)
        ys, new_sts = [], []
        for h in range(nh):
            ks, vs = slice(h * GLA_DK, (h + 1) * GLA_DK), slice(h * GLA_DV, (h + 1) * GLA_DV)
            att = jnp.where(tri, _bdot_nt(qt[:, ks], kt[:, ks]), 0.0)
            o = _bdot(att, v[:, vs]) + _bdot_nt(qt[:, ks], sts[h])
            new_sts.append(jnp.exp(b_last[:, ks]) * sts[h] + _bdot_tn(v[:, vs], kd[:, ks]))
            ys.append(_rms(o, g_o) * _silu(gate[:, vs]))
        return [jnp.concatenate(ys, axis=1)], new_sts

    ins = [_rows(q, c, gdtype=BF16), _rows(k, c, gdtype=BF16), _rows(v, c, gdtype=BF16), _rows(gate, c, gdtype=BF16), _rows(la, c), _whole(g_o)]
    outs = [_rows_out(s, nh * GLA_DV, c, BF16)]
    return Op("gla_core", f, (1, s // c), ins, outs, [(GLA_DV, GLA_DK)] * nh)


LRU_T = 256


def _lru_op(gate, u, conv_w, conv_b, w_a, b_a, w_x, b_x, lam):
    s, w = u.shape
    t = LRU_T

    def f(g, vals, sts):
        gate, u, cw, cb, w_a, b_a, w_x, b_x, lam = vals
        u_prev, h_prev = sts
        uc = cb
        for kk in range(CONV_W):
            uc = uc + cw[kk] * _shift_rows(u, u_prev, CONV_W - 1 - kk)
        ra, ri = [], []
        for n in range(LRU_BLOCKS):
            blk = uc[:, n * LRU_BLOCK:(n + 1) * LRU_BLOCK]
            ra.append(_bdot(blk, w_a[n]))
            ri.append(_bdot(blk, w_x[n]))
        r = jax.nn.sigmoid(jnp.concatenate(ra, axis=1) + b_a)
        i = jax.nn.sigmoid(jnp.concatenate(ri, axis=1) + b_x)
        log_a = -LRU_C * r * jax.nn.softplus(-lam)
        a = jnp.exp(log_a)
        bb = jnp.sqrt(_one_minus_exp(2.0 * log_a)) * (i * uc)
        zero = jnp.zeros_like(a)
        sh = 1
        while sh < t:
            a_s = _shift_rows(a - 1.0, zero, sh) + 1.0
            b_s = _shift_rows(bb, zero, sh)
            bb = a * b_s + bb
            a = a * a_s
            sh *= 2
        hs = bb + a * h_prev
        last = (lax.broadcasted_iota(jnp.int32, hs.shape, 0) == t - 1).astype(F32)
        h_last = jnp.sum(hs * last, axis=0, keepdims=True)
        return [hs * _silu(gate)], [u, h_last]

    ins = [_rows(gate, t, gdtype=BF16), _rows(u, t, gdtype=BF16), _whole(conv_w), _whole(conv_b), _whole(w_a), _whole(b_a), _whole(w_x),
           _whole(b_x), _whole(lam)]
    return Op("lru_core", f, (1, s // t), ins, [_rows_out(s, w, t, BF16)], [(t, w), (1, w)])


def _ssd_conv_op(xbc, conv_w, conv_b):
    s, w = xbc.shape
    t = CONV_T
    n_x, n_b = SSD_INNER, SSD_GROUPS * SSD_STATE

    def f(g, vals, sts):
        xbc, cw, cb = vals
        acc = cb
        for kk in range(CONV_W):
            acc = acc + cw[kk] * _shift_rows(xbc, sts[0], CONV_W - 1 - kk)
        y = _silu(acc)
        return [y[:, :n_x], y[:, n_x:n_x + n_b], y[:, n_x + n_b:]], [xbc]

    ins = [_rows(xbc, t, gdtype=BF16), _whole(conv_w), _whole(conv_b)]
    outs = [_rows_out(s, n_x, t, F32), _rows_out(s, n_b, t, BF16), _rows_out(s, n_b, t, BF16)]
    return Op("ssd_conv", f, (1, s // t), ins, outs, [(t, w)])


SSD_L = 512


def _ssd_core_op(x, bm, cm, z, dt, dt_bias, a_log, d_skip, g_norm):
    s = x.shape[0]
    c, hg, p = SSD_L, SSD_HPG, SSD_P
    gw = hg * p

    def f(g, vals, sts):
        x, bm, cm, z, dtr, dt_bias, a_log, d_skip, g_norm = vals
        tri = _tri(c)
        dt = jax.nn.softplus(dtr + dt_bias)
        da = dt * (-jnp.exp(a_log))
        cs = _cumsum_rows(da)
        cs_last = jnp.sum(da, axis=0, keepdims=True)
        cs_t = jnp.transpose(jnp.concatenate([cs, jnp.zeros((c, LANE - hg), F32)], axis=1))
        cb = _bdot_nt(cm, bm)
        ys, new_st = [], []
        for h in range(hg):
            cs_h = cs[:, h:h + 1]
            cs_row = cs_t[h:h + 1, :]
            seg = jnp.where(tri, cs_h - cs_row, 0.0)
            lmat = jnp.where(tri, jnp.exp(seg), 0.0)
            x_h = x[:, h * p:(h + 1) * p]
            xdt = x_h * dt[:, h:h + 1]
            y_diag = _bdot(cb * lmat, xdt)
            decay = jnp.exp(cs_last[:, h:h + 1] - cs_h)
            states = _bdot_tn(xdt * decay, bm)
            y_off = _bdot_nt(cm, sts[h]) * jnp.exp(cs_h)
            new_st.append(jnp.exp(cs_last[:, h:h + 1]) * sts[h] + states)
            ys.append(y_diag + y_off + d_skip[:, h:h + 1] * x_h)
        y = jnp.concatenate(ys, axis=1) * _silu(z)
        return [_rms(y, g_norm)], new_st

    ins = [In(x, (c, gw), lambda h, g: (g, h), "x"), In(bm, (c, SSD_STATE), lambda h, g: (g, h), "x", gdtype=BF16),
           In(cm, (c, SSD_STATE), lambda h, g: (g, h), "x", gdtype=BF16), In(z, (c, gw), lambda h, g: (g, h), "x", gdtype=BF16),
           In(dt, (None, c, hg), lambda h, g: (h, g, 0), "x"),
           In(dt_bias, (None, 1, hg), lambda h, g: (h, 0, 0), "p", per_h=True),
           In(a_log, (None, 1, hg), lambda h, g: (h, 0, 0), "p", per_h=True),
           In(d_skip, (None, 1, hg), lambda h, g: (h, 0, 0), "p", per_h=True),
           In(g_norm, (1, gw), lambda h, g: (0, h), "p", per_h=True)]
    outs = [Out((s, SSD_INNER), BF16, (c, gw), lambda h, g: (g, h))]
    return Op("ssd_core", f, (SSD_GROUPS, s // c), ins, outs, [(p, SSD_STATE)] * hg)


def _loss_op(h, target, final_g):
    s, n = h.shape
    t = ROW_T
    n_g = s // t

    def body(h_ref, t_ref, g_ref, loss_ref, dh_ref, dg_ref):
        step = pl.program_id(0)

        def lossf(hv, gv):
            err = _rms(hv, gv) - t_ref[...]
            return 0.5 * jnp.sum(jnp.mean(err * err, axis=-1))

        l, (dh, dg) = jax.value_and_grad(lossf, argnums=(0, 1))(h_ref[...], g_ref[...])
        dh_ref[...] = dh

        @pl.when(step == 0)
        def _():
            loss_ref[...] = jnp.zeros(loss_ref.shape, F32)
            dg_ref[...] = jnp.zeros(dg_ref.shape, F32)

        loss_ref[...] += jnp.full(loss_ref.shape, l, F32)
        dg_ref[...] += dg

    row = pl.BlockSpec((t, n), lambda g: (g, 0))
    one = pl.BlockSpec((1, n), lambda g: (0, 0))
    return pl.pallas_call(
        body, name="loss_head", grid=(n_g,), in_specs=[row, row, one],
        out_specs=[pl.BlockSpec((1, LANE), lambda g: (0, 0)), row, one],
        out_shape=[jax.ShapeDtypeStruct((1, LANE), F32), jax.ShapeDtypeStruct((s, n), F32), jax.ShapeDtypeStruct((1, n), F32)],
        compiler_params=_CP(dimension_semantics=("arbitrary",)),
    )(h, target, final_g.reshape(1, n))


def _pad_cols(w, n):
    return jnp.pad(w, ((0, 0), (0, n - w.shape[1])))


def _pad_rows(w, n):
    return jnp.pad(w, ((0, n - w.shape[0]), (0, 0)))


def _proj_bwd(tag, u, dps, ws):
    du = _mm_pieces_t(f"{tag}_du", dps, ws)
    if sum(dp.shape[1] for dp in dps) <= MM_DW_MERGE_COLS:
        dws = _mm_pieces_dw(f"{tag}_dw", u, dps)
    else:
        dws = [_mm(f"{tag}_dw{i}", u, dp, ta=True, out_dtype=BF16) for i, dp in enumerate(dps)]
    return du, dws


def _mla_layer(h, norm_g, w, cos, sin, fwd_comm=None, late_w_out=None):
    bf = lambda a: a.astype(BF16)
    w_in, w_uq, w_ukv = w["mla_w_in"], w["mla_w_uq"], w["mla_w_ukv"]
    a0, a1, a2 = MLA_Q_RANK, MLA_Q_RANK + MLA_KV_RANK, MLA_Q_RANK + MLA_KV_RANK + MLA_ROPE
    w_cq, w_ckv, w_kr, w_g = bf(w_in[:, :a0]), bf(w_in[:, a0:a1]), bf(_pad_cols(w_in[:, a1:a2], LANE)), bf(w_in[:, a2:])
    uq = w_uq.reshape(MLA_Q_RANK, MLA_HEADS, MLA_QK)
    half = MLA_ROPE // 2
    w_qn = bf(uq[:, :, :MLA_NOPE].reshape(MLA_Q_RANK, -1))
    w_q1 = bf(uq[:, :, MLA_NOPE:MLA_NOPE + half].reshape(MLA_Q_RANK, -1))
    w_q2 = bf(uq[:, :, MLA_NOPE + half:].reshape(MLA_Q_RANK, -1))
    ukv = w_ukv.reshape(MLA_KV_RANK, MLA_HEADS, MLA_NOPE + MLA_V)
    w_kn = bf(ukv[:, :, :MLA_NOPE].reshape(MLA_KV_RANK, -1))
    w_v = bf(ukv[:, :, MLA_NOPE:].reshape(MLA_KV_RANK, -1))

    n0 = _rms_op("mla_norm", h, norm_g)
    (cq, ckv, kr, gate), u = _mm_pieces("mla_in", h, (w_cq, w_ckv, w_kr, w_g), gain=norm_g)
    nq = _rms_op("mla_qnorm", cq, w["mla_g_q"], gdtype=BF16)
    nkv = _rms_op("mla_kvnorm", ckv, w["mla_g_kv"], gdtype=BF16)
    (qn, q1, q2), qn_ = _mm_pieces("mla_uq", cq, (w_qn, w_q1, w_q2), gain=w["mla_g_q"])
    (kn, v), kvn_ = _mm_pieces("mla_ukv", ckv, (w_kn, w_v), gain=w["mla_g_kv"])
    prep = _mla_prep_op(qn, q1, q2, kn, kr, v, cos, sin)
    qh, kh, vh = prep.fwd()
    attn = _mla_attn_op(qh, kh, vh)
    o, = attn.fwd(fwd_comm)
    w_out = bf(w["mla_w_out"]) if late_w_out is None else late_w_out(attn.fwd_comm_out)
    post = _mla_post_op(o, gate)
    y, = post.fwd()
    h_out = _mm("mla_out", y, w_out, add=h)

    def bwd(dh, make_comm=None):
        dy = _mm("mla_out_dy", dh, w_out, tb=True, out_dtype=BF16)
        d_w_out = _mm("mla_out_dw", y, dh, ta=True, out_dtype=BF16)
        do, dgate = post.bwd([dy])
        dqh, dkh, dvh = attn.bwd([do], comm=None if make_comm is None else make_comm(d_w_out))
        dqn, dq1, dq2, dkn, dkr, dv = prep.bwd([dqh, dkh, dvh])
        dqn_, d_uq = _proj_bwd("mla_uq", qn_, (dqn, dq1, dq2), (w_qn, w_q1, w_q2))
        dkvn_, d_ukv = _proj_bwd("mla_ukv", kvn_, (dkn, dv), (w_kn, w_v))
        dcq, d_g_q = nq.bwd([dqn_])
        dckv, d_g_kv = nkv.bwd([dkvn_])
        du, d_in = _proj_bwd("mla_in", u, (dcq, dckv, dkr, dgate), (w_cq, w_ckv, w_kr, w_g))
        dh_in, d_norm = n0.bwd([du], addto={0: dh})
        shp = (MLA_Q_RANK, MLA_HEADS, -1)
        g_uq = jnp.concatenate([d_uq[0].reshape(shp), d_uq[1].reshape(shp), d_uq[2].reshape(shp)], axis=2).reshape(MLA_Q_RANK, -1)
        shp = (MLA_KV_RANK, MLA_HEADS, -1)
        g_ukv = jnp.concatenate([d_ukv[0].reshape(shp), d_ukv[1].reshape(shp)], axis=2).reshape(MLA_KV_RANK, -1)
        g_in = jnp.concatenate([d_in[0], d_in[1], d_in[2][:, :MLA_ROPE], d_in[3]], axis=1)
        return dh_in, d_norm, {"mla_w_in": g_in, "mla_g_q": d_g_q.reshape(-1), "mla_w_uq": g_uq, "mla_g_kv": d_g_kv.reshape(-1),
                               "mla_w_ukv": g_ukv, "mla_w_out": d_w_out}, attn.bwd_comm_out

    return h_out, bwd, attn.fwd_comm_out


def _gla_layer(h, norm_g, w, fwd_comm=None):
    bf = lambda a: a.astype(BF16)
    w_in = w["gla_w_in"]
    nk, nv = GLA_HEADS * GLA_DK, GLA_HEADS * GLA_DV
    cuts = (0, nk, 2 * nk, 2 * nk + nv, 2 * nk + 2 * nv)
    w_q, w_k, w_v, w_g = (bf(w_in[:, cuts[i]:cuts[i + 1]]) for i in range(4))
    w_gk = bf(_pad_cols(w_in[:, cuts[4]:], LANE))
    w2 = _pad_rows(w["gla_w_gk2"], LANE)
    b_gk = w["gla_b_gk"].reshape(1, -1)
    g_o = w["gla_g_o"].reshape(1, -1)
    w_out = bf(w["gla_w_out"])

    n0 = _rms_op("gla_norm", h, norm_g)
    (q, k, v, gate, gk), u = _mm_pieces("gla_in", h, (w_q, w_k, w_v, w_g, w_gk), gain=norm_g)
    gop = _gla_gate_op(gk, w2, b_gk)
    la, = gop.fwd()
    core = _gla_core_op(q, k, v, gate, la, g_o)
    y, = core.fwd(fwd_comm)
    h_out = _mm("gla_out", y, w_out, add=h)

    def bwd(dh, make_comm=None):
        dy = _mm("gla_out_dy", dh, w_out, tb=True, out_dtype=BF16)
        d_w_out = _mm("gla_out_dw", y, dh, ta=True, out_dtype=BF16)
        dq, dk, dv, dgate, dla, d_g_o = core.bwd([dy], comm=None if make_comm is None else make_comm(d_w_out))
        dgk, d_w2, d_b = gop.bwd([dla])
        du, d_in = _proj_bwd("gla_in", u, (dq, dk, dv, dgate, dgk), (w_q, w_k, w_v, w_g, w_gk))
        dh_in, d_norm = n0.bwd([du], addto={0: dh})
        g_in = jnp.concatenate([d_in[0], d_in[1], d_in[2], d_in[3], d_in[4][:, :GLA_RANK]], axis=1)
        return dh_in, d_norm, {"gla_w_in": g_in, "gla_w_gk2": d_w2[:GLA_RANK], "gla_b_gk": d_b.reshape(-1), "gla_g_o": d_g_o.reshape(-1),
                               "gla_w_out": d_w_out}, core.bwd_comm_out

    return h_out, bwd, core.fwd_comm_out


def _lru_layer(h, norm_g, w, fwd_comm=None):
    bf = lambda a: a.astype(BF16)
    w_in = w["lru_w_in"]
    w_g, w_u = bf(w_in[:, :LRU_WIDTH]), bf(w_in[:, LRU_WIDTH:])
    row = lambda a: a.reshape(1, -1)
    w_out = bf(w["lru_w_out"])

    n0 = _rms_op("lru_norm", h, norm_g)
    (gate, u), u_ = _mm_pieces("lru_in", h, (w_g, w_u), gain=norm_g)
    core = _lru_op(gate, u, w["lru_conv_w"].reshape(CONV_W, 1, -1), row(w["lru_conv_b"]), w["lru_w_a"], row(w["lru_b_a"]), w["lru_w_x"],
                   row(w["lru_b_x"]), row(w["lru_lam"]))
    y, = core.fwd(fwd_comm)
    h_out = _mm("lru_out", y, w_out, add=h)

    def bwd(dh, make_comm=None):
        dy = _mm("lru_out_dy", dh, w_out, tb=True, out_dtype=BF16)
        d_w_out = _mm("lru_out_dw", y, dh, ta=True, out_dtype=BF16)
        dgate, du, d_cw, d_cb, d_wa, d_ba, d_wx, d_bx, d_lam = core.bwd([dy], comm=None if make_comm is None else make_comm(d_w_out))
        du_, d_in = _proj_bwd("lru_in", u_, (dgate, du), (w_g, w_u))
        dh_in, d_norm = n0.bwd([du_], addto={0: dh})
        return dh_in, d_norm, {"lru_w_in": jnp.concatenate(d_in, axis=1), "lru_conv_w": d_cw.reshape(CONV_W, -1), "lru_conv_b": d_cb.reshape(-1),
                               "lru_w_a": d_wa, "lru_b_a": d_ba.reshape(-1), "lru_w_x": d_wx, "lru_b_x": d_bx.reshape(-1),
                               "lru_lam": d_lam.reshape(-1), "lru_w_out": d_w_out}, core.bwd_comm_out

    return h_out, bwd, core.fwd_comm_out


def _ssd_layer(h, norm_g, w, fwd_comm=None, late_w_out=None):
    bf = lambda a: a.astype(BF16)
    s = h.shape[0]
    w_in = w["ssd_w_in"]
    conv_dim = SSD_INNER + 2 * SSD_GROUPS * SSD_STATE
    w_z, w_xbc = bf(w_in[:, :SSD_INNER]), bf(w_in[:, SSD_INNER:SSD_INNER + conv_dim])
    w_dt = bf(_pad_cols(w_in[:, SSD_INNER + conv_dim:], LANE))
    grp = lambda a: a.reshape(SSD_GROUPS, 1, SSD_HPG)

    n0 = _rms_op("ssd_norm", h, norm_g)
    (z, xbc, dtp), u = _mm_pieces("ssd_in", h, (w_z, w_xbc, w_dt), gain=norm_g)
    conv = _ssd_conv_op(xbc, w["ssd_conv_w"].reshape(CONV_W, 1, -1), w["ssd_conv_b"].reshape(1, -1))
    x, bm, cm = conv.fwd()
    dt = dtp[:, :SSD_HEADS].reshape(s, SSD_GROUPS, SSD_HPG).transpose(1, 0, 2)
    core = _ssd_core_op(x, bm, cm, z, dt, grp(w["ssd_dt_bias"]), grp(w["ssd_a_log"]), grp(w["ssd_d"]), w["ssd_g_norm"].reshape(1, -1))
    y, = core.fwd(fwd_comm)
    w_out = bf(w["ssd_w_out"]) if late_w_out is None else late_w_out(core.fwd_comm_out)
    h_out = _mm("ssd_out", y, w_out, add=h)

    def bwd(dh, make_comm=None):
        dy = _mm("ssd_out_dy", dh, w_out, tb=True, out_dtype=BF16)
        d_w_out = _mm("ssd_out_dw", y, dh, ta=True, out_dtype=BF16)
        dx, dbm, dcm, dz, ddt, d_dtb, d_alog, d_d, d_gn = core.bwd([dy], comm=None if make_comm is None else make_comm(d_w_out))
        dxbc, d_cw, d_cb = conv.bwd([dx, dbm, dcm])
        ddtp = _pad_cols(ddt.transpose(1, 0, 2).reshape(s, SSD_HEADS), LANE).astype(BF16)
        du, d_in = _proj_bwd("ssd_in", u, (dz, dxbc, ddtp), (w_z, w_xbc, w_dt))
        dh_in, d_norm = n0.bwd([du], addto={0: dh})
        g_in = jnp.concatenate([d_in[0], d_in[1], d_in[2][:, :SSD_HEADS]], axis=1)
        return dh_in, d_norm, {"ssd_w_in": g_in, "ssd_conv_w": d_cw.reshape(CONV_W, -1), "ssd_conv_b": d_cb.reshape(-1),
                               "ssd_dt_bias": d_dtb.reshape(-1), "ssd_a_log": d_alog.reshape(-1), "ssd_d": d_d.reshape(-1),
                               "ssd_g_norm": d_gn.reshape(-1), "ssd_w_out": d_w_out}, core.bwd_comm_out

    return h_out, bwd


def _rope_tables(positions):
    inv_freq = ROPE_THETA ** (-jnp.arange(0, MLA_ROPE, 2, dtype=F32) / MLA_ROPE)
    ang = positions.astype(F32)[:, None] * inv_freq
    return jnp.cos(ang), jnp.sin(ang)


WEIGHTS = ["norm_g", "final_g", "mla_w_in", "mla_g_q", "mla_w_uq", "mla_g_kv", "mla_w_ukv", "mla_w_out", "gla_w_in", "gla_w_gk2", "gla_b_gk",
           "gla_g_o", "gla_w_out", "lru_w_in", "lru_conv_w", "lru_conv_b", "lru_w_a", "lru_b_a", "lru_w_x", "lru_b_x", "lru_lam", "lru_w_out",
           "ssd_w_in", "ssd_conv_w", "ssd_conv_b", "ssd_dt_bias", "ssd_a_log", "ssd_d", "ssd_g_norm", "ssd_w_out"]
BIG = ["mla_w_in", "mla_w_uq", "mla_w_ukv", "mla_w_out", "gla_w_in", "gla_w_out", "lru_w_in", "lru_w_out", "ssd_w_in", "ssd_w_out"]
SMALL = ["gla_w_gk2", "gla_b_gk", "gla_g_o", "lru_conv_w", "lru_conv_b", "lru_b_a", "lru_b_x", "lru_lam", "ssd_conv_w", "ssd_conv_b", "ssd_g_norm"]
REPL = ["norm_g", "final_g", "mla_g_q", "mla_g_kv", "lru_w_a", "lru_w_x", "ssd_dt_bias", "ssd_a_log", "ssd_d"]
REPL_EARLY = ["lru_w_a", "lru_w_x"]
REPL_LATE = [n for n in REPL if n not in REPL_EARLY]
N_CHIPS, N_DEV = 4, 8
PACK_W = 1024
ADAM_ROWS = 256
SMALL_ROWS = 64


def _shard_axis(name):
    return 0 if name.endswith("_w_out") else -1


def _pack(arrs, dtype, row_mult):
    flat = jnp.concatenate([a.reshape(-1).astype(dtype) for a in arrs])
    per = PACK_W * row_mult
    total = -(-flat.shape[0] // per) * per
    return jnp.pad(flat, (0, total - flat.shape[0])).reshape(-1, PACK_W)


def _unpack(buf, shapes):
    flat = buf.reshape(-1)
    out, off = [], 0
    for s in shapes:
        n = math.prod(s)
        out.append(flat[off:off + n].reshape(s))
        off += n
    return out


def _mesh_pos():
    return lax.axis_index("x"), lax.axis_index("y"), lax.axis_index("c")


class GatherComm:
    def __init__(self, ops):
        self.ops = list(ops)
        n = len(self.ops)
        assert all(o.ndim == 2 and o.shape[0] % 32 == 0 for o in self.ops), [o.shape for o in self.ops]
        self.out_shapes = [jax.ShapeDtypeStruct((N_CHIPS,) + o.shape, o.dtype) for o in self.ops]
        self.sem_shapes = [pltpu.SemaphoreType.DMA((6 * n,)), pltpu.SemaphoreType.DMA((6 * n,)), pltpu.SemaphoreType.DMA((n,))]

    def _copies(self, srcs, dsts, sems):
        send_sems, recv_sems, local_sems = sems
        n = len(self.ops)
        x, y, c = _mesh_pos()
        me_id, sibling = (x, y, c), (x, y, 1 - c)
        chips = [(1 - x, y), (x, 1 - y), (1 - x, 1 - y)]
        mine = 2 * x + y

        def half(i, cc):
            h = self.ops[i].shape[0] // 2
            return pl.ds(cc * h, h)

        def copy(i, k, src, slot, cc, to):
            return pltpu.make_async_remote_copy(src_ref=src, dst_ref=dsts[i].at[slot, half(i, cc)], send_sem=send_sems.at[i * 6 + k],
                                                recv_sem=recv_sems.at[i * 6 + k], device_id=to, device_id_type=pl.DeviceIdType.MESH)

        local = [pltpu.make_async_copy(srcs[i], dsts[i].at[mine], local_sems.at[i]) for i in range(n)]
        first, ici_recvs, passed, sib_recvs = [], [], [], []
        for i in range(n):
            my_half = srcs[i].at[half(i, c)]
            for k, (px, py) in enumerate(chips):
                slot = 2 * px + py
                first.append(copy(i, k, my_half, mine, c, (px, py, c)))
                ici_recvs.append(copy(i, k, my_half, slot, c, me_id))
                passed.append(copy(i, 3 + k, dsts[i].at[slot, half(i, c)], slot, c, sibling))
                sib_recvs.append(copy(i, 3 + k, my_half, slot, 1 - c, me_id))
        return local, first, ici_recvs, passed, sib_recvs

    def start(self, srcs, dsts, sems):
        local, first, _, _, _ = self._copies(srcs, dsts, sems)
        for cp in local + first:
            cp.start()

    def forward(self, srcs, dsts, sems):
        _, _, ici_recvs, passed, _ = self._copies(srcs, dsts, sems)
        for rc, fw in zip(ici_recvs, passed):
            rc.wait_recv()
            fw.start()

    def finish(self, srcs, dsts, sems):
        local, first, _, passed, sib_recvs = self._copies(srcs, dsts, sems)
        for cp in sib_recvs:
            cp.wait_recv()
        for cp in first + passed:
            cp.wait_send()
        for cp in local:
            cp.wait()


class ExchangeComm:
    def __init__(self, chip_ops, all_ops=()):
        self.ops = list(chip_ops) + list(all_ops)
        self.per_chip = (True,) * len(chip_ops) + (False,) * len(all_ops)
        n = len(self.ops)
        self.out_shapes = [jax.ShapeDtypeStruct((N_DEV,) + o.shape[-2:], o.dtype) for o in self.ops]
        self.sem_shapes = [pltpu.SemaphoreType.DMA((7 * n,)), pltpu.SemaphoreType.DMA((7 * n,)), pltpu.SemaphoreType.DMA((n,))]

    def _copies(self, srcs, dsts, sems):
        send_sems, recv_sems, local_sems = sems
        n, per_chip = len(self.ops), self.per_chip
        x, y, c = _mesh_pos()
        me_id, sibling = (x, y, c), (x, y, 1 - c)
        chips = [(1 - x, y), (x, 1 - y), (1 - x, 1 - y)]

        def dev(px, py, pc):
            return 4 * px + 2 * py + pc

        def part(i, px, py):
            return srcs[i].at[2 * px + py] if per_chip[i] else srcs[i]

        def copy(i, k, src, slot, to):
            return pltpu.make_async_remote_copy(src_ref=src, dst_ref=dsts[i].at[slot], send_sem=send_sems.at[i * 7 + k],
                                                recv_sem=recv_sems.at[i * 7 + k], device_id=to, device_id_type=pl.DeviceIdType.MESH)

        me = dev(x, y, c)
        local = [pltpu.make_async_copy(part(i, x, y), dsts[i].at[me], local_sems.at[i]) for i in range(n)]
        first, ici_recvs, passed, sib_recvs = [], [], [], []
        for i in range(n):
            first.append(copy(i, 0, part(i, x, y), me, sibling))
            first += [copy(i, 1 + k, part(i, px, py), me, (px, py, c)) for k, (px, py) in enumerate(chips)]
            sib_recvs.append(copy(i, 0, part(i, x, y), dev(x, y, 1 - c), me_id))
            for k, (px, py) in enumerate(chips):
                slot = dev(px, py, c)
                ici_recvs.append(copy(i, 1 + k, part(i, x, y), slot, me_id))
                passed.append(copy(i, 4 + k, dsts[i].at[slot], slot, sibling))
                sib_recvs.append(copy(i, 4 + k, part(i, x, y), dev(px, py, 1 - c), me_id))
        return local, first, ici_recvs, passed, sib_recvs

    def start(self, srcs, dsts, sems):
        local, first, _, _, _ = self._copies(srcs, dsts, sems)
        for cp in local + first:
            cp.start()

    def forward(self, srcs, dsts, sems):
        _, _, ici_recvs, passed, _ = self._copies(srcs, dsts, sems)
        for rc, fw in zip(ici_recvs, passed):
            rc.wait_recv()
            fw.start()

    def finish(self, srcs, dsts, sems):
        local, first, _, passed, sib_recvs = self._copies(srcs, dsts, sems)
        for cp in sib_recvs:
            cp.wait_recv()
        for cp in first + passed:
            cp.wait_send()
        for cp in local:
            cp.wait()


def _run_comm(name, comm):
    n = len(comm.ops)

    def body(*refs):
        srcs, dsts, sems = refs[:n], refs[n:2 * n], refs[2 * n:]
        comm.start(srcs, dsts, sems)
        comm.forward(srcs, dsts, sems)
        comm.finish(srcs, dsts, sems)

    any_spec = pl.BlockSpec(memory_space=pl.ANY)
    return pl.pallas_call(body, name=name, in_specs=[any_spec] * n, out_specs=[any_spec] * n, out_shape=comm.out_shapes,
                          scratch_shapes=comm.sem_shapes)(*comm.ops)


def _adamw(name, parts, w, m, v, lead=False, comm=None):
    plist = list(parts) if isinstance(parts, (list, tuple)) else [parts]
    n_p = len(plist)
    rows, cols = w.shape[-2:]
    t = next(c for c in (ADAM_ROWS, ADAM_ROWS // 2, SMALL_ROWS) if all(p.shape[1] % c == 0 for p in plist))
    starts = [sum(p.shape[1] for p in plist[:k]) // t for k in range(n_p)]
    counts = [p.shape[1] // t for p in plist]
    assert sum(p.shape[1] for p in plist) == rows, (name, rows)
    c1 = 1.0 - ADAM_B1 ** ADAM_STEP
    c2 = 1.0 - ADAM_B2 ** ADAM_STEP

    n_c = len(comm.ops) if comm is not None else 0
    n_steps = rows // t

    def body(*refs):
        p_refs = refs[:n_p]
        w_ref, m_ref, v_ref = refs[n_p:n_p + 3]
        c_src = refs[n_p + 3:n_p + 3 + n_c]
        g_ref, d_ref, nm_ref, nv_ref = refs[n_p + 3 + n_c:n_p + 7 + n_c]
        cargs = (c_src, refs[n_p + 7 + n_c:n_p + 7 + 2 * n_c], refs[n_p + 7 + 2 * n_c:])
        if comm is not None:
            @pl.when(pl.program_id(0) == 0)
            def _():
                comm.start(*cargs)

            @pl.when(pl.program_id(0) == _forward_step(n_steps))
            def _():
                comm.forward(*cargs)
        g = None
        for k, p_ref in enumerate(p_refs):
            gk = p_ref[0].astype(F32)
            for d in range(1, N_DEV):
                gk = gk + p_ref[d].astype(F32)
            g = gk if g is None else jnp.where(pl.program_id(0) >= starts[k], gk, g)
        nm = ADAM_B1 * m_ref[...] + (1.0 - ADAM_B1) * g
        nv = ADAM_B2 * v_ref[...] + (1.0 - ADAM_B2) * (g * g)
        g_ref[...] = g
        nm_ref[...] = nm
        nv_ref[...] = nv
        d_ref[...] = -ADAM_LR * ((nm / c1) / (jnp.sqrt(nv / c2) + ADAM_EPS) + ADAM_WD * w_ref[...])
        if comm is not None:
            @pl.when(pl.program_id(0) == n_steps - 1)
            def _():
                comm.finish(*cargs)

    row = pl.BlockSpec((None, t, cols), lambda i: (0, i, 0)) if lead else pl.BlockSpec((t, cols), lambda i: (i, 0))
    c_in_specs, c_args, c_out_specs, c_out_shapes, c_sems = _carry_specs(comm)
    res = pl.pallas_call(
        body, name=name, grid=(n_steps,),
        in_specs=[pl.BlockSpec((N_DEV, t, cols), lambda i, lo=lo, n=n: (0, jnp.clip(i - lo, 0, n - 1), 0)) for lo, n in zip(starts, counts)]
        + [row, row, row] + c_in_specs,
        out_specs=[row] * 4 + c_out_specs, out_shape=[jax.ShapeDtypeStruct(w.shape, F32)] * 4 + c_out_shapes, scratch_shapes=c_sems,
        compiler_params=_CP(dimension_semantics=("arbitrary" if comm is not None else "parallel",)),
    )(*plist, w, m, v, *c_args)
    return list(res[:4]), list(res[4:])


def _train_step(x, positions, target, wts, ms, vs, raw):
    small_shapes = [wts[n].shape for n in SMALL]

    big_of = {tag: [n for n in BIG if n.startswith(tag)] for tag in ("mla", "gla", "lru", "ssd")}
    full = {n: wts[n] for n in REPL}

    def gather_comm(names, extra=()):
        return GatherComm([wts[n].astype(BF16) for n in names] + list(extra))

    def assemble(names, got):
        for k, n in enumerate(names):
            full[n] = jnp.concatenate([got[k][j] for j in range(N_CHIPS)], axis=_shard_axis(n))

    first = [n for n in big_of["mla"] if n != "mla_w_out"]
    got = _run_comm("gather_first", gather_comm(first, [_pack([wts[n] for n in SMALL], F32, SMALL_ROWS)]))
    assemble(first, got)
    per_chip_small = [_unpack(got[-1][j], small_shapes) for j in range(N_CHIPS)]
    for k, n in enumerate(SMALL):
        full[n] = jnp.concatenate([per_chip_small[j][k] for j in range(N_CHIPS)], axis=_shard_axis(n))

    cos, sin = _rope_tables(positions)
    ng = full["norm_g"]
    behind_attn = ["mla_w_out"] + big_of["gla"] + big_of["lru"]

    def mla_w_out(got):
        assemble(behind_attn, got)
        return full["mla_w_out"].astype(BF16)

    h1, b0, _ = _mla_layer(x, ng[0], full, cos, sin, fwd_comm=gather_comm(behind_attn), late_w_out=mla_w_out)

    def joined(got_k, axis):
        return jnp.concatenate([got_k[j] for j in range(N_CHIPS)], axis=axis)

    ssd_in = wts["ssd_w_in"].astype(BF16)
    half = ssd_in.shape[0] // 2
    h2, b1, got = _gla_layer(h1, ng[1], full, fwd_comm=GatherComm([ssd_in[:half]]))
    top = joined(got[0], -1)
    h3, b2, got = _lru_layer(h2, ng[2], full, fwd_comm=GatherComm([ssd_in[half:]]))
    full["ssd_w_in"] = jnp.concatenate([top, joined(got[0], -1)], axis=0)
    h4, b3 = _ssd_layer(h3, ng[3], full, fwd_comm=gather_comm(["ssd_w_out"]), late_w_out=lambda got: joined(got[0], 0))
    loss, dh, d_final = _loss_op(h4, target, full["final_g"])
    loss = loss[0, 0]
    grads = {"final_g": d_final.reshape(-1)}
    d_norms = [None] * 4

    def shards_of(n, g):
        return jnp.stack(jnp.split(g.astype(BF16), N_CHIPS, axis=_shard_axis(n)))

    def shards(n):
        return shards_of(n, grads[n])

    parts = {}
    dh, d_norms[3], gw, got = b3(dh, make_comm=lambda dw: ExchangeComm([shards_of("ssd_w_out", dw)]))
    parts["ssd_w_out"] = got[0]
    grads.update(gw)
    ssd_in_g = shards("ssd_w_in")
    half = ssd_in_g.shape[1] // 2
    dh, d_norms[2], gw, got = b2(dh, make_comm=lambda dw: ExchangeComm([ssd_in_g[:, :half], shards_of("lru_w_out", dw)]))
    parts["ssd_w_in"], parts["lru_w_out"] = [got[0]], got[1]
    grads.update(gw)
    dh, d_norms[1], gw, got = b1(dh, make_comm=lambda dw: ExchangeComm([ssd_in_g[:, half:]]))
    parts["ssd_w_in"].append(got[0])
    grads.update(gw)
    repl_early = _pack([grads[n] for n in REPL_EARLY], BF16, SMALL_ROWS)

    behind_attn = ["gla_w_out", "lru_w_in", "gla_w_in"]
    dx, d_norms[0], gw, got = b0(dh, make_comm=lambda dw: ExchangeComm([shards_of("mla_w_out", dw)] + [shards(n) for n in behind_attn],
                                                                        [repl_early]))
    parts.update(zip(["mla_w_out"] + behind_attn, got))
    repl_early_parts = got[-1]
    grads.update(gw)
    grads["norm_g"] = jnp.concatenate(d_norms, axis=0)
    psmall = jnp.stack([_pack([jnp.split(grads[n], N_CHIPS, axis=_shard_axis(n))[j] for n in SMALL], F32, SMALL_ROWS) for j in range(N_CHIPS)])
    prepl = _pack([grads[n] for n in REPL_LATE], F32, SMALL_ROWS)

    out = {}
    kinds = ("grad", "delta", "new_m", "new_v")
    late = [n for n in big_of["mla"] if n != "mla_w_out"]
    late_comm = ExchangeComm([shards(n) for n in late] + [psmall], [prepl])
    for n in ["ssd_w_in"] + [n for n in BIG if n != "ssd_w_in"]:
        if n == "ssd_w_in":
            res, late_parts = _adamw("adam_" + n, parts[n], *(r[n] for r in raw), lead=True, comm=late_comm)
            parts.update(zip(late, late_parts))
        else:
            res, _ = _adamw("adam_" + n, parts[n], *(r[n] for r in raw), lead=True)
        for kind, a in zip(kinds, res):
            out[kind, n] = a
    for tag, names, p in (("adam_small", SMALL, late_parts[-2]), ("adam_repl_early", REPL_EARLY, repl_early_parts),
                          ("adam_repl_late", REPL_LATE, late_parts[-1])):
        shapes = [wts[n].shape for n in names]
        packed = [_pack([d[n] for n in names], F32, SMALL_ROWS) for d in (wts, ms, vs)]
        for kind, buf in zip(kinds, _adamw(tag, p, *packed)[0]):
            for n, a in zip(names, _unpack(buf, shapes)):
                out[kind, n] = a
    loss = lax.psum(loss, ("x", "y", "c"))
    return loss, dx, out


def kernel(x, positions, norm_g, final_g, mla_w_in, mla_g_q, mla_w_uq, mla_g_kv, mla_w_ukv, mla_w_out, gla_w_in, gla_w_gk2, gla_b_gk, gla_g_o, gla_w_out, lru_w_in, lru_conv_w, lru_conv_b, lru_w_a, lru_b_a, lru_w_x, lru_b_x, lru_lam, lru_w_out, ssd_w_in, ssd_conv_w, ssd_conv_b, ssd_dt_bias, ssd_a_log, ssd_d, ssd_g_norm, ssd_w_out, loss_target, m_norm_g, m_final_g, m_mla_w_in, m_mla_g_q, m_mla_w_uq, m_mla_g_kv, m_mla_w_ukv, m_mla_w_out, m_gla_w_in, m_gla_w_gk2, m_gla_b_gk, m_gla_g_o, m_gla_w_out, m_lru_w_in, m_lru_conv_w, m_lru_conv_b, m_lru_w_a, m_lru_b_a, m_lru_w_x, m_lru_b_x, m_lru_lam, m_lru_w_out, m_ssd_w_in, m_ssd_conv_w, m_ssd_conv_b, m_ssd_dt_bias, m_ssd_a_log, m_ssd_d, m_ssd_g_norm, m_ssd_w_out, v_norm_g, v_final_g, v_mla_w_in, v_mla_g_q, v_mla_w_uq, v_mla_g_kv, v_mla_w_ukv, v_mla_w_out, v_gla_w_in, v_gla_w_gk2, v_gla_b_gk, v_gla_g_o, v_gla_w_out, v_lru_w_in, v_lru_conv_w, v_lru_conv_b, v_lru_w_a, v_lru_b_a, v_lru_w_x, v_lru_b_x, v_lru_lam, v_lru_w_out, v_ssd_w_in, v_ssd_conv_w, v_ssd_conv_b, v_ssd_dt_bias, v_ssd_a_log, v_ssd_d, v_ssd_g_norm, v_ssd_w_out):
    given = dict(locals())
    stacked = [n for n in WEIGHTS if n not in ("norm_g", "final_g")]

    def blocks(prefix):
        return {n: (given[prefix + n][0] if n in stacked else given[prefix + n]) for n in WEIGHTS}

    raw = [{n: given[prefix + n] for n in BIG} for prefix in ("", "m_", "v_")]
    loss, dx, out = _train_step(x[0], positions[0], loss_target[0], blocks(""), blocks("m_"), blocks("v_"), raw)
    res = [loss, dx[None]]
    for kind in ("grad", "delta", "new_m", "new_v"):
        res += [(out[kind, n][None] if n in stacked and n not in BIG else out[kind, n]) for n in WEIGHTS]
    return tuple(res)
```

```python
import functools
import math

import jax
import jax.numpy as jnp
from jax import lax
from jax.experimental import pallas as pl
from jax.experimental.pallas import tpu as pltpu

F32 = jnp.float32
BF16 = jnp.bfloat16

V7X_VMEM_BYTES = 64 * 1024 * 1024
VMEM_LIMIT = V7X_VMEM_BYTES - 8 * 1024 * 1024
LANE = 128

D_MODEL = 1024
NORM_EPS = 1e-6
MLA_HEADS, MLA_Q_RANK, MLA_KV_RANK, MLA_NOPE, MLA_ROPE, MLA_V = 16, 384, 256, 64, 32, 64
MLA_QK = MLA_NOPE + MLA_ROPE
ROPE_THETA = 10000.0
GLA_HEADS, GLA_DK, GLA_DV, GLA_RANK, GLA_TAU, GLA_CHUNK = 4, 128, 256, 16, 16.0, 64
LRU_WIDTH, LRU_BLOCKS, LRU_BLOCK, LRU_C, CONV_W = 1280, 10, 128, 8.0, 4
SSD_INNER, SSD_P, SSD_HEADS, SSD_GROUPS, SSD_HPG, SSD_STATE, SSD_CHUNK = 2048, 64, 32, 8, 4, 128, 64
ADAM_LR, ADAM_B1, ADAM_B2, ADAM_EPS, ADAM_WD, ADAM_STEP = 0.001, 0.9, 0.999, 1e-08, 0.01, 10

_CP = functools.partial(pltpu.CompilerParams, vmem_limit_bytes=VMEM_LIMIT)


def _bdot(a, b):
    return jnp.dot(a.astype(BF16), b.astype(BF16), preferred_element_type=F32)


def _bdot_nt(a, b):
    return lax.dot_general(a.astype(BF16), b.astype(BF16), (((1,), (1,)), ((), ())), preferred_element_type=F32)


def _bdot_tn(a, b):
    return lax.dot_general(a.astype(BF16), b.astype(BF16), (((0,), (0,)), ((), ())), preferred_element_type=F32)


def _tri(n):
    r = lax.broadcasted_iota(jnp.int32, (n, n), 0)
    c = lax.broadcasted_iota(jnp.int32, (n, n), 1)
    return r >= c


def _rms(x, g):
    return x * lax.rsqrt(jnp.mean(x * x, axis=-1, keepdims=True) + NORM_EPS) * g


def _silu(x):
    return x * jax.nn.sigmoid(x)


def _shift_rows(x, prev, j):
    if j == 0:
        return x
    t = x.shape[0]

    def fwd_impl(x, prev):
        row = lax.broadcasted_iota(jnp.int32, x.shape, 0)
        return jnp.where(row >= j, pltpu.roll(x, j, 0), pltpu.roll(prev, j, 0))

    @jax.custom_vjp
    def sh(x, prev):
        return fwd_impl(x, prev)

    def sh_fwd(x, prev):
        return fwd_impl(x, prev), None

    def sh_bwd(_, gy):
        row = lax.broadcasted_iota(jnp.int32, gy.shape, 0)
        back = pltpu.roll(gy, t - j, 0)
        return jnp.where(row < t - j, back, 0.0), jnp.where(row >= t - j, back, 0.0)

    sh.defvjp(sh_fwd, sh_bwd)
    return sh(x, prev)


def _cumsum_rows(x):
    zero = jnp.zeros_like(x)
    sh = 1
    while sh < x.shape[0]:
        x = x + _shift_rows(x, zero, sh)
        sh *= 2
    return x


def _one_minus_exp(x):
    series = -x * (1.0 + x * (0.5 + x * (1.0 / 6.0 + x * (1.0 / 24.0 + x * (1.0 / 120.0)))))
    return jnp.where(x > -0.05, series, 1.0 - jnp.exp(x))


def _tile(n, cap):
    if n <= cap:
        return n
    best = None
    for t in range(LANE, cap + 1, LANE):
        if n % t == 0:
            best = t
    assert best is not None, (n, cap)
    return best


MM_BLOCK_BYTES = 8 * 1024 * 1024
MM_ROWS, MM_KROWS = 256, 512
MM_TILE_BYTES = 2 * 1024 * 1024


def _mm_tiles(m, k, n, ta):
    if ta:
        return m, _tile(n, max(LANE, MM_BLOCK_BYTES // (4 * m) // LANE * LANE)), _tile(k, MM_KROWS)
    tn = _tile(n, max(LANE, MM_BLOCK_BYTES // (2 * k) // LANE * LANE))
    rows = min(4 * MM_ROWS, max(MM_ROWS, MM_TILE_BYTES // (4 * tn) // MM_ROWS * MM_ROWS))
    return _tile(m, rows), tn, k


def _mm(name, a, b, *, ta=False, tb=False, add=None, out_dtype=F32):
    m, k = (a.shape[1], a.shape[0]) if ta else a.shape
    n, kb = (b.shape[0], b.shape[1]) if tb else (b.shape[1], b.shape[0])
    assert k == kb, (name, a.shape, b.shape, ta, tb)
    tm, tn, tk = _mm_tiles(m, k, n, ta)
    nk = k // tk
    dn = (((0 if ta else 1,), (1 if tb else 0,)), ((), ()))
    has_add = add is not None

    def finish(refs, r):
        if has_add:
            r = r + refs[2][...].astype(F32)
        return r.astype(out_dtype)

    def body_one(*refs):
        a_ref, b_ref, o_ref = refs[0], refs[1], refs[-1]
        o_ref[...] = finish(refs, lax.dot_general(a_ref[...].astype(BF16), b_ref[...].astype(BF16), dn, preferred_element_type=F32))

    def body_acc(*refs):
        a_ref, b_ref = refs[0], refs[1]
        o_ref, acc = refs[-2], refs[-1]
        kk = pl.program_id(2)

        @pl.when(kk == 0)
        def _():
            acc[...] = jnp.zeros(acc.shape, F32)

        acc[...] += lax.dot_general(a_ref[...].astype(BF16), b_ref[...].astype(BF16), dn, preferred_element_type=F32)

        @pl.when(kk == nk - 1)
        def _():
            o_ref[...] = finish(refs, acc[...])

    a_spec = pl.BlockSpec((tk, tm), lambda i, j, q: (q, i)) if ta else pl.BlockSpec((tm, tk), lambda i, j, q: (i, q))
    b_spec = pl.BlockSpec((tn, tk), lambda i, j, q: (j, q)) if tb else pl.BlockSpec((tk, tn), lambda i, j, q: (q, j))
    o_spec = pl.BlockSpec((tm, tn), lambda i, j, q: (i, j))
    in_specs, args = [a_spec, b_spec], [a, b]
    if has_add:
        in_specs.append(o_spec)
        args.append(add)
    return pl.pallas_call(
        body_one if nk == 1 else body_acc, name=name, grid=(m // tm, n // tn, nk), in_specs=in_specs, out_specs=o_spec,
        out_shape=jax.ShapeDtypeStruct((m, n), out_dtype), scratch_shapes=[] if nk == 1 else [pltpu.VMEM((tm, tn), F32)],
        compiler_params=_CP(dimension_semantics=("parallel", "parallel", "arbitrary")),
    )(*args)


def _mm_pieces(name, a, bs, gain=None):
    m, k = a.shape
    n_b = len(bs)
    tm = _tile(m, MM_ROWS)
    fused = gain is not None
    n_lead = 2 if fused else 1

    def body(*refs):
        if fused:
            av = _rms(refs[0][...].astype(F32), refs[1][...]).astype(BF16)
            refs[-1][...] = av
        else:
            av = refs[0][...].astype(BF16)
        for b_ref, o_ref in zip(refs[n_lead:n_lead + n_b], refs[n_lead + n_b:n_lead + 2 * n_b]):
            o_ref[...] = jnp.dot(av, b_ref[...].astype(BF16), preferred_element_type=F32)

    row = pl.BlockSpec((tm, k), lambda i: (i, 0))
    res = pl.pallas_call(
        body, name=name, grid=(m // tm,),
        in_specs=[row] + ([pl.BlockSpec((1, k), lambda i: (0, 0))] if fused else []) + [pl.BlockSpec(b.shape, lambda i: (0, 0)) for b in bs],
        out_specs=[pl.BlockSpec((tm, b.shape[1]), lambda i: (i, 0)) for b in bs] + ([row] if fused else []),
        out_shape=[jax.ShapeDtypeStruct((m, b.shape[1]), F32) for b in bs] + ([jax.ShapeDtypeStruct((m, k), BF16)] if fused else []),
        compiler_params=_CP(dimension_semantics=("parallel",)),
    )(a, *([gain.reshape(1, k)] if fused else []), *bs)
    return (list(res[:n_b]), res[n_b]) if fused else list(res)


def _mm_pieces_t(name, dps, ws, norm=None):
    m, k = dps[0].shape[0], ws[0].shape[0]
    n_b = len(ws)
    tm = _tile(m, MM_ROWS)
    dn = (((1,), (1,)), ((), ()))
    fused = norm is not None

    def body(*refs):
        acc = None
        for d_ref, w_ref in zip(refs[:n_b], refs[n_b:2 * n_b]):
            part = lax.dot_general(d_ref[...].astype(BF16), w_ref[...].astype(BF16), dn, preferred_element_type=F32)
            acc = part if acc is None else acc + part
        if not fused:
            refs[-1][...] = acc
            return
        x_ref, g_ref, add_ref, dx_ref, dg_ref = refs[2 * n_b:]
        _, vjp = jax.vjp(_rms, x_ref[...], g_ref[...])
        dx, dg = vjp(acc)
        dx_ref[...] = dx + add_ref[...]

        @pl.when(pl.program_id(0) == 0)
        def _():
            dg_ref[...] = dg

        @pl.when(pl.program_id(0) > 0)
        def _():
            dg_ref[...] += dg

    row = pl.BlockSpec((tm, k), lambda i: (i, 0))
    one = pl.BlockSpec((1, k), lambda i: (0, 0))
    in_specs = [pl.BlockSpec((tm, d.shape[1]), lambda i: (i, 0)) for d in dps] + [pl.BlockSpec(w.shape, lambda i: (0, 0)) for w in ws]
    if not fused:
        return pl.pallas_call(body, name=name, grid=(m // tm,), in_specs=in_specs, out_specs=row, out_shape=jax.ShapeDtypeStruct((m, k), F32),
                              compiler_params=_CP(dimension_semantics=("parallel",)))(*dps, *ws)
    x, gain, add = norm
    return pl.pallas_call(
        body, name=name, grid=(m // tm,), in_specs=in_specs + [row, one, row], out_specs=[row, one],
        out_shape=[jax.ShapeDtypeStruct((m, k), F32), jax.ShapeDtypeStruct((1, k), F32)],
        compiler_params=_CP(dimension_semantics=("arbitrary",)),
    )(*dps, *ws, x, gain.reshape(1, k), add)


def _mm_pieces_dw(name, u, dps):
    s, k = u.shape
    n_b = len(dps)
    tk = _tile(s, MM_KROWS)
    n_steps = s // tk
    dn = (((0,), (0,)), ((), ()))

    def body(*refs):
        u_ref, d_refs, o_refs, accs = refs[0], refs[1:1 + n_b], refs[1 + n_b:1 + 2 * n_b], refs[1 + 2 * n_b:]
        step = pl.program_id(0)
        uv = u_ref[...].astype(BF16)
        for d_ref, o_ref, acc in zip(d_refs, o_refs, accs):
            part = lax.dot_general(uv, d_ref[...].astype(BF16), dn, preferred_element_type=F32)

            @pl.when(step == 0)
            def _(acc=acc, part=part):
                acc[...] = part

            @pl.when(step > 0)
            def _(acc=acc, part=part):
                acc[...] += part

            @pl.when(step == n_steps - 1)
            def _(acc=acc, o_ref=o_ref):
                o_ref[...] = acc[...].astype(o_ref.dtype)

    return pl.pallas_call(
        body, name=name, grid=(n_steps,),
        in_specs=[pl.BlockSpec((tk, k), lambda i: (i, 0))] + [pl.BlockSpec((tk, d.shape[1]), lambda i: (i, 0)) for d in dps],
        out_specs=[pl.BlockSpec((k, d.shape[1]), lambda i: (0, 0)) for d in dps],
        out_shape=[jax.ShapeDtypeStruct((k, d.shape[1]), BF16) for d in dps],
        scratch_shapes=[pltpu.VMEM((k, d.shape[1]), F32) for d in dps],
        compiler_params=_CP(dimension_semantics=("arbitrary",)),
    )(u, *dps)


MM_DW_MERGE_COLS = 4096


class In:
    def __init__(self, arr, block, imap, kind="x", per_h=False, gdtype=F32, gshape=None, gimap=None, prefixed=False):
        self.arr, self.block, self.imap, self.kind, self.per_h, self.gdtype = arr, tuple(block), imap, kind, per_h, gdtype
        self.prefixed = prefixed
        self.gshape = tuple(gshape) if gshape is not None else tuple(arr.shape)
        self.gimap = gimap if gimap is not None else imap

    def spec(self, rev_g=None):
        imap = self.imap
        if rev_g is None:
            return pl.BlockSpec(self.block, lambda h, g: imap(h, g))
        return pl.BlockSpec(self.block, lambda h, g: imap(h, rev_g - 1 - g))


class Out:
    def __init__(self, shape, dtype, block, imap):
        self.shape, self.dtype, self.block, self.imap = tuple(shape), dtype, tuple(block), imap

    def spec(self, rev_g=None):
        imap = self.imap
        if rev_g is None:
            return pl.BlockSpec(self.block, lambda h, g: imap(h, g))
        return pl.BlockSpec(self.block, lambda h, g: imap(h, rev_g - 1 - g))


def _load_f32(ref, rows=None):
    v = ref[...] if rows is None else ref[0:rows]
    return v.astype(F32) if jnp.issubdtype(v.dtype, jnp.floating) else v


def _state_out(grid, shape):
    nd = len(shape)
    return Out(tuple(grid) + tuple(shape), F32, (None, None) + tuple(shape), lambda h, g: (h, g) + (0,) * nd)


def _carry(comm, grid, refs, n_in, n_out, n_scr):
    n_c = len(comm.ops) if comm is not None else 0
    n_s = len(comm.sem_shapes) if comm is not None else 0
    p = 0
    in_refs = refs[p:p + n_in]; p += n_in
    c_src = refs[p:p + n_c]; p += n_c
    out_refs = refs[p:p + n_out]; p += n_out
    c_dst = refs[p:p + n_c]; p += n_c
    scr = refs[p:p + n_scr]; p += n_scr
    c_sem = refs[p:p + n_s]
    step = pl.program_id(0) * grid[1] + pl.program_id(1)
    n_steps = grid[0] * grid[1]
    when = (step == 0, step == _forward_step(n_steps), step == n_steps - 1)
    return in_refs, out_refs, scr, (c_src, c_dst, c_sem), when


def _forward_step(n_steps):
    return max(0, min(n_steps - 2, (3 * n_steps) // 4))


def _carry_specs(comm):
    if comm is None:
        return [], [], [], [], []
    any_spec = pl.BlockSpec(memory_space=pl.ANY)
    n = len(comm.ops)
    return [any_spec] * n, list(comm.ops), [any_spec] * n, list(comm.out_shapes), list(comm.sem_shapes)


def _op_fwd(name, f, grid, ins, outs, state_shapes=(), comm=None, prefix_rows=None):
    assert prefix_rows is None or not state_shapes
    n_in, n_out, n_st = len(ins), len(outs), len(state_shapes)
    st_outs = [_state_out(grid, s) for s in state_shapes]

    def body(*refs):
        in_refs, o_refs, st_scr, cargs, (first, fwd_step, last) = _carry(comm, grid, refs, n_in, n_out + n_st, n_st)
        out_refs, sv_refs = o_refs[:n_out], o_refs[n_out:]
        if comm is not None:
            @pl.when(first)
            def _():
                comm.start(*cargs)

            @pl.when(fwd_step)
            def _():
                comm.forward(*cargs)
        g = pl.program_id(1)
        if n_st:
            @pl.when(g == 0)
            def _():
                for s in st_scr:
                    s[...] = jnp.zeros(s.shape, F32)

        def compute(rows, g=g):
            vals = [_load_f32(r, rows if i.prefixed else None) for r, i in zip(in_refs, ins)]
            sts = [s[...] for s in st_scr]
            o, ns = f(g, vals, sts)
            for r, v in zip(out_refs, o):
                r[...] = v.astype(r.dtype)
            for r, s in zip(sv_refs, sts):
                r[...] = s
            for s, v in zip(st_scr, ns):
                s[...] = v

        if prefix_rows is None:
            compute(None)
        else:
            per = grid[1] // len(prefix_rows)
            for lv, rows in enumerate(prefix_rows):
                pl.when(g // per == lv)(functools.partial(compute, rows, lv) if per == 1 else functools.partial(compute, rows))
        if comm is not None:
            @pl.when(last)
            def _():
                comm.finish(*cargs)

    all_outs = list(outs) + st_outs
    c_in_specs, c_args, c_out_specs, c_out_shapes, c_sems = _carry_specs(comm)
    res = pl.pallas_call(
        body, name=name, grid=tuple(grid), in_specs=[i.spec() for i in ins] + c_in_specs,
        out_specs=[o.spec() for o in all_outs] + c_out_specs,
        out_shape=[jax.ShapeDtypeStruct(o.shape, o.dtype) for o in all_outs] + c_out_shapes,
        scratch_shapes=[pltpu.VMEM(tuple(s), F32) for s in state_shapes] + c_sems,
        compiler_params=_CP(dimension_semantics=("arbitrary", "arbitrary")),
    )(*[i.arr for i in ins], *c_args)
    n_all = n_out + n_st
    return list(res[:n_out]), list(res[n_out:n_all]), list(res[n_all:])


def _op_bwd(name, f, grid, ins, outs, state_shapes, saved, douts, addto=None, comm=None, prefix_rows=None):
    n_in, n_out, n_st = len(ins), len(outs), len(state_shapes)
    n_g = grid[1]
    assert prefix_rows is None or all(i.prefixed and i.per_h for i in ins if i.kind == "p")
    addto = addto or {}
    diff = [k for k, i in enumerate(ins) if i.kind in ("x", "p")]
    add_idx = sorted(addto)
    st_ins = [In(s, o.block, o.imap, "c") for s, o in zip(saved, [_state_out(grid, s) for s in state_shapes])]
    dout_ins = [In(d, o.block, o.imap, "c") for d, o in zip(douts, outs)]
    add_ins = []
    for k in add_idx:
        i, a = ins[k], addto[k]
        blk = i.block if i.kind == "x" else i.block[:-2] + a.shape[-2:]
        add_ins.append(In(a, blk, i.gimap if i.kind == "x" else i.imap, "c"))
    g_outs = []
    for k in diff:
        i = ins[k]
        g_outs.append(Out(i.gshape, i.gdtype if i.kind == "x" else F32, i.block, i.gimap))

    def body(*refs):
        all_in, go_refs, ds_scr, cargs, (first_step, fwd_step, last_step) = _carry(comm, grid, refs, n_in + n_st + n_out + len(add_idx),
                                                                                    len(diff), n_st)
        if comm is not None:
            @pl.when(first_step)
            def _():
                comm.start(*cargs)

            @pl.when(fwd_step)
            def _():
                comm.forward(*cargs)
        p = 0
        in_refs = all_in[p:p + n_in]; p += n_in
        sv_refs = all_in[p:p + n_st]; p += n_st
        do_refs = all_in[p:p + n_out]; p += n_out
        ad_refs = all_in[p:p + len(add_idx)]
        hh = pl.program_id(0)
        step = pl.program_id(1)
        g = n_g - 1 - step
        if n_st:
            @pl.when(step == 0)
            def _():
                for s in ds_scr:
                    s[...] = jnp.zeros(s.shape, F32)

        def compute(rows, g=g):
            vals = [_load_f32(r, rows if i.prefixed else None) for r, i in zip(in_refs, ins)]
            sts = [r[...] for r in sv_refs]

            def fw(dvals, states):
                full = list(vals)
                for k, v in zip(diff, dvals):
                    full[k] = v
                o, ns = f(g, full, states)
                return list(o), list(ns)

            _, vjp = jax.vjp(fw, [vals[k] for k in diff], sts)
            cts = [r[...].astype(F32) for r in do_refs]
            dns = [s[...] for s in ds_scr]
            dvals, dsts = vjp((cts, dns))
            adds = dict(zip(add_idx, ad_refs))
            for k, r, dv in zip(diff, go_refs, dvals):
                i = ins[k]
                if i.kind == "x":
                    if k in adds:
                        dv = dv + adds[k][...].astype(F32)
                    r[...] = dv.astype(r.dtype)
                elif rows is not None:
                    r[0:rows] += dv
                else:
                    first = (step == 0) if i.per_h else jnp.logical_and(step == 0, hh == 0)

                    @pl.when(first)
                    def _(r=r, dv=dv, k=k):
                        r[...] = dv
                        if k in adds:
                            lead = adds[k].shape[0]
                            r[0:lead] += adds[k][...]

                    @pl.when(jnp.logical_not(first))
                    def _(r=r, dv=dv):
                        r[...] += dv
            for s, v in zip(ds_scr, dsts):
                s[...] = v

        if prefix_rows is None:
            compute(None)
        else:
            @pl.when(step == 0)
            def _():
                for k, r in zip(diff, go_refs):
                    if ins[k].kind == "p":
                        r[...] = jnp.zeros(r.shape, F32)
            per = n_g // len(prefix_rows)
            for lv, rows in enumerate(prefix_rows):
                pl.when(g // per == lv)(functools.partial(compute, rows, lv) if per == 1 else functools.partial(compute, rows))
        if comm is not None:
            @pl.when(last_step)
            def _():
                comm.finish(*cargs)

    all_ins = list(ins) + st_ins + dout_ins + add_ins
    c_in_specs, c_args, c_out_specs, c_out_shapes, c_sems = _carry_specs(comm)
    res = pl.pallas_call(
        body, name=name, grid=tuple(grid), in_specs=[i.spec(n_g) for i in all_ins] + c_in_specs,
        out_specs=[o.spec(n_g) for o in g_outs] + c_out_specs,
        out_shape=[jax.ShapeDtypeStruct(o.shape, o.dtype) for o in g_outs] + c_out_shapes,
        scratch_shapes=[pltpu.VMEM(tuple(s), F32) for s in state_shapes] + c_sems,
        compiler_params=_CP(dimension_semantics=("arbitrary", "arbitrary")),
    )(*[i.arr for i in all_ins], *c_args)
    return list(res[:len(g_outs)]), list(res[len(g_outs):])


class Op:
    def __init__(self, name, f, grid, ins, outs, state_shapes=(), prefix_rows=None):
        self.name, self.f, self.grid, self.ins, self.outs, self.state_shapes = name, f, grid, ins, outs, state_shapes
        self.prefix_rows = prefix_rows
        self.saved = []

    def fwd(self, comm=None):
        res, self.saved, self.fwd_comm_out = _op_fwd(self.name + "_fwd", self.f, self.grid, self.ins, self.outs, self.state_shapes, comm,
                                                     self.prefix_rows)
        return res

    def bwd(self, douts, addto=None, comm=None):
        res, self.bwd_comm_out = _op_bwd(self.name + "_bwd", self.f, self.grid, self.ins, self.outs, self.state_shapes, self.saved, douts,
                                         addto, comm, self.prefix_rows)
        return res


def _rows(arr, t, kind="x", gdtype=F32):
    return In(arr, (t, arr.shape[1]), lambda h, g: (g, 0), kind, gdtype=gdtype)


def _whole(arr, kind="p"):
    nd = arr.ndim
    return In(arr, arr.shape, lambda h, g: (0,) * nd, kind)


def _rows_out(s, n, t, dtype):
    return Out((s, n), dtype, (t, n), lambda h, g: (g, 0))


ROW_T = 512
CONV_T = 256


def _rms_op(name, x, gain, out_dtype=BF16, gdtype=F32):
    s, n = x.shape

    def f(g, vals, sts):
        return [_rms(vals[0], vals[1])], []

    return Op(name, f, (1, s // ROW_T), [_rows(x, ROW_T, gdtype=gdtype), _whole(gain.reshape(1, n))], [_rows_out(s, n, ROW_T, out_dtype)])


def _mla_prep_op(qn, q1, q2, kn, kr, v, cos, sin):
    s = qn.shape[0]
    hd, half = MLA_HEADS, MLA_ROPE // 2

    def f(g, vals, sts):
        qn, q1, q2, kn, kr, v, cos, sin = vals
        cos_h, sin_h = jnp.tile(cos, (1, hd)), jnp.tile(sin, (1, hd))
        r1 = q1 * cos_h - q2 * sin_h
        r2 = q2 * cos_h + q1 * sin_h
        k1, k2 = kr[:, 0:half], kr[:, half:2 * half]
        kr1 = k1 * cos - k2 * sin
        kr2 = k2 * cos + k1 * sin
        zpad = jnp.zeros((qn.shape[0], LANE - MLA_QK), F32)
        qs, ks, vs = [], [], []
        for h in range(hd):
            a, b = h * MLA_NOPE, (h + 1) * MLA_NOPE
            c, d = h * half, (h + 1) * half
            qs.append(jnp.concatenate([qn[:, a:b], r1[:, c:d], r2[:, c:d], zpad], axis=1))
            ks.append(jnp.concatenate([kn[:, a:b], kr1, kr2, zpad], axis=1))
            vs.append(v[:, a:b])
        return [jnp.stack(qs, 0), jnp.stack(ks, 0), jnp.stack(vs, 0)], []

    ins = [_rows(qn, ROW_T, gdtype=BF16), _rows(q1, ROW_T, gdtype=BF16), _rows(q2, ROW_T, gdtype=BF16), _rows(kn, ROW_T, gdtype=BF16),
           _rows(kr, ROW_T, gdtype=BF16), _rows(v, ROW_T, gdtype=BF16), _rows(cos, ROW_T, "c"), _rows(sin, ROW_T, "c")]
    outs = [Out((hd, s, LANE), BF16, (hd, ROW_T, LANE), lambda h, g: (0, g, 0)),
            Out((hd, s, LANE), BF16, (hd, ROW_T, LANE), lambda h, g: (0, g, 0)),
            Out((hd, s, MLA_V), BF16, (hd, ROW_T, MLA_V), lambda h, g: (0, g, 0))]
    return Op("mla_prep", f, (1, s // ROW_T), ins, outs)


ATT_TQ = 256
ATT_LEVELS = 8


def _mla_attn_op(q, k, v):
    hd, s, _ = q.shape
    scale = MLA_QK ** -0.5

    def f(g, vals, sts):
        q, k, v = vals
        sc = _bdot_nt(q, k) * scale
        r = lax.broadcasted_iota(jnp.int32, sc.shape, 0) + g * ATT_TQ
        c = lax.broadcasted_iota(jnp.int32, sc.shape, 1)
        sc = jnp.where(r >= c, sc, -1e30)
        m = lax.stop_gradient(jnp.max(sc, axis=-1, keepdims=True))
        p = jnp.exp(sc - m)
        p = p * (1.0 / jnp.sum(p, axis=-1, keepdims=True))
        return [_bdot(p, v)], []

    ins = [In(q, (None, ATT_TQ, LANE), lambda h, g: (h, g, 0), "x", gdtype=BF16),
           In(k, (None, s, LANE), lambda h, g: (h, 0, 0), "p", per_h=True, prefixed=True),
           In(v, (None, s, MLA_V), lambda h, g: (h, 0, 0), "p", per_h=True, prefixed=True)]
    outs = [Out((hd, s, MLA_V), F32, (None, ATT_TQ, MLA_V), lambda h, g: (h, g, 0))]
    return Op("mla_attn", f, (hd, s // ATT_TQ), ins, outs, prefix_rows=[(lv + 1) * (s // ATT_LEVELS) for lv in range(ATT_LEVELS)])


def _mla_post_op(o, gate):
    hd, s, _ = o.shape

    def f(g, vals, sts):
        o, gate = vals
        cat = jnp.concatenate([o[h] for h in range(hd)], axis=1)
        return [cat * _silu(gate)], []

    ins = [In(o, (hd, ROW_T, MLA_V), lambda h, g: (0, g, 0), "x"), _rows(gate, ROW_T, gdtype=BF16)]
    return Op("mla_post", f, (1, s // ROW_T), ins, [_rows_out(s, hd * MLA_V, ROW_T, BF16)])


def _gla_gate_op(gk, w2, b):
    s = gk.shape[0]

    def f(g, vals, sts):
        gk, w2, b = vals
        return [jax.nn.log_sigmoid(_bdot(gk, w2) + b) / GLA_TAU], []

    ins = [_rows(gk, ROW_T, gdtype=BF16), _whole(w2), _whole(b)]
    return Op("gla_gate", f, (1, s // ROW_T), ins, [_rows_out(s, GLA_HEADS * GLA_DK, ROW_T, F32)])


def _gla_core_op(q, k, v, gate, la, g_o):
    s = q.shape[0]
    c, nh = GLA_CHUNK, GLA_HEADS

    def f(g, vals, sts):
        q, k, v, gate, la, g_o = vals
        tri = _tri(c)
        b = _cumsum_rows(la)
        b_last = jnp.sum(la, axis=0, keepdims=True)
        qt = q * (GLA_DK ** -0.5) * jnp.exp(b)
        kt = k * jnp.exp(-b)
        kd = k * jnp.exp(b_last - b)
        ys, new_sts = [], []
        for h in range(nh):
            ks, vs = slice(h * GLA_DK, (h + 1) * GLA_DK), slice(h * GLA_DV, (h + 1) * GLA_DV)
            att = jnp.where(tri, _bdot_nt(qt[:, ks], kt[:, ks]), 0.0)
            o = _bdot(att, v[:, vs]) + _bdot_nt(qt[:, ks], sts[h])
            new_sts.append(jnp.exp(b_last[:, ks]) * sts[h] + _bdot_tn(v[:, vs], kd[:, ks]))
            ys.append(_rms(o, g_o) * _silu(gate[:, vs]))
        return [jnp.concatenate(ys, axis=1)], new_sts

    ins = [_rows(q, c, gdtype=BF16), _rows(k, c, gdtype=BF16), _rows(v, c, gdtype=BF16), _rows(gate, c, gdtype=BF16), _rows(la, c), _whole(g_o)]
    outs = [_rows_out(s, nh * GLA_DV, c, BF16)]
    return Op("gla_core", f, (1, s // c), ins, outs, [(GLA_DV, GLA_DK)] * nh)


LRU_T = 256


def _lru_op(gate, u, conv_w, conv_b, w_a, b_a, w_x, b_x, lam):
    s, w = u.shape
    t = LRU_T

    def f(g, vals, sts):
        gate, u, cw, cb, w_a, b_a, w_x, b_x, lam = vals
        u_prev, h_prev = sts
        uc = cb
        for kk in range(CONV_W):
            uc = uc + cw[kk] * _shift_rows(u, u_prev, CONV_W - 1 - kk)
        ra, ri = [], []
        for n in range(LRU_BLOCKS):
            blk = uc[:, n * LRU_BLOCK:(n + 1) * LRU_BLOCK]
            ra.append(_bdot(blk, w_a[n]))
            ri.append(_bdot(blk, w_x[n]))
        r = jax.nn.sigmoid(jnp.concatenate(ra, axis=1) + b_a)
        i = jax.nn.sigmoid(jnp.concatenate(ri, axis=1) + b_x)
        log_a = -LRU_C * r * jax.nn.softplus(-lam)
        a = jnp.exp(log_a)
        bb = jnp.sqrt(_one_minus_exp(2.0 * log_a)) * (i * uc)
        zero = jnp.zeros_like(a)
        sh = 1
        while sh < t:
            a_s = _shift_rows(a - 1.0, zero, sh) + 1.0
            b_s = _shift_rows(bb, zero, sh)
            bb = a * b_s + bb
            a = a * a_s
            sh *= 2
        hs = bb + a * h_prev
        last = (lax.broadcasted_iota(jnp.int32, hs.shape, 0) == t - 1).astype(F32)
        h_last = jnp.sum(hs * last, axis=0, keepdims=True)
        return [hs * _silu(gate)], [u, h_last]

    ins = [_rows(gate, t, gdtype=BF16), _rows(u, t, gdtype=BF16), _whole(conv_w), _whole(conv_b), _whole(w_a), _whole(b_a), _whole(w_x),
           _whole(b_x), _whole(lam)]
    return Op("lru_core", f, (1, s // t), ins, [_rows_out(s, w, t, BF16)], [(t, w), (1, w)])


def _ssd_conv_op(xbc, conv_w, conv_b):
    s, w = xbc.shape
    t = CONV_T
    n_x, n_b = SSD_INNER, SSD_GROUPS * SSD_STATE

    def f(g, vals, sts):
        xbc, cw, cb = vals
        acc = cb
        for kk in range(CONV_W):
            acc = acc + cw[kk] * _shift_rows(xbc, sts[0], CONV_W - 1 - kk)
        y = _silu(acc)
        return [y[:, :n_x], y[:, n_x:n_x + n_b], y[:, n_x + n_b:]], [xbc]

    ins = [_rows(xbc, t, gdtype=BF16), _whole(conv_w), _whole(conv_b)]
    outs = [_rows_out(s, n_x, t, F32), _rows_out(s, n_b, t, BF16), _rows_out(s, n_b, t, BF16)]
    return Op("ssd_conv", f, (1, s // t), ins, outs, [(t, w)])


SSD_L = 512


def _ssd_core_op(x, bm, cm, z, dt, dt_bias, a_log, d_skip, g_norm):
    s = x.shape[0]
    c, hg, p = SSD_L, SSD_HPG, SSD_P
    gw = hg * p

    def f(g, vals, sts):
        x, bm, cm, z, dtr, dt_bias, a_log, d_skip, g_norm = vals
        tri = _tri(c)
        dt = jax.nn.softplus(dtr + dt_bias)
        da = dt * (-jnp.exp(a_log))
        cs = _cumsum_rows(da)
        cs_last = jnp.sum(da, axis=0, keepdims=True)
        cs_t = jnp.transpose(jnp.concatenate([cs, jnp.zeros((c, LANE - hg), F32)], axis=1))
        cb = _bdot_nt(cm, bm)
        ys, new_st = [], []
        for h in range(hg):
            cs_h = cs[:, h:h + 1]
            cs_row = cs_t[h:h + 1, :]
            seg = jnp.where(tri, cs_h - cs_row, 0.0)
            lmat = jnp.where(tri, jnp.exp(seg), 0.0)
            x_h = x[:, h * p:(h + 1) * p]
            xdt = x_h * dt[:, h:h + 1]
            y_diag = _bdot(cb * lmat, xdt)
            decay = jnp.exp(cs_last[:, h:h + 1] - cs_h)
            states = _bdot_tn(xdt * decay, bm)
            y_off = _bdot_nt(cm, sts[h]) * jnp.exp(cs_h)
            new_st.append(jnp.exp(cs_last[:, h:h + 1]) * sts[h] + states)
            ys.append(y_diag + y_off + d_skip[:, h:h + 1] * x_h)
        y = jnp.concatenate(ys, axis=1) * _silu(z)
        return [_rms(y, g_norm)], new_st

    ins = [In(x, (c, gw), lambda h, g: (g, h), "x"), In(bm, (c, SSD_STATE), lambda h, g: (g, h), "x", gdtype=BF16),
           In(cm, (c, SSD_STATE), lambda h, g: (g, h), "x", gdtype=BF16), In(z, (c, gw), lambda h, g: (g, h), "x", gdtype=BF16),
           In(dt, (None, c, hg), lambda h, g: (h, g, 0), "x"),
           In(dt_bias, (None, 1, hg), lambda h, g: (h, 0, 0), "p", per_h=True),
           In(a_log, (None, 1, hg), lambda h, g: (h, 0, 0), "p", per_h=True),
           In(d_skip, (None, 1, hg), lambda h, g: (h, 0, 0), "p", per_h=True),
           In(g_norm, (1, gw), lambda h, g: (0, h), "p", per_h=True)]
    outs = [Out((s, SSD_INNER), BF16, (c, gw), lambda h, g: (g, h))]
    return Op("ssd_core", f, (SSD_GROUPS, s // c), ins, outs, [(p, SSD_STATE)] * hg)


def _loss_op(h, target, final_g):
    s, n = h.shape
    t = ROW_T
    n_g = s // t

    def body(h_ref, t_ref, g_ref, loss_ref, dh_ref, dg_ref):
        step = pl.program_id(0)

        def lossf(hv, gv):
            err = _rms(hv, gv) - t_ref[...]
            return 0.5 * jnp.sum(jnp.mean(err * err, axis=-1))

        l, (dh, dg) = jax.value_and_grad(lossf, argnums=(0, 1))(h_ref[...], g_ref[...])
        dh_ref[...] = dh

        @pl.when(step == 0)
        def _():
            loss_ref[...] = jnp.zeros(loss_ref.shape, F32)
            dg_ref[...] = jnp.zeros(dg_ref.shape, F32)

        loss_ref[...] += jnp.full(loss_ref.shape, l, F32)
        dg_ref[...] += dg

    row = pl.BlockSpec((t, n), lambda g: (g, 0))
    one = pl.BlockSpec((1, n), lambda g: (0, 0))
    return pl.pallas_call(
        body, name="loss_head", grid=(n_g,), in_specs=[row, row, one],
        out_specs=[pl.BlockSpec((1, LANE), lambda g: (0, 0)), row, one],
        out_shape=[jax.ShapeDtypeStruct((1, LANE), F32), jax.ShapeDtypeStruct((s, n), F32), jax.ShapeDtypeStruct((1, n), F32)],
        compiler_params=_CP(dimension_semantics=("arbitrary",)),
    )(h, target, final_g.reshape(1, n))


def _pad_cols(w, n):
    return jnp.pad(w, ((0, 0), (0, n - w.shape[1])))


def _pad_rows(w, n):
    return jnp.pad(w, ((0, n - w.shape[0]), (0, 0)))


def _proj_bwd(tag, u, dps, ws, norm=None):
    du = _mm_pieces_t(f"{tag}_du", dps, ws, norm)
    if sum(dp.shape[1] for dp in dps) <= MM_DW_MERGE_COLS:
        dws = _mm_pieces_dw(f"{tag}_dw", u, dps)
    else:
        dws = [_mm(f"{tag}_dw{i}", u, dp, ta=True, out_dtype=BF16) for i, dp in enumerate(dps)]
    return du, dws


def _mla_layer(h, norm_g, w, cos, sin, fwd_comm=None, late_w_out=None):
    bf = lambda a: a.astype(BF16)
    w_in, w_uq, w_ukv = w["mla_w_in"], w["mla_w_uq"], w["mla_w_ukv"]
    a0, a1, a2 = MLA_Q_RANK, MLA_Q_RANK + MLA_KV_RANK, MLA_Q_RANK + MLA_KV_RANK + MLA_ROPE
    w_cq, w_ckv, w_kr, w_g = bf(w_in[:, :a0]), bf(w_in[:, a0:a1]), bf(_pad_cols(w_in[:, a1:a2], LANE)), bf(w_in[:, a2:])
    uq = w_uq.reshape(MLA_Q_RANK, MLA_HEADS, MLA_QK)
    half = MLA_ROPE // 2
    w_qn = bf(uq[:, :, :MLA_NOPE].reshape(MLA_Q_RANK, -1))
    w_q1 = bf(uq[:, :, MLA_NOPE:MLA_NOPE + half].reshape(MLA_Q_RANK, -1))
    w_q2 = bf(uq[:, :, MLA_NOPE + half:].reshape(MLA_Q_RANK, -1))
    ukv = w_ukv.reshape(MLA_KV_RANK, MLA_HEADS, MLA_NOPE + MLA_V)
    w_kn = bf(ukv[:, :, :MLA_NOPE].reshape(MLA_KV_RANK, -1))
    w_v = bf(ukv[:, :, MLA_NOPE:].reshape(MLA_KV_RANK, -1))

    n0 = _rms_op("mla_norm", h, norm_g)
    (cq, ckv, kr, gate), u = _mm_pieces("mla_in", h, (w_cq, w_ckv, w_kr, w_g), gain=norm_g)
    nq = _rms_op("mla_qnorm", cq, w["mla_g_q"], gdtype=BF16)
    nkv = _rms_op("mla_kvnorm", ckv, w["mla_g_kv"], gdtype=BF16)
    (qn, q1, q2), qn_ = _mm_pieces("mla_uq", cq, (w_qn, w_q1, w_q2), gain=w["mla_g_q"])
    (kn, v), kvn_ = _mm_pieces("mla_ukv", ckv, (w_kn, w_v), gain=w["mla_g_kv"])
    prep = _mla_prep_op(qn, q1, q2, kn, kr, v, cos, sin)
    qh, kh, vh = prep.fwd()
    attn = _mla_attn_op(qh, kh, vh)
    o, = attn.fwd(fwd_comm)
    w_out = bf(w["mla_w_out"]) if late_w_out is None else late_w_out(attn.fwd_comm_out)
    post = _mla_post_op(o, gate)
    y, = post.fwd()
    h_out = _mm("mla_out", y, w_out, add=h)

    def bwd(dh, make_comm=None):
        dy = _mm("mla_out_dy", dh, w_out, tb=True, out_dtype=BF16)
        d_w_out = _mm("mla_out_dw", y, dh, ta=True, out_dtype=BF16)
        do, dgate = post.bwd([dy])
        dqh, dkh, dvh = attn.bwd([do], comm=None if make_comm is None else make_comm(d_w_out))
        dqn, dq1, dq2, dkn, dkr, dv = prep.bwd([dqh, dkh, dvh])
        dqn_, d_uq = _proj_bwd("mla_uq", qn_, (dqn, dq1, dq2), (w_qn, w_q1, w_q2))
        dkvn_, d_ukv = _proj_bwd("mla_ukv", kvn_, (dkn, dv), (w_kn, w_v))
        dcq, d_g_q = nq.bwd([dqn_])
        dckv, d_g_kv = nkv.bwd([dkvn_])
        (dh_in, d_norm), d_in = _proj_bwd("mla_in", u, (dcq, dckv, dkr, dgate), (w_cq, w_ckv, w_kr, w_g), norm=(h, norm_g, dh))
        shp = (MLA_Q_RANK, MLA_HEADS, -1)
        g_uq = jnp.concatenate([d_uq[0].reshape(shp), d_uq[1].reshape(shp), d_uq[2].reshape(shp)], axis=2).reshape(MLA_Q_RANK, -1)
        shp = (MLA_KV_RANK, MLA_HEADS, -1)
        g_ukv = jnp.concatenate([d_ukv[0].reshape(shp), d_ukv[1].reshape(shp)], axis=2).reshape(MLA_KV_RANK, -1)
        g_in = jnp.concatenate([d_in[0], d_in[1], d_in[2][:, :MLA_ROPE], d_in[3]], axis=1)
        return dh_in, d_norm, {"mla_w_in": g_in, "mla_g_q": d_g_q.reshape(-1), "mla_w_uq": g_uq, "mla_g_kv": d_g_kv.reshape(-1),
                               "mla_w_ukv": g_ukv, "mla_w_out": d_w_out}, attn.bwd_comm_out

    return h_out, bwd, attn.fwd_comm_out


def _gla_layer(h, norm_g, w, fwd_comm=None):
    bf = lambda a: a.astype(BF16)
    w_in = w["gla_w_in"]
    nk, nv = GLA_HEADS * GLA_DK, GLA_HEADS * GLA_DV
    cuts = (0, nk, 2 * nk, 2 * nk + nv, 2 * nk + 2 * nv)
    w_q, w_k, w_v, w_g = (bf(w_in[:, cuts[i]:cuts[i + 1]]) for i in range(4))
    w_gk = bf(_pad_cols(w_in[:, cuts[4]:], LANE))
    w2 = _pad_rows(w["gla_w_gk2"], LANE)
    b_gk = w["gla_b_gk"].reshape(1, -1)
    g_o = w["gla_g_o"].reshape(1, -1)
    w_out = bf(w["gla_w_out"])

    n0 = _rms_op("gla_norm", h, norm_g)
    (q, k, v, gate, gk), u = _mm_pieces("gla_in", h, (w_q, w_k, w_v, w_g, w_gk), gain=norm_g)
    gop = _gla_gate_op(gk, w2, b_gk)
    la, = gop.fwd()
    core = _gla_core_op(q, k, v, gate, la, g_o)
    y, = core.fwd(fwd_comm)
    h_out = _mm("gla_out", y, w_out, add=h)

    def bwd(dh, make_comm=None):
        dy = _mm("gla_out_dy", dh, w_out, tb=True, out_dtype=BF16)
        d_w_out = _mm("gla_out_dw", y, dh, ta=True, out_dtype=BF16)
        dq, dk, dv, dgate, dla, d_g_o = core.bwd([dy], comm=None if make_comm is None else make_comm(d_w_out))
        dgk, d_w2, d_b = gop.bwd([dla])
        (dh_in, d_norm), d_in = _proj_bwd("gla_in", u, (dq, dk, dv, dgate, dgk), (w_q, w_k, w_v, w_g, w_gk), norm=(h, norm_g, dh))
        g_in = jnp.concatenate([d_in[0], d_in[1], d_in[2], d_in[3], d_in[4][:, :GLA_RANK]], axis=1)
        return dh_in, d_norm, {"gla_w_in": g_in, "gla_w_gk2": d_w2[:GLA_RANK], "gla_b_gk": d_b.reshape(-1), "gla_g_o": d_g_o.reshape(-1),
                               "gla_w_out": d_w_out}, core.bwd_comm_out

    return h_out, bwd, core.fwd_comm_out


def _lru_layer(h, norm_g, w, fwd_comm=None):
    bf = lambda a: a.astype(BF16)
    w_in = w["lru_w_in"]
    w_g, w_u = bf(w_in[:, :LRU_WIDTH]), bf(w_in[:, LRU_WIDTH:])
    row = lambda a: a.reshape(1, -1)
    w_out = bf(w["lru_w_out"])

    n0 = _rms_op("lru_norm", h, norm_g)
    (gate, u), u_ = _mm_pieces("lru_in", h, (w_g, w_u), gain=norm_g)
    core = _lru_op(gate, u, w["lru_conv_w"].reshape(CONV_W, 1, -1), row(w["lru_conv_b"]), w["lru_w_a"], row(w["lru_b_a"]), w["lru_w_x"],
                   row(w["lru_b_x"]), row(w["lru_lam"]))
    y, = core.fwd(fwd_comm)
    h_out = _mm("lru_out", y, w_out, add=h)

    def bwd(dh, make_comm=None):
        dy = _mm("lru_out_dy", dh, w_out, tb=True, out_dtype=BF16)
        d_w_out = _mm("lru_out_dw", y, dh, ta=True, out_dtype=BF16)
        dgate, du, d_cw, d_cb, d_wa, d_ba, d_wx, d_bx, d_lam = core.bwd([dy], comm=None if make_comm is None else make_comm(d_w_out))
        (dh_in, d_norm), d_in = _proj_bwd("lru_in", u_, (dgate, du), (w_g, w_u), norm=(h, norm_g, dh))
        return dh_in, d_norm, {"lru_w_in": jnp.concatenate(d_in, axis=1), "lru_conv_w": d_cw.reshape(CONV_W, -1), "lru_conv_b": d_cb.reshape(-1),
                               "lru_w_a": d_wa, "lru_b_a": d_ba.reshape(-1), "lru_w_x": d_wx, "lru_b_x": d_bx.reshape(-1),
                               "lru_lam": d_lam.reshape(-1), "lru_w_out": d_w_out}, core.bwd_comm_out

    return h_out, bwd, core.fwd_comm_out


def _ssd_layer(h, norm_g, w, fwd_comm=None, late_w_out=None):
    bf = lambda a: a.astype(BF16)
    s = h.shape[0]
    w_in = w["ssd_w_in"]
    conv_dim = SSD_INNER + 2 * SSD_GROUPS * SSD_STATE
    w_z, w_xbc = bf(w_in[:, :SSD_INNER]), bf(w_in[:, SSD_INNER:SSD_INNER + conv_dim])
    w_dt = bf(_pad_cols(w_in[:, SSD_INNER + conv_dim:], LANE))
    grp = lambda a: a.reshape(SSD_GROUPS, 1, SSD_HPG)

    n0 = _rms_op("ssd_norm", h, norm_g)
    (z, xbc, dtp), u = _mm_pieces("ssd_in", h, (w_z, w_xbc, w_dt), gain=norm_g)
    conv = _ssd_conv_op(xbc, w["ssd_conv_w"].reshape(CONV_W, 1, -1), w["ssd_conv_b"].reshape(1, -1))
    x, bm, cm = conv.fwd()
    dt = dtp[:, :SSD_HEADS].reshape(s, SSD_GROUPS, SSD_HPG).transpose(1, 0, 2)
    core = _ssd_core_op(x, bm, cm, z, dt, grp(w["ssd_dt_bias"]), grp(w["ssd_a_log"]), grp(w["ssd_d"]), w["ssd_g_norm"].reshape(1, -1))
    y, = core.fwd(fwd_comm)
    w_out = bf(w["ssd_w_out"]) if late_w_out is None else late_w_out(core.fwd_comm_out)
    h_out = _mm("ssd_out", y, w_out, add=h)

    def bwd(dh, make_comm=None):
        dy = _mm("ssd_out_dy", dh, w_out, tb=True, out_dtype=BF16)
        d_w_out = _mm("ssd_out_dw", y, dh, ta=True, out_dtype=BF16)
        dx, dbm, dcm, dz, ddt, d_dtb, d_alog, d_d, d_gn = core.bwd([dy], comm=None if make_comm is None else make_comm(d_w_out))
        dxbc, d_cw, d_cb = conv.bwd([dx, dbm, dcm])
        ddtp = _pad_cols(ddt.transpose(1, 0, 2).reshape(s, SSD_HEADS), LANE).astype(BF16)
        (dh_in, d_norm), d_in = _proj_bwd("ssd_in", u, (dz, dxbc, ddtp), (w_z, w_xbc, w_dt), norm=(h, norm_g, dh))
        g_in = jnp.concatenate([d_in[0], d_in[1], d_in[2][:, :SSD_HEADS]], axis=1)
        return dh_in, d_norm, {"ssd_w_in": g_in, "ssd_conv_w": d_cw.reshape(CONV_W, -1), "ssd_conv_b": d_cb.reshape(-1),
                               "ssd_dt_bias": d_dtb.reshape(-1), "ssd_a_log": d_alog.reshape(-1), "ssd_d": d_d.reshape(-1),
                               "ssd_g_norm": d_gn.reshape(-1), "ssd_w_out": d_w_out}, core.bwd_comm_out

    return h_out, bwd


def _rope_tables(positions):
    inv_freq = ROPE_THETA ** (-jnp.arange(0, MLA_ROPE, 2, dtype=F32) / MLA_ROPE)
    ang = positions.astype(F32)[:, None] * inv_freq
    return jnp.cos(ang), jnp.sin(ang)


WEIGHTS = ["norm_g", "final_g", "mla_w_in", "mla_g_q", "mla_w_uq", "mla_g_kv", "mla_w_ukv", "mla_w_out", "gla_w_in", "gla_w_gk2", "gla_b_gk",
           "gla_g_o", "gla_w_out", "lru_w_in", "lru_conv_w", "lru_conv_b", "lru_w_a", "lru_b_a", "lru_w_x", "lru_b_x", "lru_lam", "lru_w_out",
           "ssd_w_in", "ssd_conv_w", "ssd_conv_b", "ssd_dt_bias", "ssd_a_log", "ssd_d", "ssd_g_norm", "ssd_w_out"]
BIG = ["mla_w_in", "mla_w_uq", "mla_w_ukv", "mla_w_out", "gla_w_in", "gla_w_out", "lru_w_in", "lru_w_out", "ssd_w_in", "ssd_w_out"]
SMALL = ["gla_w_gk2", "gla_b_gk", "gla_g_o", "lru_conv_w", "lru_conv_b", "lru_b_a", "lru_b_x", "lru_lam", "ssd_conv_w", "ssd_conv_b", "ssd_g_norm"]
REPL = ["norm_g", "final_g", "mla_g_q", "mla_g_kv", "lru_w_a", "lru_w_x", "ssd_dt_bias", "ssd_a_log", "ssd_d"]
REPL_EARLY = ["lru_w_a", "lru_w_x"]
REPL_LATE = [n for n in REPL if n not in REPL_EARLY]
N_CHIPS, N_DEV = 4, 8
PACK_W = 1024
ADAM_ROWS = 256
SMALL_ROWS = 64


def _shard_axis(name):
    return 0 if name.endswith("_w_out") else -1


def _pack(arrs, dtype, row_mult):
    flat = jnp.concatenate([a.reshape(-1).astype(dtype) for a in arrs])
    per = PACK_W * row_mult
    total = -(-flat.shape[0] // per) * per
    return jnp.pad(flat, (0, total - flat.shape[0])).reshape(-1, PACK_W)


def _unpack(buf, shapes):
    flat = buf.reshape(-1)
    out, off = [], 0
    for s in shapes:
        n = math.prod(s)
        out.append(flat[off:off + n].reshape(s))
        off += n
    return out


def _mesh_pos():
    return lax.axis_index("x"), lax.axis_index("y"), lax.axis_index("c")


class GatherComm:
    def __init__(self, ops):
        self.ops = list(ops)
        n = len(self.ops)
        assert all(o.ndim == 2 and o.shape[0] % 32 == 0 for o in self.ops), [o.shape for o in self.ops]
        self.out_shapes = [jax.ShapeDtypeStruct((N_CHIPS,) + o.shape, o.dtype) for o in self.ops]
        self.sem_shapes = [pltpu.SemaphoreType.DMA((6 * n,)), pltpu.SemaphoreType.DMA((6 * n,)), pltpu.SemaphoreType.DMA((n,))]

    def _copies(self, srcs, dsts, sems):
        send_sems, recv_sems, local_sems = sems
        n = len(self.ops)
        x, y, c = _mesh_pos()
        me_id, sibling = (x, y, c), (x, y, 1 - c)
        chips = [(1 - x, y), (x, 1 - y), (1 - x, 1 - y)]
        mine = 2 * x + y

        def half(i, cc):
            h = self.ops[i].shape[0] // 2
            return pl.ds(cc * h, h)

        def copy(i, k, src, slot, cc, to):
            return pltpu.make_async_remote_copy(src_ref=src, dst_ref=dsts[i].at[slot, half(i, cc)], send_sem=send_sems.at[i * 6 + k],
                                                recv_sem=recv_sems.at[i * 6 + k], device_id=to, device_id_type=pl.DeviceIdType.MESH)

        local = [pltpu.make_async_copy(srcs[i], dsts[i].at[mine], local_sems.at[i]) for i in range(n)]
        first, ici_recvs, passed, sib_recvs = [], [], [], []
        for i in range(n):
            my_half = srcs[i].at[half(i, c)]
            for k, (px, py) in enumerate(chips):
                slot = 2 * px + py
                first.append(copy(i, k, my_half, mine, c, (px, py, c)))
                ici_recvs.append(copy(i, k, my_half, slot, c, me_id))
                passed.append(copy(i, 3 + k, dsts[i].at[slot, half(i, c)], slot, c, sibling))
                sib_recvs.append(copy(i, 3 + k, my_half, slot, 1 - c, me_id))
        return local, first, ici_recvs, passed, sib_recvs

    def start(self, srcs, dsts, sems):
        local, first, _, _, _ = self._copies(srcs, dsts, sems)
        for cp in local + first:
            cp.start()

    def forward(self, srcs, dsts, sems):
        _, _, ici_recvs, passed, _ = self._copies(srcs, dsts, sems)
        for rc, fw in zip(ici_recvs, passed):
            rc.wait_recv()
            fw.start()

    def finish(self, srcs, dsts, sems):
        local, first, _, passed, sib_recvs = self._copies(srcs, dsts, sems)
        for cp in sib_recvs:
            cp.wait_recv()
        for cp in first + passed:
            cp.wait_send()
        for cp in local:
            cp.wait()


class ExchangeComm:
    def __init__(self, chip_ops, all_ops=()):
        self.ops = list(chip_ops) + list(all_ops)
        self.per_chip = (True,) * len(chip_ops) + (False,) * len(all_ops)
        n = len(self.ops)
        self.out_shapes = [jax.ShapeDtypeStruct((N_DEV,) + o.shape[-2:], o.dtype) for o in self.ops]
        self.sem_shapes = [pltpu.SemaphoreType.DMA((7 * n,)), pltpu.SemaphoreType.DMA((7 * n,)), pltpu.SemaphoreType.DMA((n,))]

    def _copies(self, srcs, dsts, sems):
        send_sems, recv_sems, local_sems = sems
        n, per_chip = len(self.ops), self.per_chip
        x, y, c = _mesh_pos()
        me_id, sibling = (x, y, c), (x, y, 1 - c)
        chips = [(1 - x, y), (x, 1 - y), (1 - x, 1 - y)]

        def dev(px, py, pc):
            return 4 * px + 2 * py + pc

        def part(i, px, py):
            return srcs[i].at[2 * px + py] if per_chip[i] else srcs[i]

        def copy(i, k, src, slot, to):
            return pltpu.make_async_remote_copy(src_ref=src, dst_ref=dsts[i].at[slot], send_sem=send_sems.at[i * 7 + k],
                                                recv_sem=recv_sems.at[i * 7 + k], device_id=to, device_id_type=pl.DeviceIdType.MESH)

        me = dev(x, y, c)
        local = [pltpu.make_async_copy(part(i, x, y), dsts[i].at[me], local_sems.at[i]) for i in range(n)]
        first, ici_recvs, passed, sib_recvs = [], [], [], []
        for i in range(n):
            first.append(copy(i, 0, part(i, x, y), me, sibling))
            first += [copy(i, 1 + k, part(i, px, py), me, (px, py, c)) for k, (px, py) in enumerate(chips)]
            sib_recvs.append(copy(i, 0, part(i, x, y), dev(x, y, 1 - c), me_id))
            for k, (px, py) in enumerate(chips):
                slot = dev(px, py, c)
                ici_recvs.append(copy(i, 1 + k, part(i, x, y), slot, me_id))
                passed.append(copy(i, 4 + k, dsts[i].at[slot], slot, sibling))
                sib_recvs.append(copy(i, 4 + k, part(i, x, y), dev(px, py, 1 - c), me_id))
        return local, first, ici_recvs, passed, sib_recvs

    def start(self, srcs, dsts, sems):
        local, first, _, _, _ = self._copies(srcs, dsts, sems)
        for cp in local + first:
            cp.start()

    def forward(self, srcs, dsts, sems):
        _, _, ici_recvs, passed, _ = self._copies(srcs, dsts, sems)
        for rc, fw in zip(ici_recvs, passed):
            rc.wait_recv()
            fw.start()

    def finish(self, srcs, dsts, sems):
        local, first, _, passed, sib_recvs = self._copies(srcs, dsts, sems)
        for cp in sib_recvs:
            cp.wait_recv()
        for cp in first + passed:
            cp.wait_send()
        for cp in local:
            cp.wait()


def _run_comm(name, comm):
    n = len(comm.ops)

    def body(*refs):
        srcs, dsts, sems = refs[:n], refs[n:2 * n], refs[2 * n:]
        comm.start(srcs, dsts, sems)
        comm.forward(srcs, dsts, sems)
        comm.finish(srcs, dsts, sems)

    any_spec = pl.BlockSpec(memory_space=pl.ANY)
    return pl.pallas_call(body, name=name, in_specs=[any_spec] * n, out_specs=[any_spec] * n, out_shape=comm.out_shapes,
                          scratch_shapes=comm.sem_shapes)(*comm.ops)


def _adamw(name, parts, w, m, v, lead=False, comm=None):
    plist = list(parts) if isinstance(parts, (list, tuple)) else [parts]
    n_p = len(plist)
    rows, cols = w.shape[-2:]
    t = next(c for c in (ADAM_ROWS, ADAM_ROWS // 2, SMALL_ROWS) if all(p.shape[1] % c == 0 for p in plist))
    starts = [sum(p.shape[1] for p in plist[:k]) // t for k in range(n_p)]
    counts = [p.shape[1] // t for p in plist]
    assert sum(p.shape[1] for p in plist) == rows, (name, rows)
    c1 = 1.0 - ADAM_B1 ** ADAM_STEP
    c2 = 1.0 - ADAM_B2 ** ADAM_STEP

    n_c = len(comm.ops) if comm is not None else 0
    n_steps = rows // t

    def body(*refs):
        p_refs = refs[:n_p]
        w_ref, m_ref, v_ref = refs[n_p:n_p + 3]
        c_src = refs[n_p + 3:n_p + 3 + n_c]
        g_ref, d_ref, nm_ref, nv_ref = refs[n_p + 3 + n_c:n_p + 7 + n_c]
        cargs = (c_src, refs[n_p + 7 + n_c:n_p + 7 + 2 * n_c], refs[n_p + 7 + 2 * n_c:])
        if comm is not None:
            @pl.when(pl.program_id(0) == 0)
            def _():
                comm.start(*cargs)

            @pl.when(pl.program_id(0) == _forward_step(n_steps))
            def _():
                comm.forward(*cargs)
        g = None
        for k, p_ref in enumerate(p_refs):
            gk = p_ref[0].astype(F32)
            for d in range(1, N_DEV):
                gk = gk + p_ref[d].astype(F32)
            g = gk if g is None else jnp.where(pl.program_id(0) >= starts[k], gk, g)
        nm = ADAM_B1 * m_ref[...] + (1.0 - ADAM_B1) * g
        nv = ADAM_B2 * v_ref[...] + (1.0 - ADAM_B2) * (g * g)
        g_ref[...] = g
        nm_ref[...] = nm
        nv_ref[...] = nv
        d_ref[...] = -ADAM_LR * ((nm / c1) / (jnp.sqrt(nv / c2) + ADAM_EPS) + ADAM_WD * w_ref[...])
        if comm is not None:
            @pl.when(pl.program_id(0) == n_steps - 1)
            def _():
                comm.finish(*cargs)

    row = pl.BlockSpec((None, t, cols), lambda i: (0, i, 0)) if lead else pl.BlockSpec((t, cols), lambda i: (i, 0))
    c_in_specs, c_args, c_out_specs, c_out_shapes, c_sems = _carry_specs(comm)
    res = pl.pallas_call(
        body, name=name, grid=(n_steps,),
        in_specs=[pl.BlockSpec((N_DEV, t, cols), lambda i, lo=lo, n=n: (0, jnp.clip(i - lo, 0, n - 1), 0)) for lo, n in zip(starts, counts)]
        + [row, row, row] + c_in_specs,
        out_specs=[row] * 4 + c_out_specs, out_shape=[jax.ShapeDtypeStruct(w.shape, F32)] * 4 + c_out_shapes, scratch_shapes=c_sems,
        compiler_params=_CP(dimension_semantics=("arbitrary" if comm is not None else "parallel",)),
    )(*plist, w, m, v, *c_args)
    return list(res[:4]), list(res[4:])


def _train_step(x, positions, target, wts, ms, vs, raw):
    small_shapes = [wts[n].shape for n in SMALL]

    big_of = {tag: [n for n in BIG if n.startswith(tag)] for tag in ("mla", "gla", "lru", "ssd")}
    full = {n: wts[n] for n in REPL}

    def gather_comm(names, extra=()):
        return GatherComm([wts[n].astype(BF16) for n in names] + list(extra))

    def assemble(names, got):
        for k, n in enumerate(names):
            full[n] = jnp.concatenate([got[k][j] for j in range(N_CHIPS)], axis=_shard_axis(n))

    first = [n for n in big_of["mla"] if n != "mla_w_out"]
    got = _run_comm("gather_first", gather_comm(first, [_pack([wts[n] for n in SMALL], F32, SMALL_ROWS)]))
    assemble(first, got)
    per_chip_small = [_unpack(got[-1][j], small_shapes) for j in range(N_CHIPS)]
    for k, n in enumerate(SMALL):
        full[n] = jnp.concatenate([per_chip_small[j][k] for j in range(N_CHIPS)], axis=_shard_axis(n))

    cos, sin = _rope_tables(positions)
    ng = full["norm_g"]
    behind_attn = ["mla_w_out"] + big_of["gla"] + big_of["lru"]

    def mla_w_out(got):
        assemble(behind_attn, got)
        return full["mla_w_out"].astype(BF16)

    h1, b0, _ = _mla_layer(x, ng[0], full, cos, sin, fwd_comm=gather_comm(behind_attn), late_w_out=mla_w_out)

    def joined(got_k, axis):
        return jnp.concatenate([got_k[j] for j in range(N_CHIPS)], axis=axis)

    ssd_in = wts["ssd_w_in"].astype(BF16)
    half = ssd_in.shape[0] // 2
    h2, b1, got = _gla_layer(h1, ng[1], full, fwd_comm=GatherComm([ssd_in[:half]]))
    top = joined(got[0], -1)
    h3, b2, got = _lru_layer(h2, ng[2], full, fwd_comm=GatherComm([ssd_in[half:]]))
    full["ssd_w_in"] = jnp.concatenate([top, joined(got[0], -1)], axis=0)
    h4, b3 = _ssd_layer(h3, ng[3], full, fwd_comm=gather_comm(["ssd_w_out"]), late_w_out=lambda got: joined(got[0], 0))
    loss, dh, d_final = _loss_op(h4, target, full["final_g"])
    loss = loss[0, 0]
    grads = {"final_g": d_final.reshape(-1)}
    d_norms = [None] * 4

    def shards_of(n, g):
        return jnp.stack(jnp.split(g.astype(BF16), N_CHIPS, axis=_shard_axis(n)))

    def shards(n):
        return shards_of(n, grads[n])

    parts = {}
    dh, d_norms[3], gw, got = b3(dh, make_comm=lambda dw: ExchangeComm([shards_of("ssd_w_out", dw)]))
    parts["ssd_w_out"] = got[0]
    grads.update(gw)
    ssd_in_g = shards("ssd_w_in")
    half = ssd_in_g.shape[1] // 2
    dh, d_norms[2], gw, got = b2(dh, make_comm=lambda dw: ExchangeComm([ssd_in_g[:, :half], shards_of("lru_w_out", dw)]))
    parts["ssd_w_in"], parts["lru_w_out"] = [got[0]], got[1]
    grads.update(gw)
    dh, d_norms[1], gw, got = b1(dh, make_comm=lambda dw: ExchangeComm([ssd_in_g[:, half:]]))
    parts["ssd_w_in"].append(got[0])
    grads.update(gw)
    repl_early = _pack([grads[n] for n in REPL_EARLY], BF16, SMALL_ROWS)

    behind_attn = ["gla_w_out", "lru_w_in", "gla_w_in"]
    dx, d_norms[0], gw, got = b0(dh, make_comm=lambda dw: ExchangeComm([shards_of("mla_w_out", dw)] + [shards(n) for n in behind_attn],
                                                                        [repl_early]))
    parts.update(zip(["mla_w_out"] + behind_attn, got))
    repl_early_parts = got[-1]
    grads.update(gw)
    grads["norm_g"] = jnp.concatenate(d_norms, axis=0)
    psmall = jnp.stack([_pack([jnp.split(grads[n], N_CHIPS, axis=_shard_axis(n))[j] for n in SMALL], F32, SMALL_ROWS) for j in range(N_CHIPS)])
    prepl = _pack([grads[n] for n in REPL_LATE], F32, SMALL_ROWS)

    out = {}
    kinds = ("grad", "delta", "new_m", "new_v")
    late = [n for n in big_of["mla"] if n != "mla_w_out"]
    late_comm = ExchangeComm([shards(n) for n in late] + [psmall], [prepl])
    for n in ["ssd_w_in"] + [n for n in BIG if n != "ssd_w_in"]:
        if n == "ssd_w_in":
            res, late_parts = _adamw("adam_" + n, parts[n], *(r[n] for r in raw), lead=True, comm=late_comm)
            parts.update(zip(late, late_parts))
        else:
            res, _ = _adamw("adam_" + n, parts[n], *(r[n] for r in raw), lead=True)
        for kind, a in zip(kinds, res):
            out[kind, n] = a
    for tag, names, p in (("adam_small", SMALL, late_parts[-2]), ("adam_repl_early", REPL_EARLY, repl_early_parts),
                          ("adam_repl_late", REPL_LATE, late_parts[-1])):
        shapes = [wts[n].shape for n in names]
        packed = [_pack([d[n] for n in names], F32, SMALL_ROWS) for d in (wts, ms, vs)]
        for kind, buf in zip(kinds, _adamw(tag, p, *packed)[0]):
            for n, a in zip(names, _unpack(buf, shapes)):
                out[kind, n] = a
    loss = lax.psum(loss, ("x", "y", "c"))
    return loss, dx, out


def kernel(x, positions, norm_g, final_g, mla_w_in, mla_g_q, mla_w_uq, mla_g_kv, mla_w_ukv, mla_w_out, gla_w_in, gla_w_gk2, gla_b_gk, gla_g_o, gla_w_out, lru_w_in, lru_conv_w, lru_conv_b, lru_w_a, lru_b_a, lru_w_x, lru_b_x, lru_lam, lru_w_out, ssd_w_in, ssd_conv_w, ssd_conv_b, ssd_dt_bias, ssd_a_log, ssd_d, ssd_g_norm, ssd_w_out, loss_target, m_norm_g, m_final_g, m_mla_w_in, m_mla_g_q, m_mla_w_uq, m_mla_g_kv, m_mla_w_ukv, m_mla_w_out, m_gla_w_in, m_gla_w_gk2, m_gla_b_gk, m_gla_g_o, m_gla_w_out, m_lru_w_in, m_lru_conv_w, m_lru_conv_b, m_lru_w_a, m_lru_b_a, m_lru_w_x, m_lru_b_x, m_lru_lam, m_lru_w_out, m_ssd_w_in, m_ssd_conv_w, m_ssd_conv_b, m_ssd_dt_bias, m_ssd_a_log, m_ssd_d, m_ssd_g_norm, m_ssd_w_out, v_norm_g, v_final_g, v_mla_w_in, v_mla_g_q, v_mla_w_uq, v_mla_g_kv, v_mla_w_ukv, v_mla_w_out, v_gla_w_in, v_gla_w_gk2, v_gla_b_gk, v_gla_g_o, v_gla_w_out, v_lru_w_in, v_lru_conv_w, v_lru_conv_b, v_lru_w_a, v_lru_b_a, v_lru_w_x, v_lru_b_x, v_lru_lam, v_lru_w_out, v_ssd_w_in, v_ssd_conv_w, v_ssd_conv_b, v_ssd_dt_bias, v_ssd_a_log, v_ssd_d, v_ssd_g_norm, v_ssd_w_out):
    given = dict(locals())
    stacked = [n for n in WEIGHTS if n not in ("norm_g", "final_g")]

    def blocks(prefix):
        return {n: (given[prefix + n][0] if n in stacked else given[prefix + n]) for n in WEIGHTS}

    raw = [{n: given[prefix + n] for n in BIG} for prefix in ("", "m_", "v_")]
    loss, dx, out = _train_step(x[0], positions[0], loss_target[0], blocks(""), blocks("m_"), blocks("v_"), raw)
    res = [loss, dx[None]]
    for kind in ("grad", "delta", "new_m", "new_v"):
        res += [(out[kind, n][None] if n in stacked and n not in BIG else out[kind, n]) for n in WEIGHTS]
    return tuple(res)
```
